```python
import jax, jax.numpy as jnp
from jax import lax
import numpy as np

D_MODEL = 1024
BATCH = 8
SEQ = 4096
DEPTH = 1

EXPAND = 2
D_MIX = EXPAND * D_MODEL
HEAD_DIM = 64
D_SSD = D_MIX // 2
D_SB = D_MIX - D_SSD
N_SSD_HEADS = D_SSD // HEAD_DIM
N_SB_HEADS = D_SB // HEAD_DIM
SSD_GROUPS = 2
SSD_STATE = 128
CONV_WIDTH = 4
D_CONV = D_SSD + 2 * SSD_GROUPS * SSD_STATE
SSD_CHUNK = 128
SB_BLOCK = 128
D_FF = -(-8 * D_MODEL // (3 * 256)) * 256
N_MOD = 6
EPS = 1e-6
IN_SPLITS = [int(s) for s in np.cumsum([D_SSD, D_CONV, N_SSD_HEADS, D_SB, D_SB])]
D_IN_PROJ = IN_SPLITS[-1] + D_SB

kernel_name = "hymba_ssd_stickbreaking_adaln_block"


def rms_norm(x, w):
    xf = x.astype(jnp.float32)
    y = xf * lax.rsqrt(jnp.mean(xf * xf, axis=-1, keepdims=True) + EPS)
    return (y * w.astype(jnp.float32)).astype(x.dtype)


def causal_depthwise_conv(u, w, b):
    out = lax.conv_general_dilated(
        u, w[:, None, :], window_strides=(1,), padding=[(CONV_WIDTH - 1, 0)],
        dimension_numbers=("NWC", "WIO", "NWC"), feature_group_count=u.shape[-1])
    return out + b


def ssd_chunked(x, dt, a, b_mat, c_mat):
    bsz, seq, n_heads, p = x.shape
    g, n = b_mat.shape[2], b_mat.shape[3]
    e = n_heads // g
    nc, t = seq // SSD_CHUNK, SSD_CHUNK
    xg = (x * dt[..., None]).reshape(bsz, nc, t, g, e, p)
    a_dt = jnp.moveaxis((a * dt).reshape(bsz, nc, t, g, e), 2, -1)
    a_cs = jnp.cumsum(a_dt, axis=-1)
    bm = b_mat.reshape(bsz, nc, t, g, n)
    cm = c_mat.reshape(bsz, nc, t, g, n)
    idx = jnp.arange(t)
    causal = idx[:, None] >= idx[None, :]
    seg = a_cs[..., :, None] - a_cs[..., None, :]
    decay = jnp.exp(jnp.where(causal, seg, -jnp.inf))
    scores = jnp.einsum("bclgn,bcsgn->bcgls", cm, bm)
    y_diag = jnp.einsum("bcgls,bcgels,bcsgep->bclgep", scores, decay, xg)
    decay_states = jnp.exp(a_cs[..., -1:] - a_cs)
    states = jnp.einsum("bclgn,bcgel,bclgep->bcgepn", bm, decay_states, xg)
    chunk_decay = jnp.exp(a_cs[..., -1])

    def step(carry, inp):
        st, dec = inp
        return carry * dec[..., None, None] + st, carry

    init = jnp.zeros_like(states[:, 0])
    _, prev_states = lax.scan(step, init, (jnp.moveaxis(states, 1, 0), jnp.moveaxis(chunk_decay, 1, 0)))
    prev_states = jnp.moveaxis(prev_states, 0, 1)
    y_off = jnp.einsum("bclgn,bcgepn,bcgel->bclgep", cm, prev_states, jnp.exp(a_cs))
    return (y_diag + y_off).reshape(bsz, seq, n_heads, p)


def ssd_mixer(z, xbc, dt_raw, conv_w, conv_b, dt_bias, a_log, d_skip, norm_w):
    bsz, seq, _ = z.shape
    f32 = jnp.float32
    xbc = jax.nn.silu(causal_depthwise_conv(xbc, conv_w, conv_b)).astype(f32)
    xs = xbc[..., :D_SSD]
    bm = xbc[..., D_SSD:D_SSD + SSD_GROUPS * SSD_STATE].reshape(bsz, seq, SSD_GROUPS, SSD_STATE)
    cm = xbc[..., D_SSD + SSD_GROUPS * SSD_STATE:].reshape(bsz, seq, SSD_GROUPS, SSD_STATE)
    xh = xs.reshape(bsz, seq, N_SSD_HEADS, HEAD_DIM)
    dt = jax.nn.softplus(dt_raw.astype(f32) + dt_bias.astype(f32))
    a = -jnp.exp(a_log.astype(f32))
    y = ssd_chunked(xh, dt, a, bm, cm) + d_skip.astype(f32)[:, None] * xh
    y = y.reshape(bsz, seq, D_SSD) * jax.nn.silu(z.astype(f32))
    yg = y.reshape(bsz, seq, SSD_GROUPS, D_SSD // SSD_GROUPS)
    yg = yg * lax.rsqrt(jnp.mean(yg * yg, axis=-1, keepdims=True) + EPS)
    return (yg.reshape(bsz, seq, D_SSD) * norm_w.astype(f32)).astype(z.dtype)


def stick_breaking_mixer(q, k, v, q_norm_w, k_norm_w):
    bsz, seq, _ = q.shape
    f32 = jnp.float32
    q = rms_norm(q.reshape(bsz, seq, N_SB_HEADS, HEAD_DIM), q_norm_w).astype(f32).transpose(0, 2, 1, 3)
    k = rms_norm(k.reshape(bsz, seq, N_SB_HEADS, HEAD_DIM), k_norm_w).astype(f32).transpose(0, 2, 1, 3)
    vh = v.reshape(bsz, seq, N_SB_HEADS, HEAD_DIM).astype(f32).transpose(0, 2, 1, 3)
    scale = HEAD_DIM ** -0.5
    outs = []
    for blk in range(seq // SB_BLOCK):
        start, end = blk * SB_BLOCK, (blk + 1) * SB_BLOCK
        qb, kp, vp = q[:, :, start:end], k[:, :, :end], vh[:, :, :end]
        logits = jnp.einsum("bhqd,bhkd->bhqk", qb, kp) * scale
        t_pos = start + jnp.arange(SB_BLOCK)
        s_pos = jnp.arange(end)
        strict = s_pos[None, :] < t_pos[:, None]
        log_rem = jnp.where(strict, jax.nn.log_sigmoid(-logits), 0.0)
        log_between = lax.cumsum(log_rem, axis=3, reverse=True) - log_rem
        weights = jnp.where(strict, jnp.exp(jax.nn.log_sigmoid(logits) + log_between), 0.0)
        outs.append(jnp.einsum("bhqk,bhkd->bhqd", weights, vp))
    o = jnp.concatenate(outs, axis=2)
    return o.transpose(0, 2, 1, 3).reshape(bsz, seq, D_SB).astype(v.dtype)


def _fwd_setup_inputs(seed: int = 0) -> dict:
    key = jax.random.key(seed)
    ks = jax.random.split(key, 24)
    f32 = jnp.float32
    nrm = lambda k, shape, s: jax.random.normal(k, shape, f32) * s
    dt0 = jnp.exp(jax.random.uniform(ks[8], (DEPTH, N_SSD_HEADS), f32,
                                     float(np.log(1e-3)), float(np.log(1e-1))))
    dt_bias = dt0 + jnp.log(-jnp.expm1(-dt0))
    return {
        "x": nrm(ks[0], (BATCH, SEQ, D_MODEL), 1.0),
        "c": nrm(ks[1], (BATCH, D_MODEL), 1.0),
        "w_ada": nrm(ks[2], (DEPTH, D_MODEL, N_MOD * D_MODEL), 0.5 * D_MODEL ** -0.5),
        "b_ada": nrm(ks[3], (DEPTH, N_MOD * D_MODEL), 0.01),
        "norm1_w": 1.0 + nrm(ks[4], (DEPTH, D_MODEL), 0.02),
        "w_in": nrm(ks[5], (DEPTH, D_MODEL, D_IN_PROJ), D_MODEL ** -0.5),
        "conv_w": nrm(ks[6], (DEPTH, CONV_WIDTH, D_CONV), CONV_WIDTH ** -0.5),
        "conv_b": nrm(ks[7], (DEPTH, D_CONV), 0.01),
        "dt_bias": dt_bias,
        "a_log": jnp.log(jax.random.uniform(ks[9], (DEPTH, N_SSD_HEADS), f32, 1.0, 16.0)),
        "d_skip": 1.0 + nrm(ks[10], (DEPTH, N_SSD_HEADS), 0.02),
        "ssd_norm_w": 1.0 + nrm(ks[11], (DEPTH, D_SSD), 0.02),
        "q_norm_w": 1.0 + nrm(ks[12], (DEPTH, HEAD_DIM), 0.02),
        "k_norm_w": 1.0 + nrm(ks[13], (DEPTH, HEAD_DIM), 0.02),
        "w_out": nrm(ks[14], (DEPTH, D_MIX, D_MODEL), D_MIX ** -0.5),
        "norm2_w": 1.0 + nrm(ks[15], (DEPTH, D_MODEL), 0.02),
        "w_gate": nrm(ks[16], (DEPTH, D_MODEL, D_FF), D_MODEL ** -0.5),
        "w_up": nrm(ks[17], (DEPTH, D_MODEL, D_FF), D_MODEL ** -0.5),
        "w_down": nrm(ks[18], (DEPTH, D_FF, D_MODEL), D_FF ** -0.5),
    }


def _fwd_reference(x, c, w_ada, b_ada, norm1_w, w_in, conv_w, conv_b, dt_bias, a_log, d_skip,
              ssd_norm_w, q_norm_w, k_norm_w, w_out, norm2_w, w_gate, w_up, w_down):
    cond = jax.nn.silu(c)
    for layer in range(DEPTH):
        mod = (cond @ w_ada[layer] + b_ada[layer])[:, None, :]
        sh1, sc1, g1, sh2, sc2, g2 = jnp.split(mod, N_MOD, axis=-1)
        h = rms_norm(x, norm1_w[layer]) * (1.0 + sc1) + sh1
        proj = h @ w_in[layer]
        z, xbc, dt_raw, q, k, v = jnp.split(proj, IN_SPLITS, axis=-1)
        y_ssd = ssd_mixer(z, xbc, dt_raw, conv_w[layer], conv_b[layer], dt_bias[layer],
                          a_log[layer], d_skip[layer], ssd_norm_w[layer])
        y_sb = stick_breaking_mixer(q, k, v, q_norm_w[layer], k_norm_w[layer])
        mix = jnp.concatenate([y_ssd, y_sb], axis=-1) @ w_out[layer]
        x = x + g1 * mix
        h = rms_norm(x, norm2_w[layer]) * (1.0 + sc2) + sh2
        ffn = (jax.nn.silu(h @ w_gate[layer]) * (h @ w_up[layer])) @ w_down[layer]
        x = x + g2 * ffn
    return x


import jax as _jax
import jax.numpy as _jnp

TWIN_FORMAT = 'train_step'
FWD_PARAMS = ['x', 'c', 'w_ada', 'b_ada', 'norm1_w', 'w_in', 'conv_w', 'conv_b', 'dt_bias', 'a_log', 'd_skip', 'ssd_norm_w', 'q_norm_w', 'k_norm_w', 'w_out', 'norm2_w', 'w_gate', 'w_up', 'w_down']
TWIN_WEIGHTS = ['w_ada', 'b_ada', 'norm1_w', 'w_in', 'conv_w', 'conv_b', 'dt_bias', 'a_log', 'd_skip', 'ssd_norm_w', 'q_norm_w', 'k_norm_w', 'w_out', 'norm2_w', 'w_gate', 'w_up', 'w_down']
TWIN_DIFF_INPUT = 'x'
TWIN_INPUTS = ['x', 'c', 'w_ada', 'b_ada', 'norm1_w', 'w_in', 'conv_w', 'conv_b', 'dt_bias', 'a_log', 'd_skip', 'ssd_norm_w', 'q_norm_w', 'k_norm_w', 'w_out', 'norm2_w', 'w_gate', 'w_up', 'w_down', 'loss_target', 'm_w_ada', 'm_b_ada', 'm_norm1_w', 'm_w_in', 'm_conv_w', 'm_conv_b', 'm_dt_bias', 'm_a_log', 'm_d_skip', 'm_ssd_norm_w', 'm_q_norm_w', 'm_k_norm_w', 'm_w_out', 'm_norm2_w', 'm_w_gate', 'm_w_up', 'm_w_down', 'v_w_ada', 'v_b_ada', 'v_norm1_w', 'v_w_in', 'v_conv_w', 'v_conv_b', 'v_dt_bias', 'v_a_log', 'v_d_skip', 'v_ssd_norm_w', 'v_q_norm_w', 'v_k_norm_w', 'v_w_out', 'v_norm2_w', 'v_w_gate', 'v_w_up', 'v_w_down']
TWIN_OUTPUTS = ['loss', 'grad_x', 'grad_w_ada', 'grad_b_ada', 'grad_norm1_w', 'grad_w_in', 'grad_conv_w', 'grad_conv_b', 'grad_dt_bias', 'grad_a_log', 'grad_d_skip', 'grad_ssd_norm_w', 'grad_q_norm_w', 'grad_k_norm_w', 'grad_w_out', 'grad_norm2_w', 'grad_w_gate', 'grad_w_up', 'grad_w_down', 'delta_w_ada', 'delta_b_ada', 'delta_norm1_w', 'delta_w_in', 'delta_conv_w', 'delta_conv_b', 'delta_dt_bias', 'delta_a_log', 'delta_d_skip', 'delta_ssd_norm_w', 'delta_q_norm_w', 'delta_k_norm_w', 'delta_w_out', 'delta_norm2_w', 'delta_w_gate', 'delta_w_up', 'delta_w_down', 'new_m_w_ada', 'new_m_b_ada', 'new_m_norm1_w', 'new_m_w_in', 'new_m_conv_w', 'new_m_conv_b', 'new_m_dt_bias', 'new_m_a_log', 'new_m_d_skip', 'new_m_ssd_norm_w', 'new_m_q_norm_w', 'new_m_k_norm_w', 'new_m_w_out', 'new_m_norm2_w', 'new_m_w_gate', 'new_m_w_up', 'new_m_w_down', 'new_v_w_ada', 'new_v_b_ada', 'new_v_norm1_w', 'new_v_w_in', 'new_v_conv_w', 'new_v_conv_b', 'new_v_dt_bias', 'new_v_a_log', 'new_v_d_skip', 'new_v_ssd_norm_w', 'new_v_q_norm_w', 'new_v_k_norm_w', 'new_v_w_out', 'new_v_norm2_w', 'new_v_w_gate', 'new_v_w_up', 'new_v_w_down']
TWIN_LEAF_KINDS = {'loss': 'loss', 'grad_x': 'grad_x', 'grad_w_ada': 'grad_w', 'grad_b_ada': 'grad_w', 'grad_norm1_w': 'grad_w', 'grad_w_in': 'grad_w', 'grad_conv_w': 'grad_w', 'grad_conv_b': 'grad_w', 'grad_dt_bias': 'grad_w', 'grad_a_log': 'grad_w', 'grad_d_skip': 'grad_w', 'grad_ssd_norm_w': 'grad_w', 'grad_q_norm_w': 'grad_w', 'grad_k_norm_w': 'grad_w', 'grad_w_out': 'grad_w', 'grad_norm2_w': 'grad_w', 'grad_w_gate': 'grad_w', 'grad_w_up': 'grad_w', 'grad_w_down': 'grad_w', 'delta_w_ada': 'delta_w', 'delta_b_ada': 'delta_w', 'delta_norm1_w': 'delta_w', 'delta_w_in': 'delta_w', 'delta_conv_w': 'delta_w', 'delta_conv_b': 'delta_w', 'delta_dt_bias': 'delta_w', 'delta_a_log': 'delta_w', 'delta_d_skip': 'delta_w', 'delta_ssd_norm_w': 'delta_w', 'delta_q_norm_w': 'delta_w', 'delta_k_norm_w': 'delta_w', 'delta_w_out': 'delta_w', 'delta_norm2_w': 'delta_w', 'delta_w_gate': 'delta_w', 'delta_w_up': 'delta_w', 'delta_w_down': 'delta_w', 'new_m_w_ada': 'new_m', 'new_m_b_ada': 'new_m', 'new_m_norm1_w': 'new_m', 'new_m_w_in': 'new_m', 'new_m_conv_w': 'new_m', 'new_m_conv_b': 'new_m', 'new_m_dt_bias': 'new_m', 'new_m_a_log': 'new_m', 'new_m_d_skip': 'new_m', 'new_m_ssd_norm_w': 'new_m', 'new_m_q_norm_w': 'new_m', 'new_m_k_norm_w': 'new_m', 'new_m_w_out': 'new_m', 'new_m_norm2_w': 'new_m', 'new_m_w_gate': 'new_m', 'new_m_w_up': 'new_m', 'new_m_w_down': 'new_m', 'new_v_w_ada': 'new_v', 'new_v_b_ada': 'new_v', 'new_v_norm1_w': 'new_v', 'new_v_w_in': 'new_v', 'new_v_conv_w': 'new_v', 'new_v_conv_b': 'new_v', 'new_v_dt_bias': 'new_v', 'new_v_a_log': 'new_v', 'new_v_d_skip': 'new_v', 'new_v_ssd_norm_w': 'new_v', 'new_v_q_norm_w': 'new_v', 'new_v_k_norm_w': 'new_v', 'new_v_w_out': 'new_v', 'new_v_norm2_w': 'new_v', 'new_v_w_gate': 'new_v', 'new_v_w_up': 'new_v', 'new_v_w_down': 'new_v'}


def _forward(args):
    return _fwd_reference(*[args[k] for k in FWD_PARAMS])


def _output_shape():
    def fwd():
        inp = _fwd_setup_inputs(0)
        return _fwd_reference(*[inp[k] for k in FWD_PARAMS])
    out = _jax.eval_shape(fwd)
    return out.shape, out.dtype

N_MICROBATCH = 1
ADAM_LR = 0.001
ADAM_B1 = 0.9
ADAM_B2 = 0.999
ADAM_EPS = 1e-08
ADAM_WD = 0.01
ADAM_STEP = 10
PER_EXAMPLE_BATCH_AXIS = {'x': 0, 'c': 0, 'loss_target': 0}
SHARED_INPUTS = []
_WEIGHT_DTYPES = {'w_ada': _jnp.float32, 'b_ada': _jnp.float32, 'norm1_w': _jnp.float32, 'w_in': _jnp.float32, 'conv_w': _jnp.float32, 'conv_b': _jnp.float32, 'dt_bias': _jnp.float32, 'a_log': _jnp.float32, 'd_skip': _jnp.float32, 'ssd_norm_w': _jnp.float32, 'q_norm_w': _jnp.float32, 'k_norm_w': _jnp.float32, 'w_out': _jnp.float32, 'norm2_w': _jnp.float32, 'w_gate': _jnp.float32, 'w_up': _jnp.float32, 'w_down': _jnp.float32}
MOMENT_SCALE = {'w_ada': 9.171420e-01, 'b_ada': 2.229055e+00, 'norm1_w': 6.767054e-01, 'w_in': 7.131393e-02, 'conv_w': 9.825284e-02, 'conv_b': 2.149283e-01, 'dt_bias': 3.798163e-01, 'a_log': 7.544827e-01, 'd_skip': 1.130159e+00, 'ssd_norm_w': 2.232961e+00, 'q_norm_w': 1.569147e+00, 'k_norm_w': 1.561451e+00, 'w_out': 1.839714e-01, 'norm2_w': 3.198713e+00, 'w_gate': 6.615283e-02, 'w_up': 4.791315e-02, 'w_down': 7.485734e-02}


def _to_microbatches(a, axis):
    t = _jnp.moveaxis(a, axis, 0)
    t = t.reshape((N_MICROBATCH, t.shape[0] // N_MICROBATCH) + t.shape[1:])
    return _jnp.moveaxis(t, 1, axis + 1)


def setup_inputs(seed: int = 0) -> dict:
    inp = _fwd_setup_inputs(seed)
    key = _jax.random.fold_in(_jax.random.key(seed), 7919)
    shape, _ = _output_shape()
    out = dict(inp)
    out["loss_target"] = _jax.random.normal(_jax.random.fold_in(key, 0), shape, _jnp.float32)
    for i, name in enumerate(TWIN_WEIGHTS):
        w = inp[name].astype(_jnp.float32)
        if MOMENT_SCALE is None:
            s = _jnp.sqrt(_jnp.mean(_jnp.square(w)) + 1e-30)
        else:
            s = MOMENT_SCALE[name]
        km, kv = _jax.random.split(_jax.random.fold_in(key, i + 1))
        out[name] = w
        out["m_" + name] = s * _jax.random.normal(km, w.shape, _jnp.float32)
        out["v_" + name] = (s * s) * _jax.random.uniform(kv, w.shape, _jnp.float32, 0.5, 1.5)
    if N_MICROBATCH > 1:
        for name, axis in PER_EXAMPLE_BATCH_AXIS.items():
            out[name] = _to_microbatches(out[name], axis)
    return {'x': out['x'], 'c': out['c'], 'w_ada': out['w_ada'], 'b_ada': out['b_ada'], 'norm1_w': out['norm1_w'], 'w_in': out['w_in'], 'conv_w': out['conv_w'], 'conv_b': out['conv_b'], 'dt_bias': out['dt_bias'], 'a_log': out['a_log'], 'd_skip': out['d_skip'], 'ssd_norm_w': out['ssd_norm_w'], 'q_norm_w': out['q_norm_w'], 'k_norm_w': out['k_norm_w'], 'w_out': out['w_out'], 'norm2_w': out['norm2_w'], 'w_gate': out['w_gate'], 'w_up': out['w_up'], 'w_down': out['w_down'], 'loss_target': out['loss_target'], 'm_w_ada': out['m_w_ada'], 'm_b_ada': out['m_b_ada'], 'm_norm1_w': out['m_norm1_w'], 'm_w_in': out['m_w_in'], 'm_conv_w': out['m_conv_w'], 'm_conv_b': out['m_conv_b'], 'm_dt_bias': out['m_dt_bias'], 'm_a_log': out['m_a_log'], 'm_d_skip': out['m_d_skip'], 'm_ssd_norm_w': out['m_ssd_norm_w'], 'm_q_norm_w': out['m_q_norm_w'], 'm_k_norm_w': out['m_k_norm_w'], 'm_w_out': out['m_w_out'], 'm_norm2_w': out['m_norm2_w'], 'm_w_gate': out['m_w_gate'], 'm_w_up': out['m_w_up'], 'm_w_down': out['m_w_down'], 'v_w_ada': out['v_w_ada'], 'v_b_ada': out['v_b_ada'], 'v_norm1_w': out['v_norm1_w'], 'v_w_in': out['v_w_in'], 'v_conv_w': out['v_conv_w'], 'v_conv_b': out['v_conv_b'], 'v_dt_bias': out['v_dt_bias'], 'v_a_log': out['v_a_log'], 'v_d_skip': out['v_d_skip'], 'v_ssd_norm_w': out['v_ssd_norm_w'], 'v_q_norm_w': out['v_q_norm_w'], 'v_k_norm_w': out['v_k_norm_w'], 'v_w_out': out['v_w_out'], 'v_norm2_w': out['v_norm2_w'], 'v_w_gate': out['v_w_gate'], 'v_w_up': out['v_w_up'], 'v_w_down': out['v_w_down']}


def _loss(weights, diff, rest, loss_target):
    with _jax.named_scope("forward"):
        args = {**rest, TWIN_DIFF_INPUT: diff, **{k: w.astype(_WEIGHT_DTYPES[k]) for k, w in weights.items()}}
        y = _forward(args)
    with _jax.named_scope("loss_head"):
        err = _jnp.square(y.astype(_jnp.float32) - loss_target)
        return 0.5 * _jnp.sum(_jnp.mean(err, axis=-1)) if err.ndim else 0.5 * err


def _adamw(w, g, m, v):
    m = ADAM_B1 * m + (1.0 - ADAM_B1) * g
    v = ADAM_B2 * v + (1.0 - ADAM_B2) * _jnp.square(g)
    m_hat = m / (1.0 - ADAM_B1 ** ADAM_STEP)
    v_hat = v / (1.0 - ADAM_B2 ** ADAM_STEP)
    delta = -ADAM_LR * (m_hat / (_jnp.sqrt(v_hat) + ADAM_EPS) + ADAM_WD * w)
    return delta, m, v


def reference(x, c, w_ada, b_ada, norm1_w, w_in, conv_w, conv_b, dt_bias, a_log, d_skip, ssd_norm_w, q_norm_w, k_norm_w, w_out, norm2_w, w_gate, w_up, w_down, loss_target, m_w_ada, m_b_ada, m_norm1_w, m_w_in, m_conv_w, m_conv_b, m_dt_bias, m_a_log, m_d_skip, m_ssd_norm_w, m_q_norm_w, m_k_norm_w, m_w_out, m_norm2_w, m_w_gate, m_w_up, m_w_down, v_w_ada, v_b_ada, v_norm1_w, v_w_in, v_conv_w, v_conv_b, v_dt_bias, v_a_log, v_d_skip, v_ssd_norm_w, v_q_norm_w, v_k_norm_w, v_w_out, v_norm2_w, v_w_gate, v_w_up, v_w_down):
    given = dict(x=x, c=c, w_ada=w_ada, b_ada=b_ada, norm1_w=norm1_w, w_in=w_in, conv_w=conv_w, conv_b=conv_b, dt_bias=dt_bias, a_log=a_log, d_skip=d_skip, ssd_norm_w=ssd_norm_w, q_norm_w=q_norm_w, k_norm_w=k_norm_w, w_out=w_out, norm2_w=norm2_w, w_gate=w_gate, w_up=w_up, w_down=w_down, loss_target=loss_target, m_w_ada=m_w_ada, m_b_ada=m_b_ada, m_norm1_w=m_norm1_w, m_w_in=m_w_in, m_conv_w=m_conv_w, m_conv_b=m_conv_b, m_dt_bias=m_dt_bias, m_a_log=m_a_log, m_d_skip=m_d_skip, m_ssd_norm_w=m_ssd_norm_w, m_q_norm_w=m_q_norm_w, m_k_norm_w=m_k_norm_w, m_w_out=m_w_out, m_norm2_w=m_norm2_w, m_w_gate=m_w_gate, m_w_up=m_w_up, m_w_down=m_w_down, v_w_ada=v_w_ada, v_b_ada=v_b_ada, v_norm1_w=v_norm1_w, v_w_in=v_w_in, v_conv_w=v_conv_w, v_conv_b=v_conv_b, v_dt_bias=v_dt_bias, v_a_log=v_a_log, v_d_skip=v_d_skip, v_ssd_norm_w=v_ssd_norm_w, v_q_norm_w=v_q_norm_w, v_k_norm_w=v_k_norm_w, v_w_out=v_w_out, v_norm2_w=v_norm2_w, v_w_gate=v_w_gate, v_w_up=v_w_up, v_w_down=v_w_down)
    weights = {n: given[n] for n in TWIN_WEIGHTS}
    shared = {n: given[n] for n in SHARED_INPUTS}
    per_example = {n: given[n] for n in ['x', 'c']}
    grad_fn = _jax.value_and_grad(_loss, argnums=(0, 1))

    def one_microbatch(ex, loss_target):
        ex = dict(ex)
        diff = ex.pop(TWIN_DIFF_INPUT)
        return grad_fn(weights, diff, {**shared, **ex}, loss_target)

    if N_MICROBATCH == 1:
        loss, (grad_w, grad_x) = one_microbatch(per_example, given["loss_target"])
    else:
        def body(carry, xs):
            loss_sum, grad_sum = carry
            l_k, (gw_k, gx_k) = one_microbatch(xs[0], xs[1])
            with _jax.named_scope("update"):
                return (loss_sum + l_k, _jax.tree.map(_jnp.add, grad_sum, gw_k)), gx_k

        init = (_jnp.zeros((), _jnp.float32), _jax.tree.map(_jnp.zeros_like, weights))
        (loss, grad_w), grad_x = _jax.lax.scan(body, init, (per_example, given["loss_target"]))
    with _jax.named_scope("update"):
        delta_w, new_m, new_v = {}, {}, {}
        for n in TWIN_WEIGHTS:
            delta_w[n], new_m[n], new_v[n] = _adamw(weights[n], grad_w[n], given["m_" + n], given["v_" + n])
    return (loss, grad_x, *[grad_w[n] for n in TWIN_WEIGHTS], *[delta_w[n] for n in TWIN_WEIGHTS],
            *[new_m[n] for n in TWIN_WEIGHTS], *[new_v[n] for n in TWIN_WEIGHTS])
```

```python
import functools

import numpy as np
import jax
import jax.numpy as jnp
from jax import lax
from jax.experimental import pallas as pl
from jax.experimental.pallas import tpu as pltpu

F32, BF16 = jnp.float32, jnp.bfloat16
MESH = pl.DeviceIdType.MESH

D_MODEL = 1024
HEAD_DIM = 64
N_HEADS = 16
D_CONV = 1536
D_FF = 2816
D_IN_PROJ = 5648
D_PROJ_PAD = 5760
CHUNK = 128
SB_TILE = 256
EPS = 1e-6
N_SHARDS = 4
PACK_ROWS = 4096
PACK_HALF = PACK_ROWS // 2
R_IN, R_OUT, R_FF = 1424, 512, 704
SMALL_ROWS = 24

ADAM_LR, ADAM_B1, ADAM_B2, ADAM_EPS, ADAM_WD, ADAM_STEP = 0.001, 0.9, 0.999, 1e-08, 0.01, 10

VMEM_LIMIT = 48 * 1024 * 1024

_DN = {"nn": (((1,), (0,)), ((), ())), "nt": (((1,), (1,)), ((), ())), "tn": (((0,), (0,)), ((), ()))}


def _dot(a, b, dims="nn"):
    return lax.dot_general(a, b, _DN[dims], preferred_element_type=F32)


def _split(x, n):
    out = []
    for _ in range(n - 1):
        p = x.astype(BF16)
        out.append(p)
        x = x - p.astype(F32)
    out.append(x.astype(BF16))
    return out


def _dotx_r(x, b_exact, dims="nn", n=3):
    acc = None
    for p in reversed(_split(x, n)):
        t = _dot(p, b_exact, dims)
        acc = t if acc is None else acc + t
    return acc


def _dotx_l(a_exact, x, dims="nn", n=3):
    acc = None
    for p in reversed(_split(x, n)):
        t = _dot(a_exact, p, dims)
        acc = t if acc is None else acc + t
    return acc


def _sig(x):
    return 1.0 / (1.0 + jnp.exp(-x))


def _softplus(x):
    return jnp.maximum(x, 0.0) + jnp.log(1.0 + jnp.exp(-jnp.abs(x)))


def _cp(sem=None, vmem=VMEM_LIMIT):
    return pltpu.CompilerParams(dimension_semantics=sem, vmem_limit_bytes=vmem)


def _colsum(x):
    return jnp.sum(x, axis=0, keepdims=True)


def _consts():
    ch = np.arange(D_MODEL)
    expand = (np.arange(128)[:, None] == (ch // HEAD_DIM)[None, :]).astype(np.float32)
    fold = (ch[:, None] % HEAD_DIM == np.arange(128)[None, :]).astype(np.float32)
    i = np.arange(CHUNK)
    tril = (i[:, None] >= i[None, :]).astype(np.float32)
    j = np.arange(SB_TILE)
    ustrict = (j[:, None] > j[None, :]).astype(np.float32)
    ule = (j[:, None] <= j[None, :]).astype(np.float32)
    ult = (j[:, None] < j[None, :]).astype(np.float32)
    c = lambda a: jnp.asarray(a, BF16)
    return dict(expand=c(expand), hsum=c(expand.T), fold=c(fold), tril=c(tril), triu=c(tril.T),
                ustrict=c(ustrict), ule=c(ule), ult=c(ult))


def _pick(n, cap):
    best = 128
    for t in range(128, min(n, cap) + 1, 128):
        if n % t == 0:
            best = t
    return n if n <= cap else best


def _matmul(a, b, dims, out_dtype, name, tm_cap=1024, tn_cap=2048, tk_cap=1024):
    if dims == "nn":
        (m, k), (_, n) = a.shape, b.shape
    elif dims == "nt":
        (m, k), (n, _) = a.shape, b.shape
    else:
        (k, m), (_, n) = a.shape, b.shape
    tm, tn, tk = _pick(m, tm_cap), _pick(n, tn_cap), _pick(k, tk_cap)
    nk = k // tk
    a_spec = (pl.BlockSpec((tk, tm), lambda i, j, kk: (kk, i)) if dims == "tn"
              else pl.BlockSpec((tm, tk), lambda i, j, kk: (i, kk)))
    b_spec = (pl.BlockSpec((tn, tk), lambda i, j, kk: (j, kk)) if dims == "nt"
              else pl.BlockSpec((tk, tn), lambda i, j, kk: (kk, j)))

    def body(a_ref, b_ref, o_ref, acc_ref):
        kk = pl.program_id(2)
        part = _dot(a_ref[...], b_ref[...], dims)
        if nk == 1:
            o_ref[...] = part.astype(out_dtype)
        else:
            @pl.when(kk == 0)
            def _():
                acc_ref[...] = part

            @pl.when(kk > 0)
            def _():
                acc_ref[...] += part

            @pl.when(kk == nk - 1)
            def _():
                o_ref[...] = acc_ref[...].astype(out_dtype)

    return pl.pallas_call(
        body, name=name, grid=(m // tm, n // tn, nk),
        in_specs=[a_spec, b_spec],
        out_specs=pl.BlockSpec((tm, tn), lambda i, j, kk: (i, j)),
        out_shape=jax.ShapeDtypeStruct((m, n), out_dtype),
        scratch_shapes=[pltpu.VMEM((tm, tn) if nk > 1 else (8, 128), F32)],
        compiler_params=_cp(("parallel", "parallel", "arbitrary")),
    )(a, b)


def _row_spec(tm, width=D_MODEL, col=0):
    return pl.BlockSpec((tm, width), lambda i: (i, col))


def _fix_spec(shape):
    return pl.BlockSpec(shape, lambda *_: (0,) * len(shape))


def _norm_mod(x, nw, mod, row_sh, name):
    t = x.shape[0]
    tm = min(t, 512)

    def body(x_ref, nw_ref, mod_ref, h_ref):
        xv = x_ref[...]
        r = lax.rsqrt(jnp.mean(xv * xv, axis=-1, keepdims=True) + EPS)
        sh = mod_ref[row_sh:row_sh + 1, :]
        sc = mod_ref[row_sh + 1:row_sh + 2, :]
        h_ref[...] = (xv * r * nw_ref[...] * (1.0 + sc) + sh).astype(BF16)

    return pl.pallas_call(
        body, name=name, grid=(t // tm,),
        in_specs=[_row_spec(tm), _fix_spec((1, D_MODEL)), _fix_spec((8, D_MODEL))],
        out_specs=_row_spec(tm), out_shape=jax.ShapeDtypeStruct((t, D_MODEL), BF16),
        compiler_params=_cp(("parallel",)),
    )(x, nw, mod)


def _resid_norm(x, mix, nw, mod, name):
    t = x.shape[0]
    tm = min(t, 512)

    def body(x_ref, mix_ref, nw_ref, mod_ref, x1_ref, h_ref):
        x1 = x_ref[...] + mod_ref[2:3, :] * mix_ref[...]
        x1_ref[...] = x1
        r = lax.rsqrt(jnp.mean(x1 * x1, axis=-1, keepdims=True) + EPS)
        h_ref[...] = (x1 * r * nw_ref[...] * (1.0 + mod_ref[4:5, :]) + mod_ref[3:4, :]).astype(BF16)

    return pl.pallas_call(
        body, name=name, grid=(t // tm,),
        in_specs=[_row_spec(tm), _row_spec(tm), _fix_spec((1, D_MODEL)), _fix_spec((8, D_MODEL))],
        out_specs=[_row_spec(tm), _row_spec(tm)],
        out_shape=[jax.ShapeDtypeStruct((t, D_MODEL), F32), jax.ShapeDtypeStruct((t, D_MODEL), BF16)],
        compiler_params=_cp(("parallel",)),
    )(x, mix, nw, mod)


def _act_fwd(gu, name):
    t = gu.shape[0]
    tm, tn = min(t, 512), D_FF // 2
    nb = D_FF // tn

    def body(g_ref, u_ref, a_ref):
        g = g_ref[...]
        a_ref[...] = (g * _sig(g) * u_ref[...]).astype(BF16)

    return pl.pallas_call(
        body, name=name, grid=(t // tm, nb),
        in_specs=[pl.BlockSpec((tm, tn), lambda i, j: (i, j)), pl.BlockSpec((tm, tn), lambda i, j: (i, j + nb))],
        out_specs=pl.BlockSpec((tm, tn), lambda i, j: (i, j)),
        out_shape=jax.ShapeDtypeStruct((t, D_FF), BF16),
        compiler_params=_cp(("parallel", "parallel")),
    )(gu, gu)


def _act_bwd(dact, gu, name):
    t = gu.shape[0]
    tm, tn = min(t, 512), D_FF // 2
    nb = D_FF // tn

    def body(d_ref, g_ref, u_ref, dg_ref, du_ref):
        g, d = g_ref[...], d_ref[...]
        s = _sig(g)
        dg_ref[...] = (d * u_ref[...] * s * (1.0 + g * (1.0 - s))).astype(BF16)
        du_ref[...] = (d * g * s).astype(BF16)

    dg, du = pl.pallas_call(
        body, name=name, grid=(t // tm, nb),
        in_specs=[pl.BlockSpec((tm, tn), lambda i, j: (i, j)), pl.BlockSpec((tm, tn), lambda i, j: (i, j)),
                  pl.BlockSpec((tm, tn), lambda i, j: (i, j + nb))],
        out_specs=[pl.BlockSpec((tm, tn), lambda i, j: (i, j))] * 2,
        out_shape=[jax.ShapeDtypeStruct((t, D_FF), BF16)] * 2,
        compiler_params=_cp(("parallel", "parallel")),
    )(dact, gu, gu)
    return jnp.concatenate([dg, du], axis=1)


def _loss_head(x1, ffn, tgt, mod, name):
    t = x1.shape[0]
    tm = min(t, 512)

    def body(x1_ref, f_ref, t_ref, mod_ref, dffn_ref, dout_ref, dg2_ref, loss_ref):
        i = pl.program_id(0)
        g2 = mod_ref[5:6, :]
        f = f_ref[...]
        err = x1_ref[...] + g2 * f - t_ref[...]
        dout = err * (1.0 / D_MODEL)
        dout_ref[...] = dout
        dffn_ref[...] = (dout * g2).astype(BF16)
        part = jnp.zeros((8, 128), F32) + 0.5 * jnp.sum(jnp.mean(err * err, axis=-1, keepdims=True))

        @pl.when(i == 0)
        def _():
            dg2_ref[...] = _colsum(dout * f)
            loss_ref[...] = part

        @pl.when(i > 0)
        def _():
            dg2_ref[...] += _colsum(dout * f)
            loss_ref[...] += part

    return pl.pallas_call(
        body, name=name, grid=(t // tm,),
        in_specs=[_row_spec(tm), _row_spec(tm), _row_spec(tm), _fix_spec((8, D_MODEL))],
        out_specs=[_row_spec(tm), _row_spec(tm), _fix_spec((1, D_MODEL)), _fix_spec((8, 128))],
        out_shape=[jax.ShapeDtypeStruct((t, D_MODEL), BF16), jax.ShapeDtypeStruct((t, D_MODEL), F32),
                   jax.ShapeDtypeStruct((1, D_MODEL), F32), jax.ShapeDtypeStruct((8, 128), F32)],
        compiler_params=_cp(("arbitrary",)),
    )(x1, ffn, tgt, mod)


def _norm_bwd(dh, xin, dres, aux, nw, mod, row_sh, gate_row, name):
    t = xin.shape[0]
    tm = min(t, 512)
    with_gate = gate_row is not None

    def body(*refs):
        if with_gate:
            dh_ref, x_ref, dr_ref, aux_ref, nw_ref, mod_ref, dx_ref, dg_ref, acc_ref = refs
        else:
            dh_ref, x_ref, dr_ref, nw_ref, mod_ref, dx_ref, acc_ref = refs
        i = pl.program_id(0)
        xv, dhv = x_ref[...], dh_ref[...]
        r = lax.rsqrt(jnp.mean(xv * xv, axis=-1, keepdims=True) + EPS)
        xn = xv * r
        nwv = nw_ref[...]
        sc1 = 1.0 + mod_ref[row_sh + 1:row_sh + 2, :]
        dxn = dhv * (nwv * sc1)
        dx = dr_ref[...] + r * (dxn - xn * jnp.mean(dxn * xn, axis=-1, keepdims=True))
        dx_ref[...] = dx
        dhx = dhv * xn
        rows = [_colsum(dhv), _colsum(dhx * nwv), _colsum(dhx * sc1)]
        if with_gate:
            dg_ref[...] = (dx * mod_ref[gate_row:gate_row + 1, :]).astype(BF16)
            rows.append(_colsum(dx * aux_ref[...]))

        @pl.when(i == 0)
        def _():
            acc_ref[...] = jnp.zeros_like(acc_ref)

        for k, v in enumerate(rows):
            acc_ref[k:k + 1, :] += v

    ins = [dh, xin, dres] + ([aux] if with_gate else []) + [nw, mod]
    in_specs = [_row_spec(tm)] * (4 if with_gate else 3) + [_fix_spec((1, D_MODEL)), _fix_spec((8, D_MODEL))]
    out_specs = [_row_spec(tm)] + ([_row_spec(tm)] if with_gate else []) + [_fix_spec((8, D_MODEL))]
    out_shape = ([jax.ShapeDtypeStruct((t, D_MODEL), F32)]
                 + ([jax.ShapeDtypeStruct((t, D_MODEL), BF16)] if with_gate else [])
                 + [jax.ShapeDtypeStruct((8, D_MODEL), F32)])
    return pl.pallas_call(
        body, name=name, grid=(t // tm,), in_specs=in_specs, out_specs=out_specs, out_shape=out_shape,
        compiler_params=_cp(("arbitrary",)),
    )(*ins)


XBC_COL0 = 4096 // 128
DT_COL = 5632 // 128


def _conv_pre(xv, w_ref, b_ref):
    t = xv.shape[0]
    row = lax.broadcasted_iota(jnp.int32, xv.shape, 0)
    pre = xv * w_ref[3:4, :] + b_ref[...]
    shifted = []
    for k in range(3):
        s = 3 - k
        xs = jnp.where(row >= s, pltpu.roll(xv, s, 0), 0.0)
        shifted.append(xs)
        pre = pre + xs * w_ref[k:k + 1, :]
    return pre, shifted, row, t


def _conv_fwd(proj, conv_w, conv_b, name):
    t = proj.shape[0]

    def body(x_ref, w_ref, b_ref, u_ref):
        pre, _, _, _ = _conv_pre(x_ref[...], w_ref, b_ref)
        u_ref[...] = pre * _sig(pre)

    return pl.pallas_call(
        body, name=name, grid=(D_CONV // 128,),
        in_specs=[pl.BlockSpec((t, 128), lambda j: (0, XBC_COL0 + j)), pl.BlockSpec((4, 128), lambda j: (0, j)),
                  pl.BlockSpec((1, 128), lambda j: (0, j))],
        out_specs=pl.BlockSpec((t, 128), lambda j: (0, j)),
        out_shape=jax.ShapeDtypeStruct((t, D_CONV), F32),
        compiler_params=_cp(("parallel",)),
    )(proj, conv_w, conv_b)


def _conv_bwd(proj, du, conv_w, conv_b, name):
    t = proj.shape[0]

    def body(x_ref, du_ref, w_ref, b_ref, dx_ref, dw_ref, db_ref):
        pre, shifted, row, _ = _conv_pre(x_ref[...], w_ref, b_ref)
        s = _sig(pre)
        dpre = du_ref[...] * s * (1.0 + pre * (1.0 - s))
        db_ref[...] = _colsum(dpre)
        dx = dpre * w_ref[3:4, :]
        dw_ref[3:4, :] = _colsum(dpre * x_ref[...])
        for k in range(3):
            sft = 3 - k
            dw_ref[k:k + 1, :] = _colsum(dpre * shifted[k])
            back = jnp.where(row < t - sft, pltpu.roll(dpre, t - sft, 0), 0.0)
            dx = dx + back * w_ref[k:k + 1, :]
        dx_ref[...] = dx.astype(BF16)

    return pl.pallas_call(
        body, name=name, grid=(D_CONV // 128,),
        in_specs=[pl.BlockSpec((t, 128), lambda j: (0, XBC_COL0 + j)), pl.BlockSpec((t, 128), lambda j: (0, j)),
                  pl.BlockSpec((4, 128), lambda j: (0, j)), pl.BlockSpec((1, 128), lambda j: (0, j))],
        out_specs=[pl.BlockSpec((t, 128), lambda j: (0, j)), pl.BlockSpec((4, 128), lambda j: (0, j)),
                   pl.BlockSpec((1, 128), lambda j: (0, j))],
        out_shape=[jax.ShapeDtypeStruct((t, D_CONV), BF16), jax.ShapeDtypeStruct((4, D_CONV), F32),
                   jax.ShapeDtypeStruct((1, D_CONV), F32)],
        compiler_params=_cp(("parallel",)),
    )(proj, du, conv_w, conv_b)


def _ssd_common(dtraw_ref, dtb_ref, alog_ref, tril, expand, l_s, lt_s):
    lane = lax.broadcasted_iota(jnp.int32, (1, 128), 1)
    dt = _softplus(dtraw_ref[...] + dtb_ref[...])
    a = jnp.where(lane < N_HEADS, -jnp.exp(alog_ref[...]), 0.0)
    lcs = _dotx_l(tril, dt * a)
    l_s[...] = lcs
    lt_s[...] = lcs.T
    llast = l_s[CHUNK - 1:CHUNK, :]
    ea = _dotx_r(jnp.exp(lcs), expand)
    ds = _dotx_r(jnp.exp(llast - lcs), expand)
    dtx = _dotx_r(dt, expand)
    return dt, a, lcs, llast, ea, ds, dtx


def _head_col(lcs, h):
    lane = lax.broadcasted_iota(jnp.int32, lcs.shape, 1)
    return jnp.sum(jnp.where(lane == h, lcs, 0.0), axis=1, keepdims=True)


def _decay(lcs, lt_s, h, causal):
    seg = _head_col(lcs, h) - lt_s[h:h + 1, :]
    return jnp.exp(jnp.where(causal, seg, -1e30))


def _ssd_fwd(u, proj, dtb, alog, dsk, nw, cst, name):
    t = u.shape[0]
    nc = t // CHUNK

    def body(xs_ref, b_ref, c_ref, dtraw_ref, z_ref, dtb_ref, alog_ref, dsk_ref, nw_ref, tril_ref, exp_ref,
             y_ref, yn_ref, prev_ref, carry, l_s, lt_s, yd_s):
        i = pl.program_id(0)

        @pl.when(i == 0)
        def _():
            carry[...] = jnp.zeros_like(carry)

        expand = exp_ref[...]
        dt, a, lcs, llast, ea, ds, dtx = _ssd_common(dtraw_ref, dtb_ref, alog_ref, tril_ref[...], expand, l_s, lt_s)
        xs = xs_ref[...]
        xg = xs * dtx
        xgb = xg.astype(BF16)
        xgd = (xg * ds).astype(BF16)
        prev = carry[...]
        prev_ref[0] = prev
        prevb = prev.astype(BF16)
        ri = lax.broadcasted_iota(jnp.int32, (CHUNK, CHUNK), 0)
        ci = lax.broadcasted_iota(jnp.int32, (CHUNK, CHUNK), 1)
        causal = ri >= ci
        lane = lax.broadcasted_iota(jnp.int32, (1, 128), 1)
        new_states, yoff = [], []
        for g in range(2):
            bg = b_ref[:, g * 128:(g + 1) * 128].astype(BF16)
            cg = c_ref[:, g * 128:(g + 1) * 128].astype(BF16)
            sc = _dot(cg, bg, "nt")
            gs = slice(g * 512, (g + 1) * 512)
            new_states.append(_dot(bg, xgd[:, gs], "tn"))
            yoff.append(_dot(cg, prevb[:, gs]))
            for pr in range(4):
                col = g * 512 + pr * 128
                xp = xgb[:, col:col + 128]
                acc = jnp.zeros((CHUNK, 128), F32)
                for half in range(2):
                    h = g * 8 + pr * 2 + half
                    m = (sc * _decay(lcs, lt_s, h, causal)).astype(BF16)
                    keep = (lane < HEAD_DIM) if half == 0 else (lane >= HEAD_DIM)
                    acc = acc + _dot(m, jnp.where(keep, xp, jnp.zeros_like(xp)))
                yd_s[:, col:col + 128] = acc
        y = yd_s[...] + jnp.concatenate(yoff, axis=1) * ea + xs * dsk_ref[...]
        y_ref[...] = y
        carry[...] = prev * jnp.max(_dotx_r(jnp.exp(llast) + jnp.zeros((8, 128), F32), expand), axis=0, keepdims=True) \
            + jnp.concatenate(new_states, axis=1)
        z = z_ref[...]
        yz = y * (z * _sig(z))
        nwv = nw_ref[...]
        for g in range(2):
            gs = slice(g * 512, (g + 1) * 512)
            v = yz[:, gs]
            r = lax.rsqrt(jnp.mean(v * v, axis=-1, keepdims=True) + EPS)
            yn_ref[:, gs] = (v * r * nwv[:, gs]).astype(BF16)

    row = lambda w, col: pl.BlockSpec((CHUNK, w), lambda i: (i, col))
    return pl.pallas_call(
        body, name=name, grid=(nc,),
        in_specs=[row(1024, 0), row(256, 4), row(256, 5), row(128, DT_COL), row(1024, 0),
                  _fix_spec((1, 128)), _fix_spec((1, 128)), _fix_spec((1, D_MODEL)), _fix_spec((1, D_MODEL)),
                  _fix_spec((CHUNK, CHUNK)), _fix_spec((128, D_MODEL))],
        out_specs=[row(1024, 0), row(1024, 0), pl.BlockSpec((1, 128, D_MODEL), lambda i: (i, 0, 0))],
        out_shape=[jax.ShapeDtypeStruct((t, D_MODEL), F32), jax.ShapeDtypeStruct((t, D_MODEL), BF16),
                   jax.ShapeDtypeStruct((nc, 128, D_MODEL), F32)],
        scratch_shapes=[pltpu.VMEM((128, D_MODEL), F32), pltpu.VMEM((128, 128), F32), pltpu.VMEM((128, 128), F32),
                        pltpu.VMEM((CHUNK, D_MODEL), F32)],
        compiler_params=_cp(("arbitrary",)),
    )(u, u, u, proj, proj, dtb, alog, dsk, nw, cst["tril"], cst["expand"])


def _ssd_bwd(u, proj, y, prev, dycat, dtb, alog, dsk, nw, cst, name):
    t = u.shape[0]
    nc = t // CHUNK

    def body(xs_ref, b_ref, c_ref, dtraw_ref, z_ref, y_ref, prev_ref, dyn_ref, dtb_ref, alog_ref, dsk_ref, nw_ref,
             tril_ref, triu_ref, exp_ref, hs_ref,
             du_ref, ddt_ref, dz_ref, acc_ref, acc16_ref, dcarry, l_s, lt_s, dxg_s):
        i = pl.program_id(0)

        @pl.when(i == 0)
        def _():
            dcarry[...] = jnp.zeros_like(dcarry)
            acc_ref[...] = jnp.zeros_like(acc_ref)
            acc16_ref[...] = jnp.zeros_like(acc16_ref)

        expand, hsum = exp_ref[...], hs_ref[...]
        dt, a, lcs, llast, ea, ds, dtx = _ssd_common(dtraw_ref, dtb_ref, alog_ref, tril_ref[...], expand, l_s, lt_s)
        xs = xs_ref[...]
        xg = xs * dtx
        xgb = xg.astype(BF16)
        xgdf = xg * ds
        xgd = xgdf.astype(BF16)
        dsk_v, nwv = dsk_ref[...], nw_ref[...]
        z, y = z_ref[...], y_ref[...]
        sz = _sig(z)
        silz = z * sz
        yz = y * silz
        dyn = dyn_ref[...]
        dyz_parts, dnw_parts = [], []
        for g in range(2):
            gs = slice(g * 512, (g + 1) * 512)
            v = yz[:, gs]
            r = lax.rsqrt(jnp.mean(v * v, axis=-1, keepdims=True) + EPS)
            yhat = v * r
            dnw_parts.append(_colsum(dyn[:, gs] * yhat))
            dw = dyn[:, gs] * nwv[:, gs]
            dyz_parts.append(r * (dw - yhat * jnp.mean(dw * yhat, axis=-1, keepdims=True)))
        dyz = jnp.concatenate(dyz_parts, axis=1)
        dy = dyz * silz
        dz_ref[...] = (dyz * y * (sz * (1.0 + z * (1.0 - sz)))).astype(BF16)
        acc_ref[0:1, :] += jnp.concatenate(dnw_parts, axis=1)
        acc_ref[1:2, :] += _colsum(dy * xs)
        dyb = dy.astype(BF16)
        dq = (dy * ea).astype(BF16)
        dcar = dcarry[...]
        dcarb = dcar.astype(BF16)
        prev = prev_ref[0]
        prevb = prev.astype(BF16)
        ri = lax.broadcasted_iota(jnp.int32, (CHUNK, CHUNK), 0)
        ci = lax.broadcasted_iota(jnp.int32, (CHUNK, CHUNK), 1)
        causal = ri >= ci
        lane = lax.broadcasted_iota(jnp.int32, (1, 128), 1)
        dprev, dxgd, yoff = [], [], []
        dl_l = jnp.zeros((CHUNK, CHUNK), F32)
        dl_s = jnp.zeros((CHUNK, CHUNK), F32)
        for g in range(2):
            gs = slice(g * 512, (g + 1) * 512)
            bg = b_ref[:, g * 128:(g + 1) * 128].astype(BF16)
            cg = c_ref[:, g * 128:(g + 1) * 128].astype(BF16)
            sc = _dot(cg, bg, "nt")
            yoff.append(_dot(cg, prevb[:, gs]))
            dcg = _dot(dq[:, gs], prevb[:, gs], "nt")
            dprev.append(_dot(cg, dq[:, gs], "tn"))
            dbg = _dot(xgd[:, gs], dcarb[:, gs], "nt")
            dxgd.append(_dot(bg, dcarb[:, gs]))
            dsc = jnp.zeros((CHUNK, CHUNK), F32)
            for pr in range(4):
                col = g * 512 + pr * 128
                xp = xgb[:, col:col + 128]
                dyp = dyb[:, col:col + 128]
                acc = jnp.zeros((CHUNK, 128), F32)
                for half in range(2):
                    h = g * 8 + pr * 2 + half
                    dec = _decay(lcs, lt_s, h, causal)
                    mf = sc * dec
                    keep = (lane < HEAD_DIM) if half == 0 else (lane >= HEAD_DIM)
                    dyh = jnp.where(keep, dyp, jnp.zeros_like(dyp))
                    dm = _dot(dyh, xp, "nt")
                    acc = acc + _dot(mf.astype(BF16), dyh, "tn")
                    dsc = dsc + dm * dec
                    gm = dm * mf
                    dl_l = dl_l + jnp.where(ci == h, jnp.sum(gm, axis=1, keepdims=True), 0.0)
                    dl_s = dl_s + jnp.where(ri == h, jnp.sum(gm, axis=0, keepdims=True), 0.0)
                dxg_s[:, col:col + 128] = acc
            dscb = dsc.astype(BF16)
            dcg = dcg + _dot(dscb, bg)
            dbg = dbg + _dot(dscb, cg, "tn")
            du_ref[:, 1024 + g * 128:1024 + (g + 1) * 128] = dbg
            du_ref[:, 1280 + g * 128:1280 + (g + 1) * 128] = dcg
        dxgd = jnp.concatenate(dxgd, axis=1)
        dxg = dxg_s[...] + dxgd * ds
        du_ref[:, 0:1024] = dy * dsk_v + dxg * dtx
        hs1 = _dotx_r(dxg * xs, hsum)
        yoff = jnp.concatenate(yoff, axis=1) * ea
        dl = dl_l - dl_s.T + _dotx_r(dy * yoff - xgdf * dxgd, hsum)
        rows8 = lax.broadcasted_iota(jnp.int32, (8, D_MODEL), 0)
        two = jnp.where(rows8 == 0, _colsum(dxgd * xgdf), jnp.where(rows8 == 1, _colsum(dcar * prev), 0.0))
        two = _dotx_r(two, hsum)
        r8 = lax.broadcasted_iota(jnp.int32, (8, 128), 0)
        dllast = _colsum(jnp.where(r8 == 0, two, 0.0)) + _colsum(jnp.where(r8 == 1, two, 0.0)) * jnp.exp(llast)
        rowi = lax.broadcasted_iota(jnp.int32, (CHUNK, 128), 0)
        dl = dl + jnp.where(rowi == CHUNK - 1, dllast, 0.0)
        dadt = _dotx_l(triu_ref[...], dl)
        ddt = dadt * a + hs1
        draw = ddt * _sig(dtraw_ref[...] + dtb_ref[...])
        ddt_ref[...] = draw.astype(BF16)
        acc16_ref[0:1, :] += _colsum(draw)
        acc16_ref[1:2, :] += _colsum(dadt * dt) * a
        dcarry[...] = dcar * jnp.max(_dotx_r(jnp.exp(llast) + jnp.zeros((8, 128), F32), expand), axis=0, keepdims=True) \
            + jnp.concatenate(dprev, axis=1)

        @pl.when(i == nc - 1)
        def _():
            hd = _dotx_r(acc_ref[...], hsum)
            acc16_ref[2:3, :] = _colsum(jnp.where(lax.broadcasted_iota(jnp.int32, (8, 128), 0) == 1, hd, 0.0))

    rev = lambda w, col: pl.BlockSpec((CHUNK, w), lambda i: (nc - 1 - i, col))
    return pl.pallas_call(
        body, name=name, grid=(nc,),
        in_specs=[rev(1024, 0), rev(256, 4), rev(256, 5), rev(128, DT_COL), rev(1024, 0), rev(1024, 0),
                  pl.BlockSpec((1, 128, D_MODEL), lambda i: (nc - 1 - i, 0, 0)), rev(1024, 0),
                  _fix_spec((1, 128)), _fix_spec((1, 128)), _fix_spec((1, D_MODEL)), _fix_spec((1, D_MODEL)),
                  _fix_spec((CHUNK, CHUNK)), _fix_spec((CHUNK, CHUNK)), _fix_spec((128, D_MODEL)),
                  _fix_spec((D_MODEL, 128))],
        out_specs=[rev(D_CONV, 0), rev(128, 0), rev(1024, 0), _fix_spec((8, D_MODEL)), _fix_spec((8, 128))],
        out_shape=[jax.ShapeDtypeStruct((t, D_CONV), F32), jax.ShapeDtypeStruct((t, 128), BF16),
                   jax.ShapeDtypeStruct((t, D_MODEL), BF16), jax.ShapeDtypeStruct((8, D_MODEL), F32),
                   jax.ShapeDtypeStruct((8, 128), F32)],
        scratch_shapes=[pltpu.VMEM((128, D_MODEL), F32), pltpu.VMEM((128, 128), F32), pltpu.VMEM((128, 128), F32),
                        pltpu.VMEM((CHUNK, D_MODEL), F32)],
        compiler_params=_cp(("arbitrary",)),
    )(u, u, u, proj, proj, y, prev, dycat, dtb, alog, dsk, nw,
      cst["tril"], cst["triu"], cst["expand"], cst["hsum"])


def _head_rms(v, hsum, expand):
    ms = _dotx_r(v * v, hsum) * (1.0 / HEAD_DIM)
    return _dotx_r(lax.rsqrt(ms + EPS), expand)


def _qk_fwd(proj, qw, kw, cst, name):
    t = proj.shape[0]
    tm = min(t, 256)
    scale = HEAD_DIM ** -0.5

    def body(q_ref, k_ref, v_ref, qw_ref, kw_ref, hs_ref, exp_ref, qs_ref, kn_ref, vb_ref):
        hsum, expand = hs_ref[...], exp_ref[...]
        q, k = q_ref[...], k_ref[...]
        qs_ref[...] = (q * _head_rms(q, hsum, expand) * qw_ref[...] * scale).astype(BF16)
        kn_ref[...] = (k * _head_rms(k, hsum, expand) * kw_ref[...]).astype(BF16)
        vb_ref[...] = v_ref[...].astype(BF16)

    return pl.pallas_call(
        body, name=name, grid=(t // tm,),
        in_specs=[_row_spec(tm, col=1), _row_spec(tm, col=2), _row_spec(tm, col=3),
                  _fix_spec((1, D_MODEL)), _fix_spec((1, D_MODEL)), _fix_spec((D_MODEL, 128)),
                  _fix_spec((128, D_MODEL))],
        out_specs=[_row_spec(tm)] * 3, out_shape=[jax.ShapeDtypeStruct((t, D_MODEL), BF16)] * 3,
        compiler_params=_cp(("parallel",)),
    )(proj, proj, proj, qw, kw, cst["hsum"], cst["expand"])


def _qk_bwd(proj, dqs, dkn, dv, qw, kw, cst, name):
    t = proj.shape[0]
    tm = min(t, 256)
    scale = HEAD_DIM ** -0.5

    def body(q_ref, k_ref, dq_ref, dk_ref, dv_ref, qw_ref, kw_ref, hs_ref, exp_ref, fold_ref,
             oq_ref, ok_ref, ov_ref, dw_ref):
        i = pl.program_id(0)
        hsum, expand = hs_ref[...], exp_ref[...]
        rows8 = lax.broadcasted_iota(jnp.int32, (8, D_MODEL), 0)
        sums = jnp.zeros((8, D_MODEL), F32)
        for n, (x_ref, d_ref, w_ref, o_ref, sc) in enumerate(
                [(q_ref, dq_ref, qw_ref, oq_ref, scale), (k_ref, dk_ref, kw_ref, ok_ref, 1.0)]):
            xv = x_ref[...]
            r = _head_rms(xv, hsum, expand)
            xhat = xv * r
            dn = d_ref[...] * sc
            sums = sums + jnp.where(rows8 == n, _colsum(dn * xhat), 0.0)
            dw = dn * w_ref[...]
            mean = _dotx_r(_dotx_r(dw * xhat, hsum), expand) * (1.0 / HEAD_DIM)
            o_ref[...] = (r * (dw - xhat * mean)).astype(BF16)
        ov_ref[...] = dv_ref[...].astype(BF16)
        folded = _dotx_r(sums, fold_ref[...])

        @pl.when(i == 0)
        def _():
            dw_ref[...] = folded

        @pl.when(i > 0)
        def _():
            dw_ref[...] += folded

    return pl.pallas_call(
        body, name=name, grid=(t // tm,),
        in_specs=[_row_spec(tm, col=1), _row_spec(tm, col=2), _row_spec(tm), _row_spec(tm), _row_spec(tm),
                  _fix_spec((1, D_MODEL)), _fix_spec((1, D_MODEL)), _fix_spec((D_MODEL, 128)),
                  _fix_spec((128, D_MODEL)), _fix_spec((D_MODEL, 128))],
        out_specs=[_row_spec(tm)] * 3 + [_fix_spec((8, 128))],
        out_shape=[jax.ShapeDtypeStruct((t, D_MODEL), BF16)] * 3 + [jax.ShapeDtypeStruct((8, 128), F32)],
        compiler_params=_cp(("arbitrary",)),
    )(proj, proj, dqs, dkn, dv, qw, kw, cst["hsum"], cst["expand"], cst["fold"])


def _sb_masks(i, kb, tq, tk):
    tpos = i * tq + lax.broadcasted_iota(jnp.int32, (tq, 1), 0)
    spos = kb * tk + lax.broadcasted_iota(jnp.int32, (1, tk), 1)
    return spos < tpos


def _sb_fwd(qs, kn, vb, cst, name):
    t = qs.shape[0]
    tq = tk = min(t, SB_TILE)
    nq = t // tq

    def body(q_ref, k_ref, v_ref, u_ref, rt_ref, ob_ref, acc, rs):
        i = pl.program_id(1)
        lane = lax.broadcasted_iota(jnp.int32, (1, 128), 1)
        q2 = q_ref[...]
        zero = jnp.zeros_like(q2)
        qh = [jnp.where(lane < HEAD_DIM, q2, zero), jnp.where(lane >= HEAD_DIM, q2, zero)]
        acc[...] = jnp.zeros_like(acc)
        rs[...] = jnp.zeros_like(rs)
        ustrict = u_ref[...]

        def step(n, carry):
            kb = i - n
            off = pl.multiple_of(kb * tk, tk)
            k2 = k_ref[pl.ds(off, tk), :]
            v2 = v_ref[pl.ds(off, tk), :]
            strict = _sb_masks(i, kb, tq, tk)
            for h in range(2):
                s = _dot(qh[h], k2, "nt")
                sp = _softplus(s)
                r = jnp.where(strict, -sp, 0.0)
                lb = _dotx_r(r, ustrict, n=2)
                w = jnp.exp(jnp.where(strict, s - sp + lb + rs[h], -1e30))
                rs[h] = rs[h] + jnp.sum(r, axis=1, keepdims=True)
                acc[h] = acc[h] + _dot(w.astype(BF16), v2)
            return carry

        lax.fori_loop(0, i + 1, step, 0)
        rt_ref[...] = jnp.where(lane < HEAD_DIM, rs[0], rs[1])
        ob_ref[...] = jnp.where(lane < HEAD_DIM, acc[0], acc[1]).astype(BF16)

    return pl.pallas_call(
        body, name=name, grid=(D_MODEL // 128, nq),
        in_specs=[pl.BlockSpec((tq, 128), lambda j, i: (i, j)), pl.BlockSpec((t, 128), lambda j, i: (0, j)),
                  pl.BlockSpec((t, 128), lambda j, i: (0, j)), _fix_spec((tk, tk))],
        out_specs=[pl.BlockSpec((tq, 128), lambda j, i: (i, j))] * 2,
        out_shape=[jax.ShapeDtypeStruct((t, D_MODEL), F32), jax.ShapeDtypeStruct((t, D_MODEL), BF16)],
        scratch_shapes=[pltpu.VMEM((2, tq, 128), F32), pltpu.VMEM((2, tq, 1), F32)],
        compiler_params=_cp(("parallel", "parallel")),
    )(qs, kn, vb, cst["ustrict"][:tk, :tk])


def _sb_bwd(qs, kn, vb, rtot, dycat, cst, name):
    t = qs.shape[0]
    tq = tk = min(t, SB_TILE)
    nq = t // tq

    def body(q_ref, k_ref, v_ref, rt_ref, do_ref, us_ref, ui_ref, dq_ref, dk_ref, dv_ref, acc, rs, es):
        i = pl.program_id(1)
        lane = lax.broadcasted_iota(jnp.int32, (1, 128), 1)
        keep = [lane < HEAD_DIM, lane >= HEAD_DIM]
        q2, rt = q_ref[...], rt_ref[...]
        do2b = do_ref[...].astype(BF16)
        qh = [jnp.where(kp, q2, jnp.zeros_like(q2)) for kp in keep]
        doh = [jnp.where(kp, do2b, jnp.zeros_like(do2b)) for kp in keep]
        rtot_h = [jnp.sum(jnp.where(lane == n * HEAD_DIM, rt, 0.0), axis=1, keepdims=True) for n in range(2)]
        acc[...] = jnp.zeros_like(acc)
        rs[...] = jnp.zeros_like(rs)
        es[...] = jnp.zeros_like(es)

        @pl.when(i == 0)
        def _():
            dk_ref[...] = jnp.zeros_like(dk_ref)
            dv_ref[...] = jnp.zeros_like(dv_ref)

        ule, ult = us_ref[...], ui_ref[...]

        def step(kb, carry):
            off = pl.multiple_of(kb * tk, tk)
            k2 = k_ref[pl.ds(off, tk), :]
            v2 = v_ref[pl.ds(off, tk), :]
            strict = _sb_masks(i, kb, tq, tk)
            dk_t = jnp.zeros((tk, 128), F32)
            dv_t = jnp.zeros((tk, 128), F32)
            for h in range(2):
                s = _dot(qh[h], k2, "nt")
                sp = _softplus(s)
                sg = jnp.exp(s - sp)
                r = jnp.where(strict, -sp, 0.0)
                lb = (rtot_h[h] - rs[h]) - _dotx_r(r, ule, n=2)
                w = jnp.exp(jnp.where(strict, s - sp + lb, -1e30))
                e = w * _dot(doh[h], v2, "nt")
                ce = es[h] + _dotx_r(e, ult, n=2)
                dl = jnp.where(strict, e * (1.0 - sg) - ce * sg, 0.0).astype(BF16)
                rs[h] = rs[h] + jnp.sum(r, axis=1, keepdims=True)
                es[h] = es[h] + jnp.sum(e, axis=1, keepdims=True)
                acc[h] = acc[h] + _dot(dl, k2)
                dk_t = dk_t + _dot(dl, qh[h], "tn")
                dv_t = dv_t + _dot(w.astype(BF16), doh[h], "tn")
            dk_ref[pl.ds(off, tk), :] += dk_t
            dv_ref[pl.ds(off, tk), :] += dv_t
            return carry

        lax.fori_loop(0, i + 1, step, 0)
        dq_ref[...] = jnp.where(lane < HEAD_DIM, acc[0], acc[1])

    return pl.pallas_call(
        body, name=name, grid=(D_MODEL // 128, nq),
        in_specs=[pl.BlockSpec((tq, 128), lambda j, i: (i, j)), pl.BlockSpec((t, 128), lambda j, i: (0, j)),
                  pl.BlockSpec((t, 128), lambda j, i: (0, j)), pl.BlockSpec((tq, 128), lambda j, i: (i, j)),
                  pl.BlockSpec((tq, 128), lambda j, i: (i, D_MODEL // 128 + j)),
                  _fix_spec((tk, tk)), _fix_spec((tk, tk))],
        out_specs=[pl.BlockSpec((tq, 128), lambda j, i: (i, j)), pl.BlockSpec((t, 128), lambda j, i: (0, j)),
                   pl.BlockSpec((t, 128), lambda j, i: (0, j))],
        out_shape=[jax.ShapeDtypeStruct((t, D_MODEL), F32)] * 3,
        scratch_shapes=[pltpu.VMEM((2, tq, 128), F32), pltpu.VMEM((2, tq, 1), F32), pltpu.VMEM((2, tq, 1), F32)],
        compiler_params=_cp(("parallel", "arbitrary")),
    )(qs, kn, vb, rtot, dycat, cst["ule"][:tk, :tk], cst["ult"][:tk, :tk])


def _adamw(w, g, m, v, name):
    rows, cols = w.shape
    tr = rows
    for cand in (256, 128, 64, 32, 16, 8):
        if rows % cand == 0 and rows > cand:
            tr = cand
            break
    c1 = 1.0 - ADAM_B1 ** ADAM_STEP
    c2 = 1.0 - ADAM_B2 ** ADAM_STEP

    def body(w_ref, g_ref, m_ref, v_ref, d_ref, nm_ref, nv_ref):
        gv = g_ref[...]
        nm = ADAM_B1 * m_ref[...] + (1.0 - ADAM_B1) * gv
        nv = ADAM_B2 * v_ref[...] + (1.0 - ADAM_B2) * (gv * gv)
        nm_ref[...] = nm
        nv_ref[...] = nv
        d_ref[...] = -ADAM_LR * ((nm / c1) / (jnp.sqrt(nv / c2) + ADAM_EPS) + ADAM_WD * w_ref[...])

    spec = pl.BlockSpec((tr, cols), lambda i: (i, 0))
    return pl.pallas_call(
        body, name=name, grid=(rows // tr,), in_specs=[spec] * 4, out_specs=[spec] * 3,
        out_shape=[jax.ShapeDtypeStruct(w.shape, F32)] * 3, compiler_params=_cp(("parallel",)),
    )(w, g, m, v)


def _place():
    x, y, c = lax.axis_index("x"), lax.axis_index("y"), lax.axis_index("c")
    chips = [(1 - x, y), (x, 1 - y), (1 - x, 1 - y)]
    return x, y, c, chips


VM = pl.BlockSpec(memory_space=pltpu.VMEM)
HB = pl.BlockSpec(memory_space=pltpu.HBM)


def _small_gather(pack, name, reduce):
    rows = pack.shape[0]

    def body(p_ref, gat_ref, *rest):
        if reduce:
            sum_ref, ss, rs = rest
        else:
            ss, rs = rest
        x, y, c, _ = _place()
        me = 4 * x + 2 * y + c
        peers = [(x, y, 1 - c), (1 - x, y, c), (x, 1 - y, c), (1 - x, 1 - y, c),
                 (1 - x, y, 1 - c), (x, 1 - y, 1 - c), (1 - x, 1 - y, 1 - c)]

        def copy(k, slot, to):
            return pltpu.make_async_remote_copy(src_ref=p_ref, dst_ref=gat_ref.at[slot], send_sem=ss.at[k],
                                                recv_sem=rs.at[k], device_id=to, device_id_type=MESH)

        sends = [copy(k, me, p) for k, p in enumerate(peers)]
        for s in sends:
            s.start()
        gat_ref[me] = p_ref[...]
        for k, p in enumerate(peers):
            copy(k, 4 * p[0] + 2 * p[1] + p[2], p).wait_recv()
        for s in sends:
            s.wait_send()
        if reduce:
            tot = gat_ref[0]
            for b in range(1, 8):
                tot = tot + gat_ref[b]
            sum_ref[...] = tot

    out_shape = [jax.ShapeDtypeStruct((8, rows, D_MODEL), F32)]
    if reduce:
        out_shape.append(jax.ShapeDtypeStruct((rows, D_MODEL), F32))
    return pl.pallas_call(
        body, name=name, in_specs=[VM], out_specs=[VM] * len(out_shape), out_shape=out_shape,
        scratch_shapes=[pltpu.SemaphoreType.DMA((7,)), pltpu.SemaphoreType.DMA((7,))],
        compiler_params=_cp(),
    )(pack)


def _mod_exchange(cond, w_ada, b_shard, name):
    def body(c_ref, w_ref, b_ref, modp_ref, ss, rs):
        x, y, c, chips = _place()
        sh = 2 * x + y
        cv = c_ref[...]
        cv = cv * _sig(cv)
        modp_ref[sh] = jnp.dot(cv, w_ref[...], precision=lax.Precision.HIGHEST,
                               preferred_element_type=F32) + b_ref[...]

        def copy(k, slot, to):
            return pltpu.make_async_remote_copy(src_ref=modp_ref.at[slot], dst_ref=modp_ref.at[slot],
                                                send_sem=ss.at[k], recv_sem=rs.at[k], device_id=to,
                                                device_id_type=MESH)

        sends = [copy(k, sh, (*ch, c)) for k, ch in enumerate(chips)]
        for s in sends:
            s.start()
        for k, ch in enumerate(chips):
            copy(k, 2 * ch[0] + ch[1], (*ch, c)).wait_recv()
        for s in sends:
            s.wait_send()

    return pl.pallas_call(
        body, name=name, in_specs=[VM, VM, VM], out_specs=VM,
        out_shape=jax.ShapeDtypeStruct((N_SHARDS, 8, 6 * D_MODEL // N_SHARDS), F32),
        scratch_shapes=[pltpu.SemaphoreType.DMA((3,)), pltpu.SemaphoreType.DMA((3,))],
        compiler_params=_cp(),
    )(cond, w_ada, b_shard)


def _gather_weights(pack, name):
    def body(p_ref, out_ref, ss, rs, lsem):
        x, y, c, chips = _place()
        sh = 2 * x + y
        sib = (x, y, 1 - c)

        def half(slot, hc):
            return out_ref.at[slot, pl.ds(hc * PACK_HALF, PACK_HALF), :]

        def copy(k, src, slot, hc, to):
            return pltpu.make_async_remote_copy(src_ref=src, dst_ref=half(slot, hc), send_sem=ss.at[k],
                                                recv_sem=rs.at[k], device_id=to, device_id_type=MESH)

        mine = pltpu.make_async_copy(p_ref, out_ref.at[sh], lsem)
        mine.start()
        first = [copy(j, p_ref.at[pl.ds(c * PACK_HALF, PACK_HALF), :], sh, c, (*ch, c))
                 for j, ch in enumerate(chips)]
        for cp in first:
            cp.start()
        passed = []
        for j, ch in enumerate(chips):
            slot = 2 * ch[0] + ch[1]
            copy(j, half(slot, c), slot, c, (*ch, c)).wait_recv()
            fwd = copy(3 + j, half(slot, c), slot, c, sib)
            fwd.start()
            passed.append(fwd)
        for j, ch in enumerate(chips):
            slot = 2 * ch[0] + ch[1]
            copy(3 + j, half(slot, 1 - c), slot, 1 - c, sib).wait_recv()
        for cp in first + passed:
            cp.wait_send()
        mine.wait()

    return pl.pallas_call(
        body, name=name, in_specs=[HB], out_specs=HB,
        out_shape=jax.ShapeDtypeStruct((N_SHARDS, PACK_ROWS, D_MODEL), BF16),
        scratch_shapes=[pltpu.SemaphoreType.DMA((6,)), pltpu.SemaphoreType.DMA((6,)), pltpu.SemaphoreType.DMA],
        compiler_params=_cp(),
    )(pack)


def _sibling_swap(g, name):
    def body(g_ref, out_ref, ss, rs):
        x, y, c, _ = _place()
        cp = pltpu.make_async_remote_copy(
            src_ref=g_ref.at[pl.ds(0, N_SHARDS), pl.ds((1 - c) * PACK_HALF, PACK_HALF), :], dst_ref=out_ref,
            send_sem=ss, recv_sem=rs, device_id=(x, y, 1 - c), device_id_type=MESH)
        cp.start()
        cp.wait()

    return pl.pallas_call(
        body, name=name, in_specs=[HB], out_specs=HB,
        out_shape=jax.ShapeDtypeStruct((N_SHARDS, PACK_HALF, D_MODEL), g.dtype),
        scratch_shapes=[pltpu.SemaphoreType.DMA, pltpu.SemaphoreType.DMA],
        compiler_params=_cp(),
    )(g)


def _chip_sum(g, got, c_idx, name):
    tr = 512
    nb = PACK_HALF // tr

    def body(c_ref, a_ref, b_ref, s_ref, sb_ref):
        s = a_ref[...] + b_ref[...]
        s_ref[...] = s
        sb_ref[...] = s.astype(BF16)

    blk = pl.BlockSpec((1, tr, D_MODEL), lambda s, i, c_ref: (s, i, 0))
    return pl.pallas_call(
        body, name=name,
        grid_spec=pltpu.PrefetchScalarGridSpec(
            num_scalar_prefetch=1, grid=(N_SHARDS, nb),
            in_specs=[pl.BlockSpec((1, tr, D_MODEL), lambda s, i, c_ref: (s, c_ref[0] * nb + i, 0)), blk],
            out_specs=[blk, blk]),
        out_shape=[jax.ShapeDtypeStruct((N_SHARDS, PACK_HALF, D_MODEL), F32),
                   jax.ShapeDtypeStruct((N_SHARDS, PACK_HALF, D_MODEL), BF16)],
        compiler_params=_cp(("parallel", "parallel")),
    )(c_idx, g, got)


def _chip_exchange(sb, name):
    def body(s_ref, out_ref, ss, rs):
        x, y, c, chips = _place()

        def copy(k, slot, to):
            return pltpu.make_async_remote_copy(src_ref=s_ref.at[slot], dst_ref=out_ref.at[k], send_sem=ss.at[k],
                                                recv_sem=rs.at[k], device_id=to, device_id_type=MESH)

        sends = [copy(k, 2 * ch[0] + ch[1], (*ch, c)) for k, ch in enumerate(chips)]
        for cp in sends:
            cp.start()
        for cp in sends:
            cp.wait()

    return pl.pallas_call(
        body, name=name, in_specs=[HB], out_specs=HB,
        out_shape=jax.ShapeDtypeStruct((3, PACK_HALF, D_MODEL), BF16),
        scratch_shapes=[pltpu.SemaphoreType.DMA((3,)), pltpu.SemaphoreType.DMA((3,))],
        compiler_params=_cp(),
    )(sb)


def _total_half(s, got, sh_idx, name):
    tr = 512
    nb = PACK_HALF // tr

    def body(sh_ref, a_ref, r0, r1, r2, o_ref):
        o_ref[...] = ((a_ref[0] + r0[0].astype(F32)) + r1[0].astype(F32)) + r2[0].astype(F32)

    rspec = lambda k: pl.BlockSpec((1, tr, D_MODEL), lambda i, sh_ref: (k, i, 0))
    return pl.pallas_call(
        body, name=name,
        grid_spec=pltpu.PrefetchScalarGridSpec(
            num_scalar_prefetch=1, grid=(nb,),
            in_specs=[pl.BlockSpec((1, tr, D_MODEL), lambda i, sh_ref: (sh_ref[0], i, 0)),
                      rspec(0), rspec(1), rspec(2)],
            out_specs=pl.BlockSpec((tr, D_MODEL), lambda i, sh_ref: (i, 0))),
        out_shape=jax.ShapeDtypeStruct((PACK_HALF, D_MODEL), F32),
        compiler_params=_cp(("parallel",)),
    )(sh_idx, s, got, got, got)


def _join_halves(tot, name):
    def body(t_ref, out_ref, ss, rs, lsem):
        x, y, c, _ = _place()
        mine = pltpu.make_async_copy(t_ref, out_ref.at[c], lsem)
        mine.start()
        cp = pltpu.make_async_remote_copy(src_ref=t_ref, dst_ref=out_ref.at[c], send_sem=ss, recv_sem=rs,
                                          device_id=(x, y, 1 - c), device_id_type=MESH)
        cp.start()
        recv = pltpu.make_async_remote_copy(src_ref=t_ref, dst_ref=out_ref.at[1 - c], send_sem=ss, recv_sem=rs,
                                            device_id=(x, y, 1 - c), device_id_type=MESH)
        recv.wait_recv()
        cp.wait_send()
        mine.wait()

    return pl.pallas_call(
        body, name=name, in_specs=[HB], out_specs=HB,
        out_shape=jax.ShapeDtypeStruct((2, PACK_HALF, D_MODEL), F32),
        scratch_shapes=[pltpu.SemaphoreType.DMA, pltpu.SemaphoreType.DMA, pltpu.SemaphoreType.DMA],
        compiler_params=_cp(),
    )(tot)


def _w_ada_grad(cond, dmod_cols, name):
    def body(c_ref, d_ref, o_ref):
        cv = c_ref[...]
        cv = cv * _sig(cv)
        o_ref[...] = lax.dot_general(cv, d_ref[...], _DN["tn"], precision=lax.Precision.HIGHEST,
                                     preferred_element_type=F32)

    return pl.pallas_call(
        body, name=name, in_specs=[VM, VM], out_specs=VM,
        out_shape=jax.ShapeDtypeStruct((D_MODEL, dmod_cols.shape[1]), F32), compiler_params=_cp(),
    )(cond, dmod_cols)


def _pad_rows(a, rows):
    return jnp.pad(a, ((0, rows - a.shape[0]), (0, 0)))


def _pad_cols(a, cols):
    return jnp.pad(a, ((0, 0), (0, cols - a.shape[1])))


def _weight_pack(w_in, w_out, w_gate, w_up, w_down, dtype):
    parts = [_pad_rows(w_in.reshape(-1, D_MODEL), R_IN), w_out, w_gate.reshape(-1, D_MODEL),
             w_up.reshape(-1, D_MODEL), w_down]
    pack = jnp.concatenate([p.astype(dtype) for p in parts], axis=0)
    return _pad_rows(pack, PACK_ROWS)


def _unpack(p):
    o = 0
    out = []
    for r in (R_IN, R_OUT, R_FF, R_FF, R_FF):
        out.append(p[..., o:o + r, :])
        o += r
    return out


def kernel(x, c, w_ada, b_ada, norm1_w, w_in, conv_w, conv_b, dt_bias, a_log, d_skip, ssd_norm_w, q_norm_w, k_norm_w, w_out, norm2_w, w_gate, w_up, w_down, loss_target, m_w_ada, m_b_ada, m_norm1_w, m_w_in, m_conv_w, m_conv_b, m_dt_bias, m_a_log, m_d_skip, m_ssd_norm_w, m_q_norm_w, m_k_norm_w, m_w_out, m_norm2_w, m_w_gate, m_w_up, m_w_down, v_w_ada, v_b_ada, v_norm1_w, v_w_in, v_conv_w, v_conv_b, v_dt_bias, v_a_log, v_d_skip, v_ssd_norm_w, v_q_norm_w, v_k_norm_w, v_w_out, v_norm2_w, v_w_gate, v_w_up, v_w_down):
    cst = _consts()
    ax, ay, ac = lax.axis_index("x"), lax.axis_index("y"), lax.axis_index("c")
    shard = 2 * ax + ay
    me = 4 * ax + 2 * ay + ac
    xs = x[0]
    tgt = loss_target[0]
    w_in_cols = w_in.shape[2]
    conv_cols = conv_w.shape[2]

    cw_flat = _pad_cols(conv_w[0].reshape(1, -1), 2 * D_MODEL).reshape(2, D_MODEL)
    cpack = jnp.concatenate([jnp.broadcast_to(c, (8, D_MODEL)), _pad_rows(cw_flat, 8)], axis=0)
    gat = _small_gather(cpack, "gather_c", reduce=False)[0]
    c_all = gat[:, 0, :]
    cw = gat[0::2, 8:10, :].reshape(N_SHARDS, 2 * D_MODEL)[:, :4 * conv_cols].reshape(N_SHARDS, 4, conv_cols)
    conv_w_full = jnp.transpose(cw, (1, 0, 2)).reshape(4, D_CONV)

    mod_w = 6 * D_MODEL // N_SHARDS
    b_shard = lax.dynamic_slice(b_ada, (0, shard * mod_w), (1, mod_w))
    modp = _mod_exchange(c_all, w_ada[0], b_shard, "mod_exchange")
    mod_mine = lax.dynamic_slice(modp, (0, me, 0), (N_SHARDS, 1, mod_w)).reshape(6, D_MODEL)
    mod = _pad_rows(mod_mine, 8)

    gp = _gather_weights(_weight_pack(w_in[0], w_out[0], w_gate[0], w_up[0], w_down[0], BF16), "gather_weights")
    p_in, p_out, p_gate, p_up, p_down = _unpack(gp)

    def cols(p, ncol):
        return jnp.concatenate([p[s].reshape(-1)[:D_MODEL * ncol].reshape(D_MODEL, ncol) for s in range(N_SHARDS)],
                               axis=1)

    wi = cols(p_in, w_in_cols)
    w_inp = jnp.concatenate([wi[:, 0:1024], wi[:, 2576:5648], wi[:, 1024:2560], wi[:, 2560:2576],
                             jnp.zeros((D_MODEL, 112), BF16)], axis=1)
    w_o = p_out.reshape(2 * D_MODEL, D_MODEL)
    w_gu = jnp.concatenate([cols(p_gate, D_FF // N_SHARDS), cols(p_up, D_FF // N_SHARDS)], axis=1)
    w_d = p_down.reshape(D_FF, D_MODEL)

    pad128 = lambda a: _pad_cols(a, 128)
    dtb, alog = pad128(dt_bias), pad128(a_log)
    dsk = jnp.repeat(d_skip, HEAD_DIM, axis=1)
    qw, kw = jnp.tile(q_norm_w, (1, N_HEADS)), jnp.tile(k_norm_w, (1, N_HEADS))

    h1 = _norm_mod(xs, norm1_w, mod, 0, "norm1")
    proj = _matmul(h1, w_inp, "nn", F32, "in_proj")
    u = _conv_fwd(proj, conv_w_full, conv_b, "conv_fwd")
    y_ssd, yn, prev = _ssd_fwd(u, proj, dtb, alog, dsk, ssd_norm_w, cst, "ssd_fwd")
    qs, kn, vb = _qk_fwd(proj, qw, kw, cst, "qk_norm")
    rtot, ob = _sb_fwd(qs, kn, vb, cst, "sb_fwd")
    ycat = jnp.concatenate([yn, ob], axis=1)
    mix = _matmul(ycat, w_o, "nn", F32, "out_proj")
    x1, h2 = _resid_norm(xs, mix, norm2_w, mod, "resid_norm2")
    gu = _matmul(h2, w_gu, "nn", F32, "ffn_in")
    act = _act_fwd(gu, "ffn_act")
    ffn = _matmul(act, w_d, "nn", F32, "ffn_out")
    dffn, dout, dg2, loss8 = _loss_head(x1, ffn, tgt, mod, "loss_head")
    loss = lax.psum(loss8[0, 0], ("x", "y", "c"))

    dact = _matmul(dffn, w_d, "nt", F32, "d_act")
    g_down = _matmul(act, dffn, "tn", F32, "g_w_down", tk_cap=512)
    dgu = _act_bwd(dact, gu, "ffn_act_bwd")
    dh2 = _matmul(dgu, w_gu, "nt", F32, "d_h2")
    g_gu = _matmul(h2, dgu, "tn", F32, "g_w_gu", tk_cap=512)
    dx1, dmix, acc2 = _norm_bwd(dh2, x1, dout, mix, norm2_w, mod, 3, 2, "norm2_bwd")
    dycat = _matmul(dmix, w_o, "nt", F32, "d_ycat")
    g_out = _matmul(ycat, dmix, "tn", F32, "g_w_out", tk_cap=512)
    du, ddt, dz, acc_ssd, acc16 = _ssd_bwd(u, proj, y_ssd, prev, dycat, dtb, alog, dsk, ssd_norm_w, cst, "ssd_bwd")
    dqs, dkn, dv = _sb_bwd(qs, kn, vb, rtot, dycat, cst, "sb_bwd")
    dq, dk, dvb, acc_qk = _qk_bwd(proj, dqs, dkn, dv, qw, kw, cst, "qk_norm_bwd")
    dxbc, g_conv_w, g_conv_b = _conv_bwd(proj, du, conv_w_full, conv_b, "conv_bwd")
    dproj = jnp.concatenate([dz, dq, dk, dvb, dxbc, ddt], axis=1)
    dh1 = _matmul(dproj, w_inp, "nt", F32, "d_h1", tk_cap=1152)
    g_inp = _matmul(h1, dproj, "tn", F32, "g_w_in", tk_cap=512)
    grad_x, acc1 = _norm_bwd(dh1, xs, dx1, None, norm1_w, mod, 0, None, "norm1_bwd")

    last = jnp.concatenate([acc_qk[0:1, 0:64], acc_qk[1:2, 0:64], acc16[0:1, 0:16], acc16[1:2, 0:16],
                            acc16[2:3, 0:16]], axis=1)
    spack = jnp.concatenate([
        acc1[0:2], acc2[3:4], acc2[0:2], dg2,
        acc1[2:3], acc2[2:3], acc_ssd[0:1],
        _pad_cols(g_conv_b, 2 * D_MODEL).reshape(2, D_MODEL),
        g_conv_w.reshape(6, D_MODEL),
        _pad_cols(last, D_MODEL)], axis=0)
    sgat, ssum = _small_gather(_pad_rows(spack, SMALL_ROWS), "gather_small", reduce=True)
    g_b_ada = ssum[0:6].reshape(1, 6 * D_MODEL)
    g_norm1, g_norm2, g_ssdn = ssum[6:7], ssum[7:8], ssum[8:9]
    g_cb = ssum[9:11].reshape(1, 2 * D_MODEL)[:, :D_CONV]
    g_cw = lax.dynamic_slice(ssum[11:17].reshape(4, D_CONV), (0, shard * conv_cols), (4, conv_cols))
    g_qn, g_kn = ssum[17:18, 0:64], ssum[17:18, 64:128]
    g_dtb, g_alog, g_dsk = ssum[17:18, 128:144], ssum[17:18, 144:160], ssum[17:18, 160:176]
    dmod_all = sgat[:, 0:6, :].reshape(8, 6 * D_MODEL)
    g_w_ada = _w_ada_grad(c_all, lax.dynamic_slice(dmod_all, (0, shard * mod_w), (8, mod_w)), "g_w_ada")

    gi = jnp.concatenate([g_inp[:, 0:1024], g_inp[:, 4096:5632], g_inp[:, 5632:5648], g_inp[:, 1024:4096]], axis=1)
    ff = D_FF // N_SHARDS

    def shards_of_cols(g, ncol, rows):
        return jnp.stack([_pad_rows(g[:, s * ncol:(s + 1) * ncol].reshape(-1, D_MODEL), rows)
                          for s in range(N_SHARDS)])

    gpack = jnp.concatenate([
        shards_of_cols(gi, w_in_cols, R_IN), g_out.reshape(N_SHARDS, R_OUT, D_MODEL),
        shards_of_cols(g_gu[:, :D_FF], ff, R_FF), shards_of_cols(g_gu[:, D_FF:], ff, R_FF),
        g_down.reshape(N_SHARDS, R_FF, D_MODEL),
        jnp.zeros((N_SHARDS, PACK_ROWS - R_IN - R_OUT - 3 * R_FF, D_MODEL), F32)], axis=1)
    got = _sibling_swap(gpack, "rs_sibling_swap")
    csum, csum_b = _chip_sum(gpack, got, ac.reshape(1).astype(jnp.int32), "rs_chip_sum")
    got2 = _chip_exchange(csum_b, "rs_chip_exchange")
    tot = _total_half(csum, got2, shard.reshape(1).astype(jnp.int32), "rs_total")
    gfull = _join_halves(tot, "rs_join").reshape(PACK_ROWS, D_MODEL)
    r_in, r_out, r_gate, r_up, r_down = _unpack(gfull)
    g_w_in = r_in.reshape(-1)[:D_MODEL * w_in_cols].reshape(D_MODEL, w_in_cols)
    g_w_gate, g_w_up = r_gate.reshape(D_MODEL, ff), r_up.reshape(D_MODEL, ff)

    grads = dict(w_ada=g_w_ada, b_ada=g_b_ada, norm1_w=g_norm1, w_in=g_w_in, conv_w=g_cw, conv_b=g_cb,
                 dt_bias=g_dtb, a_log=g_alog, d_skip=g_dsk, ssd_norm_w=g_ssdn, q_norm_w=g_qn, k_norm_w=g_kn,
                 w_out=r_out, norm2_w=g_norm2, w_gate=g_w_gate, w_up=g_w_up, w_down=r_down)
    weights = dict(w_ada=(w_ada, m_w_ada, v_w_ada), b_ada=(b_ada, m_b_ada, v_b_ada),
                   norm1_w=(norm1_w, m_norm1_w, v_norm1_w), w_in=(w_in, m_w_in, v_w_in),
                   conv_w=(conv_w, m_conv_w, v_conv_w), conv_b=(conv_b, m_conv_b, v_conv_b),
                   dt_bias=(dt_bias, m_dt_bias, v_dt_bias), a_log=(a_log, m_a_log, v_a_log),
                   d_skip=(d_skip, m_d_skip, v_d_skip), ssd_norm_w=(ssd_norm_w, m_ssd_norm_w, v_ssd_norm_w),
                   q_norm_w=(q_norm_w, m_q_norm_w, v_q_norm_w), k_norm_w=(k_norm_w, m_k_norm_w, v_k_norm_w),
                   w_out=(w_out, m_w_out, v_w_out), norm2_w=(norm2_w, m_norm2_w, v_norm2_w),
                   w_gate=(w_gate, m_w_gate, v_w_gate), w_up=(w_up, m_w_up, v_w_up),
                   w_down=(w_down, m_w_down, v_w_down))
    names = list(weights)
    g_out_l, d_out_l, m_out_l, v_out_l = [], [], [], []
    for n in names:
        w, m, v = weights[n]
        shp = w.shape
        w2, m2, v2 = (a.reshape(shp[-2], shp[-1]) if a.ndim == 3 else a for a in (w, m, v))
        g2 = grads[n].reshape(w2.shape)
        d, nm, nv = _adamw(w2, g2, m2, v2, "adamw_" + n)
        g_out_l.append(g2.reshape(shp))
        d_out_l.append(d.reshape(shp))
        m_out_l.append(nm.reshape(shp))
        v_out_l.append(nv.reshape(shp))
    return (loss, grad_x[None], *g_out_l, *d_out_l, *m_out_l, *v_out_l)
```

```python
import functools

import numpy as np
import jax
import jax.numpy as jnp
from jax import lax
from jax.experimental import pallas as pl
from jax.experimental.pallas import tpu as pltpu

F32, BF16 = jnp.float32, jnp.bfloat16
MESH = pl.DeviceIdType.MESH

D_MODEL = 1024
HEAD_DIM = 64
N_HEADS = 16
D_CONV = 1536
D_FF = 2816
D_IN_PROJ = 5648
D_PROJ_PAD = 5760
CHUNK = 128
SB_TILE = 256
EPS = 1e-6
N_SHARDS = 4
PACK_ROWS = 4096
PACK_HALF = PACK_ROWS // 2
R_IN, R_OUT, R_FF = 1424, 512, 704
SMALL_ROWS = 24

ADAM_LR, ADAM_B1, ADAM_B2, ADAM_EPS, ADAM_WD, ADAM_STEP = 0.001, 0.9, 0.999, 1e-08, 0.01, 10

VMEM_LIMIT = 48 * 1024 * 1024

_DN = {"nn": (((1,), (0,)), ((), ())), "nt": (((1,), (1,)), ((), ())), "tn": (((0,), (0,)), ((), ()))}


def _dot(a, b, dims="nn"):
    return lax.dot_general(a, b, _DN[dims], preferred_element_type=F32)


def _split(x, n):
    out = []
    for _ in range(n - 1):
        p = x.astype(BF16)
        out.append(p)
        x = x - p.astype(F32)
    out.append(x.astype(BF16))
    return out


def _dotx_r(x, b_exact, dims="nn", n=3):
    acc = None
    for p in reversed(_split(x, n)):
        t = _dot(p, b_exact, dims)
        acc = t if acc is None else acc + t
    return acc


def _dotx_l(a_exact, x, dims="nn", n=3):
    acc = None
    for p in reversed(_split(x, n)):
        t = _dot(a_exact, p, dims)
        acc = t if acc is None else acc + t
    return acc


def _dot2(x, b2):
    hi = lax.bitcast_convert_type(lax.bitcast_convert_type(x, jnp.int32) & jnp.int32(-65536), F32)
    return _dot(jnp.concatenate([hi.astype(BF16), (x - hi).astype(BF16)], axis=1), b2)


def _sig(x):
    return 1.0 / (1.0 + jnp.exp(-x))


def _softplus(x):
    return jnp.maximum(x, 0.0) + jnp.log(1.0 + jnp.exp(-jnp.abs(x)))


def _cp(sem=None, vmem=VMEM_LIMIT):
    return pltpu.CompilerParams(dimension_semantics=sem, vmem_limit_bytes=vmem)


def _colsum(x):
    return jnp.sum(x, axis=0, keepdims=True)


def _consts():
    ch = np.arange(D_MODEL)
    expand = (np.arange(128)[:, None] == (ch // HEAD_DIM)[None, :]).astype(np.float32)
    fold = (ch[:, None] % HEAD_DIM == np.arange(128)[None, :]).astype(np.float32)
    i = np.arange(CHUNK)
    tril = (i[:, None] >= i[None, :]).astype(np.float32)
    j = np.arange(SB_TILE)
    ustrict = (j[:, None] > j[None, :]).astype(np.float32)
    ule = (j[:, None] <= j[None, :]).astype(np.float32)
    ult = (j[:, None] < j[None, :]).astype(np.float32)
    c = lambda a: jnp.asarray(a, BF16)
    return dict(expand=c(expand), hsum=c(expand.T), fold=c(fold), tril=c(tril), triu=c(tril.T),
                ustrict=ustrict, ule=ule, ult=ult)


def _doubled(tri, tk):
    b = tri[:tk, :tk]
    return jnp.asarray(np.concatenate([b, b], axis=0), BF16)


def _pick(n, cap):
    best = 128
    for t in range(128, min(n, cap) + 1, 128):
        if n % t == 0:
            best = t
    return n if n <= cap else best


def _matmul(a, b, dims, out_dtype, name, tm_cap=1024, tn_cap=2048, tk_cap=1024):
    if dims == "nn":
        (m, k), (_, n) = a.shape, b.shape
    elif dims == "nt":
        (m, k), (n, _) = a.shape, b.shape
    else:
        (k, m), (_, n) = a.shape, b.shape
    tm, tn, tk = _pick(m, tm_cap), _pick(n, tn_cap), _pick(k, tk_cap)
    nk = k // tk
    a_spec = (pl.BlockSpec((tk, tm), lambda i, j, kk: (kk, i)) if dims == "tn"
              else pl.BlockSpec((tm, tk), lambda i, j, kk: (i, kk)))
    b_spec = (pl.BlockSpec((tn, tk), lambda i, j, kk: (j, kk)) if dims == "nt"
              else pl.BlockSpec((tk, tn), lambda i, j, kk: (kk, j)))

    def body(a_ref, b_ref, o_ref, acc_ref):
        kk = pl.program_id(2)
        part = _dot(a_ref[...], b_ref[...], dims)
        if nk == 1:
            o_ref[...] = part.astype(out_dtype)
        else:
            @pl.when(kk == 0)
            def _():
                acc_ref[...] = part

            @pl.when(kk > 0)
            def _():
                acc_ref[...] += part

            @pl.when(kk == nk - 1)
            def _():
                o_ref[...] = acc_ref[...].astype(out_dtype)

    return pl.pallas_call(
        body, name=name, grid=(m // tm, n // tn, nk),
        in_specs=[a_spec, b_spec],
        out_specs=pl.BlockSpec((tm, tn), lambda i, j, kk: (i, j)),
        out_shape=jax.ShapeDtypeStruct((m, n), out_dtype),
        scratch_shapes=[pltpu.VMEM((tm, tn) if nk > 1 else (8, 128), F32)],
        compiler_params=_cp(("parallel", "parallel", "arbitrary")),
    )(a, b)


def _row_spec(tm, width=D_MODEL, col=0):
    return pl.BlockSpec((tm, width), lambda i: (i, col))


def _fix_spec(shape):
    return pl.BlockSpec(shape, lambda *_: (0,) * len(shape))


def _norm_mod(x, nw, mod, row_sh, name):
    t = x.shape[0]
    tm = min(t, 512)

    def body(x_ref, nw_ref, mod_ref, h_ref):
        xv = x_ref[...]
        r = lax.rsqrt(jnp.mean(xv * xv, axis=-1, keepdims=True) + EPS)
        sh = mod_ref[row_sh:row_sh + 1, :]
        sc = mod_ref[row_sh + 1:row_sh + 2, :]
        h_ref[...] = (xv * r * nw_ref[...] * (1.0 + sc) + sh).astype(BF16)

    return pl.pallas_call(
        body, name=name, grid=(t // tm,),
        in_specs=[_row_spec(tm), _fix_spec((1, D_MODEL)), _fix_spec((8, D_MODEL))],
        out_specs=_row_spec(tm), out_shape=jax.ShapeDtypeStruct((t, D_MODEL), BF16),
        compiler_params=_cp(("parallel",)),
    )(x, nw, mod)


def _resid_norm(x, mix, nw, mod, name):
    t = x.shape[0]
    tm = min(t, 512)

    def body(x_ref, mix_ref, nw_ref, mod_ref, x1_ref, h_ref):
        x1 = x_ref[...] + mod_ref[2:3, :] * mix_ref[...]
        x1_ref[...] = x1
        r = lax.rsqrt(jnp.mean(x1 * x1, axis=-1, keepdims=True) + EPS)
        h_ref[...] = (x1 * r * nw_ref[...] * (1.0 + mod_ref[4:5, :]) + mod_ref[3:4, :]).astype(BF16)

    return pl.pallas_call(
        body, name=name, grid=(t // tm,),
        in_specs=[_row_spec(tm), _row_spec(tm), _fix_spec((1, D_MODEL)), _fix_spec((8, D_MODEL))],
        out_specs=[_row_spec(tm), _row_spec(tm)],
        out_shape=[jax.ShapeDtypeStruct((t, D_MODEL), F32), jax.ShapeDtypeStruct((t, D_MODEL), BF16)],
        compiler_params=_cp(("parallel",)),
    )(x, mix, nw, mod)


def _act_fwd(gu, name):
    t = gu.shape[0]
    tm, tn = min(t, 512), D_FF // 2
    nb = D_FF // tn

    def body(g_ref, u_ref, a_ref):
        g = g_ref[...]
        a_ref[...] = (g * _sig(g) * u_ref[...]).astype(BF16)

    return pl.pallas_call(
        body, name=name, grid=(t // tm, nb),
        in_specs=[pl.BlockSpec((tm, tn), lambda i, j: (i, j)), pl.BlockSpec((tm, tn), lambda i, j: (i, j + nb))],
        out_specs=pl.BlockSpec((tm, tn), lambda i, j: (i, j)),
        out_shape=jax.ShapeDtypeStruct((t, D_FF), BF16),
        compiler_params=_cp(("parallel", "parallel")),
    )(gu, gu)


def _act_bwd(dact, gu, name):
    t = gu.shape[0]
    tm = min(t, 256)

    def body(d_ref, g_ref, u_ref, o_ref):
        g, d = g_ref[...], d_ref[...]
        s = _sig(g)
        o_ref[:, 0:D_FF] = (d * u_ref[...] * s * (1.0 + g * (1.0 - s))).astype(BF16)
        o_ref[:, D_FF:2 * D_FF] = (d * g * s).astype(BF16)

    return pl.pallas_call(
        body, name=name, grid=(t // tm,),
        in_specs=[pl.BlockSpec((tm, D_FF), lambda i: (i, 0)), pl.BlockSpec((tm, D_FF), lambda i: (i, 0)),
                  pl.BlockSpec((tm, D_FF), lambda i: (i, 1))],
        out_specs=pl.BlockSpec((tm, 2 * D_FF), lambda i: (i, 0)),
        out_shape=jax.ShapeDtypeStruct((t, 2 * D_FF), BF16),
        compiler_params=_cp(("parallel",)),
    )(dact, gu, gu)


def _loss_head(x1, ffn, tgt, mod, name):
    t = x1.shape[0]
    tm = min(t, 512)

    def body(x1_ref, f_ref, t_ref, mod_ref, dffn_ref, dout_ref, dg2_ref, loss_ref):
        i = pl.program_id(0)
        g2 = mod_ref[5:6, :]
        f = f_ref[...]
        err = x1_ref[...] + g2 * f - t_ref[...]
        dout = err * (1.0 / D_MODEL)
        dout_ref[...] = dout
        dffn_ref[...] = (dout * g2).astype(BF16)
        part = jnp.zeros((8, 128), F32) + 0.5 * jnp.sum(jnp.mean(err * err, axis=-1, keepdims=True))

        @pl.when(i == 0)
        def _():
            dg2_ref[...] = _colsum(dout * f)
            loss_ref[...] = part

        @pl.when(i > 0)
        def _():
            dg2_ref[...] += _colsum(dout * f)
            loss_ref[...] += part

    return pl.pallas_call(
        body, name=name, grid=(t // tm,),
        in_specs=[_row_spec(tm), _row_spec(tm), _row_spec(tm), _fix_spec((8, D_MODEL))],
        out_specs=[_row_spec(tm), _row_spec(tm), _fix_spec((1, D_MODEL)), _fix_spec((8, 128))],
        out_shape=[jax.ShapeDtypeStruct((t, D_MODEL), BF16), jax.ShapeDtypeStruct((t, D_MODEL), F32),
                   jax.ShapeDtypeStruct((1, D_MODEL), F32), jax.ShapeDtypeStruct((8, 128), F32)],
        compiler_params=_cp(("arbitrary",)),
    )(x1, ffn, tgt, mod)


def _norm_bwd(dh, xin, dres, aux, nw, mod, row_sh, gate_row, name):
    t = xin.shape[0]
    tm = min(t, 512)
    with_gate = gate_row is not None

    def body(*refs):
        if with_gate:
            dh_ref, x_ref, dr_ref, aux_ref, nw_ref, mod_ref, dx_ref, dg_ref, acc_ref = refs
        else:
            dh_ref, x_ref, dr_ref, nw_ref, mod_ref, dx_ref, acc_ref = refs
        i = pl.program_id(0)
        xv, dhv = x_ref[...], dh_ref[...]
        r = lax.rsqrt(jnp.mean(xv * xv, axis=-1, keepdims=True) + EPS)
        xn = xv * r
        nwv = nw_ref[...]
        sc1 = 1.0 + mod_ref[row_sh + 1:row_sh + 2, :]
        dxn = dhv * (nwv * sc1)
        dx = dr_ref[...] + r * (dxn - xn * jnp.mean(dxn * xn, axis=-1, keepdims=True))
        dx_ref[...] = dx
        dhx = dhv * xn
        rows = [_colsum(dhv), _colsum(dhx * nwv), _colsum(dhx * sc1)]
        if with_gate:
            dg_ref[...] = (dx * mod_ref[gate_row:gate_row + 1, :]).astype(BF16)
            rows.append(_colsum(dx * aux_ref[...]))

        @pl.when(i == 0)
        def _():
            acc_ref[...] = jnp.zeros_like(acc_ref)

        for k, v in enumerate(rows):
            acc_ref[k:k + 1, :] += v

    ins = [dh, xin, dres] + ([aux] if with_gate else []) + [nw, mod]
    in_specs = [_row_spec(tm)] * (4 if with_gate else 3) + [_fix_spec((1, D_MODEL)), _fix_spec((8, D_MODEL))]
    out_specs = [_row_spec(tm)] + ([_row_spec(tm)] if with_gate else []) + [_fix_spec((8, D_MODEL))]
    out_shape = ([jax.ShapeDtypeStruct((t, D_MODEL), F32)]
                 + ([jax.ShapeDtypeStruct((t, D_MODEL), BF16)] if with_gate else [])
                 + [jax.ShapeDtypeStruct((8, D_MODEL), F32)])
    return pl.pallas_call(
        body, name=name, grid=(t // tm,), in_specs=in_specs, out_specs=out_specs, out_shape=out_shape,
        compiler_params=_cp(("arbitrary",)),
    )(*ins)


XBC_COL0 = 4096 // 128
DT_COL = 5632 // 128


def _conv_pre(xv, w_ref, b_ref):
    t = xv.shape[0]
    row = lax.broadcasted_iota(jnp.int32, xv.shape, 0)
    pre = xv * w_ref[3:4, :] + b_ref[...]
    shifted = []
    for k in range(3):
        s = 3 - k
        xs = jnp.where(row >= s, pltpu.roll(xv, s, 0), 0.0)
        shifted.append(xs)
        pre = pre + xs * w_ref[k:k + 1, :]
    return pre, shifted, row, t


def _conv_fwd(proj, conv_w, conv_b, name):
    t = proj.shape[0]

    def body(x_ref, w_ref, b_ref, u_ref):
        pre, _, _, _ = _conv_pre(x_ref[...], w_ref, b_ref)
        u_ref[...] = pre * _sig(pre)

    return pl.pallas_call(
        body, name=name, grid=(D_CONV // 128,),
        in_specs=[pl.BlockSpec((t, 128), lambda j: (0, XBC_COL0 + j)), pl.BlockSpec((4, 128), lambda j: (0, j)),
                  pl.BlockSpec((1, 128), lambda j: (0, j))],
        out_specs=pl.BlockSpec((t, 128), lambda j: (0, j)),
        out_shape=jax.ShapeDtypeStruct((t, D_CONV), F32),
        compiler_params=_cp(("parallel",)),
    )(proj, conv_w, conv_b)


def _conv_bwd(proj, du, conv_w, conv_b, name):
    t = proj.shape[0]

    def body(x_ref, du_ref, w_ref, b_ref, dx_ref, dw_ref, db_ref):
        pre, shifted, row, _ = _conv_pre(x_ref[...], w_ref, b_ref)
        s = _sig(pre)
        dpre = du_ref[...] * s * (1.0 + pre * (1.0 - s))
        db_ref[...] = _colsum(dpre)
        dx = dpre * w_ref[3:4, :]
        dw_ref[3:4, :] = _colsum(dpre * x_ref[...])
        for k in range(3):
            sft = 3 - k
            dw_ref[k:k + 1, :] = _colsum(dpre * shifted[k])
            back = jnp.where(row < t - sft, pltpu.roll(dpre, t - sft, 0), 0.0)
            dx = dx + back * w_ref[k:k + 1, :]
        dx_ref[...] = dx.astype(BF16)

    return pl.pallas_call(
        body, name=name, grid=(D_CONV // 128,),
        in_specs=[pl.BlockSpec((t, 128), lambda j: (0, XBC_COL0 + j)), pl.BlockSpec((t, 128), lambda j: (0, j)),
                  pl.BlockSpec((4, 128), lambda j: (0, j)), pl.BlockSpec((1, 128), lambda j: (0, j))],
        out_specs=[pl.BlockSpec((t, 128), lambda j: (0, j)), pl.BlockSpec((4, 128), lambda j: (0, j)),
                   pl.BlockSpec((1, 128), lambda j: (0, j))],
        out_shape=[jax.ShapeDtypeStruct((t, D_CONV), BF16), jax.ShapeDtypeStruct((4, D_CONV), F32),
                   jax.ShapeDtypeStruct((1, D_CONV), F32)],
        compiler_params=_cp(("parallel",)),
    )(proj, du, conv_w, conv_b)


def _ssd_common(dtraw_ref, dtb_ref, alog_ref, tril, expand, l_s, lt_s):
    lane = lax.broadcasted_iota(jnp.int32, (1, 128), 1)
    dt = _softplus(dtraw_ref[...] + dtb_ref[...])
    a = jnp.where(lane < N_HEADS, -jnp.exp(alog_ref[...]), 0.0)
    lcs = _dotx_l(tril, dt * a)
    l_s[...] = lcs
    lt_s[...] = lcs.T
    llast = l_s[CHUNK - 1:CHUNK, :]
    ea = _dotx_r(jnp.exp(lcs), expand)
    ds = _dotx_r(jnp.exp(llast - lcs), expand)
    dtx = _dotx_r(dt, expand)
    return dt, a, lcs, llast, ea, ds, dtx


def _head_col(lcs, h):
    lane = lax.broadcasted_iota(jnp.int32, lcs.shape, 1)
    return jnp.sum(jnp.where(lane == h, lcs, 0.0), axis=1, keepdims=True)


def _decay(lcs, lt_s, h, causal):
    seg = _head_col(lcs, h) - lt_s[h:h + 1, :]
    return jnp.exp(jnp.where(causal, seg, -1e30))


def _ssd_fwd(u, proj, dtb, alog, dsk, nw, cst, name):
    t = u.shape[0]
    nc = t // CHUNK

    def body(xs_ref, b_ref, c_ref, dtraw_ref, z_ref, dtb_ref, alog_ref, dsk_ref, nw_ref, tril_ref, exp_ref,
             y_ref, yn_ref, prev_ref, carry, l_s, lt_s, yd_s):
        i = pl.program_id(0)

        @pl.when(i == 0)
        def _():
            carry[...] = jnp.zeros_like(carry)

        expand = exp_ref[...]
        dt, a, lcs, llast, ea, ds, dtx = _ssd_common(dtraw_ref, dtb_ref, alog_ref, tril_ref[...], expand, l_s, lt_s)
        xs = xs_ref[...]
        xg = xs * dtx
        xgb = xg.astype(BF16)
        xgd = (xg * ds).astype(BF16)
        prev = carry[...]
        prev_ref[0] = prev
        prevb = prev.astype(BF16)
        ri = lax.broadcasted_iota(jnp.int32, (CHUNK, CHUNK), 0)
        ci = lax.broadcasted_iota(jnp.int32, (CHUNK, CHUNK), 1)
        causal = ri >= ci
        lane = lax.broadcasted_iota(jnp.int32, (1, 128), 1)
        new_states, yoff = [], []
        for g in range(2):
            bg = b_ref[:, g * 128:(g + 1) * 128].astype(BF16)
            cg = c_ref[:, g * 128:(g + 1) * 128].astype(BF16)
            sc = _dot(cg, bg, "nt")
            gs = slice(g * 512, (g + 1) * 512)
            new_states.append(_dot(bg, xgd[:, gs], "tn"))
            yoff.append(_dot(cg, prevb[:, gs]))
            for pr in range(4):
                col = g * 512 + pr * 128
                xp = xgb[:, col:col + 128]
                acc = jnp.zeros((CHUNK, 128), F32)
                for half in range(2):
                    h = g * 8 + pr * 2 + half
                    m = (sc * _decay(lcs, lt_s, h, causal)).astype(BF16)
                    keep = (lane < HEAD_DIM) if half == 0 else (lane >= HEAD_DIM)
                    acc = acc + _dot(m, jnp.where(keep, xp, jnp.zeros_like(xp)))
                yd_s[:, col:col + 128] = acc
        y = yd_s[...] + jnp.concatenate(yoff, axis=1) * ea + xs * dsk_ref[...]
        y_ref[...] = y
        carry[...] = prev * jnp.max(_dotx_r(jnp.exp(llast) + jnp.zeros((8, 128), F32), expand), axis=0, keepdims=True) \
            + jnp.concatenate(new_states, axis=1)
        z = z_ref[...]
        yz = y * (z * _sig(z))
        nwv = nw_ref[...]
        for g in range(2):
            gs = slice(g * 512, (g + 1) * 512)
            v = yz[:, gs]
            r = lax.rsqrt(jnp.mean(v * v, axis=-1, keepdims=True) + EPS)
            yn_ref[:, gs] = (v * r * nwv[:, gs]).astype(BF16)

    row = lambda w, col: pl.BlockSpec((CHUNK, w), lambda i: (i, col))
    return pl.pallas_call(
        body, name=name, grid=(nc,),
        in_specs=[row(1024, 0), row(256, 4), row(256, 5), row(128, DT_COL), row(1024, 0),
                  _fix_spec((1, 128)), _fix_spec((1, 128)), _fix_spec((1, D_MODEL)), _fix_spec((1, D_MODEL)),
                  _fix_spec((CHUNK, CHUNK)), _fix_spec((128, D_MODEL))],
        out_specs=[row(1024, 0), row(1024, 0), pl.BlockSpec((1, 128, D_MODEL), lambda i: (i, 0, 0))],
        out_shape=[jax.ShapeDtypeStruct((t, D_MODEL), F32), jax.ShapeDtypeStruct((t, D_MODEL), BF16),
                   jax.ShapeDtypeStruct((nc, 128, D_MODEL), F32)],
        scratch_shapes=[pltpu.VMEM((128, D_MODEL), F32), pltpu.VMEM((128, 128), F32), pltpu.VMEM((128, 128), F32),
                        pltpu.VMEM((CHUNK, D_MODEL), F32)],
        compiler_params=_cp(("arbitrary",)),
    )(u, u, u, proj, proj, dtb, alog, dsk, nw, cst["tril"], cst["expand"])


def _ssd_bwd(u, proj, y, prev, dycat, dtb, alog, dsk, nw, cst, name):
    t = u.shape[0]
    nc = t // CHUNK

    def body(xs_ref, b_ref, c_ref, dtraw_ref, z_ref, y_ref, prev_ref, dyn_ref, dtb_ref, alog_ref, dsk_ref, nw_ref,
             tril_ref, triu_ref, exp_ref, hs_ref,
             du_ref, ddt_ref, dz_ref, acc_ref, acc16_ref, dcarry, l_s, lt_s, dxg_s):
        i = pl.program_id(0)

        @pl.when(i == 0)
        def _():
            dcarry[...] = jnp.zeros_like(dcarry)
            acc_ref[...] = jnp.zeros_like(acc_ref)
            acc16_ref[...] = jnp.zeros_like(acc16_ref)

        expand, hsum = exp_ref[...], hs_ref[...]
        dt, a, lcs, llast, ea, ds, dtx = _ssd_common(dtraw_ref, dtb_ref, alog_ref, tril_ref[...], expand, l_s, lt_s)
        xs = xs_ref[...]
        xg = xs * dtx
        xgb = xg.astype(BF16)
        xgdf = xg * ds
        xgd = xgdf.astype(BF16)
        dsk_v, nwv = dsk_ref[...], nw_ref[...]
        z, y = z_ref[...], y_ref[...]
        sz = _sig(z)
        silz = z * sz
        yz = y * silz
        dyn = dyn_ref[...]
        dyz_parts, dnw_parts = [], []
        for g in range(2):
            gs = slice(g * 512, (g + 1) * 512)
            v = yz[:, gs]
            r = lax.rsqrt(jnp.mean(v * v, axis=-1, keepdims=True) + EPS)
            yhat = v * r
            dnw_parts.append(_colsum(dyn[:, gs] * yhat))
            dw = dyn[:, gs] * nwv[:, gs]
            dyz_parts.append(r * (dw - yhat * jnp.mean(dw * yhat, axis=-1, keepdims=True)))
        dyz = jnp.concatenate(dyz_parts, axis=1)
        dy = dyz * silz
        dz_ref[...] = (dyz * y * (sz * (1.0 + z * (1.0 - sz)))).astype(BF16)
        acc_ref[0:1, :] += jnp.concatenate(dnw_parts, axis=1)
        acc_ref[1:2, :] += _colsum(dy * xs)
        dyb = dy.astype(BF16)
        dq = (dy * ea).astype(BF16)
        dcar = dcarry[...]
        dcarb = dcar.astype(BF16)
        prev = prev_ref[0]
        prevb = prev.astype(BF16)
        ri = lax.broadcasted_iota(jnp.int32, (CHUNK, CHUNK), 0)
        ci = lax.broadcasted_iota(jnp.int32, (CHUNK, CHUNK), 1)
        causal = ri >= ci
        lane = lax.broadcasted_iota(jnp.int32, (1, 128), 1)
        dprev, dxgd, yoff = [], [], []
        dl_l = jnp.zeros((CHUNK, CHUNK), F32)
        dl_s = jnp.zeros((CHUNK, CHUNK), F32)
        for g in range(2):
            gs = slice(g * 512, (g + 1) * 512)
            bg = b_ref[:, g * 128:(g + 1) * 128].astype(BF16)
            cg = c_ref[:, g * 128:(g + 1) * 128].astype(BF16)
            sc = _dot(cg, bg, "nt")
            yoff.append(_dot(cg, prevb[:, gs]))
            dcg = _dot(dq[:, gs], prevb[:, gs], "nt")
            dprev.append(_dot(cg, dq[:, gs], "tn"))
            dbg = _dot(xgd[:, gs], dcarb[:, gs], "nt")
            dxgd.append(_dot(bg, dcarb[:, gs]))
            dsc = jnp.zeros((CHUNK, CHUNK), F32)
            for pr in range(4):
                col = g * 512 + pr * 128
                xp = xgb[:, col:col + 128]
                dyp = dyb[:, col:col + 128]
                acc = jnp.zeros((CHUNK, 128), F32)
                for half in range(2):
                    h = g * 8 + pr * 2 + half
                    dec = _decay(lcs, lt_s, h, causal)
                    mf = sc * dec
                    keep = (lane < HEAD_DIM) if half == 0 else (lane >= HEAD_DIM)
                    dyh = jnp.where(keep, dyp, jnp.zeros_like(dyp))
                    dm = _dot(dyh, xp, "nt")
                    acc = acc + _dot(mf.astype(BF16), dyh, "tn")
                    dsc = dsc + dm * dec
                    gm = dm * mf
                    dl_l = dl_l + jnp.where(ci == h, jnp.sum(gm, axis=1, keepdims=True), 0.0)
                    dl_s = dl_s + jnp.where(ri == h, jnp.sum(gm, axis=0, keepdims=True), 0.0)
                dxg_s[:, col:col + 128] = acc
            dscb = dsc.astype(BF16)
            dcg = dcg + _dot(dscb, bg)
            dbg = dbg + _dot(dscb, cg, "tn")
            du_ref[:, 1024 + g * 128:1024 + (g + 1) * 128] = dbg
            du_ref[:, 1280 + g * 128:1280 + (g + 1) * 128] = dcg
        dxgd = jnp.concatenate(dxgd, axis=1)
        dxg = dxg_s[...] + dxgd * ds
        du_ref[:, 0:1024] = dy * dsk_v + dxg * dtx
        hs1 = _dotx_r(dxg * xs, hsum)
        yoff = jnp.concatenate(yoff, axis=1) * ea
        dl = dl_l - dl_s.T + _dotx_r(dy * yoff - xgdf * dxgd, hsum)
        rows8 = lax.broadcasted_iota(jnp.int32, (8, D_MODEL), 0)
        two = jnp.where(rows8 == 0, _colsum(dxgd * xgdf), jnp.where(rows8 == 1, _colsum(dcar * prev), 0.0))
        two = _dotx_r(two, hsum)
        r8 = lax.broadcasted_iota(jnp.int32, (8, 128), 0)
        dllast = _colsum(jnp.where(r8 == 0, two, 0.0)) + _colsum(jnp.where(r8 == 1, two, 0.0)) * jnp.exp(llast)
        rowi = lax.broadcasted_iota(jnp.int32, (CHUNK, 128), 0)
        dl = dl + jnp.where(rowi == CHUNK - 1, dllast, 0.0)
        dadt = _dotx_l(triu_ref[...], dl)
        ddt = dadt * a + hs1
        draw = ddt * _sig(dtraw_ref[...] + dtb_ref[...])
        ddt_ref[...] = draw.astype(BF16)
        acc16_ref[0:1, :] += _colsum(draw)
        acc16_ref[1:2, :] += _colsum(dadt * dt) * a
        dcarry[...] = dcar * jnp.max(_dotx_r(jnp.exp(llast) + jnp.zeros((8, 128), F32), expand), axis=0, keepdims=True) \
            + jnp.concatenate(dprev, axis=1)

        @pl.when(i == nc - 1)
        def _():
            hd = _dotx_r(acc_ref[...], hsum)
            acc16_ref[2:3, :] = _colsum(jnp.where(lax.broadcasted_iota(jnp.int32, (8, 128), 0) == 1, hd, 0.0))

    rev = lambda w, col: pl.BlockSpec((CHUNK, w), lambda i: (nc - 1 - i, col))
    return pl.pallas_call(
        body, name=name, grid=(nc,),
        in_specs=[rev(1024, 0), rev(256, 4), rev(256, 5), rev(128, DT_COL), rev(1024, 0), rev(1024, 0),
                  pl.BlockSpec((1, 128, D_MODEL), lambda i: (nc - 1 - i, 0, 0)), rev(1024, 0),
                  _fix_spec((1, 128)), _fix_spec((1, 128)), _fix_spec((1, D_MODEL)), _fix_spec((1, D_MODEL)),
                  _fix_spec((CHUNK, CHUNK)), _fix_spec((CHUNK, CHUNK)), _fix_spec((128, D_MODEL)),
                  _fix_spec((D_MODEL, 128))],
        out_specs=[rev(D_CONV, 0), rev(128, 0), rev(1024, 0), _fix_spec((8, D_MODEL)), _fix_spec((8, 128))],
        out_shape=[jax.ShapeDtypeStruct((t, D_CONV), F32), jax.ShapeDtypeStruct((t, 128), BF16),
                   jax.ShapeDtypeStruct((t, D_MODEL), BF16), jax.ShapeDtypeStruct((8, D_MODEL), F32),
                   jax.ShapeDtypeStruct((8, 128), F32)],
        scratch_shapes=[pltpu.VMEM((128, D_MODEL), F32), pltpu.VMEM((128, 128), F32), pltpu.VMEM((128, 128), F32),
                        pltpu.VMEM((CHUNK, D_MODEL), F32)],
        compiler_params=_cp(("arbitrary",)),
    )(u, u, u, proj, proj, y, prev, dycat, dtb, alog, dsk, nw,
      cst["tril"], cst["triu"], cst["expand"], cst["hsum"])


def _head_rms(v, hsum, expand):
    ms = _dotx_r(v * v, hsum) * (1.0 / HEAD_DIM)
    return _dotx_r(lax.rsqrt(ms + EPS), expand)


def _qk_fwd(proj, qw, kw, cst, name):
    t = proj.shape[0]
    tm = min(t, 256)
    scale = HEAD_DIM ** -0.5

    def body(q_ref, k_ref, v_ref, qw_ref, kw_ref, hs_ref, exp_ref, qs_ref, kn_ref, vb_ref):
        hsum, expand = hs_ref[...], exp_ref[...]
        q, k = q_ref[...], k_ref[...]
        qs_ref[...] = (q * _head_rms(q, hsum, expand) * qw_ref[...] * scale).astype(BF16)
        kn_ref[...] = (k * _head_rms(k, hsum, expand) * kw_ref[...]).astype(BF16)
        vb_ref[...] = v_ref[...].astype(BF16)

    return pl.pallas_call(
        body, name=name, grid=(t // tm,),
        in_specs=[_row_spec(tm, col=1), _row_spec(tm, col=2), _row_spec(tm, col=3),
                  _fix_spec((1, D_MODEL)), _fix_spec((1, D_MODEL)), _fix_spec((D_MODEL, 128)),
                  _fix_spec((128, D_MODEL))],
        out_specs=[_row_spec(tm)] * 3, out_shape=[jax.ShapeDtypeStruct((t, D_MODEL), BF16)] * 3,
        compiler_params=_cp(("parallel",)),
    )(proj, proj, proj, qw, kw, cst["hsum"], cst["expand"])


def _qk_bwd(proj, dqs, dkn, dv, qw, kw, cst, name):
    t = proj.shape[0]
    tm = min(t, 256)
    scale = HEAD_DIM ** -0.5

    def body(q_ref, k_ref, dq_ref, dk_ref, dv_ref, qw_ref, kw_ref, hs_ref, exp_ref, fold_ref,
             oq_ref, ok_ref, ov_ref, dw_ref):
        i = pl.program_id(0)
        hsum, expand = hs_ref[...], exp_ref[...]
        rows8 = lax.broadcasted_iota(jnp.int32, (8, D_MODEL), 0)
        sums = jnp.zeros((8, D_MODEL), F32)
        for n, (x_ref, d_ref, w_ref, o_ref, sc) in enumerate(
                [(q_ref, dq_ref, qw_ref, oq_ref, scale), (k_ref, dk_ref, kw_ref, ok_ref, 1.0)]):
            xv = x_ref[...]
            r = _head_rms(xv, hsum, expand)
            xhat = xv * r
            dn = d_ref[...] * sc
            sums = sums + jnp.where(rows8 == n, _colsum(dn * xhat), 0.0)
            dw = dn * w_ref[...]
            mean = _dotx_r(_dotx_r(dw * xhat, hsum), expand) * (1.0 / HEAD_DIM)
            o_ref[...] = (r * (dw - xhat * mean)).astype(BF16)
        ov_ref[...] = dv_ref[...].astype(BF16)
        folded = _dotx_r(sums, fold_ref[...])

        @pl.when(i == 0)
        def _():
            dw_ref[...] = folded

        @pl.when(i > 0)
        def _():
            dw_ref[...] += folded

    return pl.pallas_call(
        body, name=name, grid=(t // tm,),
        in_specs=[_row_spec(tm, col=1), _row_spec(tm, col=2), _row_spec(tm), _row_spec(tm), _row_spec(tm),
                  _fix_spec((1, D_MODEL)), _fix_spec((1, D_MODEL)), _fix_spec((D_MODEL, 128)),
                  _fix_spec((128, D_MODEL)), _fix_spec((D_MODEL, 128))],
        out_specs=[_row_spec(tm)] * 3 + [_fix_spec((8, 128))],
        out_shape=[jax.ShapeDtypeStruct((t, D_MODEL), BF16)] * 3 + [jax.ShapeDtypeStruct((8, 128), F32)],
        compiler_params=_cp(("arbitrary",)),
    )(proj, proj, dqs, dkn, dv, qw, kw, cst["hsum"], cst["expand"], cst["fold"])


def _sb_masks(i, kb, tq, tk):
    tpos = i * tq + lax.broadcasted_iota(jnp.int32, (tq, 1), 0)
    spos = kb * tk + lax.broadcasted_iota(jnp.int32, (1, tk), 1)
    return spos < tpos


def _sb_fwd(qs, kn, vb, cst, name):
    t = qs.shape[0]
    tq = tk = min(t, SB_TILE)
    nq = t // tq

    def body(q_ref, k_ref, v_ref, u_ref, rt_ref, ob_ref, acc, rs):
        i = pl.program_id(1)
        lane = lax.broadcasted_iota(jnp.int32, (1, 128), 1)
        q2 = q_ref[...]
        zero = jnp.zeros_like(q2)
        qh = [jnp.where(lane < HEAD_DIM, q2, zero), jnp.where(lane >= HEAD_DIM, q2, zero)]
        acc[...] = jnp.zeros_like(acc)
        rs[...] = jnp.zeros_like(rs)
        ustrict = u_ref[...]

        def tile(kb, masked):
            off = pl.multiple_of(kb * tk, tk)
            k2 = k_ref[pl.ds(off, tk), :]
            v2 = v_ref[pl.ds(off, tk), :]
            strict = _sb_masks(i, kb, tq, tk) if masked else None
            s = [_dot(qh[h], k2, "nt") for h in range(2)]
            a, r, lb = [None] * 2, [None] * 2, [None] * 2
            for h in range(2):
                sp = _softplus(s[h])
                a[h] = s[h] - sp
                r[h] = jnp.where(strict, -sp, 0.0) if masked else -sp
                lb[h] = _dot2(r[h], ustrict)
            for h in range(2):
                lw = a[h] + lb[h] + rs[h]
                w = jnp.exp(jnp.where(strict, lw, -1e30) if masked else lw)
                rs[h] = rs[h] + jnp.sum(r[h], axis=1, keepdims=True)
                acc[h] = acc[h] + _dot(w.astype(BF16), v2)

        tile(i, True)

        def step(n, carry):
            tile(i - 1 - n, False)
            return carry

        lax.fori_loop(0, i, step, 0)
        rt_ref[...] = jnp.where(lane < HEAD_DIM, rs[0], rs[1])
        ob_ref[...] = jnp.where(lane < HEAD_DIM, acc[0], acc[1]).astype(BF16)

    return pl.pallas_call(
        body, name=name, grid=(D_MODEL // 128, nq),
        in_specs=[pl.BlockSpec((tq, 128), lambda j, i: (i, j)), pl.BlockSpec((t, 128), lambda j, i: (0, j)),
                  pl.BlockSpec((t, 128), lambda j, i: (0, j)), _fix_spec((2 * tk, tk))],
        out_specs=[pl.BlockSpec((tq, 128), lambda j, i: (i, j))] * 2,
        out_shape=[jax.ShapeDtypeStruct((t, D_MODEL), F32), jax.ShapeDtypeStruct((t, D_MODEL), BF16)],
        scratch_shapes=[pltpu.VMEM((2, tq, 128), F32), pltpu.VMEM((2, tq, 1), F32)],
        compiler_params=_cp(("parallel", "parallel")),
    )(qs, kn, vb, _doubled(cst["ustrict"], tk))


def _sb_bwd(qs, kn, vb, rtot, dycat, cst, name):
    t = qs.shape[0]
    tq = tk = min(t, SB_TILE)
    nq = t // tq

    def body(q_ref, k_ref, v_ref, rt_ref, do_ref, us_ref, ui_ref, dq_ref, dk_ref, dv_ref, acc, rs, es):
        i = pl.program_id(1)
        lane = lax.broadcasted_iota(jnp.int32, (1, 128), 1)
        keep = [lane < HEAD_DIM, lane >= HEAD_DIM]
        q2, rt = q_ref[...], rt_ref[...]
        do2b = do_ref[...].astype(BF16)
        qh = [jnp.where(kp, q2, jnp.zeros_like(q2)) for kp in keep]
        doh = [jnp.where(kp, do2b, jnp.zeros_like(do2b)) for kp in keep]
        rtot_h = [jnp.sum(jnp.where(lane == n * HEAD_DIM, rt, 0.0), axis=1, keepdims=True) for n in range(2)]
        acc[...] = jnp.zeros_like(acc)
        rs[...] = jnp.zeros_like(rs)
        es[...] = jnp.zeros_like(es)

        @pl.when(i == 0)
        def _():
            dk_ref[...] = jnp.zeros_like(dk_ref)
            dv_ref[...] = jnp.zeros_like(dv_ref)

        ule, ult = us_ref[...], ui_ref[...]

        def tile(kb, masked):
            off = pl.multiple_of(kb * tk, tk)
            k2 = k_ref[pl.ds(off, tk), :]
            v2 = v_ref[pl.ds(off, tk), :]
            strict = _sb_masks(i, kb, tq, tk) if masked else None
            s = [_dot(qh[h], k2, "nt") for h in range(2)]
            dw = [_dot(doh[h], v2, "nt") for h in range(2)]
            a, sg, r, pin, w, e, cin = ([None] * 2 for _ in range(7))
            for h in range(2):
                sp = _softplus(s[h])
                a[h] = s[h] - sp
                sg[h] = jnp.exp(a[h])
                r[h] = jnp.where(strict, -sp, 0.0) if masked else -sp
                pin[h] = _dot2(r[h], ule)
            for h in range(2):
                lw = a[h] + ((rtot_h[h] - rs[h]) - pin[h])
                w[h] = jnp.exp(jnp.where(strict, lw, -1e30) if masked else lw)
                e[h] = w[h] * dw[h]
                cin[h] = _dot2(e[h], ult)
            dk_t = jnp.zeros((tk, 128), F32)
            dv_t = jnp.zeros((tk, 128), F32)
            for h in range(2):
                dl = e[h] * (1.0 - sg[h]) - (es[h] + cin[h]) * sg[h]
                dl = (jnp.where(strict, dl, 0.0) if masked else dl).astype(BF16)
                rs[h] = rs[h] + jnp.sum(r[h], axis=1, keepdims=True)
                es[h] = es[h] + jnp.sum(e[h], axis=1, keepdims=True)
                acc[h] = acc[h] + _dot(dl, k2)
                dk_t = dk_t + _dot(dl, qh[h], "tn")
                dv_t = dv_t + _dot(w[h].astype(BF16), doh[h], "tn")
            dk_ref[pl.ds(off, tk), :] += dk_t
            dv_ref[pl.ds(off, tk), :] += dv_t

        def step(kb, carry):
            tile(kb, False)
            return carry

        lax.fori_loop(0, i, step, 0)
        tile(i, True)
        dq_ref[...] = jnp.where(lane < HEAD_DIM, acc[0], acc[1])

    return pl.pallas_call(
        body, name=name, grid=(D_MODEL // 128, nq),
        in_specs=[pl.BlockSpec((tq, 128), lambda j, i: (i, j)), pl.BlockSpec((t, 128), lambda j, i: (0, j)),
                  pl.BlockSpec((t, 128), lambda j, i: (0, j)), pl.BlockSpec((tq, 128), lambda j, i: (i, j)),
                  pl.BlockSpec((tq, 128), lambda j, i: (i, D_MODEL // 128 + j)),
                  _fix_spec((2 * tk, tk)), _fix_spec((2 * tk, tk))],
        out_specs=[pl.BlockSpec((tq, 128), lambda j, i: (i, j)), pl.BlockSpec((t, 128), lambda j, i: (0, j)),
                   pl.BlockSpec((t, 128), lambda j, i: (0, j))],
        out_shape=[jax.ShapeDtypeStruct((t, D_MODEL), F32)] * 3,
        scratch_shapes=[pltpu.VMEM((2, tq, 128), F32), pltpu.VMEM((2, tq, 1), F32), pltpu.VMEM((2, tq, 1), F32)],
        compiler_params=_cp(("parallel", "arbitrary")),
    )(qs, kn, vb, rtot, dycat, _doubled(cst["ule"], tk), _doubled(cst["ult"], tk))


def _adamw(w, g, m, v, name):
    rows, cols = w.shape
    tr = rows
    for cand in (256, 128, 64, 32, 16, 8):
        if rows % cand == 0 and rows > cand:
            tr = cand
            break
    c1 = 1.0 - ADAM_B1 ** ADAM_STEP
    c2 = 1.0 - ADAM_B2 ** ADAM_STEP

    def body(w_ref, g_ref, m_ref, v_ref, d_ref, nm_ref, nv_ref):
        gv = g_ref[...]
        nm = ADAM_B1 * m_ref[...] + (1.0 - ADAM_B1) * gv
        nv = ADAM_B2 * v_ref[...] + (1.0 - ADAM_B2) * (gv * gv)
        nm_ref[...] = nm
        nv_ref[...] = nv
        d_ref[...] = -ADAM_LR * ((nm / c1) / (jnp.sqrt(nv / c2) + ADAM_EPS) + ADAM_WD * w_ref[...])

    spec = pl.BlockSpec((tr, cols), lambda i: (i, 0))
    return pl.pallas_call(
        body, name=name, grid=(rows // tr,), in_specs=[spec] * 4, out_specs=[spec] * 3,
        out_shape=[jax.ShapeDtypeStruct(w.shape, F32)] * 3, compiler_params=_cp(("parallel",)),
    )(w, g, m, v)


def _place():
    x, y, c = lax.axis_index("x"), lax.axis_index("y"), lax.axis_index("c")
    chips = [(1 - x, y), (x, 1 - y), (1 - x, 1 - y)]
    return x, y, c, chips


VM = pl.BlockSpec(memory_space=pltpu.VMEM)
HB = pl.BlockSpec(memory_space=pltpu.HBM)


def _small_gather(pack, name, reduce):
    rows = pack.shape[0]

    def body(p_ref, gat_ref, *rest):
        if reduce:
            sum_ref, ss, rs = rest
        else:
            ss, rs = rest
        x, y, c, _ = _place()
        me = 4 * x + 2 * y + c
        peers = [(x, y, 1 - c), (1 - x, y, c), (x, 1 - y, c), (1 - x, 1 - y, c),
                 (1 - x, y, 1 - c), (x, 1 - y, 1 - c), (1 - x, 1 - y, 1 - c)]

        def copy(k, slot, to):
            return pltpu.make_async_remote_copy(src_ref=p_ref, dst_ref=gat_ref.at[slot], send_sem=ss.at[k],
                                                recv_sem=rs.at[k], device_id=to, device_id_type=MESH)

        sends = [copy(k, me, p) for k, p in enumerate(peers)]
        for s in sends:
            s.start()
        gat_ref[me] = p_ref[...]
        for k, p in enumerate(peers):
            copy(k, 4 * p[0] + 2 * p[1] + p[2], p).wait_recv()
        for s in sends:
            s.wait_send()
        if reduce:
            tot = gat_ref[0]
            for b in range(1, 8):
                tot = tot + gat_ref[b]
            sum_ref[...] = tot

    out_shape = [jax.ShapeDtypeStruct((8, rows, D_MODEL), F32)]
    if reduce:
        out_shape.append(jax.ShapeDtypeStruct((rows, D_MODEL), F32))
    return pl.pallas_call(
        body, name=name, in_specs=[VM], out_specs=[VM] * len(out_shape), out_shape=out_shape,
        scratch_shapes=[pltpu.SemaphoreType.DMA((7,)), pltpu.SemaphoreType.DMA((7,))],
        compiler_params=_cp(),
    )(pack)


def _mod_exchange(cond, w_ada, b_shard, name):
    def body(c_ref, w_ref, b_ref, modp_ref, ss, rs):
        x, y, c, chips = _place()
        sh = 2 * x + y
        cv = c_ref[...]
        cv = cv * _sig(cv)
        modp_ref[sh] = jnp.dot(cv, w_ref[...], precision=lax.Precision.HIGHEST,
                               preferred_element_type=F32) + b_ref[...]

        def copy(k, slot, to):
            return pltpu.make_async_remote_copy(src_ref=modp_ref.at[slot], dst_ref=modp_ref.at[slot],
                                                send_sem=ss.at[k], recv_sem=rs.at[k], device_id=to,
                                                device_id_type=MESH)

        sends = [copy(k, sh, (*ch, c)) for k, ch in enumerate(chips)]
        for s in sends:
            s.start()
        for k, ch in enumerate(chips):
            copy(k, 2 * ch[0] + ch[1], (*ch, c)).wait_recv()
        for s in sends:
            s.wait_send()

    return pl.pallas_call(
        body, name=name, in_specs=[VM, VM, VM], out_specs=VM,
        out_shape=jax.ShapeDtypeStruct((N_SHARDS, 8, 6 * D_MODEL // N_SHARDS), F32),
        scratch_shapes=[pltpu.SemaphoreType.DMA((3,)), pltpu.SemaphoreType.DMA((3,))],
        compiler_params=_cp(),
    )(cond, w_ada, b_shard)


def _gather_weights(pack, name):
    def body(p_ref, out_ref, ss, rs):
        x, y, c, chips = _place()
        sh = 2 * x + y
        sib = (x, y, 1 - c)

        def half(slot, hc):
            return out_ref.at[slot, pl.ds(hc * PACK_HALF, PACK_HALF), :]

        def copy(k, src, slot, hc, to):
            return pltpu.make_async_remote_copy(src_ref=src, dst_ref=half(slot, hc), send_sem=ss.at[k],
                                                recv_sem=rs.at[k], device_id=to, device_id_type=MESH)

        first = [copy(j, p_ref.at[pl.ds(c * PACK_HALF, PACK_HALF), :], sh, c, (*ch, c))
                 for j, ch in enumerate(chips)]
        for cp in first:
            cp.start()
        passed = []
        for j, ch in enumerate(chips):
            slot = 2 * ch[0] + ch[1]
            copy(j, half(slot, c), slot, c, (*ch, c)).wait_recv()
            fwd = copy(3 + j, half(slot, c), slot, c, sib)
            fwd.start()
            passed.append(fwd)
        for j, ch in enumerate(chips):
            slot = 2 * ch[0] + ch[1]
            copy(3 + j, half(slot, 1 - c), slot, 1 - c, sib).wait_recv()
        for cp in first + passed:
            cp.wait_send()

    return pl.pallas_call(
        body, name=name, in_specs=[HB], out_specs=HB,
        out_shape=jax.ShapeDtypeStruct((N_SHARDS, PACK_ROWS, D_MODEL), BF16),
        scratch_shapes=[pltpu.SemaphoreType.DMA((6,)), pltpu.SemaphoreType.DMA((6,))],
        compiler_params=_cp(),
    )(pack)


def _sibling_swap(g, name):
    def body(g_ref, out_ref, ss, rs):
        x, y, c, _ = _place()
        cp = pltpu.make_async_remote_copy(
            src_ref=g_ref.at[pl.ds(0, N_SHARDS), pl.ds((1 - c) * PACK_HALF, PACK_HALF), :], dst_ref=out_ref,
            send_sem=ss, recv_sem=rs, device_id=(x, y, 1 - c), device_id_type=MESH)
        cp.start()
        cp.wait()

    return pl.pallas_call(
        body, name=name, in_specs=[HB], out_specs=HB,
        out_shape=jax.ShapeDtypeStruct((N_SHARDS, PACK_HALF, D_MODEL), g.dtype),
        scratch_shapes=[pltpu.SemaphoreType.DMA, pltpu.SemaphoreType.DMA],
        compiler_params=_cp(),
    )(g)


def _chip_sum(g, got, c_idx, name):
    tr = 512
    nb = PACK_HALF // tr

    def body(c_ref, a_ref, b_ref, s_ref, sb_ref):
        s = a_ref[...] + b_ref[...]
        s_ref[...] = s
        sb_ref[...] = s.astype(BF16)

    blk = pl.BlockSpec((1, tr, D_MODEL), lambda s, i, c_ref: (s, i, 0))
    return pl.pallas_call(
        body, name=name,
        grid_spec=pltpu.PrefetchScalarGridSpec(
            num_scalar_prefetch=1, grid=(N_SHARDS, nb),
            in_specs=[pl.BlockSpec((1, tr, D_MODEL), lambda s, i, c_ref: (s, c_ref[0] * nb + i, 0)), blk],
            out_specs=[blk, blk]),
        out_shape=[jax.ShapeDtypeStruct((N_SHARDS, PACK_HALF, D_MODEL), F32),
                   jax.ShapeDtypeStruct((N_SHARDS, PACK_HALF, D_MODEL), BF16)],
        compiler_params=_cp(("parallel", "parallel")),
    )(c_idx, g, got)


def _chip_exchange(sb, name):
    def body(s_ref, out_ref, ss, rs):
        x, y, c, chips = _place()

        def copy(k, slot, to):
            return pltpu.make_async_remote_copy(src_ref=s_ref.at[slot], dst_ref=out_ref.at[k], send_sem=ss.at[k],
                                                recv_sem=rs.at[k], device_id=to, device_id_type=MESH)

        sends = [copy(k, 2 * ch[0] + ch[1], (*ch, c)) for k, ch in enumerate(chips)]
        for cp in sends:
            cp.start()
        for cp in sends:
            cp.wait()

    return pl.pallas_call(
        body, name=name, in_specs=[HB], out_specs=HB,
        out_shape=jax.ShapeDtypeStruct((3, PACK_HALF, D_MODEL), BF16),
        scratch_shapes=[pltpu.SemaphoreType.DMA((3,)), pltpu.SemaphoreType.DMA((3,))],
        compiler_params=_cp(),
    )(sb)


def _total_half(s, got, sh_idx, name):
    tr = 512
    nb = PACK_HALF // tr

    def body(sh_ref, a_ref, r0, r1, r2, o_ref):
        o_ref[...] = ((a_ref[0] + r0[0].astype(F32)) + r1[0].astype(F32)) + r2[0].astype(F32)

    rspec = lambda k: pl.BlockSpec((1, tr, D_MODEL), lambda i, sh_ref: (k, i, 0))
    return pl.pallas_call(
        body, name=name,
        grid_spec=pltpu.PrefetchScalarGridSpec(
            num_scalar_prefetch=1, grid=(nb,),
            in_specs=[pl.BlockSpec((1, tr, D_MODEL), lambda i, sh_ref: (sh_ref[0], i, 0)),
                      rspec(0), rspec(1), rspec(2)],
            out_specs=pl.BlockSpec((tr, D_MODEL), lambda i, sh_ref: (i, 0))),
        out_shape=jax.ShapeDtypeStruct((PACK_HALF, D_MODEL), F32),
        compiler_params=_cp(("parallel",)),
    )(sh_idx, s, got, got, got)


def _join_halves(tot, name):
    def body(t_ref, out_ref, ss, rs):
        x, y, c, _ = _place()
        cp = pltpu.make_async_remote_copy(src_ref=t_ref, dst_ref=out_ref, send_sem=ss, recv_sem=rs,
                                          device_id=(x, y, 1 - c), device_id_type=MESH)
        cp.start()
        cp.wait()

    return pl.pallas_call(
        body, name=name, in_specs=[HB], out_specs=HB,
        out_shape=jax.ShapeDtypeStruct((PACK_HALF, D_MODEL), F32),
        scratch_shapes=[pltpu.SemaphoreType.DMA, pltpu.SemaphoreType.DMA],
        compiler_params=_cp(),
    )(tot)


def _w_ada_grad(cond, dmod_cols, name):
    def body(c_ref, d_ref, o_ref):
        cv = c_ref[...]
        cv = cv * _sig(cv)
        o_ref[...] = lax.dot_general(cv, d_ref[...], _DN["tn"], precision=lax.Precision.HIGHEST,
                                     preferred_element_type=F32)

    return pl.pallas_call(
        body, name=name, in_specs=[VM, VM], out_specs=VM,
        out_shape=jax.ShapeDtypeStruct((D_MODEL, dmod_cols.shape[1]), F32), compiler_params=_cp(),
    )(cond, dmod_cols)


def _pad_rows(a, rows):
    return jnp.pad(a, ((0, rows - a.shape[0]), (0, 0)))


def _pad_cols(a, cols):
    return jnp.pad(a, ((0, 0), (0, cols - a.shape[1])))


def _weight_pack(w_in, w_out, w_gate, w_up, w_down, dtype):
    parts = [_pad_rows(w_in.reshape(-1, D_MODEL), R_IN), w_out, w_gate.reshape(-1, D_MODEL),
             w_up.reshape(-1, D_MODEL), w_down]
    pack = jnp.concatenate([p.astype(dtype) for p in parts], axis=0)
    return _pad_rows(pack, PACK_ROWS)


def _unpack(p):
    o = 0
    out = []
    for r in (R_IN, R_OUT, R_FF, R_FF, R_FF):
        out.append(p[..., o:o + r, :])
        o += r
    return out


def kernel(x, c, w_ada, b_ada, norm1_w, w_in, conv_w, conv_b, dt_bias, a_log, d_skip, ssd_norm_w, q_norm_w, k_norm_w, w_out, norm2_w, w_gate, w_up, w_down, loss_target, m_w_ada, m_b_ada, m_norm1_w, m_w_in, m_conv_w, m_conv_b, m_dt_bias, m_a_log, m_d_skip, m_ssd_norm_w, m_q_norm_w, m_k_norm_w, m_w_out, m_norm2_w, m_w_gate, m_w_up, m_w_down, v_w_ada, v_b_ada, v_norm1_w, v_w_in, v_conv_w, v_conv_b, v_dt_bias, v_a_log, v_d_skip, v_ssd_norm_w, v_q_norm_w, v_k_norm_w, v_w_out, v_norm2_w, v_w_gate, v_w_up, v_w_down):
    cst = _consts()
    ax, ay, ac = lax.axis_index("x"), lax.axis_index("y"), lax.axis_index("c")
    shard = 2 * ax + ay
    me = 4 * ax + 2 * ay + ac
    xs = x[0]
    tgt = loss_target[0]
    w_in_cols = w_in.shape[2]
    conv_cols = conv_w.shape[2]

    cw_flat = _pad_cols(conv_w[0].reshape(1, -1), 2 * D_MODEL).reshape(2, D_MODEL)
    cpack = jnp.concatenate([jnp.broadcast_to(c, (8, D_MODEL)), _pad_rows(cw_flat, 8)], axis=0)
    gat = _small_gather(cpack, "gather_c", reduce=False)[0]
    c_all = gat[:, 0, :]
    cw = gat[0::2, 8:10, :].reshape(N_SHARDS, 2 * D_MODEL)[:, :4 * conv_cols].reshape(N_SHARDS, 4, conv_cols)
    conv_w_full = jnp.transpose(cw, (1, 0, 2)).reshape(4, D_CONV)

    mod_w = 6 * D_MODEL // N_SHARDS
    b_shard = lax.dynamic_slice(b_ada, (0, shard * mod_w), (1, mod_w))
    modp = _mod_exchange(c_all, w_ada[0], b_shard, "mod_exchange")
    mod_mine = lax.dynamic_slice(modp, (0, me, 0), (N_SHARDS, 1, mod_w)).reshape(6, D_MODEL)
    mod = _pad_rows(mod_mine, 8)

    wpack = _weight_pack(w_in[0], w_out[0], w_gate[0], w_up[0], w_down[0], BF16)
    gp = _gather_weights(wpack, "gather_weights")
    own = lax.broadcasted_iota(jnp.int32, (N_SHARDS, 1, 1), 0) == shard
    p_in, p_out, p_gate, p_up, p_down = _unpack(jnp.where(own, wpack[None], gp))

    def cols(p, ncol):
        return jnp.concatenate([p[s].reshape(-1)[:D_MODEL * ncol].reshape(D_MODEL, ncol) for s in range(N_SHARDS)],
                               axis=1)

    wi = cols(p_in, w_in_cols)
    w_inp = jnp.concatenate([wi[:, 0:1024], wi[:, 2576:5648], wi[:, 1024:2560], wi[:, 2560:2576],
                             jnp.zeros((D_MODEL, 112), BF16)], axis=1)
    w_o = p_out.reshape(2 * D_MODEL, D_MODEL)
    w_gu = jnp.concatenate([cols(p_gate, D_FF // N_SHARDS), cols(p_up, D_FF // N_SHARDS)], axis=1)
    w_d = p_down.reshape(D_FF, D_MODEL)

    pad128 = lambda a: _pad_cols(a, 128)
    dtb, alog = pad128(dt_bias), pad128(a_log)
    dsk = jnp.repeat(d_skip, HEAD_DIM, axis=1)
    qw, kw = jnp.tile(q_norm_w, (1, N_HEADS)), jnp.tile(k_norm_w, (1, N_HEADS))

    h1 = _norm_mod(xs, norm1_w, mod, 0, "norm1")
    proj = _matmul(h1, w_inp, "nn", F32, "in_proj")
    u = _conv_fwd(proj, conv_w_full, conv_b, "conv_fwd")
    y_ssd, yn, prev = _ssd_fwd(u, proj, dtb, alog, dsk, ssd_norm_w, cst, "ssd_fwd")
    qs, kn, vb = _qk_fwd(proj, qw, kw, cst, "qk_norm")
    rtot, ob = _sb_fwd(qs, kn, vb, cst, "sb_fwd")
    ycat = jnp.concatenate([yn, ob], axis=1)
    mix = _matmul(ycat, w_o, "nn", F32, "out_proj")
    x1, h2 = _resid_norm(xs, mix, norm2_w, mod, "resid_norm2")
    gu = _matmul(h2, w_gu, "nn", F32, "ffn_in")
    act = _act_fwd(gu, "ffn_act")
    ffn = _matmul(act, w_d, "nn", F32, "ffn_out", tk_cap=1408)
    dffn, dout, dg2, loss8 = _loss_head(x1, ffn, tgt, mod, "loss_head")
    loss = lax.psum(loss8[0, 0], ("x", "y", "c"))

    dact = _matmul(dffn, w_d, "nt", F32, "d_act")
    g_down = _matmul(act, dffn, "tn", F32, "g_w_down", tm_cap=1408)
    dgu = _act_bwd(dact, gu, "ffn_act_bwd")
    dh2 = _matmul(dgu, w_gu, "nt", F32, "d_h2", tk_cap=1408)
    g_gu = _matmul(h2, dgu, "tn", F32, "g_w_gu")
    dx1, dmix, acc2 = _norm_bwd(dh2, x1, dout, mix, norm2_w, mod, 3, 2, "norm2_bwd")
    dycat = _matmul(dmix, w_o, "nt", F32, "d_ycat")
    g_out = _matmul(ycat, dmix, "tn", F32, "g_w_out")
    du, ddt, dz, acc_ssd, acc16 = _ssd_bwd(u, proj, y_ssd, prev, dycat, dtb, alog, dsk, ssd_norm_w, cst, "ssd_bwd")
    dqs, dkn, dv = _sb_bwd(qs, kn, vb, rtot, dycat, cst, "sb_bwd")
    dq, dk, dvb, acc_qk = _qk_bwd(proj, dqs, dkn, dv, qw, kw, cst, "qk_norm_bwd")
    dxbc, g_conv_w, g_conv_b = _conv_bwd(proj, du, conv_w_full, conv_b, "conv_bwd")
    dproj = jnp.concatenate([dz, dq, dk, dvb, dxbc, ddt], axis=1)
    dh1 = _matmul(dproj, w_inp, "nt", F32, "d_h1", tk_cap=1152)
    g_inp = _matmul(h1, dproj, "tn", F32, "g_w_in")
    grad_x, acc1 = _norm_bwd(dh1, xs, dx1, None, norm1_w, mod, 0, None, "norm1_bwd")

    last = jnp.concatenate([acc_qk[0:1, 0:64], acc_qk[1:2, 0:64], acc16[0:1, 0:16], acc16[1:2, 0:16],
                            acc16[2:3, 0:16]], axis=1)
    spack = jnp.concatenate([
        acc1[0:2], acc2[3:4], acc2[0:2], dg2,
        acc1[2:3], acc2[2:3], acc_ssd[0:1],
        _pad_cols(g_conv_b, 2 * D_MODEL).reshape(2, D_MODEL),
        g_conv_w.reshape(6, D_MODEL),
        _pad_cols(last, D_MODEL)], axis=0)
    sgat, ssum = _small_gather(_pad_rows(spack, SMALL_ROWS), "gather_small", reduce=True)
    g_b_ada = ssum[0:6].reshape(1, 6 * D_MODEL)
    g_norm1, g_norm2, g_ssdn = ssum[6:7], ssum[7:8], ssum[8:9]
    g_cb = ssum[9:11].reshape(1, 2 * D_MODEL)[:, :D_CONV]
    g_cw = lax.dynamic_slice(ssum[11:17].reshape(4, D_CONV), (0, shard * conv_cols), (4, conv_cols))
    g_qn, g_kn = ssum[17:18, 0:64], ssum[17:18, 64:128]
    g_dtb, g_alog, g_dsk = ssum[17:18, 128:144], ssum[17:18, 144:160], ssum[17:18, 160:176]
    dmod_all = sgat[:, 0:6, :].reshape(8, 6 * D_MODEL)
    g_w_ada = _w_ada_grad(c_all, lax.dynamic_slice(dmod_all, (0, shard * mod_w), (8, mod_w)), "g_w_ada")

    gi = jnp.concatenate([g_inp[:, 0:1024], g_inp[:, 4096:5632], g_inp[:, 5632:5648], g_inp[:, 1024:4096]], axis=1)
    ff = D_FF // N_SHARDS

    def shards_of_cols(g, ncol, rows):
        return jnp.stack([_pad_rows(g[:, s * ncol:(s + 1) * ncol].reshape(-1, D_MODEL), rows)
                          for s in range(N_SHARDS)])

    gpack = jnp.concatenate([
        shards_of_cols(gi, w_in_cols, R_IN), g_out.reshape(N_SHARDS, R_OUT, D_MODEL),
        shards_of_cols(g_gu[:, :D_FF], ff, R_FF), shards_of_cols(g_gu[:, D_FF:], ff, R_FF),
        g_down.reshape(N_SHARDS, R_FF, D_MODEL),
        jnp.zeros((N_SHARDS, PACK_ROWS - R_IN - R_OUT - 3 * R_FF, D_MODEL), F32)], axis=1)
    got = _sibling_swap(gpack, "rs_sibling_swap")
    csum, csum_b = _chip_sum(gpack, got, ac.reshape(1).astype(jnp.int32), "rs_chip_sum")
    got2 = _chip_exchange(csum_b, "rs_chip_exchange")
    tot = _total_half(csum, got2, shard.reshape(1).astype(jnp.int32), "rs_total")
    other = _join_halves(tot, "rs_join")
    gfull = jnp.where(ac == 0, jnp.concatenate([tot, other], axis=0), jnp.concatenate([other, tot], axis=0))
    r_in, r_out, r_gate, r_up, r_down = _unpack(gfull)
    g_w_in = r_in.reshape(-1)[:D_MODEL * w_in_cols].reshape(D_MODEL, w_in_cols)
    g_w_gate, g_w_up = r_gate.reshape(D_MODEL, ff), r_up.reshape(D_MODEL, ff)

    grads = dict(w_ada=g_w_ada, b_ada=g_b_ada, norm1_w=g_norm1, w_in=g_w_in, conv_w=g_cw, conv_b=g_cb,
                 dt_bias=g_dtb, a_log=g_alog, d_skip=g_dsk, ssd_norm_w=g_ssdn, q_norm_w=g_qn, k_norm_w=g_kn,
                 w_out=r_out, norm2_w=g_norm2, w_gate=g_w_gate, w_up=g_w_up, w_down=r_down)
    weights = dict(w_ada=(w_ada, m_w_ada, v_w_ada), b_ada=(b_ada, m_b_ada, v_b_ada),
                   norm1_w=(norm1_w, m_norm1_w, v_norm1_w), w_in=(w_in, m_w_in, v_w_in),
                   conv_w=(conv_w, m_conv_w, v_conv_w), conv_b=(conv_b, m_conv_b, v_conv_b),
                   dt_bias=(dt_bias, m_dt_bias, v_dt_bias), a_log=(a_log, m_a_log, v_a_log),
                   d_skip=(d_skip, m_d_skip, v_d_skip), ssd_norm_w=(ssd_norm_w, m_ssd_norm_w, v_ssd_norm_w),
                   q_norm_w=(q_norm_w, m_q_norm_w, v_q_norm_w), k_norm_w=(k_norm_w, m_k_norm_w, v_k_norm_w),
                   w_out=(w_out, m_w_out, v_w_out), norm2_w=(norm2_w, m_norm2_w, v_norm2_w),
                   w_gate=(w_gate, m_w_gate, v_w_gate), w_up=(w_up, m_w_up, v_w_up),
                   w_down=(w_down, m_w_down, v_w_down))
    names = list(weights)
    g_out_l, d_out_l, m_out_l, v_out_l = [], [], [], []
    for n in names:
        w, m, v = weights[n]
        shp = w.shape
        w2, m2, v2 = (a.reshape(shp[-2], shp[-1]) if a.ndim == 3 else a for a in (w, m, v))
        g2 = grads[n].reshape(w2.shape)
        d, nm, nv = _adamw(w2, g2, m2, v2, "adamw_" + n)
        g_out_l.append(g2.reshape(shp))
        d_out_l.append(d.reshape(shp))
        m_out_l.append(nm.reshape(shp))
        v_out_l.append(nv.reshape(shp))
    return (loss, grad_x[None], *g_out_l, *d_out_l, *m_out_l, *v_out_l)
```

```python
import functools

import numpy as np
import jax
import jax.numpy as jnp
from jax import lax
from jax.experimental import pallas as pl
from jax.experimental.pallas import tpu as pltpu

F32, BF16 = jnp.float32, jnp.bfloat16
MESH = pl.DeviceIdType.MESH

D_MODEL = 1024
HEAD_DIM = 64
N_HEADS = 16
D_CONV = 1536
D_FF = 2816
D_IN_PROJ = 5648
D_PROJ_PAD = 5760
CHUNK = 128
SB_TILE = 256
SB_DEAD = -105.0
EPS = 1e-6
N_SHARDS = 4
PACK_ROWS = 4096
PACK_HALF = PACK_ROWS // 2
R_IN, R_OUT, R_FF = 1424, 512, 704
SMALL_ROWS = 24

ADAM_LR, ADAM_B1, ADAM_B2, ADAM_EPS, ADAM_WD, ADAM_STEP = 0.001, 0.9, 0.999, 1e-08, 0.01, 10

VMEM_LIMIT = 48 * 1024 * 1024

_DN = {"nn": (((1,), (0,)), ((), ())), "nt": (((1,), (1,)), ((), ())), "tn": (((0,), (0,)), ((), ()))}


def _dot(a, b, dims="nn"):
    return lax.dot_general(a, b, _DN[dims], preferred_element_type=F32)


def _split(x, n):
    out = []
    for _ in range(n - 1):
        p = x.astype(BF16)
        out.append(p)
        x = x - p.astype(F32)
    out.append(x.astype(BF16))
    return out


def _dotx_r(x, b_exact, dims="nn", n=3):
    acc = None
    for p in reversed(_split(x, n)):
        t = _dot(p, b_exact, dims)
        acc = t if acc is None else acc + t
    return acc


def _dotx_l(a_exact, x, dims="nn", n=3):
    acc = None
    for p in reversed(_split(x, n)):
        t = _dot(a_exact, p, dims)
        acc = t if acc is None else acc + t
    return acc


def _dot2(x, b2):
    hi = lax.bitcast_convert_type(lax.bitcast_convert_type(x, jnp.int32) & jnp.int32(-65536), F32)
    return _dot(jnp.concatenate([hi.astype(BF16), (x - hi).astype(BF16)], axis=1), b2)


def _sig(x):
    return 1.0 / (1.0 + jnp.exp(-x))


def _softplus(x):
    return jnp.maximum(x, 0.0) + jnp.log(1.0 + jnp.exp(-jnp.abs(x)))


def _cp(sem=None, vmem=VMEM_LIMIT):
    return pltpu.CompilerParams(dimension_semantics=sem, vmem_limit_bytes=vmem)


def _colsum(x):
    return jnp.sum(x, axis=0, keepdims=True)


def _consts():
    ch = np.arange(D_MODEL)
    expand = (np.arange(128)[:, None] == (ch // HEAD_DIM)[None, :]).astype(np.float32)
    fold = (ch[:, None] % HEAD_DIM == np.arange(128)[None, :]).astype(np.float32)
    i = np.arange(CHUNK)
    tril = (i[:, None] >= i[None, :]).astype(np.float32)
    j = np.arange(SB_TILE)
    ustrict = (j[:, None] > j[None, :]).astype(np.float32)
    ule = (j[:, None] <= j[None, :]).astype(np.float32)
    ult = (j[:, None] < j[None, :]).astype(np.float32)
    c = lambda a: jnp.asarray(a, BF16)
    return dict(expand=c(expand), hsum=c(expand.T), fold=c(fold), tril=c(tril), triu=c(tril.T),
                ustrict=ustrict, ule=ule, ult=ult)


def _doubled(tri, tk):
    b = tri[:tk, :tk]
    return jnp.asarray(np.concatenate([b, b], axis=0), BF16)


def _pick(n, cap):
    best = 128
    for t in range(128, min(n, cap) + 1, 128):
        if n % t == 0:
            best = t
    return n if n <= cap else best


def _matmul(a, b, dims, out_dtype, name, tm_cap=1024, tn_cap=2048, tk_cap=1024):
    if dims == "nn":
        (m, k), (_, n) = a.shape, b.shape
    elif dims == "nt":
        (m, k), (n, _) = a.shape, b.shape
    else:
        (k, m), (_, n) = a.shape, b.shape
    tm, tn, tk = _pick(m, tm_cap), _pick(n, tn_cap), _pick(k, tk_cap)
    nk = k // tk
    a_spec = (pl.BlockSpec((tk, tm), lambda i, j, kk: (kk, i)) if dims == "tn"
              else pl.BlockSpec((tm, tk), lambda i, j, kk: (i, kk)))
    b_spec = (pl.BlockSpec((tn, tk), lambda i, j, kk: (j, kk)) if dims == "nt"
              else pl.BlockSpec((tk, tn), lambda i, j, kk: (kk, j)))

    def body(a_ref, b_ref, o_ref, acc_ref):
        kk = pl.program_id(2)
        part = _dot(a_ref[...], b_ref[...], dims)
        if nk == 1:
            o_ref[...] = part.astype(out_dtype)
        else:
            @pl.when(kk == 0)
            def _():
                acc_ref[...] = part

            @pl.when(kk > 0)
            def _():
                acc_ref[...] += part

            @pl.when(kk == nk - 1)
            def _():
                o_ref[...] = acc_ref[...].astype(out_dtype)

    return pl.pallas_call(
        body, name=name, grid=(m // tm, n // tn, nk),
        in_specs=[a_spec, b_spec],
        out_specs=pl.BlockSpec((tm, tn), lambda i, j, kk: (i, j)),
        out_shape=jax.ShapeDtypeStruct((m, n), out_dtype),
        scratch_shapes=[pltpu.VMEM((tm, tn) if nk > 1 else (8, 128), F32)],
        compiler_params=_cp(("parallel", "parallel", "arbitrary")),
    )(a, b)


def _row_spec(tm, width=D_MODEL, col=0):
    return pl.BlockSpec((tm, width), lambda i: (i, col))


def _fix_spec(shape):
    return pl.BlockSpec(shape, lambda *_: (0,) * len(shape))


def _norm_mod(x, nw, mod, row_sh, name):
    t = x.shape[0]
    tm = min(t, 512)

    def body(x_ref, nw_ref, mod_ref, h_ref):
        xv = x_ref[...]
        r = lax.rsqrt(jnp.mean(xv * xv, axis=-1, keepdims=True) + EPS)
        sh = mod_ref[row_sh:row_sh + 1, :]
        sc = mod_ref[row_sh + 1:row_sh + 2, :]
        h_ref[...] = (xv * r * nw_ref[...] * (1.0 + sc) + sh).astype(BF16)

    return pl.pallas_call(
        body, name=name, grid=(t // tm,),
        in_specs=[_row_spec(tm), _fix_spec((1, D_MODEL)), _fix_spec((8, D_MODEL))],
        out_specs=_row_spec(tm), out_shape=jax.ShapeDtypeStruct((t, D_MODEL), BF16),
        compiler_params=_cp(("parallel",)),
    )(x, nw, mod)


def _resid_norm(x, mix, nw, mod, name):
    t = x.shape[0]
    tm = min(t, 512)

    def body(x_ref, mix_ref, nw_ref, mod_ref, x1_ref, h_ref):
        x1 = x_ref[...] + mod_ref[2:3, :] * mix_ref[...]
        x1_ref[...] = x1
        r = lax.rsqrt(jnp.mean(x1 * x1, axis=-1, keepdims=True) + EPS)
        h_ref[...] = (x1 * r * nw_ref[...] * (1.0 + mod_ref[4:5, :]) + mod_ref[3:4, :]).astype(BF16)

    return pl.pallas_call(
        body, name=name, grid=(t // tm,),
        in_specs=[_row_spec(tm), _row_spec(tm), _fix_spec((1, D_MODEL)), _fix_spec((8, D_MODEL))],
        out_specs=[_row_spec(tm), _row_spec(tm)],
        out_shape=[jax.ShapeDtypeStruct((t, D_MODEL), F32), jax.ShapeDtypeStruct((t, D_MODEL), BF16)],
        compiler_params=_cp(("parallel",)),
    )(x, mix, nw, mod)


def _act_fwd(gu, name):
    t = gu.shape[0]
    tm, tn = min(t, 512), D_FF // 2
    nb = D_FF // tn

    def body(g_ref, u_ref, a_ref):
        g = g_ref[...]
        a_ref[...] = (g * _sig(g) * u_ref[...]).astype(BF16)

    return pl.pallas_call(
        body, name=name, grid=(t // tm, nb),
        in_specs=[pl.BlockSpec((tm, tn), lambda i, j: (i, j)), pl.BlockSpec((tm, tn), lambda i, j: (i, j + nb))],
        out_specs=pl.BlockSpec((tm, tn), lambda i, j: (i, j)),
        out_shape=jax.ShapeDtypeStruct((t, D_FF), BF16),
        compiler_params=_cp(("parallel", "parallel")),
    )(gu, gu)


def _act_bwd(dact, gu, name):
    t = gu.shape[0]
    tm = min(t, 256)

    def body(d_ref, g_ref, u_ref, o_ref):
        g, d = g_ref[...], d_ref[...]
        s = _sig(g)
        o_ref[:, 0:D_FF] = (d * u_ref[...] * s * (1.0 + g * (1.0 - s))).astype(BF16)
        o_ref[:, D_FF:2 * D_FF] = (d * g * s).astype(BF16)

    return pl.pallas_call(
        body, name=name, grid=(t // tm,),
        in_specs=[pl.BlockSpec((tm, D_FF), lambda i: (i, 0)), pl.BlockSpec((tm, D_FF), lambda i: (i, 0)),
                  pl.BlockSpec((tm, D_FF), lambda i: (i, 1))],
        out_specs=pl.BlockSpec((tm, 2 * D_FF), lambda i: (i, 0)),
        out_shape=jax.ShapeDtypeStruct((t, 2 * D_FF), BF16),
        compiler_params=_cp(("parallel",)),
    )(dact, gu, gu)


def _loss_head(x1, ffn, tgt, mod, name):
    t = x1.shape[0]
    tm = min(t, 512)

    def body(x1_ref, f_ref, t_ref, mod_ref, dffn_ref, dout_ref, dg2_ref, loss_ref):
        i = pl.program_id(0)
        g2 = mod_ref[5:6, :]
        f = f_ref[...]
        err = x1_ref[...] + g2 * f - t_ref[...]
        dout = err * (1.0 / D_MODEL)
        dout_ref[...] = dout
        dffn_ref[...] = (dout * g2).astype(BF16)
        part = jnp.zeros((8, 128), F32) + 0.5 * jnp.sum(jnp.mean(err * err, axis=-1, keepdims=True))

        @pl.when(i == 0)
        def _():
            dg2_ref[...] = _colsum(dout * f)
            loss_ref[...] = part

        @pl.when(i > 0)
        def _():
            dg2_ref[...] += _colsum(dout * f)
            loss_ref[...] += part

    return pl.pallas_call(
        body, name=name, grid=(t // tm,),
        in_specs=[_row_spec(tm), _row_spec(tm), _row_spec(tm), _fix_spec((8, D_MODEL))],
        out_specs=[_row_spec(tm), _row_spec(tm), _fix_spec((1, D_MODEL)), _fix_spec((8, 128))],
        out_shape=[jax.ShapeDtypeStruct((t, D_MODEL), BF16), jax.ShapeDtypeStruct((t, D_MODEL), F32),
                   jax.ShapeDtypeStruct((1, D_MODEL), F32), jax.ShapeDtypeStruct((8, 128), F32)],
        compiler_params=_cp(("arbitrary",)),
    )(x1, ffn, tgt, mod)


def _norm_bwd(dh, xin, dres, aux, nw, mod, row_sh, gate_row, name):
    t = xin.shape[0]
    tm = min(t, 512)
    with_gate = gate_row is not None

    def body(*refs):
        if with_gate:
            dh_ref, x_ref, dr_ref, aux_ref, nw_ref, mod_ref, dx_ref, dg_ref, acc_ref = refs
        else:
            dh_ref, x_ref, dr_ref, nw_ref, mod_ref, dx_ref, acc_ref = refs
        i = pl.program_id(0)
        xv, dhv = x_ref[...], dh_ref[...]
        r = lax.rsqrt(jnp.mean(xv * xv, axis=-1, keepdims=True) + EPS)
        xn = xv * r
        nwv = nw_ref[...]
        sc1 = 1.0 + mod_ref[row_sh + 1:row_sh + 2, :]
        dxn = dhv * (nwv * sc1)
        dx = dr_ref[...] + r * (dxn - xn * jnp.mean(dxn * xn, axis=-1, keepdims=True))
        dx_ref[...] = dx
        dhx = dhv * xn
        rows = [_colsum(dhv), _colsum(dhx * nwv), _colsum(dhx * sc1)]
        if with_gate:
            dg_ref[...] = (dx * mod_ref[gate_row:gate_row + 1, :]).astype(BF16)
            rows.append(_colsum(dx * aux_ref[...]))

        @pl.when(i == 0)
        def _():
            acc_ref[...] = jnp.zeros_like(acc_ref)

        for k, v in enumerate(rows):
            acc_ref[k:k + 1, :] += v

    ins = [dh, xin, dres] + ([aux] if with_gate else []) + [nw, mod]
    in_specs = [_row_spec(tm)] * (4 if with_gate else 3) + [_fix_spec((1, D_MODEL)), _fix_spec((8, D_MODEL))]
    out_specs = [_row_spec(tm)] + ([_row_spec(tm)] if with_gate else []) + [_fix_spec((8, D_MODEL))]
    out_shape = ([jax.ShapeDtypeStruct((t, D_MODEL), F32)]
                 + ([jax.ShapeDtypeStruct((t, D_MODEL), BF16)] if with_gate else [])
                 + [jax.ShapeDtypeStruct((8, D_MODEL), F32)])
    return pl.pallas_call(
        body, name=name, grid=(t // tm,), in_specs=in_specs, out_specs=out_specs, out_shape=out_shape,
        compiler_params=_cp(("arbitrary",)),
    )(*ins)


XBC_COL0 = 4096 // 128
DT_COL = 5632 // 128


def _conv_pre(xv, w_ref, b_ref):
    t = xv.shape[0]
    row = lax.broadcasted_iota(jnp.int32, xv.shape, 0)
    pre = xv * w_ref[3:4, :] + b_ref[...]
    shifted = []
    for k in range(3):
        s = 3 - k
        xs = jnp.where(row >= s, pltpu.roll(xv, s, 0), 0.0)
        shifted.append(xs)
        pre = pre + xs * w_ref[k:k + 1, :]
    return pre, shifted, row, t


def _conv_fwd(proj, conv_w, conv_b, name):
    t = proj.shape[0]

    def body(x_ref, w_ref, b_ref, u_ref):
        pre, _, _, _ = _conv_pre(x_ref[...], w_ref, b_ref)
        u_ref[...] = pre * _sig(pre)

    return pl.pallas_call(
        body, name=name, grid=(D_CONV // 128,),
        in_specs=[pl.BlockSpec((t, 128), lambda j: (0, XBC_COL0 + j)), pl.BlockSpec((4, 128), lambda j: (0, j)),
                  pl.BlockSpec((1, 128), lambda j: (0, j))],
        out_specs=pl.BlockSpec((t, 128), lambda j: (0, j)),
        out_shape=jax.ShapeDtypeStruct((t, D_CONV), F32),
        compiler_params=_cp(("parallel",)),
    )(proj, conv_w, conv_b)


def _conv_bwd(proj, du, conv_w, conv_b, name):
    t = proj.shape[0]

    def body(x_ref, du_ref, w_ref, b_ref, dx_ref, dw_ref, db_ref):
        pre, shifted, row, _ = _conv_pre(x_ref[...], w_ref, b_ref)
        s = _sig(pre)
        dpre = du_ref[...] * s * (1.0 + pre * (1.0 - s))
        db_ref[...] = _colsum(dpre)
        dx = dpre * w_ref[3:4, :]
        dw_ref[3:4, :] = _colsum(dpre * x_ref[...])
        for k in range(3):
            sft = 3 - k
            dw_ref[k:k + 1, :] = _colsum(dpre * shifted[k])
            back = jnp.where(row < t - sft, pltpu.roll(dpre, t - sft, 0), 0.0)
            dx = dx + back * w_ref[k:k + 1, :]
        dx_ref[...] = dx.astype(BF16)

    return pl.pallas_call(
        body, name=name, grid=(D_CONV // 128,),
        in_specs=[pl.BlockSpec((t, 128), lambda j: (0, XBC_COL0 + j)), pl.BlockSpec((t, 128), lambda j: (0, j)),
                  pl.BlockSpec((4, 128), lambda j: (0, j)), pl.BlockSpec((1, 128), lambda j: (0, j))],
        out_specs=[pl.BlockSpec((t, 128), lambda j: (0, j)), pl.BlockSpec((4, 128), lambda j: (0, j)),
                   pl.BlockSpec((1, 128), lambda j: (0, j))],
        out_shape=[jax.ShapeDtypeStruct((t, D_CONV), BF16), jax.ShapeDtypeStruct((4, D_CONV), F32),
                   jax.ShapeDtypeStruct((1, D_CONV), F32)],
        compiler_params=_cp(("parallel",)),
    )(proj, du, conv_w, conv_b)


def _ssd_common(dtraw_ref, dtb_ref, alog_ref, tril, expand, l_s, lt_s):
    lane = lax.broadcasted_iota(jnp.int32, (1, 128), 1)
    dt = _softplus(dtraw_ref[...] + dtb_ref[...])
    a = jnp.where(lane < N_HEADS, -jnp.exp(alog_ref[...]), 0.0)
    lcs = _dotx_l(tril, dt * a)
    l_s[...] = lcs
    lt_s[...] = lcs.T
    llast = l_s[CHUNK - 1:CHUNK, :]
    ea = _dotx_r(jnp.exp(lcs), expand)
    ds = _dotx_r(jnp.exp(llast - lcs), expand)
    dtx = _dotx_r(dt, expand)
    return dt, a, lcs, llast, ea, ds, dtx


def _head_col(lcs, h):
    lane = lax.broadcasted_iota(jnp.int32, lcs.shape, 1)
    return jnp.sum(jnp.where(lane == h, lcs, 0.0), axis=1, keepdims=True)


def _decay(lcs, lt_s, h, causal):
    seg = _head_col(lcs, h) - lt_s[h:h + 1, :]
    return jnp.exp(jnp.where(causal, seg, -1e30))


def _ssd_fwd(u, proj, dtb, alog, dsk, nw, cst, name):
    t = u.shape[0]
    nc = t // CHUNK

    def body(xs_ref, b_ref, c_ref, dtraw_ref, z_ref, dtb_ref, alog_ref, dsk_ref, nw_ref, tril_ref, exp_ref,
             y_ref, yn_ref, prev_ref, carry, l_s, lt_s, yd_s):
        i = pl.program_id(0)

        @pl.when(i == 0)
        def _():
            carry[...] = jnp.zeros_like(carry)

        expand = exp_ref[...]
        dt, a, lcs, llast, ea, ds, dtx = _ssd_common(dtraw_ref, dtb_ref, alog_ref, tril_ref[...], expand, l_s, lt_s)
        xs = xs_ref[...]
        xg = xs * dtx
        xgb = xg.astype(BF16)
        xgd = (xg * ds).astype(BF16)
        prev = carry[...]
        prev_ref[0] = prev
        prevb = prev.astype(BF16)
        ri = lax.broadcasted_iota(jnp.int32, (CHUNK, CHUNK), 0)
        ci = lax.broadcasted_iota(jnp.int32, (CHUNK, CHUNK), 1)
        causal = ri >= ci
        lane = lax.broadcasted_iota(jnp.int32, (1, 128), 1)
        new_states, yoff = [], []
        for g in range(2):
            bg = b_ref[:, g * 128:(g + 1) * 128].astype(BF16)
            cg = c_ref[:, g * 128:(g + 1) * 128].astype(BF16)
            sc = _dot(cg, bg, "nt")
            gs = slice(g * 512, (g + 1) * 512)
            new_states.append(_dot(bg, xgd[:, gs], "tn"))
            yoff.append(_dot(cg, prevb[:, gs]))
            for pr in range(4):
                col = g * 512 + pr * 128
                xp = xgb[:, col:col + 128]
                acc = jnp.zeros((CHUNK, 128), F32)
                for half in range(2):
                    h = g * 8 + pr * 2 + half
                    m = (sc * _decay(lcs, lt_s, h, causal)).astype(BF16)
                    keep = (lane < HEAD_DIM) if half == 0 else (lane >= HEAD_DIM)
                    acc = acc + _dot(m, jnp.where(keep, xp, jnp.zeros_like(xp)))
                yd_s[:, col:col + 128] = acc
        y = yd_s[...] + jnp.concatenate(yoff, axis=1) * ea + xs * dsk_ref[...]
        y_ref[...] = y
        carry[...] = prev * jnp.max(_dotx_r(jnp.exp(llast) + jnp.zeros((8, 128), F32), expand), axis=0, keepdims=True) \
            + jnp.concatenate(new_states, axis=1)
        z = z_ref[...]
        yz = y * (z * _sig(z))
        nwv = nw_ref[...]
        for g in range(2):
            gs = slice(g * 512, (g + 1) * 512)
            v = yz[:, gs]
            r = lax.rsqrt(jnp.mean(v * v, axis=-1, keepdims=True) + EPS)
            yn_ref[:, gs] = (v * r * nwv[:, gs]).astype(BF16)

    row = lambda w, col: pl.BlockSpec((CHUNK, w), lambda i: (i, col))
    return pl.pallas_call(
        body, name=name, grid=(nc,),
        in_specs=[row(1024, 0), row(256, 4), row(256, 5), row(128, DT_COL), row(1024, 0),
                  _fix_spec((1, 128)), _fix_spec((1, 128)), _fix_spec((1, D_MODEL)), _fix_spec((1, D_MODEL)),
                  _fix_spec((CHUNK, CHUNK)), _fix_spec((128, D_MODEL))],
        out_specs=[row(1024, 0), row(1024, 0), pl.BlockSpec((1, 128, D_MODEL), lambda i: (i, 0, 0))],
        out_shape=[jax.ShapeDtypeStruct((t, D_MODEL), F32), jax.ShapeDtypeStruct((t, D_MODEL), BF16),
                   jax.ShapeDtypeStruct((nc, 128, D_MODEL), F32)],
        scratch_shapes=[pltpu.VMEM((128, D_MODEL), F32), pltpu.VMEM((128, 128), F32), pltpu.VMEM((128, 128), F32),
                        pltpu.VMEM((CHUNK, D_MODEL), F32)],
        compiler_params=_cp(("arbitrary",)),
    )(u, u, u, proj, proj, dtb, alog, dsk, nw, cst["tril"], cst["expand"])


def _ssd_bwd(u, proj, y, prev, dycat, dtb, alog, dsk, nw, cst, name):
    t = u.shape[0]
    nc = t // CHUNK

    def body(xs_ref, b_ref, c_ref, dtraw_ref, z_ref, y_ref, prev_ref, dyn_ref, dtb_ref, alog_ref, dsk_ref, nw_ref,
             tril_ref, triu_ref, exp_ref, hs_ref,
             du_ref, ddt_ref, dz_ref, acc_ref, acc16_ref, dcarry, l_s, lt_s, dxg_s):
        i = pl.program_id(0)

        @pl.when(i == 0)
        def _():
            dcarry[...] = jnp.zeros_like(dcarry)
            acc_ref[...] = jnp.zeros_like(acc_ref)
            acc16_ref[...] = jnp.zeros_like(acc16_ref)

        expand, hsum = exp_ref[...], hs_ref[...]
        dt, a, lcs, llast, ea, ds, dtx = _ssd_common(dtraw_ref, dtb_ref, alog_ref, tril_ref[...], expand, l_s, lt_s)
        xs = xs_ref[...]
        xg = xs * dtx
        xgb = xg.astype(BF16)
        xgdf = xg * ds
        xgd = xgdf.astype(BF16)
        dsk_v, nwv = dsk_ref[...], nw_ref[...]
        z, y = z_ref[...], y_ref[...]
        sz = _sig(z)
        silz = z * sz
        yz = y * silz
        dyn = dyn_ref[...]
        dyz_parts, dnw_parts = [], []
        for g in range(2):
            gs = slice(g * 512, (g + 1) * 512)
            v = yz[:, gs]
            r = lax.rsqrt(jnp.mean(v * v, axis=-1, keepdims=True) + EPS)
            yhat = v * r
            dnw_parts.append(_colsum(dyn[:, gs] * yhat))
            dw = dyn[:, gs] * nwv[:, gs]
            dyz_parts.append(r * (dw - yhat * jnp.mean(dw * yhat, axis=-1, keepdims=True)))
        dyz = jnp.concatenate(dyz_parts, axis=1)
        dy = dyz * silz
        dz_ref[...] = (dyz * y * (sz * (1.0 + z * (1.0 - sz)))).astype(BF16)
        acc_ref[0:1, :] += jnp.concatenate(dnw_parts, axis=1)
        acc_ref[1:2, :] += _colsum(dy * xs)
        dyb = dy.astype(BF16)
        dq = (dy * ea).astype(BF16)
        dcar = dcarry[...]
        dcarb = dcar.astype(BF16)
        prev = prev_ref[0]
        prevb = prev.astype(BF16)
        ri = lax.broadcasted_iota(jnp.int32, (CHUNK, CHUNK), 0)
        ci = lax.broadcasted_iota(jnp.int32, (CHUNK, CHUNK), 1)
        causal = ri >= ci
        lane = lax.broadcasted_iota(jnp.int32, (1, 128), 1)
        dprev, dxgd, yoff = [], [], []
        dl_l = jnp.zeros((CHUNK, CHUNK), F32)
        dl_s = jnp.zeros((CHUNK, CHUNK), F32)
        for g in range(2):
            gs = slice(g * 512, (g + 1) * 512)
            bg = b_ref[:, g * 128:(g + 1) * 128].astype(BF16)
            cg = c_ref[:, g * 128:(g + 1) * 128].astype(BF16)
            sc = _dot(cg, bg, "nt")
            yoff.append(_dot(cg, prevb[:, gs]))
            dcg = _dot(dq[:, gs], prevb[:, gs], "nt")
            dprev.append(_dot(cg, dq[:, gs], "tn"))
            dbg = _dot(xgd[:, gs], dcarb[:, gs], "nt")
            dxgd.append(_dot(bg, dcarb[:, gs]))
            dsc = jnp.zeros((CHUNK, CHUNK), F32)
            for pr in range(4):
                col = g * 512 + pr * 128
                xp = xgb[:, col:col + 128]
                dyp = dyb[:, col:col + 128]
                acc = jnp.zeros((CHUNK, 128), F32)
                for half in range(2):
                    h = g * 8 + pr * 2 + half
                    dec = _decay(lcs, lt_s, h, causal)
                    mf = sc * dec
                    keep = (lane < HEAD_DIM) if half == 0 else (lane >= HEAD_DIM)
                    dyh = jnp.where(keep, dyp, jnp.zeros_like(dyp))
                    dm = _dot(dyh, xp, "nt")
                    acc = acc + _dot(mf.astype(BF16), dyh, "tn")
                    dsc = dsc + dm * dec
                    gm = dm * mf
                    dl_l = dl_l + jnp.where(ci == h, jnp.sum(gm, axis=1, keepdims=True), 0.0)
                    dl_s = dl_s + jnp.where(ri == h, jnp.sum(gm, axis=0, keepdims=True), 0.0)
                dxg_s[:, col:col + 128] = acc
            dscb = dsc.astype(BF16)
            dcg = dcg + _dot(dscb, bg)
            dbg = dbg + _dot(dscb, cg, "tn")
            du_ref[:, 1024 + g * 128:1024 + (g + 1) * 128] = dbg
            du_ref[:, 1280 + g * 128:1280 + (g + 1) * 128] = dcg
        dxgd = jnp.concatenate(dxgd, axis=1)
        dxg = dxg_s[...] + dxgd * ds
        du_ref[:, 0:1024] = dy * dsk_v + dxg * dtx
        hs1 = _dotx_r(dxg * xs, hsum)
        yoff = jnp.concatenate(yoff, axis=1) * ea
        dl = dl_l - dl_s.T + _dotx_r(dy * yoff - xgdf * dxgd, hsum)
        rows8 = lax.broadcasted_iota(jnp.int32, (8, D_MODEL), 0)
        two = jnp.where(rows8 == 0, _colsum(dxgd * xgdf), jnp.where(rows8 == 1, _colsum(dcar * prev), 0.0))
        two = _dotx_r(two, hsum)
        r8 = lax.broadcasted_iota(jnp.int32, (8, 128), 0)
        dllast = _colsum(jnp.where(r8 == 0, two, 0.0)) + _colsum(jnp.where(r8 == 1, two, 0.0)) * jnp.exp(llast)
        rowi = lax.broadcasted_iota(jnp.int32, (CHUNK, 128), 0)
        dl = dl + jnp.where(rowi == CHUNK - 1, dllast, 0.0)
        dadt = _dotx_l(triu_ref[...], dl)
        ddt = dadt * a + hs1
        draw = ddt * _sig(dtraw_ref[...] + dtb_ref[...])
        ddt_ref[...] = draw.astype(BF16)
        acc16_ref[0:1, :] += _colsum(draw)
        acc16_ref[1:2, :] += _colsum(dadt * dt) * a
        dcarry[...] = dcar * jnp.max(_dotx_r(jnp.exp(llast) + jnp.zeros((8, 128), F32), expand), axis=0, keepdims=True) \
            + jnp.concatenate(dprev, axis=1)

        @pl.when(i == nc - 1)
        def _():
            hd = _dotx_r(acc_ref[...], hsum)
            acc16_ref[2:3, :] = _colsum(jnp.where(lax.broadcasted_iota(jnp.int32, (8, 128), 0) == 1, hd, 0.0))

    rev = lambda w, col: pl.BlockSpec((CHUNK, w), lambda i: (nc - 1 - i, col))
    return pl.pallas_call(
        body, name=name, grid=(nc,),
        in_specs=[rev(1024, 0), rev(256, 4), rev(256, 5), rev(128, DT_COL), rev(1024, 0), rev(1024, 0),
                  pl.BlockSpec((1, 128, D_MODEL), lambda i: (nc - 1 - i, 0, 0)), rev(1024, 0),
                  _fix_spec((1, 128)), _fix_spec((1, 128)), _fix_spec((1, D_MODEL)), _fix_spec((1, D_MODEL)),
                  _fix_spec((CHUNK, CHUNK)), _fix_spec((CHUNK, CHUNK)), _fix_spec((128, D_MODEL)),
                  _fix_spec((D_MODEL, 128))],
        out_specs=[rev(D_CONV, 0), rev(128, 0), rev(1024, 0), _fix_spec((8, D_MODEL)), _fix_spec((8, 128))],
        out_shape=[jax.ShapeDtypeStruct((t, D_CONV), F32), jax.ShapeDtypeStruct((t, 128), BF16),
                   jax.ShapeDtypeStruct((t, D_MODEL), BF16), jax.ShapeDtypeStruct((8, D_MODEL), F32),
                   jax.ShapeDtypeStruct((8, 128), F32)],
        scratch_shapes=[pltpu.VMEM((128, D_MODEL), F32), pltpu.VMEM((128, 128), F32), pltpu.VMEM((128, 128), F32),
                        pltpu.VMEM((CHUNK, D_MODEL), F32)],
        compiler_params=_cp(("arbitrary",)),
    )(u, u, u, proj, proj, y, prev, dycat, dtb, alog, dsk, nw,
      cst["tril"], cst["triu"], cst["expand"], cst["hsum"])


def _head_rms(v, hsum, expand):
    ms = _dotx_r(v * v, hsum) * (1.0 / HEAD_DIM)
    return _dotx_r(lax.rsqrt(ms + EPS), expand)


def _qk_fwd(proj, qw, kw, cst, name):
    t = proj.shape[0]
    tm = min(t, 256)
    scale = HEAD_DIM ** -0.5

    def body(q_ref, k_ref, v_ref, qw_ref, kw_ref, hs_ref, exp_ref, qs_ref, kn_ref, vb_ref):
        hsum, expand = hs_ref[...], exp_ref[...]
        q, k = q_ref[...], k_ref[...]
        qs_ref[...] = (q * _head_rms(q, hsum, expand) * qw_ref[...] * scale).astype(BF16)
        kn_ref[...] = (k * _head_rms(k, hsum, expand) * kw_ref[...]).astype(BF16)
        vb_ref[...] = v_ref[...].astype(BF16)

    return pl.pallas_call(
        body, name=name, grid=(t // tm,),
        in_specs=[_row_spec(tm, col=1), _row_spec(tm, col=2), _row_spec(tm, col=3),
                  _fix_spec((1, D_MODEL)), _fix_spec((1, D_MODEL)), _fix_spec((D_MODEL, 128)),
                  _fix_spec((128, D_MODEL))],
        out_specs=[_row_spec(tm)] * 3, out_shape=[jax.ShapeDtypeStruct((t, D_MODEL), BF16)] * 3,
        compiler_params=_cp(("parallel",)),
    )(proj, proj, proj, qw, kw, cst["hsum"], cst["expand"])


def _qk_bwd(proj, dqs, dkn, dv, qw, kw, cst, name):
    t = proj.shape[0]
    tm = min(t, 256)
    scale = HEAD_DIM ** -0.5

    def body(q_ref, k_ref, dq_ref, dk_ref, dv_ref, qw_ref, kw_ref, hs_ref, exp_ref, fold_ref,
             oq_ref, ok_ref, ov_ref, dw_ref):
        i = pl.program_id(0)
        hsum, expand = hs_ref[...], exp_ref[...]
        rows8 = lax.broadcasted_iota(jnp.int32, (8, D_MODEL), 0)
        sums = jnp.zeros((8, D_MODEL), F32)
        for n, (x_ref, d_ref, w_ref, o_ref, sc) in enumerate(
                [(q_ref, dq_ref, qw_ref, oq_ref, scale), (k_ref, dk_ref, kw_ref, ok_ref, 1.0)]):
            xv = x_ref[...]
            r = _head_rms(xv, hsum, expand)
            xhat = xv * r
            dn = d_ref[...] * sc
            sums = sums + jnp.where(rows8 == n, _colsum(dn * xhat), 0.0)
            dw = dn * w_ref[...]
            mean = _dotx_r(_dotx_r(dw * xhat, hsum), expand) * (1.0 / HEAD_DIM)
            o_ref[...] = (r * (dw - xhat * mean)).astype(BF16)
        ov_ref[...] = dv_ref[...].astype(BF16)
        folded = _dotx_r(sums, fold_ref[...])

        @pl.when(i == 0)
        def _():
            dw_ref[...] = folded

        @pl.when(i > 0)
        def _():
            dw_ref[...] += folded

    return pl.pallas_call(
        body, name=name, grid=(t // tm,),
        in_specs=[_row_spec(tm, col=1), _row_spec(tm, col=2), _row_spec(tm), _row_spec(tm), _row_spec(tm),
                  _fix_spec((1, D_MODEL)), _fix_spec((1, D_MODEL)), _fix_spec((D_MODEL, 128)),
                  _fix_spec((128, D_MODEL)), _fix_spec((D_MODEL, 128))],
        out_specs=[_row_spec(tm)] * 3 + [_fix_spec((8, 128))],
        out_shape=[jax.ShapeDtypeStruct((t, D_MODEL), BF16)] * 3 + [jax.ShapeDtypeStruct((8, 128), F32)],
        compiler_params=_cp(("arbitrary",)),
    )(proj, proj, dqs, dkn, dv, qw, kw, cst["hsum"], cst["expand"], cst["fold"])


def _sb_masks(i, kb, tq, tk):
    tpos = i * tq + lax.broadcasted_iota(jnp.int32, (tq, 1), 0)
    spos = kb * tk + lax.broadcasted_iota(jnp.int32, (1, tk), 1)
    return spos < tpos


def _sb_fwd(qs, kn, vb, cst, name):
    t = qs.shape[0]
    tq = tk = min(t, SB_TILE)
    nq = t // tq

    def body(q_ref, k_ref, v_ref, u_ref, rt_ref, ob_ref, cnt_ref, acc, rs):
        i = pl.program_id(1)
        lane = lax.broadcasted_iota(jnp.int32, (1, 128), 1)
        q2 = q_ref[...]
        zero = jnp.zeros_like(q2)
        qh = [jnp.where(lane < HEAD_DIM, q2, zero), jnp.where(lane >= HEAD_DIM, q2, zero)]
        acc[...] = jnp.zeros_like(acc)
        rs[...] = jnp.zeros_like(rs)
        ustrict = u_ref[...]

        def tile(kb, masked):
            off = pl.multiple_of(kb * tk, tk)
            k2 = k_ref[pl.ds(off, tk), :]
            v2 = v_ref[pl.ds(off, tk), :]
            strict = _sb_masks(i, kb, tq, tk) if masked else None
            s = [_dot(qh[h], k2, "nt") for h in range(2)]
            a, r, lb = [None] * 2, [None] * 2, [None] * 2
            for h in range(2):
                sp = _softplus(s[h])
                a[h] = s[h] - sp
                r[h] = jnp.where(strict, -sp, 0.0) if masked else -sp
                lb[h] = _dot2(r[h], ustrict)
            for h in range(2):
                lw = a[h] + lb[h] + rs[h]
                w = jnp.exp(jnp.where(strict, lw, -1e30) if masked else lw)
                rs[h] = rs[h] + jnp.sum(r[h], axis=1, keepdims=True)
                acc[h] = acc[h] + _dot(w.astype(BF16), v2)

        tile(i, True)

        def live():
            return jnp.max(jnp.maximum(rs[0], rs[1]))

        def more(c):
            return jnp.logical_and(c[0] < i, c[1] > SB_DEAD)

        def step(c):
            tile(i - 1 - c[0], False)
            return c[0] + 1, live()

        n_off, _ = lax.while_loop(more, step, (jnp.int32(0), live()))
        cnt_ref[pl.program_id(0), i] = n_off.astype(F32)
        rt_ref[...] = jnp.where(lane < HEAD_DIM, rs[0], rs[1])
        ob_ref[...] = jnp.where(lane < HEAD_DIM, acc[0], acc[1]).astype(BF16)

    return pl.pallas_call(
        body, name=name, grid=(D_MODEL // 128, nq),
        in_specs=[pl.BlockSpec((tq, 128), lambda j, i: (i, j)), pl.BlockSpec((t, 128), lambda j, i: (0, j)),
                  pl.BlockSpec((t, 128), lambda j, i: (0, j)), _fix_spec((2 * tk, tk))],
        out_specs=[pl.BlockSpec((tq, 128), lambda j, i: (i, j))] * 2 + [pl.BlockSpec(memory_space=pltpu.SMEM)],
        out_shape=[jax.ShapeDtypeStruct((t, D_MODEL), F32), jax.ShapeDtypeStruct((t, D_MODEL), BF16),
                   jax.ShapeDtypeStruct((D_MODEL // 128, nq), F32)],
        scratch_shapes=[pltpu.VMEM((2, tq, 128), F32), pltpu.VMEM((2, tq, 1), F32)],
        compiler_params=_cp(("arbitrary", "arbitrary")),
    )(qs, kn, vb, _doubled(cst["ustrict"], tk))


def _sb_bwd(qs, kn, vb, rtot, cnt, dycat, cst, name):
    t = qs.shape[0]
    tq = tk = min(t, SB_TILE)
    nq = t // tq

    def body(q_ref, k_ref, v_ref, rt_ref, do_ref, us_ref, ui_ref, cnt_ref, dq_ref, dk_ref, dv_ref, acc, rs, es):
        i = pl.program_id(1)
        lane = lax.broadcasted_iota(jnp.int32, (1, 128), 1)
        keep = [lane < HEAD_DIM, lane >= HEAD_DIM]
        q2, rt = q_ref[...], rt_ref[...]
        do2b = do_ref[...].astype(BF16)
        qh = [jnp.where(kp, q2, jnp.zeros_like(q2)) for kp in keep]
        doh = [jnp.where(kp, do2b, jnp.zeros_like(do2b)) for kp in keep]
        rtot_h = [jnp.sum(jnp.where(lane == n * HEAD_DIM, rt, 0.0), axis=1, keepdims=True) for n in range(2)]
        acc[...] = jnp.zeros_like(acc)
        rs[...] = jnp.zeros_like(rs)
        es[...] = jnp.zeros_like(es)

        @pl.when(i == 0)
        def _():
            dk_ref[...] = jnp.zeros_like(dk_ref)
            dv_ref[...] = jnp.zeros_like(dv_ref)

        ule, ult = us_ref[...], ui_ref[...]

        def tile(kb, masked):
            off = pl.multiple_of(kb * tk, tk)
            k2 = k_ref[pl.ds(off, tk), :]
            v2 = v_ref[pl.ds(off, tk), :]
            strict = _sb_masks(i, kb, tq, tk) if masked else None
            s = [_dot(qh[h], k2, "nt") for h in range(2)]
            dw = [_dot(doh[h], v2, "nt") for h in range(2)]
            a, sg, r, pin, w, e, cin = ([None] * 2 for _ in range(7))
            for h in range(2):
                sp = _softplus(s[h])
                a[h] = s[h] - sp
                sg[h] = jnp.exp(a[h])
                r[h] = jnp.where(strict, -sp, 0.0) if masked else -sp
                pin[h] = _dot2(r[h], ule)
            for h in range(2):
                lw = a[h] + ((rtot_h[h] - rs[h]) - pin[h])
                w[h] = jnp.exp(jnp.where(strict, lw, -1e30) if masked else lw)
                e[h] = w[h] * dw[h]
                cin[h] = _dot2(e[h], ult)
            dk_t = jnp.zeros((tk, 128), F32)
            dv_t = jnp.zeros((tk, 128), F32)
            for h in range(2):
                dl = e[h] * (1.0 - sg[h]) - (es[h] + cin[h]) * sg[h]
                dl = (jnp.where(strict, dl, 0.0) if masked else dl).astype(BF16)
                rs[h] = rs[h] + jnp.sum(r[h], axis=1, keepdims=True)
                es[h] = es[h] + jnp.sum(e[h], axis=1, keepdims=True)
                acc[h] = acc[h] + _dot(dl, k2)
                dk_t = dk_t + _dot(dl, qh[h], "tn")
                dv_t = dv_t + _dot(w[h].astype(BF16), doh[h], "tn")
            dk_ref[pl.ds(off, tk), :] += dk_t
            dv_ref[pl.ds(off, tk), :] += dv_t

        def step(kb, carry):
            tile(kb, False)
            return carry

        n_off = cnt_ref[pl.program_id(0), i].astype(jnp.int32)
        lax.fori_loop(i - n_off, i, step, 0)
        tile(i, True)
        dq_ref[...] = jnp.where(lane < HEAD_DIM, acc[0], acc[1])

    return pl.pallas_call(
        body, name=name, grid=(D_MODEL // 128, nq),
        in_specs=[pl.BlockSpec((tq, 128), lambda j, i: (i, j)), pl.BlockSpec((t, 128), lambda j, i: (0, j)),
                  pl.BlockSpec((t, 128), lambda j, i: (0, j)), pl.BlockSpec((tq, 128), lambda j, i: (i, j)),
                  pl.BlockSpec((tq, 128), lambda j, i: (i, D_MODEL // 128 + j)),
                  _fix_spec((2 * tk, tk)), _fix_spec((2 * tk, tk)), pl.BlockSpec(memory_space=pltpu.SMEM)],
        out_specs=[pl.BlockSpec((tq, 128), lambda j, i: (i, j)), pl.BlockSpec((t, 128), lambda j, i: (0, j)),
                   pl.BlockSpec((t, 128), lambda j, i: (0, j))],
        out_shape=[jax.ShapeDtypeStruct((t, D_MODEL), F32)] * 3,
        scratch_shapes=[pltpu.VMEM((2, tq, 128), F32), pltpu.VMEM((2, tq, 1), F32), pltpu.VMEM((2, tq, 1), F32)],
        compiler_params=_cp(("parallel", "arbitrary")),
    )(qs, kn, vb, rtot, dycat, _doubled(cst["ule"], tk), _doubled(cst["ult"], tk), cnt)


def _adamw(w, g, m, v, name):
    lead = (1,) * (w.ndim - 2)
    rows, cols = w.shape[-2:]
    tr = rows
    for cand in (256, 128, 64, 32, 16, 8):
        if rows % cand == 0 and rows > cand:
            tr = cand
            break
    c1 = 1.0 - ADAM_B1 ** ADAM_STEP
    c2 = 1.0 - ADAM_B2 ** ADAM_STEP

    def body(w_ref, g_ref, m_ref, v_ref, d_ref, nm_ref, nv_ref):
        gv = g_ref[...]
        nm = ADAM_B1 * m_ref[...] + (1.0 - ADAM_B1) * gv
        nv = ADAM_B2 * v_ref[...] + (1.0 - ADAM_B2) * (gv * gv)
        nm_ref[...] = nm
        nv_ref[...] = nv
        d_ref[...] = -ADAM_LR * ((nm / c1) / (jnp.sqrt(nv / c2) + ADAM_EPS) + ADAM_WD * w_ref[...])

    spec = pl.BlockSpec(lead + (tr, cols), lambda i: (0,) * len(lead) + (i, 0))
    return pl.pallas_call(
        body, name=name, grid=(rows // tr,), in_specs=[spec] * 4, out_specs=[spec] * 3,
        out_shape=[jax.ShapeDtypeStruct(w.shape, F32)] * 3, compiler_params=_cp(("parallel",)),
    )(w, g, m, v)


def _place():
    x, y, c = lax.axis_index("x"), lax.axis_index("y"), lax.axis_index("c")
    chips = [(1 - x, y), (x, 1 - y), (1 - x, 1 - y)]
    return x, y, c, chips


VM = pl.BlockSpec(memory_space=pltpu.VMEM)
HB = pl.BlockSpec(memory_space=pltpu.HBM)


def _small_gather(pack, name, reduce):
    rows = pack.shape[0]

    def body(p_ref, gat_ref, *rest):
        if reduce:
            sum_ref, ss, rs = rest
        else:
            ss, rs = rest
        x, y, c, _ = _place()
        me = 4 * x + 2 * y + c
        peers = [(x, y, 1 - c), (1 - x, y, c), (x, 1 - y, c), (1 - x, 1 - y, c),
                 (1 - x, y, 1 - c), (x, 1 - y, 1 - c), (1 - x, 1 - y, 1 - c)]

        def copy(k, slot, to):
            return pltpu.make_async_remote_copy(src_ref=p_ref, dst_ref=gat_ref.at[slot], send_sem=ss.at[k],
                                                recv_sem=rs.at[k], device_id=to, device_id_type=MESH)

        sends = [copy(k, me, p) for k, p in enumerate(peers)]
        for s in sends:
            s.start()
        gat_ref[me] = p_ref[...]
        for k, p in enumerate(peers):
            copy(k, 4 * p[0] + 2 * p[1] + p[2], p).wait_recv()
        for s in sends:
            s.wait_send()
        if reduce:
            tot = gat_ref[0]
            for b in range(1, 8):
                tot = tot + gat_ref[b]
            sum_ref[...] = tot

    out_shape = [jax.ShapeDtypeStruct((8, rows, D_MODEL), F32)]
    if reduce:
        out_shape.append(jax.ShapeDtypeStruct((rows, D_MODEL), F32))
    return pl.pallas_call(
        body, name=name, in_specs=[VM], out_specs=[VM] * len(out_shape), out_shape=out_shape,
        scratch_shapes=[pltpu.SemaphoreType.DMA((7,)), pltpu.SemaphoreType.DMA((7,))],
        compiler_params=_cp(),
    )(pack)


def _mod_exchange(cond, w_ada, b_shard, name):
    def body(c_ref, w_ref, b_ref, modp_ref, ss, rs):
        x, y, c, chips = _place()
        sh = 2 * x + y
        cv = c_ref[...]
        cv = cv * _sig(cv)
        modp_ref[sh] = jnp.dot(cv, w_ref[...], precision=lax.Precision.HIGHEST,
                               preferred_element_type=F32) + b_ref[...]

        def copy(k, slot, to):
            return pltpu.make_async_remote_copy(src_ref=modp_ref.at[slot], dst_ref=modp_ref.at[slot],
                                                send_sem=ss.at[k], recv_sem=rs.at[k], device_id=to,
                                                device_id_type=MESH)

        sends = [copy(k, sh, (*ch, c)) for k, ch in enumerate(chips)]
        for s in sends:
            s.start()
        for k, ch in enumerate(chips):
            copy(k, 2 * ch[0] + ch[1], (*ch, c)).wait_recv()
        for s in sends:
            s.wait_send()

    return pl.pallas_call(
        body, name=name, in_specs=[VM, VM, VM], out_specs=VM,
        out_shape=jax.ShapeDtypeStruct((N_SHARDS, 8, 6 * D_MODEL // N_SHARDS), F32),
        scratch_shapes=[pltpu.SemaphoreType.DMA((3,)), pltpu.SemaphoreType.DMA((3,))],
        compiler_params=_cp(),
    )(cond, w_ada, b_shard)


def _gather_weights(pack, name):
    def body(p_ref, out_ref, ss, rs):
        x, y, c, chips = _place()
        sh = 2 * x + y
        sib = (x, y, 1 - c)

        def half(slot, hc):
            return out_ref.at[slot, pl.ds(hc * PACK_HALF, PACK_HALF), :]

        def copy(k, src, slot, hc, to):
            return pltpu.make_async_remote_copy(src_ref=src, dst_ref=half(slot, hc), send_sem=ss.at[k],
                                                recv_sem=rs.at[k], device_id=to, device_id_type=MESH)

        first = [copy(j, p_ref.at[pl.ds(c * PACK_HALF, PACK_HALF), :], sh, c, (*ch, c))
                 for j, ch in enumerate(chips)]
        for cp in first:
            cp.start()
        passed = []
        for j, ch in enumerate(chips):
            slot = 2 * ch[0] + ch[1]
            copy(j, half(slot, c), slot, c, (*ch, c)).wait_recv()
            fwd = copy(3 + j, half(slot, c), slot, c, sib)
            fwd.start()
            passed.append(fwd)
        for j, ch in enumerate(chips):
            slot = 2 * ch[0] + ch[1]
            copy(3 + j, half(slot, 1 - c), slot, 1 - c, sib).wait_recv()
        for cp in first + passed:
            cp.wait_send()

    return pl.pallas_call(
        body, name=name, in_specs=[HB], out_specs=HB,
        out_shape=jax.ShapeDtypeStruct((N_SHARDS, PACK_ROWS, D_MODEL), BF16),
        scratch_shapes=[pltpu.SemaphoreType.DMA((6,)), pltpu.SemaphoreType.DMA((6,))],
        compiler_params=_cp(),
    )(pack)


def _sibling_swap(g, name):
    def body(g_ref, out_ref, ss, rs):
        x, y, c, _ = _place()
        cp = pltpu.make_async_remote_copy(
            src_ref=g_ref.at[pl.ds(0, N_SHARDS), pl.ds((1 - c) * PACK_HALF, PACK_HALF), :], dst_ref=out_ref,
            send_sem=ss, recv_sem=rs, device_id=(x, y, 1 - c), device_id_type=MESH)
        cp.start()
        cp.wait()

    return pl.pallas_call(
        body, name=name, in_specs=[HB], out_specs=HB,
        out_shape=jax.ShapeDtypeStruct((N_SHARDS, PACK_HALF, D_MODEL), g.dtype),
        scratch_shapes=[pltpu.SemaphoreType.DMA, pltpu.SemaphoreType.DMA],
        compiler_params=_cp(),
    )(g)


def _chip_sum(g, got, c_idx, name):
    tr = 512
    nb = PACK_HALF // tr

    def body(c_ref, a_ref, b_ref, s_ref, sb_ref):
        s = a_ref[...] + b_ref[...]
        s_ref[...] = s
        sb_ref[...] = s.astype(BF16)

    blk = pl.BlockSpec((1, tr, D_MODEL), lambda s, i, c_ref: (s, i, 0))
    return pl.pallas_call(
        body, name=name,
        grid_spec=pltpu.PrefetchScalarGridSpec(
            num_scalar_prefetch=1, grid=(N_SHARDS, nb),
            in_specs=[pl.BlockSpec((1, tr, D_MODEL), lambda s, i, c_ref: (s, c_ref[0] * nb + i, 0)), blk],
            out_specs=[blk, blk]),
        out_shape=[jax.ShapeDtypeStruct((N_SHARDS, PACK_HALF, D_MODEL), F32),
                   jax.ShapeDtypeStruct((N_SHARDS, PACK_HALF, D_MODEL), BF16)],
        compiler_params=_cp(("parallel", "parallel")),
    )(c_idx, g, got)


def _chip_exchange(sb, name):
    def body(s_ref, out_ref, ss, rs):
        x, y, c, chips = _place()

        def copy(k, slot, to):
            return pltpu.make_async_remote_copy(src_ref=s_ref.at[slot], dst_ref=out_ref.at[k], send_sem=ss.at[k],
                                                recv_sem=rs.at[k], device_id=to, device_id_type=MESH)

        sends = [copy(k, 2 * ch[0] + ch[1], (*ch, c)) for k, ch in enumerate(chips)]
        for cp in sends:
            cp.start()
        for cp in sends:
            cp.wait()

    return pl.pallas_call(
        body, name=name, in_specs=[HB], out_specs=HB,
        out_shape=jax.ShapeDtypeStruct((3, PACK_HALF, D_MODEL), BF16),
        scratch_shapes=[pltpu.SemaphoreType.DMA((3,)), pltpu.SemaphoreType.DMA((3,))],
        compiler_params=_cp(),
    )(sb)


def _total_half(s, got, sh_idx, name):
    tr = 512
    nb = PACK_HALF // tr

    def body(sh_ref, a_ref, r0, r1, r2, o_ref):
        o_ref[...] = ((a_ref[0] + r0[0].astype(F32)) + r1[0].astype(F32)) + r2[0].astype(F32)

    rspec = lambda k: pl.BlockSpec((1, tr, D_MODEL), lambda i, sh_ref: (k, i, 0))
    return pl.pallas_call(
        body, name=name,
        grid_spec=pltpu.PrefetchScalarGridSpec(
            num_scalar_prefetch=1, grid=(nb,),
            in_specs=[pl.BlockSpec((1, tr, D_MODEL), lambda i, sh_ref: (sh_ref[0], i, 0)),
                      rspec(0), rspec(1), rspec(2)],
            out_specs=pl.BlockSpec((tr, D_MODEL), lambda i, sh_ref: (i, 0))),
        out_shape=jax.ShapeDtypeStruct((PACK_HALF, D_MODEL), F32),
        compiler_params=_cp(("parallel",)),
    )(sh_idx, s, got, got, got)


def _join_halves(tot, name):
    def body(t_ref, out_ref, ss, rs):
        x, y, c, _ = _place()
        cp = pltpu.make_async_remote_copy(src_ref=t_ref, dst_ref=out_ref, send_sem=ss, recv_sem=rs,
                                          device_id=(x, y, 1 - c), device_id_type=MESH)
        cp.start()
        cp.wait()

    return pl.pallas_call(
        body, name=name, in_specs=[HB], out_specs=HB,
        out_shape=jax.ShapeDtypeStruct((PACK_HALF, D_MODEL), F32),
        scratch_shapes=[pltpu.SemaphoreType.DMA, pltpu.SemaphoreType.DMA],
        compiler_params=_cp(),
    )(tot)


def _w_ada_grad(cond, dmod_cols, name):
    def body(c_ref, d_ref, o_ref):
        cv = c_ref[...]
        cv = cv * _sig(cv)
        o_ref[...] = lax.dot_general(cv, d_ref[...], _DN["tn"], precision=lax.Precision.HIGHEST,
                                     preferred_element_type=F32)

    return pl.pallas_call(
        body, name=name, in_specs=[VM, VM], out_specs=VM,
        out_shape=jax.ShapeDtypeStruct((D_MODEL, dmod_cols.shape[1]), F32), compiler_params=_cp(),
    )(cond, dmod_cols)


def _pad_rows(a, rows):
    return jnp.pad(a, ((0, rows - a.shape[0]), (0, 0)))


def _pad_cols(a, cols):
    return jnp.pad(a, ((0, 0), (0, cols - a.shape[1])))


def _weight_pack(w_in, w_out, w_gate, w_up, w_down, dtype):
    parts = [_pad_rows(w_in.T, R_IN), w_out, w_gate.T, w_up.T, w_down]
    pack = jnp.concatenate([p.astype(dtype) for p in parts], axis=0)
    return _pad_rows(pack, PACK_ROWS)


def _unpack(p):
    o = 0
    out = []
    for r in (R_IN, R_OUT, R_FF, R_FF, R_FF):
        out.append(p[..., o:o + r, :])
        o += r
    return out


def kernel(x, c, w_ada, b_ada, norm1_w, w_in, conv_w, conv_b, dt_bias, a_log, d_skip, ssd_norm_w, q_norm_w, k_norm_w, w_out, norm2_w, w_gate, w_up, w_down, loss_target, m_w_ada, m_b_ada, m_norm1_w, m_w_in, m_conv_w, m_conv_b, m_dt_bias, m_a_log, m_d_skip, m_ssd_norm_w, m_q_norm_w, m_k_norm_w, m_w_out, m_norm2_w, m_w_gate, m_w_up, m_w_down, v_w_ada, v_b_ada, v_norm1_w, v_w_in, v_conv_w, v_conv_b, v_dt_bias, v_a_log, v_d_skip, v_ssd_norm_w, v_q_norm_w, v_k_norm_w, v_w_out, v_norm2_w, v_w_gate, v_w_up, v_w_down):
    cst = _consts()
    ax, ay, ac = lax.axis_index("x"), lax.axis_index("y"), lax.axis_index("c")
    shard = 2 * ax + ay
    me = 4 * ax + 2 * ay + ac
    xs = x[0]
    tgt = loss_target[0]
    w_in_cols = w_in.shape[2]
    conv_cols = conv_w.shape[2]

    cw_flat = _pad_cols(conv_w[0].reshape(1, -1), 2 * D_MODEL).reshape(2, D_MODEL)
    cpack = jnp.concatenate([jnp.broadcast_to(c, (8, D_MODEL)), _pad_rows(cw_flat, 8)], axis=0)
    gat = _small_gather(cpack, "gather_c", reduce=False)[0]
    c_all = gat[:, 0, :]
    cw = gat[0::2, 8:10, :].reshape(N_SHARDS, 2 * D_MODEL)[:, :4 * conv_cols].reshape(N_SHARDS, 4, conv_cols)
    conv_w_full = jnp.transpose(cw, (1, 0, 2)).reshape(4, D_CONV)

    mod_w = 6 * D_MODEL // N_SHARDS
    b_shard = lax.dynamic_slice(b_ada, (0, shard * mod_w), (1, mod_w))
    modp = _mod_exchange(c_all, w_ada[0], b_shard, "mod_exchange")
    mod_mine = lax.dynamic_slice(modp, (0, me, 0), (N_SHARDS, 1, mod_w)).reshape(6, D_MODEL)
    mod = _pad_rows(mod_mine, 8)

    wpack = _weight_pack(w_in[0], w_out[0], w_gate[0], w_up[0], w_down[0], BF16)
    gp = _gather_weights(wpack, "gather_weights")
    own = lax.broadcasted_iota(jnp.int32, (N_SHARDS, 1, 1), 0) == shard
    p_in, p_out, p_gate, p_up, p_down = _unpack(jnp.where(own, wpack[None], gp))

    wi_t = p_in[:, :w_in_cols, :].reshape(D_IN_PROJ, D_MODEL)
    w_inp_t = jnp.concatenate([wi_t[0:1024], wi_t[2576:5648], wi_t[1024:2560], wi_t[2560:2576],
                               jnp.zeros((112, D_MODEL), BF16)], axis=0)
    w_o = p_out.reshape(2 * D_MODEL, D_MODEL)
    w_gu_t = jnp.concatenate([p_gate.reshape(D_FF, D_MODEL), p_up.reshape(D_FF, D_MODEL)], axis=0)
    w_d = p_down.reshape(D_FF, D_MODEL)

    pad128 = lambda a: _pad_cols(a, 128)
    dtb, alog = pad128(dt_bias), pad128(a_log)
    dsk = jnp.repeat(d_skip, HEAD_DIM, axis=1)
    qw, kw = jnp.tile(q_norm_w, (1, N_HEADS)), jnp.tile(k_norm_w, (1, N_HEADS))

    h1 = _norm_mod(xs, norm1_w, mod, 0, "norm1")
    proj = _matmul(h1, w_inp_t, "nt", F32, "in_proj")
    u = _conv_fwd(proj, conv_w_full, conv_b, "conv_fwd")
    y_ssd, yn, prev = _ssd_fwd(u, proj, dtb, alog, dsk, ssd_norm_w, cst, "ssd_fwd")
    qs, kn, vb = _qk_fwd(proj, qw, kw, cst, "qk_norm")
    rtot, ob, cnt = _sb_fwd(qs, kn, vb, cst, "sb_fwd")
    ycat = jnp.concatenate([yn, ob], axis=1)
    mix = _matmul(ycat, w_o, "nn", F32, "out_proj")
    x1, h2 = _resid_norm(xs, mix, norm2_w, mod, "resid_norm2")
    gu = _matmul(h2, w_gu_t, "nt", F32, "ffn_in")
    act = _act_fwd(gu, "ffn_act")
    ffn = _matmul(act, w_d, "nn", F32, "ffn_out", tk_cap=1408)
    dffn, dout, dg2, loss8 = _loss_head(x1, ffn, tgt, mod, "loss_head")
    loss = lax.psum(loss8[0, 0], ("x", "y", "c"))

    dact = _matmul(dffn, w_d, "nt", F32, "d_act")
    g_down = _matmul(act, dffn, "tn", F32, "g_w_down", tm_cap=1408)
    dgu = _act_bwd(dact, gu, "ffn_act_bwd")
    dh2 = _matmul(dgu, w_gu_t, "nn", F32, "d_h2", tk_cap=1408)
    g_gu_t = _matmul(dgu, h2, "tn", F32, "g_w_gu", tm_cap=1408)
    dx1, dmix, acc2 = _norm_bwd(dh2, x1, dout, mix, norm2_w, mod, 3, 2, "norm2_bwd")
    dycat = _matmul(dmix, w_o, "nt", F32, "d_ycat")
    g_out = _matmul(ycat, dmix, "tn", F32, "g_w_out")
    du, ddt, dz, acc_ssd, acc16 = _ssd_bwd(u, proj, y_ssd, prev, dycat, dtb, alog, dsk, ssd_norm_w, cst, "ssd_bwd")
    dqs, dkn, dv = _sb_bwd(qs, kn, vb, rtot, cnt, dycat, cst, "sb_bwd")
    dq, dk, dvb, acc_qk = _qk_bwd(proj, dqs, dkn, dv, qw, kw, cst, "qk_norm_bwd")
    dxbc, g_conv_w, g_conv_b = _conv_bwd(proj, du, conv_w_full, conv_b, "conv_bwd")
    dproj = jnp.concatenate([dz, dq, dk, dvb, dxbc, ddt], axis=1)
    dh1 = _matmul(dproj, w_inp_t, "nn", F32, "d_h1", tk_cap=1152)
    g_inp_t = _matmul(dproj, h1, "tn", F32, "g_w_in", tm_cap=1920)
    grad_x, acc1 = _norm_bwd(dh1, xs, dx1, None, norm1_w, mod, 0, None, "norm1_bwd")

    last = jnp.concatenate([acc_qk[0:1, 0:64], acc_qk[1:2, 0:64], acc16[0:1, 0:16], acc16[1:2, 0:16],
                            acc16[2:3, 0:16]], axis=1)
    spack = jnp.concatenate([
        acc1[0:2], acc2[3:4], acc2[0:2], dg2,
        acc1[2:3], acc2[2:3], acc_ssd[0:1],
        _pad_cols(g_conv_b, 2 * D_MODEL).reshape(2, D_MODEL),
        g_conv_w.reshape(6, D_MODEL),
        _pad_cols(last, D_MODEL)], axis=0)
    sgat, ssum = _small_gather(_pad_rows(spack, SMALL_ROWS), "gather_small", reduce=True)
    g_b_ada = ssum[0:6].reshape(1, 6 * D_MODEL)
    g_norm1, g_norm2, g_ssdn = ssum[6:7], ssum[7:8], ssum[8:9]
    g_cb = ssum[9:11].reshape(1, 2 * D_MODEL)[:, :D_CONV]
    g_cw = lax.dynamic_slice(ssum[11:17].reshape(4, D_CONV), (0, shard * conv_cols), (4, conv_cols))
    g_qn, g_kn = ssum[17:18, 0:64], ssum[17:18, 64:128]
    g_dtb, g_alog, g_dsk = ssum[17:18, 128:144], ssum[17:18, 144:160], ssum[17:18, 160:176]
    dmod_all = sgat[:, 0:6, :].reshape(8, 6 * D_MODEL)
    g_w_ada = _w_ada_grad(c_all, lax.dynamic_slice(dmod_all, (0, shard * mod_w), (8, mod_w)), "g_w_ada")

    gi_t = jnp.concatenate([g_inp_t[0:1024], g_inp_t[4096:5632], g_inp_t[5632:5648], g_inp_t[1024:4096]], axis=0)
    ff = D_FF // N_SHARDS
    gpack = jnp.concatenate([
        jnp.pad(gi_t.reshape(N_SHARDS, w_in_cols, D_MODEL), ((0, 0), (0, R_IN - w_in_cols), (0, 0))),
        g_out.reshape(N_SHARDS, R_OUT, D_MODEL),
        g_gu_t[:D_FF].reshape(N_SHARDS, R_FF, D_MODEL), g_gu_t[D_FF:].reshape(N_SHARDS, R_FF, D_MODEL),
        g_down.reshape(N_SHARDS, R_FF, D_MODEL),
        jnp.zeros((N_SHARDS, PACK_ROWS - R_IN - R_OUT - 3 * R_FF, D_MODEL), F32)], axis=1)
    got = _sibling_swap(gpack, "rs_sibling_swap")
    csum, csum_b = _chip_sum(gpack, got, ac.reshape(1).astype(jnp.int32), "rs_chip_sum")
    got2 = _chip_exchange(csum_b, "rs_chip_exchange")
    tot = _total_half(csum, got2, shard.reshape(1).astype(jnp.int32), "rs_total")
    other = _join_halves(tot, "rs_join")
    gfull = jnp.where(ac == 0, jnp.concatenate([tot, other], axis=0), jnp.concatenate([other, tot], axis=0))
    r_in, r_out, r_gate, r_up, r_down = _unpack(gfull)
    g_w_in = r_in[:w_in_cols].T
    g_w_gate, g_w_up = r_gate.T, r_up.T

    grads = dict(w_ada=g_w_ada, b_ada=g_b_ada, norm1_w=g_norm1, w_in=g_w_in, conv_w=g_cw, conv_b=g_cb,
                 dt_bias=g_dtb, a_log=g_alog, d_skip=g_dsk, ssd_norm_w=g_ssdn, q_norm_w=g_qn, k_norm_w=g_kn,
                 w_out=r_out, norm2_w=g_norm2, w_gate=g_w_gate, w_up=g_w_up, w_down=r_down)
    weights = dict(w_ada=(w_ada, m_w_ada, v_w_ada), b_ada=(b_ada, m_b_ada, v_b_ada),
                   norm1_w=(norm1_w, m_norm1_w, v_norm1_w), w_in=(w_in, m_w_in, v_w_in),
                   conv_w=(conv_w, m_conv_w, v_conv_w), conv_b=(conv_b, m_conv_b, v_conv_b),
                   dt_bias=(dt_bias, m_dt_bias, v_dt_bias), a_log=(a_log, m_a_log, v_a_log),
                   d_skip=(d_skip, m_d_skip, v_d_skip), ssd_norm_w=(ssd_norm_w, m_ssd_norm_w, v_ssd_norm_w),
                   q_norm_w=(q_norm_w, m_q_norm_w, v_q_norm_w), k_norm_w=(k_norm_w, m_k_norm_w, v_k_norm_w),
                   w_out=(w_out, m_w_out, v_w_out), norm2_w=(norm2_w, m_norm2_w, v_norm2_w),
                   w_gate=(w_gate, m_w_gate, v_w_gate), w_up=(w_up, m_w_up, v_w_up),
                   w_down=(w_down, m_w_down, v_w_down))
    names = list(weights)
    g_out_l, d_out_l, m_out_l, v_out_l = [], [], [], []
    for n in names:
        w, m, v = weights[n]
        g = grads[n].reshape(w.shape)
        d, nm, nv = _adamw(w, g, m, v, "adamw_" + n)
        g_out_l.append(g)
        d_out_l.append(d)
        m_out_l.append(nm)
        v_out_l.append(nv)
    return (loss, grad_x[None], *g_out_l, *d_out_l, *m_out_l, *v_out_l)
```

```python
import functools

import numpy as np
import jax
import jax.numpy as jnp
from jax import lax
from jax.experimental import pallas as pl
from jax.experimental.pallas import tpu as pltpu

F32, BF16 = jnp.float32, jnp.bfloat16
MESH = pl.DeviceIdType.MESH

D_MODEL = 1024
HEAD_DIM = 64
N_HEADS = 16
D_CONV = 1536
D_FF = 2816
D_IN_PROJ = 5648
D_PROJ_PAD = 5760
CHUNK = 128
SB_TILE = 256
SB_DEAD = -105.0
EPS = 1e-6
N_SHARDS = 4
R_IN, R_OUT, R_FF = 1440, 512, 704
SMALL_ROWS = 24

ADAM_LR, ADAM_B1, ADAM_B2, ADAM_EPS, ADAM_WD, ADAM_STEP = 0.001, 0.9, 0.999, 1e-08, 0.01, 10

VMEM_LIMIT = 48 * 1024 * 1024
ADAM_BLOCK_BYTES = 3 * 512 * 1024

_DN = {"nn": (((1,), (0,)), ((), ())), "nt": (((1,), (1,)), ((), ())), "tn": (((0,), (0,)), ((), ()))}


def _dot(a, b, dims="nn"):
    return lax.dot_general(a, b, _DN[dims], preferred_element_type=F32)


def _split(x, n):
    out = []
    for _ in range(n - 1):
        p = x.astype(BF16)
        out.append(p)
        x = x - p.astype(F32)
    out.append(x.astype(BF16))
    return out


def _dotx_r(x, b_exact, dims="nn", n=3):
    acc = None
    for p in reversed(_split(x, n)):
        t = _dot(p, b_exact, dims)
        acc = t if acc is None else acc + t
    return acc


def _dotx_l(a_exact, x, dims="nn", n=3):
    acc = None
    for p in reversed(_split(x, n)):
        t = _dot(a_exact, p, dims)
        acc = t if acc is None else acc + t
    return acc


def _dot2(x, b2):
    hi = lax.bitcast_convert_type(lax.bitcast_convert_type(x, jnp.int32) & jnp.int32(-65536), F32)
    return _dot(jnp.concatenate([hi.astype(BF16), (x - hi).astype(BF16)], axis=1), b2)


def _sig(x):
    return 1.0 / (1.0 + jnp.exp(-x))


def _softplus(x):
    return jnp.maximum(x, 0.0) + jnp.log(1.0 + jnp.exp(-jnp.abs(x)))


def _cp(sem=None, vmem=VMEM_LIMIT):
    return pltpu.CompilerParams(dimension_semantics=sem, vmem_limit_bytes=vmem)


def _colsum(x):
    return jnp.sum(x, axis=0, keepdims=True)


def _consts():
    ch = np.arange(D_MODEL)
    expand = (np.arange(128)[:, None] == (ch // HEAD_DIM)[None, :]).astype(np.float32)
    fold = (ch[:, None] % HEAD_DIM == np.arange(128)[None, :]).astype(np.float32)
    i = np.arange(CHUNK)
    tril = (i[:, None] >= i[None, :]).astype(np.float32)
    j = np.arange(SB_TILE)
    ustrict = (j[:, None] > j[None, :]).astype(np.float32)
    ule = (j[:, None] <= j[None, :]).astype(np.float32)
    ult = (j[:, None] < j[None, :]).astype(np.float32)
    c = lambda a: jnp.asarray(a, BF16)
    return dict(expand=c(expand), hsum=c(expand.T), fold=c(fold), tril=c(tril), triu=c(tril.T),
                ustrict=ustrict, ule=ule, ult=ult)


def _doubled(tri, tk):
    b = tri[:tk, :tk]
    return jnp.asarray(np.concatenate([b, b], axis=0), BF16)


def _pick(n, cap):
    best = 128
    for t in range(128, min(n, cap) + 1, 128):
        if n % t == 0:
            best = t
    return n if n <= cap else best


def _matmul(a, b, dims, out_dtype, name, tm_cap=1024, tn_cap=2048, tk_cap=1024):
    if dims == "nn":
        (m, k), (_, n) = a.shape, b.shape
    elif dims == "nt":
        (m, k), (n, _) = a.shape, b.shape
    else:
        (k, m), (_, n) = a.shape, b.shape
    tm, tn, tk = _pick(m, tm_cap), _pick(n, tn_cap), _pick(k, tk_cap)
    nk = k // tk
    a_spec = (pl.BlockSpec((tk, tm), lambda i, j, kk: (kk, i)) if dims == "tn"
              else pl.BlockSpec((tm, tk), lambda i, j, kk: (i, kk)))
    b_spec = (pl.BlockSpec((tn, tk), lambda i, j, kk: (j, kk)) if dims == "nt"
              else pl.BlockSpec((tk, tn), lambda i, j, kk: (kk, j)))

    def body(a_ref, b_ref, o_ref, acc_ref):
        kk = pl.program_id(2)
        part = _dot(a_ref[...], b_ref[...], dims)
        if nk == 1:
            o_ref[...] = part.astype(out_dtype)
        else:
            @pl.when(kk == 0)
            def _():
                acc_ref[...] = part

            @pl.when(kk > 0)
            def _():
                acc_ref[...] += part

            @pl.when(kk == nk - 1)
            def _():
                o_ref[...] = acc_ref[...].astype(out_dtype)

    return pl.pallas_call(
        body, name=name, grid=(m // tm, n // tn, nk),
        in_specs=[a_spec, b_spec],
        out_specs=pl.BlockSpec((tm, tn), lambda i, j, kk: (i, j)),
        out_shape=jax.ShapeDtypeStruct((m, n), out_dtype),
        scratch_shapes=[pltpu.VMEM((tm, tn) if nk > 1 else (8, 128), F32)],
        compiler_params=_cp(("parallel", "parallel", "arbitrary")),
    )(a, b)


def _row_spec(tm, width=D_MODEL, col=0):
    return pl.BlockSpec((tm, width), lambda i: (i, col))


def _fix_spec(shape):
    return pl.BlockSpec(shape, lambda *_: (0,) * len(shape))


def _norm_mod(x, nw, mod, row_sh, name):
    t = x.shape[0]
    tm = min(t, 512)

    def body(x_ref, nw_ref, mod_ref, h_ref):
        xv = x_ref[...]
        r = lax.rsqrt(jnp.mean(xv * xv, axis=-1, keepdims=True) + EPS)
        sh = mod_ref[row_sh:row_sh + 1, :]
        sc = mod_ref[row_sh + 1:row_sh + 2, :]
        h_ref[...] = (xv * r * nw_ref[...] * (1.0 + sc) + sh).astype(BF16)

    return pl.pallas_call(
        body, name=name, grid=(t // tm,),
        in_specs=[_row_spec(tm), _fix_spec((1, D_MODEL)), _fix_spec((8, D_MODEL))],
        out_specs=_row_spec(tm), out_shape=jax.ShapeDtypeStruct((t, D_MODEL), BF16),
        compiler_params=_cp(("parallel",)),
    )(x, nw, mod)


def _resid_norm(x, mix, nw, mod, name):
    t = x.shape[0]
    tm = min(t, 512)

    def body(x_ref, mix_ref, nw_ref, mod_ref, x1_ref, h_ref):
        x1 = x_ref[...] + mod_ref[2:3, :] * mix_ref[...]
        x1_ref[...] = x1
        r = lax.rsqrt(jnp.mean(x1 * x1, axis=-1, keepdims=True) + EPS)
        h_ref[...] = (x1 * r * nw_ref[...] * (1.0 + mod_ref[4:5, :]) + mod_ref[3:4, :]).astype(BF16)

    return pl.pallas_call(
        body, name=name, grid=(t // tm,),
        in_specs=[_row_spec(tm), _row_spec(tm), _fix_spec((1, D_MODEL)), _fix_spec((8, D_MODEL))],
        out_specs=[_row_spec(tm), _row_spec(tm)],
        out_shape=[jax.ShapeDtypeStruct((t, D_MODEL), F32), jax.ShapeDtypeStruct((t, D_MODEL), BF16)],
        compiler_params=_cp(("parallel",)),
    )(x, mix, nw, mod)


def _act_fwd(gu, name):
    t = gu.shape[0]
    tm, tn = min(t, 512), D_FF // 2
    nb = D_FF // tn

    def body(g_ref, u_ref, a_ref):
        g = g_ref[...].astype(F32)
        a_ref[...] = (g * _sig(g) * u_ref[...].astype(F32)).astype(BF16)

    return pl.pallas_call(
        body, name=name, grid=(t // tm, nb),
        in_specs=[pl.BlockSpec((tm, tn), lambda i, j: (i, j)), pl.BlockSpec((tm, tn), lambda i, j: (i, j + nb))],
        out_specs=pl.BlockSpec((tm, tn), lambda i, j: (i, j)),
        out_shape=jax.ShapeDtypeStruct((t, D_FF), BF16),
        compiler_params=_cp(("parallel", "parallel")),
    )(gu, gu)


def _act_bwd(dact, gu, name):
    t = gu.shape[0]
    tm = min(t, 256)

    def body(d_ref, g_ref, u_ref, o_ref):
        g, d = g_ref[...].astype(F32), d_ref[...].astype(F32)
        s = _sig(g)
        o_ref[:, 0:D_FF] = (d * u_ref[...].astype(F32) * s * (1.0 + g * (1.0 - s))).astype(BF16)
        o_ref[:, D_FF:2 * D_FF] = (d * g * s).astype(BF16)

    return pl.pallas_call(
        body, name=name, grid=(t // tm,),
        in_specs=[pl.BlockSpec((tm, D_FF), lambda i: (i, 0)), pl.BlockSpec((tm, D_FF), lambda i: (i, 0)),
                  pl.BlockSpec((tm, D_FF), lambda i: (i, 1))],
        out_specs=pl.BlockSpec((tm, 2 * D_FF), lambda i: (i, 0)),
        out_shape=jax.ShapeDtypeStruct((t, 2 * D_FF), BF16),
        compiler_params=_cp(("parallel",)),
    )(dact, gu, gu)


def _loss_head(x1, ffn, tgt, mod, name):
    t = x1.shape[0]
    tm = min(t, 512)

    def body(x1_ref, f_ref, t_ref, mod_ref, dffn_ref, dout_ref, dg2_ref, loss_ref):
        i = pl.program_id(0)
        g2 = mod_ref[5:6, :]
        f = f_ref[...]
        err = x1_ref[...] + g2 * f - t_ref[...]
        dout = err * (1.0 / D_MODEL)
        dout_ref[...] = dout
        dffn_ref[...] = (dout * g2).astype(BF16)
        part = jnp.zeros((8, 128), F32) + 0.5 * jnp.sum(jnp.mean(err * err, axis=-1, keepdims=True))

        @pl.when(i == 0)
        def _():
            dg2_ref[...] = _colsum(dout * f)
            loss_ref[...] = part

        @pl.when(i > 0)
        def _():
            dg2_ref[...] += _colsum(dout * f)
            loss_ref[...] += part

    return pl.pallas_call(
        body, name=name, grid=(t // tm,),
        in_specs=[_row_spec(tm), _row_spec(tm), _row_spec(tm), _fix_spec((8, D_MODEL))],
        out_specs=[_row_spec(tm), _row_spec(tm), _fix_spec((1, D_MODEL)), _fix_spec((8, 128))],
        out_shape=[jax.ShapeDtypeStruct((t, D_MODEL), BF16), jax.ShapeDtypeStruct((t, D_MODEL), F32),
                   jax.ShapeDtypeStruct((1, D_MODEL), F32), jax.ShapeDtypeStruct((8, 128), F32)],
        compiler_params=_cp(("arbitrary",)),
    )(x1, ffn, tgt, mod)


def _norm_bwd(dh, xin, dres, aux, nw, mod, row_sh, gate_row, name):
    t = xin.shape[0]
    tm = min(t, 512)
    with_gate = gate_row is not None

    def body(*refs):
        if with_gate:
            dh_ref, x_ref, dr_ref, aux_ref, nw_ref, mod_ref, dx_ref, dg_ref, acc_ref = refs
        else:
            dh_ref, x_ref, dr_ref, nw_ref, mod_ref, dx_ref, acc_ref = refs
        i = pl.program_id(0)
        xv, dhv = x_ref[...], dh_ref[...]
        r = lax.rsqrt(jnp.mean(xv * xv, axis=-1, keepdims=True) + EPS)
        xn = xv * r
        nwv = nw_ref[...]
        sc1 = 1.0 + mod_ref[row_sh + 1:row_sh + 2, :]
        dxn = dhv * (nwv * sc1)
        dx = dr_ref[...] + r * (dxn - xn * jnp.mean(dxn * xn, axis=-1, keepdims=True))
        dx_ref[...] = dx
        dhx = dhv * xn
        rows = [_colsum(dhv), _colsum(dhx * nwv), _colsum(dhx * sc1)]
        if with_gate:
            dg_ref[...] = (dx * mod_ref[gate_row:gate_row + 1, :]).astype(BF16)
            rows.append(_colsum(dx * aux_ref[...]))

        @pl.when(i == 0)
        def _():
            acc_ref[...] = jnp.zeros_like(acc_ref)

        for k, v in enumerate(rows):
            acc_ref[k:k + 1, :] += v

    ins = [dh, xin, dres] + ([aux] if with_gate else []) + [nw, mod]
    in_specs = [_row_spec(tm)] * (4 if with_gate else 3) + [_fix_spec((1, D_MODEL)), _fix_spec((8, D_MODEL))]
    out_specs = [_row_spec(tm)] + ([_row_spec(tm)] if with_gate else []) + [_fix_spec((8, D_MODEL))]
    out_shape = ([jax.ShapeDtypeStruct((t, D_MODEL), F32)]
                 + ([jax.ShapeDtypeStruct((t, D_MODEL), BF16)] if with_gate else [])
                 + [jax.ShapeDtypeStruct((8, D_MODEL), F32)])
    return pl.pallas_call(
        body, name=name, grid=(t // tm,), in_specs=in_specs, out_specs=out_specs, out_shape=out_shape,
        compiler_params=_cp(("arbitrary",)),
    )(*ins)


XBC_COL0 = 4096 // 128
DT_COL = 5632 // 128


def _conv_pre(xv, w_ref, b_ref):
    t = xv.shape[0]
    row = lax.broadcasted_iota(jnp.int32, xv.shape, 0)
    pre = xv * w_ref[3:4, :] + b_ref[...]
    shifted = []
    for k in range(3):
        s = 3 - k
        xs = jnp.where(row >= s, pltpu.roll(xv, s, 0), 0.0)
        shifted.append(xs)
        pre = pre + xs * w_ref[k:k + 1, :]
    return pre, shifted, row, t


def _conv_fwd(proj, conv_w, conv_b, name):
    t = proj.shape[0]

    def body(x_ref, w_ref, b_ref, u_ref):
        pre, _, _, _ = _conv_pre(x_ref[...], w_ref, b_ref)
        u_ref[...] = pre * _sig(pre)

    return pl.pallas_call(
        body, name=name, grid=(D_CONV // 128,),
        in_specs=[pl.BlockSpec((t, 128), lambda j: (0, XBC_COL0 + j)), pl.BlockSpec((4, 128), lambda j: (0, j)),
                  pl.BlockSpec((1, 128), lambda j: (0, j))],
        out_specs=pl.BlockSpec((t, 128), lambda j: (0, j)),
        out_shape=jax.ShapeDtypeStruct((t, D_CONV), F32),
        compiler_params=_cp(("parallel",)),
    )(proj, conv_w, conv_b)


def _conv_bwd(proj, du, conv_w, conv_b, name):
    t = proj.shape[0]

    def body(x_ref, du_ref, w_ref, b_ref, dx_ref, dw_ref, db_ref):
        pre, shifted, row, _ = _conv_pre(x_ref[...], w_ref, b_ref)
        s = _sig(pre)
        dpre = du_ref[...] * s * (1.0 + pre * (1.0 - s))
        db_ref[...] = _colsum(dpre)
        dx = dpre * w_ref[3:4, :]
        dw_ref[3:4, :] = _colsum(dpre * x_ref[...])
        for k in range(3):
            sft = 3 - k
            dw_ref[k:k + 1, :] = _colsum(dpre * shifted[k])
            back = jnp.where(row < t - sft, pltpu.roll(dpre, t - sft, 0), 0.0)
            dx = dx + back * w_ref[k:k + 1, :]
        dx_ref[...] = dx.astype(BF16)

    return pl.pallas_call(
        body, name=name, grid=(D_CONV // 128,),
        in_specs=[pl.BlockSpec((t, 128), lambda j: (0, XBC_COL0 + j)), pl.BlockSpec((t, 128), lambda j: (0, j)),
                  pl.BlockSpec((4, 128), lambda j: (0, j)), pl.BlockSpec((1, 128), lambda j: (0, j))],
        out_specs=[pl.BlockSpec((t, 128), lambda j: (0, j)), pl.BlockSpec((4, 128), lambda j: (0, j)),
                   pl.BlockSpec((1, 128), lambda j: (0, j))],
        out_shape=[jax.ShapeDtypeStruct((t, D_CONV), BF16), jax.ShapeDtypeStruct((4, D_CONV), F32),
                   jax.ShapeDtypeStruct((1, D_CONV), F32)],
        compiler_params=_cp(("parallel",)),
    )(proj, du, conv_w, conv_b)


def _ssd_common(dtraw_ref, dtb_ref, alog_ref, tril, expand, l_s, lt_s):
    lane = lax.broadcasted_iota(jnp.int32, (1, 128), 1)
    dt = _softplus(dtraw_ref[...] + dtb_ref[...])
    a = jnp.where(lane < N_HEADS, -jnp.exp(alog_ref[...]), 0.0)
    lcs = _dotx_l(tril, dt * a)
    l_s[...] = lcs
    lt_s[...] = lcs.T
    llast = l_s[CHUNK - 1:CHUNK, :]
    ea = _dotx_r(jnp.exp(lcs), expand)
    ds = _dotx_r(jnp.exp(llast - lcs), expand)
    dtx = _dotx_r(dt, expand)
    return dt, a, lcs, llast, ea, ds, dtx


def _head_col(lcs, h):
    lane = lax.broadcasted_iota(jnp.int32, lcs.shape, 1)
    return jnp.sum(jnp.where(lane == h, lcs, 0.0), axis=1, keepdims=True)


def _decay(lcs, lt_s, h, causal):
    seg = _head_col(lcs, h) - lt_s[h:h + 1, :]
    return jnp.exp(jnp.where(causal, seg, -1e30))


def _ssd_fwd(u, proj, dtb, alog, dsk, nw, cst, name):
    t = u.shape[0]
    nc = t // CHUNK

    def body(xs_ref, b_ref, c_ref, dtraw_ref, z_ref, dtb_ref, alog_ref, dsk_ref, nw_ref, tril_ref, exp_ref,
             y_ref, yn_ref, prev_ref, carry, l_s, lt_s, yd_s):
        i = pl.program_id(0)

        @pl.when(i == 0)
        def _():
            carry[...] = jnp.zeros_like(carry)

        expand = exp_ref[...]
        dt, a, lcs, llast, ea, ds, dtx = _ssd_common(dtraw_ref, dtb_ref, alog_ref, tril_ref[...], expand, l_s, lt_s)
        xs = xs_ref[...]
        xg = xs * dtx
        xgb = xg.astype(BF16)
        xgd = (xg * ds).astype(BF16)
        prev = carry[...]
        prev_ref[0] = prev
        prevb = prev.astype(BF16)
        ri = lax.broadcasted_iota(jnp.int32, (CHUNK, CHUNK), 0)
        ci = lax.broadcasted_iota(jnp.int32, (CHUNK, CHUNK), 1)
        causal = ri >= ci
        lane = lax.broadcasted_iota(jnp.int32, (1, 128), 1)
        new_states, yoff = [], []
        for g in range(2):
            bg = b_ref[:, g * 128:(g + 1) * 128].astype(BF16)
            cg = c_ref[:, g * 128:(g + 1) * 128].astype(BF16)
            sc = _dot(cg, bg, "nt")
            gs = slice(g * 512, (g + 1) * 512)
            new_states.append(_dot(bg, xgd[:, gs], "tn"))
            yoff.append(_dot(cg, prevb[:, gs]))
            for pr in range(4):
                col = g * 512 + pr * 128
                xp = xgb[:, col:col + 128]
                acc = jnp.zeros((CHUNK, 128), F32)
                for half in range(2):
                    h = g * 8 + pr * 2 + half
                    m = (sc * _decay(lcs, lt_s, h, causal)).astype(BF16)
                    keep = (lane < HEAD_DIM) if half == 0 else (lane >= HEAD_DIM)
                    acc = acc + _dot(m, jnp.where(keep, xp, jnp.zeros_like(xp)))
                yd_s[:, col:col + 128] = acc
        y = yd_s[...] + jnp.concatenate(yoff, axis=1) * ea + xs * dsk_ref[...]
        y_ref[...] = y
        carry[...] = prev * jnp.max(_dotx_r(jnp.exp(llast) + jnp.zeros((8, 128), F32), expand), axis=0, keepdims=True) \
            + jnp.concatenate(new_states, axis=1)
        z = z_ref[...]
        yz = y * (z * _sig(z))
        nwv = nw_ref[...]
        for g in range(2):
            gs = slice(g * 512, (g + 1) * 512)
            v = yz[:, gs]
            r = lax.rsqrt(jnp.mean(v * v, axis=-1, keepdims=True) + EPS)
            yn_ref[:, gs] = (v * r * nwv[:, gs]).astype(BF16)

    row = lambda w, col: pl.BlockSpec((CHUNK, w), lambda i: (i, col))
    return pl.pallas_call(
        body, name=name, grid=(nc,),
        in_specs=[row(1024, 0), row(256, 4), row(256, 5), row(128, DT_COL), row(1024, 0),
                  _fix_spec((1, 128)), _fix_spec((1, 128)), _fix_spec((1, D_MODEL)), _fix_spec((1, D_MODEL)),
                  _fix_spec((CHUNK, CHUNK)), _fix_spec((128, D_MODEL))],
        out_specs=[row(1024, 0), row(1024, 0), pl.BlockSpec((1, 128, D_MODEL), lambda i: (i, 0, 0))],
        out_shape=[jax.ShapeDtypeStruct((t, D_MODEL), F32), jax.ShapeDtypeStruct((t, D_MODEL), BF16),
                   jax.ShapeDtypeStruct((nc, 128, D_MODEL), F32)],
        scratch_shapes=[pltpu.VMEM((128, D_MODEL), F32), pltpu.VMEM((128, 128), F32), pltpu.VMEM((128, 128), F32),
                        pltpu.VMEM((CHUNK, D_MODEL), F32)],
        compiler_params=_cp(("arbitrary",)),
    )(u, u, u, proj, proj, dtb, alog, dsk, nw, cst["tril"], cst["expand"])


def _ssd_bwd(u, proj, y, prev, dycat, dtb, alog, dsk, nw, cst, name):
    t = u.shape[0]
    nc = t // CHUNK

    def body(xs_ref, b_ref, c_ref, dtraw_ref, z_ref, y_ref, prev_ref, dyn_ref, dtb_ref, alog_ref, dsk_ref, nw_ref,
             tril_ref, triu_ref, exp_ref, hs_ref,
             du_ref, ddt_ref, dz_ref, acc_ref, acc16_ref, dcarry, l_s, lt_s, dxg_s):
        i = pl.program_id(0)

        @pl.when(i == 0)
        def _():
            dcarry[...] = jnp.zeros_like(dcarry)
            acc_ref[...] = jnp.zeros_like(acc_ref)
            acc16_ref[...] = jnp.zeros_like(acc16_ref)

        expand, hsum = exp_ref[...], hs_ref[...]
        dt, a, lcs, llast, ea, ds, dtx = _ssd_common(dtraw_ref, dtb_ref, alog_ref, tril_ref[...], expand, l_s, lt_s)
        xs = xs_ref[...]
        xg = xs * dtx
        xgb = xg.astype(BF16)
        xgdf = xg * ds
        xgd = xgdf.astype(BF16)
        dsk_v, nwv = dsk_ref[...], nw_ref[...]
        z, y = z_ref[...], y_ref[...]
        sz = _sig(z)
        silz = z * sz
        yz = y * silz
        dyn = dyn_ref[...]
        dyz_parts, dnw_parts = [], []
        for g in range(2):
            gs = slice(g * 512, (g + 1) * 512)
            v = yz[:, gs]
            r = lax.rsqrt(jnp.mean(v * v, axis=-1, keepdims=True) + EPS)
            yhat = v * r
            dnw_parts.append(_colsum(dyn[:, gs] * yhat))
            dw = dyn[:, gs] * nwv[:, gs]
            dyz_parts.append(r * (dw - yhat * jnp.mean(dw * yhat, axis=-1, keepdims=True)))
        dyz = jnp.concatenate(dyz_parts, axis=1)
        dy = dyz * silz
        dz_ref[...] = (dyz * y * (sz * (1.0 + z * (1.0 - sz)))).astype(BF16)
        acc_ref[0:1, :] += jnp.concatenate(dnw_parts, axis=1)
        acc_ref[1:2, :] += _colsum(dy * xs)
        dyb = dy.astype(BF16)
        dq = (dy * ea).astype(BF16)
        dcar = dcarry[...]
        dcarb = dcar.astype(BF16)
        prev = prev_ref[0]
        prevb = prev.astype(BF16)
        ri = lax.broadcasted_iota(jnp.int32, (CHUNK, CHUNK), 0)
        ci = lax.broadcasted_iota(jnp.int32, (CHUNK, CHUNK), 1)
        causal = ri >= ci
        lane = lax.broadcasted_iota(jnp.int32, (1, 128), 1)
        dprev, dxgd, yoff = [], [], []
        dl_l = jnp.zeros((CHUNK, CHUNK), F32)
        dl_s = jnp.zeros((CHUNK, CHUNK), F32)
        for g in range(2):
            gs = slice(g * 512, (g + 1) * 512)
            bg = b_ref[:, g * 128:(g + 1) * 128].astype(BF16)
            cg = c_ref[:, g * 128:(g + 1) * 128].astype(BF16)
            sc = _dot(cg, bg, "nt")
            yoff.append(_dot(cg, prevb[:, gs]))
            dcg = _dot(dq[:, gs], prevb[:, gs], "nt")
            dprev.append(_dot(cg, dq[:, gs], "tn"))
            dbg = _dot(xgd[:, gs], dcarb[:, gs], "nt")
            dxgd.append(_dot(bg, dcarb[:, gs]))
            dsc = jnp.zeros((CHUNK, CHUNK), F32)
            for pr in range(4):
                col = g * 512 + pr * 128
                xp = xgb[:, col:col + 128]
                dyp = dyb[:, col:col + 128]
                acc = jnp.zeros((CHUNK, 128), F32)
                for half in range(2):
                    h = g * 8 + pr * 2 + half
                    dec = _decay(lcs, lt_s, h, causal)
                    mf = sc * dec
                    keep = (lane < HEAD_DIM) if half == 0 else (lane >= HEAD_DIM)
                    dyh = jnp.where(keep, dyp, jnp.zeros_like(dyp))
                    dm = _dot(dyh, xp, "nt")
                    acc = acc + _dot(mf.astype(BF16), dyh, "tn")
                    dsc = dsc + dm * dec
                    gm = dm * mf
                    dl_l = dl_l + jnp.where(ci == h, jnp.sum(gm, axis=1, keepdims=True), 0.0)
                    dl_s = dl_s + jnp.where(ri == h, jnp.sum(gm, axis=0, keepdims=True), 0.0)
                dxg_s[:, col:col + 128] = acc
            dscb = dsc.astype(BF16)
            dcg = dcg + _dot(dscb, bg)
            dbg = dbg + _dot(dscb, cg, "tn")
            du_ref[:, 1024 + g * 128:1024 + (g + 1) * 128] = dbg
            du_ref[:, 1280 + g * 128:1280 + (g + 1) * 128] = dcg
        dxgd = jnp.concatenate(dxgd, axis=1)
        dxg = dxg_s[...] + dxgd * ds
        du_ref[:, 0:1024] = dy * dsk_v + dxg * dtx
        hs1 = _dotx_r(dxg * xs, hsum)
        yoff = jnp.concatenate(yoff, axis=1) * ea
        dl = dl_l - dl_s.T + _dotx_r(dy * yoff - xgdf * dxgd, hsum)
        rows8 = lax.broadcasted_iota(jnp.int32, (8, D_MODEL), 0)
        two = jnp.where(rows8 == 0, _colsum(dxgd * xgdf), jnp.where(rows8 == 1, _colsum(dcar * prev), 0.0))
        two = _dotx_r(two, hsum)
        r8 = lax.broadcasted_iota(jnp.int32, (8, 128), 0)
        dllast = _colsum(jnp.where(r8 == 0, two, 0.0)) + _colsum(jnp.where(r8 == 1, two, 0.0)) * jnp.exp(llast)
        rowi = lax.broadcasted_iota(jnp.int32, (CHUNK, 128), 0)
        dl = dl + jnp.where(rowi == CHUNK - 1, dllast, 0.0)
        dadt = _dotx_l(triu_ref[...], dl)
        ddt = dadt * a + hs1
        draw = ddt * _sig(dtraw_ref[...] + dtb_ref[...])
        ddt_ref[...] = draw.astype(BF16)
        acc16_ref[0:1, :] += _colsum(draw)
        acc16_ref[1:2, :] += _colsum(dadt * dt) * a
        dcarry[...] = dcar * jnp.max(_dotx_r(jnp.exp(llast) + jnp.zeros((8, 128), F32), expand), axis=0, keepdims=True) \
            + jnp.concatenate(dprev, axis=1)

        @pl.when(i == nc - 1)
        def _():
            hd = _dotx_r(acc_ref[...], hsum)
            acc16_ref[2:3, :] = _colsum(jnp.where(lax.broadcasted_iota(jnp.int32, (8, 128), 0) == 1, hd, 0.0))

    rev = lambda w, col: pl.BlockSpec((CHUNK, w), lambda i: (nc - 1 - i, col))
    return pl.pallas_call(
        body, name=name, grid=(nc,),
        in_specs=[rev(1024, 0), rev(256, 4), rev(256, 5), rev(128, DT_COL), rev(1024, 0), rev(1024, 0),
                  pl.BlockSpec((1, 128, D_MODEL), lambda i: (nc - 1 - i, 0, 0)), rev(1024, 0),
                  _fix_spec((1, 128)), _fix_spec((1, 128)), _fix_spec((1, D_MODEL)), _fix_spec((1, D_MODEL)),
                  _fix_spec((CHUNK, CHUNK)), _fix_spec((CHUNK, CHUNK)), _fix_spec((128, D_MODEL)),
                  _fix_spec((D_MODEL, 128))],
        out_specs=[rev(D_CONV, 0), rev(128, 0), rev(1024, 0), _fix_spec((8, D_MODEL)), _fix_spec((8, 128))],
        out_shape=[jax.ShapeDtypeStruct((t, D_CONV), F32), jax.ShapeDtypeStruct((t, 128), BF16),
                   jax.ShapeDtypeStruct((t, D_MODEL), BF16), jax.ShapeDtypeStruct((8, D_MODEL), F32),
                   jax.ShapeDtypeStruct((8, 128), F32)],
        scratch_shapes=[pltpu.VMEM((128, D_MODEL), F32), pltpu.VMEM((128, 128), F32), pltpu.VMEM((128, 128), F32),
                        pltpu.VMEM((CHUNK, D_MODEL), F32)],
        compiler_params=_cp(("arbitrary",)),
    )(u, u, u, proj, proj, y, prev, dycat, dtb, alog, dsk, nw,
      cst["tril"], cst["triu"], cst["expand"], cst["hsum"])


def _head_rms(v, hsum, expand):
    ms = _dotx_r(v * v, hsum, n=2) * (1.0 / HEAD_DIM)
    return _dotx_r(lax.rsqrt(ms + EPS), expand, n=2)


def _qk_fwd(proj, qw, kw, cst, name):
    t = proj.shape[0]
    tm = min(t, 256)
    scale = HEAD_DIM ** -0.5

    def body(q_ref, k_ref, v_ref, qw_ref, kw_ref, hs_ref, exp_ref, qs_ref, kn_ref, vb_ref):
        hsum, expand = hs_ref[...], exp_ref[...]
        q, k = q_ref[...], k_ref[...]
        qs_ref[...] = (q * _head_rms(q, hsum, expand) * qw_ref[...] * scale).astype(BF16)
        kn_ref[...] = (k * _head_rms(k, hsum, expand) * kw_ref[...]).astype(BF16)
        vb_ref[...] = v_ref[...].astype(BF16)

    return pl.pallas_call(
        body, name=name, grid=(t // tm,),
        in_specs=[_row_spec(tm, col=1), _row_spec(tm, col=2), _row_spec(tm, col=3),
                  _fix_spec((1, D_MODEL)), _fix_spec((1, D_MODEL)), _fix_spec((D_MODEL, 128)),
                  _fix_spec((128, D_MODEL))],
        out_specs=[_row_spec(tm)] * 3, out_shape=[jax.ShapeDtypeStruct((t, D_MODEL), BF16)] * 3,
        compiler_params=_cp(("parallel",)),
    )(proj, proj, proj, qw, kw, cst["hsum"], cst["expand"])


def _qk_bwd(proj, dqs, dkn, dv, qw, kw, cst, name):
    t = proj.shape[0]
    tm = min(t, 256)
    scale = HEAD_DIM ** -0.5

    def body(q_ref, k_ref, dq_ref, dk_ref, dv_ref, qw_ref, kw_ref, hs_ref, exp_ref, fold_ref,
             oq_ref, ok_ref, ov_ref, dw_ref):
        i = pl.program_id(0)
        hsum, expand = hs_ref[...], exp_ref[...]
        rows8 = lax.broadcasted_iota(jnp.int32, (8, D_MODEL), 0)
        sums = jnp.zeros((8, D_MODEL), F32)
        for n, (x_ref, d_ref, w_ref, o_ref, sc) in enumerate(
                [(q_ref, dq_ref, qw_ref, oq_ref, scale), (k_ref, dk_ref, kw_ref, ok_ref, 1.0)]):
            xv = x_ref[...]
            r = _head_rms(xv, hsum, expand)
            xhat = xv * r
            dn = d_ref[...] * sc
            sums = sums + jnp.where(rows8 == n, _colsum(dn * xhat), 0.0)
            dw = dn * w_ref[...]
            mean = _dotx_r(_dotx_r(dw * xhat, hsum, n=2), expand, n=2) * (1.0 / HEAD_DIM)
            o_ref[...] = (r * (dw - xhat * mean)).astype(BF16)
        ov_ref[...] = dv_ref[...].astype(BF16)
        folded = _dotx_r(sums, fold_ref[...])

        @pl.when(i == 0)
        def _():
            dw_ref[...] = folded

        @pl.when(i > 0)
        def _():
            dw_ref[...] += folded

    return pl.pallas_call(
        body, name=name, grid=(t // tm,),
        in_specs=[_row_spec(tm, col=1), _row_spec(tm, col=2), _row_spec(tm), _row_spec(tm), _row_spec(tm),
                  _fix_spec((1, D_MODEL)), _fix_spec((1, D_MODEL)), _fix_spec((D_MODEL, 128)),
                  _fix_spec((128, D_MODEL)), _fix_spec((D_MODEL, 128))],
        out_specs=[_row_spec(tm)] * 3 + [_fix_spec((8, 128))],
        out_shape=[jax.ShapeDtypeStruct((t, D_MODEL), BF16)] * 3 + [jax.ShapeDtypeStruct((8, 128), F32)],
        compiler_params=_cp(("arbitrary",)),
    )(proj, proj, dqs, dkn, dv, qw, kw, cst["hsum"], cst["expand"], cst["fold"])


def _sb_masks(i, kb, tq, tk):
    tpos = i * tq + lax.broadcasted_iota(jnp.int32, (tq, 1), 0)
    spos = kb * tk + lax.broadcasted_iota(jnp.int32, (1, tk), 1)
    return spos < tpos


def _grid_marks(n0, n1):
    j, i = pl.program_id(0), pl.program_id(1)
    return (jnp.logical_and(j == 0, i == 0), jnp.logical_and(j == n0 // 2, i == 0),
            jnp.logical_and(j == n0 - 1, i == n1 - 1))


def _sb_fwd(qs, kn, vb, pack, cst, name):
    t = qs.shape[0]
    tq = tk = min(t, SB_TILE)
    nq = t // tq
    npair = D_MODEL // 128

    def body(q_ref, k_ref, v_ref, u_ref, p_ref, rt_ref, ob_ref, cnt_ref, gat_ref, acc, rs, gss, grs):
        at_first, at_mid, at_last = _grid_marks(npair, nq)
        g_start, g_relay, g_finish = _gather_stages(p_ref, gat_ref, gss, grs)
        pl.when(at_first)(g_start)
        pl.when(at_mid)(g_relay)
        i = pl.program_id(1)
        lane = lax.broadcasted_iota(jnp.int32, (1, 128), 1)
        q2 = q_ref[...]
        zero = jnp.zeros_like(q2)
        qh = [jnp.where(lane < HEAD_DIM, q2, zero), jnp.where(lane >= HEAD_DIM, q2, zero)]
        acc[...] = jnp.zeros_like(acc)
        rs[...] = jnp.zeros_like(rs)
        ustrict = u_ref[...]

        def tile(kb, masked):
            off = pl.multiple_of(kb * tk, tk)
            k2 = k_ref[pl.ds(off, tk), :]
            v2 = v_ref[pl.ds(off, tk), :]
            strict = _sb_masks(i, kb, tq, tk) if masked else None
            s = [_dot(qh[h], k2, "nt") for h in range(2)]
            a, r, lb = [None] * 2, [None] * 2, [None] * 2
            for h in range(2):
                sp = _softplus(s[h])
                a[h] = s[h] - sp
                r[h] = jnp.where(strict, -sp, 0.0) if masked else -sp
                lb[h] = _dot2(r[h], ustrict)
            for h in range(2):
                lw = a[h] + lb[h] + rs[h]
                w = jnp.exp(jnp.where(strict, lw, -1e30) if masked else lw)
                rs[h] = rs[h] + jnp.sum(r[h], axis=1, keepdims=True)
                acc[h] = acc[h] + _dot(w.astype(BF16), v2)

        tile(i, True)

        def live():
            return jnp.max(jnp.maximum(rs[0], rs[1]))

        def more(c):
            return jnp.logical_and(c[0] < i, c[1] > SB_DEAD)

        def step(c):
            tile(i - 1 - c[0], False)
            return c[0] + 1, live()

        n_off, _ = lax.while_loop(more, step, (jnp.int32(0), live()))
        cnt_ref[pl.program_id(0), i] = n_off.astype(F32)
        rt_ref[...] = jnp.where(lane < HEAD_DIM, rs[0], rs[1])
        ob_ref[...] = jnp.where(lane < HEAD_DIM, acc[0], acc[1]).astype(BF16)
        pl.when(at_last)(g_finish)

    return pl.pallas_call(
        body, name=name, grid=(npair, nq),
        in_specs=[pl.BlockSpec((tq, 128), lambda j, i: (i, j)), pl.BlockSpec((t, 128), lambda j, i: (0, j)),
                  pl.BlockSpec((t, 128), lambda j, i: (0, j)), _fix_spec((2 * tk, tk)), HB],
        out_specs=[pl.BlockSpec((tq, 128), lambda j, i: (i, j))] * 2 + [pl.BlockSpec(memory_space=pltpu.SMEM), HB],
        out_shape=[jax.ShapeDtypeStruct((t, D_MODEL), F32), jax.ShapeDtypeStruct((t, D_MODEL), BF16),
                   jax.ShapeDtypeStruct((npair, nq), F32),
                   jax.ShapeDtypeStruct((N_SHARDS,) + pack.shape, pack.dtype)],
        scratch_shapes=[pltpu.VMEM((2, tq, 128), F32), pltpu.VMEM((2, tq, 1), F32)] + GATHER_SEMS,
        compiler_params=_cp(("arbitrary", "arbitrary")),
    )(qs, kn, vb, _doubled(cst["ustrict"], tk), pack)


def _sb_bwd(qs, kn, vb, rtot, cnt, dycat, csum_b, cst, name):
    t = qs.shape[0]
    tq = tk = min(t, SB_TILE)
    nq = t // tq
    npair = D_MODEL // 128

    def body(q_ref, k_ref, v_ref, rt_ref, do_ref, us_ref, ui_ref, cnt_ref, xs_ref, dq_ref, dk_ref, dv_ref, xr_ref,
             acc, rs, es, xss, xrs):
        at_first, _, at_last = _grid_marks(npair, nq)
        x_start, x_finish = _exchange_stages(xs_ref, xr_ref, xss, xrs)
        pl.when(at_first)(x_start)
        i = pl.program_id(1)
        lane = lax.broadcasted_iota(jnp.int32, (1, 128), 1)
        keep = [lane < HEAD_DIM, lane >= HEAD_DIM]
        q2, rt = q_ref[...], rt_ref[...]
        do2b = do_ref[...].astype(BF16)
        qh = [jnp.where(kp, q2, jnp.zeros_like(q2)) for kp in keep]
        doh = [jnp.where(kp, do2b, jnp.zeros_like(do2b)) for kp in keep]
        rtot_h = [jnp.sum(jnp.where(lane == n * HEAD_DIM, rt, 0.0), axis=1, keepdims=True) for n in range(2)]
        acc[...] = jnp.zeros_like(acc)
        rs[...] = jnp.zeros_like(rs)
        es[...] = jnp.zeros_like(es)

        @pl.when(i == 0)
        def _():
            dk_ref[...] = jnp.zeros_like(dk_ref)
            dv_ref[...] = jnp.zeros_like(dv_ref)

        ule, ult = us_ref[...], ui_ref[...]

        def tile(kb, masked):
            off = pl.multiple_of(kb * tk, tk)
            k2 = k_ref[pl.ds(off, tk), :]
            v2 = v_ref[pl.ds(off, tk), :]
            strict = _sb_masks(i, kb, tq, tk) if masked else None
            s = [_dot(qh[h], k2, "nt") for h in range(2)]
            dw = [_dot(doh[h], v2, "nt") for h in range(2)]
            a, sg, r, pin, w, e, cin = ([None] * 2 for _ in range(7))
            for h in range(2):
                sp = _softplus(s[h])
                a[h] = s[h] - sp
                sg[h] = jnp.exp(a[h])
                r[h] = jnp.where(strict, -sp, 0.0) if masked else -sp
                pin[h] = _dot2(r[h], ule)
            for h in range(2):
                lw = a[h] + ((rtot_h[h] - rs[h]) - pin[h])
                w[h] = jnp.exp(jnp.where(strict, lw, -1e30) if masked else lw)
                e[h] = w[h] * dw[h]
                cin[h] = _dot2(e[h], ult)
            dk_t = jnp.zeros((tk, 128), F32)
            dv_t = jnp.zeros((tk, 128), F32)
            for h in range(2):
                dl = e[h] * (1.0 - sg[h]) - (es[h] + cin[h]) * sg[h]
                dl = (jnp.where(strict, dl, 0.0) if masked else dl).astype(BF16)
                rs[h] = rs[h] + jnp.sum(r[h], axis=1, keepdims=True)
                es[h] = es[h] + jnp.sum(e[h], axis=1, keepdims=True)
                acc[h] = acc[h] + _dot(dl, k2)
                dk_t = dk_t + _dot(dl, qh[h], "tn")
                dv_t = dv_t + _dot(w[h].astype(BF16), doh[h], "tn")
            dk_ref[pl.ds(off, tk), :] += dk_t
            dv_ref[pl.ds(off, tk), :] += dv_t

        def step(kb, carry):
            tile(kb, False)
            return carry

        n_off = cnt_ref[pl.program_id(0), i].astype(jnp.int32)
        lax.fori_loop(i - n_off, i, step, 0)
        tile(i, True)
        dq_ref[...] = jnp.where(lane < HEAD_DIM, acc[0], acc[1])
        pl.when(at_last)(x_finish)

    return pl.pallas_call(
        body, name=name, grid=(npair, nq),
        in_specs=[pl.BlockSpec((tq, 128), lambda j, i: (i, j)), pl.BlockSpec((t, 128), lambda j, i: (0, j)),
                  pl.BlockSpec((t, 128), lambda j, i: (0, j)), pl.BlockSpec((tq, 128), lambda j, i: (i, j)),
                  pl.BlockSpec((tq, 128), lambda j, i: (i, npair + j)),
                  _fix_spec((2 * tk, tk)), _fix_spec((2 * tk, tk)), pl.BlockSpec(memory_space=pltpu.SMEM), HB],
        out_specs=[pl.BlockSpec((tq, 128), lambda j, i: (i, j)), pl.BlockSpec((t, 128), lambda j, i: (0, j)),
                   pl.BlockSpec((t, 128), lambda j, i: (0, j)), HB],
        out_shape=[jax.ShapeDtypeStruct((t, D_MODEL), F32)] * 3
        + [jax.ShapeDtypeStruct((3,) + csum_b.shape[1:], csum_b.dtype)],
        scratch_shapes=[pltpu.VMEM((2, tq, 128), F32), pltpu.VMEM((2, tq, 1), F32), pltpu.VMEM((2, tq, 1), F32)]
        + EXCHANGE_SEMS,
        compiler_params=_cp(("arbitrary", "arbitrary")),
    )(qs, kn, vb, rtot, dycat, _doubled(cst["ule"], tk), _doubled(cst["ult"], tk), cnt, csum_b)


def _adamw(w, g, m, v, name):
    lead = (1,) * (w.ndim - 2)
    rows, cols = w.shape[-2:]
    fits = [d for d in range(8, rows, 8) if rows % d == 0 and d * cols * 4 <= ADAM_BLOCK_BYTES]
    tr = max(fits) if fits else rows
    c1 = 1.0 - ADAM_B1 ** ADAM_STEP
    c2 = 1.0 - ADAM_B2 ** ADAM_STEP

    def body(w_ref, g_ref, m_ref, v_ref, d_ref, nm_ref, nv_ref):
        gv = g_ref[...]
        nm = ADAM_B1 * m_ref[...] + (1.0 - ADAM_B1) * gv
        nv = ADAM_B2 * v_ref[...] + (1.0 - ADAM_B2) * (gv * gv)
        nm_ref[...] = nm
        nv_ref[...] = nv
        d_ref[...] = -ADAM_LR * ((nm / c1) / (jnp.sqrt(nv / c2) + ADAM_EPS) + ADAM_WD * w_ref[...])

    spec = pl.BlockSpec(lead + (tr, cols), lambda i: (0,) * len(lead) + (i, 0))
    return pl.pallas_call(
        body, name=name, grid=(rows // tr,), in_specs=[spec] * 4, out_specs=[spec] * 3,
        out_shape=[jax.ShapeDtypeStruct(w.shape, F32)] * 3, compiler_params=_cp(("parallel",)),
    )(w, g, m, v)


def _place():
    x, y, c = lax.axis_index("x"), lax.axis_index("y"), lax.axis_index("c")
    chips = [(1 - x, y), (x, 1 - y), (1 - x, 1 - y)]
    return x, y, c, chips


VM = pl.BlockSpec(memory_space=pltpu.VMEM)
HB = pl.BlockSpec(memory_space=pltpu.HBM)


def _small_gather(pack, name, reduce):
    rows = pack.shape[0]

    def body(p_ref, gat_ref, *rest):
        if reduce:
            sum_ref, ss, rs = rest
        else:
            ss, rs = rest
        x, y, c, _ = _place()
        me = 4 * x + 2 * y + c
        peers = [(x, y, 1 - c), (1 - x, y, c), (x, 1 - y, c), (1 - x, 1 - y, c),
                 (1 - x, y, 1 - c), (x, 1 - y, 1 - c), (1 - x, 1 - y, 1 - c)]

        def copy(k, slot, to):
            return pltpu.make_async_remote_copy(src_ref=p_ref, dst_ref=gat_ref.at[slot], send_sem=ss.at[k],
                                                recv_sem=rs.at[k], device_id=to, device_id_type=MESH)

        sends = [copy(k, me, p) for k, p in enumerate(peers)]
        for s in sends:
            s.start()
        gat_ref[me] = p_ref[...]
        for k, p in enumerate(peers):
            copy(k, 4 * p[0] + 2 * p[1] + p[2], p).wait_recv()
        for s in sends:
            s.wait_send()
        if reduce:
            tot = gat_ref[0]
            for b in range(1, 8):
                tot = tot + gat_ref[b]
            sum_ref[...] = tot

    out_shape = [jax.ShapeDtypeStruct((8, rows, D_MODEL), F32)]
    if reduce:
        out_shape.append(jax.ShapeDtypeStruct((rows, D_MODEL), F32))
    return pl.pallas_call(
        body, name=name, in_specs=[VM], out_specs=[VM] * len(out_shape), out_shape=out_shape,
        scratch_shapes=[pltpu.SemaphoreType.DMA((7,)), pltpu.SemaphoreType.DMA((7,))],
        compiler_params=_cp(),
    )(pack)


def _mod_exchange(cond, w_ada, b_shard, name):
    def body(c_ref, w_ref, b_ref, modp_ref, ss, rs):
        x, y, c, chips = _place()
        sh = 2 * x + y
        cv = c_ref[...]
        cv = cv * _sig(cv)
        modp_ref[sh] = jnp.dot(cv, w_ref[...], precision=lax.Precision.HIGHEST,
                               preferred_element_type=F32) + b_ref[...]

        def copy(k, slot, to):
            return pltpu.make_async_remote_copy(src_ref=modp_ref.at[slot], dst_ref=modp_ref.at[slot],
                                                send_sem=ss.at[k], recv_sem=rs.at[k], device_id=to,
                                                device_id_type=MESH)

        sends = [copy(k, sh, (*ch, c)) for k, ch in enumerate(chips)]
        for s in sends:
            s.start()
        for k, ch in enumerate(chips):
            copy(k, 2 * ch[0] + ch[1], (*ch, c)).wait_recv()
        for s in sends:
            s.wait_send()

    return pl.pallas_call(
        body, name=name, in_specs=[VM, VM, VM], out_specs=VM,
        out_shape=jax.ShapeDtypeStruct((N_SHARDS, 8, 6 * D_MODEL // N_SHARDS), F32),
        scratch_shapes=[pltpu.SemaphoreType.DMA((3,)), pltpu.SemaphoreType.DMA((3,))],
        compiler_params=_cp(),
    )(cond, w_ada, b_shard)


def _gather_stages(p_ref, out_ref, ss, rs):
    hf = p_ref.shape[0] // 2
    x, y, c, chips = _place()
    sh = 2 * x + y
    sib = (x, y, 1 - c)
    slots = [2 * ch[0] + ch[1] for ch in chips]

    def half(slot, hc):
        return out_ref.at[slot, pl.ds(hc * hf, hf), :]

    def copy(k, src, slot, hc, to):
        return pltpu.make_async_remote_copy(src_ref=src, dst_ref=half(slot, hc), send_sem=ss.at[k],
                                            recv_sem=rs.at[k], device_id=to, device_id_type=MESH)

    def first():
        return [copy(j, p_ref.at[pl.ds(c * hf, hf), :], sh, c, (*ch, c)) for j, ch in enumerate(chips)]

    def passed():
        return [copy(3 + j, half(slots[j], c), slots[j], c, sib) for j in range(3)]

    def start():
        for cp in first():
            cp.start()

    def relay():
        for j, cp in enumerate(passed()):
            copy(j, half(slots[j], c), slots[j], c, (*chips[j], c)).wait_recv()
            cp.start()

    def finish():
        for j in range(3):
            copy(3 + j, half(slots[j], 1 - c), slots[j], 1 - c, sib).wait_recv()
        for cp in first() + passed():
            cp.wait_send()

    return start, relay, finish


GATHER_SEMS = [pltpu.SemaphoreType.DMA((6,)), pltpu.SemaphoreType.DMA((6,))]


def _gather_weights(pack, name):
    def body(p_ref, out_ref, ss, rs):
        for stage in _gather_stages(p_ref, out_ref, ss, rs):
            stage()

    return pl.pallas_call(
        body, name=name, in_specs=[HB], out_specs=HB,
        out_shape=jax.ShapeDtypeStruct((N_SHARDS,) + pack.shape, pack.dtype),
        scratch_shapes=GATHER_SEMS, compiler_params=_cp(),
    )(pack)


def _sibling_swap(g, name):
    hf = g.shape[1] // 2

    def body(g_ref, out_ref, ss, rs):
        x, y, c, _ = _place()
        cp = pltpu.make_async_remote_copy(
            src_ref=g_ref.at[pl.ds(0, N_SHARDS), pl.ds((1 - c) * hf, hf), :], dst_ref=out_ref,
            send_sem=ss, recv_sem=rs, device_id=(x, y, 1 - c), device_id_type=MESH)
        cp.start()
        cp.wait()

    return pl.pallas_call(
        body, name=name, in_specs=[HB], out_specs=HB,
        out_shape=jax.ShapeDtypeStruct((N_SHARDS, hf, D_MODEL), g.dtype),
        scratch_shapes=[pltpu.SemaphoreType.DMA, pltpu.SemaphoreType.DMA],
        compiler_params=_cp(),
    )(g)


def _row_tile(rows, width_bytes, cap_bytes):
    fits = [d for d in range(8, rows + 1, 8) if rows % d == 0 and d * width_bytes <= cap_bytes]
    return max(fits)


def _chip_sum(g, got, c_idx, name):
    hf = got.shape[1]
    tr = _row_tile(hf, D_MODEL * 4, 3 << 20)
    nb = hf // tr

    def body(c_ref, a_ref, b_ref, s_ref, sb_ref):
        s = a_ref[...] + b_ref[...]
        s_ref[...] = s
        sb_ref[...] = s.astype(BF16)

    blk = pl.BlockSpec((1, tr, D_MODEL), lambda s, i, c_ref: (s, i, 0))
    return pl.pallas_call(
        body, name=name,
        grid_spec=pltpu.PrefetchScalarGridSpec(
            num_scalar_prefetch=1, grid=(N_SHARDS, nb),
            in_specs=[pl.BlockSpec((1, tr, D_MODEL), lambda s, i, c_ref: (s, c_ref[0] * nb + i, 0)), blk],
            out_specs=[blk, blk]),
        out_shape=[jax.ShapeDtypeStruct((N_SHARDS, hf, D_MODEL), F32),
                   jax.ShapeDtypeStruct((N_SHARDS, hf, D_MODEL), BF16)],
        compiler_params=_cp(("parallel", "parallel")),
    )(c_idx, g, got)


def _exchange_stages(s_ref, out_ref, ss, rs):
    x, y, c, chips = _place()

    def sends():
        return [pltpu.make_async_remote_copy(src_ref=s_ref.at[2 * ch[0] + ch[1]], dst_ref=out_ref.at[k],
                                             send_sem=ss.at[k], recv_sem=rs.at[k], device_id=(*ch, c),
                                             device_id_type=MESH) for k, ch in enumerate(chips)]

    def start():
        for cp in sends():
            cp.start()

    def finish():
        for cp in sends():
            cp.wait()

    return start, finish


EXCHANGE_SEMS = [pltpu.SemaphoreType.DMA((3,)), pltpu.SemaphoreType.DMA((3,))]


def _chip_exchange(sb, name):
    def body(s_ref, out_ref, ss, rs):
        for stage in _exchange_stages(s_ref, out_ref, ss, rs):
            stage()

    return pl.pallas_call(
        body, name=name, in_specs=[HB], out_specs=HB,
        out_shape=jax.ShapeDtypeStruct((3,) + sb.shape[1:], sb.dtype),
        scratch_shapes=EXCHANGE_SEMS, compiler_params=_cp(),
    )(sb)


def _total_half(s, got, sh_idx, name):
    hf = got.shape[1]
    tr = _row_tile(hf, D_MODEL * 4, 3 << 20)
    nb = hf // tr

    def body(sh_ref, a_ref, r0, r1, r2, o_ref):
        o_ref[...] = ((a_ref[0] + r0[0].astype(F32)) + r1[0].astype(F32)) + r2[0].astype(F32)

    rspec = lambda k: pl.BlockSpec((1, tr, D_MODEL), lambda i, sh_ref: (k, i, 0))
    return pl.pallas_call(
        body, name=name,
        grid_spec=pltpu.PrefetchScalarGridSpec(
            num_scalar_prefetch=1, grid=(nb,),
            in_specs=[pl.BlockSpec((1, tr, D_MODEL), lambda i, sh_ref: (sh_ref[0], i, 0)),
                      rspec(0), rspec(1), rspec(2)],
            out_specs=pl.BlockSpec((tr, D_MODEL), lambda i, sh_ref: (i, 0))),
        out_shape=jax.ShapeDtypeStruct((hf, D_MODEL), F32),
        compiler_params=_cp(("parallel",)),
    )(sh_idx, s, got, got, got)


def _join_halves(tot, name):
    def body(t_ref, out_ref, ss, rs):
        x, y, c, _ = _place()
        cp = pltpu.make_async_remote_copy(src_ref=t_ref, dst_ref=out_ref, send_sem=ss, recv_sem=rs,
                                          device_id=(x, y, 1 - c), device_id_type=MESH)
        cp.start()
        cp.wait()

    return pl.pallas_call(
        body, name=name, in_specs=[HB], out_specs=HB,
        out_shape=jax.ShapeDtypeStruct(tot.shape, F32),
        scratch_shapes=[pltpu.SemaphoreType.DMA, pltpu.SemaphoreType.DMA],
        compiler_params=_cp(),
    )(tot)


def _w_ada_grad(cond, dmod_cols, name):
    def body(c_ref, d_ref, o_ref):
        cv = c_ref[...]
        cv = cv * _sig(cv)
        o_ref[...] = lax.dot_general(cv, d_ref[...], _DN["tn"], precision=lax.Precision.HIGHEST,
                                     preferred_element_type=F32)

    return pl.pallas_call(
        body, name=name, in_specs=[VM, VM], out_specs=VM,
        out_shape=jax.ShapeDtypeStruct((D_MODEL, dmod_cols.shape[1]), F32), compiler_params=_cp(),
    )(cond, dmod_cols)


def _pad_rows(a, rows):
    return jnp.pad(a, ((0, rows - a.shape[0]), (0, 0)))


def _pad_cols(a, cols):
    return jnp.pad(a, ((0, 0), (0, cols - a.shape[1])))


def _unpack_rest(p):
    o = 0
    out = []
    for r in (R_OUT, R_FF, R_FF, R_FF):
        out.append(p[..., o:o + r, :])
        o += r
    return out


def _reduce_tail(csum, got2, shard, ac, tag):
    tot = _total_half(csum, got2, shard.reshape(1).astype(jnp.int32), "rs_total_" + tag)
    other = _join_halves(tot, "rs_join_" + tag)
    return jnp.where(ac == 0, jnp.concatenate([tot, other], axis=0), jnp.concatenate([other, tot], axis=0))


def _reduce_head(gpack, ac, tag):
    got = _sibling_swap(gpack, "rs_sibling_swap_" + tag)
    return _chip_sum(gpack, got, ac.reshape(1).astype(jnp.int32), "rs_chip_sum_" + tag)


def kernel(x, c, w_ada, b_ada, norm1_w, w_in, conv_w, conv_b, dt_bias, a_log, d_skip, ssd_norm_w, q_norm_w, k_norm_w, w_out, norm2_w, w_gate, w_up, w_down, loss_target, m_w_ada, m_b_ada, m_norm1_w, m_w_in, m_conv_w, m_conv_b, m_dt_bias, m_a_log, m_d_skip, m_ssd_norm_w, m_q_norm_w, m_k_norm_w, m_w_out, m_norm2_w, m_w_gate, m_w_up, m_w_down, v_w_ada, v_b_ada, v_norm1_w, v_w_in, v_conv_w, v_conv_b, v_dt_bias, v_a_log, v_d_skip, v_ssd_norm_w, v_q_norm_w, v_k_norm_w, v_w_out, v_norm2_w, v_w_gate, v_w_up, v_w_down):
    cst = _consts()
    ax, ay, ac = lax.axis_index("x"), lax.axis_index("y"), lax.axis_index("c")
    shard = 2 * ax + ay
    me = 4 * ax + 2 * ay + ac
    xs = x[0]
    tgt = loss_target[0]
    w_in_cols = w_in.shape[2]
    conv_cols = conv_w.shape[2]

    cw_flat = _pad_cols(conv_w[0].reshape(1, -1), 2 * D_MODEL).reshape(2, D_MODEL)
    cpack = jnp.concatenate([jnp.broadcast_to(c, (8, D_MODEL)), _pad_rows(cw_flat, 8)], axis=0)
    gat = _small_gather(cpack, "gather_c", reduce=False)[0]
    c_all = gat[:, 0, :]
    cw = gat[0::2, 8:10, :].reshape(N_SHARDS, 2 * D_MODEL)[:, :4 * conv_cols].reshape(N_SHARDS, 4, conv_cols)
    conv_w_full = jnp.transpose(cw, (1, 0, 2)).reshape(4, D_CONV)

    mod_w = 6 * D_MODEL // N_SHARDS
    b_shard = lax.dynamic_slice(b_ada, (0, shard * mod_w), (1, mod_w))
    modp = _mod_exchange(c_all, w_ada[0], b_shard, "mod_exchange")
    mod_mine = lax.dynamic_slice(modp, (0, me, 0), (N_SHARDS, 1, mod_w)).reshape(6, D_MODEL)
    mod = _pad_rows(mod_mine, 8)

    tr3 = lambda a: jnp.transpose(a, (0, 2, 1))
    lin = lambda a: tr3(a).reshape(-1, 128)
    unlin = lambda a: tr3(a.reshape(1, w_in_cols, D_MODEL))
    wpack_in = _pad_rows(tr3(w_in.astype(BF16))[0], R_IN)
    wpack_rest = jnp.concatenate([w_out[0], tr3(w_gate)[0], tr3(w_up)[0], w_down[0]], axis=0).astype(BF16)
    own = lax.broadcasted_iota(jnp.int32, (N_SHARDS, 1, 1), 0) == shard
    p_in = jnp.where(own, wpack_in[None], _gather_weights(wpack_in, "gather_w_in"))
    wi_t = p_in[:, :w_in_cols, :].reshape(D_IN_PROJ, D_MODEL)
    w_inp_t = jnp.concatenate([wi_t[0:1024], wi_t[2576:5648], wi_t[1024:2560], wi_t[2560:2576],
                               jnp.zeros((112, D_MODEL), BF16)], axis=0)

    pad128 = lambda a: _pad_cols(a, 128)
    dtb, alog = pad128(dt_bias), pad128(a_log)
    dsk = jnp.repeat(d_skip, HEAD_DIM, axis=1)
    qw, kw = jnp.tile(q_norm_w, (1, N_HEADS)), jnp.tile(k_norm_w, (1, N_HEADS))

    h1 = _norm_mod(xs, norm1_w, mod, 0, "norm1")
    proj = _matmul(h1, w_inp_t, "nt", F32, "in_proj")
    u = _conv_fwd(proj, conv_w_full, conv_b, "conv_fwd")
    y_ssd, yn, prev = _ssd_fwd(u, proj, dtb, alog, dsk, ssd_norm_w, cst, "ssd_fwd")
    qs, kn, vb = _qk_fwd(proj, qw, kw, cst, "qk_norm")
    rtot, ob, cnt, gp_rest = _sb_fwd(qs, kn, vb, wpack_rest, cst, "sb_fwd")
    p_out, p_gate, p_up, p_down = _unpack_rest(jnp.where(own, wpack_rest[None], gp_rest))
    w_o = p_out.reshape(2 * D_MODEL, D_MODEL)
    w_gu_t = jnp.concatenate([p_gate.reshape(D_FF, D_MODEL), p_up.reshape(D_FF, D_MODEL)], axis=0)
    w_d = p_down.reshape(D_FF, D_MODEL)
    ycat = jnp.concatenate([yn, ob], axis=1)
    mix = _matmul(ycat, w_o, "nn", F32, "out_proj")
    x1, h2 = _resid_norm(xs, mix, norm2_w, mod, "resid_norm2")
    gu = _matmul(h2, w_gu_t, "nt", BF16, "ffn_in")
    act = _act_fwd(gu, "ffn_act")
    ffn = _matmul(act, w_d, "nn", F32, "ffn_out", tk_cap=1408)
    dffn, dout, dg2, loss8 = _loss_head(x1, ffn, tgt, mod, "loss_head")
    loss = lax.psum(loss8[0, 0], ("x", "y", "c"))

    dact = _matmul(dffn, w_d, "nt", BF16, "d_act")
    g_down = _matmul(act, dffn, "tn", F32, "g_w_down", tm_cap=1408)
    dgu = _act_bwd(dact, gu, "ffn_act_bwd")
    dh2 = _matmul(dgu, w_gu_t, "nn", F32, "d_h2", tk_cap=1408)
    g_gu_t = _matmul(dgu, h2, "tn", F32, "g_w_gu", tm_cap=1408)
    dx1, dmix, acc2 = _norm_bwd(dh2, x1, dout, mix, norm2_w, mod, 3, 2, "norm2_bwd")
    dycat = _matmul(dmix, w_o, "nt", F32, "d_ycat")
    g_out = _matmul(ycat, dmix, "tn", F32, "g_w_out")
    gpack_rest = jnp.concatenate([
        g_out.reshape(N_SHARDS, R_OUT, D_MODEL),
        g_gu_t[:D_FF].reshape(N_SHARDS, R_FF, D_MODEL), g_gu_t[D_FF:].reshape(N_SHARDS, R_FF, D_MODEL),
        g_down.reshape(N_SHARDS, R_FF, D_MODEL)], axis=1)
    csum_r, csum_rb = _reduce_head(gpack_rest, ac, "rest")
    du, ddt, dz, acc_ssd, acc16 = _ssd_bwd(u, proj, y_ssd, prev, dycat, dtb, alog, dsk, ssd_norm_w, cst, "ssd_bwd")
    dqs, dkn, dv, got_r = _sb_bwd(qs, kn, vb, rtot, cnt, dycat, csum_rb, cst, "sb_bwd")
    r_out, r_gate, r_up, r_down = _unpack_rest(_reduce_tail(csum_r, got_r, shard, ac, "rest"))
    dq, dk, dvb, acc_qk = _qk_bwd(proj, dqs, dkn, dv, qw, kw, cst, "qk_norm_bwd")
    dxbc, g_conv_w, g_conv_b = _conv_bwd(proj, du, conv_w_full, conv_b, "conv_bwd")
    dproj = jnp.concatenate([dz, dq, dk, dvb, dxbc, ddt], axis=1)
    dh1 = _matmul(dproj, w_inp_t, "nn", F32, "d_h1", tk_cap=1152)
    g_inp_t = _matmul(dproj, h1, "tn", F32, "g_w_in", tm_cap=1920)
    grad_x, acc1 = _norm_bwd(dh1, xs, dx1, None, norm1_w, mod, 0, None, "norm1_bwd")

    last = jnp.concatenate([acc_qk[0:1, 0:64], acc_qk[1:2, 0:64], acc16[0:1, 0:16], acc16[1:2, 0:16],
                            acc16[2:3, 0:16]], axis=1)
    spack = jnp.concatenate([
        acc1[0:2], acc2[3:4], acc2[0:2], dg2,
        acc1[2:3], acc2[2:3], acc_ssd[0:1],
        _pad_cols(g_conv_b, 2 * D_MODEL).reshape(2, D_MODEL),
        g_conv_w.reshape(6, D_MODEL),
        _pad_cols(last, D_MODEL)], axis=0)
    sgat, ssum = _small_gather(_pad_rows(spack, SMALL_ROWS), "gather_small", reduce=True)
    g_b_ada = ssum[0:6].reshape(1, 6 * D_MODEL)
    g_norm1, g_norm2, g_ssdn = ssum[6:7], ssum[7:8], ssum[8:9]
    g_cb = ssum[9:11].reshape(1, 2 * D_MODEL)[:, :D_CONV]
    g_cw = lax.dynamic_slice(ssum[11:17].reshape(4, D_CONV), (0, shard * conv_cols), (4, conv_cols))
    g_qn, g_kn = ssum[17:18, 0:64], ssum[17:18, 64:128]
    g_dtb, g_alog, g_dsk = ssum[17:18, 128:144], ssum[17:18, 144:160], ssum[17:18, 160:176]
    dmod_all = sgat[:, 0:6, :].reshape(8, 6 * D_MODEL)
    g_w_ada = _w_ada_grad(c_all, lax.dynamic_slice(dmod_all, (0, shard * mod_w), (8, mod_w)), "g_w_ada")

    gi_t = jnp.concatenate([g_inp_t[0:1024], g_inp_t[4096:5632], g_inp_t[5632:5648], g_inp_t[1024:4096]], axis=0)
    gpack_in = jnp.pad(gi_t.reshape(N_SHARDS, w_in_cols, D_MODEL), ((0, 0), (0, R_IN - w_in_cols), (0, 0)))
    csum_i, csum_ib = _reduce_head(gpack_in, ac, "in")
    r_in = _reduce_tail(csum_i, _chip_exchange(csum_ib, "rs_chip_exchange_in"), shard, ac, "in")

    grads = dict(w_ada=g_w_ada, b_ada=g_b_ada, norm1_w=g_norm1, w_in=r_in[:w_in_cols].reshape(-1, 128), conv_w=g_cw,
                 conv_b=g_cb, dt_bias=g_dtb, a_log=g_alog, d_skip=g_dsk, ssd_norm_w=g_ssdn, q_norm_w=g_qn,
                 k_norm_w=g_kn, w_out=r_out, norm2_w=g_norm2, w_gate=r_gate, w_up=r_up, w_down=r_down)
    weights = dict(w_ada=(w_ada, m_w_ada, v_w_ada), b_ada=(b_ada, m_b_ada, v_b_ada),
                   norm1_w=(norm1_w, m_norm1_w, v_norm1_w), w_in=(w_in, m_w_in, v_w_in),
                   conv_w=(conv_w, m_conv_w, v_conv_w), conv_b=(conv_b, m_conv_b, v_conv_b),
                   dt_bias=(dt_bias, m_dt_bias, v_dt_bias), a_log=(a_log, m_a_log, v_a_log),
                   d_skip=(d_skip, m_d_skip, v_d_skip), ssd_norm_w=(ssd_norm_w, m_ssd_norm_w, v_ssd_norm_w),
                   q_norm_w=(q_norm_w, m_q_norm_w, v_q_norm_w), k_norm_w=(k_norm_w, m_k_norm_w, v_k_norm_w),
                   w_out=(w_out, m_w_out, v_w_out), norm2_w=(norm2_w, m_norm2_w, v_norm2_w),
                   w_gate=(w_gate, m_w_gate, v_w_gate), w_up=(w_up, m_w_up, v_w_up),
                   w_down=(w_down, m_w_down, v_w_down))
    views = dict(w_in=(lin, unlin), w_gate=(tr3, tr3), w_up=(tr3, tr3))
    same = lambda a: a
    names = list(weights)
    g_out_l, d_out_l, m_out_l, v_out_l = [], [], [], []
    for n in names:
        view, back = views.get(n, (same, same))
        w, m, v = (view(a) for a in weights[n])
        g = grads[n].reshape(w.shape)
        d, nm, nv = _adamw(w, g, m, v, "adamw_" + n)
        g_out_l.append(back(g))
        d_out_l.append(back(d))
        m_out_l.append(back(nm))
        v_out_l.append(back(nv))
    return (loss, grad_x[None], *g_out_l, *d_out_l, *m_out_l, *v_out_l)
```

```python
import functools

import numpy as np
import jax
import jax.numpy as jnp
from jax import lax
from jax.experimental import pallas as pl
from jax.experimental.pallas import tpu as pltpu

F32, BF16 = jnp.float32, jnp.bfloat16
MESH = pl.DeviceIdType.MESH

D_MODEL = 1024
HEAD_DIM = 64
N_HEADS = 16
D_CONV = 1536
D_FF = 2816
D_IN_PROJ = 5648
D_PROJ_PAD = 5760
CHUNK = 128
SB_TILE = 256
SB_DEAD = -105.0
EPS = 1e-6
N_SHARDS = 4
R_IN, R_OUT, R_FF = 1440, 512, 704
SMALL_ROWS = 24

ADAM_LR, ADAM_B1, ADAM_B2, ADAM_EPS, ADAM_WD, ADAM_STEP = 0.001, 0.9, 0.999, 1e-08, 0.01, 10

VMEM_LIMIT = 48 * 1024 * 1024
ADAM_BLOCK_BYTES = 3 * 512 * 1024

_DN = {"nn": (((1,), (0,)), ((), ())), "nt": (((1,), (1,)), ((), ())), "tn": (((0,), (0,)), ((), ()))}


def _dot(a, b, dims="nn"):
    return lax.dot_general(a, b, _DN[dims], preferred_element_type=F32)


def _pieces(x, n):
    out = []
    for _ in range(n - 1):
        hi = lax.bitcast_convert_type(lax.bitcast_convert_type(x, jnp.int32) & jnp.int32(-65536), F32)
        out.append(hi.astype(BF16))
        x = x - hi
    out.append(x.astype(BF16))
    return out


def _dotx_r(x, b_exact, n=3):
    return _dot(jnp.concatenate(_pieces(x, n), axis=1), jnp.concatenate([b_exact] * n, axis=0))


def _dotx_l(a_exact, x, n=3):
    return _dot(jnp.concatenate([a_exact] * n, axis=1), jnp.concatenate(_pieces(x, n), axis=0))


def _dot2(x, b2):
    return _dot(jnp.concatenate(_pieces(x, 2), axis=1), b2)


def _sig(x):
    return 1.0 / (1.0 + jnp.exp(-x))


def _softplus(x):
    return jnp.maximum(x, 0.0) + jnp.log(1.0 + jnp.exp(-jnp.abs(x)))


def _cp(sem=None, vmem=VMEM_LIMIT):
    return pltpu.CompilerParams(dimension_semantics=sem, vmem_limit_bytes=vmem)


def _colsum(x):
    return jnp.sum(x, axis=0, keepdims=True)


def _consts():
    ch = np.arange(D_MODEL)
    expand = (np.arange(128)[:, None] == (ch // HEAD_DIM)[None, :]).astype(np.float32)
    fold = (ch[:, None] % HEAD_DIM == np.arange(128)[None, :]).astype(np.float32)
    i = np.arange(CHUNK)
    tril = (i[:, None] >= i[None, :]).astype(np.float32)
    j = np.arange(SB_TILE)
    ustrict = (j[:, None] > j[None, :]).astype(np.float32)
    ule = (j[:, None] <= j[None, :]).astype(np.float32)
    ult = (j[:, None] < j[None, :]).astype(np.float32)
    c = lambda a: jnp.asarray(a, BF16)
    return dict(expand=c(expand), hsum=c(expand.T), fold=c(fold), tril=c(tril), triu=c(tril.T),
                ustrict=ustrict, ule=ule, ult=ult)


def _doubled(tri, tk):
    b = tri[:tk, :tk]
    return jnp.asarray(np.concatenate([b, b], axis=0), BF16)


def _pick(n, cap):
    best = 128
    for t in range(128, min(n, cap) + 1, 128):
        if n % t == 0:
            best = t
    return n if n <= cap else best


def _matmul(a, b, dims, out_dtype, name, tm_cap=1024, tn_cap=2048, tk_cap=1024, exchange=None):
    if dims == "nn":
        (m, k), (_, n) = a.shape, b.shape
    elif dims == "nt":
        (m, k), (n, _) = a.shape, b.shape
    else:
        (k, m), (_, n) = a.shape, b.shape
    tm, tn, tk = _pick(m, tm_cap), _pick(n, tn_cap), _pick(k, tk_cap)
    nk = k // tk
    a_spec = (pl.BlockSpec((tk, tm), lambda i, j, kk: (kk, i)) if dims == "tn"
              else pl.BlockSpec((tm, tk), lambda i, j, kk: (i, kk)))
    b_spec = (pl.BlockSpec((tn, tk), lambda i, j, kk: (j, kk)) if dims == "nt"
              else pl.BlockSpec((tk, tn), lambda i, j, kk: (kk, j)))

    grid = (m // tm, n // tn, nk)

    def body(a_ref, b_ref, *rest):
        if exchange is None:
            o_ref, acc_ref = rest
        else:
            xs_ref, o_ref, xr_ref, acc_ref, xss, xrs = rest
            ids = [pl.program_id(d) for d in range(3)]
            x_start, x_finish = _exchange_stages(xs_ref, xr_ref, xss, xrs)
            pl.when(functools.reduce(jnp.logical_and, [p == 0 for p in ids]))(x_start)
        kk = pl.program_id(2)
        part = _dot(a_ref[...], b_ref[...], dims)
        if nk == 1:
            o_ref[...] = part.astype(out_dtype)
        else:
            @pl.when(kk == 0)
            def _():
                acc_ref[...] = part

            @pl.when(kk > 0)
            def _():
                acc_ref[...] += part

            @pl.when(kk == nk - 1)
            def _():
                o_ref[...] = acc_ref[...].astype(out_dtype)
        if exchange is not None:
            pl.when(functools.reduce(jnp.logical_and, [p == g - 1 for p, g in zip(ids, grid)]))(x_finish)

    in_specs = [a_spec, b_spec]
    out_specs = [pl.BlockSpec((tm, tn), lambda i, j, kk: (i, j))]
    out_shape = [jax.ShapeDtypeStruct((m, n), out_dtype)]
    scratch = [pltpu.VMEM((tm, tn) if nk > 1 else (8, 128), F32)]
    args = [a, b]
    if exchange is not None:
        in_specs.append(HB)
        out_specs.append(HB)
        out_shape.append(jax.ShapeDtypeStruct((3,) + exchange.shape[1:], exchange.dtype))
        scratch += EXCHANGE_SEMS
        args.append(exchange)
    out = pl.pallas_call(
        body, name=name, grid=grid, in_specs=in_specs, out_specs=out_specs, out_shape=out_shape,
        scratch_shapes=scratch,
        compiler_params=_cp(("parallel", "parallel", "arbitrary") if exchange is None else ("arbitrary",) * 3),
    )(*args)
    return out[0] if exchange is None else out


def _row_spec(tm, width=D_MODEL, col=0):
    return pl.BlockSpec((tm, width), lambda i: (i, col))


def _fix_spec(shape):
    return pl.BlockSpec(shape, lambda *_: (0,) * len(shape))


def _norm_mod(x, nw, mod, row_sh, name):
    t = x.shape[0]
    tm = min(t, 512)

    def body(x_ref, nw_ref, mod_ref, h_ref):
        xv = x_ref[...]
        r = lax.rsqrt(jnp.mean(xv * xv, axis=-1, keepdims=True) + EPS)
        sh = mod_ref[row_sh:row_sh + 1, :]
        sc = mod_ref[row_sh + 1:row_sh + 2, :]
        h_ref[...] = (xv * r * nw_ref[...] * (1.0 + sc) + sh).astype(BF16)

    return pl.pallas_call(
        body, name=name, grid=(t // tm,),
        in_specs=[_row_spec(tm), _fix_spec((1, D_MODEL)), _fix_spec((8, D_MODEL))],
        out_specs=_row_spec(tm), out_shape=jax.ShapeDtypeStruct((t, D_MODEL), BF16),
        compiler_params=_cp(("parallel",)),
    )(x, nw, mod)


def _resid_norm(x, mix, nw, mod, name):
    t = x.shape[0]
    tm = min(t, 512)

    def body(x_ref, mix_ref, nw_ref, mod_ref, x1_ref, h_ref):
        x1 = x_ref[...] + mod_ref[2:3, :] * mix_ref[...]
        x1_ref[...] = x1
        r = lax.rsqrt(jnp.mean(x1 * x1, axis=-1, keepdims=True) + EPS)
        h_ref[...] = (x1 * r * nw_ref[...] * (1.0 + mod_ref[4:5, :]) + mod_ref[3:4, :]).astype(BF16)

    return pl.pallas_call(
        body, name=name, grid=(t // tm,),
        in_specs=[_row_spec(tm), _row_spec(tm), _fix_spec((1, D_MODEL)), _fix_spec((8, D_MODEL))],
        out_specs=[_row_spec(tm), _row_spec(tm)],
        out_shape=[jax.ShapeDtypeStruct((t, D_MODEL), F32), jax.ShapeDtypeStruct((t, D_MODEL), BF16)],
        compiler_params=_cp(("parallel",)),
    )(x, mix, nw, mod)


def _act_fwd(gu, name):
    t = gu.shape[0]
    tm, tn = min(t, 512), D_FF // 2
    nb = D_FF // tn

    def body(g_ref, u_ref, a_ref):
        g = g_ref[...].astype(F32)
        a_ref[...] = (g * _sig(g) * u_ref[...].astype(F32)).astype(BF16)

    return pl.pallas_call(
        body, name=name, grid=(t // tm, nb),
        in_specs=[pl.BlockSpec((tm, tn), lambda i, j: (i, j)), pl.BlockSpec((tm, tn), lambda i, j: (i, j + nb))],
        out_specs=pl.BlockSpec((tm, tn), lambda i, j: (i, j)),
        out_shape=jax.ShapeDtypeStruct((t, D_FF), BF16),
        compiler_params=_cp(("parallel", "parallel")),
    )(gu, gu)


def _act_bwd(dact, gu, name):
    t = gu.shape[0]
    tm = min(t, 256)

    def body(d_ref, g_ref, u_ref, o_ref):
        g, d = g_ref[...].astype(F32), d_ref[...].astype(F32)
        s = _sig(g)
        o_ref[:, 0:D_FF] = (d * u_ref[...].astype(F32) * s * (1.0 + g * (1.0 - s))).astype(BF16)
        o_ref[:, D_FF:2 * D_FF] = (d * g * s).astype(BF16)

    return pl.pallas_call(
        body, name=name, grid=(t // tm,),
        in_specs=[pl.BlockSpec((tm, D_FF), lambda i: (i, 0)), pl.BlockSpec((tm, D_FF), lambda i: (i, 0)),
                  pl.BlockSpec((tm, D_FF), lambda i: (i, 1))],
        out_specs=pl.BlockSpec((tm, 2 * D_FF), lambda i: (i, 0)),
        out_shape=jax.ShapeDtypeStruct((t, 2 * D_FF), BF16),
        compiler_params=_cp(("parallel",)),
    )(dact, gu, gu)


def _loss_head(x1, ffn, tgt, mod, name):
    t = x1.shape[0]
    tm = min(t, 512)

    def body(x1_ref, f_ref, t_ref, mod_ref, dffn_ref, dout_ref, dg2_ref, loss_ref):
        i = pl.program_id(0)
        g2 = mod_ref[5:6, :]
        f = f_ref[...]
        err = x1_ref[...] + g2 * f - t_ref[...]
        dout = err * (1.0 / D_MODEL)
        dout_ref[...] = dout
        dffn_ref[...] = (dout * g2).astype(BF16)
        part = jnp.zeros((8, 128), F32) + 0.5 * jnp.sum(jnp.mean(err * err, axis=-1, keepdims=True))

        @pl.when(i == 0)
        def _():
            dg2_ref[...] = _colsum(dout * f)
            loss_ref[...] = part

        @pl.when(i > 0)
        def _():
            dg2_ref[...] += _colsum(dout * f)
            loss_ref[...] += part

    return pl.pallas_call(
        body, name=name, grid=(t // tm,),
        in_specs=[_row_spec(tm), _row_spec(tm), _row_spec(tm), _fix_spec((8, D_MODEL))],
        out_specs=[_row_spec(tm), _row_spec(tm), _fix_spec((1, D_MODEL)), _fix_spec((8, 128))],
        out_shape=[jax.ShapeDtypeStruct((t, D_MODEL), BF16), jax.ShapeDtypeStruct((t, D_MODEL), F32),
                   jax.ShapeDtypeStruct((1, D_MODEL), F32), jax.ShapeDtypeStruct((8, 128), F32)],
        compiler_params=_cp(("arbitrary",)),
    )(x1, ffn, tgt, mod)


def _norm_bwd(dh, xin, dres, aux, nw, mod, row_sh, gate_row, name):
    t = xin.shape[0]
    tm = min(t, 512)
    with_gate = gate_row is not None

    def body(*refs):
        if with_gate:
            dh_ref, x_ref, dr_ref, aux_ref, nw_ref, mod_ref, dx_ref, dg_ref, acc_ref = refs
        else:
            dh_ref, x_ref, dr_ref, nw_ref, mod_ref, dx_ref, acc_ref = refs
        i = pl.program_id(0)
        xv, dhv = x_ref[...], dh_ref[...]
        r = lax.rsqrt(jnp.mean(xv * xv, axis=-1, keepdims=True) + EPS)
        xn = xv * r
        nwv = nw_ref[...]
        sc1 = 1.0 + mod_ref[row_sh + 1:row_sh + 2, :]
        dxn = dhv * (nwv * sc1)
        dx = dr_ref[...] + r * (dxn - xn * jnp.mean(dxn * xn, axis=-1, keepdims=True))
        dx_ref[...] = dx
        dhx = dhv * xn
        rows = [_colsum(dhv), _colsum(dhx * nwv), _colsum(dhx * sc1)]
        if with_gate:
            dg_ref[...] = (dx * mod_ref[gate_row:gate_row + 1, :]).astype(BF16)
            rows.append(_colsum(dx * aux_ref[...]))

        @pl.when(i == 0)
        def _():
            acc_ref[...] = jnp.zeros_like(acc_ref)

        for k, v in enumerate(rows):
            acc_ref[k:k + 1, :] += v

    ins = [dh, xin, dres] + ([aux] if with_gate else []) + [nw, mod]
    in_specs = [_row_spec(tm)] * (4 if with_gate else 3) + [_fix_spec((1, D_MODEL)), _fix_spec((8, D_MODEL))]
    out_specs = [_row_spec(tm)] + ([_row_spec(tm)] if with_gate else []) + [_fix_spec((8, D_MODEL))]
    out_shape = ([jax.ShapeDtypeStruct((t, D_MODEL), F32)]
                 + ([jax.ShapeDtypeStruct((t, D_MODEL), BF16)] if with_gate else [])
                 + [jax.ShapeDtypeStruct((8, D_MODEL), F32)])
    return pl.pallas_call(
        body, name=name, grid=(t // tm,), in_specs=in_specs, out_specs=out_specs, out_shape=out_shape,
        compiler_params=_cp(("arbitrary",)),
    )(*ins)


XBC_COL0 = 4096 // 128
DT_COL = 5632 // 128


def _conv_pre(xv, w_ref, b_ref):
    t = xv.shape[0]
    row = lax.broadcasted_iota(jnp.int32, xv.shape, 0)
    pre = xv * w_ref[3:4, :] + b_ref[...]
    shifted = []
    for k in range(3):
        s = 3 - k
        xs = jnp.where(row >= s, pltpu.roll(xv, s, 0), 0.0)
        shifted.append(xs)
        pre = pre + xs * w_ref[k:k + 1, :]
    return pre, shifted, row, t


def _conv_fwd(proj, conv_w, conv_b, name):
    t = proj.shape[0]

    def body(x_ref, w_ref, b_ref, u_ref):
        pre, _, _, _ = _conv_pre(x_ref[...], w_ref, b_ref)
        u_ref[...] = pre * _sig(pre)

    return pl.pallas_call(
        body, name=name, grid=(D_CONV // 128,),
        in_specs=[pl.BlockSpec((t, 128), lambda j: (0, XBC_COL0 + j)), pl.BlockSpec((4, 128), lambda j: (0, j)),
                  pl.BlockSpec((1, 128), lambda j: (0, j))],
        out_specs=pl.BlockSpec((t, 128), lambda j: (0, j)),
        out_shape=jax.ShapeDtypeStruct((t, D_CONV), F32),
        compiler_params=_cp(("parallel",)),
    )(proj, conv_w, conv_b)


def _conv_bwd(proj, du, conv_w, conv_b, name):
    t = proj.shape[0]

    def body(x_ref, du_ref, w_ref, b_ref, dx_ref, dw_ref, db_ref):
        pre, shifted, row, _ = _conv_pre(x_ref[...], w_ref, b_ref)
        s = _sig(pre)
        dpre = du_ref[...] * s * (1.0 + pre * (1.0 - s))
        db_ref[...] = _colsum(dpre)
        dx = dpre * w_ref[3:4, :]
        dw_ref[3:4, :] = _colsum(dpre * x_ref[...])
        for k in range(3):
            sft = 3 - k
            dw_ref[k:k + 1, :] = _colsum(dpre * shifted[k])
            back = jnp.where(row < t - sft, pltpu.roll(dpre, t - sft, 0), 0.0)
            dx = dx + back * w_ref[k:k + 1, :]
        dx_ref[...] = dx.astype(BF16)

    return pl.pallas_call(
        body, name=name, grid=(D_CONV // 128,),
        in_specs=[pl.BlockSpec((t, 128), lambda j: (0, XBC_COL0 + j)), pl.BlockSpec((t, 128), lambda j: (0, j)),
                  pl.BlockSpec((4, 128), lambda j: (0, j)), pl.BlockSpec((1, 128), lambda j: (0, j))],
        out_specs=[pl.BlockSpec((t, 128), lambda j: (0, j)), pl.BlockSpec((4, 128), lambda j: (0, j)),
                   pl.BlockSpec((1, 128), lambda j: (0, j))],
        out_shape=[jax.ShapeDtypeStruct((t, D_CONV), BF16), jax.ShapeDtypeStruct((4, D_CONV), F32),
                   jax.ShapeDtypeStruct((1, D_CONV), F32)],
        compiler_params=_cp(("parallel",)),
    )(proj, du, conv_w, conv_b)


def _ssd_common(dtraw_ref, dtb_ref, alog_ref, tril, expand, l_s, lt_s):
    lane = lax.broadcasted_iota(jnp.int32, (1, 128), 1)
    dt = _softplus(dtraw_ref[...] + dtb_ref[...])
    a = jnp.where(lane < N_HEADS, -jnp.exp(alog_ref[...]), 0.0)
    lcs = _dotx_l(tril, dt * a)
    l_s[...] = lcs
    lt_s[...] = lcs.T
    llast = l_s[CHUNK - 1:CHUNK, :]
    ea = _dotx_r(jnp.exp(lcs), expand, n=2)
    ds = _dotx_r(jnp.exp(llast - lcs), expand, n=2)
    dtx = _dotx_r(dt, expand, n=2)
    return dt, a, lcs, llast, ea, ds, dtx


def _head_col(lcs, h):
    lane = lax.broadcasted_iota(jnp.int32, lcs.shape, 1)
    return jnp.sum(jnp.where(lane == h, lcs, 0.0), axis=1, keepdims=True)


def _decay(lcs, lt_s, h, causal):
    seg = _head_col(lcs, h) - lt_s[h:h + 1, :]
    return jnp.exp(jnp.where(causal, seg, -1e30))


def _ssd_fwd(u, proj, dtb, alog, dsk, nw, cst, name):
    t = u.shape[0]
    nc = t // CHUNK

    def body(xs_ref, b_ref, c_ref, dtraw_ref, z_ref, dtb_ref, alog_ref, dsk_ref, nw_ref, tril_ref, exp_ref,
             y_ref, yn_ref, prev_ref, carry, l_s, lt_s, yd_s):
        i = pl.program_id(0)

        @pl.when(i == 0)
        def _():
            carry[...] = jnp.zeros_like(carry)

        expand = exp_ref[...]
        dt, a, lcs, llast, ea, ds, dtx = _ssd_common(dtraw_ref, dtb_ref, alog_ref, tril_ref[...], expand, l_s, lt_s)
        xs = xs_ref[...]
        xg = xs * dtx
        xgb = xg.astype(BF16)
        xgd = (xg * ds).astype(BF16)
        prev = carry[...]
        prev_ref[0] = prev
        prevb = prev.astype(BF16)
        ri = lax.broadcasted_iota(jnp.int32, (CHUNK, CHUNK), 0)
        ci = lax.broadcasted_iota(jnp.int32, (CHUNK, CHUNK), 1)
        causal = ri >= ci
        lane = lax.broadcasted_iota(jnp.int32, (1, 128), 1)
        new_states, yoff = [], []
        for g in range(2):
            bg = b_ref[:, g * 128:(g + 1) * 128].astype(BF16)
            cg = c_ref[:, g * 128:(g + 1) * 128].astype(BF16)
            sc = _dot(cg, bg, "nt")
            gs = slice(g * 512, (g + 1) * 512)
            new_states.append(_dot(bg, xgd[:, gs], "tn"))
            yoff.append(_dot(cg, prevb[:, gs]))
            for pr in range(4):
                col = g * 512 + pr * 128
                xp = xgb[:, col:col + 128]
                acc = jnp.zeros((CHUNK, 128), F32)
                for half in range(2):
                    h = g * 8 + pr * 2 + half
                    m = (sc * _decay(lcs, lt_s, h, causal)).astype(BF16)
                    keep = (lane < HEAD_DIM) if half == 0 else (lane >= HEAD_DIM)
                    acc = acc + _dot(m, jnp.where(keep, xp, jnp.zeros_like(xp)))
                yd_s[:, col:col + 128] = acc
        y = yd_s[...] + jnp.concatenate(yoff, axis=1) * ea + xs * dsk_ref[...]
        y_ref[...] = y
        carry[...] = prev * jnp.max(_dotx_r(jnp.exp(llast) + jnp.zeros((8, 128), F32), expand, n=2), axis=0, keepdims=True) \
            + jnp.concatenate(new_states, axis=1)
        z = z_ref[...]
        yz = y * (z * _sig(z))
        nwv = nw_ref[...]
        for g in range(2):
            gs = slice(g * 512, (g + 1) * 512)
            v = yz[:, gs]
            r = lax.rsqrt(jnp.mean(v * v, axis=-1, keepdims=True) + EPS)
            yn_ref[:, gs] = (v * r * nwv[:, gs]).astype(BF16)

    row = lambda w, col: pl.BlockSpec((CHUNK, w), lambda i: (i, col))
    return pl.pallas_call(
        body, name=name, grid=(nc,),
        in_specs=[row(1024, 0), row(256, 4), row(256, 5), row(128, DT_COL), row(1024, 0),
                  _fix_spec((1, 128)), _fix_spec((1, 128)), _fix_spec((1, D_MODEL)), _fix_spec((1, D_MODEL)),
                  _fix_spec((CHUNK, CHUNK)), _fix_spec((128, D_MODEL))],
        out_specs=[row(1024, 0), row(1024, 0), pl.BlockSpec((1, 128, D_MODEL), lambda i: (i, 0, 0))],
        out_shape=[jax.ShapeDtypeStruct((t, D_MODEL), F32), jax.ShapeDtypeStruct((t, D_MODEL), BF16),
                   jax.ShapeDtypeStruct((nc, 128, D_MODEL), F32)],
        scratch_shapes=[pltpu.VMEM((128, D_MODEL), F32), pltpu.VMEM((128, 128), F32), pltpu.VMEM((128, 128), F32),
                        pltpu.VMEM((CHUNK, D_MODEL), F32)],
        compiler_params=_cp(("arbitrary",)),
    )(u, u, u, proj, proj, dtb, alog, dsk, nw, cst["tril"], cst["expand"])


def _ssd_bwd(u, proj, y, prev, dycat, dtb, alog, dsk, nw, cst, name):
    t = u.shape[0]
    nc = t // CHUNK

    def body(xs_ref, b_ref, c_ref, dtraw_ref, z_ref, y_ref, prev_ref, dyn_ref, dtb_ref, alog_ref, dsk_ref, nw_ref,
             tril_ref, triu_ref, exp_ref, hs_ref,
             du_ref, ddt_ref, dz_ref, acc_ref, acc16_ref, dcarry, l_s, lt_s, dxg_s):
        i = pl.program_id(0)

        @pl.when(i == 0)
        def _():
            dcarry[...] = jnp.zeros_like(dcarry)
            acc_ref[...] = jnp.zeros_like(acc_ref)
            acc16_ref[...] = jnp.zeros_like(acc16_ref)

        expand, hsum = exp_ref[...], hs_ref[...]
        dt, a, lcs, llast, ea, ds, dtx = _ssd_common(dtraw_ref, dtb_ref, alog_ref, tril_ref[...], expand, l_s, lt_s)
        xs = xs_ref[...]
        xg = xs * dtx
        xgb = xg.astype(BF16)
        xgdf = xg * ds
        xgd = xgdf.astype(BF16)
        dsk_v, nwv = dsk_ref[...], nw_ref[...]
        z, y = z_ref[...], y_ref[...]
        sz = _sig(z)
        silz = z * sz
        yz = y * silz
        dyn = dyn_ref[...]
        dyz_parts, dnw_parts = [], []
        for g in range(2):
            gs = slice(g * 512, (g + 1) * 512)
            v = yz[:, gs]
            r = lax.rsqrt(jnp.mean(v * v, axis=-1, keepdims=True) + EPS)
            yhat = v * r
            dnw_parts.append(_colsum(dyn[:, gs] * yhat))
            dw = dyn[:, gs] * nwv[:, gs]
            dyz_parts.append(r * (dw - yhat * jnp.mean(dw * yhat, axis=-1, keepdims=True)))
        dyz = jnp.concatenate(dyz_parts, axis=1)
        dy = dyz * silz
        dz_ref[...] = (dyz * y * (sz * (1.0 + z * (1.0 - sz)))).astype(BF16)
        acc_ref[0:1, :] += jnp.concatenate(dnw_parts, axis=1)
        acc_ref[1:2, :] += _colsum(dy * xs)
        dyb = dy.astype(BF16)
        dq = (dy * ea).astype(BF16)
        dcar = dcarry[...]
        dcarb = dcar.astype(BF16)
        prev = prev_ref[0]
        prevb = prev.astype(BF16)
        ri = lax.broadcasted_iota(jnp.int32, (CHUNK, CHUNK), 0)
        ci = lax.broadcasted_iota(jnp.int32, (CHUNK, CHUNK), 1)
        causal = ri >= ci
        lane = lax.broadcasted_iota(jnp.int32, (1, 128), 1)
        dprev, dxgd, yoff = [], [], []
        dl_l = jnp.zeros((CHUNK, CHUNK), F32)
        dl_s = jnp.zeros((CHUNK, CHUNK), F32)
        for g in range(2):
            gs = slice(g * 512, (g + 1) * 512)
            bg = b_ref[:, g * 128:(g + 1) * 128].astype(BF16)
            cg = c_ref[:, g * 128:(g + 1) * 128].astype(BF16)
            sc = _dot(cg, bg, "nt")
            yoff.append(_dot(cg, prevb[:, gs]))
            dcg = _dot(dq[:, gs], prevb[:, gs], "nt")
            dprev.append(_dot(cg, dq[:, gs], "tn"))
            dbg = _dot(xgd[:, gs], dcarb[:, gs], "nt")
            dxgd.append(_dot(bg, dcarb[:, gs]))
            dsc = jnp.zeros((CHUNK, CHUNK), F32)
            for pr in range(4):
                col = g * 512 + pr * 128
                xp = xgb[:, col:col + 128]
                dyp = dyb[:, col:col + 128]
                acc = jnp.zeros((CHUNK, 128), F32)
                for half in range(2):
                    h = g * 8 + pr * 2 + half
                    dec = _decay(lcs, lt_s, h, causal)
                    mf = sc * dec
                    keep = (lane < HEAD_DIM) if half == 0 else (lane >= HEAD_DIM)
                    dyh = jnp.where(keep, dyp, jnp.zeros_like(dyp))
                    dm = _dot(dyh, xp, "nt")
                    acc = acc + _dot(mf.astype(BF16), dyh, "tn")
                    dsc = dsc + dm * dec
                    gm = dm * mf
                    dl_l = dl_l + jnp.where(ci == h, jnp.sum(gm, axis=1, keepdims=True), 0.0)
                    dl_s = dl_s + jnp.where(ri == h, jnp.sum(gm, axis=0, keepdims=True), 0.0)
                dxg_s[:, col:col + 128] = acc
            dscb = dsc.astype(BF16)
            dcg = dcg + _dot(dscb, bg)
            dbg = dbg + _dot(dscb, cg, "tn")
            du_ref[:, 1024 + g * 128:1024 + (g + 1) * 128] = dbg
            du_ref[:, 1280 + g * 128:1280 + (g + 1) * 128] = dcg
        dxgd = jnp.concatenate(dxgd, axis=1)
        dxg = dxg_s[...] + dxgd * ds
        du_ref[:, 0:1024] = dy * dsk_v + dxg * dtx
        hs1 = _dotx_r(dxg * xs, hsum)
        yoff = jnp.concatenate(yoff, axis=1) * ea
        dl = dl_l - dl_s.T + _dotx_r(dy * yoff - xgdf * dxgd, hsum)
        rows8 = lax.broadcasted_iota(jnp.int32, (8, D_MODEL), 0)
        two = jnp.where(rows8 == 0, _colsum(dxgd * xgdf), jnp.where(rows8 == 1, _colsum(dcar * prev), 0.0))
        two = _dotx_r(two, hsum)
        r8 = lax.broadcasted_iota(jnp.int32, (8, 128), 0)
        dllast = _colsum(jnp.where(r8 == 0, two, 0.0)) + _colsum(jnp.where(r8 == 1, two, 0.0)) * jnp.exp(llast)
        rowi = lax.broadcasted_iota(jnp.int32, (CHUNK, 128), 0)
        dl = dl + jnp.where(rowi == CHUNK - 1, dllast, 0.0)
        dadt = _dotx_l(triu_ref[...], dl)
        ddt = dadt * a + hs1
        draw = ddt * _sig(dtraw_ref[...] + dtb_ref[...])
        ddt_ref[...] = draw.astype(BF16)
        acc16_ref[0:1, :] += _colsum(draw)
        acc16_ref[1:2, :] += _colsum(dadt * dt) * a
        dcarry[...] = dcar * jnp.max(_dotx_r(jnp.exp(llast) + jnp.zeros((8, 128), F32), expand, n=2), axis=0, keepdims=True) \
            + jnp.concatenate(dprev, axis=1)

        @pl.when(i == nc - 1)
        def _():
            hd = _dotx_r(acc_ref[...], hsum)
            acc16_ref[2:3, :] = _colsum(jnp.where(lax.broadcasted_iota(jnp.int32, (8, 128), 0) == 1, hd, 0.0))

    rev = lambda w, col: pl.BlockSpec((CHUNK, w), lambda i: (nc - 1 - i, col))
    return pl.pallas_call(
        body, name=name, grid=(nc,),
        in_specs=[rev(1024, 0), rev(256, 4), rev(256, 5), rev(128, DT_COL), rev(1024, 0), rev(1024, 0),
                  pl.BlockSpec((1, 128, D_MODEL), lambda i: (nc - 1 - i, 0, 0)), rev(1024, 0),
                  _fix_spec((1, 128)), _fix_spec((1, 128)), _fix_spec((1, D_MODEL)), _fix_spec((1, D_MODEL)),
                  _fix_spec((CHUNK, CHUNK)), _fix_spec((CHUNK, CHUNK)), _fix_spec((128, D_MODEL)),
                  _fix_spec((D_MODEL, 128))],
        out_specs=[rev(D_CONV, 0), rev(128, 0), rev(1024, 0), _fix_spec((8, D_MODEL)), _fix_spec((8, 128))],
        out_shape=[jax.ShapeDtypeStruct((t, D_CONV), F32), jax.ShapeDtypeStruct((t, 128), BF16),
                   jax.ShapeDtypeStruct((t, D_MODEL), BF16), jax.ShapeDtypeStruct((8, D_MODEL), F32),
                   jax.ShapeDtypeStruct((8, 128), F32)],
        scratch_shapes=[pltpu.VMEM((128, D_MODEL), F32), pltpu.VMEM((128, 128), F32), pltpu.VMEM((128, 128), F32),
                        pltpu.VMEM((CHUNK, D_MODEL), F32)],
        compiler_params=_cp(("arbitrary",)),
    )(u, u, u, proj, proj, y, prev, dycat, dtb, alog, dsk, nw,
      cst["tril"], cst["triu"], cst["expand"], cst["hsum"])


def _head_rms(v, hsum, expand):
    ms = _dotx_r(v * v, hsum, n=2) * (1.0 / HEAD_DIM)
    return _dotx_r(lax.rsqrt(ms + EPS), expand, n=2)


def _qk_fwd(proj, qw, kw, cst, name):
    t = proj.shape[0]
    tm = min(t, 256)
    scale = HEAD_DIM ** -0.5

    def body(q_ref, k_ref, v_ref, qw_ref, kw_ref, hs_ref, exp_ref, qs_ref, kn_ref, vb_ref):
        hsum, expand = hs_ref[...], exp_ref[...]
        q, k = q_ref[...], k_ref[...]
        qs_ref[...] = (q * _head_rms(q, hsum, expand) * qw_ref[...] * scale).astype(BF16)
        kn_ref[...] = (k * _head_rms(k, hsum, expand) * kw_ref[...]).astype(BF16)
        vb_ref[...] = v_ref[...].astype(BF16)

    return pl.pallas_call(
        body, name=name, grid=(t // tm,),
        in_specs=[_row_spec(tm, col=1), _row_spec(tm, col=2), _row_spec(tm, col=3),
                  _fix_spec((1, D_MODEL)), _fix_spec((1, D_MODEL)), _fix_spec((D_MODEL, 128)),
                  _fix_spec((128, D_MODEL))],
        out_specs=[_row_spec(tm)] * 3, out_shape=[jax.ShapeDtypeStruct((t, D_MODEL), BF16)] * 3,
        compiler_params=_cp(("parallel",)),
    )(proj, proj, proj, qw, kw, cst["hsum"], cst["expand"])


def _qk_bwd(proj, dqs, dkn, dv, qw, kw, cst, name):
    t = proj.shape[0]
    tm = min(t, 256)
    scale = HEAD_DIM ** -0.5

    def body(q_ref, k_ref, dq_ref, dk_ref, dv_ref, qw_ref, kw_ref, hs_ref, exp_ref, fold_ref,
             oq_ref, ok_ref, ov_ref, dw_ref):
        i = pl.program_id(0)
        hsum, expand = hs_ref[...], exp_ref[...]
        rows8 = lax.broadcasted_iota(jnp.int32, (8, D_MODEL), 0)
        sums = jnp.zeros((8, D_MODEL), F32)
        for n, (x_ref, d_ref, w_ref, o_ref, sc) in enumerate(
                [(q_ref, dq_ref, qw_ref, oq_ref, scale), (k_ref, dk_ref, kw_ref, ok_ref, 1.0)]):
            xv = x_ref[...]
            r = _head_rms(xv, hsum, expand)
            xhat = xv * r
            dn = d_ref[...] * sc
            sums = sums + jnp.where(rows8 == n, _colsum(dn * xhat), 0.0)
            dw = dn * w_ref[...]
            mean = _dotx_r(_dotx_r(dw * xhat, hsum, n=2), expand, n=2) * (1.0 / HEAD_DIM)
            o_ref[...] = (r * (dw - xhat * mean)).astype(BF16)
        ov_ref[...] = dv_ref[...].astype(BF16)
        folded = _dotx_r(sums, fold_ref[...])

        @pl.when(i == 0)
        def _():
            dw_ref[...] = folded

        @pl.when(i > 0)
        def _():
            dw_ref[...] += folded

    return pl.pallas_call(
        body, name=name, grid=(t // tm,),
        in_specs=[_row_spec(tm, col=1), _row_spec(tm, col=2), _row_spec(tm), _row_spec(tm), _row_spec(tm),
                  _fix_spec((1, D_MODEL)), _fix_spec((1, D_MODEL)), _fix_spec((D_MODEL, 128)),
                  _fix_spec((128, D_MODEL)), _fix_spec((D_MODEL, 128))],
        out_specs=[_row_spec(tm)] * 3 + [_fix_spec((8, 128))],
        out_shape=[jax.ShapeDtypeStruct((t, D_MODEL), BF16)] * 3 + [jax.ShapeDtypeStruct((8, 128), F32)],
        compiler_params=_cp(("arbitrary",)),
    )(proj, proj, dqs, dkn, dv, qw, kw, cst["hsum"], cst["expand"], cst["fold"])


def _sb_masks(i, kb, tq, tk):
    tpos = i * tq + lax.broadcasted_iota(jnp.int32, (tq, 1), 0)
    spos = kb * tk + lax.broadcasted_iota(jnp.int32, (1, tk), 1)
    return spos < tpos


def _grid_marks(n0, n1):
    j, i = pl.program_id(0), pl.program_id(1)
    return (jnp.logical_and(j == 0, i == 0), jnp.logical_and(j == n0 // 2, i == 0),
            jnp.logical_and(j == n0 - 1, i == n1 - 1))


def _sb_fwd(qs, kn, vb, pack, cst, name):
    t = qs.shape[0]
    tq = tk = min(t, SB_TILE)
    nq = t // tq
    npair = D_MODEL // 128

    def body(q_ref, k_ref, v_ref, u_ref, p_ref, rt_ref, ob_ref, cnt_ref, gat_ref, acc, rs, gss, grs):
        at_first, at_mid, at_last = _grid_marks(npair, nq)
        g_start, g_relay, g_finish = _gather_stages(p_ref, gat_ref, gss, grs)
        pl.when(at_first)(g_start)
        pl.when(at_mid)(g_relay)
        i = pl.program_id(1)
        lane = lax.broadcasted_iota(jnp.int32, (1, 128), 1)
        q2 = q_ref[...]
        zero = jnp.zeros_like(q2)
        qh = [jnp.where(lane < HEAD_DIM, q2, zero), jnp.where(lane >= HEAD_DIM, q2, zero)]
        acc[...] = jnp.zeros_like(acc)
        rs[...] = jnp.zeros_like(rs)
        ustrict = u_ref[...]

        def tile(kb, masked):
            off = pl.multiple_of(kb * tk, tk)
            k2 = k_ref[pl.ds(off, tk), :]
            v2 = v_ref[pl.ds(off, tk), :]
            strict = _sb_masks(i, kb, tq, tk) if masked else None
            s = [_dot(qh[h], k2, "nt") for h in range(2)]
            a, r, lb = [None] * 2, [None] * 2, [None] * 2
            for h in range(2):
                sp = _softplus(s[h])
                a[h] = s[h] - sp
                r[h] = jnp.where(strict, -sp, 0.0) if masked else -sp
                lb[h] = _dot2(r[h], ustrict)
            for h in range(2):
                lw = a[h] + lb[h] + rs[h]
                w = jnp.exp(jnp.where(strict, lw, -1e30) if masked else lw)
                rs[h] = rs[h] + jnp.sum(r[h], axis=1, keepdims=True)
                acc[h] = acc[h] + _dot(w.astype(BF16), v2)

        tile(i, True)

        def live():
            return jnp.max(jnp.maximum(rs[0], rs[1]))

        def more(c):
            return jnp.logical_and(c[0] < i, c[1] > SB_DEAD)

        def step(c):
            tile(i - 1 - c[0], False)
            return c[0] + 1, live()

        n_off, _ = lax.while_loop(more, step, (jnp.int32(0), live()))
        cnt_ref[pl.program_id(0), i] = n_off.astype(F32)
        rt_ref[...] = jnp.where(lane < HEAD_DIM, rs[0], rs[1])
        ob_ref[...] = jnp.where(lane < HEAD_DIM, acc[0], acc[1]).astype(BF16)
        pl.when(at_last)(g_finish)

    return pl.pallas_call(
        body, name=name, grid=(npair, nq),
        in_specs=[pl.BlockSpec((tq, 128), lambda j, i: (i, j)), pl.BlockSpec((t, 128), lambda j, i: (0, j)),
                  pl.BlockSpec((t, 128), lambda j, i: (0, j)), _fix_spec((2 * tk, tk)), HB],
        out_specs=[pl.BlockSpec((tq, 128), lambda j, i: (i, j))] * 2 + [pl.BlockSpec(memory_space=pltpu.SMEM), HB],
        out_shape=[jax.ShapeDtypeStruct((t, D_MODEL), F32), jax.ShapeDtypeStruct((t, D_MODEL), BF16),
                   jax.ShapeDtypeStruct((npair, nq), F32),
                   jax.ShapeDtypeStruct((N_SHARDS,) + pack.shape, pack.dtype)],
        scratch_shapes=[pltpu.VMEM((2, tq, 128), F32), pltpu.VMEM((2, tq, 1), F32)] + GATHER_SEMS,
        compiler_params=_cp(("arbitrary", "arbitrary")),
    )(qs, kn, vb, _doubled(cst["ustrict"], tk), pack)


def _sb_bwd(qs, kn, vb, rtot, cnt, dycat, csum_b, cst, name):
    t = qs.shape[0]
    tq = tk = min(t, SB_TILE)
    nq = t // tq
    npair = D_MODEL // 128

    def body(q_ref, k_ref, v_ref, rt_ref, do_ref, us_ref, ui_ref, cnt_ref, xs_ref, dq_ref, dk_ref, dv_ref, xr_ref,
             acc, rs, es, xss, xrs):
        at_first, _, at_last = _grid_marks(npair, nq)
        x_start, x_finish = _exchange_stages(xs_ref, xr_ref, xss, xrs)
        pl.when(at_first)(x_start)
        i = pl.program_id(1)
        lane = lax.broadcasted_iota(jnp.int32, (1, 128), 1)
        keep = [lane < HEAD_DIM, lane >= HEAD_DIM]
        q2, rt = q_ref[...], rt_ref[...]
        do2b = do_ref[...].astype(BF16)
        qh = [jnp.where(kp, q2, jnp.zeros_like(q2)) for kp in keep]
        doh = [jnp.where(kp, do2b, jnp.zeros_like(do2b)) for kp in keep]
        rtot_h = [jnp.sum(jnp.where(lane == n * HEAD_DIM, rt, 0.0), axis=1, keepdims=True) for n in range(2)]
        acc[...] = jnp.zeros_like(acc)
        rs[...] = jnp.zeros_like(rs)
        es[...] = jnp.zeros_like(es)

        @pl.when(i == 0)
        def _():
            dk_ref[...] = jnp.zeros_like(dk_ref)
            dv_ref[...] = jnp.zeros_like(dv_ref)

        ule, ult = us_ref[...], ui_ref[...]

        def tile(kb, masked):
            off = pl.multiple_of(kb * tk, tk)
            k2 = k_ref[pl.ds(off, tk), :]
            v2 = v_ref[pl.ds(off, tk), :]
            strict = _sb_masks(i, kb, tq, tk) if masked else None
            s = [_dot(qh[h], k2, "nt") for h in range(2)]
            dw = [_dot(doh[h], v2, "nt") for h in range(2)]
            a, sg, r, pin, w, e, cin = ([None] * 2 for _ in range(7))
            for h in range(2):
                sp = _softplus(s[h])
                a[h] = s[h] - sp
                sg[h] = jnp.exp(a[h])
                r[h] = jnp.where(strict, -sp, 0.0) if masked else -sp
                pin[h] = _dot2(r[h], ule)
            for h in range(2):
                lw = a[h] + ((rtot_h[h] - rs[h]) - pin[h])
                w[h] = jnp.exp(jnp.where(strict, lw, -1e30) if masked else lw)
                e[h] = w[h] * dw[h]
                cin[h] = _dot2(e[h], ult)
            dk_t = jnp.zeros((tk, 128), F32)
            dv_t = jnp.zeros((tk, 128), F32)
            for h in range(2):
                dl = e[h] * (1.0 - sg[h]) - (es[h] + cin[h]) * sg[h]
                dl = (jnp.where(strict, dl, 0.0) if masked else dl).astype(BF16)
                rs[h] = rs[h] + jnp.sum(r[h], axis=1, keepdims=True)
                es[h] = es[h] + jnp.sum(e[h], axis=1, keepdims=True)
                acc[h] = acc[h] + _dot(dl, k2)
                dk_t = dk_t + _dot(dl, qh[h], "tn")
                dv_t = dv_t + _dot(w[h].astype(BF16), doh[h], "tn")
            dk_ref[pl.ds(off, tk), :] += dk_t
            dv_ref[pl.ds(off, tk), :] += dv_t

        def step(kb, carry):
            tile(kb, False)
            return carry

        n_off = cnt_ref[pl.program_id(0), i].astype(jnp.int32)
        lax.fori_loop(i - n_off, i, step, 0)
        tile(i, True)
        dq_ref[...] = jnp.where(lane < HEAD_DIM, acc[0], acc[1])
        pl.when(at_last)(x_finish)

    return pl.pallas_call(
        body, name=name, grid=(npair, nq),
        in_specs=[pl.BlockSpec((tq, 128), lambda j, i: (i, j)), pl.BlockSpec((t, 128), lambda j, i: (0, j)),
                  pl.BlockSpec((t, 128), lambda j, i: (0, j)), pl.BlockSpec((tq, 128), lambda j, i: (i, j)),
                  pl.BlockSpec((tq, 128), lambda j, i: (i, npair + j)),
                  _fix_spec((2 * tk, tk)), _fix_spec((2 * tk, tk)), pl.BlockSpec(memory_space=pltpu.SMEM), HB],
        out_specs=[pl.BlockSpec((tq, 128), lambda j, i: (i, j)), pl.BlockSpec((t, 128), lambda j, i: (0, j)),
                   pl.BlockSpec((t, 128), lambda j, i: (0, j)), HB],
        out_shape=[jax.ShapeDtypeStruct((t, D_MODEL), F32)] * 3
        + [jax.ShapeDtypeStruct((3,) + csum_b.shape[1:], csum_b.dtype)],
        scratch_shapes=[pltpu.VMEM((2, tq, 128), F32), pltpu.VMEM((2, tq, 1), F32), pltpu.VMEM((2, tq, 1), F32)]
        + EXCHANGE_SEMS,
        compiler_params=_cp(("arbitrary", "arbitrary")),
    )(qs, kn, vb, rtot, dycat, _doubled(cst["ule"], tk), _doubled(cst["ult"], tk), cnt, csum_b)


def _adamw(w, g, m, v, name):
    lead = (1,) * (w.ndim - 2)
    rows, cols = w.shape[-2:]
    fits = [d for d in range(8, rows, 8) if rows % d == 0 and d * cols * 4 <= ADAM_BLOCK_BYTES]
    tr = max(fits) if fits else rows
    c1 = 1.0 - ADAM_B1 ** ADAM_STEP
    c2 = 1.0 - ADAM_B2 ** ADAM_STEP

    def body(w_ref, g_ref, m_ref, v_ref, d_ref, nm_ref, nv_ref):
        gv = g_ref[...]
        nm = ADAM_B1 * m_ref[...] + (1.0 - ADAM_B1) * gv
        nv = ADAM_B2 * v_ref[...] + (1.0 - ADAM_B2) * (gv * gv)
        nm_ref[...] = nm
        nv_ref[...] = nv
        d_ref[...] = -ADAM_LR * ((nm / c1) / (jnp.sqrt(nv / c2) + ADAM_EPS) + ADAM_WD * w_ref[...])

    spec = pl.BlockSpec(lead + (tr, cols), lambda i: (0,) * len(lead) + (i, 0))
    return pl.pallas_call(
        body, name=name, grid=(rows // tr,), in_specs=[spec] * 4, out_specs=[spec] * 3,
        out_shape=[jax.ShapeDtypeStruct(w.shape, F32)] * 3, compiler_params=_cp(("parallel",)),
    )(w, g, m, v)


def _place():
    x, y, c = lax.axis_index("x"), lax.axis_index("y"), lax.axis_index("c")
    chips = [(1 - x, y), (x, 1 - y), (1 - x, 1 - y)]
    return x, y, c, chips


VM = pl.BlockSpec(memory_space=pltpu.VMEM)
HB = pl.BlockSpec(memory_space=pltpu.HBM)


def _small_gather(pack, name, reduce):
    rows = pack.shape[0]

    def body(p_ref, gat_ref, *rest):
        if reduce:
            sum_ref, ss, rs = rest
        else:
            ss, rs = rest
        x, y, c, _ = _place()
        me = 4 * x + 2 * y + c
        peers = [(x, y, 1 - c), (1 - x, y, c), (x, 1 - y, c), (1 - x, 1 - y, c),
                 (1 - x, y, 1 - c), (x, 1 - y, 1 - c), (1 - x, 1 - y, 1 - c)]

        def copy(k, slot, to):
            return pltpu.make_async_remote_copy(src_ref=p_ref, dst_ref=gat_ref.at[slot], send_sem=ss.at[k],
                                                recv_sem=rs.at[k], device_id=to, device_id_type=MESH)

        sends = [copy(k, me, p) for k, p in enumerate(peers)]
        for s in sends:
            s.start()
        gat_ref[me] = p_ref[...]
        for k, p in enumerate(peers):
            copy(k, 4 * p[0] + 2 * p[1] + p[2], p).wait_recv()
        for s in sends:
            s.wait_send()
        if reduce:
            tot = gat_ref[0]
            for b in range(1, 8):
                tot = tot + gat_ref[b]
            sum_ref[...] = tot

    out_shape = [jax.ShapeDtypeStruct((8, rows, D_MODEL), F32)]
    if reduce:
        out_shape.append(jax.ShapeDtypeStruct((rows, D_MODEL), F32))
    return pl.pallas_call(
        body, name=name, in_specs=[VM], out_specs=[VM] * len(out_shape), out_shape=out_shape,
        scratch_shapes=[pltpu.SemaphoreType.DMA((7,)), pltpu.SemaphoreType.DMA((7,))],
        compiler_params=_cp(),
    )(pack)


def _mod_exchange(cond, w_ada, b_shard, name):
    def body(c_ref, w_ref, b_ref, modp_ref, ss, rs):
        x, y, c, chips = _place()
        sh = 2 * x + y
        cv = c_ref[...]
        cv = cv * _sig(cv)
        modp_ref[sh] = jnp.dot(cv, w_ref[...], precision=lax.Precision.HIGHEST,
                               preferred_element_type=F32) + b_ref[...]

        def copy(k, slot, to):
            return pltpu.make_async_remote_copy(src_ref=modp_ref.at[slot], dst_ref=modp_ref.at[slot],
                                                send_sem=ss.at[k], recv_sem=rs.at[k], device_id=to,
                                                device_id_type=MESH)

        sends = [copy(k, sh, (*ch, c)) for k, ch in enumerate(chips)]
        for s in sends:
            s.start()
        for k, ch in enumerate(chips):
            copy(k, 2 * ch[0] + ch[1], (*ch, c)).wait_recv()
        for s in sends:
            s.wait_send()

    return pl.pallas_call(
        body, name=name, in_specs=[VM, VM, VM], out_specs=VM,
        out_shape=jax.ShapeDtypeStruct((N_SHARDS, 8, 6 * D_MODEL // N_SHARDS), F32),
        scratch_shapes=[pltpu.SemaphoreType.DMA((3,)), pltpu.SemaphoreType.DMA((3,))],
        compiler_params=_cp(),
    )(cond, w_ada, b_shard)


def _gather_stages(p_ref, out_ref, ss, rs):
    hf = p_ref.shape[0] // 2
    x, y, c, chips = _place()
    sh = 2 * x + y
    sib = (x, y, 1 - c)
    slots = [2 * ch[0] + ch[1] for ch in chips]

    def half(slot, hc):
        return out_ref.at[slot, pl.ds(hc * hf, hf), :]

    def copy(k, src, slot, hc, to):
        return pltpu.make_async_remote_copy(src_ref=src, dst_ref=half(slot, hc), send_sem=ss.at[k],
                                            recv_sem=rs.at[k], device_id=to, device_id_type=MESH)

    def first():
        return [copy(j, p_ref.at[pl.ds(c * hf, hf), :], sh, c, (*ch, c)) for j, ch in enumerate(chips)]

    def passed():
        return [copy(3 + j, half(slots[j], c), slots[j], c, sib) for j in range(3)]

    def start():
        for cp in first():
            cp.start()

    def relay():
        for j, cp in enumerate(passed()):
            copy(j, half(slots[j], c), slots[j], c, (*chips[j], c)).wait_recv()
            cp.start()

    def finish():
        for j in range(3):
            copy(3 + j, half(slots[j], 1 - c), slots[j], 1 - c, sib).wait_recv()
        for cp in first() + passed():
            cp.wait_send()

    return start, relay, finish


GATHER_SEMS = [pltpu.SemaphoreType.DMA((6,)), pltpu.SemaphoreType.DMA((6,))]


def _gather_weights(pack, name):
    def body(p_ref, out_ref, ss, rs):
        for stage in _gather_stages(p_ref, out_ref, ss, rs):
            stage()

    return pl.pallas_call(
        body, name=name, in_specs=[HB], out_specs=HB,
        out_shape=jax.ShapeDtypeStruct((N_SHARDS,) + pack.shape, pack.dtype),
        scratch_shapes=GATHER_SEMS, compiler_params=_cp(),
    )(pack)


def _sibling_swap(g, name):
    hf = g.shape[1] // 2

    def body(g_ref, out_ref, ss, rs):
        x, y, c, _ = _place()
        cp = pltpu.make_async_remote_copy(
            src_ref=g_ref.at[pl.ds(0, N_SHARDS), pl.ds((1 - c) * hf, hf), :], dst_ref=out_ref,
            send_sem=ss, recv_sem=rs, device_id=(x, y, 1 - c), device_id_type=MESH)
        cp.start()
        cp.wait()

    return pl.pallas_call(
        body, name=name, in_specs=[HB], out_specs=HB,
        out_shape=jax.ShapeDtypeStruct((N_SHARDS, hf, D_MODEL), g.dtype),
        scratch_shapes=[pltpu.SemaphoreType.DMA, pltpu.SemaphoreType.DMA],
        compiler_params=_cp(),
    )(g)


def _row_tile(rows, width_bytes, cap_bytes):
    fits = [d for d in range(8, rows + 1, 8) if rows % d == 0 and d * width_bytes <= cap_bytes]
    return max(fits)


def _chip_sum(g, got, c_idx, name):
    hf = got.shape[1]
    tr = _row_tile(hf, D_MODEL * 4, 3 << 20)
    nb = hf // tr

    def body(c_ref, a_ref, b_ref, s_ref, sb_ref):
        s = a_ref[...] + b_ref[...]
        s_ref[...] = s
        sb_ref[...] = s.astype(BF16)

    blk = pl.BlockSpec((1, tr, D_MODEL), lambda s, i, c_ref: (s, i, 0))
    return pl.pallas_call(
        body, name=name,
        grid_spec=pltpu.PrefetchScalarGridSpec(
            num_scalar_prefetch=1, grid=(N_SHARDS, nb),
            in_specs=[pl.BlockSpec((1, tr, D_MODEL), lambda s, i, c_ref: (s, c_ref[0] * nb + i, 0)), blk],
            out_specs=[blk, blk]),
        out_shape=[jax.ShapeDtypeStruct((N_SHARDS, hf, D_MODEL), F32),
                   jax.ShapeDtypeStruct((N_SHARDS, hf, D_MODEL), BF16)],
        compiler_params=_cp(("parallel", "parallel")),
    )(c_idx, g, got)


def _exchange_stages(s_ref, out_ref, ss, rs):
    x, y, c, chips = _place()

    def sends():
        return [pltpu.make_async_remote_copy(src_ref=s_ref.at[2 * ch[0] + ch[1]], dst_ref=out_ref.at[k],
                                             send_sem=ss.at[k], recv_sem=rs.at[k], device_id=(*ch, c),
                                             device_id_type=MESH) for k, ch in enumerate(chips)]

    def start():
        for cp in sends():
            cp.start()

    def finish():
        for cp in sends():
            cp.wait()

    return start, finish


EXCHANGE_SEMS = [pltpu.SemaphoreType.DMA((3,)), pltpu.SemaphoreType.DMA((3,))]


def _total_half(s, got, sh_idx, name):
    hf = got.shape[1]
    tr = _row_tile(hf, D_MODEL * 4, 3 << 20)
    nb = hf // tr

    def body(sh_ref, a_ref, r0, r1, r2, o_ref):
        o_ref[...] = ((a_ref[0] + r0[0].astype(F32)) + r1[0].astype(F32)) + r2[0].astype(F32)

    rspec = lambda k: pl.BlockSpec((1, tr, D_MODEL), lambda i, sh_ref: (k, i, 0))
    return pl.pallas_call(
        body, name=name,
        grid_spec=pltpu.PrefetchScalarGridSpec(
            num_scalar_prefetch=1, grid=(nb,),
            in_specs=[pl.BlockSpec((1, tr, D_MODEL), lambda i, sh_ref: (sh_ref[0], i, 0)),
                      rspec(0), rspec(1), rspec(2)],
            out_specs=pl.BlockSpec((tr, D_MODEL), lambda i, sh_ref: (i, 0))),
        out_shape=jax.ShapeDtypeStruct((hf, D_MODEL), F32),
        compiler_params=_cp(("parallel",)),
    )(sh_idx, s, got, got, got)


def _join_halves(tot, name):
    def body(t_ref, out_ref, ss, rs):
        x, y, c, _ = _place()
        cp = pltpu.make_async_remote_copy(src_ref=t_ref, dst_ref=out_ref, send_sem=ss, recv_sem=rs,
                                          device_id=(x, y, 1 - c), device_id_type=MESH)
        cp.start()
        cp.wait()

    return pl.pallas_call(
        body, name=name, in_specs=[HB], out_specs=HB,
        out_shape=jax.ShapeDtypeStruct(tot.shape, F32),
        scratch_shapes=[pltpu.SemaphoreType.DMA, pltpu.SemaphoreType.DMA],
        compiler_params=_cp(),
    )(tot)


def _w_ada_grad(cond, dmod_cols, name):
    def body(c_ref, d_ref, o_ref):
        cv = c_ref[...]
        cv = cv * _sig(cv)
        o_ref[...] = lax.dot_general(cv, d_ref[...], _DN["tn"], precision=lax.Precision.HIGHEST,
                                     preferred_element_type=F32)

    return pl.pallas_call(
        body, name=name, in_specs=[VM, VM], out_specs=VM,
        out_shape=jax.ShapeDtypeStruct((D_MODEL, dmod_cols.shape[1]), F32), compiler_params=_cp(),
    )(cond, dmod_cols)


def _pad_rows(a, rows):
    return jnp.pad(a, ((0, rows - a.shape[0]), (0, 0)))


def _pad_cols(a, cols):
    return jnp.pad(a, ((0, 0), (0, cols - a.shape[1])))


def _unpack_rest(p):
    o = 0
    out = []
    for r in (R_OUT, R_FF, R_FF, R_FF):
        out.append(p[..., o:o + r, :])
        o += r
    return out


def _reduce_tail(csum, got2, shard, ac, tag):
    tot = _total_half(csum, got2, shard.reshape(1).astype(jnp.int32), "rs_total_" + tag)
    other = _join_halves(tot, "rs_join_" + tag)
    return jnp.where(ac == 0, jnp.concatenate([tot, other], axis=0), jnp.concatenate([other, tot], axis=0))


def _reduce_head(gpack, ac, tag):
    got = _sibling_swap(gpack, "rs_sibling_swap_" + tag)
    return _chip_sum(gpack, got, ac.reshape(1).astype(jnp.int32), "rs_chip_sum_" + tag)


def kernel(x, c, w_ada, b_ada, norm1_w, w_in, conv_w, conv_b, dt_bias, a_log, d_skip, ssd_norm_w, q_norm_w, k_norm_w, w_out, norm2_w, w_gate, w_up, w_down, loss_target, m_w_ada, m_b_ada, m_norm1_w, m_w_in, m_conv_w, m_conv_b, m_dt_bias, m_a_log, m_d_skip, m_ssd_norm_w, m_q_norm_w, m_k_norm_w, m_w_out, m_norm2_w, m_w_gate, m_w_up, m_w_down, v_w_ada, v_b_ada, v_norm1_w, v_w_in, v_conv_w, v_conv_b, v_dt_bias, v_a_log, v_d_skip, v_ssd_norm_w, v_q_norm_w, v_k_norm_w, v_w_out, v_norm2_w, v_w_gate, v_w_up, v_w_down):
    cst = _consts()
    ax, ay, ac = lax.axis_index("x"), lax.axis_index("y"), lax.axis_index("c")
    shard = 2 * ax + ay
    me = 4 * ax + 2 * ay + ac
    xs = x[0]
    tgt = loss_target[0]
    w_in_cols = w_in.shape[2]
    conv_cols = conv_w.shape[2]

    cw_flat = _pad_cols(conv_w[0].reshape(1, -1), 2 * D_MODEL).reshape(2, D_MODEL)
    cpack = jnp.concatenate([jnp.broadcast_to(c, (8, D_MODEL)), _pad_rows(cw_flat, 8)], axis=0)
    gat = _small_gather(cpack, "gather_c", reduce=False)[0]
    c_all = gat[:, 0, :]
    cw = gat[0::2, 8:10, :].reshape(N_SHARDS, 2 * D_MODEL)[:, :4 * conv_cols].reshape(N_SHARDS, 4, conv_cols)
    conv_w_full = jnp.transpose(cw, (1, 0, 2)).reshape(4, D_CONV)

    mod_w = 6 * D_MODEL // N_SHARDS
    b_shard = lax.dynamic_slice(b_ada, (0, shard * mod_w), (1, mod_w))
    modp = _mod_exchange(c_all, w_ada[0], b_shard, "mod_exchange")
    mod_mine = lax.dynamic_slice(modp, (0, me, 0), (N_SHARDS, 1, mod_w)).reshape(6, D_MODEL)
    mod = _pad_rows(mod_mine, 8)

    tr3 = lambda a: jnp.transpose(a, (0, 2, 1))
    lin = lambda a: tr3(a).reshape(-1, 128)
    unlin = lambda a: tr3(a.reshape(1, w_in_cols, D_MODEL))
    wpack_in = _pad_rows(tr3(w_in.astype(BF16))[0], R_IN)
    wpack_rest = jnp.concatenate([w_out[0], tr3(w_gate)[0], tr3(w_up)[0], w_down[0]], axis=0).astype(BF16)
    own = lax.broadcasted_iota(jnp.int32, (N_SHARDS, 1, 1), 0) == shard
    p_in = jnp.where(own, wpack_in[None], _gather_weights(wpack_in, "gather_w_in"))
    wi_t = p_in[:, :w_in_cols, :].reshape(D_IN_PROJ, D_MODEL)
    w_inp_t = jnp.concatenate([wi_t[0:1024], wi_t[2576:5648], wi_t[1024:2560], wi_t[2560:2576],
                               jnp.zeros((112, D_MODEL), BF16)], axis=0)

    pad128 = lambda a: _pad_cols(a, 128)
    dtb, alog = pad128(dt_bias), pad128(a_log)
    dsk = jnp.repeat(d_skip, HEAD_DIM, axis=1)
    qw, kw = jnp.tile(q_norm_w, (1, N_HEADS)), jnp.tile(k_norm_w, (1, N_HEADS))

    h1 = _norm_mod(xs, norm1_w, mod, 0, "norm1")
    proj = _matmul(h1, w_inp_t, "nt", F32, "in_proj")
    u = _conv_fwd(proj, conv_w_full, conv_b, "conv_fwd")
    y_ssd, yn, prev = _ssd_fwd(u, proj, dtb, alog, dsk, ssd_norm_w, cst, "ssd_fwd")
    qs, kn, vb = _qk_fwd(proj, qw, kw, cst, "qk_norm")
    rtot, ob, cnt, gp_rest = _sb_fwd(qs, kn, vb, wpack_rest, cst, "sb_fwd")
    p_out, p_gate, p_up, p_down = _unpack_rest(jnp.where(own, wpack_rest[None], gp_rest))
    w_o = p_out.reshape(2 * D_MODEL, D_MODEL)
    w_gu_t = jnp.concatenate([p_gate.reshape(D_FF, D_MODEL), p_up.reshape(D_FF, D_MODEL)], axis=0)
    w_d = p_down.reshape(D_FF, D_MODEL)
    ycat = jnp.concatenate([yn, ob], axis=1)
    mix = _matmul(ycat, w_o, "nn", F32, "out_proj")
    x1, h2 = _resid_norm(xs, mix, norm2_w, mod, "resid_norm2")
    gu = _matmul(h2, w_gu_t, "nt", BF16, "ffn_in")
    act = _act_fwd(gu, "ffn_act")
    ffn = _matmul(act, w_d, "nn", F32, "ffn_out", tk_cap=1408)
    dffn, dout, dg2, loss8 = _loss_head(x1, ffn, tgt, mod, "loss_head")
    loss = lax.psum(loss8[0, 0], ("x", "y", "c"))

    dact = _matmul(dffn, w_d, "nt", BF16, "d_act")
    g_down = _matmul(act, dffn, "tn", F32, "g_w_down", tm_cap=1408)
    dgu = _act_bwd(dact, gu, "ffn_act_bwd")
    dh2 = _matmul(dgu, w_gu_t, "nn", F32, "d_h2", tk_cap=1408)
    g_gu_t = _matmul(dgu, h2, "tn", F32, "g_w_gu", tm_cap=1408)
    dx1, dmix, acc2 = _norm_bwd(dh2, x1, dout, mix, norm2_w, mod, 3, 2, "norm2_bwd")
    dycat = _matmul(dmix, w_o, "nt", F32, "d_ycat")
    g_out = _matmul(ycat, dmix, "tn", F32, "g_w_out")
    gpack_rest = jnp.concatenate([
        g_out.reshape(N_SHARDS, R_OUT, D_MODEL),
        g_gu_t[:D_FF].reshape(N_SHARDS, R_FF, D_MODEL), g_gu_t[D_FF:].reshape(N_SHARDS, R_FF, D_MODEL),
        g_down.reshape(N_SHARDS, R_FF, D_MODEL)], axis=1)
    csum_r, csum_rb = _reduce_head(gpack_rest, ac, "rest")
    du, ddt, dz, acc_ssd, acc16 = _ssd_bwd(u, proj, y_ssd, prev, dycat, dtb, alog, dsk, ssd_norm_w, cst, "ssd_bwd")
    dqs, dkn, dv, got_r = _sb_bwd(qs, kn, vb, rtot, cnt, dycat, csum_rb, cst, "sb_bwd")
    r_out, r_gate, r_up, r_down = _unpack_rest(_reduce_tail(csum_r, got_r, shard, ac, "rest"))
    dq, dk, dvb, acc_qk = _qk_bwd(proj, dqs, dkn, dv, qw, kw, cst, "qk_norm_bwd")
    dxbc, g_conv_w, g_conv_b = _conv_bwd(proj, du, conv_w_full, conv_b, "conv_bwd")
    dproj = jnp.concatenate([dz, dq, dk, dvb, dxbc, ddt], axis=1)
    g_inp_t = _matmul(dproj, h1, "tn", F32, "g_w_in", tm_cap=1920)
    gi_t = jnp.concatenate([g_inp_t[0:1024], g_inp_t[4096:5632], g_inp_t[5632:5648], g_inp_t[1024:4096]], axis=0)
    gpack_in = jnp.pad(gi_t.reshape(N_SHARDS, w_in_cols, D_MODEL), ((0, 0), (0, R_IN - w_in_cols), (0, 0)))
    csum_i, csum_ib = _reduce_head(gpack_in, ac, "in")
    dh1, got_i = _matmul(dproj, w_inp_t, "nn", F32, "d_h1", tk_cap=1152, exchange=csum_ib)
    r_in = _reduce_tail(csum_i, got_i, shard, ac, "in")
    grad_x, acc1 = _norm_bwd(dh1, xs, dx1, None, norm1_w, mod, 0, None, "norm1_bwd")

    last = jnp.concatenate([acc_qk[0:1, 0:64], acc_qk[1:2, 0:64], acc16[0:1, 0:16], acc16[1:2, 0:16],
                            acc16[2:3, 0:16]], axis=1)
    spack = jnp.concatenate([
        acc1[0:2], acc2[3:4], acc2[0:2], dg2,
        acc1[2:3], acc2[2:3], acc_ssd[0:1],
        _pad_cols(g_conv_b, 2 * D_MODEL).reshape(2, D_MODEL),
        g_conv_w.reshape(6, D_MODEL),
        _pad_cols(last, D_MODEL)], axis=0)
    sgat, ssum = _small_gather(_pad_rows(spack, SMALL_ROWS), "gather_small", reduce=True)
    g_b_ada = ssum[0:6].reshape(1, 6 * D_MODEL)
    g_norm1, g_norm2, g_ssdn = ssum[6:7], ssum[7:8], ssum[8:9]
    g_cb = ssum[9:11].reshape(1, 2 * D_MODEL)[:, :D_CONV]
    g_cw = lax.dynamic_slice(ssum[11:17].reshape(4, D_CONV), (0, shard * conv_cols), (4, conv_cols))
    g_qn, g_kn = ssum[17:18, 0:64], ssum[17:18, 64:128]
    g_dtb, g_alog, g_dsk = ssum[17:18, 128:144], ssum[17:18, 144:160], ssum[17:18, 160:176]
    dmod_all = sgat[:, 0:6, :].reshape(8, 6 * D_MODEL)
    g_w_ada = _w_ada_grad(c_all, lax.dynamic_slice(dmod_all, (0, shard * mod_w), (8, mod_w)), "g_w_ada")


    grads = dict(w_ada=g_w_ada, b_ada=g_b_ada, norm1_w=g_norm1, w_in=r_in[:w_in_cols].reshape(-1, 128), conv_w=g_cw,
                 conv_b=g_cb, dt_bias=g_dtb, a_log=g_alog, d_skip=g_dsk, ssd_norm_w=g_ssdn, q_norm_w=g_qn,
                 k_norm_w=g_kn, w_out=r_out, norm2_w=g_norm2, w_gate=r_gate, w_up=r_up, w_down=r_down)
    weights = dict(w_ada=(w_ada, m_w_ada, v_w_ada), b_ada=(b_ada, m_b_ada, v_b_ada),
                   norm1_w=(norm1_w, m_norm1_w, v_norm1_w), w_in=(w_in, m_w_in, v_w_in),
                   conv_w=(conv_w, m_conv_w, v_conv_w), conv_b=(conv_b, m_conv_b, v_conv_b),
                   dt_bias=(dt_bias, m_dt_bias, v_dt_bias), a_log=(a_log, m_a_log, v_a_log),
                   d_skip=(d_skip, m_d_skip, v_d_skip), ssd_norm_w=(ssd_norm_w, m_ssd_norm_w, v_ssd_norm_w),
                   q_norm_w=(q_norm_w, m_q_norm_w, v_q_norm_w), k_norm_w=(k_norm_w, m_k_norm_w, v_k_norm_w),
                   w_out=(w_out, m_w_out, v_w_out), norm2_w=(norm2_w, m_norm2_w, v_norm2_w),
                   w_gate=(w_gate, m_w_gate, v_w_gate), w_up=(w_up, m_w_up, v_w_up),
                   w_down=(w_down, m_w_down, v_w_down))
    views = dict(w_in=(lin, unlin), w_gate=(tr3, tr3), w_up=(tr3, tr3))
    same = lambda a: a
    names = list(weights)
    g_out_l, d_out_l, m_out_l, v_out_l = [], [], [], []
    for n in names:
        view, back = views.get(n, (same, same))
        w, m, v = (view(a) for a in weights[n])
        g = grads[n].reshape(w.shape)
        d, nm, nv = _adamw(w, g, m, v, "adamw_" + n)
        g_out_l.append(back(g))
        d_out_l.append(back(d))
        m_out_l.append(back(nm))
        v_out_l.append(back(nv))
    return (loss, grad_x[None], *g_out_l, *d_out_l, *m_out_l, *v_out_l)
```

```python
import functools

import numpy as np
import jax
import jax.numpy as jnp
from jax import lax
from jax.experimental import pallas as pl
from jax.experimental.pallas import tpu as pltpu

F32, BF16 = jnp.float32, jnp.bfloat16
MESH = pl.DeviceIdType.MESH

D_MODEL = 1024
HEAD_DIM = 64
N_HEADS = 16
D_CONV = 1536
D_FF = 2816
D_IN_PROJ = 5648
D_PROJ_PAD = 5760
CHUNK = 128
SB_TILE = 256
SB_PAIRS = 2
SB_LANES = 128 * SB_PAIRS
SB_DEAD = -105.0
EPS = 1e-6
N_SHARDS = 4
R_IN, R_OUT, R_FF = 1440, 512, 704
SMALL_ROWS = 24

ADAM_LR, ADAM_B1, ADAM_B2, ADAM_EPS, ADAM_WD, ADAM_STEP = 0.001, 0.9, 0.999, 1e-08, 0.01, 10

VMEM_LIMIT = 48 * 1024 * 1024
ADAM_BLOCK_BYTES = 3 * 512 * 1024

_DN = {"nn": (((1,), (0,)), ((), ())), "nt": (((1,), (1,)), ((), ())), "tn": (((0,), (0,)), ((), ()))}


def _dot(a, b, dims="nn"):
    return lax.dot_general(a, b, _DN[dims], preferred_element_type=F32)


def _pieces(x, n):
    out = []
    for _ in range(n - 1):
        hi = lax.bitcast_convert_type(lax.bitcast_convert_type(x, jnp.int32) & jnp.int32(-65536), F32)
        out.append(hi.astype(BF16))
        x = x - hi
    out.append(x.astype(BF16))
    return out


def _dotx_r(x, b_exact, n=3):
    return _dot(jnp.concatenate(_pieces(x, n), axis=1), jnp.concatenate([b_exact] * n, axis=0))


def _dotx_l(a_exact, x, n=3):
    return _dot(jnp.concatenate([a_exact] * n, axis=1), jnp.concatenate(_pieces(x, n), axis=0))


def _dot2(x, b2):
    return _dot(jnp.concatenate(_pieces(x, 2), axis=1), b2)


def _sig(x):
    return 1.0 / (1.0 + jnp.exp(-x))


def _softplus(x):
    return jnp.maximum(x, 0.0) + jnp.log(1.0 + jnp.exp(-jnp.abs(x)))


def _cp(sem=None, vmem=VMEM_LIMIT):
    return pltpu.CompilerParams(dimension_semantics=sem, vmem_limit_bytes=vmem)


def _colsum(x):
    return jnp.sum(x, axis=0, keepdims=True)


def _consts():
    ch = np.arange(D_MODEL)
    expand = (np.arange(128)[:, None] == (ch // HEAD_DIM)[None, :]).astype(np.float32)
    fold = (ch[:, None] % HEAD_DIM == np.arange(128)[None, :]).astype(np.float32)
    i = np.arange(CHUNK)
    tril = (i[:, None] >= i[None, :]).astype(np.float32)
    j = np.arange(SB_TILE)
    ustrict = (j[:, None] > j[None, :]).astype(np.float32)
    ule = (j[:, None] <= j[None, :]).astype(np.float32)
    ult = (j[:, None] < j[None, :]).astype(np.float32)
    c = lambda a: jnp.asarray(a, BF16)
    return dict(expand=c(expand), hsum=c(expand.T), fold=c(fold), tril=c(tril), triu=c(tril.T),
                ustrict=ustrict, ule=ule, ult=ult)


def _doubled(tri, tk):
    b = tri[:tk, :tk]
    return jnp.asarray(np.concatenate([b, b], axis=0), BF16)


def _pick(n, cap):
    best = 128
    for t in range(128, min(n, cap) + 1, 128):
        if n % t == 0:
            best = t
    return n if n <= cap else best


def _matmul(a, b, dims, out_dtype, name, tm_cap=1024, tn_cap=2048, tk_cap=1024, exchange=None):
    if dims == "nn":
        (m, k), (_, n) = a.shape, b.shape
    elif dims == "nt":
        (m, k), (n, _) = a.shape, b.shape
    else:
        (k, m), (_, n) = a.shape, b.shape
    tm, tn, tk = _pick(m, tm_cap), _pick(n, tn_cap), _pick(k, tk_cap)
    nk = k // tk
    a_spec = (pl.BlockSpec((tk, tm), lambda i, j, kk: (kk, i)) if dims == "tn"
              else pl.BlockSpec((tm, tk), lambda i, j, kk: (i, kk)))
    b_spec = (pl.BlockSpec((tn, tk), lambda i, j, kk: (j, kk)) if dims == "nt"
              else pl.BlockSpec((tk, tn), lambda i, j, kk: (kk, j)))

    grid = (m // tm, n // tn, nk)

    def body(a_ref, b_ref, *rest):
        if exchange is None:
            o_ref, acc_ref = rest
        else:
            xs_ref, o_ref, xr_ref, acc_ref, xss, xrs = rest
            ids = [pl.program_id(d) for d in range(3)]
            x_start, x_finish = _exchange_stages(xs_ref, xr_ref, xss, xrs)
            pl.when(functools.reduce(jnp.logical_and, [p == 0 for p in ids]))(x_start)
        kk = pl.program_id(2)
        part = _dot(a_ref[...], b_ref[...], dims)
        if nk == 1:
            o_ref[...] = part.astype(out_dtype)
        else:
            @pl.when(kk == 0)
            def _():
                acc_ref[...] = part

            @pl.when(kk > 0)
            def _():
                acc_ref[...] += part

            @pl.when(kk == nk - 1)
            def _():
                o_ref[...] = acc_ref[...].astype(out_dtype)
        if exchange is not None:
            pl.when(functools.reduce(jnp.logical_and, [p == g - 1 for p, g in zip(ids, grid)]))(x_finish)

    in_specs = [a_spec, b_spec]
    out_specs = [pl.BlockSpec((tm, tn), lambda i, j, kk: (i, j))]
    out_shape = [jax.ShapeDtypeStruct((m, n), out_dtype)]
    scratch = [pltpu.VMEM((tm, tn) if nk > 1 else (8, 128), F32)]
    args = [a, b]
    if exchange is not None:
        in_specs.append(HB)
        out_specs.append(HB)
        out_shape.append(jax.ShapeDtypeStruct((3,) + exchange.shape[1:], exchange.dtype))
        scratch += EXCHANGE_SEMS
        args.append(exchange)
    out = pl.pallas_call(
        body, name=name, grid=grid, in_specs=in_specs, out_specs=out_specs, out_shape=out_shape,
        scratch_shapes=scratch,
        compiler_params=_cp(("parallel", "parallel", "arbitrary") if exchange is None else ("arbitrary",) * 3),
    )(*args)
    return out[0] if exchange is None else out


def _row_spec(tm, width=D_MODEL, col=0):
    return pl.BlockSpec((tm, width), lambda i: (i, col))


def _fix_spec(shape):
    return pl.BlockSpec(shape, lambda *_: (0,) * len(shape))


def _norm_mod(x, nw, mod, row_sh, name):
    t = x.shape[0]
    tm = min(t, 512)

    def body(x_ref, nw_ref, mod_ref, h_ref):
        xv = x_ref[...]
        r = lax.rsqrt(jnp.mean(xv * xv, axis=-1, keepdims=True) + EPS)
        sh = mod_ref[row_sh:row_sh + 1, :]
        sc = mod_ref[row_sh + 1:row_sh + 2, :]
        h_ref[...] = (xv * r * nw_ref[...] * (1.0 + sc) + sh).astype(BF16)

    return pl.pallas_call(
        body, name=name, grid=(t // tm,),
        in_specs=[_row_spec(tm), _fix_spec((1, D_MODEL)), _fix_spec((8, D_MODEL))],
        out_specs=_row_spec(tm), out_shape=jax.ShapeDtypeStruct((t, D_MODEL), BF16),
        compiler_params=_cp(("parallel",)),
    )(x, nw, mod)


def _resid_norm(x, mix, nw, mod, name):
    t = x.shape[0]
    tm = min(t, 512)

    def body(x_ref, mix_ref, nw_ref, mod_ref, x1_ref, h_ref):
        x1 = x_ref[...] + mod_ref[2:3, :] * mix_ref[...]
        x1_ref[...] = x1
        r = lax.rsqrt(jnp.mean(x1 * x1, axis=-1, keepdims=True) + EPS)
        h_ref[...] = (x1 * r * nw_ref[...] * (1.0 + mod_ref[4:5, :]) + mod_ref[3:4, :]).astype(BF16)

    return pl.pallas_call(
        body, name=name, grid=(t // tm,),
        in_specs=[_row_spec(tm), _row_spec(tm), _fix_spec((1, D_MODEL)), _fix_spec((8, D_MODEL))],
        out_specs=[_row_spec(tm), _row_spec(tm)],
        out_shape=[jax.ShapeDtypeStruct((t, D_MODEL), F32), jax.ShapeDtypeStruct((t, D_MODEL), BF16)],
        compiler_params=_cp(("parallel",)),
    )(x, mix, nw, mod)


def _act_fwd(gu, name):
    t = gu.shape[0]
    tm, tn = min(t, 512), D_FF // 2
    nb = D_FF // tn

    def body(g_ref, u_ref, a_ref):
        g = g_ref[...].astype(F32)
        a_ref[...] = (g * _sig(g) * u_ref[...].astype(F32)).astype(BF16)

    return pl.pallas_call(
        body, name=name, grid=(t // tm, nb),
        in_specs=[pl.BlockSpec((tm, tn), lambda i, j: (i, j)), pl.BlockSpec((tm, tn), lambda i, j: (i, j + nb))],
        out_specs=pl.BlockSpec((tm, tn), lambda i, j: (i, j)),
        out_shape=jax.ShapeDtypeStruct((t, D_FF), BF16),
        compiler_params=_cp(("parallel", "parallel")),
    )(gu, gu)


def _act_bwd(dact, gu, name):
    t = gu.shape[0]
    tm = min(t, 256)

    def body(d_ref, g_ref, u_ref, o_ref):
        g, d = g_ref[...].astype(F32), d_ref[...].astype(F32)
        s = _sig(g)
        o_ref[:, 0:D_FF] = (d * u_ref[...].astype(F32) * s * (1.0 + g * (1.0 - s))).astype(BF16)
        o_ref[:, D_FF:2 * D_FF] = (d * g * s).astype(BF16)

    return pl.pallas_call(
        body, name=name, grid=(t // tm,),
        in_specs=[pl.BlockSpec((tm, D_FF), lambda i: (i, 0)), pl.BlockSpec((tm, D_FF), lambda i: (i, 0)),
                  pl.BlockSpec((tm, D_FF), lambda i: (i, 1))],
        out_specs=pl.BlockSpec((tm, 2 * D_FF), lambda i: (i, 0)),
        out_shape=jax.ShapeDtypeStruct((t, 2 * D_FF), BF16),
        compiler_params=_cp(("parallel",)),
    )(dact, gu, gu)


def _loss_head(x1, ffn, tgt, mod, name):
    t = x1.shape[0]
    tm = min(t, 512)

    def body(x1_ref, f_ref, t_ref, mod_ref, dffn_ref, dout_ref, dg2_ref, loss_ref):
        i = pl.program_id(0)
        g2 = mod_ref[5:6, :]
        f = f_ref[...]
        err = x1_ref[...] + g2 * f - t_ref[...]
        dout = err * (1.0 / D_MODEL)
        dout_ref[...] = dout
        dffn_ref[...] = (dout * g2).astype(BF16)
        part = jnp.zeros((8, 128), F32) + 0.5 * jnp.sum(jnp.mean(err * err, axis=-1, keepdims=True))

        @pl.when(i == 0)
        def _():
            dg2_ref[...] = _colsum(dout * f)
            loss_ref[...] = part

        @pl.when(i > 0)
        def _():
            dg2_ref[...] += _colsum(dout * f)
            loss_ref[...] += part

    return pl.pallas_call(
        body, name=name, grid=(t // tm,),
        in_specs=[_row_spec(tm), _row_spec(tm), _row_spec(tm), _fix_spec((8, D_MODEL))],
        out_specs=[_row_spec(tm), _row_spec(tm), _fix_spec((1, D_MODEL)), _fix_spec((8, 128))],
        out_shape=[jax.ShapeDtypeStruct((t, D_MODEL), BF16), jax.ShapeDtypeStruct((t, D_MODEL), F32),
                   jax.ShapeDtypeStruct((1, D_MODEL), F32), jax.ShapeDtypeStruct((8, 128), F32)],
        compiler_params=_cp(("arbitrary",)),
    )(x1, ffn, tgt, mod)


def _norm_bwd(dh, xin, dres, aux, nw, mod, row_sh, gate_row, name):
    t = xin.shape[0]
    tm = min(t, 512)
    with_gate = gate_row is not None

    def body(*refs):
        if with_gate:
            dh_ref, x_ref, dr_ref, aux_ref, nw_ref, mod_ref, dx_ref, dg_ref, acc_ref = refs
        else:
            dh_ref, x_ref, dr_ref, nw_ref, mod_ref, dx_ref, acc_ref = refs
        i = pl.program_id(0)
        xv, dhv = x_ref[...], dh_ref[...]
        r = lax.rsqrt(jnp.mean(xv * xv, axis=-1, keepdims=True) + EPS)
        xn = xv * r
        nwv = nw_ref[...]
        sc1 = 1.0 + mod_ref[row_sh + 1:row_sh + 2, :]
        dxn = dhv * (nwv * sc1)
        dx = dr_ref[...] + r * (dxn - xn * jnp.mean(dxn * xn, axis=-1, keepdims=True))
        dx_ref[...] = dx
        dhx = dhv * xn
        rows = [_colsum(dhv), _colsum(dhx * nwv), _colsum(dhx * sc1)]
        if with_gate:
            dg_ref[...] = (dx * mod_ref[gate_row:gate_row + 1, :]).astype(BF16)
            rows.append(_colsum(dx * aux_ref[...]))

        @pl.when(i == 0)
        def _():
            acc_ref[...] = jnp.zeros_like(acc_ref)

        for k, v in enumerate(rows):
            acc_ref[k:k + 1, :] += v

    ins = [dh, xin, dres] + ([aux] if with_gate else []) + [nw, mod]
    in_specs = [_row_spec(tm)] * (4 if with_gate else 3) + [_fix_spec((1, D_MODEL)), _fix_spec((8, D_MODEL))]
    out_specs = [_row_spec(tm)] + ([_row_spec(tm)] if with_gate else []) + [_fix_spec((8, D_MODEL))]
    out_shape = ([jax.ShapeDtypeStruct((t, D_MODEL), F32)]
                 + ([jax.ShapeDtypeStruct((t, D_MODEL), BF16)] if with_gate else [])
                 + [jax.ShapeDtypeStruct((8, D_MODEL), F32)])
    return pl.pallas_call(
        body, name=name, grid=(t // tm,), in_specs=in_specs, out_specs=out_specs, out_shape=out_shape,
        compiler_params=_cp(("arbitrary",)),
    )(*ins)


XBC_COL0 = 4096 // 128
DT_COL = 5632 // 128


def _conv_pre(xv, w_ref, b_ref):
    t = xv.shape[0]
    row = lax.broadcasted_iota(jnp.int32, xv.shape, 0)
    pre = xv * w_ref[3:4, :] + b_ref[...]
    shifted = []
    for k in range(3):
        s = 3 - k
        xs = jnp.where(row >= s, pltpu.roll(xv, s, 0), 0.0)
        shifted.append(xs)
        pre = pre + xs * w_ref[k:k + 1, :]
    return pre, shifted, row, t


def _conv_fwd(proj, conv_w, conv_b, name):
    t = proj.shape[0]

    def body(x_ref, w_ref, b_ref, u_ref):
        pre, _, _, _ = _conv_pre(x_ref[...], w_ref, b_ref)
        u_ref[...] = pre * _sig(pre)

    return pl.pallas_call(
        body, name=name, grid=(D_CONV // 128,),
        in_specs=[pl.BlockSpec((t, 128), lambda j: (0, XBC_COL0 + j)), pl.BlockSpec((4, 128), lambda j: (0, j)),
                  pl.BlockSpec((1, 128), lambda j: (0, j))],
        out_specs=pl.BlockSpec((t, 128), lambda j: (0, j)),
        out_shape=jax.ShapeDtypeStruct((t, D_CONV), F32),
        compiler_params=_cp(("parallel",)),
    )(proj, conv_w, conv_b)


def _conv_bwd(proj, du, conv_w, conv_b, name):
    t = proj.shape[0]

    def body(x_ref, du_ref, w_ref, b_ref, dx_ref, dw_ref, db_ref):
        pre, shifted, row, _ = _conv_pre(x_ref[...], w_ref, b_ref)
        s = _sig(pre)
        dpre = du_ref[...] * s * (1.0 + pre * (1.0 - s))
        db_ref[...] = _colsum(dpre)
        dx = dpre * w_ref[3:4, :]
        dw_ref[3:4, :] = _colsum(dpre * x_ref[...])
        for k in range(3):
            sft = 3 - k
            dw_ref[k:k + 1, :] = _colsum(dpre * shifted[k])
            back = jnp.where(row < t - sft, pltpu.roll(dpre, t - sft, 0), 0.0)
            dx = dx + back * w_ref[k:k + 1, :]
        dx_ref[...] = dx.astype(BF16)

    return pl.pallas_call(
        body, name=name, grid=(D_CONV // 128,),
        in_specs=[pl.BlockSpec((t, 128), lambda j: (0, XBC_COL0 + j)), pl.BlockSpec((t, 128), lambda j: (0, j)),
                  pl.BlockSpec((4, 128), lambda j: (0, j)), pl.BlockSpec((1, 128), lambda j: (0, j))],
        out_specs=[pl.BlockSpec((t, 128), lambda j: (0, j)), pl.BlockSpec((4, 128), lambda j: (0, j)),
                   pl.BlockSpec((1, 128), lambda j: (0, j))],
        out_shape=[jax.ShapeDtypeStruct((t, D_CONV), BF16), jax.ShapeDtypeStruct((4, D_CONV), F32),
                   jax.ShapeDtypeStruct((1, D_CONV), F32)],
        compiler_params=_cp(("parallel",)),
    )(proj, du, conv_w, conv_b)


def _ssd_common(dtraw_ref, dtb_ref, alog_ref, tril, expand, l_s, lt_s):
    lane = lax.broadcasted_iota(jnp.int32, (1, 128), 1)
    dt = _softplus(dtraw_ref[...] + dtb_ref[...])
    a = jnp.where(lane < N_HEADS, -jnp.exp(alog_ref[...]), 0.0)
    lcs = _dotx_l(tril, dt * a)
    l_s[...] = lcs
    lt_s[...] = lcs.T
    llast = l_s[CHUNK - 1:CHUNK, :]
    ea = _dotx_r(jnp.exp(lcs), expand, n=2)
    ds = _dotx_r(jnp.exp(llast - lcs), expand, n=2)
    dtx = _dotx_r(dt, expand, n=2)
    return dt, a, lcs, llast, ea, ds, dtx


def _head_col(lcs, h):
    lane = lax.broadcasted_iota(jnp.int32, lcs.shape, 1)
    return jnp.sum(jnp.where(lane == h, lcs, 0.0), axis=1, keepdims=True)


def _decay(lcs, lt_s, h, causal):
    seg = _head_col(lcs, h) - lt_s[h:h + 1, :]
    return jnp.exp(jnp.where(causal, seg, -1e30))


def _ssd_fwd(u, proj, dtb, alog, dsk, nw, cst, name):
    t = u.shape[0]
    nc = t // CHUNK

    def body(xs_ref, b_ref, c_ref, dtraw_ref, z_ref, dtb_ref, alog_ref, dsk_ref, nw_ref, tril_ref, exp_ref,
             y_ref, yn_ref, prev_ref, carry, l_s, lt_s, yd_s):
        i = pl.program_id(0)

        @pl.when(i == 0)
        def _():
            carry[...] = jnp.zeros_like(carry)

        expand = exp_ref[...]
        dt, a, lcs, llast, ea, ds, dtx = _ssd_common(dtraw_ref, dtb_ref, alog_ref, tril_ref[...], expand, l_s, lt_s)
        xs = xs_ref[...]
        xg = xs * dtx
        xgb = xg.astype(BF16)
        xgd = (xg * ds).astype(BF16)
        prev = carry[...]
        prev_ref[0] = prev
        prevb = prev.astype(BF16)
        ri = lax.broadcasted_iota(jnp.int32, (CHUNK, CHUNK), 0)
        ci = lax.broadcasted_iota(jnp.int32, (CHUNK, CHUNK), 1)
        causal = ri >= ci
        lane = lax.broadcasted_iota(jnp.int32, (1, 128), 1)
        new_states, yoff = [], []
        for g in range(2):
            bg = b_ref[:, g * 128:(g + 1) * 128].astype(BF16)
            cg = c_ref[:, g * 128:(g + 1) * 128].astype(BF16)
            sc = _dot(cg, bg, "nt")
            gs = slice(g * 512, (g + 1) * 512)
            new_states.append(_dot(bg, xgd[:, gs], "tn"))
            yoff.append(_dot(cg, prevb[:, gs]))
            for pr in range(4):
                col = g * 512 + pr * 128
                xp = xgb[:, col:col + 128]
                acc = jnp.zeros((CHUNK, 128), F32)
                for half in range(2):
                    h = g * 8 + pr * 2 + half
                    m = (sc * _decay(lcs, lt_s, h, causal)).astype(BF16)
                    keep = (lane < HEAD_DIM) if half == 0 else (lane >= HEAD_DIM)
                    acc = acc + _dot(m, jnp.where(keep, xp, jnp.zeros_like(xp)))
                yd_s[:, col:col + 128] = acc
        y = yd_s[...] + jnp.concatenate(yoff, axis=1) * ea + xs * dsk_ref[...]
        y_ref[...] = y
        carry[...] = prev * jnp.max(_dotx_r(jnp.exp(llast) + jnp.zeros((8, 128), F32), expand, n=2), axis=0, keepdims=True) \
            + jnp.concatenate(new_states, axis=1)
        z = z_ref[...]
        yz = y * (z * _sig(z))
        nwv = nw_ref[...]
        for g in range(2):
            gs = slice(g * 512, (g + 1) * 512)
            v = yz[:, gs]
            r = lax.rsqrt(jnp.mean(v * v, axis=-1, keepdims=True) + EPS)
            yn_ref[:, gs] = (v * r * nwv[:, gs]).astype(BF16)

    row = lambda w, col: pl.BlockSpec((CHUNK, w), lambda i: (i, col))
    return pl.pallas_call(
        body, name=name, grid=(nc,),
        in_specs=[row(1024, 0), row(256, 4), row(256, 5), row(128, DT_COL), row(1024, 0),
                  _fix_spec((1, 128)), _fix_spec((1, 128)), _fix_spec((1, D_MODEL)), _fix_spec((1, D_MODEL)),
                  _fix_spec((CHUNK, CHUNK)), _fix_spec((128, D_MODEL))],
        out_specs=[row(1024, 0), row(1024, 0), pl.BlockSpec((1, 128, D_MODEL), lambda i: (i, 0, 0))],
        out_shape=[jax.ShapeDtypeStruct((t, D_MODEL), F32), jax.ShapeDtypeStruct((t, D_MODEL), BF16),
                   jax.ShapeDtypeStruct((nc, 128, D_MODEL), F32)],
        scratch_shapes=[pltpu.VMEM((128, D_MODEL), F32), pltpu.VMEM((128, 128), F32), pltpu.VMEM((128, 128), F32),
                        pltpu.VMEM((CHUNK, D_MODEL), F32)],
        compiler_params=_cp(("arbitrary",)),
    )(u, u, u, proj, proj, dtb, alog, dsk, nw, cst["tril"], cst["expand"])


def _ssd_bwd(u, proj, y, prev, dycat, dtb, alog, dsk, nw, cst, name):
    t = u.shape[0]
    nc = t // CHUNK

    def body(xs_ref, b_ref, c_ref, dtraw_ref, z_ref, y_ref, prev_ref, dyn_ref, dtb_ref, alog_ref, dsk_ref, nw_ref,
             tril_ref, triu_ref, exp_ref, hs_ref,
             du_ref, ddt_ref, dz_ref, acc_ref, acc16_ref, dcarry, l_s, lt_s, dxg_s):
        i = pl.program_id(0)

        @pl.when(i == 0)
        def _():
            dcarry[...] = jnp.zeros_like(dcarry)
            acc_ref[...] = jnp.zeros_like(acc_ref)
            acc16_ref[...] = jnp.zeros_like(acc16_ref)

        expand, hsum = exp_ref[...], hs_ref[...]
        dt, a, lcs, llast, ea, ds, dtx = _ssd_common(dtraw_ref, dtb_ref, alog_ref, tril_ref[...], expand, l_s, lt_s)
        xs = xs_ref[...]
        xg = xs * dtx
        xgb = xg.astype(BF16)
        xgdf = xg * ds
        xgd = xgdf.astype(BF16)
        dsk_v, nwv = dsk_ref[...], nw_ref[...]
        z, y = z_ref[...], y_ref[...]
        sz = _sig(z)
        silz = z * sz
        yz = y * silz
        dyn = dyn_ref[...]
        dyz_parts, dnw_parts = [], []
        for g in range(2):
            gs = slice(g * 512, (g + 1) * 512)
            v = yz[:, gs]
            r = lax.rsqrt(jnp.mean(v * v, axis=-1, keepdims=True) + EPS)
            yhat = v * r
            dnw_parts.append(_colsum(dyn[:, gs] * yhat))
            dw = dyn[:, gs] * nwv[:, gs]
            dyz_parts.append(r * (dw - yhat * jnp.mean(dw * yhat, axis=-1, keepdims=True)))
        dyz = jnp.concatenate(dyz_parts, axis=1)
        dy = dyz * silz
        dz_ref[...] = (dyz * y * (sz * (1.0 + z * (1.0 - sz)))).astype(BF16)
        acc_ref[0:1, :] += jnp.concatenate(dnw_parts, axis=1)
        acc_ref[1:2, :] += _colsum(dy * xs)
        dyb = dy.astype(BF16)
        dq = (dy * ea).astype(BF16)
        dcar = dcarry[...]
        dcarb = dcar.astype(BF16)
        prev = prev_ref[0]
        prevb = prev.astype(BF16)
        ri = lax.broadcasted_iota(jnp.int32, (CHUNK, CHUNK), 0)
        ci = lax.broadcasted_iota(jnp.int32, (CHUNK, CHUNK), 1)
        causal = ri >= ci
        lane = lax.broadcasted_iota(jnp.int32, (1, 128), 1)
        dprev, dxgd, yoff = [], [], []
        dl_l = jnp.zeros((CHUNK, CHUNK), F32)
        dl_s = jnp.zeros((CHUNK, CHUNK), F32)
        for g in range(2):
            gs = slice(g * 512, (g + 1) * 512)
            bg = b_ref[:, g * 128:(g + 1) * 128].astype(BF16)
            cg = c_ref[:, g * 128:(g + 1) * 128].astype(BF16)
            sc = _dot(cg, bg, "nt")
            yoff.append(_dot(cg, prevb[:, gs]))
            dcg = _dot(dq[:, gs], prevb[:, gs], "nt")
            dprev.append(_dot(cg, dq[:, gs], "tn"))
            dbg = _dot(xgd[:, gs], dcarb[:, gs], "nt")
            dxgd.append(_dot(bg, dcarb[:, gs]))
            dsc = jnp.zeros((CHUNK, CHUNK), F32)
            for pr in range(4):
                col = g * 512 + pr * 128
                xp = xgb[:, col:col + 128]
                dyp = dyb[:, col:col + 128]
                acc = jnp.zeros((CHUNK, 128), F32)
                for half in range(2):
                    h = g * 8 + pr * 2 + half
                    dec = _decay(lcs, lt_s, h, causal)
                    mf = sc * dec
                    keep = (lane < HEAD_DIM) if half == 0 else (lane >= HEAD_DIM)
                    dyh = jnp.where(keep, dyp, jnp.zeros_like(dyp))
                    dm = _dot(dyh, xp, "nt")
                    acc = acc + _dot(mf.astype(BF16), dyh, "tn")
                    dsc = dsc + dm * dec
                    gm = dm * mf
                    dl_l = dl_l + jnp.where(ci == h, jnp.sum(gm, axis=1, keepdims=True), 0.0)
                    dl_s = dl_s + jnp.where(ri == h, jnp.sum(gm, axis=0, keepdims=True), 0.0)
                dxg_s[:, col:col + 128] = acc
            dscb = dsc.astype(BF16)
            dcg = dcg + _dot(dscb, bg)
            dbg = dbg + _dot(dscb, cg, "tn")
            du_ref[:, 1024 + g * 128:1024 + (g + 1) * 128] = dbg
            du_ref[:, 1280 + g * 128:1280 + (g + 1) * 128] = dcg
        dxgd = jnp.concatenate(dxgd, axis=1)
        dxg = dxg_s[...] + dxgd * ds
        du_ref[:, 0:1024] = dy * dsk_v + dxg * dtx
        hs1 = _dotx_r(dxg * xs, hsum)
        yoff = jnp.concatenate(yoff, axis=1) * ea
        dl = dl_l - dl_s.T + _dotx_r(dy * yoff - xgdf * dxgd, hsum)
        rows8 = lax.broadcasted_iota(jnp.int32, (8, D_MODEL), 0)
        two = jnp.where(rows8 == 0, _colsum(dxgd * xgdf), jnp.where(rows8 == 1, _colsum(dcar * prev), 0.0))
        two = _dotx_r(two, hsum)
        r8 = lax.broadcasted_iota(jnp.int32, (8, 128), 0)
        dllast = _colsum(jnp.where(r8 == 0, two, 0.0)) + _colsum(jnp.where(r8 == 1, two, 0.0)) * jnp.exp(llast)
        rowi = lax.broadcasted_iota(jnp.int32, (CHUNK, 128), 0)
        dl = dl + jnp.where(rowi == CHUNK - 1, dllast, 0.0)
        dadt = _dotx_l(triu_ref[...], dl)
        ddt = dadt * a + hs1
        draw = ddt * _sig(dtraw_ref[...] + dtb_ref[...])
        ddt_ref[...] = draw.astype(BF16)
        acc16_ref[0:1, :] += _colsum(draw)
        acc16_ref[1:2, :] += _colsum(dadt * dt) * a
        dcarry[...] = dcar * jnp.max(_dotx_r(jnp.exp(llast) + jnp.zeros((8, 128), F32), expand, n=2), axis=0, keepdims=True) \
            + jnp.concatenate(dprev, axis=1)

        @pl.when(i == nc - 1)
        def _():
            hd = _dotx_r(acc_ref[...], hsum)
            acc16_ref[2:3, :] = _colsum(jnp.where(lax.broadcasted_iota(jnp.int32, (8, 128), 0) == 1, hd, 0.0))

    rev = lambda w, col: pl.BlockSpec((CHUNK, w), lambda i: (nc - 1 - i, col))
    return pl.pallas_call(
        body, name=name, grid=(nc,),
        in_specs=[rev(1024, 0), rev(256, 4), rev(256, 5), rev(128, DT_COL), rev(1024, 0), rev(1024, 0),
                  pl.BlockSpec((1, 128, D_MODEL), lambda i: (nc - 1 - i, 0, 0)), rev(1024, 0),
                  _fix_spec((1, 128)), _fix_spec((1, 128)), _fix_spec((1, D_MODEL)), _fix_spec((1, D_MODEL)),
                  _fix_spec((CHUNK, CHUNK)), _fix_spec((CHUNK, CHUNK)), _fix_spec((128, D_MODEL)),
                  _fix_spec((D_MODEL, 128))],
        out_specs=[rev(D_CONV, 0), rev(128, 0), rev(1024, 0), _fix_spec((8, D_MODEL)), _fix_spec((8, 128))],
        out_shape=[jax.ShapeDtypeStruct((t, D_CONV), F32), jax.ShapeDtypeStruct((t, 128), BF16),
                   jax.ShapeDtypeStruct((t, D_MODEL), BF16), jax.ShapeDtypeStruct((8, D_MODEL), F32),
                   jax.ShapeDtypeStruct((8, 128), F32)],
        scratch_shapes=[pltpu.VMEM((128, D_MODEL), F32), pltpu.VMEM((128, 128), F32), pltpu.VMEM((128, 128), F32),
                        pltpu.VMEM((CHUNK, D_MODEL), F32)],
        compiler_params=_cp(("arbitrary",)),
    )(u, u, u, proj, proj, y, prev, dycat, dtb, alog, dsk, nw,
      cst["tril"], cst["triu"], cst["expand"], cst["hsum"])


def _head_rms(v, hsum, expand):
    ms = _dotx_r(v * v, hsum, n=2) * (1.0 / HEAD_DIM)
    return _dotx_r(lax.rsqrt(ms + EPS), expand, n=2)


def _qk_fwd(proj, qw, kw, cst, name):
    t = proj.shape[0]
    tm = min(t, 256)
    scale = HEAD_DIM ** -0.5

    def body(q_ref, k_ref, v_ref, qw_ref, kw_ref, hs_ref, exp_ref, qs_ref, kn_ref, vb_ref):
        hsum, expand = hs_ref[...], exp_ref[...]
        q, k = q_ref[...], k_ref[...]
        qs_ref[...] = (q * _head_rms(q, hsum, expand) * qw_ref[...] * scale).astype(BF16)
        kn_ref[...] = (k * _head_rms(k, hsum, expand) * kw_ref[...]).astype(BF16)
        vb_ref[...] = v_ref[...].astype(BF16)

    return pl.pallas_call(
        body, name=name, grid=(t // tm,),
        in_specs=[_row_spec(tm, col=1), _row_spec(tm, col=2), _row_spec(tm, col=3),
                  _fix_spec((1, D_MODEL)), _fix_spec((1, D_MODEL)), _fix_spec((D_MODEL, 128)),
                  _fix_spec((128, D_MODEL))],
        out_specs=[_row_spec(tm)] * 3, out_shape=[jax.ShapeDtypeStruct((t, D_MODEL), BF16)] * 3,
        compiler_params=_cp(("parallel",)),
    )(proj, proj, proj, qw, kw, cst["hsum"], cst["expand"])


def _qk_bwd(proj, dqs, dkn, dv, qw, kw, cst, name):
    t = proj.shape[0]
    tm = min(t, 256)
    scale = HEAD_DIM ** -0.5

    def body(q_ref, k_ref, dq_ref, dk_ref, dv_ref, qw_ref, kw_ref, hs_ref, exp_ref, fold_ref,
             oq_ref, ok_ref, ov_ref, dw_ref):
        i = pl.program_id(0)
        hsum, expand = hs_ref[...], exp_ref[...]
        rows8 = lax.broadcasted_iota(jnp.int32, (8, D_MODEL), 0)
        sums = jnp.zeros((8, D_MODEL), F32)
        for n, (x_ref, d_ref, w_ref, o_ref, sc) in enumerate(
                [(q_ref, dq_ref, qw_ref, oq_ref, scale), (k_ref, dk_ref, kw_ref, ok_ref, 1.0)]):
            xv = x_ref[...]
            r = _head_rms(xv, hsum, expand)
            xhat = xv * r
            dn = d_ref[...] * sc
            sums = sums + jnp.where(rows8 == n, _colsum(dn * xhat), 0.0)
            dw = dn * w_ref[...]
            mean = _dotx_r(_dotx_r(dw * xhat, hsum, n=2), expand, n=2) * (1.0 / HEAD_DIM)
            o_ref[...] = (r * (dw - xhat * mean)).astype(BF16)
        ov_ref[...] = dv_ref[...].astype(BF16)
        folded = _dotx_r(sums, fold_ref[...])

        @pl.when(i == 0)
        def _():
            dw_ref[...] = folded

        @pl.when(i > 0)
        def _():
            dw_ref[...] += folded

    return pl.pallas_call(
        body, name=name, grid=(t // tm,),
        in_specs=[_row_spec(tm, col=1), _row_spec(tm, col=2), _row_spec(tm), _row_spec(tm), _row_spec(tm),
                  _fix_spec((1, D_MODEL)), _fix_spec((1, D_MODEL)), _fix_spec((D_MODEL, 128)),
                  _fix_spec((128, D_MODEL)), _fix_spec((D_MODEL, 128))],
        out_specs=[_row_spec(tm)] * 3 + [_fix_spec((8, 128))],
        out_shape=[jax.ShapeDtypeStruct((t, D_MODEL), BF16)] * 3 + [jax.ShapeDtypeStruct((8, 128), F32)],
        compiler_params=_cp(("arbitrary",)),
    )(proj, proj, dqs, dkn, dv, qw, kw, cst["hsum"], cst["expand"], cst["fold"])


def _sb_masks(i, kb, tq, tk):
    tpos = i * tq + lax.broadcasted_iota(jnp.int32, (tq, 1), 0)
    spos = kb * tk + lax.broadcasted_iota(jnp.int32, (1, tk), 1)
    return spos < tpos


def _grid_marks(n0, n1):
    j, i = pl.program_id(0), pl.program_id(1)
    return (jnp.logical_and(j == 0, i == 0), jnp.logical_and(j == n0 // 2, i == 0),
            jnp.logical_and(j == n0 - 1, i == n1 - 1))


def _sb_fwd(qs, kn, vb, pack, cst, name):
    t = qs.shape[0]
    tq = tk = min(t, SB_TILE)
    nq = t // tq
    ngrp = D_MODEL // SB_LANES
    nh = 2 * SB_PAIRS
    lanes = lambda p: slice(p * 128, (p + 1) * 128)

    def body(q_ref, k_ref, v_ref, u_ref, p_ref, rt_ref, ob_ref, cnt_ref, gat_ref, acc, rs, gss, grs):
        at_first, at_mid, at_last = _grid_marks(ngrp, nq)
        g_start, g_relay, g_finish = _gather_stages(p_ref, gat_ref, gss, grs)
        pl.when(at_first)(g_start)
        pl.when(at_mid)(g_relay)
        i = pl.program_id(1)
        lane = lax.broadcasted_iota(jnp.int32, (1, 128), 1)
        qh = []
        for p in range(SB_PAIRS):
            q2 = q_ref[:, lanes(p)]
            zero = jnp.zeros_like(q2)
            qh += [jnp.where(lane < HEAD_DIM, q2, zero), jnp.where(lane >= HEAD_DIM, q2, zero)]
        acc[...] = jnp.zeros_like(acc)
        rs[...] = jnp.zeros_like(rs)
        ustrict = u_ref[...]

        def tile(kb, masked):
            off = pl.multiple_of(kb * tk, tk)
            k2 = [k_ref[pl.ds(off, tk), lanes(p)] for p in range(SB_PAIRS)]
            v2 = [v_ref[pl.ds(off, tk), lanes(p)] for p in range(SB_PAIRS)]
            strict = _sb_masks(i, kb, tq, tk) if masked else None
            s = [_dot(qh[h], k2[h // 2], "nt") for h in range(nh)]
            a, r, lb = [None] * nh, [None] * nh, [None] * nh
            for h in range(nh):
                sp = _softplus(s[h])
                a[h] = s[h] - sp
                r[h] = jnp.where(strict, -sp, 0.0) if masked else -sp
                lb[h] = _dot2(r[h], ustrict)
            for h in range(nh):
                lw = a[h] + lb[h] + rs[h]
                w = jnp.exp(jnp.where(strict, lw, -1e30) if masked else lw)
                rs[h] = rs[h] + jnp.sum(r[h], axis=1, keepdims=True)
                acc[h] = acc[h] + _dot(w.astype(BF16), v2[h // 2])

        tile(i, True)

        def live():
            return jnp.max(functools.reduce(jnp.maximum, [rs[h] for h in range(nh)]))

        def more(c):
            return jnp.logical_and(c[0] < i, c[1] > SB_DEAD)

        def step(c):
            tile(i - 1 - c[0], False)
            return c[0] + 1, live()

        n_off, _ = lax.while_loop(more, step, (jnp.int32(0), live()))
        cnt_ref[pl.program_id(0), i] = n_off.astype(F32)
        for p in range(SB_PAIRS):
            rt_ref[:, lanes(p)] = jnp.where(lane < HEAD_DIM, rs[2 * p], rs[2 * p + 1])
            ob_ref[:, lanes(p)] = jnp.where(lane < HEAD_DIM, acc[2 * p], acc[2 * p + 1]).astype(BF16)
        pl.when(at_last)(g_finish)

    blk = lambda rows, imap: pl.BlockSpec((rows, SB_LANES), imap)
    return pl.pallas_call(
        body, name=name, grid=(ngrp, nq),
        in_specs=[blk(tq, lambda j, i: (i, j)), blk(t, lambda j, i: (0, j)), blk(t, lambda j, i: (0, j)),
                  _fix_spec((2 * tk, tk)), HB],
        out_specs=[blk(tq, lambda j, i: (i, j))] * 2 + [pl.BlockSpec(memory_space=pltpu.SMEM), HB],
        out_shape=[jax.ShapeDtypeStruct((t, D_MODEL), F32), jax.ShapeDtypeStruct((t, D_MODEL), BF16),
                   jax.ShapeDtypeStruct((ngrp, nq), F32),
                   jax.ShapeDtypeStruct((N_SHARDS,) + pack.shape, pack.dtype)],
        scratch_shapes=[pltpu.VMEM((nh, tq, 128), F32), pltpu.VMEM((nh, tq, 1), F32)] + GATHER_SEMS,
        compiler_params=_cp(("arbitrary", "arbitrary")),
    )(qs, kn, vb, _doubled(cst["ustrict"], tk), pack)


def _sb_bwd(qs, kn, vb, rtot, cnt, dycat, csum_b, cst, name):
    t = qs.shape[0]
    tq = tk = min(t, SB_TILE)
    nq = t // tq
    ngrp = D_MODEL // SB_LANES
    nh = 2 * SB_PAIRS
    lanes = lambda p: slice(p * 128, (p + 1) * 128)

    def body(q_ref, k_ref, v_ref, rt_ref, do_ref, us_ref, ui_ref, cnt_ref, xs_ref, dq_ref, dk_ref, dv_ref, xr_ref,
             acc, rs, es, xss, xrs):
        at_first, _, at_last = _grid_marks(ngrp, nq)
        x_start, x_finish = _exchange_stages(xs_ref, xr_ref, xss, xrs)
        pl.when(at_first)(x_start)
        i = pl.program_id(1)
        lane = lax.broadcasted_iota(jnp.int32, (1, 128), 1)
        keep = [lane < HEAD_DIM, lane >= HEAD_DIM]
        qh, doh, rtot_h = [], [], []
        for p in range(SB_PAIRS):
            q2, rt = q_ref[:, lanes(p)], rt_ref[:, lanes(p)]
            do2b = do_ref[:, lanes(p)].astype(BF16)
            qh += [jnp.where(kp, q2, jnp.zeros_like(q2)) for kp in keep]
            doh += [jnp.where(kp, do2b, jnp.zeros_like(do2b)) for kp in keep]
            rtot_h += [jnp.sum(jnp.where(lane == n * HEAD_DIM, rt, 0.0), axis=1, keepdims=True) for n in range(2)]
        acc[...] = jnp.zeros_like(acc)
        rs[...] = jnp.zeros_like(rs)
        es[...] = jnp.zeros_like(es)

        @pl.when(i == 0)
        def _():
            dk_ref[...] = jnp.zeros_like(dk_ref)
            dv_ref[...] = jnp.zeros_like(dv_ref)

        ule, ult = us_ref[...], ui_ref[...]

        def tile(kb, masked):
            off = pl.multiple_of(kb * tk, tk)
            k2 = [k_ref[pl.ds(off, tk), lanes(p)] for p in range(SB_PAIRS)]
            v2 = [v_ref[pl.ds(off, tk), lanes(p)] for p in range(SB_PAIRS)]
            strict = _sb_masks(i, kb, tq, tk) if masked else None
            s = [_dot(qh[h], k2[h // 2], "nt") for h in range(nh)]
            dw = [_dot(doh[h], v2[h // 2], "nt") for h in range(nh)]
            a, sg, r, pin, w, e, cin = ([None] * nh for _ in range(7))
            for h in range(nh):
                sp = _softplus(s[h])
                a[h] = s[h] - sp
                sg[h] = jnp.exp(a[h])
                r[h] = jnp.where(strict, -sp, 0.0) if masked else -sp
                pin[h] = _dot2(r[h], ule)
            for h in range(nh):
                lw = a[h] + ((rtot_h[h] - rs[h]) - pin[h])
                w[h] = jnp.exp(jnp.where(strict, lw, -1e30) if masked else lw)
                e[h] = w[h] * dw[h]
                cin[h] = _dot2(e[h], ult)
            for p in range(SB_PAIRS):
                dk_t = jnp.zeros((tk, 128), F32)
                dv_t = jnp.zeros((tk, 128), F32)
                for h in (2 * p, 2 * p + 1):
                    dl = e[h] * (1.0 - sg[h]) - (es[h] + cin[h]) * sg[h]
                    dl = (jnp.where(strict, dl, 0.0) if masked else dl).astype(BF16)
                    rs[h] = rs[h] + jnp.sum(r[h], axis=1, keepdims=True)
                    es[h] = es[h] + jnp.sum(e[h], axis=1, keepdims=True)
                    acc[h] = acc[h] + _dot(dl, k2[p])
                    dk_t = dk_t + _dot(dl, qh[h], "tn")
                    dv_t = dv_t + _dot(w[h].astype(BF16), doh[h], "tn")
                dk_ref[pl.ds(off, tk), lanes(p)] += dk_t
                dv_ref[pl.ds(off, tk), lanes(p)] += dv_t

        def step(kb, carry):
            tile(kb, False)
            return carry

        n_off = cnt_ref[pl.program_id(0), i].astype(jnp.int32)
        lax.fori_loop(i - n_off, i, step, 0)
        tile(i, True)
        for p in range(SB_PAIRS):
            dq_ref[:, lanes(p)] = jnp.where(lane < HEAD_DIM, acc[2 * p], acc[2 * p + 1])
        pl.when(at_last)(x_finish)

    blk = lambda rows, imap: pl.BlockSpec((rows, SB_LANES), imap)
    return pl.pallas_call(
        body, name=name, grid=(ngrp, nq),
        in_specs=[blk(tq, lambda j, i: (i, j)), blk(t, lambda j, i: (0, j)), blk(t, lambda j, i: (0, j)),
                  blk(tq, lambda j, i: (i, j)), blk(tq, lambda j, i: (i, ngrp + j)),
                  _fix_spec((2 * tk, tk)), _fix_spec((2 * tk, tk)), pl.BlockSpec(memory_space=pltpu.SMEM), HB],
        out_specs=[blk(tq, lambda j, i: (i, j)), blk(t, lambda j, i: (0, j)), blk(t, lambda j, i: (0, j)), HB],
        out_shape=[jax.ShapeDtypeStruct((t, D_MODEL), F32)] * 3
        + [jax.ShapeDtypeStruct((3,) + csum_b.shape[1:], csum_b.dtype)],
        scratch_shapes=[pltpu.VMEM((nh, tq, 128), F32), pltpu.VMEM((nh, tq, 1), F32), pltpu.VMEM((nh, tq, 1), F32)]
        + EXCHANGE_SEMS,
        compiler_params=_cp(("arbitrary", "arbitrary")),
    )(qs, kn, vb, rtot, dycat, _doubled(cst["ule"], tk), _doubled(cst["ult"], tk), cnt, csum_b)


def _adamw(w, g, m, v, name):
    lead = (1,) * (w.ndim - 2)
    rows, cols = w.shape[-2:]
    fits = [d for d in range(8, rows, 8) if rows % d == 0 and d * cols * 4 <= ADAM_BLOCK_BYTES]
    tr = max(fits) if fits else rows
    c1 = 1.0 - ADAM_B1 ** ADAM_STEP
    c2 = 1.0 - ADAM_B2 ** ADAM_STEP

    def body(w_ref, g_ref, m_ref, v_ref, d_ref, nm_ref, nv_ref):
        gv = g_ref[...]
        nm = ADAM_B1 * m_ref[...] + (1.0 - ADAM_B1) * gv
        nv = ADAM_B2 * v_ref[...] + (1.0 - ADAM_B2) * (gv * gv)
        nm_ref[...] = nm
        nv_ref[...] = nv
        d_ref[...] = -ADAM_LR * ((nm / c1) / (jnp.sqrt(nv / c2) + ADAM_EPS) + ADAM_WD * w_ref[...])

    spec = pl.BlockSpec(lead + (tr, cols), lambda i: (0,) * len(lead) + (i, 0))
    return pl.pallas_call(
        body, name=name, grid=(rows // tr,), in_specs=[spec] * 4, out_specs=[spec] * 3,
        out_shape=[jax.ShapeDtypeStruct(w.shape, F32)] * 3, compiler_params=_cp(("parallel",)),
    )(w, g, m, v)


def _place():
    x, y, c = lax.axis_index("x"), lax.axis_index("y"), lax.axis_index("c")
    chips = [(1 - x, y), (x, 1 - y), (1 - x, 1 - y)]
    return x, y, c, chips


VM = pl.BlockSpec(memory_space=pltpu.VMEM)
HB = pl.BlockSpec(memory_space=pltpu.HBM)


def _small_gather(pack, name, reduce):
    rows = pack.shape[0]

    def body(p_ref, gat_ref, *rest):
        if reduce:
            sum_ref, ss, rs = rest
        else:
            ss, rs = rest
        x, y, c, _ = _place()
        me = 4 * x + 2 * y + c
        peers = [(x, y, 1 - c), (1 - x, y, c), (x, 1 - y, c), (1 - x, 1 - y, c),
                 (1 - x, y, 1 - c), (x, 1 - y, 1 - c), (1 - x, 1 - y, 1 - c)]

        def copy(k, slot, to):
            return pltpu.make_async_remote_copy(src_ref=p_ref, dst_ref=gat_ref.at[slot], send_sem=ss.at[k],
                                                recv_sem=rs.at[k], device_id=to, device_id_type=MESH)

        sends = [copy(k, me, p) for k, p in enumerate(peers)]
        for s in sends:
            s.start()
        gat_ref[me] = p_ref[...]
        for k, p in enumerate(peers):
            copy(k, 4 * p[0] + 2 * p[1] + p[2], p).wait_recv()
        for s in sends:
            s.wait_send()
        if reduce:
            tot = gat_ref[0]
            for b in range(1, 8):
                tot = tot + gat_ref[b]
            sum_ref[...] = tot

    out_shape = [jax.ShapeDtypeStruct((8, rows, D_MODEL), F32)]
    if reduce:
        out_shape.append(jax.ShapeDtypeStruct((rows, D_MODEL), F32))
    return pl.pallas_call(
        body, name=name, in_specs=[VM], out_specs=[VM] * len(out_shape), out_shape=out_shape,
        scratch_shapes=[pltpu.SemaphoreType.DMA((7,)), pltpu.SemaphoreType.DMA((7,))],
        compiler_params=_cp(),
    )(pack)


def _mod_exchange(cond, w_ada, b_shard, name):
    def body(c_ref, w_ref, b_ref, modp_ref, ss, rs):
        x, y, c, chips = _place()
        sh = 2 * x + y
        cv = c_ref[...]
        cv = cv * _sig(cv)
        modp_ref[sh] = jnp.dot(cv, w_ref[...], precision=lax.Precision.HIGHEST,
                               preferred_element_type=F32) + b_ref[...]

        def copy(k, slot, to):
            return pltpu.make_async_remote_copy(src_ref=modp_ref.at[slot], dst_ref=modp_ref.at[slot],
                                                send_sem=ss.at[k], recv_sem=rs.at[k], device_id=to,
                                                device_id_type=MESH)

        sends = [copy(k, sh, (*ch, c)) for k, ch in enumerate(chips)]
        for s in sends:
            s.start()
        for k, ch in enumerate(chips):
            copy(k, 2 * ch[0] + ch[1], (*ch, c)).wait_recv()
        for s in sends:
            s.wait_send()

    return pl.pallas_call(
        body, name=name, in_specs=[VM, VM, VM], out_specs=VM,
        out_shape=jax.ShapeDtypeStruct((N_SHARDS, 8, 6 * D_MODEL // N_SHARDS), F32),
        scratch_shapes=[pltpu.SemaphoreType.DMA((3,)), pltpu.SemaphoreType.DMA((3,))],
        compiler_params=_cp(),
    )(cond, w_ada, b_shard)


def _gather_stages(p_ref, out_ref, ss, rs):
    hf = p_ref.shape[0] // 2
    x, y, c, chips = _place()
    sh = 2 * x + y
    sib = (x, y, 1 - c)
    slots = [2 * ch[0] + ch[1] for ch in chips]

    def half(slot, hc):
        return out_ref.at[slot, pl.ds(hc * hf, hf), :]

    def copy(k, src, slot, hc, to):
        return pltpu.make_async_remote_copy(src_ref=src, dst_ref=half(slot, hc), send_sem=ss.at[k],
                                            recv_sem=rs.at[k], device_id=to, device_id_type=MESH)

    def first():
        return [copy(j, p_ref.at[pl.ds(c * hf, hf), :], sh, c, (*ch, c)) for j, ch in enumerate(chips)]

    def passed():
        return [copy(3 + j, half(slots[j], c), slots[j], c, sib) for j in range(3)]

    def start():
        for cp in first():
            cp.start()

    def relay():
        for j, cp in enumerate(passed()):
            copy(j, half(slots[j], c), slots[j], c, (*chips[j], c)).wait_recv()
            cp.start()

    def finish():
        for j in range(3):
            copy(3 + j, half(slots[j], 1 - c), slots[j], 1 - c, sib).wait_recv()
        for cp in first() + passed():
            cp.wait_send()

    return start, relay, finish


GATHER_SEMS = [pltpu.SemaphoreType.DMA((6,)), pltpu.SemaphoreType.DMA((6,))]


def _gather_weights(pack, name):
    def body(p_ref, out_ref, ss, rs):
        for stage in _gather_stages(p_ref, out_ref, ss, rs):
            stage()

    return pl.pallas_call(
        body, name=name, in_specs=[HB], out_specs=HB,
        out_shape=jax.ShapeDtypeStruct((N_SHARDS,) + pack.shape, pack.dtype),
        scratch_shapes=GATHER_SEMS, compiler_params=_cp(),
    )(pack)


def _sibling_swap(g, name):
    hf = g.shape[1] // 2

    def body(g_ref, out_ref, ss, rs):
        x, y, c, _ = _place()
        cp = pltpu.make_async_remote_copy(
            src_ref=g_ref.at[pl.ds(0, N_SHARDS), pl.ds((1 - c) * hf, hf), :], dst_ref=out_ref,
            send_sem=ss, recv_sem=rs, device_id=(x, y, 1 - c), device_id_type=MESH)
        cp.start()
        cp.wait()

    return pl.pallas_call(
        body, name=name, in_specs=[HB], out_specs=HB,
        out_shape=jax.ShapeDtypeStruct((N_SHARDS, hf, D_MODEL), g.dtype),
        scratch_shapes=[pltpu.SemaphoreType.DMA, pltpu.SemaphoreType.DMA],
        compiler_params=_cp(),
    )(g)


def _row_tile(rows, width_bytes, cap_bytes):
    fits = [d for d in range(8, rows + 1, 8) if rows % d == 0 and d * width_bytes <= cap_bytes]
    return max(fits)


def _chip_sum(g, got, c_idx, name):
    hf = got.shape[1]
    tr = _row_tile(hf, D_MODEL * 4, 3 << 20)
    nb = hf // tr

    def body(c_ref, a_ref, b_ref, s_ref, sb_ref):
        s = a_ref[...] + b_ref[...]
        s_ref[...] = s
        sb_ref[...] = s.astype(BF16)

    blk = pl.BlockSpec((1, tr, D_MODEL), lambda s, i, c_ref: (s, i, 0))
    return pl.pallas_call(
        body, name=name,
        grid_spec=pltpu.PrefetchScalarGridSpec(
            num_scalar_prefetch=1, grid=(N_SHARDS, nb),
            in_specs=[pl.BlockSpec((1, tr, D_MODEL), lambda s, i, c_ref: (s, c_ref[0] * nb + i, 0)), blk],
            out_specs=[blk, blk]),
        out_shape=[jax.ShapeDtypeStruct((N_SHARDS, hf, D_MODEL), F32),
                   jax.ShapeDtypeStruct((N_SHARDS, hf, D_MODEL), BF16)],
        compiler_params=_cp(("parallel", "parallel")),
    )(c_idx, g, got)


def _exchange_stages(s_ref, out_ref, ss, rs):
    x, y, c, chips = _place()

    def sends():
        return [pltpu.make_async_remote_copy(src_ref=s_ref.at[2 * ch[0] + ch[1]], dst_ref=out_ref.at[k],
                                             send_sem=ss.at[k], recv_sem=rs.at[k], device_id=(*ch, c),
                                             device_id_type=MESH) for k, ch in enumerate(chips)]

    def start():
        for cp in sends():
            cp.start()

    def finish():
        for cp in sends():
            cp.wait()

    return start, finish


EXCHANGE_SEMS = [pltpu.SemaphoreType.DMA((3,)), pltpu.SemaphoreType.DMA((3,))]


def _total_half(s, got, sh_idx, name):
    hf = got.shape[1]
    tr = _row_tile(hf, D_MODEL * 4, 3 << 20)
    nb = hf // tr

    def body(sh_ref, a_ref, r0, r1, r2, o_ref):
        o_ref[...] = ((a_ref[0] + r0[0].astype(F32)) + r1[0].astype(F32)) + r2[0].astype(F32)

    rspec = lambda k: pl.BlockSpec((1, tr, D_MODEL), lambda i, sh_ref: (k, i, 0))
    return pl.pallas_call(
        body, name=name,
        grid_spec=pltpu.PrefetchScalarGridSpec(
            num_scalar_prefetch=1, grid=(nb,),
            in_specs=[pl.BlockSpec((1, tr, D_MODEL), lambda i, sh_ref: (sh_ref[0], i, 0)),
                      rspec(0), rspec(1), rspec(2)],
            out_specs=pl.BlockSpec((tr, D_MODEL), lambda i, sh_ref: (i, 0))),
        out_shape=jax.ShapeDtypeStruct((hf, D_MODEL), F32),
        compiler_params=_cp(("parallel",)),
    )(sh_idx, s, got, got, got)


def _join_halves(tot, name):
    def body(t_ref, out_ref, ss, rs):
        x, y, c, _ = _place()
        cp = pltpu.make_async_remote_copy(src_ref=t_ref, dst_ref=out_ref, send_sem=ss, recv_sem=rs,
                                          device_id=(x, y, 1 - c), device_id_type=MESH)
        cp.start()
        cp.wait()

    return pl.pallas_call(
        body, name=name, in_specs=[HB], out_specs=HB,
        out_shape=jax.ShapeDtypeStruct(tot.shape, F32),
        scratch_shapes=[pltpu.SemaphoreType.DMA, pltpu.SemaphoreType.DMA],
        compiler_params=_cp(),
    )(tot)


def _w_ada_grad(cond, dmod_cols, name):
    def body(c_ref, d_ref, o_ref):
        cv = c_ref[...]
        cv = cv * _sig(cv)
        o_ref[...] = lax.dot_general(cv, d_ref[...], _DN["tn"], precision=lax.Precision.HIGHEST,
                                     preferred_element_type=F32)

    return pl.pallas_call(
        body, name=name, in_specs=[VM, VM], out_specs=VM,
        out_shape=jax.ShapeDtypeStruct((D_MODEL, dmod_cols.shape[1]), F32), compiler_params=_cp(),
    )(cond, dmod_cols)


def _pad_rows(a, rows):
    return jnp.pad(a, ((0, rows - a.shape[0]), (0, 0)))


def _pad_cols(a, cols):
    return jnp.pad(a, ((0, 0), (0, cols - a.shape[1])))


def _unpack_rest(p):
    o = 0
    out = []
    for r in (R_OUT, R_FF, R_FF, R_FF):
        out.append(p[..., o:o + r, :])
        o += r
    return out


def _reduce_tail(csum, got2, shard, ac, tag):
    tot = _total_half(csum, got2, shard.reshape(1).astype(jnp.int32), "rs_total_" + tag)
    other = _join_halves(tot, "rs_join_" + tag)
    return jnp.where(ac == 0, jnp.concatenate([tot, other], axis=0), jnp.concatenate([other, tot], axis=0))


def _reduce_head(gpack, ac, tag):
    got = _sibling_swap(gpack, "rs_sibling_swap_" + tag)
    return _chip_sum(gpack, got, ac.reshape(1).astype(jnp.int32), "rs_chip_sum_" + tag)


def kernel(x, c, w_ada, b_ada, norm1_w, w_in, conv_w, conv_b, dt_bias, a_log, d_skip, ssd_norm_w, q_norm_w, k_norm_w, w_out, norm2_w, w_gate, w_up, w_down, loss_target, m_w_ada, m_b_ada, m_norm1_w, m_w_in, m_conv_w, m_conv_b, m_dt_bias, m_a_log, m_d_skip, m_ssd_norm_w, m_q_norm_w, m_k_norm_w, m_w_out, m_norm2_w, m_w_gate, m_w_up, m_w_down, v_w_ada, v_b_ada, v_norm1_w, v_w_in, v_conv_w, v_conv_b, v_dt_bias, v_a_log, v_d_skip, v_ssd_norm_w, v_q_norm_w, v_k_norm_w, v_w_out, v_norm2_w, v_w_gate, v_w_up, v_w_down):
    cst = _consts()
    ax, ay, ac = lax.axis_index("x"), lax.axis_index("y"), lax.axis_index("c")
    shard = 2 * ax + ay
    me = 4 * ax + 2 * ay + ac
    xs = x[0]
    tgt = loss_target[0]
    w_in_cols = w_in.shape[2]
    conv_cols = conv_w.shape[2]

    cw_flat = _pad_cols(conv_w[0].reshape(1, -1), 2 * D_MODEL).reshape(2, D_MODEL)
    cpack = jnp.concatenate([jnp.broadcast_to(c, (8, D_MODEL)), _pad_rows(cw_flat, 8)], axis=0)
    gat = _small_gather(cpack, "gather_c", reduce=False)[0]
    c_all = gat[:, 0, :]
    cw = gat[0::2, 8:10, :].reshape(N_SHARDS, 2 * D_MODEL)[:, :4 * conv_cols].reshape(N_SHARDS, 4, conv_cols)
    conv_w_full = jnp.transpose(cw, (1, 0, 2)).reshape(4, D_CONV)

    mod_w = 6 * D_MODEL // N_SHARDS
    b_shard = lax.dynamic_slice(b_ada, (0, shard * mod_w), (1, mod_w))
    modp = _mod_exchange(c_all, w_ada[0], b_shard, "mod_exchange")
    mod_mine = lax.dynamic_slice(modp, (0, me, 0), (N_SHARDS, 1, mod_w)).reshape(6, D_MODEL)
    mod = _pad_rows(mod_mine, 8)

    tr3 = lambda a: jnp.transpose(a, (0, 2, 1))
    lin = lambda a: tr3(a).reshape(-1, 128)
    unlin = lambda a: tr3(a.reshape(1, w_in_cols, D_MODEL))
    wpack_in = _pad_rows(tr3(w_in.astype(BF16))[0], R_IN)
    wpack_rest = jnp.concatenate([w_out[0], tr3(w_gate)[0], tr3(w_up)[0], w_down[0]], axis=0).astype(BF16)
    own = lax.broadcasted_iota(jnp.int32, (N_SHARDS, 1, 1), 0) == shard
    p_in = jnp.where(own, wpack_in[None], _gather_weights(wpack_in, "gather_w_in"))
    wi_t = p_in[:, :w_in_cols, :].reshape(D_IN_PROJ, D_MODEL)
    w_inp_t = jnp.concatenate([wi_t[0:1024], wi_t[2576:5648], wi_t[1024:2560], wi_t[2560:2576],
                               jnp.zeros((112, D_MODEL), BF16)], axis=0)

    pad128 = lambda a: _pad_cols(a, 128)
    dtb, alog = pad128(dt_bias), pad128(a_log)
    dsk = jnp.repeat(d_skip, HEAD_DIM, axis=1)
    qw, kw = jnp.tile(q_norm_w, (1, N_HEADS)), jnp.tile(k_norm_w, (1, N_HEADS))

    h1 = _norm_mod(xs, norm1_w, mod, 0, "norm1")
    proj = _matmul(h1, w_inp_t, "nt", F32, "in_proj")
    u = _conv_fwd(proj, conv_w_full, conv_b, "conv_fwd")
    y_ssd, yn, prev = _ssd_fwd(u, proj, dtb, alog, dsk, ssd_norm_w, cst, "ssd_fwd")
    qs, kn, vb = _qk_fwd(proj, qw, kw, cst, "qk_norm")
    rtot, ob, cnt, gp_rest = _sb_fwd(qs, kn, vb, wpack_rest, cst, "sb_fwd")
    p_out, p_gate, p_up, p_down = _unpack_rest(jnp.where(own, wpack_rest[None], gp_rest))
    w_o = p_out.reshape(2 * D_MODEL, D_MODEL)
    w_gu_t = jnp.concatenate([p_gate.reshape(D_FF, D_MODEL), p_up.reshape(D_FF, D_MODEL)], axis=0)
    w_d = p_down.reshape(D_FF, D_MODEL)
    ycat = jnp.concatenate([yn, ob], axis=1)
    mix = _matmul(ycat, w_o, "nn", F32, "out_proj")
    x1, h2 = _resid_norm(xs, mix, norm2_w, mod, "resid_norm2")
    gu = _matmul(h2, w_gu_t, "nt", BF16, "ffn_in")
    act = _act_fwd(gu, "ffn_act")
    ffn = _matmul(act, w_d, "nn", F32, "ffn_out", tk_cap=1408)
    dffn, dout, dg2, loss8 = _loss_head(x1, ffn, tgt, mod, "loss_head")
    loss = lax.psum(loss8[0, 0], ("x", "y", "c"))

    dact = _matmul(dffn, w_d, "nt", BF16, "d_act")
    g_down = _matmul(act, dffn, "tn", F32, "g_w_down", tm_cap=1408)
    dgu = _act_bwd(dact, gu, "ffn_act_bwd")
    dh2 = _matmul(dgu, w_gu_t, "nn", F32, "d_h2", tk_cap=1408)
    g_gu_t = _matmul(dgu, h2, "tn", F32, "g_w_gu", tm_cap=1408)
    dx1, dmix, acc2 = _norm_bwd(dh2, x1, dout, mix, norm2_w, mod, 3, 2, "norm2_bwd")
    dycat = _matmul(dmix, w_o, "nt", F32, "d_ycat")
    g_out = _matmul(ycat, dmix, "tn", F32, "g_w_out")
    gpack_rest = jnp.concatenate([
        g_out.reshape(N_SHARDS, R_OUT, D_MODEL),
        g_gu_t[:D_FF].reshape(N_SHARDS, R_FF, D_MODEL), g_gu_t[D_FF:].reshape(N_SHARDS, R_FF, D_MODEL),
        g_down.reshape(N_SHARDS, R_FF, D_MODEL)], axis=1)
    csum_r, csum_rb = _reduce_head(gpack_rest, ac, "rest")
    du, ddt, dz, acc_ssd, acc16 = _ssd_bwd(u, proj, y_ssd, prev, dycat, dtb, alog, dsk, ssd_norm_w, cst, "ssd_bwd")
    dqs, dkn, dv, got_r = _sb_bwd(qs, kn, vb, rtot, cnt, dycat, csum_rb, cst, "sb_bwd")
    r_out, r_gate, r_up, r_down = _unpack_rest(_reduce_tail(csum_r, got_r, shard, ac, "rest"))
    dq, dk, dvb, acc_qk = _qk_bwd(proj, dqs, dkn, dv, qw, kw, cst, "qk_norm_bwd")
    dxbc, g_conv_w, g_conv_b = _conv_bwd(proj, du, conv_w_full, conv_b, "conv_bwd")
    dproj = jnp.concatenate([dz, dq, dk, dvb, dxbc, ddt], axis=1)
    g_inp_t = _matmul(dproj, h1, "tn", F32, "g_w_in", tm_cap=1920)
    gi_t = jnp.concatenate([g_inp_t[0:1024], g_inp_t[4096:5632], g_inp_t[5632:5648], g_inp_t[1024:4096]], axis=0)
    gpack_in = jnp.pad(gi_t.reshape(N_SHARDS, w_in_cols, D_MODEL), ((0, 0), (0, R_IN - w_in_cols), (0, 0)))
    csum_i, csum_ib = _reduce_head(gpack_in, ac, "in")
    dh1, got_i = _matmul(dproj, w_inp_t, "nn", F32, "d_h1", tk_cap=1152, exchange=csum_ib)
    r_in = _reduce_tail(csum_i, got_i, shard, ac, "in")
    grad_x, acc1 = _norm_bwd(dh1, xs, dx1, None, norm1_w, mod, 0, None, "norm1_bwd")

    last = jnp.concatenate([acc_qk[0:1, 0:64], acc_qk[1:2, 0:64], acc16[0:1, 0:16], acc16[1:2, 0:16],
                            acc16[2:3, 0:16]], axis=1)
    spack = jnp.concatenate([
        acc1[0:2], acc2[3:4], acc2[0:2], dg2,
        acc1[2:3], acc2[2:3], acc_ssd[0:1],
        _pad_cols(g_conv_b, 2 * D_MODEL).reshape(2, D_MODEL),
        g_conv_w.reshape(6, D_MODEL),
        _pad_cols(last, D_MODEL)], axis=0)
    sgat, ssum = _small_gather(_pad_rows(spack, SMALL_ROWS), "gather_small", reduce=True)
    g_b_ada = ssum[0:6].reshape(1, 6 * D_MODEL)
    g_norm1, g_norm2, g_ssdn = ssum[6:7], ssum[7:8], ssum[8:9]
    g_cb = ssum[9:11].reshape(1, 2 * D_MODEL)[:, :D_CONV]
    g_cw = lax.dynamic_slice(ssum[11:17].reshape(4, D_CONV), (0, shard * conv_cols), (4, conv_cols))
    g_qn, g_kn = ssum[17:18, 0:64], ssum[17:18, 64:128]
    g_dtb, g_alog, g_dsk = ssum[17:18, 128:144], ssum[17:18, 144:160], ssum[17:18, 160:176]
    dmod_all = sgat[:, 0:6, :].reshape(8, 6 * D_MODEL)
    g_w_ada = _w_ada_grad(c_all, lax.dynamic_slice(dmod_all, (0, shard * mod_w), (8, mod_w)), "g_w_ada")


    grads = dict(w_ada=g_w_ada, b_ada=g_b_ada, norm1_w=g_norm1, w_in=r_in[:w_in_cols].reshape(-1, 128), conv_w=g_cw,
                 conv_b=g_cb, dt_bias=g_dtb, a_log=g_alog, d_skip=g_dsk, ssd_norm_w=g_ssdn, q_norm_w=g_qn,
                 k_norm_w=g_kn, w_out=r_out, norm2_w=g_norm2, w_gate=r_gate, w_up=r_up, w_down=r_down)
    weights = dict(w_ada=(w_ada, m_w_ada, v_w_ada), b_ada=(b_ada, m_b_ada, v_b_ada),
                   norm1_w=(norm1_w, m_norm1_w, v_norm1_w), w_in=(w_in, m_w_in, v_w_in),
                   conv_w=(conv_w, m_conv_w, v_conv_w), conv_b=(conv_b, m_conv_b, v_conv_b),
                   dt_bias=(dt_bias, m_dt_bias, v_dt_bias), a_log=(a_log, m_a_log, v_a_log),
                   d_skip=(d_skip, m_d_skip, v_d_skip), ssd_norm_w=(ssd_norm_w, m_ssd_norm_w, v_ssd_norm_w),
                   q_norm_w=(q_norm_w, m_q_norm_w, v_q_norm_w), k_norm_w=(k_norm_w, m_k_norm_w, v_k_norm_w),
                   w_out=(w_out, m_w_out, v_w_out), norm2_w=(norm2_w, m_norm2_w, v_norm2_w),
                   w_gate=(w_gate, m_w_gate, v_w_gate), w_up=(w_up, m_w_up, v_w_up),
                   w_down=(w_down, m_w_down, v_w_down))
    views = dict(w_in=(lin, unlin), w_gate=(tr3, tr3), w_up=(tr3, tr3))
    same = lambda a: a
    names = list(weights)
    g_out_l, d_out_l, m_out_l, v_out_l = [], [], [], []
    for n in names:
        view, back = views.get(n, (same, same))
        w, m, v = (view(a) for a in weights[n])
        g = grads[n].reshape(w.shape)
        d, nm, nv = _adamw(w, g, m, v, "adamw_" + n)
        g_out_l.append(back(g))
        d_out_l.append(back(d))
        m_out_l.append(back(nm))
        v_out_l.append(back(nv))
    return (loss, grad_x[None], *g_out_l, *d_out_l, *m_out_l, *v_out_l)
```

```python
import functools

import numpy as np
import jax
import jax.numpy as jnp
from jax import lax
from jax.experimental import pallas as pl
from jax.experimental.pallas import tpu as pltpu

F32, BF16 = jnp.float32, jnp.bfloat16
MESH = pl.DeviceIdType.MESH

D_MODEL = 1024
HEAD_DIM = 64
N_HEADS = 16
D_CONV = 1536
D_FF = 2816
D_IN_PROJ = 5648
D_PROJ_PAD = 5760
CHUNK = 128
SB_TILE = 256
SB_PAIRS = 2
SB_LANES = 128 * SB_PAIRS
SB_DEAD = -105.0
EPS = 1e-6
N_SHARDS = 4
R_IN, R_OUT, R_FF = 1440, 512, 704
SMALL_ROWS = 24

ADAM_LR, ADAM_B1, ADAM_B2, ADAM_EPS, ADAM_WD, ADAM_STEP = 0.001, 0.9, 0.999, 1e-08, 0.01, 10

VMEM_LIMIT = 48 * 1024 * 1024
ADAM_BLOCK_BYTES = 3 * 512 * 1024

_DN = {"nn": (((1,), (0,)), ((), ())), "nt": (((1,), (1,)), ((), ())), "tn": (((0,), (0,)), ((), ()))}


def _dot(a, b, dims="nn"):
    return lax.dot_general(a, b, _DN[dims], preferred_element_type=F32)


def _pieces(x, n):
    out = []
    for _ in range(n - 1):
        hi = lax.bitcast_convert_type(lax.bitcast_convert_type(x, jnp.int32) & jnp.int32(-65536), F32)
        out.append(hi.astype(BF16))
        x = x - hi
    out.append(x.astype(BF16))
    return out


def _dotx_r(x, b_exact, n=3):
    return _dot(jnp.concatenate(_pieces(x, n), axis=1), jnp.concatenate([b_exact] * n, axis=0))


def _dotx_l(a_exact, x, n=3):
    return _dot(jnp.concatenate([a_exact] * n, axis=1), jnp.concatenate(_pieces(x, n), axis=0))


def _dot2(x, b2):
    return _dot(jnp.concatenate(_pieces(x, 2), axis=1), b2)


def _sig(x):
    return 1.0 / (1.0 + jnp.exp(-x))


def _softplus(x):
    return jnp.maximum(x, 0.0) + jnp.log(1.0 + jnp.exp(-jnp.abs(x)))


def _cp(sem=None, vmem=VMEM_LIMIT):
    return pltpu.CompilerParams(dimension_semantics=sem, vmem_limit_bytes=vmem)


def _colsum(x):
    return jnp.sum(x, axis=0, keepdims=True)


def _consts():
    ch = np.arange(D_MODEL)
    expand = (np.arange(128)[:, None] == (ch // HEAD_DIM)[None, :]).astype(np.float32)
    fold = (ch[:, None] % HEAD_DIM == np.arange(128)[None, :]).astype(np.float32)
    i = np.arange(CHUNK)
    tril = (i[:, None] >= i[None, :]).astype(np.float32)
    j = np.arange(SB_TILE)
    ustrict = (j[:, None] > j[None, :]).astype(np.float32)
    ule = (j[:, None] <= j[None, :]).astype(np.float32)
    ult = (j[:, None] < j[None, :]).astype(np.float32)
    c = lambda a: jnp.asarray(a, BF16)
    return dict(expand=c(expand), hsum=c(expand.T), fold=c(fold), tril=c(tril), triu=c(tril.T),
                ustrict=ustrict, ule=ule, ult=ult)


def _doubled(tri, tk):
    b = tri[:tk, :tk]
    return jnp.asarray(np.concatenate([b, b], axis=0), BF16)


def _pick(n, cap):
    best = 128
    for t in range(128, min(n, cap) + 1, 128):
        if n % t == 0:
            best = t
    return n if n <= cap else best


def _matmul(a, b, dims, out_dtype, name, tm_cap=1024, tn_cap=2048, tk_cap=1024, exchange=None):
    if dims == "nn":
        (m, k), (_, n) = a.shape, b.shape
    elif dims == "nt":
        (m, k), (n, _) = a.shape, b.shape
    else:
        (k, m), (_, n) = a.shape, b.shape
    tm, tn, tk = _pick(m, tm_cap), _pick(n, tn_cap), _pick(k, tk_cap)
    nk = k // tk
    a_spec = (pl.BlockSpec((tk, tm), lambda i, j, kk: (kk, i)) if dims == "tn"
              else pl.BlockSpec((tm, tk), lambda i, j, kk: (i, kk)))
    b_spec = (pl.BlockSpec((tn, tk), lambda i, j, kk: (j, kk)) if dims == "nt"
              else pl.BlockSpec((tk, tn), lambda i, j, kk: (kk, j)))

    grid = (m // tm, n // tn, nk)

    def body(a_ref, b_ref, *rest):
        if exchange is None:
            o_ref, acc_ref = rest
        else:
            xs_ref, o_ref, xr_ref, acc_ref, xss, xrs = rest
            ids = [pl.program_id(d) for d in range(3)]
            x_start, x_finish = _exchange_stages(xs_ref, xr_ref, xss, xrs)
            pl.when(functools.reduce(jnp.logical_and, [p == 0 for p in ids]))(x_start)
        kk = pl.program_id(2)
        part = _dot(a_ref[...], b_ref[...], dims)
        if nk == 1:
            o_ref[...] = part.astype(out_dtype)
        else:
            @pl.when(kk == 0)
            def _():
                acc_ref[...] = part

            @pl.when(kk > 0)
            def _():
                acc_ref[...] += part

            @pl.when(kk == nk - 1)
            def _():
                o_ref[...] = acc_ref[...].astype(out_dtype)
        if exchange is not None:
            pl.when(functools.reduce(jnp.logical_and, [p == g - 1 for p, g in zip(ids, grid)]))(x_finish)

    in_specs = [a_spec, b_spec]
    out_specs = [pl.BlockSpec((tm, tn), lambda i, j, kk: (i, j))]
    out_shape = [jax.ShapeDtypeStruct((m, n), out_dtype)]
    scratch = [pltpu.VMEM((tm, tn) if nk > 1 else (8, 128), F32)]
    args = [a, b]
    if exchange is not None:
        in_specs.append(HB)
        out_specs.append(HB)
        out_shape.append(jax.ShapeDtypeStruct((3,) + exchange.shape[1:], exchange.dtype))
        scratch += EXCHANGE_SEMS
        args.append(exchange)
    out = pl.pallas_call(
        body, name=name, grid=grid, in_specs=in_specs, out_specs=out_specs, out_shape=out_shape,
        scratch_shapes=scratch,
        compiler_params=_cp(("parallel", "parallel", "arbitrary") if exchange is None else ("arbitrary",) * 3),
    )(*args)
    return out[0] if exchange is None else out


def _row_spec(tm, width=D_MODEL, col=0):
    return pl.BlockSpec((tm, width), lambda i: (i, col))


def _fix_spec(shape):
    return pl.BlockSpec(shape, lambda *_: (0,) * len(shape))


def _norm_mod(x, nw, mod, row_sh, name):
    t = x.shape[0]
    tm = min(t, 512)

    def body(x_ref, nw_ref, mod_ref, h_ref):
        xv = x_ref[...]
        r = lax.rsqrt(jnp.mean(xv * xv, axis=-1, keepdims=True) + EPS)
        sh = mod_ref[row_sh:row_sh + 1, :]
        sc = mod_ref[row_sh + 1:row_sh + 2, :]
        h_ref[...] = (xv * r * nw_ref[...] * (1.0 + sc) + sh).astype(BF16)

    return pl.pallas_call(
        body, name=name, grid=(t // tm,),
        in_specs=[_row_spec(tm), _fix_spec((1, D_MODEL)), _fix_spec((8, D_MODEL))],
        out_specs=_row_spec(tm), out_shape=jax.ShapeDtypeStruct((t, D_MODEL), BF16),
        compiler_params=_cp(("parallel",)),
    )(x, nw, mod)


def _resid_norm(x, mix, nw, mod, name):
    t = x.shape[0]
    tm = min(t, 512)

    def body(x_ref, mix_ref, nw_ref, mod_ref, x1_ref, h_ref):
        x1 = x_ref[...] + mod_ref[2:3, :] * mix_ref[...]
        x1_ref[...] = x1
        r = lax.rsqrt(jnp.mean(x1 * x1, axis=-1, keepdims=True) + EPS)
        h_ref[...] = (x1 * r * nw_ref[...] * (1.0 + mod_ref[4:5, :]) + mod_ref[3:4, :]).astype(BF16)

    return pl.pallas_call(
        body, name=name, grid=(t // tm,),
        in_specs=[_row_spec(tm), _row_spec(tm), _fix_spec((1, D_MODEL)), _fix_spec((8, D_MODEL))],
        out_specs=[_row_spec(tm), _row_spec(tm)],
        out_shape=[jax.ShapeDtypeStruct((t, D_MODEL), F32), jax.ShapeDtypeStruct((t, D_MODEL), BF16)],
        compiler_params=_cp(("parallel",)),
    )(x, mix, nw, mod)


def _act_fwd(gu, name):
    t = gu.shape[0]
    tm, tn = min(t, 512), D_FF // 2
    nb = D_FF // tn

    def body(g_ref, u_ref, a_ref):
        g = g_ref[...].astype(F32)
        a_ref[...] = (g * _sig(g) * u_ref[...].astype(F32)).astype(BF16)

    return pl.pallas_call(
        body, name=name, grid=(t // tm, nb),
        in_specs=[pl.BlockSpec((tm, tn), lambda i, j: (i, j)), pl.BlockSpec((tm, tn), lambda i, j: (i, j + nb))],
        out_specs=pl.BlockSpec((tm, tn), lambda i, j: (i, j)),
        out_shape=jax.ShapeDtypeStruct((t, D_FF), BF16),
        compiler_params=_cp(("parallel", "parallel")),
    )(gu, gu)


def _act_bwd(dact, gu, name):
    t = gu.shape[0]
    tm = min(t, 256)

    def body(d_ref, g_ref, u_ref, o_ref):
        g, d = g_ref[...].astype(F32), d_ref[...].astype(F32)
        s = _sig(g)
        o_ref[:, 0:D_FF] = (d * u_ref[...].astype(F32) * s * (1.0 + g * (1.0 - s))).astype(BF16)
        o_ref[:, D_FF:2 * D_FF] = (d * g * s).astype(BF16)

    return pl.pallas_call(
        body, name=name, grid=(t // tm,),
        in_specs=[pl.BlockSpec((tm, D_FF), lambda i: (i, 0)), pl.BlockSpec((tm, D_FF), lambda i: (i, 0)),
                  pl.BlockSpec((tm, D_FF), lambda i: (i, 1))],
        out_specs=pl.BlockSpec((tm, 2 * D_FF), lambda i: (i, 0)),
        out_shape=jax.ShapeDtypeStruct((t, 2 * D_FF), BF16),
        compiler_params=_cp(("parallel",)),
    )(dact, gu, gu)


def _loss_head(x1, ffn, tgt, mod, name):
    t = x1.shape[0]
    tm = min(t, 512)

    def body(x1_ref, f_ref, t_ref, mod_ref, dffn_ref, dout_ref, dg2_ref, loss_ref):
        i = pl.program_id(0)
        g2 = mod_ref[5:6, :]
        f = f_ref[...]
        err = x1_ref[...] + g2 * f - t_ref[...]
        dout = err * (1.0 / D_MODEL)
        dout_ref[...] = dout
        dffn_ref[...] = (dout * g2).astype(BF16)
        part = jnp.zeros((8, 128), F32) + 0.5 * jnp.sum(jnp.mean(err * err, axis=-1, keepdims=True))

        @pl.when(i == 0)
        def _():
            dg2_ref[...] = _colsum(dout * f)
            loss_ref[...] = part

        @pl.when(i > 0)
        def _():
            dg2_ref[...] += _colsum(dout * f)
            loss_ref[...] += part

    return pl.pallas_call(
        body, name=name, grid=(t // tm,),
        in_specs=[_row_spec(tm), _row_spec(tm), _row_spec(tm), _fix_spec((8, D_MODEL))],
        out_specs=[_row_spec(tm), _row_spec(tm), _fix_spec((1, D_MODEL)), _fix_spec((8, 128))],
        out_shape=[jax.ShapeDtypeStruct((t, D_MODEL), BF16), jax.ShapeDtypeStruct((t, D_MODEL), F32),
                   jax.ShapeDtypeStruct((1, D_MODEL), F32), jax.ShapeDtypeStruct((8, 128), F32)],
        compiler_params=_cp(("arbitrary",)),
    )(x1, ffn, tgt, mod)


def _norm_bwd(dh, xin, dres, aux, nw, mod, row_sh, gate_row, name):
    t = xin.shape[0]
    tm = min(t, 512)
    with_gate = gate_row is not None

    def body(*refs):
        if with_gate:
            dh_ref, x_ref, dr_ref, aux_ref, nw_ref, mod_ref, dx_ref, dg_ref, acc_ref = refs
        else:
            dh_ref, x_ref, dr_ref, nw_ref, mod_ref, dx_ref, acc_ref = refs
        i = pl.program_id(0)
        xv, dhv = x_ref[...], dh_ref[...]
        r = lax.rsqrt(jnp.mean(xv * xv, axis=-1, keepdims=True) + EPS)
        xn = xv * r
        nwv = nw_ref[...]
        sc1 = 1.0 + mod_ref[row_sh + 1:row_sh + 2, :]
        dxn = dhv * (nwv * sc1)
        dx = dr_ref[...] + r * (dxn - xn * jnp.mean(dxn * xn, axis=-1, keepdims=True))
        dx_ref[...] = dx
        dhx = dhv * xn
        rows = [_colsum(dhv), _colsum(dhx * nwv), _colsum(dhx * sc1)]
        if with_gate:
            dg_ref[...] = (dx * mod_ref[gate_row:gate_row + 1, :]).astype(BF16)
            rows.append(_colsum(dx * aux_ref[...]))

        @pl.when(i == 0)
        def _():
            acc_ref[...] = jnp.zeros_like(acc_ref)

        for k, v in enumerate(rows):
            acc_ref[k:k + 1, :] += v

    ins = [dh, xin, dres] + ([aux] if with_gate else []) + [nw, mod]
    in_specs = [_row_spec(tm)] * (4 if with_gate else 3) + [_fix_spec((1, D_MODEL)), _fix_spec((8, D_MODEL))]
    out_specs = [_row_spec(tm)] + ([_row_spec(tm)] if with_gate else []) + [_fix_spec((8, D_MODEL))]
    out_shape = ([jax.ShapeDtypeStruct((t, D_MODEL), F32)]
                 + ([jax.ShapeDtypeStruct((t, D_MODEL), BF16)] if with_gate else [])
                 + [jax.ShapeDtypeStruct((8, D_MODEL), F32)])
    return pl.pallas_call(
        body, name=name, grid=(t // tm,), in_specs=in_specs, out_specs=out_specs, out_shape=out_shape,
        compiler_params=_cp(("arbitrary",)),
    )(*ins)


XBC_COL0 = 4096 // 128
DT_COL = 5632 // 128


def _conv_pre(xv, w_ref, b_ref):
    t = xv.shape[0]
    row = lax.broadcasted_iota(jnp.int32, xv.shape, 0)
    pre = xv * w_ref[3:4, :] + b_ref[...]
    shifted = []
    for k in range(3):
        s = 3 - k
        xs = jnp.where(row >= s, pltpu.roll(xv, s, 0), 0.0)
        shifted.append(xs)
        pre = pre + xs * w_ref[k:k + 1, :]
    return pre, shifted, row, t


def _conv_fwd(proj, conv_w, conv_b, name):
    t = proj.shape[0]

    def body(x_ref, w_ref, b_ref, u_ref):
        pre, _, _, _ = _conv_pre(x_ref[...], w_ref, b_ref)
        u_ref[...] = pre * _sig(pre)

    return pl.pallas_call(
        body, name=name, grid=(D_CONV // 128,),
        in_specs=[pl.BlockSpec((t, 128), lambda j: (0, XBC_COL0 + j)), pl.BlockSpec((4, 128), lambda j: (0, j)),
                  pl.BlockSpec((1, 128), lambda j: (0, j))],
        out_specs=pl.BlockSpec((t, 128), lambda j: (0, j)),
        out_shape=jax.ShapeDtypeStruct((t, D_CONV), F32),
        compiler_params=_cp(("parallel",)),
    )(proj, conv_w, conv_b)


def _conv_bwd(proj, du, conv_w, conv_b, name):
    t = proj.shape[0]

    def body(x_ref, du_ref, w_ref, b_ref, dx_ref, dw_ref, db_ref):
        pre, shifted, row, _ = _conv_pre(x_ref[...], w_ref, b_ref)
        s = _sig(pre)
        dpre = du_ref[...] * s * (1.0 + pre * (1.0 - s))
        db_ref[...] = _colsum(dpre)
        dx = dpre * w_ref[3:4, :]
        dw_ref[3:4, :] = _colsum(dpre * x_ref[...])
        for k in range(3):
            sft = 3 - k
            dw_ref[k:k + 1, :] = _colsum(dpre * shifted[k])
            back = jnp.where(row < t - sft, pltpu.roll(dpre, t - sft, 0), 0.0)
            dx = dx + back * w_ref[k:k + 1, :]
        dx_ref[...] = dx.astype(BF16)

    return pl.pallas_call(
        body, name=name, grid=(D_CONV // 128,),
        in_specs=[pl.BlockSpec((t, 128), lambda j: (0, XBC_COL0 + j)), pl.BlockSpec((t, 128), lambda j: (0, j)),
                  pl.BlockSpec((4, 128), lambda j: (0, j)), pl.BlockSpec((1, 128), lambda j: (0, j))],
        out_specs=[pl.BlockSpec((t, 128), lambda j: (0, j)), pl.BlockSpec((4, 128), lambda j: (0, j)),
                   pl.BlockSpec((1, 128), lambda j: (0, j))],
        out_shape=[jax.ShapeDtypeStruct((t, D_CONV), BF16), jax.ShapeDtypeStruct((4, D_CONV), F32),
                   jax.ShapeDtypeStruct((1, D_CONV), F32)],
        compiler_params=_cp(("parallel",)),
    )(proj, du, conv_w, conv_b)


def _ssd_common(dtraw_ref, dtb_ref, alog_ref, tril, expand, l_s, lt_s):
    lane = lax.broadcasted_iota(jnp.int32, (1, 128), 1)
    dt = _softplus(dtraw_ref[...] + dtb_ref[...])
    a = jnp.where(lane < N_HEADS, -jnp.exp(alog_ref[...]), 0.0)
    lcs = _dotx_l(tril, dt * a)
    l_s[...] = lcs
    lt_s[...] = lcs.T
    llast = l_s[CHUNK - 1:CHUNK, :]
    ea = _dotx_r(jnp.exp(lcs), expand, n=2)
    ds = _dotx_r(jnp.exp(llast - lcs), expand, n=2)
    dtx = _dotx_r(dt, expand, n=2)
    return dt, a, lcs, llast, ea, ds, dtx


def _head_col(lcs, h):
    lane = lax.broadcasted_iota(jnp.int32, lcs.shape, 1)
    return jnp.sum(jnp.where(lane == h, lcs, 0.0), axis=1, keepdims=True)


def _decay(lcs, lt_s, h, causal):
    seg = _head_col(lcs, h) - lt_s[h:h + 1, :]
    return jnp.exp(jnp.where(causal, seg, -1e30))


def _ssd_fwd(u, proj, dtb, alog, dsk, nw, cst, name):
    t = u.shape[0]
    nc = t // CHUNK

    def body(xs_ref, b_ref, c_ref, dtraw_ref, z_ref, dtb_ref, alog_ref, dsk_ref, nw_ref, tril_ref, exp_ref,
             y_ref, yn_ref, prev_ref, carry, l_s, lt_s, yd_s):
        i = pl.program_id(0)

        @pl.when(i == 0)
        def _():
            carry[...] = jnp.zeros_like(carry)

        expand = exp_ref[...]
        dt, a, lcs, llast, ea, ds, dtx = _ssd_common(dtraw_ref, dtb_ref, alog_ref, tril_ref[...], expand, l_s, lt_s)
        xs = xs_ref[...]
        xg = xs * dtx
        xgb = xg.astype(BF16)
        xgd = (xg * ds).astype(BF16)
        prev = carry[...]
        prev_ref[0] = prev
        prevb = prev.astype(BF16)
        ri = lax.broadcasted_iota(jnp.int32, (CHUNK, CHUNK), 0)
        ci = lax.broadcasted_iota(jnp.int32, (CHUNK, CHUNK), 1)
        causal = ri >= ci
        lane = lax.broadcasted_iota(jnp.int32, (1, 128), 1)
        new_states, yoff = [], []
        for g in range(2):
            bg = b_ref[:, g * 128:(g + 1) * 128].astype(BF16)
            cg = c_ref[:, g * 128:(g + 1) * 128].astype(BF16)
            sc = _dot(cg, bg, "nt")
            gs = slice(g * 512, (g + 1) * 512)
            new_states.append(_dot(bg, xgd[:, gs], "tn"))
            yoff.append(_dot(cg, prevb[:, gs]))
            for pr in range(4):
                col = g * 512 + pr * 128
                xp = xgb[:, col:col + 128]
                acc = jnp.zeros((CHUNK, 128), F32)
                for half in range(2):
                    h = g * 8 + pr * 2 + half
                    m = (sc * _decay(lcs, lt_s, h, causal)).astype(BF16)
                    keep = (lane < HEAD_DIM) if half == 0 else (lane >= HEAD_DIM)
                    acc = acc + _dot(m, jnp.where(keep, xp, jnp.zeros_like(xp)))
                yd_s[:, col:col + 128] = acc
        y = yd_s[...] + jnp.concatenate(yoff, axis=1) * ea + xs * dsk_ref[...]
        y_ref[...] = y
        carry[...] = prev * jnp.max(_dotx_r(jnp.exp(llast) + jnp.zeros((8, 128), F32), expand, n=2), axis=0, keepdims=True) \
            + jnp.concatenate(new_states, axis=1)
        z = z_ref[...]
        yz = y * (z * _sig(z))
        nwv = nw_ref[...]
        for g in range(2):
            gs = slice(g * 512, (g + 1) * 512)
            v = yz[:, gs]
            r = lax.rsqrt(jnp.mean(v * v, axis=-1, keepdims=True) + EPS)
            yn_ref[:, gs] = (v * r * nwv[:, gs]).astype(BF16)

    row = lambda w, col: pl.BlockSpec((CHUNK, w), lambda i: (i, col))
    return pl.pallas_call(
        body, name=name, grid=(nc,),
        in_specs=[row(1024, 0), row(256, 4), row(256, 5), row(128, DT_COL), row(1024, 0),
                  _fix_spec((1, 128)), _fix_spec((1, 128)), _fix_spec((1, D_MODEL)), _fix_spec((1, D_MODEL)),
                  _fix_spec((CHUNK, CHUNK)), _fix_spec((128, D_MODEL))],
        out_specs=[row(1024, 0), row(1024, 0), pl.BlockSpec((1, 128, D_MODEL), lambda i: (i, 0, 0))],
        out_shape=[jax.ShapeDtypeStruct((t, D_MODEL), F32), jax.ShapeDtypeStruct((t, 2 * D_MODEL), BF16),
                   jax.ShapeDtypeStruct((nc, 128, D_MODEL), F32)],
        scratch_shapes=[pltpu.VMEM((128, D_MODEL), F32), pltpu.VMEM((128, 128), F32), pltpu.VMEM((128, 128), F32),
                        pltpu.VMEM((CHUNK, D_MODEL), F32)],
        compiler_params=_cp(("arbitrary",)),
    )(u, u, u, proj, proj, dtb, alog, dsk, nw, cst["tril"], cst["expand"])


def _ssd_bwd(u, proj, y, prev, dycat, dtb, alog, dsk, nw, cst, name):
    t = u.shape[0]
    nc = t // CHUNK

    def body(xs_ref, b_ref, c_ref, dtraw_ref, z_ref, y_ref, prev_ref, dyn_ref, dtb_ref, alog_ref, dsk_ref, nw_ref,
             tril_ref, triu_ref, exp_ref, hs_ref,
             du_ref, ddt_ref, dz_ref, acc_ref, acc16_ref, dcarry, l_s, lt_s, dxg_s):
        i = pl.program_id(0)

        @pl.when(i == 0)
        def _():
            dcarry[...] = jnp.zeros_like(dcarry)
            acc_ref[...] = jnp.zeros_like(acc_ref)
            acc16_ref[...] = jnp.zeros_like(acc16_ref)

        expand, hsum = exp_ref[...], hs_ref[...]
        dt, a, lcs, llast, ea, ds, dtx = _ssd_common(dtraw_ref, dtb_ref, alog_ref, tril_ref[...], expand, l_s, lt_s)
        xs = xs_ref[...]
        xg = xs * dtx
        xgb = xg.astype(BF16)
        xgdf = xg * ds
        xgd = xgdf.astype(BF16)
        dsk_v, nwv = dsk_ref[...], nw_ref[...]
        z, y = z_ref[...], y_ref[...]
        sz = _sig(z)
        silz = z * sz
        yz = y * silz
        dyn = dyn_ref[...]
        dyz_parts, dnw_parts = [], []
        for g in range(2):
            gs = slice(g * 512, (g + 1) * 512)
            v = yz[:, gs]
            r = lax.rsqrt(jnp.mean(v * v, axis=-1, keepdims=True) + EPS)
            yhat = v * r
            dnw_parts.append(_colsum(dyn[:, gs] * yhat))
            dw = dyn[:, gs] * nwv[:, gs]
            dyz_parts.append(r * (dw - yhat * jnp.mean(dw * yhat, axis=-1, keepdims=True)))
        dyz = jnp.concatenate(dyz_parts, axis=1)
        dy = dyz * silz
        dz_ref[...] = (dyz * y * (sz * (1.0 + z * (1.0 - sz)))).astype(BF16)
        acc_ref[0:1, :] += jnp.concatenate(dnw_parts, axis=1)
        acc_ref[1:2, :] += _colsum(dy * xs)
        dyb = dy.astype(BF16)
        dq = (dy * ea).astype(BF16)
        dcar = dcarry[...]
        dcarb = dcar.astype(BF16)
        prev = prev_ref[0]
        prevb = prev.astype(BF16)
        ri = lax.broadcasted_iota(jnp.int32, (CHUNK, CHUNK), 0)
        ci = lax.broadcasted_iota(jnp.int32, (CHUNK, CHUNK), 1)
        causal = ri >= ci
        lane = lax.broadcasted_iota(jnp.int32, (1, 128), 1)
        dprev, dxgd, yoff = [], [], []
        dl_l = jnp.zeros((CHUNK, CHUNK), F32)
        dl_s = jnp.zeros((CHUNK, CHUNK), F32)
        for g in range(2):
            gs = slice(g * 512, (g + 1) * 512)
            bg = b_ref[:, g * 128:(g + 1) * 128].astype(BF16)
            cg = c_ref[:, g * 128:(g + 1) * 128].astype(BF16)
            sc = _dot(cg, bg, "nt")
            yoff.append(_dot(cg, prevb[:, gs]))
            dcg = _dot(dq[:, gs], prevb[:, gs], "nt")
            dprev.append(_dot(cg, dq[:, gs], "tn"))
            dbg = _dot(xgd[:, gs], dcarb[:, gs], "nt")
            dxgd.append(_dot(bg, dcarb[:, gs]))
            dsc = jnp.zeros((CHUNK, CHUNK), F32)
            for pr in range(4):
                col = g * 512 + pr * 128
                xp = xgb[:, col:col + 128]
                dyp = dyb[:, col:col + 128]
                acc = jnp.zeros((CHUNK, 128), F32)
                for half in range(2):
                    h = g * 8 + pr * 2 + half
                    dec = _decay(lcs, lt_s, h, causal)
                    mf = sc * dec
                    keep = (lane < HEAD_DIM) if half == 0 else (lane >= HEAD_DIM)
                    dyh = jnp.where(keep, dyp, jnp.zeros_like(dyp))
                    dm = _dot(dyh, xp, "nt")
                    acc = acc + _dot(mf.astype(BF16), dyh, "tn")
                    dsc = dsc + dm * dec
                    gm = dm * mf
                    dl_l = dl_l + jnp.where(ci == h, jnp.sum(gm, axis=1, keepdims=True), 0.0)
                    dl_s = dl_s + jnp.where(ri == h, jnp.sum(gm, axis=0, keepdims=True), 0.0)
                dxg_s[:, col:col + 128] = acc
            dscb = dsc.astype(BF16)
            dcg = dcg + _dot(dscb, bg)
            dbg = dbg + _dot(dscb, cg, "tn")
            du_ref[:, 1024 + g * 128:1024 + (g + 1) * 128] = dbg
            du_ref[:, 1280 + g * 128:1280 + (g + 1) * 128] = dcg
        dxgd = jnp.concatenate(dxgd, axis=1)
        dxg = dxg_s[...] + dxgd * ds
        du_ref[:, 0:1024] = dy * dsk_v + dxg * dtx
        hs1 = _dotx_r(dxg * xs, hsum)
        yoff = jnp.concatenate(yoff, axis=1) * ea
        dl = dl_l - dl_s.T + _dotx_r(dy * yoff - xgdf * dxgd, hsum)
        rows8 = lax.broadcasted_iota(jnp.int32, (8, D_MODEL), 0)
        two = jnp.where(rows8 == 0, _colsum(dxgd * xgdf), jnp.where(rows8 == 1, _colsum(dcar * prev), 0.0))
        two = _dotx_r(two, hsum)
        r8 = lax.broadcasted_iota(jnp.int32, (8, 128), 0)
        dllast = _colsum(jnp.where(r8 == 0, two, 0.0)) + _colsum(jnp.where(r8 == 1, two, 0.0)) * jnp.exp(llast)
        rowi = lax.broadcasted_iota(jnp.int32, (CHUNK, 128), 0)
        dl = dl + jnp.where(rowi == CHUNK - 1, dllast, 0.0)
        dadt = _dotx_l(triu_ref[...], dl)
        ddt = dadt * a + hs1
        draw = ddt * _sig(dtraw_ref[...] + dtb_ref[...])
        ddt_ref[...] = draw.astype(BF16)
        acc16_ref[0:1, :] += _colsum(draw)
        acc16_ref[1:2, :] += _colsum(dadt * dt) * a
        dcarry[...] = dcar * jnp.max(_dotx_r(jnp.exp(llast) + jnp.zeros((8, 128), F32), expand, n=2), axis=0, keepdims=True) \
            + jnp.concatenate(dprev, axis=1)

        @pl.when(i == nc - 1)
        def _():
            hd = _dotx_r(acc_ref[...], hsum)
            acc16_ref[2:3, :] = _colsum(jnp.where(lax.broadcasted_iota(jnp.int32, (8, 128), 0) == 1, hd, 0.0))

    rev = lambda w, col: pl.BlockSpec((CHUNK, w), lambda i: (nc - 1 - i, col))
    return pl.pallas_call(
        body, name=name, grid=(nc,),
        in_specs=[rev(1024, 0), rev(256, 4), rev(256, 5), rev(128, DT_COL), rev(1024, 0), rev(1024, 0),
                  pl.BlockSpec((1, 128, D_MODEL), lambda i: (nc - 1 - i, 0, 0)), rev(1024, 0),
                  _fix_spec((1, 128)), _fix_spec((1, 128)), _fix_spec((1, D_MODEL)), _fix_spec((1, D_MODEL)),
                  _fix_spec((CHUNK, CHUNK)), _fix_spec((CHUNK, CHUNK)), _fix_spec((128, D_MODEL)),
                  _fix_spec((D_MODEL, 128))],
        out_specs=[rev(D_CONV, 0), rev(128, 0), rev(1024, 0), _fix_spec((8, D_MODEL)), _fix_spec((8, 128))],
        out_shape=[jax.ShapeDtypeStruct((t, D_CONV), F32), jax.ShapeDtypeStruct((t, 128), BF16),
                   jax.ShapeDtypeStruct((t, D_MODEL), BF16), jax.ShapeDtypeStruct((8, D_MODEL), F32),
                   jax.ShapeDtypeStruct((8, 128), F32)],
        scratch_shapes=[pltpu.VMEM((128, D_MODEL), F32), pltpu.VMEM((128, 128), F32), pltpu.VMEM((128, 128), F32),
                        pltpu.VMEM((CHUNK, D_MODEL), F32)],
        compiler_params=_cp(("arbitrary",)),
    )(u, u, u, proj, proj, y, prev, dycat, dtb, alog, dsk, nw,
      cst["tril"], cst["triu"], cst["expand"], cst["hsum"])


def _head_rms(v, hsum, expand):
    ms = _dotx_r(v * v, hsum, n=2) * (1.0 / HEAD_DIM)
    return _dotx_r(lax.rsqrt(ms + EPS), expand, n=2)


def _qk_fwd(proj, qw, kw, cst, name):
    t = proj.shape[0]
    tm = min(t, 256)
    scale = HEAD_DIM ** -0.5

    def body(q_ref, k_ref, v_ref, qw_ref, kw_ref, hs_ref, exp_ref, qs_ref, kn_ref, vb_ref):
        hsum, expand = hs_ref[...], exp_ref[...]
        q, k = q_ref[...], k_ref[...]
        qs_ref[...] = (q * _head_rms(q, hsum, expand) * qw_ref[...] * scale).astype(BF16)
        kn_ref[...] = (k * _head_rms(k, hsum, expand) * kw_ref[...]).astype(BF16)
        vb_ref[...] = v_ref[...].astype(BF16)

    return pl.pallas_call(
        body, name=name, grid=(t // tm,),
        in_specs=[_row_spec(tm, col=1), _row_spec(tm, col=2), _row_spec(tm, col=3),
                  _fix_spec((1, D_MODEL)), _fix_spec((1, D_MODEL)), _fix_spec((D_MODEL, 128)),
                  _fix_spec((128, D_MODEL))],
        out_specs=[_row_spec(tm)] * 3, out_shape=[jax.ShapeDtypeStruct((t, D_MODEL), BF16)] * 3,
        compiler_params=_cp(("parallel",)),
    )(proj, proj, proj, qw, kw, cst["hsum"], cst["expand"])


def _qk_bwd(proj, dqs, dkn, dv, qw, kw, cst, name):
    t = proj.shape[0]
    tm = min(t, 256)
    scale = HEAD_DIM ** -0.5

    def body(q_ref, k_ref, dq_ref, dk_ref, dv_ref, qw_ref, kw_ref, hs_ref, exp_ref, fold_ref,
             oq_ref, ok_ref, ov_ref, dw_ref):
        i = pl.program_id(0)
        hsum, expand = hs_ref[...], exp_ref[...]
        rows8 = lax.broadcasted_iota(jnp.int32, (8, D_MODEL), 0)
        sums = jnp.zeros((8, D_MODEL), F32)
        for n, (x_ref, d_ref, w_ref, o_ref, sc) in enumerate(
                [(q_ref, dq_ref, qw_ref, oq_ref, scale), (k_ref, dk_ref, kw_ref, ok_ref, 1.0)]):
            xv = x_ref[...]
            r = _head_rms(xv, hsum, expand)
            xhat = xv * r
            dn = d_ref[...] * sc
            sums = sums + jnp.where(rows8 == n, _colsum(dn * xhat), 0.0)
            dw = dn * w_ref[...]
            mean = _dotx_r(_dotx_r(dw * xhat, hsum, n=1), expand, n=2) * (1.0 / HEAD_DIM)
            o_ref[...] = (r * (dw - xhat * mean)).astype(BF16)
        ov_ref[...] = dv_ref[...].astype(BF16)
        folded = _dotx_r(sums, fold_ref[...])

        @pl.when(i == 0)
        def _():
            dw_ref[...] = folded

        @pl.when(i > 0)
        def _():
            dw_ref[...] += folded

    return pl.pallas_call(
        body, name=name, grid=(t // tm,),
        in_specs=[_row_spec(tm, col=1), _row_spec(tm, col=2), _row_spec(tm), _row_spec(tm), _row_spec(tm),
                  _fix_spec((1, D_MODEL)), _fix_spec((1, D_MODEL)), _fix_spec((D_MODEL, 128)),
                  _fix_spec((128, D_MODEL)), _fix_spec((D_MODEL, 128))],
        out_specs=[_row_spec(tm)] * 3 + [_fix_spec((8, 128))],
        out_shape=[jax.ShapeDtypeStruct((t, D_MODEL), BF16)] * 3 + [jax.ShapeDtypeStruct((8, 128), F32)],
        compiler_params=_cp(("arbitrary",)),
    )(proj, proj, dqs, dkn, dv, qw, kw, cst["hsum"], cst["expand"], cst["fold"])


def _sb_masks(i, kb, tq, tk):
    tpos = i * tq + lax.broadcasted_iota(jnp.int32, (tq, 1), 0)
    spos = kb * tk + lax.broadcasted_iota(jnp.int32, (1, tk), 1)
    return spos < tpos


def _grid_marks(n0, n1):
    j, i = pl.program_id(0), pl.program_id(1)
    return (jnp.logical_and(j == 0, i == 0), jnp.logical_and(j == n0 // 2, i == 0),
            jnp.logical_and(j == n0 - 1, i == n1 - 1))


def _sb_fwd(qs, kn, vb, pack, ycat, cst, name):
    t = qs.shape[0]
    tq = tk = min(t, SB_TILE)
    nq = t // tq
    ngrp = D_MODEL // SB_LANES
    nh = 2 * SB_PAIRS
    lanes = lambda p: slice(p * 128, (p + 1) * 128)

    def body(q_ref, k_ref, v_ref, u_ref, p_ref, yc_ref, rt_ref, ob_ref, cnt_ref, gat_ref, acc, rs, gss, grs):
        del yc_ref
        at_first, at_mid, at_last = _grid_marks(ngrp, nq)
        g_start, g_relay, g_finish = _gather_stages(p_ref, gat_ref, gss, grs)
        pl.when(at_first)(g_start)
        pl.when(at_mid)(g_relay)
        i = pl.program_id(1)
        lane = lax.broadcasted_iota(jnp.int32, (1, 128), 1)
        qh = []
        for p in range(SB_PAIRS):
            q2 = q_ref[:, lanes(p)]
            zero = jnp.zeros_like(q2)
            qh += [jnp.where(lane < HEAD_DIM, q2, zero), jnp.where(lane >= HEAD_DIM, q2, zero)]
        acc[...] = jnp.zeros_like(acc)
        rs[...] = jnp.zeros_like(rs)
        ustrict = u_ref[...]

        def tile(kb, masked):
            off = pl.multiple_of(kb * tk, tk)
            k2 = [k_ref[pl.ds(off, tk), lanes(p)] for p in range(SB_PAIRS)]
            v2 = [v_ref[pl.ds(off, tk), lanes(p)] for p in range(SB_PAIRS)]
            strict = _sb_masks(i, kb, tq, tk) if masked else None
            s = [_dot(qh[h], k2[h // 2], "nt") for h in range(nh)]
            a, r, lb = [None] * nh, [None] * nh, [None] * nh
            for h in range(nh):
                sp = _softplus(s[h])
                a[h] = s[h] - sp
                r[h] = jnp.where(strict, -sp, 0.0) if masked else -sp
                lb[h] = _dot2(r[h], ustrict)
            for h in range(nh):
                lw = a[h] + lb[h] + rs[h]
                w = jnp.exp(jnp.where(strict, lw, -1e30) if masked else lw)
                rs[h] = rs[h] + jnp.sum(r[h], axis=1, keepdims=True)
                acc[h] = acc[h] + _dot(w.astype(BF16), v2[h // 2])

        tile(i, True)

        def live():
            return jnp.max(functools.reduce(jnp.maximum, [rs[h] for h in range(nh)]))

        def more(c):
            return jnp.logical_and(c[0] < i, c[1] > SB_DEAD)

        def step(c):
            tile(i - 1 - c[0], False)
            return c[0] + 1, live()

        n_off, _ = lax.while_loop(more, step, (jnp.int32(0), live()))
        cnt_ref[pl.program_id(0), i] = n_off.astype(F32)
        for p in range(SB_PAIRS):
            rt_ref[:, lanes(p)] = jnp.where(lane < HEAD_DIM, rs[2 * p], rs[2 * p + 1])
            ob_ref[:, lanes(p)] = jnp.where(lane < HEAD_DIM, acc[2 * p], acc[2 * p + 1]).astype(BF16)
        pl.when(at_last)(g_finish)

    blk = lambda rows, imap: pl.BlockSpec((rows, SB_LANES), imap)
    return pl.pallas_call(
        body, name=name, grid=(ngrp, nq),
        in_specs=[blk(tq, lambda j, i: (i, j)), blk(t, lambda j, i: (0, j)), blk(t, lambda j, i: (0, j)),
                  _fix_spec((2 * tk, tk)), HB, pl.BlockSpec(memory_space=pl.ANY)],
        out_specs=[blk(tq, lambda j, i: (i, j)), blk(tq, lambda j, i: (i, ngrp + j)),
                   pl.BlockSpec(memory_space=pltpu.SMEM), HB],
        out_shape=[jax.ShapeDtypeStruct((t, D_MODEL), F32), jax.ShapeDtypeStruct(ycat.shape, ycat.dtype),
                   jax.ShapeDtypeStruct((ngrp, nq), F32),
                   jax.ShapeDtypeStruct((N_SHARDS,) + pack.shape, pack.dtype)],
        scratch_shapes=[pltpu.VMEM((nh, tq, 128), F32), pltpu.VMEM((nh, tq, 1), F32)] + GATHER_SEMS,
        input_output_aliases={5: 1},
        compiler_params=_cp(("arbitrary", "arbitrary")),
    )(qs, kn, vb, _doubled(cst["ustrict"], tk), pack, ycat)


def _sb_bwd(qs, kn, vb, rtot, cnt, dycat, csum_b, cst, name):
    t = qs.shape[0]
    tq = tk = min(t, SB_TILE)
    nq = t // tq
    ngrp = D_MODEL // SB_LANES
    nh = 2 * SB_PAIRS
    lanes = lambda p: slice(p * 128, (p + 1) * 128)

    def body(q_ref, k_ref, v_ref, rt_ref, do_ref, us_ref, ui_ref, cnt_ref, xs_ref, dq_ref, dk_ref, dv_ref, xr_ref,
             acc, rs, es, xss, xrs):
        at_first, _, at_last = _grid_marks(ngrp, nq)
        x_start, x_finish = _exchange_stages(xs_ref, xr_ref, xss, xrs)
        pl.when(at_first)(x_start)
        i = pl.program_id(1)
        lane = lax.broadcasted_iota(jnp.int32, (1, 128), 1)
        keep = [lane < HEAD_DIM, lane >= HEAD_DIM]
        qh, doh, rtot_h = [], [], []
        for p in range(SB_PAIRS):
            q2, rt = q_ref[:, lanes(p)], rt_ref[:, lanes(p)]
            do2b = do_ref[:, lanes(p)].astype(BF16)
            qh += [jnp.where(kp, q2, jnp.zeros_like(q2)) for kp in keep]
            doh += [jnp.where(kp, do2b, jnp.zeros_like(do2b)) for kp in keep]
            rtot_h += [jnp.sum(jnp.where(lane == n * HEAD_DIM, rt, 0.0), axis=1, keepdims=True) for n in range(2)]
        acc[...] = jnp.zeros_like(acc)
        rs[...] = jnp.zeros_like(rs)
        es[...] = jnp.zeros_like(es)

        @pl.when(i == 0)
        def _():
            dk_ref[...] = jnp.zeros_like(dk_ref)
            dv_ref[...] = jnp.zeros_like(dv_ref)

        ule, ult = us_ref[...], ui_ref[...]

        def tile(kb, masked):
            off = pl.multiple_of(kb * tk, tk)
            k2 = [k_ref[pl.ds(off, tk), lanes(p)] for p in range(SB_PAIRS)]
            v2 = [v_ref[pl.ds(off, tk), lanes(p)] for p in range(SB_PAIRS)]
            strict = _sb_masks(i, kb, tq, tk) if masked else None
            s = [_dot(qh[h], k2[h // 2], "nt") for h in range(nh)]
            dw = [_dot(doh[h], v2[h // 2], "nt") for h in range(nh)]
            a, sg, r, pin, w, e, cin = ([None] * nh for _ in range(7))
            for h in range(nh):
                sp = _softplus(s[h])
                a[h] = s[h] - sp
                sg[h] = jnp.exp(a[h])
                r[h] = jnp.where(strict, -sp, 0.0) if masked else -sp
                pin[h] = _dot2(r[h], ule)
            for h in range(nh):
                lw = a[h] + ((rtot_h[h] - rs[h]) - pin[h])
                w[h] = jnp.exp(jnp.where(strict, lw, -1e30) if masked else lw)
                e[h] = w[h] * dw[h]
                cin[h] = _dot2(e[h], ult)
            for p in range(SB_PAIRS):
                dk_t = jnp.zeros((tk, 128), F32)
                dv_t = jnp.zeros((tk, 128), F32)
                for h in (2 * p, 2 * p + 1):
                    dl = e[h] * (1.0 - sg[h]) - (es[h] + cin[h]) * sg[h]
                    dl = (jnp.where(strict, dl, 0.0) if masked else dl).astype(BF16)
                    rs[h] = rs[h] + jnp.sum(r[h], axis=1, keepdims=True)
                    es[h] = es[h] + jnp.sum(e[h], axis=1, keepdims=True)
                    acc[h] = acc[h] + _dot(dl, k2[p])
                    dk_t = dk_t + _dot(dl, qh[h], "tn")
                    dv_t = dv_t + _dot(w[h].astype(BF16), doh[h], "tn")
                dk_ref[pl.ds(off, tk), lanes(p)] += dk_t
                dv_ref[pl.ds(off, tk), lanes(p)] += dv_t

        def step(kb, carry):
            tile(kb, False)
            return carry

        n_off = cnt_ref[pl.program_id(0), i].astype(jnp.int32)
        lax.fori_loop(i - n_off, i, step, 0)
        tile(i, True)
        for p in range(SB_PAIRS):
            dq_ref[:, lanes(p)] = jnp.where(lane < HEAD_DIM, acc[2 * p], acc[2 * p + 1])
        pl.when(at_last)(x_finish)

    blk = lambda rows, imap: pl.BlockSpec((rows, SB_LANES), imap)
    return pl.pallas_call(
        body, name=name, grid=(ngrp, nq),
        in_specs=[blk(tq, lambda j, i: (i, j)), blk(t, lambda j, i: (0, j)), blk(t, lambda j, i: (0, j)),
                  blk(tq, lambda j, i: (i, j)), blk(tq, lambda j, i: (i, ngrp + j)),
                  _fix_spec((2 * tk, tk)), _fix_spec((2 * tk, tk)), pl.BlockSpec(memory_space=pltpu.SMEM), HB],
        out_specs=[blk(tq, lambda j, i: (i, j)), blk(t, lambda j, i: (0, j)), blk(t, lambda j, i: (0, j)), HB],
        out_shape=[jax.ShapeDtypeStruct((t, D_MODEL), F32)] * 3
        + [jax.ShapeDtypeStruct((3,) + csum_b.shape[1:], csum_b.dtype)],
        scratch_shapes=[pltpu.VMEM((nh, tq, 128), F32), pltpu.VMEM((nh, tq, 1), F32), pltpu.VMEM((nh, tq, 1), F32)]
        + EXCHANGE_SEMS,
        compiler_params=_cp(("arbitrary", "arbitrary")),
    )(qs, kn, vb, rtot, dycat, _doubled(cst["ule"], tk), _doubled(cst["ult"], tk), cnt, csum_b)


def _adamw(w, g, m, v, name):
    lead = (1,) * (w.ndim - 2)
    rows, cols = w.shape[-2:]
    fits = [d for d in range(8, rows, 8) if rows % d == 0 and d * cols * 4 <= ADAM_BLOCK_BYTES]
    tr = max(fits) if fits else rows
    c1 = 1.0 - ADAM_B1 ** ADAM_STEP
    c2 = 1.0 - ADAM_B2 ** ADAM_STEP

    def body(w_ref, g_ref, m_ref, v_ref, d_ref, nm_ref, nv_ref):
        gv = g_ref[...]
        nm = ADAM_B1 * m_ref[...] + (1.0 - ADAM_B1) * gv
        nv = ADAM_B2 * v_ref[...] + (1.0 - ADAM_B2) * (gv * gv)
        nm_ref[...] = nm
        nv_ref[...] = nv
        d_ref[...] = -ADAM_LR * ((nm / c1) / (jnp.sqrt(nv / c2) + ADAM_EPS) + ADAM_WD * w_ref[...])

    spec = pl.BlockSpec(lead + (tr, cols), lambda i: (0,) * len(lead) + (i, 0))
    return pl.pallas_call(
        body, name=name, grid=(rows // tr,), in_specs=[spec] * 4, out_specs=[spec] * 3,
        out_shape=[jax.ShapeDtypeStruct(w.shape, F32)] * 3, compiler_params=_cp(("parallel",)),
    )(w, g, m, v)


def _place():
    x, y, c = lax.axis_index("x"), lax.axis_index("y"), lax.axis_index("c")
    chips = [(1 - x, y), (x, 1 - y), (1 - x, 1 - y)]
    return x, y, c, chips


VM = pl.BlockSpec(memory_space=pltpu.VMEM)
HB = pl.BlockSpec(memory_space=pltpu.HBM)


def _gather_all(p_ref, gat_ref, ss, rs):
    x, y, c, _ = _place()
    me = 4 * x + 2 * y + c
    peers = [(x, y, 1 - c), (1 - x, y, c), (x, 1 - y, c), (1 - x, 1 - y, c),
             (1 - x, y, 1 - c), (x, 1 - y, 1 - c), (1 - x, 1 - y, 1 - c)]

    def copy(k, slot, to):
        return pltpu.make_async_remote_copy(src_ref=p_ref, dst_ref=gat_ref.at[slot], send_sem=ss.at[k],
                                            recv_sem=rs.at[k], device_id=to, device_id_type=MESH)

    sends = [copy(k, me, p) for k, p in enumerate(peers)]
    for s in sends:
        s.start()
    gat_ref[me] = p_ref[...]
    for k, p in enumerate(peers):
        copy(k, 4 * p[0] + 2 * p[1] + p[2], p).wait_recv()
    for s in sends:
        s.wait_send()


ALL_SEMS = [pltpu.SemaphoreType.DMA((7,)), pltpu.SemaphoreType.DMA((7,))]


def _small_reduce(pack, name):
    rows = pack.shape[0]

    def body(p_ref, gat_ref, sum_ref, ss, rs):
        _gather_all(p_ref, gat_ref, ss, rs)
        tot = gat_ref[0]
        for b in range(1, 8):
            tot = tot + gat_ref[b]
        sum_ref[...] = tot

    return pl.pallas_call(
        body, name=name, in_specs=[VM], out_specs=[VM, VM],
        out_shape=[jax.ShapeDtypeStruct((8, rows, D_MODEL), F32), jax.ShapeDtypeStruct((rows, D_MODEL), F32)],
        scratch_shapes=ALL_SEMS, compiler_params=_cp(),
    )(pack)


def _prologue(cpack, w_ada, b_shard, wpack_in, name):
    def body(cp_ref, w_ref, b_ref, p_ref, gat_ref, modp_ref, gin_ref, ss1, rs1, ss2, rs2, gss, grs):
        g_start, g_relay, g_finish = _gather_stages(p_ref, gin_ref, gss, grs)
        g_start()
        _gather_all(cp_ref, gat_ref, ss1, rs1)
        x, y, c, chips = _place()
        sh = 2 * x + y
        row = lax.broadcasted_iota(jnp.int32, (8, D_MODEL), 0)
        cv = jnp.zeros((8, D_MODEL), F32)
        for b in range(8):
            cv = jnp.where(row == b, gat_ref[b, 0:8, :], cv)
        cv = cv * _sig(cv)
        modp_ref[sh] = jnp.dot(cv, w_ref[...], precision=lax.Precision.HIGHEST,
                               preferred_element_type=F32) + b_ref[...]

        def copy(k, slot, to):
            return pltpu.make_async_remote_copy(src_ref=modp_ref.at[slot], dst_ref=modp_ref.at[slot],
                                                send_sem=ss2.at[k], recv_sem=rs2.at[k], device_id=to,
                                                device_id_type=MESH)

        sends = [copy(k, sh, (*ch, c)) for k, ch in enumerate(chips)]
        for s in sends:
            s.start()
        for k, ch in enumerate(chips):
            copy(k, 2 * ch[0] + ch[1], (*ch, c)).wait_recv()
        for s in sends:
            s.wait_send()
        g_relay()
        g_finish()

    return pl.pallas_call(
        body, name=name, in_specs=[VM, VM, VM, HB], out_specs=[VM, VM, HB],
        out_shape=[jax.ShapeDtypeStruct((8,) + cpack.shape, F32),
                   jax.ShapeDtypeStruct((N_SHARDS, 8, 6 * D_MODEL // N_SHARDS), F32),
                   jax.ShapeDtypeStruct((N_SHARDS,) + wpack_in.shape, wpack_in.dtype)],
        scratch_shapes=ALL_SEMS + EXCHANGE_SEMS + GATHER_SEMS, compiler_params=_cp(),
    )(cpack, w_ada, b_shard, wpack_in)


def _gather_stages(p_ref, out_ref, ss, rs):
    hf = p_ref.shape[0] // 2
    x, y, c, chips = _place()
    sh = 2 * x + y
    sib = (x, y, 1 - c)
    slots = [2 * ch[0] + ch[1] for ch in chips]

    def half(slot, hc):
        return out_ref.at[slot, pl.ds(hc * hf, hf), :]

    def copy(k, src, slot, hc, to):
        return pltpu.make_async_remote_copy(src_ref=src, dst_ref=half(slot, hc), send_sem=ss.at[k],
                                            recv_sem=rs.at[k], device_id=to, device_id_type=MESH)

    def first():
        return [copy(j, p_ref.at[pl.ds(c * hf, hf), :], sh, c, (*ch, c)) for j, ch in enumerate(chips)]

    def passed():
        return [copy(3 + j, half(slots[j], c), slots[j], c, sib) for j in range(3)]

    def start():
        for cp in first():
            cp.start()

    def relay():
        for j, cp in enumerate(passed()):
            copy(j, half(slots[j], c), slots[j], c, (*chips[j], c)).wait_recv()
            cp.start()

    def finish():
        for j in range(3):
            copy(3 + j, half(slots[j], 1 - c), slots[j], 1 - c, sib).wait_recv()
        for cp in first() + passed():
            cp.wait_send()

    return start, relay, finish


GATHER_SEMS = [pltpu.SemaphoreType.DMA((6,)), pltpu.SemaphoreType.DMA((6,))]


def _sibling_swap(g, name):
    hf = g.shape[1] // 2

    def body(g_ref, out_ref, ss, rs):
        x, y, c, _ = _place()
        cp = pltpu.make_async_remote_copy(
            src_ref=g_ref.at[pl.ds(0, N_SHARDS), pl.ds((1 - c) * hf, hf), :], dst_ref=out_ref,
            send_sem=ss, recv_sem=rs, device_id=(x, y, 1 - c), device_id_type=MESH)
        cp.start()
        cp.wait()

    return pl.pallas_call(
        body, name=name, in_specs=[HB], out_specs=HB,
        out_shape=jax.ShapeDtypeStruct((N_SHARDS, hf, D_MODEL), g.dtype),
        scratch_shapes=[pltpu.SemaphoreType.DMA, pltpu.SemaphoreType.DMA],
        compiler_params=_cp(),
    )(g)


def _row_tile(rows, width_bytes, cap_bytes):
    fits = [d for d in range(8, rows + 1, 8) if rows % d == 0 and d * width_bytes <= cap_bytes]
    return max(fits)


def _chip_sum(g, got, c_idx, name):
    hf = got.shape[1]
    tr = _row_tile(hf, D_MODEL * 4, 3 << 20)
    nb = hf // tr

    def body(c_ref, a_ref, b_ref, s_ref, sb_ref):
        s = a_ref[...] + b_ref[...]
        s_ref[...] = s
        sb_ref[...] = s.astype(BF16)

    blk = pl.BlockSpec((1, tr, D_MODEL), lambda s, i, c_ref: (s, i, 0))
    return pl.pallas_call(
        body, name=name,
        grid_spec=pltpu.PrefetchScalarGridSpec(
            num_scalar_prefetch=1, grid=(N_SHARDS, nb),
            in_specs=[pl.BlockSpec((1, tr, D_MODEL), lambda s, i, c_ref: (s, c_ref[0] * nb + i, 0)), blk],
            out_specs=[blk, blk]),
        out_shape=[jax.ShapeDtypeStruct((N_SHARDS, hf, D_MODEL), F32),
                   jax.ShapeDtypeStruct((N_SHARDS, hf, D_MODEL), BF16)],
        compiler_params=_cp(("parallel", "parallel")),
    )(c_idx, g, got)


def _exchange_stages(s_ref, out_ref, ss, rs):
    x, y, c, chips = _place()

    def sends():
        return [pltpu.make_async_remote_copy(src_ref=s_ref.at[2 * ch[0] + ch[1]], dst_ref=out_ref.at[k],
                                             send_sem=ss.at[k], recv_sem=rs.at[k], device_id=(*ch, c),
                                             device_id_type=MESH) for k, ch in enumerate(chips)]

    def start():
        for cp in sends():
            cp.start()

    def finish():
        for cp in sends():
            cp.wait()

    return start, finish


EXCHANGE_SEMS = [pltpu.SemaphoreType.DMA((3,)), pltpu.SemaphoreType.DMA((3,))]


def _total_half(s, got, sh_idx, name):
    hf = got.shape[1]
    tr = _row_tile(hf, D_MODEL * 4, 3 << 20)
    nb = hf // tr

    def body(sh_ref, a_ref, r0, r1, r2, o_ref):
        o_ref[...] = ((a_ref[0] + r0[0].astype(F32)) + r1[0].astype(F32)) + r2[0].astype(F32)

    rspec = lambda k: pl.BlockSpec((1, tr, D_MODEL), lambda i, sh_ref: (k, i, 0))
    return pl.pallas_call(
        body, name=name,
        grid_spec=pltpu.PrefetchScalarGridSpec(
            num_scalar_prefetch=1, grid=(nb,),
            in_specs=[pl.BlockSpec((1, tr, D_MODEL), lambda i, sh_ref: (sh_ref[0], i, 0)),
                      rspec(0), rspec(1), rspec(2)],
            out_specs=pl.BlockSpec((tr, D_MODEL), lambda i, sh_ref: (i, 0))),
        out_shape=jax.ShapeDtypeStruct((hf, D_MODEL), F32),
        compiler_params=_cp(("parallel",)),
    )(sh_idx, s, got, got, got)


def _join_halves(tot, name):
    def body(t_ref, out_ref, ss, rs):
        x, y, c, _ = _place()
        cp = pltpu.make_async_remote_copy(src_ref=t_ref, dst_ref=out_ref, send_sem=ss, recv_sem=rs,
                                          device_id=(x, y, 1 - c), device_id_type=MESH)
        cp.start()
        cp.wait()

    return pl.pallas_call(
        body, name=name, in_specs=[HB], out_specs=HB,
        out_shape=jax.ShapeDtypeStruct(tot.shape, F32),
        scratch_shapes=[pltpu.SemaphoreType.DMA, pltpu.SemaphoreType.DMA],
        compiler_params=_cp(),
    )(tot)


def _w_ada_grad(cond, dmod_cols, name):
    def body(c_ref, d_ref, o_ref):
        cv = c_ref[...]
        cv = cv * _sig(cv)
        o_ref[...] = lax.dot_general(cv, d_ref[...], _DN["tn"], precision=lax.Precision.HIGHEST,
                                     preferred_element_type=F32)

    return pl.pallas_call(
        body, name=name, in_specs=[VM, VM], out_specs=VM,
        out_shape=jax.ShapeDtypeStruct((D_MODEL, dmod_cols.shape[1]), F32), compiler_params=_cp(),
    )(cond, dmod_cols)


def _pad_rows(a, rows):
    return jnp.pad(a, ((0, rows - a.shape[0]), (0, 0)))


def _pad_cols(a, cols):
    return jnp.pad(a, ((0, 0), (0, cols - a.shape[1])))


def _unpack_rest(p):
    o = 0
    out = []
    for r in (R_OUT, R_FF, R_FF, R_FF):
        out.append(p[..., o:o + r, :])
        o += r
    return out


def _reduce_tail(csum, got2, shard, ac, tag):
    tot = _total_half(csum, got2, shard.reshape(1).astype(jnp.int32), "rs_total_" + tag)
    other = _join_halves(tot, "rs_join_" + tag)
    return jnp.where(ac == 0, jnp.concatenate([tot, other], axis=0), jnp.concatenate([other, tot], axis=0))


def _reduce_head(gpack, ac, tag):
    got = _sibling_swap(gpack, "rs_sibling_swap_" + tag)
    return _chip_sum(gpack, got, ac.reshape(1).astype(jnp.int32), "rs_chip_sum_" + tag)


def kernel(x, c, w_ada, b_ada, norm1_w, w_in, conv_w, conv_b, dt_bias, a_log, d_skip, ssd_norm_w, q_norm_w, k_norm_w, w_out, norm2_w, w_gate, w_up, w_down, loss_target, m_w_ada, m_b_ada, m_norm1_w, m_w_in, m_conv_w, m_conv_b, m_dt_bias, m_a_log, m_d_skip, m_ssd_norm_w, m_q_norm_w, m_k_norm_w, m_w_out, m_norm2_w, m_w_gate, m_w_up, m_w_down, v_w_ada, v_b_ada, v_norm1_w, v_w_in, v_conv_w, v_conv_b, v_dt_bias, v_a_log, v_d_skip, v_ssd_norm_w, v_q_norm_w, v_k_norm_w, v_w_out, v_norm2_w, v_w_gate, v_w_up, v_w_down):
    cst = _consts()
    ax, ay, ac = lax.axis_index("x"), lax.axis_index("y"), lax.axis_index("c")
    shard = 2 * ax + ay
    me = 4 * ax + 2 * ay + ac
    xs = x[0]
    tgt = loss_target[0]
    w_in_cols = w_in.shape[2]
    conv_cols = conv_w.shape[2]

    tr3 = lambda a: jnp.transpose(a, (0, 2, 1))
    lin = lambda a: tr3(a).reshape(-1, 128)
    unlin = lambda a: tr3(a.reshape(1, w_in_cols, D_MODEL))
    wpack_in = _pad_rows(tr3(w_in.astype(BF16))[0], R_IN)
    wpack_rest = jnp.concatenate([w_out[0], tr3(w_gate)[0], tr3(w_up)[0], w_down[0]], axis=0).astype(BF16)
    own = lax.broadcasted_iota(jnp.int32, (N_SHARDS, 1, 1), 0) == shard

    cw_flat = _pad_cols(conv_w[0].reshape(1, -1), 2 * D_MODEL).reshape(2, D_MODEL)
    cpack = jnp.concatenate([jnp.broadcast_to(c, (8, D_MODEL)), _pad_rows(cw_flat, 8)], axis=0)
    mod_w = 6 * D_MODEL // N_SHARDS
    b_shard = lax.dynamic_slice(b_ada, (0, shard * mod_w), (1, mod_w))
    gat, modp, gp_in = _prologue(cpack, w_ada[0], b_shard, wpack_in, "prologue")
    c_all = gat[:, 0, :]
    cw = gat[0::2, 8:10, :].reshape(N_SHARDS, 2 * D_MODEL)[:, :4 * conv_cols].reshape(N_SHARDS, 4, conv_cols)
    conv_w_full = jnp.transpose(cw, (1, 0, 2)).reshape(4, D_CONV)
    mod_mine = lax.dynamic_slice(modp, (0, me, 0), (N_SHARDS, 1, mod_w)).reshape(6, D_MODEL)
    mod = _pad_rows(mod_mine, 8)
    p_in = jnp.where(own, wpack_in[None], gp_in)
    wi_t = p_in[:, :w_in_cols, :].reshape(D_IN_PROJ, D_MODEL)
    w_inp_t = jnp.concatenate([wi_t[0:1024], wi_t[2576:5648], wi_t[1024:2560], wi_t[2560:2576],
                               jnp.zeros((112, D_MODEL), BF16)], axis=0)

    pad128 = lambda a: _pad_cols(a, 128)
    dtb, alog = pad128(dt_bias), pad128(a_log)
    dsk = jnp.repeat(d_skip, HEAD_DIM, axis=1)
    qw, kw = jnp.tile(q_norm_w, (1, N_HEADS)), jnp.tile(k_norm_w, (1, N_HEADS))

    h1 = _norm_mod(xs, norm1_w, mod, 0, "norm1")
    proj = _matmul(h1, w_inp_t, "nt", F32, "in_proj")
    u = _conv_fwd(proj, conv_w_full, conv_b, "conv_fwd")
    y_ssd, yn, prev = _ssd_fwd(u, proj, dtb, alog, dsk, ssd_norm_w, cst, "ssd_fwd")
    qs, kn, vb = _qk_fwd(proj, qw, kw, cst, "qk_norm")
    rtot, ycat, cnt, gp_rest = _sb_fwd(qs, kn, vb, wpack_rest, yn, cst, "sb_fwd")
    p_out, p_gate, p_up, p_down = _unpack_rest(jnp.where(own, wpack_rest[None], gp_rest))
    w_o = p_out.reshape(2 * D_MODEL, D_MODEL)
    w_gu_t = jnp.concatenate([p_gate.reshape(D_FF, D_MODEL), p_up.reshape(D_FF, D_MODEL)], axis=0)
    w_d = p_down.reshape(D_FF, D_MODEL)
    mix = _matmul(ycat, w_o, "nn", F32, "out_proj")
    x1, h2 = _resid_norm(xs, mix, norm2_w, mod, "resid_norm2")
    gu = _matmul(h2, w_gu_t, "nt", BF16, "ffn_in")
    act = _act_fwd(gu, "ffn_act")
    ffn = _matmul(act, w_d, "nn", F32, "ffn_out", tk_cap=1408)
    dffn, dout, dg2, loss8 = _loss_head(x1, ffn, tgt, mod, "loss_head")
    loss = lax.psum(loss8[0, 0], ("x", "y", "c"))

    dact = _matmul(dffn, w_d, "nt", BF16, "d_act")
    g_down = _matmul(act, dffn, "tn", F32, "g_w_down", tm_cap=1408)
    dgu = _act_bwd(dact, gu, "ffn_act_bwd")
    dh2 = _matmul(dgu, w_gu_t, "nn", F32, "d_h2", tk_cap=1408)
    g_gu_t = _matmul(dgu, h2, "tn", F32, "g_w_gu", tm_cap=1408)
    dx1, dmix, acc2 = _norm_bwd(dh2, x1, dout, mix, norm2_w, mod, 3, 2, "norm2_bwd")
    dycat = _matmul(dmix, w_o, "nt", F32, "d_ycat")
    g_out = _matmul(ycat, dmix, "tn", F32, "g_w_out")
    gpack_rest = jnp.concatenate([
        g_out.reshape(N_SHARDS, R_OUT, D_MODEL),
        g_gu_t[:D_FF].reshape(N_SHARDS, R_FF, D_MODEL), g_gu_t[D_FF:].reshape(N_SHARDS, R_FF, D_MODEL),
        g_down.reshape(N_SHARDS, R_FF, D_MODEL)], axis=1)
    csum_r, csum_rb = _reduce_head(gpack_rest, ac, "rest")
    du, ddt, dz, acc_ssd, acc16 = _ssd_bwd(u, proj, y_ssd, prev, dycat, dtb, alog, dsk, ssd_norm_w, cst, "ssd_bwd")
    dqs, dkn, dv, got_r = _sb_bwd(qs, kn, vb, rtot, cnt, dycat, csum_rb, cst, "sb_bwd")
    r_out, r_gate, r_up, r_down = _unpack_rest(_reduce_tail(csum_r, got_r, shard, ac, "rest"))
    dq, dk, dvb, acc_qk = _qk_bwd(proj, dqs, dkn, dv, qw, kw, cst, "qk_norm_bwd")
    dxbc, g_conv_w, g_conv_b = _conv_bwd(proj, du, conv_w_full, conv_b, "conv_bwd")
    dproj = jnp.concatenate([dz, dq, dk, dvb, dxbc, ddt], axis=1)
    g_inp_t = _matmul(dproj, h1, "tn", F32, "g_w_in", tm_cap=1920)
    gi_t = jnp.concatenate([g_inp_t[0:1024], g_inp_t[4096:5632], g_inp_t[5632:5648], g_inp_t[1024:4096]], axis=0)
    gpack_in = jnp.pad(gi_t.reshape(N_SHARDS, w_in_cols, D_MODEL), ((0, 0), (0, R_IN - w_in_cols), (0, 0)))
    csum_i, csum_ib = _reduce_head(gpack_in, ac, "in")
    dh1, got_i = _matmul(dproj, w_inp_t, "nn", F32, "d_h1", tk_cap=1152, exchange=csum_ib)
    r_in = _reduce_tail(csum_i, got_i, shard, ac, "in")
    grad_x, acc1 = _norm_bwd(dh1, xs, dx1, None, norm1_w, mod, 0, None, "norm1_bwd")

    last = jnp.concatenate([acc_qk[0:1, 0:64], acc_qk[1:2, 0:64], acc16[0:1, 0:16], acc16[1:2, 0:16],
                            acc16[2:3, 0:16]], axis=1)
    spack = jnp.concatenate([
        acc1[0:2], acc2[3:4], acc2[0:2], dg2,
        acc1[2:3], acc2[2:3], acc_ssd[0:1],
        _pad_cols(g_conv_b, 2 * D_MODEL).reshape(2, D_MODEL),
        g_conv_w.reshape(6, D_MODEL),
        _pad_cols(last, D_MODEL)], axis=0)
    sgat, ssum = _small_reduce(_pad_rows(spack, SMALL_ROWS), "gather_small")
    g_b_ada = ssum[0:6].reshape(1, 6 * D_MODEL)
    g_norm1, g_norm2, g_ssdn = ssum[6:7], ssum[7:8], ssum[8:9]
    g_cb = ssum[9:11].reshape(1, 2 * D_MODEL)[:, :D_CONV]
    g_cw = lax.dynamic_slice(ssum[11:17].reshape(4, D_CONV), (0, shard * conv_cols), (4, conv_cols))
    g_qn, g_kn = ssum[17:18, 0:64], ssum[17:18, 64:128]
    g_dtb, g_alog, g_dsk = ssum[17:18, 128:144], ssum[17:18, 144:160], ssum[17:18, 160:176]
    dmod_all = sgat[:, 0:6, :].reshape(8, 6 * D_MODEL)
    g_w_ada = _w_ada_grad(c_all, lax.dynamic_slice(dmod_all, (0, shard * mod_w), (8, mod_w)), "g_w_ada")


    grads = dict(w_ada=g_w_ada, b_ada=g_b_ada, norm1_w=g_norm1, w_in=r_in[:w_in_cols].reshape(-1, 128), conv_w=g_cw,
                 conv_b=g_cb, dt_bias=g_dtb, a_log=g_alog, d_skip=g_dsk, ssd_norm_w=g_ssdn, q_norm_w=g_qn,
                 k_norm_w=g_kn, w_out=r_out, norm2_w=g_norm2, w_gate=r_gate, w_up=r_up, w_down=r_down)
    weights = dict(w_ada=(w_ada, m_w_ada, v_w_ada), b_ada=(b_ada, m_b_ada, v_b_ada),
                   norm1_w=(norm1_w, m_norm1_w, v_norm1_w), w_in=(w_in, m_w_in, v_w_in),
                   conv_w=(conv_w, m_conv_w, v_conv_w), conv_b=(conv_b, m_conv_b, v_conv_b),
                   dt_bias=(dt_bias, m_dt_bias, v_dt_bias), a_log=(a_log, m_a_log, v_a_log),
                   d_skip=(d_skip, m_d_skip, v_d_skip), ssd_norm_w=(ssd_norm_w, m_ssd_norm_w, v_ssd_norm_w),
                   q_norm_w=(q_norm_w, m_q_norm_w, v_q_norm_w), k_norm_w=(k_norm_w, m_k_norm_w, v_k_norm_w),
                   w_out=(w_out, m_w_out, v_w_out), norm2_w=(norm2_w, m_norm2_w, v_norm2_w),
                   w_gate=(w_gate, m_w_gate, v_w_gate), w_up=(w_up, m_w_up, v_w_up),
                   w_down=(w_down, m_w_down, v_w_down))
    views = dict(w_in=(lin, unlin), w_gate=(tr3, tr3), w_up=(tr3, tr3))
    same = lambda a: a
    names = list(weights)
    g_out_l, d_out_l, m_out_l, v_out_l = [], [], [], []
    for n in names:
        view, back = views.get(n, (same, same))
        w, m, v = (view(a) for a in weights[n])
        g = grads[n].reshape(w.shape)
        d, nm, nv = _adamw(w, g, m, v, "adamw_" + n)
        g_out_l.append(back(g))
        d_out_l.append(back(d))
        m_out_l.append(back(nm))
        v_out_l.append(back(nv))
    return (loss, grad_x[None], *g_out_l, *d_out_l, *m_out_l, *v_out_l)
```

```python
import functools

import numpy as np
import jax
import jax.numpy as jnp
from jax import lax
from jax.experimental import pallas as pl
from jax.experimental.pallas import tpu as pltpu

F32, BF16 = jnp.float32, jnp.bfloat16
MESH = pl.DeviceIdType.MESH

D_MODEL = 1024
HEAD_DIM = 64
N_HEADS = 16
D_CONV = 1536
D_FF = 2816
D_IN_PROJ = 5648
D_PROJ_PAD = 5760
CHUNK = 128
SB_TILE = 256
SB_PAIRS = 2
SB_LANES = 128 * SB_PAIRS
SB_DEAD = -105.0
EPS = 1e-6
N_SHARDS = 4
R_IN, R_OUT, R_FF = 1440, 512, 704
SMALL_ROWS = 24

ADAM_LR, ADAM_B1, ADAM_B2, ADAM_EPS, ADAM_WD, ADAM_STEP = 0.001, 0.9, 0.999, 1e-08, 0.01, 10

VMEM_LIMIT = 48 * 1024 * 1024
ADAM_BLOCK_BYTES = 3 * 512 * 1024

_DN = {"nn": (((1,), (0,)), ((), ())), "nt": (((1,), (1,)), ((), ())), "tn": (((0,), (0,)), ((), ()))}


def _dot(a, b, dims="nn"):
    return lax.dot_general(a, b, _DN[dims], preferred_element_type=F32)


def _pieces(x, n):
    out = []
    for _ in range(n - 1):
        hi = lax.bitcast_convert_type(lax.bitcast_convert_type(x, jnp.int32) & jnp.int32(-65536), F32)
        out.append(hi.astype(BF16))
        x = x - hi
    out.append(x.astype(BF16))
    return out


def _dotx_r(x, b_exact, n=3):
    return _dot(jnp.concatenate(_pieces(x, n), axis=1), jnp.concatenate([b_exact] * n, axis=0))


def _dotx_l(a_exact, x, n=3):
    return _dot(jnp.concatenate([a_exact] * n, axis=1), jnp.concatenate(_pieces(x, n), axis=0))


def _dot2(x, b2):
    return _dot(jnp.concatenate(_pieces(x, 2), axis=1), b2)


def _sig(x):
    return 1.0 / (1.0 + jnp.exp(-x))


def _softplus(x):
    return jnp.maximum(x, 0.0) + jnp.log(1.0 + jnp.exp(-jnp.abs(x)))


def _cp(sem=None, vmem=VMEM_LIMIT):
    return pltpu.CompilerParams(dimension_semantics=sem, vmem_limit_bytes=vmem)


def _colsum(x):
    return jnp.sum(x, axis=0, keepdims=True)


def _consts():
    ch = np.arange(D_MODEL)
    expand = (np.arange(128)[:, None] == (ch // HEAD_DIM)[None, :]).astype(np.float32)
    fold = (ch[:, None] % HEAD_DIM == np.arange(128)[None, :]).astype(np.float32)
    i = np.arange(CHUNK)
    tril = (i[:, None] >= i[None, :]).astype(np.float32)
    j = np.arange(SB_TILE)
    ustrict = (j[:, None] > j[None, :]).astype(np.float32)
    ule = (j[:, None] <= j[None, :]).astype(np.float32)
    ult = (j[:, None] < j[None, :]).astype(np.float32)
    c = lambda a: jnp.asarray(a, BF16)
    return dict(expand=c(expand), hsum=c(expand.T), fold=c(fold), tril=c(tril), triu=c(tril.T),
                ustrict=ustrict, ule=ule, ult=ult)


def _doubled(tri, tk):
    b = tri[:tk, :tk]
    return jnp.asarray(np.concatenate([b, b], axis=0), BF16)


def _pick(n, cap):
    best = 128
    for t in range(128, min(n, cap) + 1, 128):
        if n % t == 0:
            best = t
    return n if n <= cap else best


def _matmul(a, b, dims, out_dtype, name, tm_cap=1024, tn_cap=2048, tk_cap=1024, exchange=None):
    if dims == "nn":
        (m, k), (_, n) = a.shape, b.shape
    elif dims == "nt":
        (m, k), (n, _) = a.shape, b.shape
    else:
        (k, m), (_, n) = a.shape, b.shape
    tm, tn, tk = _pick(m, tm_cap), _pick(n, tn_cap), _pick(k, tk_cap)
    nk = k // tk
    a_spec = (pl.BlockSpec((tk, tm), lambda i, j, kk: (kk, i)) if dims == "tn"
              else pl.BlockSpec((tm, tk), lambda i, j, kk: (i, kk)))
    b_spec = (pl.BlockSpec((tn, tk), lambda i, j, kk: (j, kk)) if dims == "nt"
              else pl.BlockSpec((tk, tn), lambda i, j, kk: (kk, j)))

    grid = (m // tm, n // tn, nk)

    def body(a_ref, b_ref, *rest):
        if exchange is None:
            o_ref, acc_ref = rest
        else:
            xs_ref, o_ref, xr_ref, acc_ref, xss, xrs = rest
            ids = [pl.program_id(d) for d in range(3)]
            x_start, x_finish = _exchange_stages(xs_ref, xr_ref, xss, xrs)
            pl.when(functools.reduce(jnp.logical_and, [p == 0 for p in ids]))(x_start)
        kk = pl.program_id(2)
        part = _dot(a_ref[...], b_ref[...], dims)
        if nk == 1:
            o_ref[...] = part.astype(out_dtype)
        else:
            @pl.when(kk == 0)
            def _():
                acc_ref[...] = part

            @pl.when(kk > 0)
            def _():
                acc_ref[...] += part

            @pl.when(kk == nk - 1)
            def _():
                o_ref[...] = acc_ref[...].astype(out_dtype)
        if exchange is not None:
            pl.when(functools.reduce(jnp.logical_and, [p == g - 1 for p, g in zip(ids, grid)]))(x_finish)

    in_specs = [a_spec, b_spec]
    out_specs = [pl.BlockSpec((tm, tn), lambda i, j, kk: (i, j))]
    out_shape = [jax.ShapeDtypeStruct((m, n), out_dtype)]
    scratch = [pltpu.VMEM((tm, tn) if nk > 1 else (8, 128), F32)]
    args = [a, b]
    if exchange is not None:
        in_specs.append(HB)
        out_specs.append(HB)
        out_shape.append(jax.ShapeDtypeStruct((3,) + exchange.shape[1:], exchange.dtype))
        scratch += EXCHANGE_SEMS
        args.append(exchange)
    out = pl.pallas_call(
        body, name=name, grid=grid, in_specs=in_specs, out_specs=out_specs, out_shape=out_shape,
        scratch_shapes=scratch,
        compiler_params=_cp(("parallel", "parallel", "arbitrary") if exchange is None else ("arbitrary",) * 3),
    )(*args)
    return out[0] if exchange is None else out


def _row_spec(tm, width=D_MODEL, col=0):
    return pl.BlockSpec((tm, width), lambda i: (i, col))


def _fix_spec(shape):
    return pl.BlockSpec(shape, lambda *_: (0,) * len(shape))


def _norm_mod(x, nw, mod, row_sh, name):
    t = x.shape[0]
    tm = min(t, 512)

    def body(x_ref, nw_ref, mod_ref, h_ref):
        xv = x_ref[...]
        r = lax.rsqrt(jnp.mean(xv * xv, axis=-1, keepdims=True) + EPS)
        sh = mod_ref[row_sh:row_sh + 1, :]
        sc = mod_ref[row_sh + 1:row_sh + 2, :]
        h_ref[...] = (xv * r * nw_ref[...] * (1.0 + sc) + sh).astype(BF16)

    return pl.pallas_call(
        body, name=name, grid=(t // tm,),
        in_specs=[_row_spec(tm), _fix_spec((1, D_MODEL)), _fix_spec((8, D_MODEL))],
        out_specs=_row_spec(tm), out_shape=jax.ShapeDtypeStruct((t, D_MODEL), BF16),
        compiler_params=_cp(("parallel",)),
    )(x, nw, mod)


def _resid_norm(x, mix, nw, mod, name):
    t = x.shape[0]
    tm = min(t, 512)

    def body(x_ref, mix_ref, nw_ref, mod_ref, x1_ref, h_ref):
        x1 = x_ref[...] + mod_ref[2:3, :] * mix_ref[...]
        x1_ref[...] = x1
        r = lax.rsqrt(jnp.mean(x1 * x1, axis=-1, keepdims=True) + EPS)
        h_ref[...] = (x1 * r * nw_ref[...] * (1.0 + mod_ref[4:5, :]) + mod_ref[3:4, :]).astype(BF16)

    return pl.pallas_call(
        body, name=name, grid=(t // tm,),
        in_specs=[_row_spec(tm), _row_spec(tm), _fix_spec((1, D_MODEL)), _fix_spec((8, D_MODEL))],
        out_specs=[_row_spec(tm), _row_spec(tm)],
        out_shape=[jax.ShapeDtypeStruct((t, D_MODEL), F32), jax.ShapeDtypeStruct((t, D_MODEL), BF16)],
        compiler_params=_cp(("parallel",)),
    )(x, mix, nw, mod)


def _act_fwd(gu, name):
    t = gu.shape[0]
    tm, tn = min(t, 512), D_FF // 2
    nb = D_FF // tn

    def body(g_ref, u_ref, a_ref):
        g = g_ref[...].astype(F32)
        a_ref[...] = (g * _sig(g) * u_ref[...].astype(F32)).astype(BF16)

    return pl.pallas_call(
        body, name=name, grid=(t // tm, nb),
        in_specs=[pl.BlockSpec((tm, tn), lambda i, j: (i, j)), pl.BlockSpec((tm, tn), lambda i, j: (i, j + nb))],
        out_specs=pl.BlockSpec((tm, tn), lambda i, j: (i, j)),
        out_shape=jax.ShapeDtypeStruct((t, D_FF), BF16),
        compiler_params=_cp(("parallel", "parallel")),
    )(gu, gu)


def _act_bwd(dact, gu, name):
    t = gu.shape[0]
    tm = min(t, 256)

    def body(d_ref, g_ref, u_ref, o_ref):
        g, d = g_ref[...].astype(F32), d_ref[...].astype(F32)
        s = _sig(g)
        o_ref[:, 0:D_FF] = (d * u_ref[...].astype(F32) * s * (1.0 + g * (1.0 - s))).astype(BF16)
        o_ref[:, D_FF:2 * D_FF] = (d * g * s).astype(BF16)

    return pl.pallas_call(
        body, name=name, grid=(t // tm,),
        in_specs=[pl.BlockSpec((tm, D_FF), lambda i: (i, 0)), pl.BlockSpec((tm, D_FF), lambda i: (i, 0)),
                  pl.BlockSpec((tm, D_FF), lambda i: (i, 1))],
        out_specs=pl.BlockSpec((tm, 2 * D_FF), lambda i: (i, 0)),
        out_shape=jax.ShapeDtypeStruct((t, 2 * D_FF), BF16),
        compiler_params=_cp(("parallel",)),
    )(dact, gu, gu)


def _loss_head(x1, ffn, tgt, mod, name):
    t = x1.shape[0]
    tm = min(t, 512)

    def body(x1_ref, f_ref, t_ref, mod_ref, dffn_ref, dout_ref, dg2_ref, loss_ref):
        i = pl.program_id(0)
        g2 = mod_ref[5:6, :]
        f = f_ref[...]
        err = x1_ref[...] + g2 * f - t_ref[...]
        dout = err * (1.0 / D_MODEL)
        dout_ref[...] = dout
        dffn_ref[...] = (dout * g2).astype(BF16)
        part = jnp.zeros((8, 128), F32) + 0.5 * jnp.sum(jnp.mean(err * err, axis=-1, keepdims=True))

        @pl.when(i == 0)
        def _():
            dg2_ref[...] = _colsum(dout * f)
            loss_ref[...] = part

        @pl.when(i > 0)
        def _():
            dg2_ref[...] += _colsum(dout * f)
            loss_ref[...] += part

    return pl.pallas_call(
        body, name=name, grid=(t // tm,),
        in_specs=[_row_spec(tm), _row_spec(tm), _row_spec(tm), _fix_spec((8, D_MODEL))],
        out_specs=[_row_spec(tm), _row_spec(tm), _fix_spec((1, D_MODEL)), _fix_spec((8, 128))],
        out_shape=[jax.ShapeDtypeStruct((t, D_MODEL), BF16), jax.ShapeDtypeStruct((t, D_MODEL), F32),
                   jax.ShapeDtypeStruct((1, D_MODEL), F32), jax.ShapeDtypeStruct((8, 128), F32)],
        compiler_params=_cp(("arbitrary",)),
    )(x1, ffn, tgt, mod)


def _norm_bwd(dh, xin, dres, aux, nw, mod, row_sh, gate_row, name):
    t = xin.shape[0]
    tm = min(t, 512)
    with_gate = gate_row is not None

    def body(*refs):
        if with_gate:
            dh_ref, x_ref, dr_ref, aux_ref, nw_ref, mod_ref, dx_ref, dg_ref, acc_ref = refs
        else:
            dh_ref, x_ref, dr_ref, nw_ref, mod_ref, dx_ref, acc_ref = refs
        i = pl.program_id(0)
        xv, dhv = x_ref[...], dh_ref[...]
        r = lax.rsqrt(jnp.mean(xv * xv, axis=-1, keepdims=True) + EPS)
        xn = xv * r
        nwv = nw_ref[...]
        sc1 = 1.0 + mod_ref[row_sh + 1:row_sh + 2, :]
        dxn = dhv * (nwv * sc1)
        dx = dr_ref[...] + r * (dxn - xn * jnp.mean(dxn * xn, axis=-1, keepdims=True))
        dx_ref[...] = dx
        dhx = dhv * xn
        rows = [_colsum(dhv), _colsum(dhx * nwv), _colsum(dhx * sc1)]
        if with_gate:
            dg_ref[...] = (dx * mod_ref[gate_row:gate_row + 1, :]).astype(BF16)
            rows.append(_colsum(dx * aux_ref[...]))

        @pl.when(i == 0)
        def _():
            acc_ref[...] = jnp.zeros_like(acc_ref)

        for k, v in enumerate(rows):
            acc_ref[k:k + 1, :] += v

    ins = [dh, xin, dres] + ([aux] if with_gate else []) + [nw, mod]
    in_specs = [_row_spec(tm)] * (4 if with_gate else 3) + [_fix_spec((1, D_MODEL)), _fix_spec((8, D_MODEL))]
    out_specs = [_row_spec(tm)] + ([_row_spec(tm)] if with_gate else []) + [_fix_spec((8, D_MODEL))]
    out_shape = ([jax.ShapeDtypeStruct((t, D_MODEL), F32)]
                 + ([jax.ShapeDtypeStruct((t, D_MODEL), BF16)] if with_gate else [])
                 + [jax.ShapeDtypeStruct((8, D_MODEL), F32)])
    return pl.pallas_call(
        body, name=name, grid=(t // tm,), in_specs=in_specs, out_specs=out_specs, out_shape=out_shape,
        compiler_params=_cp(("arbitrary",)),
    )(*ins)


XBC_COL0 = 4096 // 128
DT_COL = 5632 // 128


def _conv_pre(xv, w_ref, b_ref):
    t = xv.shape[0]
    row = lax.broadcasted_iota(jnp.int32, xv.shape, 0)
    pre = xv * w_ref[3:4, :] + b_ref[...]
    shifted = []
    for k in range(3):
        s = 3 - k
        xs = jnp.where(row >= s, pltpu.roll(xv, s, 0), 0.0)
        shifted.append(xs)
        pre = pre + xs * w_ref[k:k + 1, :]
    return pre, shifted, row, t


def _conv_fwd(proj, conv_w, conv_b, name):
    t = proj.shape[0]

    def body(x_ref, w_ref, b_ref, u_ref):
        pre, _, _, _ = _conv_pre(x_ref[...], w_ref, b_ref)
        u_ref[...] = pre * _sig(pre)

    return pl.pallas_call(
        body, name=name, grid=(D_CONV // 128,),
        in_specs=[pl.BlockSpec((t, 128), lambda j: (0, XBC_COL0 + j)), pl.BlockSpec((4, 128), lambda j: (0, j)),
                  pl.BlockSpec((1, 128), lambda j: (0, j))],
        out_specs=pl.BlockSpec((t, 128), lambda j: (0, j)),
        out_shape=jax.ShapeDtypeStruct((t, D_CONV), F32),
        compiler_params=_cp(("parallel",)),
    )(proj, conv_w, conv_b)


def _conv_bwd(proj, du, conv_w, conv_b, name):
    t = proj.shape[0]

    def body(x_ref, du_ref, w_ref, b_ref, dx_ref, dw_ref, db_ref):
        pre, shifted, row, _ = _conv_pre(x_ref[...], w_ref, b_ref)
        s = _sig(pre)
        dpre = du_ref[...] * s * (1.0 + pre * (1.0 - s))
        db_ref[...] = _colsum(dpre)
        dx = dpre * w_ref[3:4, :]
        dw_ref[3:4, :] = _colsum(dpre * x_ref[...])
        for k in range(3):
            sft = 3 - k
            dw_ref[k:k + 1, :] = _colsum(dpre * shifted[k])
            back = jnp.where(row < t - sft, pltpu.roll(dpre, t - sft, 0), 0.0)
            dx = dx + back * w_ref[k:k + 1, :]
        dx_ref[...] = dx.astype(BF16)

    return pl.pallas_call(
        body, name=name, grid=(D_CONV // 128,),
        in_specs=[pl.BlockSpec((t, 128), lambda j: (0, XBC_COL0 + j)), pl.BlockSpec((t, 128), lambda j: (0, j)),
                  pl.BlockSpec((4, 128), lambda j: (0, j)), pl.BlockSpec((1, 128), lambda j: (0, j))],
        out_specs=[pl.BlockSpec((t, 128), lambda j: (0, j)), pl.BlockSpec((4, 128), lambda j: (0, j)),
                   pl.BlockSpec((1, 128), lambda j: (0, j))],
        out_shape=[jax.ShapeDtypeStruct((t, D_CONV), BF16), jax.ShapeDtypeStruct((4, D_CONV), F32),
                   jax.ShapeDtypeStruct((1, D_CONV), F32)],
        compiler_params=_cp(("parallel",)),
    )(proj, du, conv_w, conv_b)


def _ssd_common(dtraw_ref, dtb_ref, alog_ref, tril, expand, l_s, lt_s):
    lane = lax.broadcasted_iota(jnp.int32, (1, 128), 1)
    dt = _softplus(dtraw_ref[...] + dtb_ref[...])
    a = jnp.where(lane < N_HEADS, -jnp.exp(alog_ref[...]), 0.0)
    lcs = _dotx_l(tril, dt * a)
    l_s[...] = lcs
    lt_s[...] = lcs.T
    llast = l_s[CHUNK - 1:CHUNK, :]
    ea = _dotx_r(jnp.exp(lcs), expand, n=2)
    ds = _dotx_r(jnp.exp(llast - lcs), expand, n=2)
    dtx = _dotx_r(dt, expand, n=2)
    return dt, a, lcs, llast, ea, ds, dtx


def _head_col(lcs, h):
    lane = lax.broadcasted_iota(jnp.int32, lcs.shape, 1)
    return jnp.sum(jnp.where(lane == h, lcs, 0.0), axis=1, keepdims=True)


def _decay(lcs, lt_s, h, causal):
    seg = _head_col(lcs, h) - lt_s[h:h + 1, :]
    return jnp.exp(jnp.where(causal, seg, -1e30))


def _ssd_fwd(u, proj, dtb, alog, dsk, nw, cst, name):
    t = u.shape[0]
    nc = t // CHUNK

    def body(xs_ref, b_ref, c_ref, dtraw_ref, z_ref, dtb_ref, alog_ref, dsk_ref, nw_ref, tril_ref, exp_ref,
             y_ref, yn_ref, prev_ref, carry, l_s, lt_s, yd_s):
        i = pl.program_id(0)

        @pl.when(i == 0)
        def _():
            carry[...] = jnp.zeros_like(carry)

        expand = exp_ref[...]
        dt, a, lcs, llast, ea, ds, dtx = _ssd_common(dtraw_ref, dtb_ref, alog_ref, tril_ref[...], expand, l_s, lt_s)
        xs = xs_ref[...]
        xg = xs * dtx
        xgb = xg.astype(BF16)
        xgd = (xg * ds).astype(BF16)
        prev = carry[...]
        prev_ref[0] = prev
        prevb = prev.astype(BF16)
        ri = lax.broadcasted_iota(jnp.int32, (CHUNK, CHUNK), 0)
        ci = lax.broadcasted_iota(jnp.int32, (CHUNK, CHUNK), 1)
        causal = ri >= ci
        lane = lax.broadcasted_iota(jnp.int32, (1, 128), 1)
        new_states, yoff = [], []
        for g in range(2):
            bg = b_ref[:, g * 128:(g + 1) * 128].astype(BF16)
            cg = c_ref[:, g * 128:(g + 1) * 128].astype(BF16)
            sc = _dot(cg, bg, "nt")
            gs = slice(g * 512, (g + 1) * 512)
            new_states.append(_dot(bg, xgd[:, gs], "tn"))
            yoff.append(_dot(cg, prevb[:, gs]))
            for pr in range(4):
                col = g * 512 + pr * 128
                xp = xgb[:, col:col + 128]
                acc = jnp.zeros((CHUNK, 128), F32)
                for half in range(2):
                    h = g * 8 + pr * 2 + half
                    m = (sc * _decay(lcs, lt_s, h, causal)).astype(BF16)
                    keep = (lane < HEAD_DIM) if half == 0 else (lane >= HEAD_DIM)
                    acc = acc + _dot(m, jnp.where(keep, xp, jnp.zeros_like(xp)))
                yd_s[:, col:col + 128] = acc
        y = yd_s[...] + jnp.concatenate(yoff, axis=1) * ea + xs * dsk_ref[...]
        y_ref[...] = y
        carry[...] = prev * jnp.max(_dotx_r(jnp.exp(llast) + jnp.zeros((8, 128), F32), expand, n=2), axis=0, keepdims=True) \
            + jnp.concatenate(new_states, axis=1)
        z = z_ref[...]
        yz = y * (z * _sig(z))
        nwv = nw_ref[...]
        for g in range(2):
            gs = slice(g * 512, (g + 1) * 512)
            v = yz[:, gs]
            r = lax.rsqrt(jnp.mean(v * v, axis=-1, keepdims=True) + EPS)
            yn_ref[:, gs] = (v * r * nwv[:, gs]).astype(BF16)

    row = lambda w, col: pl.BlockSpec((CHUNK, w), lambda i: (i, col))
    return pl.pallas_call(
        body, name=name, grid=(nc,),
        in_specs=[row(1024, 0), row(256, 4), row(256, 5), row(128, DT_COL), row(1024, 0),
                  _fix_spec((1, 128)), _fix_spec((1, 128)), _fix_spec((1, D_MODEL)), _fix_spec((1, D_MODEL)),
                  _fix_spec((CHUNK, CHUNK)), _fix_spec((128, D_MODEL))],
        out_specs=[row(1024, 0), row(1024, 0), pl.BlockSpec((1, 128, D_MODEL), lambda i: (i, 0, 0))],
        out_shape=[jax.ShapeDtypeStruct((t, D_MODEL), F32), jax.ShapeDtypeStruct((t, 2 * D_MODEL), BF16),
                   jax.ShapeDtypeStruct((nc, 128, D_MODEL), F32)],
        scratch_shapes=[pltpu.VMEM((128, D_MODEL), F32), pltpu.VMEM((128, 128), F32), pltpu.VMEM((128, 128), F32),
                        pltpu.VMEM((CHUNK, D_MODEL), F32)],
        compiler_params=_cp(("arbitrary",)),
    )(u, u, u, proj, proj, dtb, alog, dsk, nw, cst["tril"], cst["expand"])


def _ssd_bwd(u, proj, y, prev, dycat, dtb, alog, dsk, nw, gpack, cst, name):
    t = u.shape[0]
    nc = t // CHUNK

    def body(xs_ref, b_ref, c_ref, dtraw_ref, z_ref, y_ref, prev_ref, dyn_ref, dtb_ref, alog_ref, dsk_ref, nw_ref,
             tril_ref, triu_ref, exp_ref, hs_ref, g_ref,
             du_ref, ddt_ref, dz_ref, acc_ref, acc16_ref, got_ref, dcarry, l_s, lt_s, dxg_s, wss, wrs):
        i = pl.program_id(0)
        w_start, w_finish = _swap_stages(g_ref, got_ref, wss, wrs)
        pl.when(i == 0)(w_start)

        @pl.when(i == 0)
        def _():
            dcarry[...] = jnp.zeros_like(dcarry)
            acc_ref[...] = jnp.zeros_like(acc_ref)
            acc16_ref[...] = jnp.zeros_like(acc16_ref)

        expand, hsum = exp_ref[...], hs_ref[...]
        dt, a, lcs, llast, ea, ds, dtx = _ssd_common(dtraw_ref, dtb_ref, alog_ref, tril_ref[...], expand, l_s, lt_s)
        xs = xs_ref[...]
        xg = xs * dtx
        xgb = xg.astype(BF16)
        xgdf = xg * ds
        xgd = xgdf.astype(BF16)
        dsk_v, nwv = dsk_ref[...], nw_ref[...]
        z, y = z_ref[...], y_ref[...]
        sz = _sig(z)
        silz = z * sz
        yz = y * silz
        dyn = dyn_ref[...]
        dyz_parts, dnw_parts = [], []
        for g in range(2):
            gs = slice(g * 512, (g + 1) * 512)
            v = yz[:, gs]
            r = lax.rsqrt(jnp.mean(v * v, axis=-1, keepdims=True) + EPS)
            yhat = v * r
            dnw_parts.append(_colsum(dyn[:, gs] * yhat))
            dw = dyn[:, gs] * nwv[:, gs]
            dyz_parts.append(r * (dw - yhat * jnp.mean(dw * yhat, axis=-1, keepdims=True)))
        dyz = jnp.concatenate(dyz_parts, axis=1)
        dy = dyz * silz
        dz_ref[...] = (dyz * y * (sz * (1.0 + z * (1.0 - sz)))).astype(BF16)
        acc_ref[0:1, :] += jnp.concatenate(dnw_parts, axis=1)
        acc_ref[1:2, :] += _colsum(dy * xs)
        dyb = dy.astype(BF16)
        dq = (dy * ea).astype(BF16)
        dcar = dcarry[...]
        dcarb = dcar.astype(BF16)
        prev = prev_ref[0]
        prevb = prev.astype(BF16)
        ri = lax.broadcasted_iota(jnp.int32, (CHUNK, CHUNK), 0)
        ci = lax.broadcasted_iota(jnp.int32, (CHUNK, CHUNK), 1)
        causal = ri >= ci
        lane = lax.broadcasted_iota(jnp.int32, (1, 128), 1)
        dprev, dxgd, yoff = [], [], []
        dl_l = jnp.zeros((CHUNK, CHUNK), F32)
        dl_s = jnp.zeros((CHUNK, CHUNK), F32)
        for g in range(2):
            gs = slice(g * 512, (g + 1) * 512)
            bg = b_ref[:, g * 128:(g + 1) * 128].astype(BF16)
            cg = c_ref[:, g * 128:(g + 1) * 128].astype(BF16)
            sc = _dot(cg, bg, "nt")
            yoff.append(_dot(cg, prevb[:, gs]))
            dcg = _dot(dq[:, gs], prevb[:, gs], "nt")
            dprev.append(_dot(cg, dq[:, gs], "tn"))
            dbg = _dot(xgd[:, gs], dcarb[:, gs], "nt")
            dxgd.append(_dot(bg, dcarb[:, gs]))
            dsc = jnp.zeros((CHUNK, CHUNK), F32)
            for pr in range(4):
                col = g * 512 + pr * 128
                xp = xgb[:, col:col + 128]
                dyp = dyb[:, col:col + 128]
                acc = jnp.zeros((CHUNK, 128), F32)
                for half in range(2):
                    h = g * 8 + pr * 2 + half
                    dec = _decay(lcs, lt_s, h, causal)
                    mf = sc * dec
                    keep = (lane < HEAD_DIM) if half == 0 else (lane >= HEAD_DIM)
                    dyh = jnp.where(keep, dyp, jnp.zeros_like(dyp))
                    dm = _dot(dyh, xp, "nt")
                    acc = acc + _dot(mf.astype(BF16), dyh, "tn")
                    dsc = dsc + dm * dec
                    gm = dm * mf
                    dl_l = dl_l + jnp.where(ci == h, jnp.sum(gm, axis=1, keepdims=True), 0.0)
                    dl_s = dl_s + jnp.where(ri == h, jnp.sum(gm, axis=0, keepdims=True), 0.0)
                dxg_s[:, col:col + 128] = acc
            dscb = dsc.astype(BF16)
            dcg = dcg + _dot(dscb, bg)
            dbg = dbg + _dot(dscb, cg, "tn")
            du_ref[:, 1024 + g * 128:1024 + (g + 1) * 128] = dbg
            du_ref[:, 1280 + g * 128:1280 + (g + 1) * 128] = dcg
        dxgd = jnp.concatenate(dxgd, axis=1)
        dxg = dxg_s[...] + dxgd * ds
        du_ref[:, 0:1024] = dy * dsk_v + dxg * dtx
        hs1 = _dotx_r(dxg * xs, hsum)
        yoff = jnp.concatenate(yoff, axis=1) * ea
        dl = dl_l - dl_s.T + _dotx_r(dy * yoff - xgdf * dxgd, hsum)
        rows8 = lax.broadcasted_iota(jnp.int32, (8, D_MODEL), 0)
        two = jnp.where(rows8 == 0, _colsum(dxgd * xgdf), jnp.where(rows8 == 1, _colsum(dcar * prev), 0.0))
        two = _dotx_r(two, hsum)
        r8 = lax.broadcasted_iota(jnp.int32, (8, 128), 0)
        dllast = _colsum(jnp.where(r8 == 0, two, 0.0)) + _colsum(jnp.where(r8 == 1, two, 0.0)) * jnp.exp(llast)
        rowi = lax.broadcasted_iota(jnp.int32, (CHUNK, 128), 0)
        dl = dl + jnp.where(rowi == CHUNK - 1, dllast, 0.0)
        dadt = _dotx_l(triu_ref[...], dl)
        ddt = dadt * a + hs1
        draw = ddt * _sig(dtraw_ref[...] + dtb_ref[...])
        ddt_ref[...] = draw.astype(BF16)
        acc16_ref[0:1, :] += _colsum(draw)
        acc16_ref[1:2, :] += _colsum(dadt * dt) * a
        dcarry[...] = dcar * jnp.max(_dotx_r(jnp.exp(llast) + jnp.zeros((8, 128), F32), expand, n=2), axis=0, keepdims=True) \
            + jnp.concatenate(dprev, axis=1)

        @pl.when(i == nc - 1)
        def _():
            hd = _dotx_r(acc_ref[...], hsum)
            acc16_ref[2:3, :] = _colsum(jnp.where(lax.broadcasted_iota(jnp.int32, (8, 128), 0) == 1, hd, 0.0))

        pl.when(i == nc - 1)(w_finish)

    rev = lambda w, col: pl.BlockSpec((CHUNK, w), lambda i: (nc - 1 - i, col))
    return pl.pallas_call(
        body, name=name, grid=(nc,),
        in_specs=[rev(1024, 0), rev(256, 4), rev(256, 5), rev(128, DT_COL), rev(1024, 0), rev(1024, 0),
                  pl.BlockSpec((1, 128, D_MODEL), lambda i: (nc - 1 - i, 0, 0)), rev(1024, 0),
                  _fix_spec((1, 128)), _fix_spec((1, 128)), _fix_spec((1, D_MODEL)), _fix_spec((1, D_MODEL)),
                  _fix_spec((CHUNK, CHUNK)), _fix_spec((CHUNK, CHUNK)), _fix_spec((128, D_MODEL)),
                  _fix_spec((D_MODEL, 128)), HB],
        out_specs=[rev(D_CONV, 0), rev(128, 0), rev(1024, 0), _fix_spec((8, D_MODEL)), _fix_spec((8, 128)), HB],
        out_shape=[jax.ShapeDtypeStruct((t, D_CONV), F32), jax.ShapeDtypeStruct((t, 128), BF16),
                   jax.ShapeDtypeStruct((t, D_MODEL), BF16), jax.ShapeDtypeStruct((8, D_MODEL), F32),
                   jax.ShapeDtypeStruct((8, 128), F32), _swap_shape(gpack)],
        scratch_shapes=[pltpu.VMEM((128, D_MODEL), F32), pltpu.VMEM((128, 128), F32), pltpu.VMEM((128, 128), F32),
                        pltpu.VMEM((CHUNK, D_MODEL), F32)] + SWAP_SEMS,
        compiler_params=_cp(("arbitrary",)),
    )(u, u, u, proj, proj, y, prev, dycat, dtb, alog, dsk, nw,
      cst["tril"], cst["triu"], cst["expand"], cst["hsum"], gpack)


def _head_rms(v, hsum, expand):
    ms = _dotx_r(v * v, hsum, n=2) * (1.0 / HEAD_DIM)
    return _dotx_r(lax.rsqrt(ms + EPS), expand, n=2)


def _qk_fwd(proj, qw, kw, cst, name):
    t = proj.shape[0]
    tm = min(t, 256)
    scale = HEAD_DIM ** -0.5

    def body(q_ref, k_ref, v_ref, qw_ref, kw_ref, hs_ref, exp_ref, qs_ref, kn_ref, vb_ref):
        hsum, expand = hs_ref[...], exp_ref[...]
        q, k = q_ref[...], k_ref[...]
        qs_ref[...] = (q * _head_rms(q, hsum, expand) * qw_ref[...] * scale).astype(BF16)
        kn_ref[...] = (k * _head_rms(k, hsum, expand) * kw_ref[...]).astype(BF16)
        vb_ref[...] = v_ref[...].astype(BF16)

    return pl.pallas_call(
        body, name=name, grid=(t // tm,),
        in_specs=[_row_spec(tm, col=1), _row_spec(tm, col=2), _row_spec(tm, col=3),
                  _fix_spec((1, D_MODEL)), _fix_spec((1, D_MODEL)), _fix_spec((D_MODEL, 128)),
                  _fix_spec((128, D_MODEL))],
        out_specs=[_row_spec(tm)] * 3, out_shape=[jax.ShapeDtypeStruct((t, D_MODEL), BF16)] * 3,
        compiler_params=_cp(("parallel",)),
    )(proj, proj, proj, qw, kw, cst["hsum"], cst["expand"])


def _qk_bwd(proj, dqs, dkn, dv, qw, kw, cst, name):
    t = proj.shape[0]
    tm = min(t, 256)
    scale = HEAD_DIM ** -0.5

    def body(q_ref, k_ref, dq_ref, dk_ref, dv_ref, qw_ref, kw_ref, hs_ref, exp_ref, fold_ref,
             oq_ref, ok_ref, ov_ref, dw_ref):
        i = pl.program_id(0)
        hsum, expand = hs_ref[...], exp_ref[...]
        rows8 = lax.broadcasted_iota(jnp.int32, (8, D_MODEL), 0)
        sums = jnp.zeros((8, D_MODEL), F32)
        for n, (x_ref, d_ref, w_ref, o_ref, sc) in enumerate(
                [(q_ref, dq_ref, qw_ref, oq_ref, scale), (k_ref, dk_ref, kw_ref, ok_ref, 1.0)]):
            xv = x_ref[...]
            r = _head_rms(xv, hsum, expand)
            xhat = xv * r
            dn = d_ref[...] * sc
            sums = sums + jnp.where(rows8 == n, _colsum(dn * xhat), 0.0)
            dw = dn * w_ref[...]
            mean = _dotx_r(_dotx_r(dw * xhat, hsum, n=1), expand, n=2) * (1.0 / HEAD_DIM)
            o_ref[...] = (r * (dw - xhat * mean)).astype(BF16)
        ov_ref[...] = dv_ref[...].astype(BF16)
        folded = _dotx_r(sums, fold_ref[...])

        @pl.when(i == 0)
        def _():
            dw_ref[...] = folded

        @pl.when(i > 0)
        def _():
            dw_ref[...] += folded

    return pl.pallas_call(
        body, name=name, grid=(t // tm,),
        in_specs=[_row_spec(tm, col=1), _row_spec(tm, col=2), _row_spec(tm), _row_spec(tm), _row_spec(tm),
                  _fix_spec((1, D_MODEL)), _fix_spec((1, D_MODEL)), _fix_spec((D_MODEL, 128)),
                  _fix_spec((128, D_MODEL)), _fix_spec((D_MODEL, 128))],
        out_specs=[_row_spec(tm)] * 3 + [_fix_spec((8, 128))],
        out_shape=[jax.ShapeDtypeStruct((t, D_MODEL), BF16)] * 3 + [jax.ShapeDtypeStruct((8, 128), F32)],
        compiler_params=_cp(("arbitrary",)),
    )(proj, proj, dqs, dkn, dv, qw, kw, cst["hsum"], cst["expand"], cst["fold"])


def _sb_masks(i, kb, tq, tk):
    tpos = i * tq + lax.broadcasted_iota(jnp.int32, (tq, 1), 0)
    spos = kb * tk + lax.broadcasted_iota(jnp.int32, (1, tk), 1)
    return spos < tpos


def _grid_marks(n0, n1):
    j, i = pl.program_id(0), pl.program_id(1)
    return (jnp.logical_and(j == 0, i == 0), jnp.logical_and(j == n0 // 2, i == 0),
            jnp.logical_and(j == n0 - 1, i == n1 - 1))


def _sb_fwd(qs, kn, vb, pack, ycat, cst, name):
    t = qs.shape[0]
    tq = tk = min(t, SB_TILE)
    nq = t // tq
    ngrp = D_MODEL // SB_LANES
    nh = 2 * SB_PAIRS
    lanes = lambda p: slice(p * 128, (p + 1) * 128)

    def body(q_ref, k_ref, v_ref, u_ref, p_ref, yc_ref, rt_ref, ob_ref, cnt_ref, gat_ref, acc, rs, gss, grs):
        del yc_ref
        at_first, at_mid, at_last = _grid_marks(ngrp, nq)
        g_start, g_relay, g_finish = _gather_stages(p_ref, gat_ref, gss, grs)
        pl.when(at_first)(g_start)
        pl.when(at_mid)(g_relay)
        i = pl.program_id(1)
        lane = lax.broadcasted_iota(jnp.int32, (1, 128), 1)
        qh = []
        for p in range(SB_PAIRS):
            q2 = q_ref[:, lanes(p)]
            zero = jnp.zeros_like(q2)
            qh += [jnp.where(lane < HEAD_DIM, q2, zero), jnp.where(lane >= HEAD_DIM, q2, zero)]
        acc[...] = jnp.zeros_like(acc)
        rs[...] = jnp.zeros_like(rs)
        ustrict = u_ref[...]

        def tile(kb, masked):
            off = pl.multiple_of(kb * tk, tk)
            k2 = [k_ref[pl.ds(off, tk), lanes(p)] for p in range(SB_PAIRS)]
            v2 = [v_ref[pl.ds(off, tk), lanes(p)] for p in range(SB_PAIRS)]
            strict = _sb_masks(i, kb, tq, tk) if masked else None
            s = [_dot(qh[h], k2[h // 2], "nt") for h in range(nh)]
            a, r, lb = [None] * nh, [None] * nh, [None] * nh
            for h in range(nh):
                sp = _softplus(s[h])
                a[h] = s[h] - sp
                r[h] = jnp.where(strict, -sp, 0.0) if masked else -sp
                lb[h] = _dot2(r[h], ustrict)
            for h in range(nh):
                lw = a[h] + lb[h] + rs[h]
                w = jnp.exp(jnp.where(strict, lw, -1e30) if masked else lw)
                rs[h] = rs[h] + jnp.sum(r[h], axis=1, keepdims=True)
                acc[h] = acc[h] + _dot(w.astype(BF16), v2[h // 2])

        tile(i, True)

        def live():
            return jnp.max(functools.reduce(jnp.maximum, [rs[h] for h in range(nh)]))

        def more(c):
            return jnp.logical_and(c[0] < i, c[1] > SB_DEAD)

        def step(c):
            tile(i - 1 - c[0], False)
            return c[0] + 1, live()

        n_off, _ = lax.while_loop(more, step, (jnp.int32(0), live()))
        cnt_ref[pl.program_id(0), i] = n_off.astype(F32)
        for p in range(SB_PAIRS):
            rt_ref[:, lanes(p)] = jnp.where(lane < HEAD_DIM, rs[2 * p], rs[2 * p + 1])
            ob_ref[:, lanes(p)] = jnp.where(lane < HEAD_DIM, acc[2 * p], acc[2 * p + 1]).astype(BF16)
        pl.when(at_last)(g_finish)

    blk = lambda rows, imap: pl.BlockSpec((rows, SB_LANES), imap)
    return pl.pallas_call(
        body, name=name, grid=(ngrp, nq),
        in_specs=[blk(tq, lambda j, i: (i, j)), blk(t, lambda j, i: (0, j)), blk(t, lambda j, i: (0, j)),
                  _fix_spec((2 * tk, tk)), HB, pl.BlockSpec(memory_space=pl.ANY)],
        out_specs=[blk(tq, lambda j, i: (i, j)), blk(tq, lambda j, i: (i, ngrp + j)),
                   pl.BlockSpec(memory_space=pltpu.SMEM), HB],
        out_shape=[jax.ShapeDtypeStruct((t, D_MODEL), F32), jax.ShapeDtypeStruct(ycat.shape, ycat.dtype),
                   jax.ShapeDtypeStruct((ngrp, nq), F32),
                   jax.ShapeDtypeStruct((N_SHARDS,) + pack.shape, pack.dtype)],
        scratch_shapes=[pltpu.VMEM((nh, tq, 128), F32), pltpu.VMEM((nh, tq, 1), F32)] + GATHER_SEMS,
        input_output_aliases={5: 1},
        compiler_params=_cp(("arbitrary", "arbitrary")),
    )(qs, kn, vb, _doubled(cst["ustrict"], tk), pack, ycat)


def _sb_bwd(qs, kn, vb, rtot, cnt, dycat, csum_b, cst, name):
    t = qs.shape[0]
    tq = tk = min(t, SB_TILE)
    nq = t // tq
    ngrp = D_MODEL // SB_LANES
    nh = 2 * SB_PAIRS
    lanes = lambda p: slice(p * 128, (p + 1) * 128)

    def body(q_ref, k_ref, v_ref, rt_ref, do_ref, us_ref, ui_ref, cnt_ref, xs_ref, dq_ref, dk_ref, dv_ref, xr_ref,
             acc, rs, es, xss, xrs):
        at_first, _, at_last = _grid_marks(ngrp, nq)
        x_start, x_finish = _exchange_stages(xs_ref, xr_ref, xss, xrs)
        pl.when(at_first)(x_start)
        i = pl.program_id(1)
        lane = lax.broadcasted_iota(jnp.int32, (1, 128), 1)
        keep = [lane < HEAD_DIM, lane >= HEAD_DIM]
        qh, doh, rtot_h = [], [], []
        for p in range(SB_PAIRS):
            q2, rt = q_ref[:, lanes(p)], rt_ref[:, lanes(p)]
            do2b = do_ref[:, lanes(p)].astype(BF16)
            qh += [jnp.where(kp, q2, jnp.zeros_like(q2)) for kp in keep]
            doh += [jnp.where(kp, do2b, jnp.zeros_like(do2b)) for kp in keep]
            rtot_h += [jnp.sum(jnp.where(lane == n * HEAD_DIM, rt, 0.0), axis=1, keepdims=True) for n in range(2)]
        acc[...] = jnp.zeros_like(acc)
        rs[...] = jnp.zeros_like(rs)
        es[...] = jnp.zeros_like(es)

        @pl.when(i == 0)
        def _():
            dk_ref[...] = jnp.zeros_like(dk_ref)
            dv_ref[...] = jnp.zeros_like(dv_ref)

        ule, ult = us_ref[...], ui_ref[...]

        def tile(kb, masked):
            off = pl.multiple_of(kb * tk, tk)
            k2 = [k_ref[pl.ds(off, tk), lanes(p)] for p in range(SB_PAIRS)]
            v2 = [v_ref[pl.ds(off, tk), lanes(p)] for p in range(SB_PAIRS)]
            strict = _sb_masks(i, kb, tq, tk) if masked else None
            s = [_dot(qh[h], k2[h // 2], "nt") for h in range(nh)]
            dw = [_dot(doh[h], v2[h // 2], "nt") for h in range(nh)]
            a, sg, r, pin, w, e, cin = ([None] * nh for _ in range(7))
            for h in range(nh):
                sp = _softplus(s[h])
                a[h] = s[h] - sp
                sg[h] = jnp.exp(a[h])
                r[h] = jnp.where(strict, -sp, 0.0) if masked else -sp
                pin[h] = _dot2(r[h], ule)
            for h in range(nh):
                lw = a[h] + ((rtot_h[h] - rs[h]) - pin[h])
                w[h] = jnp.exp(jnp.where(strict, lw, -1e30) if masked else lw)
                e[h] = w[h] * dw[h]
                cin[h] = _dot2(e[h], ult)
            for p in range(SB_PAIRS):
                dk_t = jnp.zeros((tk, 128), F32)
                dv_t = jnp.zeros((tk, 128), F32)
                for h in (2 * p, 2 * p + 1):
                    dl = e[h] * (1.0 - sg[h]) - (es[h] + cin[h]) * sg[h]
                    dl = (jnp.where(strict, dl, 0.0) if masked else dl).astype(BF16)
                    rs[h] = rs[h] + jnp.sum(r[h], axis=1, keepdims=True)
                    es[h] = es[h] + jnp.sum(e[h], axis=1, keepdims=True)
                    acc[h] = acc[h] + _dot(dl, k2[p])
                    dk_t = dk_t + _dot(dl, qh[h], "tn")
                    dv_t = dv_t + _dot(w[h].astype(BF16), doh[h], "tn")
                dk_ref[pl.ds(off, tk), lanes(p)] += dk_t
                dv_ref[pl.ds(off, tk), lanes(p)] += dv_t

        def step(kb, carry):
            tile(kb, False)
            return carry

        n_off = cnt_ref[pl.program_id(0), i].astype(jnp.int32)
        lax.fori_loop(i - n_off, i, step, 0)
        tile(i, True)
        for p in range(SB_PAIRS):
            dq_ref[:, lanes(p)] = jnp.where(lane < HEAD_DIM, acc[2 * p], acc[2 * p + 1])
        pl.when(at_last)(x_finish)

    blk = lambda rows, imap: pl.BlockSpec((rows, SB_LANES), imap)
    return pl.pallas_call(
        body, name=name, grid=(ngrp, nq),
        in_specs=[blk(tq, lambda j, i: (i, j)), blk(t, lambda j, i: (0, j)), blk(t, lambda j, i: (0, j)),
                  blk(tq, lambda j, i: (i, j)), blk(tq, lambda j, i: (i, ngrp + j)),
                  _fix_spec((2 * tk, tk)), _fix_spec((2 * tk, tk)), pl.BlockSpec(memory_space=pltpu.SMEM), HB],
        out_specs=[blk(tq, lambda j, i: (i, j)), blk(t, lambda j, i: (0, j)), blk(t, lambda j, i: (0, j)), HB],
        out_shape=[jax.ShapeDtypeStruct((t, D_MODEL), F32)] * 3
        + [jax.ShapeDtypeStruct((3,) + csum_b.shape[1:], csum_b.dtype)],
        scratch_shapes=[pltpu.VMEM((nh, tq, 128), F32), pltpu.VMEM((nh, tq, 1), F32), pltpu.VMEM((nh, tq, 1), F32)]
        + EXCHANGE_SEMS,
        compiler_params=_cp(("arbitrary", "arbitrary")),
    )(qs, kn, vb, rtot, dycat, _doubled(cst["ule"], tk), _doubled(cst["ult"], tk), cnt, csum_b)


def _adamw(w, g, m, v, name):
    lead = (1,) * (w.ndim - 2)
    rows, cols = w.shape[-2:]
    fits = [d for d in range(8, rows, 8) if rows % d == 0 and d * cols * 4 <= ADAM_BLOCK_BYTES]
    tr = max(fits) if fits else rows
    c1 = 1.0 - ADAM_B1 ** ADAM_STEP
    c2 = 1.0 - ADAM_B2 ** ADAM_STEP

    def body(w_ref, g_ref, m_ref, v_ref, d_ref, nm_ref, nv_ref):
        gv = g_ref[...]
        nm = ADAM_B1 * m_ref[...] + (1.0 - ADAM_B1) * gv
        nv = ADAM_B2 * v_ref[...] + (1.0 - ADAM_B2) * (gv * gv)
        nm_ref[...] = nm
        nv_ref[...] = nv
        d_ref[...] = -ADAM_LR * ((nm / c1) / (jnp.sqrt(nv / c2) + ADAM_EPS) + ADAM_WD * w_ref[...])

    spec = pl.BlockSpec(lead + (tr, cols), lambda i: (0,) * len(lead) + (i, 0))
    return pl.pallas_call(
        body, name=name, grid=(rows // tr,), in_specs=[spec] * 4, out_specs=[spec] * 3,
        out_shape=[jax.ShapeDtypeStruct(w.shape, F32)] * 3, compiler_params=_cp(("parallel",)),
    )(w, g, m, v)


def _place():
    x, y, c = lax.axis_index("x"), lax.axis_index("y"), lax.axis_index("c")
    chips = [(1 - x, y), (x, 1 - y), (1 - x, 1 - y)]
    return x, y, c, chips


VM = pl.BlockSpec(memory_space=pltpu.VMEM)
HB = pl.BlockSpec(memory_space=pltpu.HBM)


def _gather_all(p_ref, gat_ref, ss, rs):
    x, y, c, _ = _place()
    me = 4 * x + 2 * y + c
    peers = [(x, y, 1 - c), (1 - x, y, c), (x, 1 - y, c), (1 - x, 1 - y, c),
             (1 - x, y, 1 - c), (x, 1 - y, 1 - c), (1 - x, 1 - y, 1 - c)]

    def copy(k, slot, to):
        return pltpu.make_async_remote_copy(src_ref=p_ref, dst_ref=gat_ref.at[slot], send_sem=ss.at[k],
                                            recv_sem=rs.at[k], device_id=to, device_id_type=MESH)

    sends = [copy(k, me, p) for k, p in enumerate(peers)]
    for s in sends:
        s.start()
    gat_ref[me] = p_ref[...]
    for k, p in enumerate(peers):
        copy(k, 4 * p[0] + 2 * p[1] + p[2], p).wait_recv()
    for s in sends:
        s.wait_send()


ALL_SEMS = [pltpu.SemaphoreType.DMA((7,)), pltpu.SemaphoreType.DMA((7,))]


def _small_reduce(pack, name):
    rows = pack.shape[0]

    def body(p_ref, gat_ref, sum_ref, ss, rs):
        _gather_all(p_ref, gat_ref, ss, rs)
        tot = gat_ref[0]
        for b in range(1, 8):
            tot = tot + gat_ref[b]
        sum_ref[...] = tot

    return pl.pallas_call(
        body, name=name, in_specs=[VM], out_specs=[VM, VM],
        out_shape=[jax.ShapeDtypeStruct((8, rows, D_MODEL), F32), jax.ShapeDtypeStruct((rows, D_MODEL), F32)],
        scratch_shapes=ALL_SEMS, compiler_params=_cp(),
    )(pack)


def _prologue(cpack, w_ada, b_shard, wpack_in, name):
    def body(cp_ref, w_ref, b_ref, p_ref, gat_ref, modp_ref, gin_ref, ss1, rs1, ss2, rs2, gss, grs):
        g_start, g_relay, g_finish = _gather_stages(p_ref, gin_ref, gss, grs)
        g_start()
        _gather_all(cp_ref, gat_ref, ss1, rs1)
        x, y, c, chips = _place()
        sh = 2 * x + y
        row = lax.broadcasted_iota(jnp.int32, (8, D_MODEL), 0)
        cv = jnp.zeros((8, D_MODEL), F32)
        for b in range(8):
            cv = jnp.where(row == b, gat_ref[b, 0:8, :], cv)
        cv = cv * _sig(cv)
        modp_ref[sh] = jnp.dot(cv, w_ref[...], precision=lax.Precision.HIGHEST,
                               preferred_element_type=F32) + b_ref[...]

        def copy(k, slot, to):
            return pltpu.make_async_remote_copy(src_ref=modp_ref.at[slot], dst_ref=modp_ref.at[slot],
                                                send_sem=ss2.at[k], recv_sem=rs2.at[k], device_id=to,
                                                device_id_type=MESH)

        sends = [copy(k, sh, (*ch, c)) for k, ch in enumerate(chips)]
        for s in sends:
            s.start()
        for k, ch in enumerate(chips):
            copy(k, 2 * ch[0] + ch[1], (*ch, c)).wait_recv()
        for s in sends:
            s.wait_send()
        g_relay()
        g_finish()

    return pl.pallas_call(
        body, name=name, in_specs=[VM, VM, VM, HB], out_specs=[VM, VM, HB],
        out_shape=[jax.ShapeDtypeStruct((8,) + cpack.shape, F32),
                   jax.ShapeDtypeStruct((N_SHARDS, 8, 6 * D_MODEL // N_SHARDS), F32),
                   jax.ShapeDtypeStruct((N_SHARDS,) + wpack_in.shape, wpack_in.dtype)],
        scratch_shapes=ALL_SEMS + EXCHANGE_SEMS + GATHER_SEMS, compiler_params=_cp(),
    )(cpack, w_ada, b_shard, wpack_in)


def _gather_stages(p_ref, out_ref, ss, rs):
    hf = p_ref.shape[0] // 2
    x, y, c, chips = _place()
    sh = 2 * x + y
    sib = (x, y, 1 - c)
    slots = [2 * ch[0] + ch[1] for ch in chips]

    def half(slot, hc):
        return out_ref.at[slot, pl.ds(hc * hf, hf), :]

    def copy(k, src, slot, hc, to):
        return pltpu.make_async_remote_copy(src_ref=src, dst_ref=half(slot, hc), send_sem=ss.at[k],
                                            recv_sem=rs.at[k], device_id=to, device_id_type=MESH)

    def first():
        return [copy(j, p_ref.at[pl.ds(c * hf, hf), :], sh, c, (*ch, c)) for j, ch in enumerate(chips)]

    def passed():
        return [copy(3 + j, half(slots[j], c), slots[j], c, sib) for j in range(3)]

    def start():
        for cp in first():
            cp.start()

    def relay():
        for j, cp in enumerate(passed()):
            copy(j, half(slots[j], c), slots[j], c, (*chips[j], c)).wait_recv()
            cp.start()

    def finish():
        for j in range(3):
            copy(3 + j, half(slots[j], 1 - c), slots[j], 1 - c, sib).wait_recv()
        for cp in first() + passed():
            cp.wait_send()

    return start, relay, finish


GATHER_SEMS = [pltpu.SemaphoreType.DMA((6,)), pltpu.SemaphoreType.DMA((6,))]


def _swap_stages(g_ref, out_ref, ss, rs):
    hf = g_ref.shape[1] // 2
    x, y, c, _ = _place()

    def copy():
        return pltpu.make_async_remote_copy(
            src_ref=g_ref.at[pl.ds(0, N_SHARDS), pl.ds((1 - c) * hf, hf), :], dst_ref=out_ref,
            send_sem=ss, recv_sem=rs, device_id=(x, y, 1 - c), device_id_type=MESH)

    return (lambda: copy().start()), (lambda: copy().wait())


SWAP_SEMS = [pltpu.SemaphoreType.DMA, pltpu.SemaphoreType.DMA]


def _swap_shape(g):
    return jax.ShapeDtypeStruct((N_SHARDS, g.shape[1] // 2, D_MODEL), g.dtype)


def _sibling_swap(g, name):
    def body(g_ref, out_ref, ss, rs):
        for stage in _swap_stages(g_ref, out_ref, ss, rs):
            stage()

    return pl.pallas_call(
        body, name=name, in_specs=[HB], out_specs=HB, out_shape=_swap_shape(g),
        scratch_shapes=SWAP_SEMS, compiler_params=_cp(),
    )(g)


def _row_tile(rows, width_bytes, cap_bytes):
    fits = [d for d in range(8, rows + 1, 8) if rows % d == 0 and d * width_bytes <= cap_bytes]
    return max(fits)


def _chip_sum(g, got, c_idx, name):
    hf = got.shape[1]
    tr = _row_tile(hf, D_MODEL * 4, 3 << 20)
    nb = hf // tr

    def body(c_ref, a_ref, b_ref, s_ref, sb_ref):
        s = a_ref[...] + b_ref[...]
        s_ref[...] = s
        sb_ref[...] = s.astype(BF16)

    blk = pl.BlockSpec((1, tr, D_MODEL), lambda s, i, c_ref: (s, i, 0))
    return pl.pallas_call(
        body, name=name,
        grid_spec=pltpu.PrefetchScalarGridSpec(
            num_scalar_prefetch=1, grid=(N_SHARDS, nb),
            in_specs=[pl.BlockSpec((1, tr, D_MODEL), lambda s, i, c_ref: (s, c_ref[0] * nb + i, 0)), blk],
            out_specs=[blk, blk]),
        out_shape=[jax.ShapeDtypeStruct((N_SHARDS, hf, D_MODEL), F32),
                   jax.ShapeDtypeStruct((N_SHARDS, hf, D_MODEL), BF16)],
        compiler_params=_cp(("parallel", "parallel")),
    )(c_idx, g, got)


def _exchange_stages(s_ref, out_ref, ss, rs):
    x, y, c, chips = _place()

    def sends():
        return [pltpu.make_async_remote_copy(src_ref=s_ref.at[2 * ch[0] + ch[1]], dst_ref=out_ref.at[k],
                                             send_sem=ss.at[k], recv_sem=rs.at[k], device_id=(*ch, c),
                                             device_id_type=MESH) for k, ch in enumerate(chips)]

    def start():
        for cp in sends():
            cp.start()

    def finish():
        for cp in sends():
            cp.wait()

    return start, finish


EXCHANGE_SEMS = [pltpu.SemaphoreType.DMA((3,)), pltpu.SemaphoreType.DMA((3,))]


def _total_half(s, got, sh_idx, name):
    hf = got.shape[1]
    tr = _row_tile(hf, D_MODEL * 4, 3 << 20)
    nb = hf // tr

    def body(sh_ref, a_ref, r0, r1, r2, o_ref):
        o_ref[...] = ((a_ref[0] + r0[0].astype(F32)) + r1[0].astype(F32)) + r2[0].astype(F32)

    rspec = lambda k: pl.BlockSpec((1, tr, D_MODEL), lambda i, sh_ref: (k, i, 0))
    return pl.pallas_call(
        body, name=name,
        grid_spec=pltpu.PrefetchScalarGridSpec(
            num_scalar_prefetch=1, grid=(nb,),
            in_specs=[pl.BlockSpec((1, tr, D_MODEL), lambda i, sh_ref: (sh_ref[0], i, 0)),
                      rspec(0), rspec(1), rspec(2)],
            out_specs=pl.BlockSpec((tr, D_MODEL), lambda i, sh_ref: (i, 0))),
        out_shape=jax.ShapeDtypeStruct((hf, D_MODEL), F32),
        compiler_params=_cp(("parallel",)),
    )(sh_idx, s, got, got, got)


def _join_halves(tot, name):
    def body(t_ref, out_ref, ss, rs):
        x, y, c, _ = _place()
        cp = pltpu.make_async_remote_copy(src_ref=t_ref, dst_ref=out_ref, send_sem=ss, recv_sem=rs,
                                          device_id=(x, y, 1 - c), device_id_type=MESH)
        cp.start()
        cp.wait()

    return pl.pallas_call(
        body, name=name, in_specs=[HB], out_specs=HB,
        out_shape=jax.ShapeDtypeStruct(tot.shape, F32),
        scratch_shapes=[pltpu.SemaphoreType.DMA, pltpu.SemaphoreType.DMA],
        compiler_params=_cp(),
    )(tot)


def _w_ada_grad(cond, dmod_cols, name):
    def body(c_ref, d_ref, o_ref):
        cv = c_ref[...]
        cv = cv * _sig(cv)
        o_ref[...] = lax.dot_general(cv, d_ref[...], _DN["tn"], precision=lax.Precision.HIGHEST,
                                     preferred_element_type=F32)

    return pl.pallas_call(
        body, name=name, in_specs=[VM, VM], out_specs=VM,
        out_shape=jax.ShapeDtypeStruct((D_MODEL, dmod_cols.shape[1]), F32), compiler_params=_cp(),
    )(cond, dmod_cols)


def _pad_rows(a, rows):
    return jnp.pad(a, ((0, rows - a.shape[0]), (0, 0)))


def _pad_cols(a, cols):
    return jnp.pad(a, ((0, 0), (0, cols - a.shape[1])))


def _unpack_rest(p):
    o = 0
    out = []
    for r in (R_OUT, R_FF, R_FF, R_FF):
        out.append(p[..., o:o + r, :])
        o += r
    return out


def _reduce_tail(csum, got2, shard, ac, tag):
    tot = _total_half(csum, got2, shard.reshape(1).astype(jnp.int32), "rs_total_" + tag)
    other = _join_halves(tot, "rs_join_" + tag)
    return jnp.where(ac == 0, jnp.concatenate([tot, other], axis=0), jnp.concatenate([other, tot], axis=0))


def _reduce_head(gpack, ac, tag):
    got = _sibling_swap(gpack, "rs_sibling_swap_" + tag)
    return _chip_sum(gpack, got, ac.reshape(1).astype(jnp.int32), "rs_chip_sum_" + tag)


def kernel(x, c, w_ada, b_ada, norm1_w, w_in, conv_w, conv_b, dt_bias, a_log, d_skip, ssd_norm_w, q_norm_w, k_norm_w, w_out, norm2_w, w_gate, w_up, w_down, loss_target, m_w_ada, m_b_ada, m_norm1_w, m_w_in, m_conv_w, m_conv_b, m_dt_bias, m_a_log, m_d_skip, m_ssd_norm_w, m_q_norm_w, m_k_norm_w, m_w_out, m_norm2_w, m_w_gate, m_w_up, m_w_down, v_w_ada, v_b_ada, v_norm1_w, v_w_in, v_conv_w, v_conv_b, v_dt_bias, v_a_log, v_d_skip, v_ssd_norm_w, v_q_norm_w, v_k_norm_w, v_w_out, v_norm2_w, v_w_gate, v_w_up, v_w_down):
    cst = _consts()
    ax, ay, ac = lax.axis_index("x"), lax.axis_index("y"), lax.axis_index("c")
    shard = 2 * ax + ay
    me = 4 * ax + 2 * ay + ac
    xs = x[0]
    tgt = loss_target[0]
    w_in_cols = w_in.shape[2]
    conv_cols = conv_w.shape[2]

    tr3 = lambda a: jnp.transpose(a, (0, 2, 1))
    lin = lambda a: tr3(a).reshape(-1, 128)
    unlin = lambda a: tr3(a.reshape(1, w_in_cols, D_MODEL))
    wpack_in = _pad_rows(tr3(w_in.astype(BF16))[0], R_IN)
    wpack_rest = jnp.concatenate([w_out[0], tr3(w_gate)[0], tr3(w_up)[0], w_down[0]], axis=0).astype(BF16)
    own = lax.broadcasted_iota(jnp.int32, (N_SHARDS, 1, 1), 0) == shard

    cw_flat = _pad_cols(conv_w[0].reshape(1, -1), 2 * D_MODEL).reshape(2, D_MODEL)
    cpack = jnp.concatenate([jnp.broadcast_to(c, (8, D_MODEL)), _pad_rows(cw_flat, 8)], axis=0)
    mod_w = 6 * D_MODEL // N_SHARDS
    b_shard = lax.dynamic_slice(b_ada, (0, shard * mod_w), (1, mod_w))
    gat, modp, gp_in = _prologue(cpack, w_ada[0], b_shard, wpack_in, "prologue")
    c_all = gat[:, 0, :]
    cw = gat[0::2, 8:10, :].reshape(N_SHARDS, 2 * D_MODEL)[:, :4 * conv_cols].reshape(N_SHARDS, 4, conv_cols)
    conv_w_full = jnp.transpose(cw, (1, 0, 2)).reshape(4, D_CONV)
    mod_mine = lax.dynamic_slice(modp, (0, me, 0), (N_SHARDS, 1, mod_w)).reshape(6, D_MODEL)
    mod = _pad_rows(mod_mine, 8)
    p_in = jnp.where(own, wpack_in[None], gp_in)
    wi_t = p_in[:, :w_in_cols, :].reshape(D_IN_PROJ, D_MODEL)
    w_inp_t = jnp.concatenate([wi_t[0:1024], wi_t[2576:5648], wi_t[1024:2560], wi_t[2560:2576],
                               jnp.zeros((112, D_MODEL), BF16)], axis=0)

    pad128 = lambda a: _pad_cols(a, 128)
    dtb, alog = pad128(dt_bias), pad128(a_log)
    dsk = jnp.repeat(d_skip, HEAD_DIM, axis=1)
    qw, kw = jnp.tile(q_norm_w, (1, N_HEADS)), jnp.tile(k_norm_w, (1, N_HEADS))

    h1 = _norm_mod(xs, norm1_w, mod, 0, "norm1")
    proj = _matmul(h1, w_inp_t, "nt", F32, "in_proj")
    u = _conv_fwd(proj, conv_w_full, conv_b, "conv_fwd")
    y_ssd, yn, prev = _ssd_fwd(u, proj, dtb, alog, dsk, ssd_norm_w, cst, "ssd_fwd")
    qs, kn, vb = _qk_fwd(proj, qw, kw, cst, "qk_norm")
    rtot, ycat, cnt, gp_rest = _sb_fwd(qs, kn, vb, wpack_rest, yn, cst, "sb_fwd")
    p_out, p_gate, p_up, p_down = _unpack_rest(jnp.where(own, wpack_rest[None], gp_rest))
    w_o = p_out.reshape(2 * D_MODEL, D_MODEL)
    w_gu_t = jnp.concatenate([p_gate.reshape(D_FF, D_MODEL), p_up.reshape(D_FF, D_MODEL)], axis=0)
    w_d = p_down.reshape(D_FF, D_MODEL)
    mix = _matmul(ycat, w_o, "nn", F32, "out_proj")
    x1, h2 = _resid_norm(xs, mix, norm2_w, mod, "resid_norm2")
    gu = _matmul(h2, w_gu_t, "nt", BF16, "ffn_in")
    act = _act_fwd(gu, "ffn_act")
    ffn = _matmul(act, w_d, "nn", F32, "ffn_out", tk_cap=1408)
    dffn, dout, dg2, loss8 = _loss_head(x1, ffn, tgt, mod, "loss_head")
    loss = lax.psum(loss8[0, 0], ("x", "y", "c"))

    dact = _matmul(dffn, w_d, "nt", BF16, "d_act")
    g_down = _matmul(act, dffn, "tn", F32, "g_w_down", tm_cap=1408)
    dgu = _act_bwd(dact, gu, "ffn_act_bwd")
    dh2 = _matmul(dgu, w_gu_t, "nn", F32, "d_h2", tk_cap=1408)
    g_gu_t = _matmul(dgu, h2, "tn", F32, "g_w_gu", tm_cap=1408)
    dx1, dmix, acc2 = _norm_bwd(dh2, x1, dout, mix, norm2_w, mod, 3, 2, "norm2_bwd")
    dycat = _matmul(dmix, w_o, "nt", F32, "d_ycat")
    g_out = _matmul(ycat, dmix, "tn", F32, "g_w_out")
    gpack_rest = jnp.concatenate([
        g_out.reshape(N_SHARDS, R_OUT, D_MODEL),
        g_gu_t[:D_FF].reshape(N_SHARDS, R_FF, D_MODEL), g_gu_t[D_FF:].reshape(N_SHARDS, R_FF, D_MODEL),
        g_down.reshape(N_SHARDS, R_FF, D_MODEL)], axis=1)
    du, ddt, dz, acc_ssd, acc16, got_s = _ssd_bwd(u, proj, y_ssd, prev, dycat, dtb, alog, dsk, ssd_norm_w,
                                                  gpack_rest, cst, "ssd_bwd")
    csum_r, csum_rb = _chip_sum(gpack_rest, got_s, ac.reshape(1).astype(jnp.int32), "rs_chip_sum_rest")
    dqs, dkn, dv, got_r = _sb_bwd(qs, kn, vb, rtot, cnt, dycat, csum_rb, cst, "sb_bwd")
    r_out, r_gate, r_up, r_down = _unpack_rest(_reduce_tail(csum_r, got_r, shard, ac, "rest"))
    dq, dk, dvb, acc_qk = _qk_bwd(proj, dqs, dkn, dv, qw, kw, cst, "qk_norm_bwd")
    dxbc, g_conv_w, g_conv_b = _conv_bwd(proj, du, conv_w_full, conv_b, "conv_bwd")
    dproj = jnp.concatenate([dz, dq, dk, dvb, dxbc, ddt], axis=1)
    g_inp_t = _matmul(dproj, h1, "tn", F32, "g_w_in", tm_cap=1920)
    gi_t = jnp.concatenate([g_inp_t[0:1024], g_inp_t[4096:5632], g_inp_t[5632:5648], g_inp_t[1024:4096]], axis=0)
    gpack_in = jnp.pad(gi_t.reshape(N_SHARDS, w_in_cols, D_MODEL), ((0, 0), (0, R_IN - w_in_cols), (0, 0)))
    csum_i, csum_ib = _reduce_head(gpack_in, ac, "in")
    dh1, got_i = _matmul(dproj, w_inp_t, "nn", F32, "d_h1", tk_cap=1152, exchange=csum_ib)
    r_in = _reduce_tail(csum_i, got_i, shard, ac, "in")
    grad_x, acc1 = _norm_bwd(dh1, xs, dx1, None, norm1_w, mod, 0, None, "norm1_bwd")

    last = jnp.concatenate([acc_qk[0:1, 0:64], acc_qk[1:2, 0:64], acc16[0:1, 0:16], acc16[1:2, 0:16],
                            acc16[2:3, 0:16]], axis=1)
    spack = jnp.concatenate([
        acc1[0:2], acc2[3:4], acc2[0:2], dg2,
        acc1[2:3], acc2[2:3], acc_ssd[0:1],
        _pad_cols(g_conv_b, 2 * D_MODEL).reshape(2, D_MODEL),
        g_conv_w.reshape(6, D_MODEL),
        _pad_cols(last, D_MODEL)], axis=0)
    sgat, ssum = _small_reduce(_pad_rows(spack, SMALL_ROWS), "gather_small")
    g_b_ada = ssum[0:6].reshape(1, 6 * D_MODEL)
    g_norm1, g_norm2, g_ssdn = ssum[6:7], ssum[7:8], ssum[8:9]
    g_cb = ssum[9:11].reshape(1, 2 * D_MODEL)[:, :D_CONV]
    g_cw = lax.dynamic_slice(ssum[11:17].reshape(4, D_CONV), (0, shard * conv_cols), (4, conv_cols))
    g_qn, g_kn = ssum[17:18, 0:64], ssum[17:18, 64:128]
    g_dtb, g_alog, g_dsk = ssum[17:18, 128:144], ssum[17:18, 144:160], ssum[17:18, 160:176]
    dmod_all = sgat[:, 0:6, :].reshape(8, 6 * D_MODEL)
    g_w_ada = _w_ada_grad(c_all, lax.dynamic_slice(dmod_all, (0, shard * mod_w), (8, mod_w)), "g_w_ada")


    grads = dict(w_ada=g_w_ada, b_ada=g_b_ada, norm1_w=g_norm1, w_in=r_in[:w_in_cols].reshape(-1, 128), conv_w=g_cw,
                 conv_b=g_cb, dt_bias=g_dtb, a_log=g_alog, d_skip=g_dsk, ssd_norm_w=g_ssdn, q_norm_w=g_qn,
                 k_norm_w=g_kn, w_out=r_out, norm2_w=g_norm2, w_gate=r_gate, w_up=r_up, w_down=r_down)
    weights = dict(w_ada=(w_ada, m_w_ada, v_w_ada), b_ada=(b_ada, m_b_ada, v_b_ada),
                   norm1_w=(norm1_w, m_norm1_w, v_norm1_w), w_in=(w_in, m_w_in, v_w_in),
                   conv_w=(conv_w, m_conv_w, v_conv_w), conv_b=(conv_b, m_conv_b, v_conv_b),
                   dt_bias=(dt_bias, m_dt_bias, v_dt_bias), a_log=(a_log, m_a_log, v_a_log),
                   d_skip=(d_skip, m_d_skip, v_d_skip), ssd_norm_w=(ssd_norm_w, m_ssd_norm_w, v_ssd_norm_w),
                   q_norm_w=(q_norm_w, m_q_norm_w, v_q_norm_w), k_norm_w=(k_norm_w, m_k_norm_w, v_k_norm_w),
                   w_out=(w_out, m_w_out, v_w_out), norm2_w=(norm2_w, m_norm2_w, v_norm2_w),
                   w_gate=(w_gate, m_w_gate, v_w_gate), w_up=(w_up, m_w_up, v_w_up),
                   w_down=(w_down, m_w_down, v_w_down))
    views = dict(w_in=(lin, unlin), w_gate=(tr3, tr3), w_up=(tr3, tr3))
    same = lambda a: a
    names = list(weights)
    g_out_l, d_out_l, m_out_l, v_out_l = [], [], [], []
    for n in names:
        view, back = views.get(n, (same, same))
        w, m, v = (view(a) for a in weights[n])
        g = grads[n].reshape(w.shape)
        d, nm, nv = _adamw(w, g, m, v, "adamw_" + n)
        g_out_l.append(back(g))
        d_out_l.append(back(d))
        m_out_l.append(back(nm))
        v_out_l.append(back(nv))
    return (loss, grad_x[None], *g_out_l, *d_out_l, *m_out_l, *v_out_l)
```

```python
import functools

import numpy as np
import jax
import jax.numpy as jnp
from jax import lax
from jax.experimental import pallas as pl
from jax.experimental.pallas import tpu as pltpu

F32, BF16 = jnp.float32, jnp.bfloat16
MESH = pl.DeviceIdType.MESH

D_MODEL = 1024
HEAD_DIM = 64
N_HEADS = 16
D_CONV = 1536
D_FF = 2816
D_IN_PROJ = 5648
D_PROJ_PAD = 5760
CHUNK = 128
SB_TILE = 256
FWD_PAIRS, BWD_PAIRS = 2, 2
SB_DEAD = -105.0
EPS = 1e-6
N_SHARDS = 4
R_IN, R_OUT, R_FF = 1440, 512, 704
SMALL_ROWS = 24

ADAM_LR, ADAM_B1, ADAM_B2, ADAM_EPS, ADAM_WD, ADAM_STEP = 0.001, 0.9, 0.999, 1e-08, 0.01, 10

VMEM_LIMIT = 48 * 1024 * 1024
ADAM_BLOCK_BYTES = 3 * 512 * 1024

_DN = {"nn": (((1,), (0,)), ((), ())), "nt": (((1,), (1,)), ((), ())), "tn": (((0,), (0,)), ((), ()))}


def _dot(a, b, dims="nn"):
    return lax.dot_general(a, b, _DN[dims], preferred_element_type=F32)


def _pieces(x, n):
    out = []
    for _ in range(n - 1):
        hi = lax.bitcast_convert_type(lax.bitcast_convert_type(x, jnp.int32) & jnp.int32(-65536), F32)
        out.append(hi.astype(BF16))
        x = x - hi
    out.append(x.astype(BF16))
    return out


def _dotx_r(x, b_exact, n=3):
    return _dot(jnp.concatenate(_pieces(x, n), axis=1), jnp.concatenate([b_exact] * n, axis=0))


def _dotx_l(a_exact, x, n=3):
    return _dot(jnp.concatenate([a_exact] * n, axis=1), jnp.concatenate(_pieces(x, n), axis=0))


def _dot2(x, b2):
    return _dot(jnp.concatenate(_pieces(x, 2), axis=1), b2)


def _sig(x):
    return 1.0 / (1.0 + jnp.exp(-x))


def _softplus(x):
    return jnp.maximum(x, 0.0) + jnp.log(1.0 + jnp.exp(-jnp.abs(x)))


def _cp(sem=None, vmem=VMEM_LIMIT):
    return pltpu.CompilerParams(dimension_semantics=sem, vmem_limit_bytes=vmem)


def _colsum(x):
    return jnp.sum(x, axis=0, keepdims=True)


def _consts():
    ch = np.arange(D_MODEL)
    expand = (np.arange(128)[:, None] == (ch // HEAD_DIM)[None, :]).astype(np.float32)
    fold = (ch[:, None] % HEAD_DIM == np.arange(128)[None, :]).astype(np.float32)
    i = np.arange(CHUNK)
    tril = (i[:, None] >= i[None, :]).astype(np.float32)
    j = np.arange(SB_TILE)
    ustrict = (j[:, None] > j[None, :]).astype(np.float32)
    ule = (j[:, None] <= j[None, :]).astype(np.float32)
    ult = (j[:, None] < j[None, :]).astype(np.float32)
    c = lambda a: jnp.asarray(a, BF16)
    return dict(expand=c(expand), hsum=c(expand.T), fold=c(fold), tril=c(tril), triu=c(tril.T),
                ustrict=ustrict, ule=ule, ult=ult)


def _doubled(tri, tk):
    b = tri[:tk, :tk]
    return jnp.asarray(np.concatenate([b, b], axis=0), BF16)


def _pick(n, cap):
    best = 128
    for t in range(128, min(n, cap) + 1, 128):
        if n % t == 0:
            best = t
    return n if n <= cap else best


def _matmul(a, b, dims, out_dtype, name, tm_cap=1024, tn_cap=2048, tk_cap=1024, exchange=None):
    if dims == "nn":
        (m, k), (_, n) = a.shape, b.shape
    elif dims == "nt":
        (m, k), (n, _) = a.shape, b.shape
    else:
        (k, m), (_, n) = a.shape, b.shape
    tm, tn, tk = _pick(m, tm_cap), _pick(n, tn_cap), _pick(k, tk_cap)
    nk = k // tk
    a_spec = (pl.BlockSpec((tk, tm), lambda i, j, kk: (kk, i)) if dims == "tn"
              else pl.BlockSpec((tm, tk), lambda i, j, kk: (i, kk)))
    b_spec = (pl.BlockSpec((tn, tk), lambda i, j, kk: (j, kk)) if dims == "nt"
              else pl.BlockSpec((tk, tn), lambda i, j, kk: (kk, j)))

    grid = (m // tm, n // tn, nk)

    def body(a_ref, b_ref, *rest):
        if exchange is None:
            o_ref, acc_ref = rest
        else:
            xs_ref, o_ref, xr_ref, acc_ref, xss, xrs = rest
            ids = [pl.program_id(d) for d in range(3)]
            x_start, x_finish = _exchange_stages(xs_ref, xr_ref, xss, xrs)
            pl.when(functools.reduce(jnp.logical_and, [p == 0 for p in ids]))(x_start)
        kk = pl.program_id(2)
        part = _dot(a_ref[...], b_ref[...], dims)
        if nk == 1:
            o_ref[...] = part.astype(out_dtype)
        else:
            @pl.when(kk == 0)
            def _():
                acc_ref[...] = part

            @pl.when(kk > 0)
            def _():
                acc_ref[...] += part

            @pl.when(kk == nk - 1)
            def _():
                o_ref[...] = acc_ref[...].astype(out_dtype)
        if exchange is not None:
            pl.when(functools.reduce(jnp.logical_and, [p == g - 1 for p, g in zip(ids, grid)]))(x_finish)

    in_specs = [a_spec, b_spec]
    out_specs = [pl.BlockSpec((tm, tn), lambda i, j, kk: (i, j))]
    out_shape = [jax.ShapeDtypeStruct((m, n), out_dtype)]
    scratch = [pltpu.VMEM((tm, tn) if nk > 1 else (8, 128), F32)]
    args = [a, b]
    if exchange is not None:
        in_specs.append(HB)
        out_specs.append(HB)
        out_shape.append(jax.ShapeDtypeStruct((3,) + exchange.shape[1:], exchange.dtype))
        scratch += EXCHANGE_SEMS
        args.append(exchange)
    out = pl.pallas_call(
        body, name=name, grid=grid, in_specs=in_specs, out_specs=out_specs, out_shape=out_shape,
        scratch_shapes=scratch,
        compiler_params=_cp(("parallel", "parallel", "arbitrary") if exchange is None else ("arbitrary",) * 3),
    )(*args)
    return out[0] if exchange is None else out


def _row_spec(tm, width=D_MODEL, col=0):
    return pl.BlockSpec((tm, width), lambda i: (i, col))


def _fix_spec(shape):
    return pl.BlockSpec(shape, lambda *_: (0,) * len(shape))


def _norm_mod(x, nw, mod, row_sh, name):
    t = x.shape[0]
    tm = min(t, 512)

    def body(x_ref, nw_ref, mod_ref, h_ref):
        xv = x_ref[...]
        r = lax.rsqrt(jnp.mean(xv * xv, axis=-1, keepdims=True) + EPS)
        sh = mod_ref[row_sh:row_sh + 1, :]
        sc = mod_ref[row_sh + 1:row_sh + 2, :]
        h_ref[...] = (xv * r * nw_ref[...] * (1.0 + sc) + sh).astype(BF16)

    return pl.pallas_call(
        body, name=name, grid=(t // tm,),
        in_specs=[_row_spec(tm), _fix_spec((1, D_MODEL)), _fix_spec((8, D_MODEL))],
        out_specs=_row_spec(tm), out_shape=jax.ShapeDtypeStruct((t, D_MODEL), BF16),
        compiler_params=_cp(("parallel",)),
    )(x, nw, mod)


def _resid_norm(x, mix, nw, mod, name):
    t = x.shape[0]
    tm = min(t, 512)

    def body(x_ref, mix_ref, nw_ref, mod_ref, x1_ref, h_ref):
        x1 = x_ref[...] + mod_ref[2:3, :] * mix_ref[...]
        x1_ref[...] = x1
        r = lax.rsqrt(jnp.mean(x1 * x1, axis=-1, keepdims=True) + EPS)
        h_ref[...] = (x1 * r * nw_ref[...] * (1.0 + mod_ref[4:5, :]) + mod_ref[3:4, :]).astype(BF16)

    return pl.pallas_call(
        body, name=name, grid=(t // tm,),
        in_specs=[_row_spec(tm), _row_spec(tm), _fix_spec((1, D_MODEL)), _fix_spec((8, D_MODEL))],
        out_specs=[_row_spec(tm), _row_spec(tm)],
        out_shape=[jax.ShapeDtypeStruct((t, D_MODEL), F32), jax.ShapeDtypeStruct((t, D_MODEL), BF16)],
        compiler_params=_cp(("parallel",)),
    )(x, mix, nw, mod)


def _act_fwd(gu, name):
    t = gu.shape[0]
    tm, tn = min(t, 512), D_FF // 2
    nb = D_FF // tn

    def body(g_ref, u_ref, a_ref):
        g = g_ref[...].astype(F32)
        a_ref[...] = (g * _sig(g) * u_ref[...].astype(F32)).astype(BF16)

    return pl.pallas_call(
        body, name=name, grid=(t // tm, nb),
        in_specs=[pl.BlockSpec((tm, tn), lambda i, j: (i, j)), pl.BlockSpec((tm, tn), lambda i, j: (i, j + nb))],
        out_specs=pl.BlockSpec((tm, tn), lambda i, j: (i, j)),
        out_shape=jax.ShapeDtypeStruct((t, D_FF), BF16),
        compiler_params=_cp(("parallel", "parallel")),
    )(gu, gu)


def _act_bwd(dact, gu, name):
    t = gu.shape[0]
    tm = min(t, 256)

    def body(d_ref, g_ref, u_ref, o_ref):
        g, d = g_ref[...].astype(F32), d_ref[...].astype(F32)
        s = _sig(g)
        o_ref[:, 0:D_FF] = (d * u_ref[...].astype(F32) * s * (1.0 + g * (1.0 - s))).astype(BF16)
        o_ref[:, D_FF:2 * D_FF] = (d * g * s).astype(BF16)

    return pl.pallas_call(
        body, name=name, grid=(t // tm,),
        in_specs=[pl.BlockSpec((tm, D_FF), lambda i: (i, 0)), pl.BlockSpec((tm, D_FF), lambda i: (i, 0)),
                  pl.BlockSpec((tm, D_FF), lambda i: (i, 1))],
        out_specs=pl.BlockSpec((tm, 2 * D_FF), lambda i: (i, 0)),
        out_shape=jax.ShapeDtypeStruct((t, 2 * D_FF), BF16),
        compiler_params=_cp(("parallel",)),
    )(dact, gu, gu)


def _loss_head(x1, ffn, tgt, mod, name):
    t = x1.shape[0]
    tm = min(t, 512)

    def body(x1_ref, f_ref, t_ref, mod_ref, dffn_ref, dout_ref, dg2_ref, loss_ref):
        i = pl.program_id(0)
        g2 = mod_ref[5:6, :]
        f = f_ref[...]
        err = x1_ref[...] + g2 * f - t_ref[...]
        dout = err * (1.0 / D_MODEL)
        dout_ref[...] = dout
        dffn_ref[...] = (dout * g2).astype(BF16)
        part = jnp.zeros((8, 128), F32) + 0.5 * jnp.sum(jnp.mean(err * err, axis=-1, keepdims=True))

        @pl.when(i == 0)
        def _():
            dg2_ref[...] = _colsum(dout * f)
            loss_ref[...] = part

        @pl.when(i > 0)
        def _():
            dg2_ref[...] += _colsum(dout * f)
            loss_ref[...] += part

    return pl.pallas_call(
        body, name=name, grid=(t // tm,),
        in_specs=[_row_spec(tm), _row_spec(tm), _row_spec(tm), _fix_spec((8, D_MODEL))],
        out_specs=[_row_spec(tm), _row_spec(tm), _fix_spec((1, D_MODEL)), _fix_spec((8, 128))],
        out_shape=[jax.ShapeDtypeStruct((t, D_MODEL), BF16), jax.ShapeDtypeStruct((t, D_MODEL), F32),
                   jax.ShapeDtypeStruct((1, D_MODEL), F32), jax.ShapeDtypeStruct((8, 128), F32)],
        compiler_params=_cp(("arbitrary",)),
    )(x1, ffn, tgt, mod)


def _norm_bwd(dh, xin, dres, aux, nw, mod, row_sh, gate_row, name, join=None):
    t = xin.shape[0]
    tm = min(t, 512)
    nsteps = t // tm
    with_gate = gate_row is not None

    def body(*refs):
        if join is not None:
            *refs, jss, jrs = refs
        if with_gate:
            dh_ref, x_ref, dr_ref, aux_ref, nw_ref, mod_ref, *rest = refs
        else:
            dh_ref, x_ref, dr_ref, nw_ref, mod_ref, *rest = refs
        if join is not None:
            j_ref, *rest, jo_ref = rest
            j_start, j_finish = _join_stages(j_ref, jo_ref, jss, jrs)
            pl.when(pl.program_id(0) == 0)(j_start)
        if with_gate:
            dx_ref, dg_ref, acc_ref = rest
        else:
            dx_ref, acc_ref = rest
        i = pl.program_id(0)
        xv, dhv = x_ref[...], dh_ref[...]
        r = lax.rsqrt(jnp.mean(xv * xv, axis=-1, keepdims=True) + EPS)
        xn = xv * r
        nwv = nw_ref[...]
        sc1 = 1.0 + mod_ref[row_sh + 1:row_sh + 2, :]
        dxn = dhv * (nwv * sc1)
        dx = dr_ref[...] + r * (dxn - xn * jnp.mean(dxn * xn, axis=-1, keepdims=True))
        dx_ref[...] = dx
        dhx = dhv * xn
        rows = [_colsum(dhv), _colsum(dhx * nwv), _colsum(dhx * sc1)]
        if with_gate:
            dg_ref[...] = (dx * mod_ref[gate_row:gate_row + 1, :]).astype(BF16)
            rows.append(_colsum(dx * aux_ref[...]))

        @pl.when(i == 0)
        def _():
            acc_ref[...] = jnp.zeros_like(acc_ref)

        for k, v in enumerate(rows):
            acc_ref[k:k + 1, :] += v
        if join is not None:
            pl.when(i == nsteps - 1)(j_finish)

    ins = [dh, xin, dres] + ([aux] if with_gate else []) + [nw, mod]
    in_specs = [_row_spec(tm)] * (4 if with_gate else 3) + [_fix_spec((1, D_MODEL)), _fix_spec((8, D_MODEL))]
    out_specs = [_row_spec(tm)] + ([_row_spec(tm)] if with_gate else []) + [_fix_spec((8, D_MODEL))]
    out_shape = ([jax.ShapeDtypeStruct((t, D_MODEL), F32)]
                 + ([jax.ShapeDtypeStruct((t, D_MODEL), BF16)] if with_gate else [])
                 + [jax.ShapeDtypeStruct((8, D_MODEL), F32)])
    scratch = []
    if join is not None:
        ins.append(join)
        in_specs.append(HB)
        out_specs.append(HB)
        out_shape.append(jax.ShapeDtypeStruct(join.shape, join.dtype))
        scratch = SWAP_SEMS
    return pl.pallas_call(
        body, name=name, grid=(nsteps,), in_specs=in_specs, out_specs=out_specs, out_shape=out_shape,
        scratch_shapes=scratch, compiler_params=_cp(("arbitrary",)),
    )(*ins)


XBC_COL0 = 4096 // 128
DT_COL = 5632 // 128


def _conv_pre(xv, w_ref, b_ref):
    t = xv.shape[0]
    row = lax.broadcasted_iota(jnp.int32, xv.shape, 0)
    pre = xv * w_ref[3:4, :] + b_ref[...]
    shifted = []
    for k in range(3):
        s = 3 - k
        xs = jnp.where(row >= s, pltpu.roll(xv, s, 0), 0.0)
        shifted.append(xs)
        pre = pre + xs * w_ref[k:k + 1, :]
    return pre, shifted, row, t


def _conv_fwd(proj, conv_w, conv_b, name):
    t = proj.shape[0]

    def body(x_ref, w_ref, b_ref, u_ref):
        pre, _, _, _ = _conv_pre(x_ref[...], w_ref, b_ref)
        u_ref[...] = pre * _sig(pre)

    return pl.pallas_call(
        body, name=name, grid=(D_CONV // 128,),
        in_specs=[pl.BlockSpec((t, 128), lambda j: (0, XBC_COL0 + j)), pl.BlockSpec((4, 128), lambda j: (0, j)),
                  pl.BlockSpec((1, 128), lambda j: (0, j))],
        out_specs=pl.BlockSpec((t, 128), lambda j: (0, j)),
        out_shape=jax.ShapeDtypeStruct((t, D_CONV), F32),
        compiler_params=_cp(("parallel",)),
    )(proj, conv_w, conv_b)


def _conv_bwd(proj, du, conv_w, conv_b, name):
    t = proj.shape[0]

    def body(x_ref, du_ref, w_ref, b_ref, dx_ref, dw_ref, db_ref):
        pre, shifted, row, _ = _conv_pre(x_ref[...], w_ref, b_ref)
        s = _sig(pre)
        dpre = du_ref[...] * s * (1.0 + pre * (1.0 - s))
        db_ref[...] = _colsum(dpre)
        dx = dpre * w_ref[3:4, :]
        dw_ref[3:4, :] = _colsum(dpre * x_ref[...])
        for k in range(3):
            sft = 3 - k
            dw_ref[k:k + 1, :] = _colsum(dpre * shifted[k])
            back = jnp.where(row < t - sft, pltpu.roll(dpre, t - sft, 0), 0.0)
            dx = dx + back * w_ref[k:k + 1, :]
        dx_ref[...] = dx.astype(BF16)

    return pl.pallas_call(
        body, name=name, grid=(D_CONV // 128,),
        in_specs=[pl.BlockSpec((t, 128), lambda j: (0, XBC_COL0 + j)), pl.BlockSpec((t, 128), lambda j: (0, j)),
                  pl.BlockSpec((4, 128), lambda j: (0, j)), pl.BlockSpec((1, 128), lambda j: (0, j))],
        out_specs=[pl.BlockSpec((t, 128), lambda j: (0, j)), pl.BlockSpec((4, 128), lambda j: (0, j)),
                   pl.BlockSpec((1, 128), lambda j: (0, j))],
        out_shape=[jax.ShapeDtypeStruct((t, D_CONV), BF16), jax.ShapeDtypeStruct((4, D_CONV), F32),
                   jax.ShapeDtypeStruct((1, D_CONV), F32)],
        compiler_params=_cp(("parallel",)),
    )(proj, du, conv_w, conv_b)


def _ssd_common(dtraw_ref, dtb_ref, alog_ref, tril, expand, l_s, lt_s):
    lane = lax.broadcasted_iota(jnp.int32, (1, 128), 1)
    dt = _softplus(dtraw_ref[...] + dtb_ref[...])
    a = jnp.where(lane < N_HEADS, -jnp.exp(alog_ref[...]), 0.0)
    lcs = _dotx_l(tril, dt * a)
    l_s[...] = lcs
    lt_s[...] = lcs.T
    llast = l_s[CHUNK - 1:CHUNK, :]
    ea = _dotx_r(jnp.exp(lcs), expand, n=2)
    ds = _dotx_r(jnp.exp(llast - lcs), expand, n=2)
    dtx = _dotx_r(dt, expand, n=2)
    return dt, a, lcs, llast, ea, ds, dtx


def _head_col(lcs, h):
    lane = lax.broadcasted_iota(jnp.int32, lcs.shape, 1)
    return jnp.sum(jnp.where(lane == h, lcs, 0.0), axis=1, keepdims=True)


def _decay(lcs, lt_s, h, causal):
    seg = _head_col(lcs, h) - lt_s[h:h + 1, :]
    return jnp.exp(jnp.where(causal, seg, -1e30))


def _ssd_fwd(u, proj, dtb, alog, dsk, nw, cst, name):
    t = u.shape[0]
    nc = t // CHUNK

    def body(xs_ref, b_ref, c_ref, dtraw_ref, z_ref, dtb_ref, alog_ref, dsk_ref, nw_ref, tril_ref, exp_ref,
             y_ref, yn_ref, prev_ref, carry, l_s, lt_s, yd_s):
        i = pl.program_id(0)

        @pl.when(i == 0)
        def _():
            carry[...] = jnp.zeros_like(carry)

        expand = exp_ref[...]
        dt, a, lcs, llast, ea, ds, dtx = _ssd_common(dtraw_ref, dtb_ref, alog_ref, tril_ref[...], expand, l_s, lt_s)
        xs = xs_ref[...]
        xg = xs * dtx
        xgb = xg.astype(BF16)
        xgd = (xg * ds).astype(BF16)
        prev = carry[...]
        prev_ref[0] = prev
        prevb = prev.astype(BF16)
        ri = lax.broadcasted_iota(jnp.int32, (CHUNK, CHUNK), 0)
        ci = lax.broadcasted_iota(jnp.int32, (CHUNK, CHUNK), 1)
        causal = ri >= ci
        lane = lax.broadcasted_iota(jnp.int32, (1, 128), 1)
        new_states, yoff = [], []
        for g in range(2):
            bg = b_ref[:, g * 128:(g + 1) * 128].astype(BF16)
            cg = c_ref[:, g * 128:(g + 1) * 128].astype(BF16)
            sc = _dot(cg, bg, "nt")
            gs = slice(g * 512, (g + 1) * 512)
            new_states.append(_dot(bg, xgd[:, gs], "tn"))
            yoff.append(_dot(cg, prevb[:, gs]))
            for pr in range(4):
                col = g * 512 + pr * 128
                xp = xgb[:, col:col + 128]
                acc = jnp.zeros((CHUNK, 128), F32)
                for half in range(2):
                    h = g * 8 + pr * 2 + half
                    m = (sc * _decay(lcs, lt_s, h, causal)).astype(BF16)
                    keep = (lane < HEAD_DIM) if half == 0 else (lane >= HEAD_DIM)
                    acc = acc + _dot(m, jnp.where(keep, xp, jnp.zeros_like(xp)))
                yd_s[:, col:col + 128] = acc
        y = yd_s[...] + jnp.concatenate(yoff, axis=1) * ea + xs * dsk_ref[...]
        y_ref[...] = y
        carry[...] = prev * jnp.max(_dotx_r(jnp.exp(llast) + jnp.zeros((8, 128), F32), expand, n=2), axis=0, keepdims=True) \
            + jnp.concatenate(new_states, axis=1)
        z = z_ref[...]
        yz = y * (z * _sig(z))
        nwv = nw_ref[...]
        for g in range(2):
            gs = slice(g * 512, (g + 1) * 512)
            v = yz[:, gs]
            r = lax.rsqrt(jnp.mean(v * v, axis=-1, keepdims=True) + EPS)
            yn_ref[:, gs] = (v * r * nwv[:, gs]).astype(BF16)

    row = lambda w, col: pl.BlockSpec((CHUNK, w), lambda i: (i, col))
    return pl.pallas_call(
        body, name=name, grid=(nc,),
        in_specs=[row(1024, 0), row(256, 4), row(256, 5), row(128, DT_COL), row(1024, 0),
                  _fix_spec((1, 128)), _fix_spec((1, 128)), _fix_spec((1, D_MODEL)), _fix_spec((1, D_MODEL)),
                  _fix_spec((CHUNK, CHUNK)), _fix_spec((128, D_MODEL))],
        out_specs=[row(1024, 0), row(1024, 0), pl.BlockSpec((1, 128, D_MODEL), lambda i: (i, 0, 0))],
        out_shape=[jax.ShapeDtypeStruct((t, D_MODEL), F32), jax.ShapeDtypeStruct((t, 2 * D_MODEL), BF16),
                   jax.ShapeDtypeStruct((nc, 128, D_MODEL), F32)],
        scratch_shapes=[pltpu.VMEM((128, D_MODEL), F32), pltpu.VMEM((128, 128), F32), pltpu.VMEM((128, 128), F32),
                        pltpu.VMEM((CHUNK, D_MODEL), F32)],
        compiler_params=_cp(("arbitrary",)),
    )(u, u, u, proj, proj, dtb, alog, dsk, nw, cst["tril"], cst["expand"])


def _ssd_bwd(u, proj, y, prev, dycat, dtb, alog, dsk, nw, gpack, cst, name):
    t = u.shape[0]
    nc = t // CHUNK

    def body(xs_ref, b_ref, c_ref, dtraw_ref, z_ref, y_ref, prev_ref, dyn_ref, dtb_ref, alog_ref, dsk_ref, nw_ref,
             tril_ref, triu_ref, exp_ref, hs_ref, g_ref,
             du_ref, ddt_ref, dz_ref, acc_ref, acc16_ref, got_ref, dcarry, l_s, lt_s, dxg_s, wss, wrs):
        i = pl.program_id(0)
        w_start, w_finish = _swap_stages(g_ref, got_ref, wss, wrs)
        pl.when(i == 0)(w_start)

        @pl.when(i == 0)
        def _():
            dcarry[...] = jnp.zeros_like(dcarry)
            acc_ref[...] = jnp.zeros_like(acc_ref)
            acc16_ref[...] = jnp.zeros_like(acc16_ref)

        expand, hsum = exp_ref[...], hs_ref[...]
        dt, a, lcs, llast, ea, ds, dtx = _ssd_common(dtraw_ref, dtb_ref, alog_ref, tril_ref[...], expand, l_s, lt_s)
        xs = xs_ref[...]
        xg = xs * dtx
        xgb = xg.astype(BF16)
        xgdf = xg * ds
        xgd = xgdf.astype(BF16)
        dsk_v, nwv = dsk_ref[...], nw_ref[...]
        z, y = z_ref[...], y_ref[...]
        sz = _sig(z)
        silz = z * sz
        yz = y * silz
        dyn = dyn_ref[...]
        dyz_parts, dnw_parts = [], []
        for g in range(2):
            gs = slice(g * 512, (g + 1) * 512)
            v = yz[:, gs]
            r = lax.rsqrt(jnp.mean(v * v, axis=-1, keepdims=True) + EPS)
            yhat = v * r
            dnw_parts.append(_colsum(dyn[:, gs] * yhat))
            dw = dyn[:, gs] * nwv[:, gs]
            dyz_parts.append(r * (dw - yhat * jnp.mean(dw * yhat, axis=-1, keepdims=True)))
        dyz = jnp.concatenate(dyz_parts, axis=1)
        dy = dyz * silz
        dz_ref[...] = (dyz * y * (sz * (1.0 + z * (1.0 - sz)))).astype(BF16)
        acc_ref[0:1, :] += jnp.concatenate(dnw_parts, axis=1)
        acc_ref[1:2, :] += _colsum(dy * xs)
        dyb = dy.astype(BF16)
        dq = (dy * ea).astype(BF16)
        dcar = dcarry[...]
        dcarb = dcar.astype(BF16)
        prev = prev_ref[0]
        prevb = prev.astype(BF16)
        ri = lax.broadcasted_iota(jnp.int32, (CHUNK, CHUNK), 0)
        ci = lax.broadcasted_iota(jnp.int32, (CHUNK, CHUNK), 1)
        causal = ri >= ci
        lane = lax.broadcasted_iota(jnp.int32, (1, 128), 1)
        dprev, dxgd, yoff = [], [], []
        dl_l = jnp.zeros((CHUNK, CHUNK), F32)
        dl_s = jnp.zeros((CHUNK, CHUNK), F32)
        for g in range(2):
            gs = slice(g * 512, (g + 1) * 512)
            bg = b_ref[:, g * 128:(g + 1) * 128].astype(BF16)
            cg = c_ref[:, g * 128:(g + 1) * 128].astype(BF16)
            sc = _dot(cg, bg, "nt")
            yoff.append(_dot(cg, prevb[:, gs]))
            dcg = _dot(dq[:, gs], prevb[:, gs], "nt")
            dprev.append(_dot(cg, dq[:, gs], "tn"))
            dbg = _dot(xgd[:, gs], dcarb[:, gs], "nt")
            dxgd.append(_dot(bg, dcarb[:, gs]))
            dsc = jnp.zeros((CHUNK, CHUNK), F32)
            for pr in range(4):
                col = g * 512 + pr * 128
                xp = xgb[:, col:col + 128]
                dyp = dyb[:, col:col + 128]
                acc = jnp.zeros((CHUNK, 128), F32)
                for half in range(2):
                    h = g * 8 + pr * 2 + half
                    dec = _decay(lcs, lt_s, h, causal)
                    mf = sc * dec
                    keep = (lane < HEAD_DIM) if half == 0 else (lane >= HEAD_DIM)
                    dyh = jnp.where(keep, dyp, jnp.zeros_like(dyp))
                    dm = _dot(dyh, xp, "nt")
                    acc = acc + _dot(mf.astype(BF16), dyh, "tn")
                    dsc = dsc + dm * dec
                    gm = dm * mf
                    dl_l = dl_l + jnp.where(ci == h, jnp.sum(gm, axis=1, keepdims=True), 0.0)
                    dl_s = dl_s + jnp.where(ri == h, jnp.sum(gm, axis=0, keepdims=True), 0.0)
                dxg_s[:, col:col + 128] = acc
            dscb = dsc.astype(BF16)
            dcg = dcg + _dot(dscb, bg)
            dbg = dbg + _dot(dscb, cg, "tn")
            du_ref[:, 1024 + g * 128:1024 + (g + 1) * 128] = dbg
            du_ref[:, 1280 + g * 128:1280 + (g + 1) * 128] = dcg
        dxgd = jnp.concatenate(dxgd, axis=1)
        dxg = dxg_s[...] + dxgd * ds
        du_ref[:, 0:1024] = dy * dsk_v + dxg * dtx
        hs1 = _dotx_r(dxg * xs, hsum)
        yoff = jnp.concatenate(yoff, axis=1) * ea
        dl = dl_l - dl_s.T + _dotx_r(dy * yoff - xgdf * dxgd, hsum)
        rows8 = lax.broadcasted_iota(jnp.int32, (8, D_MODEL), 0)
        two = jnp.where(rows8 == 0, _colsum(dxgd * xgdf), jnp.where(rows8 == 1, _colsum(dcar * prev), 0.0))
        two = _dotx_r(two, hsum)
        r8 = lax.broadcasted_iota(jnp.int32, (8, 128), 0)
        dllast = _colsum(jnp.where(r8 == 0, two, 0.0)) + _colsum(jnp.where(r8 == 1, two, 0.0)) * jnp.exp(llast)
        rowi = lax.broadcasted_iota(jnp.int32, (CHUNK, 128), 0)
        dl = dl + jnp.where(rowi == CHUNK - 1, dllast, 0.0)
        dadt = _dotx_l(triu_ref[...], dl)
        ddt = dadt * a + hs1
        draw = ddt * _sig(dtraw_ref[...] + dtb_ref[...])
        ddt_ref[...] = draw.astype(BF16)
        acc16_ref[0:1, :] += _colsum(draw)
        acc16_ref[1:2, :] += _colsum(dadt * dt) * a
        dcarry[...] = dcar * jnp.max(_dotx_r(jnp.exp(llast) + jnp.zeros((8, 128), F32), expand, n=2), axis=0, keepdims=True) \
            + jnp.concatenate(dprev, axis=1)

        @pl.when(i == nc - 1)
        def _():
            hd = _dotx_r(acc_ref[...], hsum)
            acc16_ref[2:3, :] = _colsum(jnp.where(lax.broadcasted_iota(jnp.int32, (8, 128), 0) == 1, hd, 0.0))

        pl.when(i == nc - 1)(w_finish)

    rev = lambda w, col: pl.BlockSpec((CHUNK, w), lambda i: (nc - 1 - i, col))
    return pl.pallas_call(
        body, name=name, grid=(nc,),
        in_specs=[rev(1024, 0), rev(256, 4), rev(256, 5), rev(128, DT_COL), rev(1024, 0), rev(1024, 0),
                  pl.BlockSpec((1, 128, D_MODEL), lambda i: (nc - 1 - i, 0, 0)), rev(1024, 0),
                  _fix_spec((1, 128)), _fix_spec((1, 128)), _fix_spec((1, D_MODEL)), _fix_spec((1, D_MODEL)),
                  _fix_spec((CHUNK, CHUNK)), _fix_spec((CHUNK, CHUNK)), _fix_spec((128, D_MODEL)),
                  _fix_spec((D_MODEL, 128)), HB],
        out_specs=[rev(D_CONV, 0), rev(128, 0), rev(1024, 0), _fix_spec((8, D_MODEL)), _fix_spec((8, 128)), HB],
        out_shape=[jax.ShapeDtypeStruct((t, D_CONV), F32), jax.ShapeDtypeStruct((t, 128), BF16),
                   jax.ShapeDtypeStruct((t, D_MODEL), BF16), jax.ShapeDtypeStruct((8, D_MODEL), F32),
                   jax.ShapeDtypeStruct((8, 128), F32), _swap_shape(gpack)],
        scratch_shapes=[pltpu.VMEM((128, D_MODEL), F32), pltpu.VMEM((128, 128), F32), pltpu.VMEM((128, 128), F32),
                        pltpu.VMEM((CHUNK, D_MODEL), F32)] + SWAP_SEMS,
        compiler_params=_cp(("arbitrary",)),
    )(u, u, u, proj, proj, y, prev, dycat, dtb, alog, dsk, nw,
      cst["tril"], cst["triu"], cst["expand"], cst["hsum"], gpack)


def _head_rms(v, hsum, expand):
    ms = _dotx_r(v * v, hsum, n=2) * (1.0 / HEAD_DIM)
    return _dotx_r(lax.rsqrt(ms + EPS), expand, n=2)


def _qk_fwd(proj, qw, kw, cst, name):
    t = proj.shape[0]
    tm = min(t, 256)
    scale = HEAD_DIM ** -0.5

    def body(q_ref, k_ref, v_ref, qw_ref, kw_ref, hs_ref, exp_ref, qs_ref, kn_ref, vb_ref):
        hsum, expand = hs_ref[...], exp_ref[...]
        q, k = q_ref[...], k_ref[...]
        qs_ref[...] = (q * _head_rms(q, hsum, expand) * qw_ref[...] * scale).astype(BF16)
        kn_ref[...] = (k * _head_rms(k, hsum, expand) * kw_ref[...]).astype(BF16)
        vb_ref[...] = v_ref[...].astype(BF16)

    return pl.pallas_call(
        body, name=name, grid=(t // tm,),
        in_specs=[_row_spec(tm, col=1), _row_spec(tm, col=2), _row_spec(tm, col=3),
                  _fix_spec((1, D_MODEL)), _fix_spec((1, D_MODEL)), _fix_spec((D_MODEL, 128)),
                  _fix_spec((128, D_MODEL))],
        out_specs=[_row_spec(tm)] * 3, out_shape=[jax.ShapeDtypeStruct((t, D_MODEL), BF16)] * 3,
        compiler_params=_cp(("parallel",)),
    )(proj, proj, proj, qw, kw, cst["hsum"], cst["expand"])


def _qk_bwd(proj, dqs, dkn, dv, qw, kw, join, cst, name):
    t = proj.shape[0]
    tm = min(t, 256)
    nsteps = t // tm
    scale = HEAD_DIM ** -0.5

    def body(q_ref, k_ref, dq_ref, dk_ref, dv_ref, qw_ref, kw_ref, hs_ref, exp_ref, fold_ref, j_ref,
             oq_ref, ok_ref, ov_ref, dw_ref, jo_ref, jss, jrs):
        i = pl.program_id(0)
        j_start, j_finish = _join_stages(j_ref, jo_ref, jss, jrs)
        pl.when(i == 0)(j_start)
        hsum, expand = hs_ref[...], exp_ref[...]
        rows8 = lax.broadcasted_iota(jnp.int32, (8, D_MODEL), 0)
        sums = jnp.zeros((8, D_MODEL), F32)
        for n, (x_ref, d_ref, w_ref, o_ref, sc) in enumerate(
                [(q_ref, dq_ref, qw_ref, oq_ref, scale), (k_ref, dk_ref, kw_ref, ok_ref, 1.0)]):
            xv = x_ref[...]
            r = _head_rms(xv, hsum, expand)
            xhat = xv * r
            dn = d_ref[...] * sc
            sums = sums + jnp.where(rows8 == n, _colsum(dn * xhat), 0.0)
            dw = dn * w_ref[...]
            mean = _dotx_r(_dotx_r(dw * xhat, hsum, n=1), expand, n=2) * (1.0 / HEAD_DIM)
            o_ref[...] = (r * (dw - xhat * mean)).astype(BF16)
        ov_ref[...] = dv_ref[...].astype(BF16)
        folded = _dotx_r(sums, fold_ref[...])

        @pl.when(i == 0)
        def _():
            dw_ref[...] = folded

        @pl.when(i > 0)
        def _():
            dw_ref[...] += folded

        pl.when(i == nsteps - 1)(j_finish)

    return pl.pallas_call(
        body, name=name, grid=(nsteps,),
        in_specs=[_row_spec(tm, col=1), _row_spec(tm, col=2), _row_spec(tm), _row_spec(tm), _row_spec(tm),
                  _fix_spec((1, D_MODEL)), _fix_spec((1, D_MODEL)), _fix_spec((D_MODEL, 128)),
                  _fix_spec((128, D_MODEL)), _fix_spec((D_MODEL, 128)), HB],
        out_specs=[_row_spec(tm)] * 3 + [_fix_spec((8, 128)), HB],
        out_shape=[jax.ShapeDtypeStruct((t, D_MODEL), BF16)] * 3 + [jax.ShapeDtypeStruct((8, 128), F32),
                                                                   jax.ShapeDtypeStruct(join.shape, join.dtype)],
        scratch_shapes=SWAP_SEMS, compiler_params=_cp(("arbitrary",)),
    )(proj, proj, dqs, dkn, dv, qw, kw, cst["hsum"], cst["expand"], cst["fold"], join)


def _sb_masks(i, kb, tq, tk):
    tpos = i * tq + lax.broadcasted_iota(jnp.int32, (tq, 1), 0)
    spos = kb * tk + lax.broadcasted_iota(jnp.int32, (1, tk), 1)
    return spos < tpos


def _grid_marks(n0, n1):
    j, i = pl.program_id(0), pl.program_id(1)
    return (jnp.logical_and(j == 0, i == 0), jnp.logical_and(j == n0 // 2, i == 0),
            jnp.logical_and(j == n0 - 1, i == n1 - 1))


def _sb_fwd(qs, kn, vb, pack, ycat, cst, name):
    t = qs.shape[0]
    tq = tk = min(t, SB_TILE)
    nq = t // tq
    pairs = FWD_PAIRS
    ngrp = D_MODEL // (128 * pairs)
    nh = 2 * pairs
    lanes = lambda p: slice(p * 128, (p + 1) * 128)

    def body(q_ref, k_ref, v_ref, u_ref, p_ref, yc_ref, rt_ref, ob_ref, cnt_ref, gat_ref, acc, rs, gss, grs):
        del yc_ref
        at_first, at_mid, at_last = _grid_marks(ngrp, nq)
        g_start, g_relay, g_finish = _gather_stages(p_ref, gat_ref, gss, grs)
        pl.when(at_first)(g_start)
        pl.when(at_mid)(g_relay)
        i = pl.program_id(1)
        lane = lax.broadcasted_iota(jnp.int32, (1, 128), 1)
        qh = []
        for p in range(pairs):
            q2 = q_ref[:, lanes(p)]
            zero = jnp.zeros_like(q2)
            qh += [jnp.where(lane < HEAD_DIM, q2, zero), jnp.where(lane >= HEAD_DIM, q2, zero)]
        acc[...] = jnp.zeros_like(acc)
        rs[...] = jnp.zeros_like(rs)
        ustrict = u_ref[...]

        def tile(kb, masked):
            off = pl.multiple_of(kb * tk, tk)
            k2 = [k_ref[pl.ds(off, tk), lanes(p)] for p in range(pairs)]
            v2 = [v_ref[pl.ds(off, tk), lanes(p)] for p in range(pairs)]
            strict = _sb_masks(i, kb, tq, tk) if masked else None
            s = [_dot(qh[h], k2[h // 2], "nt") for h in range(nh)]
            a, r, lb = [None] * nh, [None] * nh, [None] * nh
            for h in range(nh):
                sp = _softplus(s[h])
                a[h] = s[h] - sp
                r[h] = jnp.where(strict, -sp, 0.0) if masked else -sp
                lb[h] = _dot2(r[h], ustrict)
            for h in range(nh):
                lw = a[h] + lb[h] + rs[h]
                w = jnp.exp(jnp.where(strict, lw, -1e30) if masked else lw)
                rs[h] = rs[h] + jnp.sum(r[h], axis=1, keepdims=True)
                acc[h] = acc[h] + _dot(w.astype(BF16), v2[h // 2])

        tile(i, True)

        def live():
            return jnp.max(functools.reduce(jnp.maximum, [rs[h] for h in range(nh)]))

        def more(c):
            return jnp.logical_and(c[0] < i, c[1] > SB_DEAD)

        def step(c):
            tile(i - 1 - c[0], False)
            return c[0] + 1, live()

        n_off, _ = lax.while_loop(more, step, (jnp.int32(0), live()))
        cnt_ref[pl.program_id(0), i] = n_off.astype(F32)
        for p in range(pairs):
            rt_ref[:, lanes(p)] = jnp.where(lane < HEAD_DIM, rs[2 * p], rs[2 * p + 1])
            ob_ref[:, lanes(p)] = jnp.where(lane < HEAD_DIM, acc[2 * p], acc[2 * p + 1]).astype(BF16)
        pl.when(at_last)(g_finish)

    blk = lambda rows, imap: pl.BlockSpec((rows, (128 * pairs)), imap)
    return pl.pallas_call(
        body, name=name, grid=(ngrp, nq),
        in_specs=[blk(tq, lambda j, i: (i, j)), blk(t, lambda j, i: (0, j)), blk(t, lambda j, i: (0, j)),
                  _fix_spec((2 * tk, tk)), HB, pl.BlockSpec(memory_space=pl.ANY)],
        out_specs=[blk(tq, lambda j, i: (i, j)), blk(tq, lambda j, i: (i, ngrp + j)),
                   pl.BlockSpec(memory_space=pltpu.SMEM), HB],
        out_shape=[jax.ShapeDtypeStruct((t, D_MODEL), F32), jax.ShapeDtypeStruct(ycat.shape, ycat.dtype),
                   jax.ShapeDtypeStruct((ngrp, nq), F32),
                   jax.ShapeDtypeStruct((N_SHARDS,) + pack.shape, pack.dtype)],
        scratch_shapes=[pltpu.VMEM((nh, tq, 128), F32), pltpu.VMEM((nh, tq, 1), F32)] + GATHER_SEMS,
        input_output_aliases={5: 1},
        compiler_params=_cp(("arbitrary", "arbitrary")),
    )(qs, kn, vb, _doubled(cst["ustrict"], tk), pack, ycat)


def _sb_bwd(qs, kn, vb, rtot, cnt, dycat, csum_b, cst, name):
    t = qs.shape[0]
    tq = tk = min(t, SB_TILE)
    nq = t // tq
    pairs = BWD_PAIRS
    ngrp = D_MODEL // (128 * pairs)
    nh = 2 * pairs
    lanes = lambda p: slice(p * 128, (p + 1) * 128)

    def body(q_ref, k_ref, v_ref, rt_ref, do_ref, us_ref, ui_ref, cnt_ref, xs_ref, dq_ref, dk_ref, dv_ref, xr_ref,
             acc, rs, es, xss, xrs):
        at_first, _, at_last = _grid_marks(ngrp, nq)
        x_start, x_finish = _exchange_stages(xs_ref, xr_ref, xss, xrs)
        pl.when(at_first)(x_start)
        i = pl.program_id(1)
        lane = lax.broadcasted_iota(jnp.int32, (1, 128), 1)
        keep = [lane < HEAD_DIM, lane >= HEAD_DIM]
        qh, doh, rtot_h = [], [], []
        for p in range(pairs):
            q2, rt = q_ref[:, lanes(p)], rt_ref[:, lanes(p)]
            do2b = do_ref[:, lanes(p)].astype(BF16)
            qh += [jnp.where(kp, q2, jnp.zeros_like(q2)) for kp in keep]
            doh += [jnp.where(kp, do2b, jnp.zeros_like(do2b)) for kp in keep]
            rtot_h += [jnp.sum(jnp.where(lane == n * HEAD_DIM, rt, 0.0), axis=1, keepdims=True) for n in range(2)]
        acc[...] = jnp.zeros_like(acc)
        rs[...] = jnp.zeros_like(rs)
        es[...] = jnp.zeros_like(es)

        @pl.when(i == 0)
        def _():
            dk_ref[...] = jnp.zeros_like(dk_ref)
            dv_ref[...] = jnp.zeros_like(dv_ref)

        ule, ult = us_ref[...], ui_ref[...]

        def tile(kb, masked):
            off = pl.multiple_of(kb * tk, tk)
            k2 = [k_ref[pl.ds(off, tk), lanes(p)] for p in range(pairs)]
            v2 = [v_ref[pl.ds(off, tk), lanes(p)] for p in range(pairs)]
            strict = _sb_masks(i, kb, tq, tk) if masked else None
            s = [_dot(qh[h], k2[h // 2], "nt") for h in range(nh)]
            dw = [_dot(doh[h], v2[h // 2], "nt") for h in range(nh)]
            a, sg, r, pin, w, e, cin = ([None] * nh for _ in range(7))
            for h in range(nh):
                sp = _softplus(s[h])
                a[h] = s[h] - sp
                sg[h] = jnp.exp(a[h])
                r[h] = jnp.where(strict, -sp, 0.0) if masked else -sp
                pin[h] = _dot2(r[h], ule)
            for h in range(nh):
                lw = a[h] + ((rtot_h[h] - rs[h]) - pin[h])
                w[h] = jnp.exp(jnp.where(strict, lw, -1e30) if masked else lw)
                e[h] = w[h] * dw[h]
                cin[h] = _dot2(e[h], ult)
            for p in range(pairs):
                dk_t = jnp.zeros((tk, 128), F32)
                dv_t = jnp.zeros((tk, 128), F32)
                for h in (2 * p, 2 * p + 1):
                    dl = e[h] * (1.0 - sg[h]) - (es[h] + cin[h]) * sg[h]
                    dl = (jnp.where(strict, dl, 0.0) if masked else dl).astype(BF16)
                    rs[h] = rs[h] + jnp.sum(r[h], axis=1, keepdims=True)
                    es[h] = es[h] + jnp.sum(e[h], axis=1, keepdims=True)
                    acc[h] = acc[h] + _dot(dl, k2[p])
                    dk_t = dk_t + _dot(dl, qh[h], "tn")
                    dv_t = dv_t + _dot(w[h].astype(BF16), doh[h], "tn")
                dk_ref[pl.ds(off, tk), lanes(p)] += dk_t
                dv_ref[pl.ds(off, tk), lanes(p)] += dv_t

        def step(kb, carry):
            tile(kb, False)
            return carry

        n_off = cnt_ref[pl.program_id(0) * BWD_PAIRS // FWD_PAIRS, i].astype(jnp.int32)
        lax.fori_loop(i - n_off, i, step, 0)
        tile(i, True)
        for p in range(pairs):
            dq_ref[:, lanes(p)] = jnp.where(lane < HEAD_DIM, acc[2 * p], acc[2 * p + 1])
        pl.when(at_last)(x_finish)

    blk = lambda rows, imap: pl.BlockSpec((rows, (128 * pairs)), imap)
    return pl.pallas_call(
        body, name=name, grid=(ngrp, nq),
        in_specs=[blk(tq, lambda j, i: (i, j)), blk(t, lambda j, i: (0, j)), blk(t, lambda j, i: (0, j)),
                  blk(tq, lambda j, i: (i, j)), blk(tq, lambda j, i: (i, ngrp + j)),
                  _fix_spec((2 * tk, tk)), _fix_spec((2 * tk, tk)), pl.BlockSpec(memory_space=pltpu.SMEM), HB],
        out_specs=[blk(tq, lambda j, i: (i, j)), blk(t, lambda j, i: (0, j)), blk(t, lambda j, i: (0, j)), HB],
        out_shape=[jax.ShapeDtypeStruct((t, D_MODEL), F32)] * 3
        + [jax.ShapeDtypeStruct((3,) + csum_b.shape[1:], csum_b.dtype)],
        scratch_shapes=[pltpu.VMEM((nh, tq, 128), F32), pltpu.VMEM((nh, tq, 1), F32), pltpu.VMEM((nh, tq, 1), F32)]
        + EXCHANGE_SEMS,
        compiler_params=_cp(("arbitrary", "arbitrary")),
    )(qs, kn, vb, rtot, dycat, _doubled(cst["ule"], tk), _doubled(cst["ult"], tk), cnt, csum_b)


def _adamw(w, g, m, v, name):
    lead = (1,) * (w.ndim - 2)
    rows, cols = w.shape[-2:]
    fits = [d for d in range(8, rows, 8) if rows % d == 0 and d * cols * 4 <= ADAM_BLOCK_BYTES]
    tr = max(fits) if fits else rows
    c1 = 1.0 - ADAM_B1 ** ADAM_STEP
    c2 = 1.0 - ADAM_B2 ** ADAM_STEP

    def body(w_ref, g_ref, m_ref, v_ref, d_ref, nm_ref, nv_ref):
        gv = g_ref[...]
        nm = ADAM_B1 * m_ref[...] + (1.0 - ADAM_B1) * gv
        nv = ADAM_B2 * v_ref[...] + (1.0 - ADAM_B2) * (gv * gv)
        nm_ref[...] = nm
        nv_ref[...] = nv
        d_ref[...] = -ADAM_LR * ((nm / c1) / (jnp.sqrt(nv / c2) + ADAM_EPS) + ADAM_WD * w_ref[...])

    spec = pl.BlockSpec(lead + (tr, cols), lambda i: (0,) * len(lead) + (i, 0))
    return pl.pallas_call(
        body, name=name, grid=(rows // tr,), in_specs=[spec] * 4, out_specs=[spec] * 3,
        out_shape=[jax.ShapeDtypeStruct(w.shape, F32)] * 3, compiler_params=_cp(("parallel",)),
    )(w, g, m, v)


def _place():
    x, y, c = lax.axis_index("x"), lax.axis_index("y"), lax.axis_index("c")
    chips = [(1 - x, y), (x, 1 - y), (1 - x, 1 - y)]
    return x, y, c, chips


VM = pl.BlockSpec(memory_space=pltpu.VMEM)
HB = pl.BlockSpec(memory_space=pltpu.HBM)


def _gather_all(p_ref, gat_ref, ss, rs):
    x, y, c, _ = _place()
    me = 4 * x + 2 * y + c
    peers = [(x, y, 1 - c), (1 - x, y, c), (x, 1 - y, c), (1 - x, 1 - y, c),
             (1 - x, y, 1 - c), (x, 1 - y, 1 - c), (1 - x, 1 - y, 1 - c)]

    def copy(k, slot, to):
        return pltpu.make_async_remote_copy(src_ref=p_ref, dst_ref=gat_ref.at[slot], send_sem=ss.at[k],
                                            recv_sem=rs.at[k], device_id=to, device_id_type=MESH)

    sends = [copy(k, me, p) for k, p in enumerate(peers)]
    for s in sends:
        s.start()
    gat_ref[me] = p_ref[...]
    for k, p in enumerate(peers):
        copy(k, 4 * p[0] + 2 * p[1] + p[2], p).wait_recv()
    for s in sends:
        s.wait_send()


ALL_SEMS = [pltpu.SemaphoreType.DMA((7,)), pltpu.SemaphoreType.DMA((7,))]


def _small_reduce(pack, name):
    rows = pack.shape[0]

    def body(p_ref, gat_ref, sum_ref, ss, rs):
        _gather_all(p_ref, gat_ref, ss, rs)
        tot = gat_ref[0]
        for b in range(1, 8):
            tot = tot + gat_ref[b]
        sum_ref[...] = tot

    return pl.pallas_call(
        body, name=name, in_specs=[VM], out_specs=[VM, VM],
        out_shape=[jax.ShapeDtypeStruct((8, rows, D_MODEL), F32), jax.ShapeDtypeStruct((rows, D_MODEL), F32)],
        scratch_shapes=ALL_SEMS, compiler_params=_cp(),
    )(pack)


def _prologue(cpack, w_ada, b_shard, wpack_in, name):
    def body(cp_ref, w_ref, b_ref, p_ref, gat_ref, modp_ref, gin_ref, ss1, rs1, ss2, rs2, gss, grs):
        g_start, g_relay, g_finish = _gather_stages(p_ref, gin_ref, gss, grs)
        g_start()
        _gather_all(cp_ref, gat_ref, ss1, rs1)
        x, y, c, chips = _place()
        sh = 2 * x + y
        row = lax.broadcasted_iota(jnp.int32, (8, D_MODEL), 0)
        cv = jnp.zeros((8, D_MODEL), F32)
        for b in range(8):
            cv = jnp.where(row == b, gat_ref[b, 0:8, :], cv)
        cv = cv * _sig(cv)
        modp_ref[sh] = jnp.dot(cv, w_ref[...], precision=lax.Precision.HIGHEST,
                               preferred_element_type=F32) + b_ref[...]

        def copy(k, slot, to):
            return pltpu.make_async_remote_copy(src_ref=modp_ref.at[slot], dst_ref=modp_ref.at[slot],
                                                send_sem=ss2.at[k], recv_sem=rs2.at[k], device_id=to,
                                                device_id_type=MESH)

        sends = [copy(k, sh, (*ch, c)) for k, ch in enumerate(chips)]
        for s in sends:
            s.start()
        for k, ch in enumerate(chips):
            copy(k, 2 * ch[0] + ch[1], (*ch, c)).wait_recv()
        for s in sends:
            s.wait_send()
        g_relay()
        g_finish()

    return pl.pallas_call(
        body, name=name, in_specs=[VM, VM, VM, HB], out_specs=[VM, VM, HB],
        out_shape=[jax.ShapeDtypeStruct((8,) + cpack.shape, F32),
                   jax.ShapeDtypeStruct((N_SHARDS, 8, 6 * D_MODEL // N_SHARDS), F32),
                   jax.ShapeDtypeStruct((N_SHARDS,) + wpack_in.shape, wpack_in.dtype)],
        scratch_shapes=ALL_SEMS + EXCHANGE_SEMS + GATHER_SEMS, compiler_params=_cp(),
    )(cpack, w_ada, b_shard, wpack_in)


def _gather_stages(p_ref, out_ref, ss, rs):
    hf = p_ref.shape[0] // 2
    x, y, c, chips = _place()
    sh = 2 * x + y
    sib = (x, y, 1 - c)
    slots = [2 * ch[0] + ch[1] for ch in chips]

    def half(slot, hc):
        return out_ref.at[slot, pl.ds(hc * hf, hf), :]

    def copy(k, src, slot, hc, to):
        return pltpu.make_async_remote_copy(src_ref=src, dst_ref=half(slot, hc), send_sem=ss.at[k],
                                            recv_sem=rs.at[k], device_id=to, device_id_type=MESH)

    def first():
        return [copy(j, p_ref.at[pl.ds(c * hf, hf), :], sh, c, (*ch, c)) for j, ch in enumerate(chips)]

    def passed():
        return [copy(3 + j, half(slots[j], c), slots[j], c, sib) for j in range(3)]

    def start():
        for cp in first():
            cp.start()

    def relay():
        for j, cp in enumerate(passed()):
            copy(j, half(slots[j], c), slots[j], c, (*chips[j], c)).wait_recv()
            cp.start()

    def finish():
        for j in range(3):
            copy(3 + j, half(slots[j], 1 - c), slots[j], 1 - c, sib).wait_recv()
        for cp in first() + passed():
            cp.wait_send()

    return start, relay, finish


GATHER_SEMS = [pltpu.SemaphoreType.DMA((6,)), pltpu.SemaphoreType.DMA((6,))]


def _swap_stages(g_ref, out_ref, ss, rs):
    hf = g_ref.shape[1] // 2
    x, y, c, _ = _place()

    def copy():
        return pltpu.make_async_remote_copy(
            src_ref=g_ref.at[pl.ds(0, N_SHARDS), pl.ds((1 - c) * hf, hf), :], dst_ref=out_ref,
            send_sem=ss, recv_sem=rs, device_id=(x, y, 1 - c), device_id_type=MESH)

    return (lambda: copy().start()), (lambda: copy().wait())


SWAP_SEMS = [pltpu.SemaphoreType.DMA, pltpu.SemaphoreType.DMA]


def _swap_shape(g):
    return jax.ShapeDtypeStruct((N_SHARDS, g.shape[1] // 2, D_MODEL), g.dtype)


def _sibling_swap(g, name):
    def body(g_ref, out_ref, ss, rs):
        for stage in _swap_stages(g_ref, out_ref, ss, rs):
            stage()

    return pl.pallas_call(
        body, name=name, in_specs=[HB], out_specs=HB, out_shape=_swap_shape(g),
        scratch_shapes=SWAP_SEMS, compiler_params=_cp(),
    )(g)


def _row_tile(rows, width_bytes, cap_bytes):
    fits = [d for d in range(8, rows + 1, 8) if rows % d == 0 and d * width_bytes <= cap_bytes]
    return max(fits)


def _chip_sum(g, got, c_idx, name):
    hf = got.shape[1]
    tr = _row_tile(hf, D_MODEL * 4, 3 << 20)
    nb = hf // tr

    def body(c_ref, a_ref, b_ref, s_ref, sb_ref):
        s = a_ref[...] + b_ref[...]
        s_ref[...] = s
        sb_ref[...] = s.astype(BF16)

    blk = pl.BlockSpec((1, tr, D_MODEL), lambda s, i, c_ref: (s, i, 0))
    return pl.pallas_call(
        body, name=name,
        grid_spec=pltpu.PrefetchScalarGridSpec(
            num_scalar_prefetch=1, grid=(N_SHARDS, nb),
            in_specs=[pl.BlockSpec((1, tr, D_MODEL), lambda s, i, c_ref: (s, c_ref[0] * nb + i, 0)), blk],
            out_specs=[blk, blk]),
        out_shape=[jax.ShapeDtypeStruct((N_SHARDS, hf, D_MODEL), F32),
                   jax.ShapeDtypeStruct((N_SHARDS, hf, D_MODEL), BF16)],
        compiler_params=_cp(("parallel", "parallel")),
    )(c_idx, g, got)


def _exchange_stages(s_ref, out_ref, ss, rs):
    x, y, c, chips = _place()

    def sends():
        return [pltpu.make_async_remote_copy(src_ref=s_ref.at[2 * ch[0] + ch[1]], dst_ref=out_ref.at[k],
                                             send_sem=ss.at[k], recv_sem=rs.at[k], device_id=(*ch, c),
                                             device_id_type=MESH) for k, ch in enumerate(chips)]

    def start():
        for cp in sends():
            cp.start()

    def finish():
        for cp in sends():
            cp.wait()

    return start, finish


EXCHANGE_SEMS = [pltpu.SemaphoreType.DMA((3,)), pltpu.SemaphoreType.DMA((3,))]


def _total_half(s, got, sh_idx, name):
    hf = got.shape[1]
    tr = _row_tile(hf, D_MODEL * 4, 3 << 20)
    nb = hf // tr

    def body(sh_ref, a_ref, r0, r1, r2, o_ref):
        o_ref[...] = ((a_ref[0] + r0[0].astype(F32)) + r1[0].astype(F32)) + r2[0].astype(F32)

    rspec = lambda k: pl.BlockSpec((1, tr, D_MODEL), lambda i, sh_ref: (k, i, 0))
    return pl.pallas_call(
        body, name=name,
        grid_spec=pltpu.PrefetchScalarGridSpec(
            num_scalar_prefetch=1, grid=(nb,),
            in_specs=[pl.BlockSpec((1, tr, D_MODEL), lambda i, sh_ref: (sh_ref[0], i, 0)),
                      rspec(0), rspec(1), rspec(2)],
            out_specs=pl.BlockSpec((tr, D_MODEL), lambda i, sh_ref: (i, 0))),
        out_shape=jax.ShapeDtypeStruct((hf, D_MODEL), F32),
        compiler_params=_cp(("parallel",)),
    )(sh_idx, s, got, got, got)


def _join_stages(t_ref, out_ref, ss, rs):
    x, y, c, _ = _place()

    def copy():
        return pltpu.make_async_remote_copy(src_ref=t_ref, dst_ref=out_ref, send_sem=ss, recv_sem=rs,
                                            device_id=(x, y, 1 - c), device_id_type=MESH)

    return (lambda: copy().start()), (lambda: copy().wait())


def _w_ada_grad(cond, dmod_cols, name):
    def body(c_ref, d_ref, o_ref):
        cv = c_ref[...]
        cv = cv * _sig(cv)
        o_ref[...] = lax.dot_general(cv, d_ref[...], _DN["tn"], precision=lax.Precision.HIGHEST,
                                     preferred_element_type=F32)

    return pl.pallas_call(
        body, name=name, in_specs=[VM, VM], out_specs=VM,
        out_shape=jax.ShapeDtypeStruct((D_MODEL, dmod_cols.shape[1]), F32), compiler_params=_cp(),
    )(cond, dmod_cols)


def _pad_rows(a, rows):
    return jnp.pad(a, ((0, rows - a.shape[0]), (0, 0)))


def _pad_cols(a, cols):
    return jnp.pad(a, ((0, 0), (0, cols - a.shape[1])))


def _unpack_rest(p):
    o = 0
    out = []
    for r in (R_OUT, R_FF, R_FF, R_FF):
        out.append(p[..., o:o + r, :])
        o += r
    return out


def _both_halves(tot, other, ac):
    return jnp.where(ac == 0, jnp.concatenate([tot, other], axis=0), jnp.concatenate([other, tot], axis=0))


def _reduce_head(gpack, ac, tag):
    got = _sibling_swap(gpack, "rs_sibling_swap_" + tag)
    return _chip_sum(gpack, got, ac.reshape(1).astype(jnp.int32), "rs_chip_sum_" + tag)


def kernel(x, c, w_ada, b_ada, norm1_w, w_in, conv_w, conv_b, dt_bias, a_log, d_skip, ssd_norm_w, q_norm_w, k_norm_w, w_out, norm2_w, w_gate, w_up, w_down, loss_target, m_w_ada, m_b_ada, m_norm1_w, m_w_in, m_conv_w, m_conv_b, m_dt_bias, m_a_log, m_d_skip, m_ssd_norm_w, m_q_norm_w, m_k_norm_w, m_w_out, m_norm2_w, m_w_gate, m_w_up, m_w_down, v_w_ada, v_b_ada, v_norm1_w, v_w_in, v_conv_w, v_conv_b, v_dt_bias, v_a_log, v_d_skip, v_ssd_norm_w, v_q_norm_w, v_k_norm_w, v_w_out, v_norm2_w, v_w_gate, v_w_up, v_w_down):
    cst = _consts()
    ax, ay, ac = lax.axis_index("x"), lax.axis_index("y"), lax.axis_index("c")
    shard = 2 * ax + ay
    me = 4 * ax + 2 * ay + ac
    xs = x[0]
    tgt = loss_target[0]
    w_in_cols = w_in.shape[2]
    conv_cols = conv_w.shape[2]

    tr3 = lambda a: jnp.transpose(a, (0, 2, 1))
    lin = lambda a: tr3(a).reshape(-1, 128)
    unlin = lambda a: tr3(a.reshape(1, w_in_cols, D_MODEL))
    wpack_in = _pad_rows(tr3(w_in.astype(BF16))[0], R_IN)
    wpack_rest = jnp.concatenate([w_out[0], tr3(w_gate)[0], tr3(w_up)[0], w_down[0]], axis=0).astype(BF16)
    own = lax.broadcasted_iota(jnp.int32, (N_SHARDS, 1, 1), 0) == shard

    cw_flat = _pad_cols(conv_w[0].reshape(1, -1), 2 * D_MODEL).reshape(2, D_MODEL)
    cpack = jnp.concatenate([jnp.broadcast_to(c, (8, D_MODEL)), _pad_rows(cw_flat, 8)], axis=0)
    mod_w = 6 * D_MODEL // N_SHARDS
    b_shard = lax.dynamic_slice(b_ada, (0, shard * mod_w), (1, mod_w))
    gat, modp, gp_in = _prologue(cpack, w_ada[0], b_shard, wpack_in, "prologue")
    c_all = gat[:, 0, :]
    cw = gat[0::2, 8:10, :].reshape(N_SHARDS, 2 * D_MODEL)[:, :4 * conv_cols].reshape(N_SHARDS, 4, conv_cols)
    conv_w_full = jnp.transpose(cw, (1, 0, 2)).reshape(4, D_CONV)
    mod_mine = lax.dynamic_slice(modp, (0, me, 0), (N_SHARDS, 1, mod_w)).reshape(6, D_MODEL)
    mod = _pad_rows(mod_mine, 8)
    p_in = jnp.where(own, wpack_in[None], gp_in)
    wi_t = p_in[:, :w_in_cols, :].reshape(D_IN_PROJ, D_MODEL)
    w_inp_t = jnp.concatenate([wi_t[0:1024], wi_t[2576:5648], wi_t[1024:2560], wi_t[2560:2576],
                               jnp.zeros((112, D_MODEL), BF16)], axis=0)

    pad128 = lambda a: _pad_cols(a, 128)
    dtb, alog = pad128(dt_bias), pad128(a_log)
    dsk = jnp.repeat(d_skip, HEAD_DIM, axis=1)
    qw, kw = jnp.tile(q_norm_w, (1, N_HEADS)), jnp.tile(k_norm_w, (1, N_HEADS))

    h1 = _norm_mod(xs, norm1_w, mod, 0, "norm1")
    proj = _matmul(h1, w_inp_t, "nt", F32, "in_proj")
    u = _conv_fwd(proj, conv_w_full, conv_b, "conv_fwd")
    y_ssd, yn, prev = _ssd_fwd(u, proj, dtb, alog, dsk, ssd_norm_w, cst, "ssd_fwd")
    qs, kn, vb = _qk_fwd(proj, qw, kw, cst, "qk_norm")
    rtot, ycat, cnt, gp_rest = _sb_fwd(qs, kn, vb, wpack_rest, yn, cst, "sb_fwd")
    p_out, p_gate, p_up, p_down = _unpack_rest(jnp.where(own, wpack_rest[None], gp_rest))
    w_o = p_out.reshape(2 * D_MODEL, D_MODEL)
    w_gu_t = jnp.concatenate([p_gate.reshape(D_FF, D_MODEL), p_up.reshape(D_FF, D_MODEL)], axis=0)
    w_d = p_down.reshape(D_FF, D_MODEL)
    mix = _matmul(ycat, w_o, "nn", F32, "out_proj")
    x1, h2 = _resid_norm(xs, mix, norm2_w, mod, "resid_norm2")
    gu = _matmul(h2, w_gu_t, "nt", BF16, "ffn_in")
    act = _act_fwd(gu, "ffn_act")
    ffn = _matmul(act, w_d, "nn", F32, "ffn_out", tk_cap=1408)
    dffn, dout, dg2, loss8 = _loss_head(x1, ffn, tgt, mod, "loss_head")
    loss = lax.psum(loss8[0, 0], ("x", "y", "c"))

    dact = _matmul(dffn, w_d, "nt", BF16, "d_act")
    g_down = _matmul(act, dffn, "tn", F32, "g_w_down", tm_cap=1408)
    dgu = _act_bwd(dact, gu, "ffn_act_bwd")
    dh2 = _matmul(dgu, w_gu_t, "nn", F32, "d_h2", tk_cap=1408)
    g_gu_t = _matmul(dgu, h2, "tn", F32, "g_w_gu", tm_cap=1408)
    dx1, dmix, acc2 = _norm_bwd(dh2, x1, dout, mix, norm2_w, mod, 3, 2, "norm2_bwd")
    dycat = _matmul(dmix, w_o, "nt", F32, "d_ycat")
    g_out = _matmul(ycat, dmix, "tn", F32, "g_w_out")
    gpack_rest = jnp.concatenate([
        g_out.reshape(N_SHARDS, R_OUT, D_MODEL),
        g_gu_t[:D_FF].reshape(N_SHARDS, R_FF, D_MODEL), g_gu_t[D_FF:].reshape(N_SHARDS, R_FF, D_MODEL),
        g_down.reshape(N_SHARDS, R_FF, D_MODEL)], axis=1)
    du, ddt, dz, acc_ssd, acc16, got_s = _ssd_bwd(u, proj, y_ssd, prev, dycat, dtb, alog, dsk, ssd_norm_w,
                                                  gpack_rest, cst, "ssd_bwd")
    csum_r, csum_rb = _chip_sum(gpack_rest, got_s, ac.reshape(1).astype(jnp.int32), "rs_chip_sum_rest")
    dqs, dkn, dv, got_r = _sb_bwd(qs, kn, vb, rtot, cnt, dycat, csum_rb, cst, "sb_bwd")
    sh_idx = shard.reshape(1).astype(jnp.int32)
    tot_r = _total_half(csum_r, got_r, sh_idx, "rs_total_rest")
    dq, dk, dvb, acc_qk, other_r = _qk_bwd(proj, dqs, dkn, dv, qw, kw, tot_r, cst, "qk_norm_bwd")
    r_out, r_gate, r_up, r_down = _unpack_rest(_both_halves(tot_r, other_r, ac))
    dxbc, g_conv_w, g_conv_b = _conv_bwd(proj, du, conv_w_full, conv_b, "conv_bwd")
    dproj = jnp.concatenate([dz, dq, dk, dvb, dxbc, ddt], axis=1)
    g_inp_t = _matmul(dproj, h1, "tn", F32, "g_w_in", tm_cap=1920)
    gi_t = jnp.concatenate([g_inp_t[0:1024], g_inp_t[4096:5632], g_inp_t[5632:5648], g_inp_t[1024:4096]], axis=0)
    gpack_in = jnp.pad(gi_t.reshape(N_SHARDS, w_in_cols, D_MODEL), ((0, 0), (0, R_IN - w_in_cols), (0, 0)))
    csum_i, csum_ib = _reduce_head(gpack_in, ac, "in")
    dh1, got_i = _matmul(dproj, w_inp_t, "nn", F32, "d_h1", tk_cap=1152, exchange=csum_ib)
    tot_i = _total_half(csum_i, got_i, sh_idx, "rs_total_in")
    grad_x, acc1, other_i = _norm_bwd(dh1, xs, dx1, None, norm1_w, mod, 0, None, "norm1_bwd", join=tot_i)
    r_in = _both_halves(tot_i, other_i, ac)

    last = jnp.concatenate([acc_qk[0:1, 0:64], acc_qk[1:2, 0:64], acc16[0:1, 0:16], acc16[1:2, 0:16],
                            acc16[2:3, 0:16]], axis=1)
    spack = jnp.concatenate([
        acc1[0:2], acc2[3:4], acc2[0:2], dg2,
        acc1[2:3], acc2[2:3], acc_ssd[0:1],
        _pad_cols(g_conv_b, 2 * D_MODEL).reshape(2, D_MODEL),
        g_conv_w.reshape(6, D_MODEL),
        _pad_cols(last, D_MODEL)], axis=0)
    sgat, ssum = _small_reduce(_pad_rows(spack, SMALL_ROWS), "gather_small")
    g_b_ada = ssum[0:6].reshape(1, 6 * D_MODEL)
    g_norm1, g_norm2, g_ssdn = ssum[6:7], ssum[7:8], ssum[8:9]
    g_cb = ssum[9:11].reshape(1, 2 * D_MODEL)[:, :D_CONV]
    g_cw = lax.dynamic_slice(ssum[11:17].reshape(4, D_CONV), (0, shard * conv_cols), (4, conv_cols))
    g_qn, g_kn = ssum[17:18, 0:64], ssum[17:18, 64:128]
    g_dtb, g_alog, g_dsk = ssum[17:18, 128:144], ssum[17:18, 144:160], ssum[17:18, 160:176]
    dmod_all = sgat[:, 0:6, :].reshape(8, 6 * D_MODEL)
    g_w_ada = _w_ada_grad(c_all, lax.dynamic_slice(dmod_all, (0, shard * mod_w), (8, mod_w)), "g_w_ada")


    grads = dict(w_ada=g_w_ada, b_ada=g_b_ada, norm1_w=g_norm1, w_in=r_in[:w_in_cols].reshape(-1, 128), conv_w=g_cw,
                 conv_b=g_cb, dt_bias=g_dtb, a_log=g_alog, d_skip=g_dsk, ssd_norm_w=g_ssdn, q_norm_w=g_qn,
                 k_norm_w=g_kn, w_out=r_out, norm2_w=g_norm2, w_gate=r_gate, w_up=r_up, w_down=r_down)
    weights = dict(w_ada=(w_ada, m_w_ada, v_w_ada), b_ada=(b_ada, m_b_ada, v_b_ada),
                   norm1_w=(norm1_w, m_norm1_w, v_norm1_w), w_in=(w_in, m_w_in, v_w_in),
                   conv_w=(conv_w, m_conv_w, v_conv_w), conv_b=(conv_b, m_conv_b, v_conv_b),
                   dt_bias=(dt_bias, m_dt_bias, v_dt_bias), a_log=(a_log, m_a_log, v_a_log),
                   d_skip=(d_skip, m_d_skip, v_d_skip), ssd_norm_w=(ssd_norm_w, m_ssd_norm_w, v_ssd_norm_w),
                   q_norm_w=(q_norm_w, m_q_norm_w, v_q_norm_w), k_norm_w=(k_norm_w, m_k_norm_w, v_k_norm_w),
                   w_out=(w_out, m_w_out, v_w_out), norm2_w=(norm2_w, m_norm2_w, v_norm2_w),
                   w_gate=(w_gate, m_w_gate, v_w_gate), w_up=(w_up, m_w_up, v_w_up),
                   w_down=(w_down, m_w_down, v_w_down))
    views = dict(w_in=(lin, unlin), w_gate=(tr3, tr3), w_up=(tr3, tr3))
    same = lambda a: a
    names = list(weights)
    g_out_l, d_out_l, m_out_l, v_out_l = [], [], [], []
    for n in names:
        view, back = views.get(n, (same, same))
        w, m, v = (view(a) for a in weights[n])
        g = grads[n].reshape(w.shape)
        d, nm, nv = _adamw(w, g, m, v, "adamw_" + n)
        g_out_l.append(back(g))
        d_out_l.append(back(d))
        m_out_l.append(back(nm))
        v_out_l.append(back(nv))
    return (loss, grad_x[None], *g_out_l, *d_out_l, *m_out_l, *v_out_l)
```

```python
import functools

import numpy as np
import jax
import jax.numpy as jnp
from jax import lax
from jax.experimental import pallas as pl
from jax.experimental.pallas import tpu as pltpu

F32, BF16 = jnp.float32, jnp.bfloat16
MESH = pl.DeviceIdType.MESH

D_MODEL = 1024
HEAD_DIM = 64
N_HEADS = 16
D_CONV = 1536
D_FF = 2816
D_IN_PROJ = 5648
D_PROJ_PAD = 5760
CHUNK = 128
SB_TILE = 256
SB_PAIRS = 2
SB_LANES = 128 * SB_PAIRS
SB_DEAD = -105.0
EPS = 1e-6
N_SHARDS = 4
R_IN, R_OUT, R_FF = 1440, 512, 704
SMALL_ROWS = 24

ADAM_LR, ADAM_B1, ADAM_B2, ADAM_EPS, ADAM_WD, ADAM_STEP = 0.001, 0.9, 0.999, 1e-08, 0.01, 10

VMEM_LIMIT = 48 * 1024 * 1024
ADAM_BLOCK_BYTES = 3 * 512 * 1024

_DN = {"nn": (((1,), (0,)), ((), ())), "nt": (((1,), (1,)), ((), ())), "tn": (((0,), (0,)), ((), ()))}


def _dot(a, b, dims="nn"):
    return lax.dot_general(a, b, _DN[dims], preferred_element_type=F32)


def _pieces(x, n):
    out = []
    for _ in range(n - 1):
        hi = lax.bitcast_convert_type(lax.bitcast_convert_type(x, jnp.int32) & jnp.int32(-65536), F32)
        out.append(hi.astype(BF16))
        x = x - hi
    out.append(x.astype(BF16))
    return out


def _dotx_r(x, b_exact, n=3):
    return _dot(jnp.concatenate(_pieces(x, n), axis=1), jnp.concatenate([b_exact] * n, axis=0))


def _dotx_l(a_exact, x, n=3):
    return _dot(jnp.concatenate([a_exact] * n, axis=1), jnp.concatenate(_pieces(x, n), axis=0))


def _dot2(x, b2):
    return _dot(jnp.concatenate(_pieces(x, 2), axis=1), b2)


def _sig(x):
    return 1.0 / (1.0 + jnp.exp(-x))


def _softplus(x):
    return jnp.maximum(x, 0.0) + jnp.log(1.0 + jnp.exp(-jnp.abs(x)))


def _cp(sem=None, vmem=VMEM_LIMIT):
    return pltpu.CompilerParams(dimension_semantics=sem, vmem_limit_bytes=vmem)


def _colsum(x):
    return jnp.sum(x, axis=0, keepdims=True)


def _consts():
    ch = np.arange(D_MODEL)
    expand = (np.arange(128)[:, None] == (ch // HEAD_DIM)[None, :]).astype(np.float32)
    fold = (ch[:, None] % HEAD_DIM == np.arange(128)[None, :]).astype(np.float32)
    i = np.arange(CHUNK)
    tril = (i[:, None] >= i[None, :]).astype(np.float32)
    j = np.arange(SB_TILE)
    ustrict = (j[:, None] > j[None, :]).astype(np.float32)
    ule = (j[:, None] <= j[None, :]).astype(np.float32)
    ult = (j[:, None] < j[None, :]).astype(np.float32)
    c = lambda a: jnp.asarray(a, BF16)
    return dict(expand=c(expand), hsum=c(expand.T), fold=c(fold), tril=c(tril), triu=c(tril.T),
                ustrict=ustrict, ule=ule, ult=ult)


def _doubled(tri, tk):
    b = tri[:tk, :tk]
    return jnp.asarray(np.concatenate([b, b], axis=0), BF16)


def _pick(n, cap):
    best = 128
    for t in range(128, min(n, cap) + 1, 128):
        if n % t == 0:
            best = t
    return n if n <= cap else best


def _matmul(a, b, dims, out_dtype, name, tm_cap=1024, tn_cap=2048, tk_cap=1024, exchange=None):
    if dims == "nn":
        (m, k), (_, n) = a.shape, b.shape
    elif dims == "nt":
        (m, k), (n, _) = a.shape, b.shape
    else:
        (k, m), (_, n) = a.shape, b.shape
    tm, tn, tk = _pick(m, tm_cap), _pick(n, tn_cap), _pick(k, tk_cap)
    nk = k // tk
    a_spec = (pl.BlockSpec((tk, tm), lambda i, j, kk: (kk, i)) if dims == "tn"
              else pl.BlockSpec((tm, tk), lambda i, j, kk: (i, kk)))
    b_spec = (pl.BlockSpec((tn, tk), lambda i, j, kk: (j, kk)) if dims == "nt"
              else pl.BlockSpec((tk, tn), lambda i, j, kk: (kk, j)))

    grid = (m // tm, n // tn, nk)

    def body(a_ref, b_ref, *rest):
        if exchange is None:
            o_ref, acc_ref = rest
        else:
            xs_ref, o_ref, xr_ref, acc_ref, xss, xrs = rest
            ids = [pl.program_id(d) for d in range(3)]
            x_start, x_finish = _exchange_stages(xs_ref, xr_ref, xss, xrs)
            pl.when(functools.reduce(jnp.logical_and, [p == 0 for p in ids]))(x_start)
        kk = pl.program_id(2)
        part = _dot(a_ref[...], b_ref[...], dims)
        if nk == 1:
            o_ref[...] = part.astype(out_dtype)
        else:
            @pl.when(kk == 0)
            def _():
                acc_ref[...] = part

            @pl.when(kk > 0)
            def _():
                acc_ref[...] += part

            @pl.when(kk == nk - 1)
            def _():
                o_ref[...] = acc_ref[...].astype(out_dtype)
        if exchange is not None:
            pl.when(functools.reduce(jnp.logical_and, [p == g - 1 for p, g in zip(ids, grid)]))(x_finish)

    in_specs = [a_spec, b_spec]
    out_specs = [pl.BlockSpec((tm, tn), lambda i, j, kk: (i, j))]
    out_shape = [jax.ShapeDtypeStruct((m, n), out_dtype)]
    scratch = [pltpu.VMEM((tm, tn) if nk > 1 else (8, 128), F32)]
    args = [a, b]
    if exchange is not None:
        in_specs.append(HB)
        out_specs.append(HB)
        out_shape.append(jax.ShapeDtypeStruct((3,) + exchange.shape[1:], exchange.dtype))
        scratch += EXCHANGE_SEMS
        args.append(exchange)
    out = pl.pallas_call(
        body, name=name, grid=grid, in_specs=in_specs, out_specs=out_specs, out_shape=out_shape,
        scratch_shapes=scratch,
        compiler_params=_cp(("parallel", "parallel", "arbitrary") if exchange is None else ("arbitrary",) * 3),
    )(*args)
    return out[0] if exchange is None else out


def _row_spec(tm, width=D_MODEL, col=0):
    return pl.BlockSpec((tm, width), lambda i: (i, col))


def _fix_spec(shape):
    return pl.BlockSpec(shape, lambda *_: (0,) * len(shape))


def _norm_mod(x, nw, mod, row_sh, name):
    t = x.shape[0]
    tm = min(t, 512)

    def body(x_ref, nw_ref, mod_ref, h_ref):
        xv = x_ref[...]
        r = lax.rsqrt(jnp.mean(xv * xv, axis=-1, keepdims=True) + EPS)
        sh = mod_ref[row_sh:row_sh + 1, :]
        sc = mod_ref[row_sh + 1:row_sh + 2, :]
        h_ref[...] = (xv * r * nw_ref[...] * (1.0 + sc) + sh).astype(BF16)

    return pl.pallas_call(
        body, name=name, grid=(t // tm,),
        in_specs=[_row_spec(tm), _fix_spec((1, D_MODEL)), _fix_spec((8, D_MODEL))],
        out_specs=_row_spec(tm), out_shape=jax.ShapeDtypeStruct((t, D_MODEL), BF16),
        compiler_params=_cp(("parallel",)),
    )(x, nw, mod)


def _resid_norm(x, mix, nw, mod, name):
    t = x.shape[0]
    tm = min(t, 512)

    def body(x_ref, mix_ref, nw_ref, mod_ref, x1_ref, h_ref):
        x1 = x_ref[...] + mod_ref[2:3, :] * mix_ref[...]
        x1_ref[...] = x1
        r = lax.rsqrt(jnp.mean(x1 * x1, axis=-1, keepdims=True) + EPS)
        h_ref[...] = (x1 * r * nw_ref[...] * (1.0 + mod_ref[4:5, :]) + mod_ref[3:4, :]).astype(BF16)

    return pl.pallas_call(
        body, name=name, grid=(t // tm,),
        in_specs=[_row_spec(tm), _row_spec(tm), _fix_spec((1, D_MODEL)), _fix_spec((8, D_MODEL))],
        out_specs=[_row_spec(tm), _row_spec(tm)],
        out_shape=[jax.ShapeDtypeStruct((t, D_MODEL), F32), jax.ShapeDtypeStruct((t, D_MODEL), BF16)],
        compiler_params=_cp(("parallel",)),
    )(x, mix, nw, mod)


def _act_fwd(gu, name):
    t = gu.shape[0]
    tm, tn = min(t, 512), D_FF // 2
    nb = D_FF // tn

    def body(g_ref, u_ref, a_ref):
        g = g_ref[...].astype(F32)
        a_ref[...] = (g * _sig(g) * u_ref[...].astype(F32)).astype(BF16)

    return pl.pallas_call(
        body, name=name, grid=(t // tm, nb),
        in_specs=[pl.BlockSpec((tm, tn), lambda i, j: (i, j)), pl.BlockSpec((tm, tn), lambda i, j: (i, j + nb))],
        out_specs=pl.BlockSpec((tm, tn), lambda i, j: (i, j)),
        out_shape=jax.ShapeDtypeStruct((t, D_FF), BF16),
        compiler_params=_cp(("parallel", "parallel")),
    )(gu, gu)


def _act_bwd(dact, gu, name):
    t = gu.shape[0]
    tm = min(t, 256)

    def body(d_ref, g_ref, u_ref, o_ref):
        g, d = g_ref[...].astype(F32), d_ref[...].astype(F32)
        s = _sig(g)
        o_ref[:, 0:D_FF] = (d * u_ref[...].astype(F32) * s * (1.0 + g * (1.0 - s))).astype(BF16)
        o_ref[:, D_FF:2 * D_FF] = (d * g * s).astype(BF16)

    return pl.pallas_call(
        body, name=name, grid=(t // tm,),
        in_specs=[pl.BlockSpec((tm, D_FF), lambda i: (i, 0)), pl.BlockSpec((tm, D_FF), lambda i: (i, 0)),
                  pl.BlockSpec((tm, D_FF), lambda i: (i, 1))],
        out_specs=pl.BlockSpec((tm, 2 * D_FF), lambda i: (i, 0)),
        out_shape=jax.ShapeDtypeStruct((t, 2 * D_FF), BF16),
        compiler_params=_cp(("parallel",)),
    )(dact, gu, gu)


def _loss_head(x1, ffn, tgt, mod, name):
    t = x1.shape[0]
    tm = min(t, 512)

    def body(x1_ref, f_ref, t_ref, mod_ref, dffn_ref, dout_ref, dg2_ref, loss_ref):
        i = pl.program_id(0)
        g2 = mod_ref[5:6, :]
        f = f_ref[...]
        err = x1_ref[...] + g2 * f - t_ref[...]
        dout = err * (1.0 / D_MODEL)
        dout_ref[...] = dout
        dffn_ref[...] = (dout * g2).astype(BF16)
        part = jnp.zeros((8, 128), F32) + 0.5 * jnp.sum(jnp.mean(err * err, axis=-1, keepdims=True))

        @pl.when(i == 0)
        def _():
            dg2_ref[...] = _colsum(dout * f)
            loss_ref[...] = part

        @pl.when(i > 0)
        def _():
            dg2_ref[...] += _colsum(dout * f)
            loss_ref[...] += part

    return pl.pallas_call(
        body, name=name, grid=(t // tm,),
        in_specs=[_row_spec(tm), _row_spec(tm), _row_spec(tm), _fix_spec((8, D_MODEL))],
        out_specs=[_row_spec(tm), _row_spec(tm), _fix_spec((1, D_MODEL)), _fix_spec((8, 128))],
        out_shape=[jax.ShapeDtypeStruct((t, D_MODEL), BF16), jax.ShapeDtypeStruct((t, D_MODEL), F32),
                   jax.ShapeDtypeStruct((1, D_MODEL), F32), jax.ShapeDtypeStruct((8, 128), F32)],
        compiler_params=_cp(("arbitrary",)),
    )(x1, ffn, tgt, mod)


def _norm_bwd(dh, xin, dres, aux, nw, mod, row_sh, gate_row, name):
    t = xin.shape[0]
    tm = min(t, 512)
    with_gate = gate_row is not None

    def body(*refs):
        if with_gate:
            dh_ref, x_ref, dr_ref, aux_ref, nw_ref, mod_ref, dx_ref, dg_ref, acc_ref = refs
        else:
            dh_ref, x_ref, dr_ref, nw_ref, mod_ref, dx_ref, acc_ref = refs
        i = pl.program_id(0)
        xv, dhv = x_ref[...], dh_ref[...]
        r = lax.rsqrt(jnp.mean(xv * xv, axis=-1, keepdims=True) + EPS)
        xn = xv * r
        nwv = nw_ref[...]
        sc1 = 1.0 + mod_ref[row_sh + 1:row_sh + 2, :]
        dxn = dhv * (nwv * sc1)
        dx = dr_ref[...] + r * (dxn - xn * jnp.mean(dxn * xn, axis=-1, keepdims=True))
        dx_ref[...] = dx
        dhx = dhv * xn
        rows = [_colsum(dhv), _colsum(dhx * nwv), _colsum(dhx * sc1)]
        if with_gate:
            dg_ref[...] = (dx * mod_ref[gate_row:gate_row + 1, :]).astype(BF16)
            rows.append(_colsum(dx * aux_ref[...]))

        @pl.when(i == 0)
        def _():
            acc_ref[...] = jnp.zeros_like(acc_ref)

        for k, v in enumerate(rows):
            acc_ref[k:k + 1, :] += v

    ins = [dh, xin, dres] + ([aux] if with_gate else []) + [nw, mod]
    in_specs = [_row_spec(tm)] * (4 if with_gate else 3) + [_fix_spec((1, D_MODEL)), _fix_spec((8, D_MODEL))]
    out_specs = [_row_spec(tm)] + ([_row_spec(tm)] if with_gate else []) + [_fix_spec((8, D_MODEL))]
    out_shape = ([jax.ShapeDtypeStruct((t, D_MODEL), F32)]
                 + ([jax.ShapeDtypeStruct((t, D_MODEL), BF16)] if with_gate else [])
                 + [jax.ShapeDtypeStruct((8, D_MODEL), F32)])
    return pl.pallas_call(
        body, name=name, grid=(t // tm,), in_specs=in_specs, out_specs=out_specs, out_shape=out_shape,
        compiler_params=_cp(("arbitrary",)),
    )(*ins)


XBC_COL0 = 4096 // 128
DT_COL = 5632 // 128


def _conv_pre(xv, w_ref, b_ref):
    t = xv.shape[0]
    row = lax.broadcasted_iota(jnp.int32, xv.shape, 0)
    pre = xv * w_ref[3:4, :] + b_ref[...]
    shifted = []
    for k in range(3):
        s = 3 - k
        xs = jnp.where(row >= s, pltpu.roll(xv, s, 0), 0.0)
        shifted.append(xs)
        pre = pre + xs * w_ref[k:k + 1, :]
    return pre, shifted, row, t


def _conv_fwd(proj, conv_w, conv_b, name):
    t = proj.shape[0]

    def body(x_ref, w_ref, b_ref, u_ref):
        pre, _, _, _ = _conv_pre(x_ref[...], w_ref, b_ref)
        u_ref[...] = pre * _sig(pre)

    return pl.pallas_call(
        body, name=name, grid=(D_CONV // 128,),
        in_specs=[pl.BlockSpec((t, 128), lambda j: (0, XBC_COL0 + j)), pl.BlockSpec((4, 128), lambda j: (0, j)),
                  pl.BlockSpec((1, 128), lambda j: (0, j))],
        out_specs=pl.BlockSpec((t, 128), lambda j: (0, j)),
        out_shape=jax.ShapeDtypeStruct((t, D_CONV), F32),
        compiler_params=_cp(("parallel",)),
    )(proj, conv_w, conv_b)


def _conv_bwd(proj, du, conv_w, conv_b, name):
    t = proj.shape[0]

    def body(x_ref, du_ref, w_ref, b_ref, dx_ref, dw_ref, db_ref):
        pre, shifted, row, _ = _conv_pre(x_ref[...], w_ref, b_ref)
        s = _sig(pre)
        dpre = du_ref[...] * s * (1.0 + pre * (1.0 - s))
        db_ref[...] = _colsum(dpre)
        dx = dpre * w_ref[3:4, :]
        dw_ref[3:4, :] = _colsum(dpre * x_ref[...])
        for k in range(3):
            sft = 3 - k
            dw_ref[k:k + 1, :] = _colsum(dpre * shifted[k])
            back = jnp.where(row < t - sft, pltpu.roll(dpre, t - sft, 0), 0.0)
            dx = dx + back * w_ref[k:k + 1, :]
        dx_ref[...] = dx.astype(BF16)

    return pl.pallas_call(
        body, name=name, grid=(D_CONV // 128,),
        in_specs=[pl.BlockSpec((t, 128), lambda j: (0, XBC_COL0 + j)), pl.BlockSpec((t, 128), lambda j: (0, j)),
                  pl.BlockSpec((4, 128), lambda j: (0, j)), pl.BlockSpec((1, 128), lambda j: (0, j))],
        out_specs=[pl.BlockSpec((t, 128), lambda j: (0, j)), pl.BlockSpec((4, 128), lambda j: (0, j)),
                   pl.BlockSpec((1, 128), lambda j: (0, j))],
        out_shape=[jax.ShapeDtypeStruct((t, D_CONV), BF16), jax.ShapeDtypeStruct((4, D_CONV), F32),
                   jax.ShapeDtypeStruct((1, D_CONV), F32)],
        compiler_params=_cp(("parallel",)),
    )(proj, du, conv_w, conv_b)


def _ssd_common(dtraw_ref, dtb_ref, alog_ref, tril, expand, l_s, lt_s):
    lane = lax.broadcasted_iota(jnp.int32, (1, 128), 1)
    dt = _softplus(dtraw_ref[...] + dtb_ref[...])
    a = jnp.where(lane < N_HEADS, -jnp.exp(alog_ref[...]), 0.0)
    lcs = _dotx_l(tril, dt * a)
    l_s[...] = lcs
    lt_s[...] = lcs.T
    llast = l_s[CHUNK - 1:CHUNK, :]
    ea = _dotx_r(jnp.exp(lcs), expand, n=2)
    ds = _dotx_r(jnp.exp(llast - lcs), expand, n=2)
    dtx = _dotx_r(dt, expand, n=2)
    return dt, a, lcs, llast, ea, ds, dtx


def _head_col(lcs, h):
    lane = lax.broadcasted_iota(jnp.int32, lcs.shape, 1)
    return jnp.sum(jnp.where(lane == h, lcs, 0.0), axis=1, keepdims=True)


def _decay(lcs, lt_s, h, causal):
    seg = _head_col(lcs, h) - lt_s[h:h + 1, :]
    return jnp.exp(jnp.where(causal, seg, -1e30))


def _ssd_fwd(u, proj, dtb, alog, dsk, nw, cst, name):
    t = u.shape[0]
    nc = t // CHUNK

    def body(xs_ref, b_ref, c_ref, dtraw_ref, z_ref, dtb_ref, alog_ref, dsk_ref, nw_ref, tril_ref, exp_ref,
             y_ref, yn_ref, prev_ref, carry, l_s, lt_s, yd_s):
        i = pl.program_id(0)

        @pl.when(i == 0)
        def _():
            carry[...] = jnp.zeros_like(carry)

        expand = exp_ref[...]
        dt, a, lcs, llast, ea, ds, dtx = _ssd_common(dtraw_ref, dtb_ref, alog_ref, tril_ref[...], expand, l_s, lt_s)
        xs = xs_ref[...]
        xg = xs * dtx
        xgb = xg.astype(BF16)
        xgd = (xg * ds).astype(BF16)
        prev = carry[...]
        prev_ref[0] = prev
        prevb = prev.astype(BF16)
        ri = lax.broadcasted_iota(jnp.int32, (CHUNK, CHUNK), 0)
        ci = lax.broadcasted_iota(jnp.int32, (CHUNK, CHUNK), 1)
        causal = ri >= ci
        lane = lax.broadcasted_iota(jnp.int32, (1, 128), 1)
        new_states, yoff = [], []
        for g in range(2):
            bg = b_ref[:, g * 128:(g + 1) * 128].astype(BF16)
            cg = c_ref[:, g * 128:(g + 1) * 128].astype(BF16)
            sc = _dot(cg, bg, "nt")
            gs = slice(g * 512, (g + 1) * 512)
            new_states.append(_dot(bg, xgd[:, gs], "tn"))
            yoff.append(_dot(cg, prevb[:, gs]))
            for pr in range(4):
                col = g * 512 + pr * 128
                xp = xgb[:, col:col + 128]
                acc = jnp.zeros((CHUNK, 128), F32)
                for half in range(2):
                    h = g * 8 + pr * 2 + half
                    m = (sc * _decay(lcs, lt_s, h, causal)).astype(BF16)
                    keep = (lane < HEAD_DIM) if half == 0 else (lane >= HEAD_DIM)
                    acc = acc + _dot(m, jnp.where(keep, xp, jnp.zeros_like(xp)))
                yd_s[:, col:col + 128] = acc
        y = yd_s[...] + jnp.concatenate(yoff, axis=1) * ea + xs * dsk_ref[...]
        y_ref[...] = y
        carry[...] = prev * jnp.max(_dotx_r(jnp.exp(llast) + jnp.zeros((8, 128), F32), expand, n=2), axis=0, keepdims=True) \
            + jnp.concatenate(new_states, axis=1)
        z = z_ref[...]
        yz = y * (z * _sig(z))
        nwv = nw_ref[...]
        for g in range(2):
            gs = slice(g * 512, (g + 1) * 512)
            v = yz[:, gs]
            r = lax.rsqrt(jnp.mean(v * v, axis=-1, keepdims=True) + EPS)
            yn_ref[:, gs] = (v * r * nwv[:, gs]).astype(BF16)

    row = lambda w, col: pl.BlockSpec((CHUNK, w), lambda i: (i, col))
    return pl.pallas_call(
        body, name=name, grid=(nc,),
        in_specs=[row(1024, 0), row(256, 4), row(256, 5), row(128, DT_COL), row(1024, 0),
                  _fix_spec((1, 128)), _fix_spec((1, 128)), _fix_spec((1, D_MODEL)), _fix_spec((1, D_MODEL)),
                  _fix_spec((CHUNK, CHUNK)), _fix_spec((128, D_MODEL))],
        out_specs=[row(1024, 0), row(1024, 0), pl.BlockSpec((1, 128, D_MODEL), lambda i: (i, 0, 0))],
        out_shape=[jax.ShapeDtypeStruct((t, D_MODEL), F32), jax.ShapeDtypeStruct((t, 2 * D_MODEL), BF16),
                   jax.ShapeDtypeStruct((nc, 128, D_MODEL), F32)],
        scratch_shapes=[pltpu.VMEM((128, D_MODEL), F32), pltpu.VMEM((128, 128), F32), pltpu.VMEM((128, 128), F32),
                        pltpu.VMEM((CHUNK, D_MODEL), F32)],
        compiler_params=_cp(("arbitrary",)),
    )(u, u, u, proj, proj, dtb, alog, dsk, nw, cst["tril"], cst["expand"])


def _ssd_bwd(u, proj, y, prev, dycat, dtb, alog, dsk, nw, gpack, cst, name):
    t = u.shape[0]
    nc = t // CHUNK

    def body(xs_ref, b_ref, c_ref, dtraw_ref, z_ref, y_ref, prev_ref, dyn_ref, dtb_ref, alog_ref, dsk_ref, nw_ref,
             tril_ref, triu_ref, exp_ref, hs_ref, g_ref,
             du_ref, ddt_ref, dz_ref, acc_ref, acc16_ref, got_ref, dcarry, l_s, lt_s, dxg_s, wss, wrs):
        i = pl.program_id(0)
        w_start, w_finish = _swap_stages(g_ref, got_ref, wss, wrs)
        pl.when(i == 0)(w_start)

        @pl.when(i == 0)
        def _():
            dcarry[...] = jnp.zeros_like(dcarry)
            acc_ref[...] = jnp.zeros_like(acc_ref)
            acc16_ref[...] = jnp.zeros_like(acc16_ref)

        expand, hsum = exp_ref[...], hs_ref[...]
        dt, a, lcs, llast, ea, ds, dtx = _ssd_common(dtraw_ref, dtb_ref, alog_ref, tril_ref[...], expand, l_s, lt_s)
        xs = xs_ref[...]
        xg = xs * dtx
        xgb = xg.astype(BF16)
        xgdf = xg * ds
        xgd = xgdf.astype(BF16)
        dsk_v, nwv = dsk_ref[...], nw_ref[...]
        z, y = z_ref[...], y_ref[...]
        sz = _sig(z)
        silz = z * sz
        yz = y * silz
        dyn = dyn_ref[...]
        dyz_parts, dnw_parts = [], []
        for g in range(2):
            gs = slice(g * 512, (g + 1) * 512)
            v = yz[:, gs]
            r = lax.rsqrt(jnp.mean(v * v, axis=-1, keepdims=True) + EPS)
            yhat = v * r
            dnw_parts.append(_colsum(dyn[:, gs] * yhat))
            dw = dyn[:, gs] * nwv[:, gs]
            dyz_parts.append(r * (dw - yhat * jnp.mean(dw * yhat, axis=-1, keepdims=True)))
        dyz = jnp.concatenate(dyz_parts, axis=1)
        dy = dyz * silz
        dz_ref[...] = (dyz * y * (sz * (1.0 + z * (1.0 - sz)))).astype(BF16)
        acc_ref[0:1, :] += jnp.concatenate(dnw_parts, axis=1)
        acc_ref[1:2, :] += _colsum(dy * xs)
        dyb = dy.astype(BF16)
        dq = (dy * ea).astype(BF16)
        dcar = dcarry[...]
        dcarb = dcar.astype(BF16)
        prev = prev_ref[0]
        prevb = prev.astype(BF16)
        ri = lax.broadcasted_iota(jnp.int32, (CHUNK, CHUNK), 0)
        ci = lax.broadcasted_iota(jnp.int32, (CHUNK, CHUNK), 1)
        causal = ri >= ci
        lane = lax.broadcasted_iota(jnp.int32, (1, 128), 1)
        dprev, dxgd, yoff = [], [], []
        dl_l = jnp.zeros((CHUNK, CHUNK), F32)
        dl_s = jnp.zeros((CHUNK, CHUNK), F32)
        for g in range(2):
            gs = slice(g * 512, (g + 1) * 512)
            bg = b_ref[:, g * 128:(g + 1) * 128].astype(BF16)
            cg = c_ref[:, g * 128:(g + 1) * 128].astype(BF16)
            sc = _dot(cg, bg, "nt")
            yoff.append(_dot(cg, prevb[:, gs]))
            dcg = _dot(dq[:, gs], prevb[:, gs], "nt")
            dprev.append(_dot(cg, dq[:, gs], "tn"))
            dbg = _dot(xgd[:, gs], dcarb[:, gs], "nt")
            dxgd.append(_dot(bg, dcarb[:, gs]))
            dsc = jnp.zeros((CHUNK, CHUNK), F32)
            for pr in range(4):
                col = g * 512 + pr * 128
                xp = xgb[:, col:col + 128]
                dyp = dyb[:, col:col + 128]
                acc = jnp.zeros((CHUNK, 128), F32)
                for half in range(2):
                    h = g * 8 + pr * 2 + half
                    dec = _decay(lcs, lt_s, h, causal)
                    mf = sc * dec
                    keep = (lane < HEAD_DIM) if half == 0 else (lane >= HEAD_DIM)
                    dyh = jnp.where(keep, dyp, jnp.zeros_like(dyp))
                    dm = _dot(dyh, xp, "nt")
                    acc = acc + _dot(mf.astype(BF16), dyh, "tn")
                    dsc = dsc + dm * dec
                    gm = dm * mf
                    dl_l = dl_l + jnp.where(ci == h, jnp.sum(gm, axis=1, keepdims=True), 0.0)
                    dl_s = dl_s + jnp.where(ri == h, jnp.sum(gm, axis=0, keepdims=True), 0.0)
                dxg_s[:, col:col + 128] = acc
            dscb = dsc.astype(BF16)
            dcg = dcg + _dot(dscb, bg)
            dbg = dbg + _dot(dscb, cg, "tn")
            du_ref[:, 1024 + g * 128:1024 + (g + 1) * 128] = dbg
            du_ref[:, 1280 + g * 128:1280 + (g + 1) * 128] = dcg
        dxgd = jnp.concatenate(dxgd, axis=1)
        dxg = dxg_s[...] + dxgd * ds
        du_ref[:, 0:1024] = dy * dsk_v + dxg * dtx
        hs1 = _dotx_r(dxg * xs, hsum, n=2)
        yoff = jnp.concatenate(yoff, axis=1) * ea
        dl = dl_l - dl_s.T + _dotx_r(dy * yoff - xgdf * dxgd, hsum, n=2)
        rows8 = lax.broadcasted_iota(jnp.int32, (8, D_MODEL), 0)
        two = jnp.where(rows8 == 0, _colsum(dxgd * xgdf), jnp.where(rows8 == 1, _colsum(dcar * prev), 0.0))
        two = _dotx_r(two, hsum)
        r8 = lax.broadcasted_iota(jnp.int32, (8, 128), 0)
        dllast = _colsum(jnp.where(r8 == 0, two, 0.0)) + _colsum(jnp.where(r8 == 1, two, 0.0)) * jnp.exp(llast)
        rowi = lax.broadcasted_iota(jnp.int32, (CHUNK, 128), 0)
        dl = dl + jnp.where(rowi == CHUNK - 1, dllast, 0.0)
        dadt = _dotx_l(triu_ref[...], dl)
        ddt = dadt * a + hs1
        draw = ddt * _sig(dtraw_ref[...] + dtb_ref[...])
        ddt_ref[...] = draw.astype(BF16)
        acc16_ref[0:1, :] += _colsum(draw)
        acc16_ref[1:2, :] += _colsum(dadt * dt) * a
        dcarry[...] = dcar * jnp.max(_dotx_r(jnp.exp(llast) + jnp.zeros((8, 128), F32), expand, n=2), axis=0, keepdims=True) \
            + jnp.concatenate(dprev, axis=1)

        @pl.when(i == nc - 1)
        def _():
            hd = _dotx_r(acc_ref[...], hsum)
            acc16_ref[2:3, :] = _colsum(jnp.where(lax.broadcasted_iota(jnp.int32, (8, 128), 0) == 1, hd, 0.0))

        pl.when(i == nc - 1)(w_finish)

    rev = lambda w, col: pl.BlockSpec((CHUNK, w), lambda i: (nc - 1 - i, col))
    return pl.pallas_call(
        body, name=name, grid=(nc,),
        in_specs=[rev(1024, 0), rev(256, 4), rev(256, 5), rev(128, DT_COL), rev(1024, 0), rev(1024, 0),
                  pl.BlockSpec((1, 128, D_MODEL), lambda i: (nc - 1 - i, 0, 0)), rev(1024, 0),
                  _fix_spec((1, 128)), _fix_spec((1, 128)), _fix_spec((1, D_MODEL)), _fix_spec((1, D_MODEL)),
                  _fix_spec((CHUNK, CHUNK)), _fix_spec((CHUNK, CHUNK)), _fix_spec((128, D_MODEL)),
                  _fix_spec((D_MODEL, 128)), HB],
        out_specs=[rev(D_CONV, 0), rev(128, 0), rev(1024, 0), _fix_spec((8, D_MODEL)), _fix_spec((8, 128)), HB],
        out_shape=[jax.ShapeDtypeStruct((t, D_CONV), F32), jax.ShapeDtypeStruct((t, 128), BF16),
                   jax.ShapeDtypeStruct((t, D_MODEL), BF16), jax.ShapeDtypeStruct((8, D_MODEL), F32),
                   jax.ShapeDtypeStruct((8, 128), F32), _swap_shape(gpack)],
        scratch_shapes=[pltpu.VMEM((128, D_MODEL), F32), pltpu.VMEM((128, 128), F32), pltpu.VMEM((128, 128), F32),
                        pltpu.VMEM((CHUNK, D_MODEL), F32)] + SWAP_SEMS,
        compiler_params=_cp(("arbitrary",)),
    )(u, u, u, proj, proj, y, prev, dycat, dtb, alog, dsk, nw,
      cst["tril"], cst["triu"], cst["expand"], cst["hsum"], gpack)


def _head_rms(v, hsum, expand):
    ms = _dotx_r(v * v, hsum, n=2) * (1.0 / HEAD_DIM)
    return _dotx_r(lax.rsqrt(ms + EPS), expand, n=2)


def _qk_fwd(proj, qw, kw, cst, name):
    t = proj.shape[0]
    tm = min(t, 256)
    scale = HEAD_DIM ** -0.5

    def body(q_ref, k_ref, v_ref, qw_ref, kw_ref, hs_ref, exp_ref, qs_ref, kn_ref, vb_ref):
        hsum, expand = hs_ref[...], exp_ref[...]
        q, k = q_ref[...], k_ref[...]
        qs_ref[...] = (q * _head_rms(q, hsum, expand) * qw_ref[...] * scale).astype(BF16)
        kn_ref[...] = (k * _head_rms(k, hsum, expand) * kw_ref[...]).astype(BF16)
        vb_ref[...] = v_ref[...].astype(BF16)

    return pl.pallas_call(
        body, name=name, grid=(t // tm,),
        in_specs=[_row_spec(tm, col=1), _row_spec(tm, col=2), _row_spec(tm, col=3),
                  _fix_spec((1, D_MODEL)), _fix_spec((1, D_MODEL)), _fix_spec((D_MODEL, 128)),
                  _fix_spec((128, D_MODEL))],
        out_specs=[_row_spec(tm)] * 3, out_shape=[jax.ShapeDtypeStruct((t, D_MODEL), BF16)] * 3,
        compiler_params=_cp(("parallel",)),
    )(proj, proj, proj, qw, kw, cst["hsum"], cst["expand"])


def _qk_bwd(proj, dqs, dkn, dv, qw, kw, cst, name):
    t = proj.shape[0]
    tm = min(t, 256)
    scale = HEAD_DIM ** -0.5

    def body(q_ref, k_ref, dq_ref, dk_ref, dv_ref, qw_ref, kw_ref, hs_ref, exp_ref, fold_ref,
             oq_ref, ok_ref, ov_ref, dw_ref):
        i = pl.program_id(0)
        hsum, expand = hs_ref[...], exp_ref[...]
        rows8 = lax.broadcasted_iota(jnp.int32, (8, D_MODEL), 0)
        sums = jnp.zeros((8, D_MODEL), F32)
        for n, (x_ref, d_ref, w_ref, o_ref, sc) in enumerate(
                [(q_ref, dq_ref, qw_ref, oq_ref, scale), (k_ref, dk_ref, kw_ref, ok_ref, 1.0)]):
            xv = x_ref[...]
            r = _head_rms(xv, hsum, expand)
            xhat = xv * r
            dn = d_ref[...] * sc
            sums = sums + jnp.where(rows8 == n, _colsum(dn * xhat), 0.0)
            dw = dn * w_ref[...]
            mean = _dotx_r(_dotx_r(dw * xhat, hsum, n=1), expand, n=2) * (1.0 / HEAD_DIM)
            o_ref[...] = (r * (dw - xhat * mean)).astype(BF16)
        ov_ref[...] = dv_ref[...].astype(BF16)
        folded = _dotx_r(sums, fold_ref[...])

        @pl.when(i == 0)
        def _():
            dw_ref[...] = folded

        @pl.when(i > 0)
        def _():
            dw_ref[...] += folded

    return pl.pallas_call(
        body, name=name, grid=(t // tm,),
        in_specs=[_row_spec(tm, col=1), _row_spec(tm, col=2), _row_spec(tm), _row_spec(tm), _row_spec(tm),
                  _fix_spec((1, D_MODEL)), _fix_spec((1, D_MODEL)), _fix_spec((D_MODEL, 128)),
                  _fix_spec((128, D_MODEL)), _fix_spec((D_MODEL, 128))],
        out_specs=[_row_spec(tm)] * 3 + [_fix_spec((8, 128))],
        out_shape=[jax.ShapeDtypeStruct((t, D_MODEL), BF16)] * 3 + [jax.ShapeDtypeStruct((8, 128), F32)],
        compiler_params=_cp(("arbitrary",)),
    )(proj, proj, dqs, dkn, dv, qw, kw, cst["hsum"], cst["expand"], cst["fold"])


def _sb_masks(i, kb, tq, tk):
    tpos = i * tq + lax.broadcasted_iota(jnp.int32, (tq, 1), 0)
    spos = kb * tk + lax.broadcasted_iota(jnp.int32, (1, tk), 1)
    return spos < tpos


def _grid_marks(n0, n1):
    j, i = pl.program_id(0), pl.program_id(1)
    return (jnp.logical_and(j == 0, i == 0), jnp.logical_and(j == n0 // 2, i == 0),
            jnp.logical_and(j == n0 - 1, i == n1 - 1))


def _sb_fwd(qs, kn, vb, pack, ycat, cst, name):
    t = qs.shape[0]
    tq = tk = min(t, SB_TILE)
    nq = t // tq
    ngrp = D_MODEL // SB_LANES
    nh = 2 * SB_PAIRS
    lanes = lambda p: slice(p * 128, (p + 1) * 128)

    def body(q_ref, k_ref, v_ref, u_ref, p_ref, yc_ref, rt_ref, ob_ref, cnt_ref, gat_ref, acc, rs, gss, grs):
        del yc_ref
        at_first, at_mid, at_last = _grid_marks(ngrp, nq)
        g_start, g_relay, g_finish = _gather_stages(p_ref, gat_ref, gss, grs)
        pl.when(at_first)(g_start)
        pl.when(at_mid)(g_relay)
        i = pl.program_id(1)
        lane = lax.broadcasted_iota(jnp.int32, (1, 128), 1)
        qh = []
        for p in range(SB_PAIRS):
            q2 = q_ref[:, lanes(p)]
            zero = jnp.zeros_like(q2)
            qh += [jnp.where(lane < HEAD_DIM, q2, zero), jnp.where(lane >= HEAD_DIM, q2, zero)]
        acc[...] = jnp.zeros_like(acc)
        rs[...] = jnp.zeros_like(rs)
        ustrict = u_ref[...]

        def tile(kb, masked):
            off = pl.multiple_of(kb * tk, tk)
            k2 = [k_ref[pl.ds(off, tk), lanes(p)] for p in range(SB_PAIRS)]
            v2 = [v_ref[pl.ds(off, tk), lanes(p)] for p in range(SB_PAIRS)]
            strict = _sb_masks(i, kb, tq, tk) if masked else None
            s = [_dot(qh[h], k2[h // 2], "nt") for h in range(nh)]
            a, r, lb = [None] * nh, [None] * nh, [None] * nh
            for h in range(nh):
                sp = _softplus(s[h])
                a[h] = s[h] - sp
                r[h] = jnp.where(strict, -sp, 0.0) if masked else -sp
                lb[h] = _dot2(r[h], ustrict)
            for h in range(nh):
                lw = a[h] + lb[h] + rs[h]
                w = jnp.exp(jnp.where(strict, lw, -1e30) if masked else lw)
                rs[h] = rs[h] + jnp.sum(r[h], axis=1, keepdims=True)
                acc[h] = acc[h] + _dot(w.astype(BF16), v2[h // 2])

        tile(i, True)

        def live():
            return jnp.max(functools.reduce(jnp.maximum, [rs[h] for h in range(nh)]))

        def more(c):
            return jnp.logical_and(c[0] < i, c[1] > SB_DEAD)

        def step(c):
            tile(i - 1 - c[0], False)
            return c[0] + 1, live()

        n_off, _ = lax.while_loop(more, step, (jnp.int32(0), live()))
        cnt_ref[pl.program_id(0), i] = n_off.astype(F32)
        for p in range(SB_PAIRS):
            rt_ref[:, lanes(p)] = jnp.where(lane < HEAD_DIM, rs[2 * p], rs[2 * p + 1])
            ob_ref[:, lanes(p)] = jnp.where(lane < HEAD_DIM, acc[2 * p], acc[2 * p + 1]).astype(BF16)
        pl.when(at_last)(g_finish)

    blk = lambda rows, imap: pl.BlockSpec((rows, SB_LANES), imap)
    return pl.pallas_call(
        body, name=name, grid=(ngrp, nq),
        in_specs=[blk(tq, lambda j, i: (i, j)), blk(t, lambda j, i: (0, j)), blk(t, lambda j, i: (0, j)),
                  _fix_spec((2 * tk, tk)), HB, pl.BlockSpec(memory_space=pl.ANY)],
        out_specs=[blk(tq, lambda j, i: (i, j)), blk(tq, lambda j, i: (i, ngrp + j)),
                   pl.BlockSpec(memory_space=pltpu.SMEM), HB],
        out_shape=[jax.ShapeDtypeStruct((t, D_MODEL), F32), jax.ShapeDtypeStruct(ycat.shape, ycat.dtype),
                   jax.ShapeDtypeStruct((ngrp, nq), F32),
                   jax.ShapeDtypeStruct((N_SHARDS,) + pack.shape, pack.dtype)],
        scratch_shapes=[pltpu.VMEM((nh, tq, 128), F32), pltpu.VMEM((nh, tq, 1), F32)] + GATHER_SEMS,
        input_output_aliases={5: 1},
        compiler_params=_cp(("arbitrary", "arbitrary")),
    )(qs, kn, vb, _doubled(cst["ustrict"], tk), pack, ycat)


def _sb_bwd(qs, kn, vb, rtot, cnt, dycat, csum_b, cst, name):
    t = qs.shape[0]
    tq = tk = min(t, SB_TILE)
    nq = t // tq
    ngrp = D_MODEL // SB_LANES
    nh = 2 * SB_PAIRS
    lanes = lambda p: slice(p * 128, (p + 1) * 128)

    def body(q_ref, k_ref, v_ref, rt_ref, do_ref, us_ref, ui_ref, cnt_ref, xs_ref, dq_ref, dk_ref, dv_ref, xr_ref,
             acc, rs, es, xss, xrs):
        at_first, _, at_last = _grid_marks(ngrp, nq)
        x_start, x_finish = _exchange_stages(xs_ref, xr_ref, xss, xrs)
        pl.when(at_first)(x_start)
        i = pl.program_id(1)
        lane = lax.broadcasted_iota(jnp.int32, (1, 128), 1)
        keep = [lane < HEAD_DIM, lane >= HEAD_DIM]
        qh, doh, rtot_h = [], [], []
        for p in range(SB_PAIRS):
            q2, rt = q_ref[:, lanes(p)], rt_ref[:, lanes(p)]
            do2b = do_ref[:, lanes(p)].astype(BF16)
            qh += [jnp.where(kp, q2, jnp.zeros_like(q2)) for kp in keep]
            doh += [jnp.where(kp, do2b, jnp.zeros_like(do2b)) for kp in keep]
            rtot_h += [jnp.sum(jnp.where(lane == n * HEAD_DIM, rt, 0.0), axis=1, keepdims=True) for n in range(2)]
        acc[...] = jnp.zeros_like(acc)
        rs[...] = jnp.zeros_like(rs)
        es[...] = jnp.zeros_like(es)

        @pl.when(i == 0)
        def _():
            dk_ref[...] = jnp.zeros_like(dk_ref)
            dv_ref[...] = jnp.zeros_like(dv_ref)

        ule, ult = us_ref[...], ui_ref[...]

        def tile(kb, masked):
            off = pl.multiple_of(kb * tk, tk)
            k2 = [k_ref[pl.ds(off, tk), lanes(p)] for p in range(SB_PAIRS)]
            v2 = [v_ref[pl.ds(off, tk), lanes(p)] for p in range(SB_PAIRS)]
            strict = _sb_masks(i, kb, tq, tk) if masked else None
            s = [_dot(qh[h], k2[h // 2], "nt") for h in range(nh)]
            dw = [_dot(doh[h], v2[h // 2], "nt") for h in range(nh)]
            a, sg, r, pin, w, e, cin = ([None] * nh for _ in range(7))
            for h in range(nh):
                sp = _softplus(s[h])
                a[h] = s[h] - sp
                sg[h] = jnp.exp(a[h])
                r[h] = jnp.where(strict, -sp, 0.0) if masked else -sp
                pin[h] = _dot2(r[h], ule)
            for h in range(nh):
                lw = a[h] + ((rtot_h[h] - rs[h]) - pin[h])
                w[h] = jnp.exp(jnp.where(strict, lw, -1e30) if masked else lw)
                e[h] = w[h] * dw[h]
                cin[h] = _dot2(e[h], ult)
            for p in range(SB_PAIRS):
                dk_t = jnp.zeros((tk, 128), F32)
                dv_t = jnp.zeros((tk, 128), F32)
                for h in (2 * p, 2 * p + 1):
                    dl = e[h] * (1.0 - sg[h]) - (es[h] + cin[h]) * sg[h]
                    dl = (jnp.where(strict, dl, 0.0) if masked else dl).astype(BF16)
                    rs[h] = rs[h] + jnp.sum(r[h], axis=1, keepdims=True)
                    es[h] = es[h] + jnp.sum(e[h], axis=1, keepdims=True)
                    acc[h] = acc[h] + _dot(dl, k2[p])
                    dk_t = dk_t + _dot(dl, qh[h], "tn")
                    dv_t = dv_t + _dot(w[h].astype(BF16), doh[h], "tn")
                dk_ref[pl.ds(off, tk), lanes(p)] += dk_t
                dv_ref[pl.ds(off, tk), lanes(p)] += dv_t

        def step(kb, carry):
            tile(kb, False)
            return carry

        n_off = cnt_ref[pl.program_id(0), i].astype(jnp.int32)
        lax.fori_loop(i - n_off, i, step, 0)
        tile(i, True)
        for p in range(SB_PAIRS):
            dq_ref[:, lanes(p)] = jnp.where(lane < HEAD_DIM, acc[2 * p], acc[2 * p + 1])
        pl.when(at_last)(x_finish)

    blk = lambda rows, imap: pl.BlockSpec((rows, SB_LANES), imap)
    return pl.pallas_call(
        body, name=name, grid=(ngrp, nq),
        in_specs=[blk(tq, lambda j, i: (i, j)), blk(t, lambda j, i: (0, j)), blk(t, lambda j, i: (0, j)),
                  blk(tq, lambda j, i: (i, j)), blk(tq, lambda j, i: (i, ngrp + j)),
                  _fix_spec((2 * tk, tk)), _fix_spec((2 * tk, tk)), pl.BlockSpec(memory_space=pltpu.SMEM), HB],
        out_specs=[blk(tq, lambda j, i: (i, j)), blk(t, lambda j, i: (0, j)), blk(t, lambda j, i: (0, j)), HB],
        out_shape=[jax.ShapeDtypeStruct((t, D_MODEL), F32)] * 3
        + [jax.ShapeDtypeStruct((3,) + csum_b.shape[1:], csum_b.dtype)],
        scratch_shapes=[pltpu.VMEM((nh, tq, 128), F32), pltpu.VMEM((nh, tq, 1), F32), pltpu.VMEM((nh, tq, 1), F32)]
        + EXCHANGE_SEMS,
        compiler_params=_cp(("arbitrary", "arbitrary")),
    )(qs, kn, vb, rtot, dycat, _doubled(cst["ule"], tk), _doubled(cst["ult"], tk), cnt, csum_b)


def _adamw(w, g, m, v, name):
    lead = (1,) * (w.ndim - 2)
    rows, cols = w.shape[-2:]
    fits = [d for d in range(8, rows, 8) if rows % d == 0 and d * cols * 4 <= ADAM_BLOCK_BYTES]
    tr = max(fits) if fits else rows
    c1 = 1.0 - ADAM_B1 ** ADAM_STEP
    c2 = 1.0 - ADAM_B2 ** ADAM_STEP

    def body(w_ref, g_ref, m_ref, v_ref, d_ref, nm_ref, nv_ref):
        gv = g_ref[...]
        nm = ADAM_B1 * m_ref[...] + (1.0 - ADAM_B1) * gv
        nv = ADAM_B2 * v_ref[...] + (1.0 - ADAM_B2) * (gv * gv)
        nm_ref[...] = nm
        nv_ref[...] = nv
        d_ref[...] = -ADAM_LR * ((nm / c1) / (jnp.sqrt(nv / c2) + ADAM_EPS) + ADAM_WD * w_ref[...])

    spec = pl.BlockSpec(lead + (tr, cols), lambda i: (0,) * len(lead) + (i, 0))
    return pl.pallas_call(
        body, name=name, grid=(rows // tr,), in_specs=[spec] * 4, out_specs=[spec] * 3,
        out_shape=[jax.ShapeDtypeStruct(w.shape, F32)] * 3, compiler_params=_cp(("parallel",)),
    )(w, g, m, v)


def _place():
    x, y, c = lax.axis_index("x"), lax.axis_index("y"), lax.axis_index("c")
    chips = [(1 - x, y), (x, 1 - y), (1 - x, 1 - y)]
    return x, y, c, chips


VM = pl.BlockSpec(memory_space=pltpu.VMEM)
HB = pl.BlockSpec(memory_space=pltpu.HBM)


def _gather_all(p_ref, gat_ref, ss, rs):
    x, y, c, _ = _place()
    me = 4 * x + 2 * y + c
    peers = [(x, y, 1 - c), (1 - x, y, c), (x, 1 - y, c), (1 - x, 1 - y, c),
             (1 - x, y, 1 - c), (x, 1 - y, 1 - c), (1 - x, 1 - y, 1 - c)]

    def copy(k, slot, to):
        return pltpu.make_async_remote_copy(src_ref=p_ref, dst_ref=gat_ref.at[slot], send_sem=ss.at[k],
                                            recv_sem=rs.at[k], device_id=to, device_id_type=MESH)

    sends = [copy(k, me, p) for k, p in enumerate(peers)]
    for s in sends:
        s.start()
    gat_ref[me] = p_ref[...]
    for k, p in enumerate(peers):
        copy(k, 4 * p[0] + 2 * p[1] + p[2], p).wait_recv()
    for s in sends:
        s.wait_send()


ALL_SEMS = [pltpu.SemaphoreType.DMA((7,)), pltpu.SemaphoreType.DMA((7,))]


def _small_reduce(pack, name):
    rows = pack.shape[0]

    def body(p_ref, gat_ref, sum_ref, ss, rs):
        _gather_all(p_ref, gat_ref, ss, rs)
        tot = gat_ref[0]
        for b in range(1, 8):
            tot = tot + gat_ref[b]
        sum_ref[...] = tot

    return pl.pallas_call(
        body, name=name, in_specs=[VM], out_specs=[VM, VM],
        out_shape=[jax.ShapeDtypeStruct((8, rows, D_MODEL), F32), jax.ShapeDtypeStruct((rows, D_MODEL), F32)],
        scratch_shapes=ALL_SEMS, compiler_params=_cp(),
    )(pack)


def _prologue(cpack, w_ada, b_shard, wpack_in, name):
    def body(cp_ref, w_ref, b_ref, p_ref, gat_ref, modp_ref, gin_ref, ss1, rs1, ss2, rs2, gss, grs):
        g_start, g_relay, g_finish = _gather_stages(p_ref, gin_ref, gss, grs)
        g_start()
        _gather_all(cp_ref, gat_ref, ss1, rs1)
        x, y, c, chips = _place()
        sh = 2 * x + y
        row = lax.broadcasted_iota(jnp.int32, (8, D_MODEL), 0)
        cv = jnp.zeros((8, D_MODEL), F32)
        for b in range(8):
            cv = jnp.where(row == b, gat_ref[b, 0:8, :], cv)
        cv = cv * _sig(cv)
        modp_ref[sh] = jnp.dot(cv, w_ref[...], precision=lax.Precision.HIGHEST,
                               preferred_element_type=F32) + b_ref[...]

        def copy(k, slot, to):
            return pltpu.make_async_remote_copy(src_ref=modp_ref.at[slot], dst_ref=modp_ref.at[slot],
                                                send_sem=ss2.at[k], recv_sem=rs2.at[k], device_id=to,
                                                device_id_type=MESH)

        sends = [copy(k, sh, (*ch, c)) for k, ch in enumerate(chips)]
        for s in sends:
            s.start()
        for k, ch in enumerate(chips):
            copy(k, 2 * ch[0] + ch[1], (*ch, c)).wait_recv()
        for s in sends:
            s.wait_send()
        g_relay()
        g_finish()

    return pl.pallas_call(
        body, name=name, in_specs=[VM, VM, VM, HB], out_specs=[VM, VM, HB],
        out_shape=[jax.ShapeDtypeStruct((8,) + cpack.shape, F32),
                   jax.ShapeDtypeStruct((N_SHARDS, 8, 6 * D_MODEL // N_SHARDS), F32),
                   jax.ShapeDtypeStruct((N_SHARDS,) + wpack_in.shape, wpack_in.dtype)],
        scratch_shapes=ALL_SEMS + EXCHANGE_SEMS + GATHER_SEMS, compiler_params=_cp(),
    )(cpack, w_ada, b_shard, wpack_in)


def _gather_stages(p_ref, out_ref, ss, rs):
    hf = p_ref.shape[0] // 2
    x, y, c, chips = _place()
    sh = 2 * x + y
    sib = (x, y, 1 - c)
    slots = [2 * ch[0] + ch[1] for ch in chips]

    def half(slot, hc):
        return out_ref.at[slot, pl.ds(hc * hf, hf), :]

    def copy(k, src, slot, hc, to):
        return pltpu.make_async_remote_copy(src_ref=src, dst_ref=half(slot, hc), send_sem=ss.at[k],
                                            recv_sem=rs.at[k], device_id=to, device_id_type=MESH)

    def first():
        return [copy(j, p_ref.at[pl.ds(c * hf, hf), :], sh, c, (*ch, c)) for j, ch in enumerate(chips)]

    def passed():
        return [copy(3 + j, half(slots[j], c), slots[j], c, sib) for j in range(3)]

    def own():
        return pltpu.make_async_remote_copy(src_ref=p_ref, dst_ref=out_ref.at[sh], send_sem=ss.at[6],
                                            recv_sem=rs.at[6], device_id=sib, device_id_type=MESH)

    def start():
        for cp in first() + [own()]:
            cp.start()

    def relay():
        for j, cp in enumerate(passed()):
            copy(j, half(slots[j], c), slots[j], c, (*chips[j], c)).wait_recv()
            cp.start()

    def finish():
        for j in range(3):
            copy(3 + j, half(slots[j], 1 - c), slots[j], 1 - c, sib).wait_recv()
        own().wait()
        for cp in first() + passed():
            cp.wait_send()

    return start, relay, finish


GATHER_SEMS = [pltpu.SemaphoreType.DMA((7,)), pltpu.SemaphoreType.DMA((7,))]


def _swap_stages(g_ref, out_ref, ss, rs):
    hf = g_ref.shape[1] // 2
    x, y, c, _ = _place()

    def copy():
        return pltpu.make_async_remote_copy(
            src_ref=g_ref.at[pl.ds(0, N_SHARDS), pl.ds((1 - c) * hf, hf), :], dst_ref=out_ref,
            send_sem=ss, recv_sem=rs, device_id=(x, y, 1 - c), device_id_type=MESH)

    return (lambda: copy().start()), (lambda: copy().wait())


SWAP_SEMS = [pltpu.SemaphoreType.DMA, pltpu.SemaphoreType.DMA]


def _swap_shape(g):
    return jax.ShapeDtypeStruct((N_SHARDS, g.shape[1] // 2, D_MODEL), g.dtype)


def _sibling_swap(g, name):
    def body(g_ref, out_ref, ss, rs):
        for stage in _swap_stages(g_ref, out_ref, ss, rs):
            stage()

    return pl.pallas_call(
        body, name=name, in_specs=[HB], out_specs=HB, out_shape=_swap_shape(g),
        scratch_shapes=SWAP_SEMS, compiler_params=_cp(),
    )(g)


def _row_tile(rows, width_bytes, cap_bytes):
    fits = [d for d in range(8, rows + 1, 8) if rows % d == 0 and d * width_bytes <= cap_bytes]
    return max(fits)


def _chip_sum(g, got, c_idx, name):
    hf = got.shape[1]
    tr = _row_tile(hf, D_MODEL * 4, 3 << 20)
    nb = hf // tr

    def body(c_ref, a_ref, b_ref, s_ref, sb_ref):
        s = a_ref[...] + b_ref[...]
        s_ref[...] = s
        sb_ref[...] = s.astype(BF16)

    blk = pl.BlockSpec((1, tr, D_MODEL), lambda s, i, c_ref: (s, i, 0))
    return pl.pallas_call(
        body, name=name,
        grid_spec=pltpu.PrefetchScalarGridSpec(
            num_scalar_prefetch=1, grid=(N_SHARDS, nb),
            in_specs=[pl.BlockSpec((1, tr, D_MODEL), lambda s, i, c_ref: (s, c_ref[0] * nb + i, 0)), blk],
            out_specs=[blk, blk]),
        out_shape=[jax.ShapeDtypeStruct((N_SHARDS, hf, D_MODEL), F32),
                   jax.ShapeDtypeStruct((N_SHARDS, hf, D_MODEL), BF16)],
        compiler_params=_cp(("parallel", "parallel")),
    )(c_idx, g, got)


def _exchange_stages(s_ref, out_ref, ss, rs):
    x, y, c, chips = _place()

    def sends():
        return [pltpu.make_async_remote_copy(src_ref=s_ref.at[2 * ch[0] + ch[1]], dst_ref=out_ref.at[k],
                                             send_sem=ss.at[k], recv_sem=rs.at[k], device_id=(*ch, c),
                                             device_id_type=MESH) for k, ch in enumerate(chips)]

    def start():
        for cp in sends():
            cp.start()

    def finish():
        for cp in sends():
            cp.wait()

    return start, finish


EXCHANGE_SEMS = [pltpu.SemaphoreType.DMA((3,)), pltpu.SemaphoreType.DMA((3,))]


def _total_half(s, got, sh_idx, name):
    hf = got.shape[1]
    tr = _row_tile(hf, D_MODEL * 4, 3 << 20)
    nb = hf // tr

    def body(sh_ref, a_ref, r0, r1, r2, o_ref):
        o_ref[...] = ((a_ref[0] + r0[0].astype(F32)) + r1[0].astype(F32)) + r2[0].astype(F32)

    rspec = lambda k: pl.BlockSpec((1, tr, D_MODEL), lambda i, sh_ref: (k, i, 0))
    return pl.pallas_call(
        body, name=name,
        grid_spec=pltpu.PrefetchScalarGridSpec(
            num_scalar_prefetch=1, grid=(nb,),
            in_specs=[pl.BlockSpec((1, tr, D_MODEL), lambda i, sh_ref: (sh_ref[0], i, 0)),
                      rspec(0), rspec(1), rspec(2)],
            out_specs=pl.BlockSpec((tr, D_MODEL), lambda i, sh_ref: (i, 0))),
        out_shape=jax.ShapeDtypeStruct((hf, D_MODEL), F32),
        compiler_params=_cp(("parallel",)),
    )(sh_idx, s, got, got, got)


def _join_halves(tot, name):
    def body(t_ref, out_ref, ss, rs):
        x, y, c, _ = _place()
        cp = pltpu.make_async_remote_copy(src_ref=t_ref, dst_ref=out_ref, send_sem=ss, recv_sem=rs,
                                          device_id=(x, y, 1 - c), device_id_type=MESH)
        cp.start()
        cp.wait()

    return pl.pallas_call(
        body, name=name, in_specs=[HB], out_specs=HB,
        out_shape=jax.ShapeDtypeStruct(tot.shape, F32),
        scratch_shapes=[pltpu.SemaphoreType.DMA, pltpu.SemaphoreType.DMA],
        compiler_params=_cp(),
    )(tot)


def _w_ada_grad(cond, dmod_cols, name):
    def body(c_ref, d_ref, o_ref):
        cv = c_ref[...]
        cv = cv * _sig(cv)
        o_ref[...] = lax.dot_general(cv, d_ref[...], _DN["tn"], precision=lax.Precision.HIGHEST,
                                     preferred_element_type=F32)

    return pl.pallas_call(
        body, name=name, in_specs=[VM, VM], out_specs=VM,
        out_shape=jax.ShapeDtypeStruct((D_MODEL, dmod_cols.shape[1]), F32), compiler_params=_cp(),
    )(cond, dmod_cols)


def _pad_rows(a, rows):
    return jnp.pad(a, ((0, rows - a.shape[0]), (0, 0)))


def _pad_cols(a, cols):
    return jnp.pad(a, ((0, 0), (0, cols - a.shape[1])))


def _unpack_rest(p):
    o = 0
    out = []
    for r in (R_OUT, R_FF, R_FF, R_FF):
        out.append(p[..., o:o + r, :])
        o += r
    return out


def _reduce_tail(csum, got2, shard, ac, tag):
    tot = _total_half(csum, got2, shard.reshape(1).astype(jnp.int32), "rs_total_" + tag)
    other = _join_halves(tot, "rs_join_" + tag)
    return jnp.where(ac == 0, jnp.concatenate([tot, other], axis=0), jnp.concatenate([other, tot], axis=0))


def _reduce_head(gpack, ac, tag):
    got = _sibling_swap(gpack, "rs_sibling_swap_" + tag)
    return _chip_sum(gpack, got, ac.reshape(1).astype(jnp.int32), "rs_chip_sum_" + tag)


def kernel(x, c, w_ada, b_ada, norm1_w, w_in, conv_w, conv_b, dt_bias, a_log, d_skip, ssd_norm_w, q_norm_w, k_norm_w, w_out, norm2_w, w_gate, w_up, w_down, loss_target, m_w_ada, m_b_ada, m_norm1_w, m_w_in, m_conv_w, m_conv_b, m_dt_bias, m_a_log, m_d_skip, m_ssd_norm_w, m_q_norm_w, m_k_norm_w, m_w_out, m_norm2_w, m_w_gate, m_w_up, m_w_down, v_w_ada, v_b_ada, v_norm1_w, v_w_in, v_conv_w, v_conv_b, v_dt_bias, v_a_log, v_d_skip, v_ssd_norm_w, v_q_norm_w, v_k_norm_w, v_w_out, v_norm2_w, v_w_gate, v_w_up, v_w_down):
    cst = _consts()
    ax, ay, ac = lax.axis_index("x"), lax.axis_index("y"), lax.axis_index("c")
    shard = 2 * ax + ay
    me = 4 * ax + 2 * ay + ac
    xs = x[0]
    tgt = loss_target[0]
    w_in_cols = w_in.shape[2]
    conv_cols = conv_w.shape[2]

    tr3 = lambda a: jnp.transpose(a, (0, 2, 1))
    lin = lambda a: tr3(a).reshape(-1, 128)
    unlin = lambda a: tr3(a.reshape(1, w_in_cols, D_MODEL))
    wpack_in = _pad_rows(tr3(w_in.astype(BF16))[0], R_IN)
    wpack_rest = jnp.concatenate([w_out[0], tr3(w_gate)[0], tr3(w_up)[0], w_down[0]], axis=0).astype(BF16)

    cw_flat = _pad_cols(conv_w[0].reshape(1, -1), 2 * D_MODEL).reshape(2, D_MODEL)
    cpack = jnp.concatenate([jnp.broadcast_to(c, (8, D_MODEL)), _pad_rows(cw_flat, 8)], axis=0)
    mod_w = 6 * D_MODEL // N_SHARDS
    b_shard = lax.dynamic_slice(b_ada, (0, shard * mod_w), (1, mod_w))
    gat, modp, gp_in = _prologue(cpack, w_ada[0], b_shard, wpack_in, "prologue")
    c_all = gat[:, 0, :]
    cw = gat[0::2, 8:10, :].reshape(N_SHARDS, 2 * D_MODEL)[:, :4 * conv_cols].reshape(N_SHARDS, 4, conv_cols)
    conv_w_full = jnp.transpose(cw, (1, 0, 2)).reshape(4, D_CONV)
    mod_mine = lax.dynamic_slice(modp, (0, me, 0), (N_SHARDS, 1, mod_w)).reshape(6, D_MODEL)
    mod = _pad_rows(mod_mine, 8)
    wi_t = gp_in[:, :w_in_cols, :].reshape(D_IN_PROJ, D_MODEL)
    w_inp_t = jnp.concatenate([wi_t[0:1024], wi_t[2576:5648], wi_t[1024:2560], wi_t[2560:2576],
                               jnp.zeros((D_PROJ_PAD - D_IN_PROJ, D_MODEL), BF16)], axis=0)

    pad128 = lambda a: _pad_cols(a, 128)
    dtb, alog = pad128(dt_bias), pad128(a_log)
    dsk = jnp.repeat(d_skip, HEAD_DIM, axis=1)
    qw, kw = jnp.tile(q_norm_w, (1, N_HEADS)), jnp.tile(k_norm_w, (1, N_HEADS))

    h1 = _norm_mod(xs, norm1_w, mod, 0, "norm1")
    proj = _matmul(h1, w_inp_t, "nt", F32, "in_proj")
    u = _conv_fwd(proj, conv_w_full, conv_b, "conv_fwd")
    y_ssd, yn, prev = _ssd_fwd(u, proj, dtb, alog, dsk, ssd_norm_w, cst, "ssd_fwd")
    qs, kn, vb = _qk_fwd(proj, qw, kw, cst, "qk_norm")
    rtot, ycat, cnt, gp_rest = _sb_fwd(qs, kn, vb, wpack_rest, yn, cst, "sb_fwd")
    p_out, p_gate, p_up, p_down = _unpack_rest(gp_rest)
    w_o = p_out.reshape(2 * D_MODEL, D_MODEL)
    w_gu_t = jnp.concatenate([p_gate.reshape(D_FF, D_MODEL), p_up.reshape(D_FF, D_MODEL)], axis=0)
    w_d = p_down.reshape(D_FF, D_MODEL)
    mix = _matmul(ycat, w_o, "nn", F32, "out_proj")
    x1, h2 = _resid_norm(xs, mix, norm2_w, mod, "resid_norm2")
    gu = _matmul(h2, w_gu_t, "nt", BF16, "ffn_in")
    act = _act_fwd(gu, "ffn_act")
    ffn = _matmul(act, w_d, "nn", F32, "ffn_out", tk_cap=1408)
    dffn, dout, dg2, loss8 = _loss_head(x1, ffn, tgt, mod, "loss_head")
    loss = lax.psum(loss8[0, 0], ("x", "y", "c"))

    dact = _matmul(dffn, w_d, "nt", BF16, "d_act")
    g_down = _matmul(act, dffn, "tn", F32, "g_w_down", tm_cap=1408)
    dgu = _act_bwd(dact, gu, "ffn_act_bwd")
    dh2 = _matmul(dgu, w_gu_t, "nn", F32, "d_h2", tk_cap=1408)
    g_gu_t = _matmul(dgu, h2, "tn", F32, "g_w_gu", tm_cap=1408)
    dx1, dmix, acc2 = _norm_bwd(dh2, x1, dout, mix, norm2_w, mod, 3, 2, "norm2_bwd")
    dycat = _matmul(dmix, w_o, "nt", F32, "d_ycat")
    g_out = _matmul(ycat, dmix, "tn", F32, "g_w_out")
    gpack_rest = jnp.concatenate([
        g_out.reshape(N_SHARDS, R_OUT, D_MODEL),
        g_gu_t[:D_FF].reshape(N_SHARDS, R_FF, D_MODEL), g_gu_t[D_FF:].reshape(N_SHARDS, R_FF, D_MODEL),
        g_down.reshape(N_SHARDS, R_FF, D_MODEL)], axis=1)
    du, ddt, dz, acc_ssd, acc16, got_s = _ssd_bwd(u, proj, y_ssd, prev, dycat, dtb, alog, dsk, ssd_norm_w,
                                                  gpack_rest, cst, "ssd_bwd")
    csum_r, csum_rb = _chip_sum(gpack_rest, got_s, ac.reshape(1).astype(jnp.int32), "rs_chip_sum_rest")
    dqs, dkn, dv, got_r = _sb_bwd(qs, kn, vb, rtot, cnt, dycat, csum_rb, cst, "sb_bwd")
    r_out, r_gate, r_up, r_down = _unpack_rest(_reduce_tail(csum_r, got_r, shard, ac, "rest"))
    dq, dk, dvb, acc_qk = _qk_bwd(proj, dqs, dkn, dv, qw, kw, cst, "qk_norm_bwd")
    dxbc, g_conv_w, g_conv_b = _conv_bwd(proj, du, conv_w_full, conv_b, "conv_bwd")
    dproj = jnp.concatenate([dz, dq, dk, dvb, dxbc, ddt], axis=1)
    g_inp_t = _matmul(dproj, h1, "tn", F32, "g_w_in", tm_cap=1920)
    gi_t = jnp.concatenate([g_inp_t[0:1024], g_inp_t[4096:5632], g_inp_t[5632:5648], g_inp_t[1024:4096]], axis=0)
    gpack_in = jnp.pad(gi_t.reshape(N_SHARDS, w_in_cols, D_MODEL), ((0, 0), (0, R_IN - w_in_cols), (0, 0)))
    csum_i, csum_ib = _reduce_head(gpack_in, ac, "in")
    dh1, got_i = _matmul(dproj, w_inp_t, "nn", F32, "d_h1", tk_cap=1152, exchange=csum_ib)
    r_in = _reduce_tail(csum_i, got_i, shard, ac, "in")
    grad_x, acc1 = _norm_bwd(dh1, xs, dx1, None, norm1_w, mod, 0, None, "norm1_bwd")

    last = jnp.concatenate([acc_qk[0:1, 0:64], acc_qk[1:2, 0:64], acc16[0:1, 0:16], acc16[1:2, 0:16],
                            acc16[2:3, 0:16]], axis=1)
    spack = jnp.concatenate([
        acc1[0:2], acc2[3:4], acc2[0:2], dg2,
        acc1[2:3], acc2[2:3], acc_ssd[0:1],
        _pad_cols(g_conv_b, 2 * D_MODEL).reshape(2, D_MODEL),
        g_conv_w.reshape(6, D_MODEL),
        _pad_cols(last, D_MODEL)], axis=0)
    sgat, ssum = _small_reduce(_pad_rows(spack, SMALL_ROWS), "gather_small")
    g_b_ada = ssum[0:6].reshape(1, 6 * D_MODEL)
    g_norm1, g_norm2, g_ssdn = ssum[6:7], ssum[7:8], ssum[8:9]
    g_cb = ssum[9:11].reshape(1, 2 * D_MODEL)[:, :D_CONV]
    g_cw = lax.dynamic_slice(ssum[11:17].reshape(4, D_CONV), (0, shard * conv_cols), (4, conv_cols))
    g_qn, g_kn = ssum[17:18, 0:64], ssum[17:18, 64:128]
    g_dtb, g_alog, g_dsk = ssum[17:18, 128:144], ssum[17:18, 144:160], ssum[17:18, 160:176]
    dmod_all = sgat[:, 0:6, :].reshape(8, 6 * D_MODEL)
    g_w_ada = _w_ada_grad(c_all, lax.dynamic_slice(dmod_all, (0, shard * mod_w), (8, mod_w)), "g_w_ada")


    grads = dict(w_ada=g_w_ada, b_ada=g_b_ada, norm1_w=g_norm1, w_in=r_in[:w_in_cols].reshape(-1, 128), conv_w=g_cw,
                 conv_b=g_cb, dt_bias=g_dtb, a_log=g_alog, d_skip=g_dsk, ssd_norm_w=g_ssdn, q_norm_w=g_qn,
                 k_norm_w=g_kn, w_out=r_out, norm2_w=g_norm2, w_gate=r_gate, w_up=r_up, w_down=r_down)
    weights = dict(w_ada=(w_ada, m_w_ada, v_w_ada), b_ada=(b_ada, m_b_ada, v_b_ada),
                   norm1_w=(norm1_w, m_norm1_w, v_norm1_w), w_in=(w_in, m_w_in, v_w_in),
                   conv_w=(conv_w, m_conv_w, v_conv_w), conv_b=(conv_b, m_conv_b, v_conv_b),
                   dt_bias=(dt_bias, m_dt_bias, v_dt_bias), a_log=(a_log, m_a_log, v_a_log),
                   d_skip=(d_skip, m_d_skip, v_d_skip), ssd_norm_w=(ssd_norm_w, m_ssd_norm_w, v_ssd_norm_w),
                   q_norm_w=(q_norm_w, m_q_norm_w, v_q_norm_w), k_norm_w=(k_norm_w, m_k_norm_w, v_k_norm_w),
                   w_out=(w_out, m_w_out, v_w_out), norm2_w=(norm2_w, m_norm2_w, v_norm2_w),
                   w_gate=(w_gate, m_w_gate, v_w_gate), w_up=(w_up, m_w_up, v_w_up),
                   w_down=(w_down, m_w_down, v_w_down))
    views = dict(w_in=(lin, unlin), w_gate=(tr3, tr3), w_up=(tr3, tr3))
    same = lambda a: a
    names = list(weights)
    g_out_l, d_out_l, m_out_l, v_out_l = [], [], [], []
    for n in names:
        view, back = views.get(n, (same, same))
        w, m, v = (view(a) for a in weights[n])
        g = grads[n].reshape(w.shape)
        d, nm, nv = _adamw(w, g, m, v, "adamw_" + n)
        g_out_l.append(back(g))
        d_out_l.append(back(d))
        m_out_l.append(back(nm))
        v_out_l.append(back(nv))
    return (loss, grad_x[None], *g_out_l, *d_out_l, *m_out_l, *v_out_l)
```

```python
import functools

import numpy as np
import jax
import jax.numpy as jnp
from jax import lax
from jax.experimental import pallas as pl
from jax.experimental.pallas import tpu as pltpu

F32, BF16 = jnp.float32, jnp.bfloat16
MESH = pl.DeviceIdType.MESH

D_MODEL = 1024
HEAD_DIM = 64
N_HEADS = 16
D_CONV = 1536
D_FF = 2816
D_IN_PROJ = 5648
D_PROJ_PAD = 5760
CHUNK = 128
SB_TILE = 256
SB_PAIRS = 2
SB_LANES = 128 * SB_PAIRS
SB_DEAD = -105.0
EPS = 1e-6
N_SHARDS = 4
R_IN, R_OUT, R_FF = 1440, 512, 704
SMALL_ROWS = 24

ADAM_LR, ADAM_B1, ADAM_B2, ADAM_EPS, ADAM_WD, ADAM_STEP = 0.001, 0.9, 0.999, 1e-08, 0.01, 10

VMEM_LIMIT = 48 * 1024 * 1024
ADAM_BLOCK_BYTES = 3 * 512 * 1024

_DN = {"nn": (((1,), (0,)), ((), ())), "nt": (((1,), (1,)), ((), ())), "tn": (((0,), (0,)), ((), ()))}


def _dot(a, b, dims="nn"):
    return lax.dot_general(a, b, _DN[dims], preferred_element_type=F32)


def _pieces(x, n):
    out = []
    for _ in range(n - 1):
        hi = lax.bitcast_convert_type(lax.bitcast_convert_type(x, jnp.int32) & jnp.int32(-65536), F32)
        out.append(hi.astype(BF16))
        x = x - hi
    out.append(x.astype(BF16))
    return out


def _dotx_r(x, b_exact, n=3):
    return _dot(jnp.concatenate(_pieces(x, n), axis=1), jnp.concatenate([b_exact] * n, axis=0))


def _dotx_l(a_exact, x, n=3):
    return _dot(jnp.concatenate([a_exact] * n, axis=1), jnp.concatenate(_pieces(x, n), axis=0))


def _dot2(x, b2):
    return _dot(jnp.concatenate(_pieces(x, 2), axis=1), b2)


def _sig(x):
    return 1.0 / (1.0 + jnp.exp(-x))


def _softplus(x):
    return jnp.maximum(x, 0.0) + jnp.log(1.0 + jnp.exp(-jnp.abs(x)))


def _cp(sem=None, vmem=VMEM_LIMIT):
    return pltpu.CompilerParams(dimension_semantics=sem, vmem_limit_bytes=vmem)


def _colsum(x):
    return jnp.sum(x, axis=0, keepdims=True)


def _consts():
    ch = np.arange(D_MODEL)
    expand = (np.arange(128)[:, None] == (ch // HEAD_DIM)[None, :]).astype(np.float32)
    fold = (ch[:, None] % HEAD_DIM == np.arange(128)[None, :]).astype(np.float32)
    i = np.arange(CHUNK)
    tril = (i[:, None] >= i[None, :]).astype(np.float32)
    j = np.arange(SB_TILE)
    ustrict = (j[:, None] > j[None, :]).astype(np.float32)
    ule = (j[:, None] <= j[None, :]).astype(np.float32)
    ult = (j[:, None] < j[None, :]).astype(np.float32)
    c = lambda a: jnp.asarray(a, BF16)
    return dict(expand=c(expand), hsum=c(expand.T), fold=c(fold), tril=c(tril), triu=c(tril.T),
                ustrict=ustrict, ule=ule, ult=ult)


def _doubled(tri, tk):
    b = tri[:tk, :tk]
    return jnp.asarray(np.concatenate([b, b], axis=0), BF16)


def _pick(n, cap):
    best = 128
    for t in range(128, min(n, cap) + 1, 128):
        if n % t == 0:
            best = t
    return n if n <= cap else best


def _matmul(a, b, dims, out_dtype, name, tm_cap=1024, tn_cap=2048, tk_cap=1024, exchange=None):
    if dims == "nn":
        (m, k), (_, n) = a.shape, b.shape
    elif dims == "nt":
        (m, k), (n, _) = a.shape, b.shape
    else:
        (k, m), (_, n) = a.shape, b.shape
    tm, tn, tk = _pick(m, tm_cap), _pick(n, tn_cap), _pick(k, tk_cap)
    nk = k // tk
    a_spec = (pl.BlockSpec((tk, tm), lambda i, j, kk: (kk, i)) if dims == "tn"
              else pl.BlockSpec((tm, tk), lambda i, j, kk: (i, kk)))
    b_spec = (pl.BlockSpec((tn, tk), lambda i, j, kk: (j, kk)) if dims == "nt"
              else pl.BlockSpec((tk, tn), lambda i, j, kk: (kk, j)))

    grid = (m // tm, n // tn, nk)

    def body(a_ref, b_ref, *rest):
        if exchange is None:
            o_ref, acc_ref = rest
        else:
            xs_ref, o_ref, xr_ref, acc_ref, xss, xrs = rest
            ids = [pl.program_id(d) for d in range(3)]
            x_start, x_finish = _exchange_stages(xs_ref, xr_ref, xss, xrs)
            pl.when(functools.reduce(jnp.logical_and, [p == 0 for p in ids]))(x_start)
        kk = pl.program_id(2)
        part = _dot(a_ref[...], b_ref[...], dims)
        if nk == 1:
            o_ref[...] = part.astype(out_dtype)
        else:
            @pl.when(kk == 0)
            def _():
                acc_ref[...] = part

            @pl.when(kk > 0)
            def _():
                acc_ref[...] += part

            @pl.when(kk == nk - 1)
            def _():
                o_ref[...] = acc_ref[...].astype(out_dtype)
        if exchange is not None:
            pl.when(functools.reduce(jnp.logical_and, [p == g - 1 for p, g in zip(ids, grid)]))(x_finish)

    in_specs = [a_spec, b_spec]
    out_specs = [pl.BlockSpec((tm, tn), lambda i, j, kk: (i, j))]
    out_shape = [jax.ShapeDtypeStruct((m, n), out_dtype)]
    scratch = [pltpu.VMEM((tm, tn) if nk > 1 else (8, 128), F32)]
    args = [a, b]
    if exchange is not None:
        in_specs.append(HB)
        out_specs.append(HB)
        out_shape.append(jax.ShapeDtypeStruct((3,) + exchange.shape[1:], exchange.dtype))
        scratch += EXCHANGE_SEMS
        args.append(exchange)
    out = pl.pallas_call(
        body, name=name, grid=grid, in_specs=in_specs, out_specs=out_specs, out_shape=out_shape,
        scratch_shapes=scratch,
        compiler_params=_cp(("parallel", "parallel", "arbitrary") if exchange is None else ("arbitrary",) * 3),
    )(*args)
    return out[0] if exchange is None else out


def _row_spec(tm, width=D_MODEL, col=0):
    return pl.BlockSpec((tm, width), lambda i: (i, col))


def _fix_spec(shape):
    return pl.BlockSpec(shape, lambda *_: (0,) * len(shape))


def _norm_mod(x, nw, mod, row_sh, name):
    t = x.shape[0]
    tm = min(t, 512)

    def body(x_ref, nw_ref, mod_ref, h_ref):
        xv = x_ref[...]
        r = lax.rsqrt(jnp.mean(xv * xv, axis=-1, keepdims=True) + EPS)
        sh = mod_ref[row_sh:row_sh + 1, :]
        sc = mod_ref[row_sh + 1:row_sh + 2, :]
        h_ref[...] = (xv * r * nw_ref[...] * (1.0 + sc) + sh).astype(BF16)

    return pl.pallas_call(
        body, name=name, grid=(t // tm,),
        in_specs=[_row_spec(tm), _fix_spec((1, D_MODEL)), _fix_spec((8, D_MODEL))],
        out_specs=_row_spec(tm), out_shape=jax.ShapeDtypeStruct((t, D_MODEL), BF16),
        compiler_params=_cp(("parallel",)),
    )(x, nw, mod)


def _resid_norm(x, mix, nw, mod, name):
    t = x.shape[0]
    tm = min(t, 512)

    def body(x_ref, mix_ref, nw_ref, mod_ref, x1_ref, h_ref):
        x1 = x_ref[...] + mod_ref[2:3, :] * mix_ref[...]
        x1_ref[...] = x1
        r = lax.rsqrt(jnp.mean(x1 * x1, axis=-1, keepdims=True) + EPS)
        h_ref[...] = (x1 * r * nw_ref[...] * (1.0 + mod_ref[4:5, :]) + mod_ref[3:4, :]).astype(BF16)

    return pl.pallas_call(
        body, name=name, grid=(t // tm,),
        in_specs=[_row_spec(tm), _row_spec(tm), _fix_spec((1, D_MODEL)), _fix_spec((8, D_MODEL))],
        out_specs=[_row_spec(tm), _row_spec(tm)],
        out_shape=[jax.ShapeDtypeStruct((t, D_MODEL), F32), jax.ShapeDtypeStruct((t, D_MODEL), BF16)],
        compiler_params=_cp(("parallel",)),
    )(x, mix, nw, mod)


def _act_fwd(gu, name):
    t = gu.shape[0]
    tm, tn = min(t, 512), D_FF // 2
    nb = D_FF // tn

    def body(g_ref, u_ref, a_ref):
        g = g_ref[...].astype(F32)
        a_ref[...] = (g * _sig(g) * u_ref[...].astype(F32)).astype(BF16)

    return pl.pallas_call(
        body, name=name, grid=(t // tm, nb),
        in_specs=[pl.BlockSpec((tm, tn), lambda i, j: (i, j)), pl.BlockSpec((tm, tn), lambda i, j: (i, j + nb))],
        out_specs=pl.BlockSpec((tm, tn), lambda i, j: (i, j)),
        out_shape=jax.ShapeDtypeStruct((t, D_FF), BF16),
        compiler_params=_cp(("parallel", "parallel")),
    )(gu, gu)


def _act_bwd(dact, gu, name):
    t = gu.shape[0]
    tm = min(t, 256)

    def body(d_ref, g_ref, u_ref, o_ref):
        g, d = g_ref[...].astype(F32), d_ref[...].astype(F32)
        s = _sig(g)
        o_ref[:, 0:D_FF] = (d * u_ref[...].astype(F32) * s * (1.0 + g * (1.0 - s))).astype(BF16)
        o_ref[:, D_FF:2 * D_FF] = (d * g * s).astype(BF16)

    return pl.pallas_call(
        body, name=name, grid=(t // tm,),
        in_specs=[pl.BlockSpec((tm, D_FF), lambda i: (i, 0)), pl.BlockSpec((tm, D_FF), lambda i: (i, 0)),
                  pl.BlockSpec((tm, D_FF), lambda i: (i, 1))],
        out_specs=pl.BlockSpec((tm, 2 * D_FF), lambda i: (i, 0)),
        out_shape=jax.ShapeDtypeStruct((t, 2 * D_FF), BF16),
        compiler_params=_cp(("parallel",)),
    )(dact, gu, gu)


def _loss_head(x1, ffn, tgt, mod, name):
    t = x1.shape[0]
    tm = min(t, 512)

    def body(x1_ref, f_ref, t_ref, mod_ref, dffn_ref, dout_ref, dg2_ref, loss_ref):
        i = pl.program_id(0)
        g2 = mod_ref[5:6, :]
        f = f_ref[...]
        err = x1_ref[...] + g2 * f - t_ref[...]
        dout = err * (1.0 / D_MODEL)
        dout_ref[...] = dout
        dffn_ref[...] = (dout * g2).astype(BF16)
        part = jnp.zeros((8, 128), F32) + 0.5 * jnp.sum(jnp.mean(err * err, axis=-1, keepdims=True))

        @pl.when(i == 0)
        def _():
            dg2_ref[...] = _colsum(dout * f)
            loss_ref[...] = part

        @pl.when(i > 0)
        def _():
            dg2_ref[...] += _colsum(dout * f)
            loss_ref[...] += part

    return pl.pallas_call(
        body, name=name, grid=(t // tm,),
        in_specs=[_row_spec(tm), _row_spec(tm), _row_spec(tm), _fix_spec((8, D_MODEL))],
        out_specs=[_row_spec(tm), _row_spec(tm), _fix_spec((1, D_MODEL)), _fix_spec((8, 128))],
        out_shape=[jax.ShapeDtypeStruct((t, D_MODEL), BF16), jax.ShapeDtypeStruct((t, D_MODEL), F32),
                   jax.ShapeDtypeStruct((1, D_MODEL), F32), jax.ShapeDtypeStruct((8, 128), F32)],
        compiler_params=_cp(("arbitrary",)),
    )(x1, ffn, tgt, mod)


def _norm_bwd(dh, xin, dres, aux, nw, mod, row_sh, gate_row, name):
    t = xin.shape[0]
    tm = min(t, 512)
    with_gate = gate_row is not None

    def body(*refs):
        if with_gate:
            dh_ref, x_ref, dr_ref, aux_ref, nw_ref, mod_ref, dx_ref, dg_ref, acc_ref = refs
        else:
            dh_ref, x_ref, dr_ref, nw_ref, mod_ref, dx_ref, acc_ref = refs
        i = pl.program_id(0)
        xv, dhv = x_ref[...], dh_ref[...]
        r = lax.rsqrt(jnp.mean(xv * xv, axis=-1, keepdims=True) + EPS)
        xn = xv * r
        nwv = nw_ref[...]
        sc1 = 1.0 + mod_ref[row_sh + 1:row_sh + 2, :]
        dxn = dhv * (nwv * sc1)
        dx = dr_ref[...] + r * (dxn - xn * jnp.mean(dxn * xn, axis=-1, keepdims=True))
        dx_ref[...] = dx
        dhx = dhv * xn
        rows = [_colsum(dhv), _colsum(dhx * nwv), _colsum(dhx * sc1)]
        if with_gate:
            dg_ref[...] = (dx * mod_ref[gate_row:gate_row + 1, :]).astype(BF16)
            rows.append(_colsum(dx * aux_ref[...]))

        @pl.when(i == 0)
        def _():
            acc_ref[...] = jnp.zeros_like(acc_ref)

        for k, v in enumerate(rows):
            acc_ref[k:k + 1, :] += v

    ins = [dh, xin, dres] + ([aux] if with_gate else []) + [nw, mod]
    in_specs = [_row_spec(tm)] * (4 if with_gate else 3) + [_fix_spec((1, D_MODEL)), _fix_spec((8, D_MODEL))]
    out_specs = [_row_spec(tm)] + ([_row_spec(tm)] if with_gate else []) + [_fix_spec((8, D_MODEL))]
    out_shape = ([jax.ShapeDtypeStruct((t, D_MODEL), F32)]
                 + ([jax.ShapeDtypeStruct((t, D_MODEL), BF16)] if with_gate else [])
                 + [jax.ShapeDtypeStruct((8, D_MODEL), F32)])
    return pl.pallas_call(
        body, name=name, grid=(t // tm,), in_specs=in_specs, out_specs=out_specs, out_shape=out_shape,
        compiler_params=_cp(("arbitrary",)),
    )(*ins)


XBC_COL0 = 4096 // 128
DT_COL = 5632 // 128


def _conv_pre(xv, w_ref, b_ref):
    t = xv.shape[0]
    row = lax.broadcasted_iota(jnp.int32, xv.shape, 0)
    pre = xv * w_ref[3:4, :] + b_ref[...]
    shifted = []
    for k in range(3):
        s = 3 - k
        xs = jnp.where(row >= s, pltpu.roll(xv, s, 0), 0.0)
        shifted.append(xs)
        pre = pre + xs * w_ref[k:k + 1, :]
    return pre, shifted, row, t


def _conv_fwd(proj, conv_w, conv_b, name):
    t = proj.shape[0]

    def body(x_ref, w_ref, b_ref, u_ref):
        pre, _, _, _ = _conv_pre(x_ref[...], w_ref, b_ref)
        u_ref[...] = pre * _sig(pre)

    return pl.pallas_call(
        body, name=name, grid=(D_CONV // 128,),
        in_specs=[pl.BlockSpec((t, 128), lambda j: (0, XBC_COL0 + j)), pl.BlockSpec((4, 128), lambda j: (0, j)),
                  pl.BlockSpec((1, 128), lambda j: (0, j))],
        out_specs=pl.BlockSpec((t, 128), lambda j: (0, j)),
        out_shape=jax.ShapeDtypeStruct((t, D_CONV), F32),
        compiler_params=_cp(("parallel",)),
    )(proj, conv_w, conv_b)


def _conv_bwd(proj, du, conv_w, conv_b, name):
    t = proj.shape[0]

    def body(x_ref, du_ref, w_ref, b_ref, dx_ref, dw_ref, db_ref):
        pre, shifted, row, _ = _conv_pre(x_ref[...], w_ref, b_ref)
        s = _sig(pre)
        dpre = du_ref[...] * s * (1.0 + pre * (1.0 - s))
        db_ref[...] = _colsum(dpre)
        dx = dpre * w_ref[3:4, :]
        dw_ref[3:4, :] = _colsum(dpre * x_ref[...])
        for k in range(3):
            sft = 3 - k
            dw_ref[k:k + 1, :] = _colsum(dpre * shifted[k])
            back = jnp.where(row < t - sft, pltpu.roll(dpre, t - sft, 0), 0.0)
            dx = dx + back * w_ref[k:k + 1, :]
        dx_ref[...] = dx.astype(BF16)

    return pl.pallas_call(
        body, name=name, grid=(D_CONV // 128,),
        in_specs=[pl.BlockSpec((t, 128), lambda j: (0, XBC_COL0 + j)), pl.BlockSpec((t, 128), lambda j: (0, j)),
                  pl.BlockSpec((4, 128), lambda j: (0, j)), pl.BlockSpec((1, 128), lambda j: (0, j))],
        out_specs=[pl.BlockSpec((t, 128), lambda j: (0, j)), pl.BlockSpec((4, 128), lambda j: (0, j)),
                   pl.BlockSpec((1, 128), lambda j: (0, j))],
        out_shape=[jax.ShapeDtypeStruct((t, D_CONV), BF16), jax.ShapeDtypeStruct((4, D_CONV), F32),
                   jax.ShapeDtypeStruct((1, D_CONV), F32)],
        compiler_params=_cp(("parallel",)),
    )(proj, du, conv_w, conv_b)


def _ssd_common(dtraw_ref, dtb_ref, alog_ref, tril, expand, l_s, lt_s):
    lane = lax.broadcasted_iota(jnp.int32, (1, 128), 1)
    dt = _softplus(dtraw_ref[...] + dtb_ref[...])
    a = jnp.where(lane < N_HEADS, -jnp.exp(alog_ref[...]), 0.0)
    lcs = _dotx_l(tril, dt * a)
    l_s[...] = lcs
    lt_s[...] = lcs.T
    llast = l_s[CHUNK - 1:CHUNK, :]
    ea = _dotx_r(jnp.exp(lcs), expand, n=2)
    ds = _dotx_r(jnp.exp(llast - lcs), expand, n=2)
    dtx = _dotx_r(dt, expand, n=2)
    return dt, a, lcs, llast, ea, ds, dtx


def _head_col(lcs, h):
    lane = lax.broadcasted_iota(jnp.int32, lcs.shape, 1)
    return jnp.sum(jnp.where(lane == h, lcs, 0.0), axis=1, keepdims=True)


def _decay(lcs, lt_s, h, causal):
    seg = _head_col(lcs, h) - lt_s[h:h + 1, :]
    return jnp.exp(jnp.where(causal, seg, -1e30))


def _ssd_fwd(u, proj, dtb, alog, dsk, nw, cst, name):
    t = u.shape[0]
    nc = t // CHUNK

    def body(xs_ref, b_ref, c_ref, dtraw_ref, z_ref, dtb_ref, alog_ref, dsk_ref, nw_ref, tril_ref, exp_ref,
             y_ref, yn_ref, prev_ref, carry, l_s, lt_s, yd_s):
        i = pl.program_id(0)

        @pl.when(i == 0)
        def _():
            carry[...] = jnp.zeros_like(carry)

        expand = exp_ref[...]
        dt, a, lcs, llast, ea, ds, dtx = _ssd_common(dtraw_ref, dtb_ref, alog_ref, tril_ref[...], expand, l_s, lt_s)
        xs = xs_ref[...]
        xg = xs * dtx
        xgb = xg.astype(BF16)
        xgd = (xg * ds).astype(BF16)
        prev = carry[...]
        prev_ref[0] = prev
        prevb = prev.astype(BF16)
        ri = lax.broadcasted_iota(jnp.int32, (CHUNK, CHUNK), 0)
        ci = lax.broadcasted_iota(jnp.int32, (CHUNK, CHUNK), 1)
        causal = ri >= ci
        lane = lax.broadcasted_iota(jnp.int32, (1, 128), 1)
        new_states, yoff = [], []
        for g in range(2):
            bg = b_ref[:, g * 128:(g + 1) * 128].astype(BF16)
            cg = c_ref[:, g * 128:(g + 1) * 128].astype(BF16)
            sc = _dot(cg, bg, "nt")
            gs = slice(g * 512, (g + 1) * 512)
            new_states.append(_dot(bg, xgd[:, gs], "tn"))
            yoff.append(_dot(cg, prevb[:, gs]))
            for pr in range(4):
                col = g * 512 + pr * 128
                xp = xgb[:, col:col + 128]
                acc = jnp.zeros((CHUNK, 128), F32)
                for half in range(2):
                    h = g * 8 + pr * 2 + half
                    m = (sc * _decay(lcs, lt_s, h, causal)).astype(BF16)
                    keep = (lane < HEAD_DIM) if half == 0 else (lane >= HEAD_DIM)
                    acc = acc + _dot(m, jnp.where(keep, xp, jnp.zeros_like(xp)))
                yd_s[:, col:col + 128] = acc
        y = yd_s[...] + jnp.concatenate(yoff, axis=1) * ea + xs * dsk_ref[...]
        y_ref[...] = y
        carry[...] = prev * jnp.max(_dotx_r(jnp.exp(llast) + jnp.zeros((8, 128), F32), expand, n=2), axis=0, keepdims=True) \
            + jnp.concatenate(new_states, axis=1)
        z = z_ref[...]
        yz = y * (z * _sig(z))
        nwv = nw_ref[...]
        for g in range(2):
            gs = slice(g * 512, (g + 1) * 512)
            v = yz[:, gs]
            r = lax.rsqrt(jnp.mean(v * v, axis=-1, keepdims=True) + EPS)
            yn_ref[:, gs] = (v * r * nwv[:, gs]).astype(BF16)

    row = lambda w, col: pl.BlockSpec((CHUNK, w), lambda i: (i, col))
    return pl.pallas_call(
        body, name=name, grid=(nc,),
        in_specs=[row(1024, 0), row(256, 4), row(256, 5), row(128, DT_COL), row(1024, 0),
                  _fix_spec((1, 128)), _fix_spec((1, 128)), _fix_spec((1, D_MODEL)), _fix_spec((1, D_MODEL)),
                  _fix_spec((CHUNK, CHUNK)), _fix_spec((128, D_MODEL))],
        out_specs=[row(1024, 0), row(1024, 0), pl.BlockSpec((1, 128, D_MODEL), lambda i: (i, 0, 0))],
        out_shape=[jax.ShapeDtypeStruct((t, D_MODEL), F32), jax.ShapeDtypeStruct((t, 2 * D_MODEL), BF16),
                   jax.ShapeDtypeStruct((nc, 128, D_MODEL), F32)],
        scratch_shapes=[pltpu.VMEM((128, D_MODEL), F32), pltpu.VMEM((128, 128), F32), pltpu.VMEM((128, 128), F32),
                        pltpu.VMEM((CHUNK, D_MODEL), F32)],
        compiler_params=_cp(("arbitrary",)),
    )(u, u, u, proj, proj, dtb, alog, dsk, nw, cst["tril"], cst["expand"])


def _ssd_bwd(u, proj, y, prev, dycat, dtb, alog, dsk, nw, gpack, cst, name):
    t = u.shape[0]
    nc = t // CHUNK

    def body(xs_ref, b_ref, c_ref, dtraw_ref, z_ref, y_ref, prev_ref, dyn_ref, dtb_ref, alog_ref, dsk_ref, nw_ref,
             tril_ref, triu_ref, exp_ref, hs_ref, g_ref,
             du_ref, ddt_ref, dz_ref, acc_ref, acc16_ref, got_ref, dcarry, l_s, lt_s, dxg_s, wss, wrs):
        i = pl.program_id(0)
        w_start, w_finish = _swap_stages(g_ref, got_ref, wss, wrs)
        pl.when(i == 0)(w_start)

        @pl.when(i == 0)
        def _():
            dcarry[...] = jnp.zeros_like(dcarry)
            acc_ref[...] = jnp.zeros_like(acc_ref)
            acc16_ref[...] = jnp.zeros_like(acc16_ref)

        expand, hsum = exp_ref[...], hs_ref[...]
        dt, a, lcs, llast, ea, ds, dtx = _ssd_common(dtraw_ref, dtb_ref, alog_ref, tril_ref[...], expand, l_s, lt_s)
        xs = xs_ref[...]
        xg = xs * dtx
        xgb = xg.astype(BF16)
        xgdf = xg * ds
        xgd = xgdf.astype(BF16)
        dsk_v, nwv = dsk_ref[...], nw_ref[...]
        z, y = z_ref[...], y_ref[...]
        sz = _sig(z)
        silz = z * sz
        yz = y * silz
        dyn = dyn_ref[...]
        dyz_parts, dnw_parts = [], []
        for g in range(2):
            gs = slice(g * 512, (g + 1) * 512)
            v = yz[:, gs]
            r = lax.rsqrt(jnp.mean(v * v, axis=-1, keepdims=True) + EPS)
            yhat = v * r
            dnw_parts.append(_colsum(dyn[:, gs] * yhat))
            dw = dyn[:, gs] * nwv[:, gs]
            dyz_parts.append(r * (dw - yhat * jnp.mean(dw * yhat, axis=-1, keepdims=True)))
        dyz = jnp.concatenate(dyz_parts, axis=1)
        dy = dyz * silz
        dz_ref[...] = (dyz * y * (sz * (1.0 + z * (1.0 - sz)))).astype(BF16)
        acc_ref[0:1, :] += jnp.concatenate(dnw_parts, axis=1)
        acc_ref[1:2, :] += _colsum(dy * xs)
        dyb = dy.astype(BF16)
        dq = (dy * ea).astype(BF16)
        dcar = dcarry[...]
        dcarb = dcar.astype(BF16)
        prev = prev_ref[0]
        prevb = prev.astype(BF16)
        ri = lax.broadcasted_iota(jnp.int32, (CHUNK, CHUNK), 0)
        ci = lax.broadcasted_iota(jnp.int32, (CHUNK, CHUNK), 1)
        causal = ri >= ci
        lane = lax.broadcasted_iota(jnp.int32, (1, 128), 1)
        dprev, dxgd, yoff = [], [], []
        dl_l = jnp.zeros((CHUNK, CHUNK), F32)
        dl_s = jnp.zeros((CHUNK, CHUNK), F32)
        for g in range(2):
            gs = slice(g * 512, (g + 1) * 512)
            bg = b_ref[:, g * 128:(g + 1) * 128].astype(BF16)
            cg = c_ref[:, g * 128:(g + 1) * 128].astype(BF16)
            sc = _dot(cg, bg, "nt")
            yoff.append(_dot(cg, prevb[:, gs]))
            dcg = _dot(dq[:, gs], prevb[:, gs], "nt")
            dprev.append(_dot(cg, dq[:, gs], "tn"))
            dbg = _dot(xgd[:, gs], dcarb[:, gs], "nt")
            dxgd.append(_dot(bg, dcarb[:, gs]))
            dsc = jnp.zeros((CHUNK, CHUNK), F32)
            for pr in range(4):
                col = g * 512 + pr * 128
                xp = xgb[:, col:col + 128]
                dyp = dyb[:, col:col + 128]
                acc = jnp.zeros((CHUNK, 128), F32)
                for half in range(2):
                    h = g * 8 + pr * 2 + half
                    dec = _decay(lcs, lt_s, h, causal)
                    mf = sc * dec
                    keep = (lane < HEAD_DIM) if half == 0 else (lane >= HEAD_DIM)
                    dyh = jnp.where(keep, dyp, jnp.zeros_like(dyp))
                    dm = _dot(dyh, xp, "nt")
                    acc = acc + _dot(mf.astype(BF16), dyh, "tn")
                    dsc = dsc + dm * dec
                    gm = dm * mf
                    dl_l = dl_l + jnp.where(ci == h, jnp.sum(gm, axis=1, keepdims=True), 0.0)
                    dl_s = dl_s + jnp.where(ri == h, jnp.sum(gm, axis=0, keepdims=True), 0.0)
                dxg_s[:, col:col + 128] = acc
            dscb = dsc.astype(BF16)
            dcg = dcg + _dot(dscb, bg)
            dbg = dbg + _dot(dscb, cg, "tn")
            du_ref[:, 1024 + g * 128:1024 + (g + 1) * 128] = dbg
            du_ref[:, 1280 + g * 128:1280 + (g + 1) * 128] = dcg
        dxgd = jnp.concatenate(dxgd, axis=1)
        dxg = dxg_s[...] + dxgd * ds
        du_ref[:, 0:1024] = dy * dsk_v + dxg * dtx
        hs1 = _dotx_r(dxg * xs, hsum, n=2)
        yoff = jnp.concatenate(yoff, axis=1) * ea
        dl = dl_l - dl_s.T + _dotx_r(dy * yoff - xgdf * dxgd, hsum, n=2)
        rows8 = lax.broadcasted_iota(jnp.int32, (8, D_MODEL), 0)
        two = jnp.where(rows8 == 0, _colsum(dxgd * xgdf), jnp.where(rows8 == 1, _colsum(dcar * prev), 0.0))
        two = _dotx_r(two, hsum)
        r8 = lax.broadcasted_iota(jnp.int32, (8, 128), 0)
        dllast = _colsum(jnp.where(r8 == 0, two, 0.0)) + _colsum(jnp.where(r8 == 1, two, 0.0)) * jnp.exp(llast)
        rowi = lax.broadcasted_iota(jnp.int32, (CHUNK, 128), 0)
        dl = dl + jnp.where(rowi == CHUNK - 1, dllast, 0.0)
        dadt = _dotx_l(triu_ref[...], dl)
        ddt = dadt * a + hs1
        draw = ddt * _sig(dtraw_ref[...] + dtb_ref[...])
        ddt_ref[...] = draw.astype(BF16)
        acc16_ref[0:1, :] += _colsum(draw)
        acc16_ref[1:2, :] += _colsum(dadt * dt) * a
        dcarry[...] = dcar * jnp.max(_dotx_r(jnp.exp(llast) + jnp.zeros((8, 128), F32), expand, n=2), axis=0, keepdims=True) \
            + jnp.concatenate(dprev, axis=1)

        @pl.when(i == nc - 1)
        def _():
            hd = _dotx_r(acc_ref[...], hsum)
            acc16_ref[2:3, :] = _colsum(jnp.where(lax.broadcasted_iota(jnp.int32, (8, 128), 0) == 1, hd, 0.0))

        pl.when(i == nc - 1)(w_finish)

    rev = lambda w, col: pl.BlockSpec((CHUNK, w), lambda i: (nc - 1 - i, col))
    return pl.pallas_call(
        body, name=name, grid=(nc,),
        in_specs=[rev(1024, 0), rev(256, 4), rev(256, 5), rev(128, DT_COL), rev(1024, 0), rev(1024, 0),
                  pl.BlockSpec((1, 128, D_MODEL), lambda i: (nc - 1 - i, 0, 0)), rev(1024, 0),
                  _fix_spec((1, 128)), _fix_spec((1, 128)), _fix_spec((1, D_MODEL)), _fix_spec((1, D_MODEL)),
                  _fix_spec((CHUNK, CHUNK)), _fix_spec((CHUNK, CHUNK)), _fix_spec((128, D_MODEL)),
                  _fix_spec((D_MODEL, 128)), HB],
        out_specs=[rev(D_CONV, 0), rev(128, 0), rev(1024, 0), _fix_spec((8, D_MODEL)), _fix_spec((8, 128)), HB],
        out_shape=[jax.ShapeDtypeStruct((t, D_CONV), F32), jax.ShapeDtypeStruct((t, 128), BF16),
                   jax.ShapeDtypeStruct((t, D_MODEL), BF16), jax.ShapeDtypeStruct((8, D_MODEL), F32),
                   jax.ShapeDtypeStruct((8, 128), F32), _swap_shape(gpack)],
        scratch_shapes=[pltpu.VMEM((128, D_MODEL), F32), pltpu.VMEM((128, 128), F32), pltpu.VMEM((128, 128), F32),
                        pltpu.VMEM((CHUNK, D_MODEL), F32)] + SWAP_SEMS,
        compiler_params=_cp(("arbitrary",)),
    )(u, u, u, proj, proj, y, prev, dycat, dtb, alog, dsk, nw,
      cst["tril"], cst["triu"], cst["expand"], cst["hsum"], gpack)


def _head_rms(v, hsum, expand):
    ms = _dotx_r(v * v, hsum, n=2) * (1.0 / HEAD_DIM)
    return _dotx_r(lax.rsqrt(ms + EPS), expand, n=2)


def _qk_fwd(proj, qw, kw, cst, name):
    t = proj.shape[0]
    tm = min(t, 256)
    scale = HEAD_DIM ** -0.5

    def body(q_ref, k_ref, v_ref, qw_ref, kw_ref, hs_ref, exp_ref, qs_ref, kn_ref, vb_ref):
        hsum, expand = hs_ref[...], exp_ref[...]
        q, k = q_ref[...], k_ref[...]
        qs_ref[...] = (q * _head_rms(q, hsum, expand) * qw_ref[...] * scale).astype(BF16)
        kn_ref[...] = (k * _head_rms(k, hsum, expand) * kw_ref[...]).astype(BF16)
        vb_ref[...] = v_ref[...].astype(BF16)

    return pl.pallas_call(
        body, name=name, grid=(t // tm,),
        in_specs=[_row_spec(tm, col=1), _row_spec(tm, col=2), _row_spec(tm, col=3),
                  _fix_spec((1, D_MODEL)), _fix_spec((1, D_MODEL)), _fix_spec((D_MODEL, 128)),
                  _fix_spec((128, D_MODEL))],
        out_specs=[_row_spec(tm)] * 3, out_shape=[jax.ShapeDtypeStruct((t, D_MODEL), BF16)] * 3,
        compiler_params=_cp(("parallel",)),
    )(proj, proj, proj, qw, kw, cst["hsum"], cst["expand"])


def _qk_bwd(proj, dqs, dkn, dv, qw, kw, cst, name):
    t = proj.shape[0]
    tm = min(t, 256)
    scale = HEAD_DIM ** -0.5

    def body(q_ref, k_ref, dq_ref, dk_ref, dv_ref, qw_ref, kw_ref, hs_ref, exp_ref, fold_ref,
             oq_ref, ok_ref, ov_ref, dw_ref):
        i = pl.program_id(0)
        hsum, expand = hs_ref[...], exp_ref[...]
        rows8 = lax.broadcasted_iota(jnp.int32, (8, D_MODEL), 0)
        sums = jnp.zeros((8, D_MODEL), F32)
        for n, (x_ref, d_ref, w_ref, o_ref, sc) in enumerate(
                [(q_ref, dq_ref, qw_ref, oq_ref, scale), (k_ref, dk_ref, kw_ref, ok_ref, 1.0)]):
            xv = x_ref[...]
            r = _head_rms(xv, hsum, expand)
            xhat = xv * r
            dn = d_ref[...] * sc
            sums = sums + jnp.where(rows8 == n, _colsum(dn * xhat), 0.0)
            dw = dn * w_ref[...]
            mean = _dotx_r(_dotx_r(dw * xhat, hsum, n=1), expand, n=2) * (1.0 / HEAD_DIM)
            o_ref[...] = (r * (dw - xhat * mean)).astype(BF16)
        ov_ref[...] = dv_ref[...].astype(BF16)
        folded = _dotx_r(sums, fold_ref[...])

        @pl.when(i == 0)
        def _():
            dw_ref[...] = folded

        @pl.when(i > 0)
        def _():
            dw_ref[...] += folded

    return pl.pallas_call(
        body, name=name, grid=(t // tm,),
        in_specs=[_row_spec(tm, col=1), _row_spec(tm, col=2), _row_spec(tm), _row_spec(tm), _row_spec(tm),
                  _fix_spec((1, D_MODEL)), _fix_spec((1, D_MODEL)), _fix_spec((D_MODEL, 128)),
                  _fix_spec((128, D_MODEL)), _fix_spec((D_MODEL, 128))],
        out_specs=[_row_spec(tm)] * 3 + [_fix_spec((8, 128))],
        out_shape=[jax.ShapeDtypeStruct((t, D_MODEL), BF16)] * 3 + [jax.ShapeDtypeStruct((8, 128), F32)],
        compiler_params=_cp(("arbitrary",)),
    )(proj, proj, dqs, dkn, dv, qw, kw, cst["hsum"], cst["expand"], cst["fold"])


def _sb_masks(i, kb, tq, tk):
    tpos = i * tq + lax.broadcasted_iota(jnp.int32, (tq, 1), 0)
    spos = kb * tk + lax.broadcasted_iota(jnp.int32, (1, tk), 1)
    return spos < tpos


def _grid_marks(n0, n1):
    j, i = pl.program_id(0), pl.program_id(1)
    return (jnp.logical_and(j == 0, i == 0), jnp.logical_and(j == n0 // 2, i == 0),
            jnp.logical_and(j == n0 - 1, i == n1 - 1))


def _sb_fwd(qs, kn, vb, pack, ycat, cst, name):
    t = qs.shape[0]
    tq = tk = min(t, SB_TILE)
    nq = t // tq
    ngrp = D_MODEL // SB_LANES
    nh = 2 * SB_PAIRS
    lanes = lambda p: slice(p * 128, (p + 1) * 128)

    def body(q_ref, k_ref, v_ref, u_ref, p_ref, yc_ref, rt_ref, ob_ref, cnt_ref, gat_ref, acc, rs, gss, grs):
        del yc_ref
        at_first, at_mid, at_last = _grid_marks(ngrp, nq)
        g_start, g_relay, g_finish = _gather_stages(p_ref, gat_ref, gss, grs)
        pl.when(at_first)(g_start)
        pl.when(at_mid)(g_relay)
        i = pl.program_id(1)
        lane = lax.broadcasted_iota(jnp.int32, (1, 128), 1)
        qh = []
        for p in range(SB_PAIRS):
            q2 = q_ref[:, lanes(p)]
            zero = jnp.zeros_like(q2)
            qh += [jnp.where(lane < HEAD_DIM, q2, zero), jnp.where(lane >= HEAD_DIM, q2, zero)]
        acc[...] = jnp.zeros_like(acc)
        rs[...] = jnp.zeros_like(rs)
        ustrict = u_ref[...]

        def tile(kb, masked):
            off = pl.multiple_of(kb * tk, tk)
            k2 = [k_ref[pl.ds(off, tk), lanes(p)] for p in range(SB_PAIRS)]
            v2 = [v_ref[pl.ds(off, tk), lanes(p)] for p in range(SB_PAIRS)]
            strict = _sb_masks(i, kb, tq, tk) if masked else None
            s = [_dot(qh[h], k2[h // 2], "nt") for h in range(nh)]
            a, r, lb = [None] * nh, [None] * nh, [None] * nh
            for h in range(nh):
                sp = _softplus(s[h])
                a[h] = s[h] - sp
                r[h] = jnp.where(strict, -sp, 0.0) if masked else -sp
                lb[h] = _dot2(r[h], ustrict)
            for h in range(nh):
                lw = a[h] + lb[h] + rs[h]
                w = jnp.exp(jnp.where(strict, lw, -1e30) if masked else lw)
                rs[h] = rs[h] + jnp.sum(r[h], axis=1, keepdims=True)
                acc[h] = acc[h] + _dot(w.astype(BF16), v2[h // 2])

        tile(i, True)

        def live():
            return jnp.max(functools.reduce(jnp.maximum, [rs[h] for h in range(nh)]))

        def more(c):
            return jnp.logical_and(c[0] < i, c[1] > SB_DEAD)

        def step(c):
            tile(i - 1 - c[0], False)
            return c[0] + 1, live()

        n_off, _ = lax.while_loop(more, step, (jnp.int32(0), live()))
        cnt_ref[pl.program_id(0), i] = n_off.astype(F32)
        for p in range(SB_PAIRS):
            rt_ref[:, lanes(p)] = jnp.where(lane < HEAD_DIM, rs[2 * p], rs[2 * p + 1])
            ob_ref[:, lanes(p)] = jnp.where(lane < HEAD_DIM, acc[2 * p], acc[2 * p + 1]).astype(BF16)
        pl.when(at_last)(g_finish)

    blk = lambda rows, imap: pl.BlockSpec((rows, SB_LANES), imap)
    return pl.pallas_call(
        body, name=name, grid=(ngrp, nq),
        in_specs=[blk(tq, lambda j, i: (i, j)), blk(t, lambda j, i: (0, j)), blk(t, lambda j, i: (0, j)),
                  _fix_spec((2 * tk, tk)), HB, pl.BlockSpec(memory_space=pl.ANY)],
        out_specs=[blk(tq, lambda j, i: (i, j)), blk(tq, lambda j, i: (i, ngrp + j)),
                   pl.BlockSpec(memory_space=pltpu.SMEM), HB],
        out_shape=[jax.ShapeDtypeStruct((t, D_MODEL), F32), jax.ShapeDtypeStruct(ycat.shape, ycat.dtype),
                   jax.ShapeDtypeStruct((ngrp, nq), F32),
                   jax.ShapeDtypeStruct((N_SHARDS,) + pack.shape, pack.dtype)],
        scratch_shapes=[pltpu.VMEM((nh, tq, 128), F32), pltpu.VMEM((nh, tq, 1), F32)] + GATHER_SEMS,
        input_output_aliases={5: 1},
        compiler_params=_cp(("arbitrary", "arbitrary")),
    )(qs, kn, vb, _doubled(cst["ustrict"], tk), pack, ycat)


def _sb_bwd(qs, kn, vb, rtot, cnt, dycat, csum_b, cst, name):
    t = qs.shape[0]
    tq = tk = min(t, SB_TILE)
    nq = t // tq
    ngrp = D_MODEL // SB_LANES
    nh = 2 * SB_PAIRS
    lanes = lambda p: slice(p * 128, (p + 1) * 128)

    def body(q_ref, k_ref, v_ref, rt_ref, do_ref, us_ref, ui_ref, cnt_ref, xs_ref, dq_ref, dk_ref, dv_ref, xr_ref,
             acc, rs, es, xss, xrs):
        at_first, _, at_last = _grid_marks(ngrp, nq)
        x_start, x_finish = _exchange_stages(xs_ref, xr_ref, xss, xrs)
        pl.when(at_first)(x_start)
        i = pl.program_id(1)
        lane = lax.broadcasted_iota(jnp.int32, (1, 128), 1)
        keep = [lane < HEAD_DIM, lane >= HEAD_DIM]
        qh, doh, rtot_h = [], [], []
        for p in range(SB_PAIRS):
            q2, rt = q_ref[:, lanes(p)], rt_ref[:, lanes(p)]
            do2b = do_ref[:, lanes(p)].astype(BF16)
            qh += [jnp.where(kp, q2, jnp.zeros_like(q2)) for kp in keep]
            doh += [jnp.where(kp, do2b, jnp.zeros_like(do2b)) for kp in keep]
            rtot_h += [jnp.sum(jnp.where(lane == n * HEAD_DIM, rt, 0.0), axis=1, keepdims=True) for n in range(2)]
        acc[...] = jnp.zeros_like(acc)
        rs[...] = jnp.zeros_like(rs)
        es[...] = jnp.zeros_like(es)

        @pl.when(i == 0)
        def _():
            dk_ref[...] = jnp.zeros_like(dk_ref)
            dv_ref[...] = jnp.zeros_like(dv_ref)

        ule, ult = us_ref[...], ui_ref[...]

        def tile(kb, masked):
            off = pl.multiple_of(kb * tk, tk)
            k2 = [k_ref[pl.ds(off, tk), lanes(p)] for p in range(SB_PAIRS)]
            v2 = [v_ref[pl.ds(off, tk), lanes(p)] for p in range(SB_PAIRS)]
            strict = _sb_masks(i, kb, tq, tk) if masked else None
            s = [_dot(qh[h], k2[h // 2], "nt") for h in range(nh)]
            dw = [_dot(doh[h], v2[h // 2], "nt") for h in range(nh)]
            a, sg, r, pin, w, e, cin = ([None] * nh for _ in range(7))
            for h in range(nh):
                sp = _softplus(s[h])
                a[h] = s[h] - sp
                sg[h] = jnp.exp(a[h])
                r[h] = jnp.where(strict, -sp, 0.0) if masked else -sp
                pin[h] = _dot2(r[h], ule)
            for h in range(nh):
                lw = a[h] + ((rtot_h[h] - rs[h]) - pin[h])
                w[h] = jnp.exp(jnp.where(strict, lw, -1e30) if masked else lw)
                e[h] = w[h] * dw[h]
                cin[h] = _dot2(e[h], ult)
            for p in range(SB_PAIRS):
                dk_t = jnp.zeros((tk, 128), F32)
                dv_t = jnp.zeros((tk, 128), F32)
                for h in (2 * p, 2 * p + 1):
                    dl = e[h] * (1.0 - sg[h]) - (es[h] + cin[h]) * sg[h]
                    dl = (jnp.where(strict, dl, 0.0) if masked else dl).astype(BF16)
                    rs[h] = rs[h] + jnp.sum(r[h], axis=1, keepdims=True)
                    es[h] = es[h] + jnp.sum(e[h], axis=1, keepdims=True)
                    acc[h] = acc[h] + _dot(dl, k2[p])
                    dk_t = dk_t + _dot(dl, qh[h], "tn")
                    dv_t = dv_t + _dot(w[h].astype(BF16), doh[h], "tn")
                dk_ref[pl.ds(off, tk), lanes(p)] += dk_t
                dv_ref[pl.ds(off, tk), lanes(p)] += dv_t

        def step(kb, carry):
            tile(kb, False)
            return carry

        n_off = cnt_ref[pl.program_id(0), i].astype(jnp.int32)
        lax.fori_loop(i - n_off, i, step, 0)
        tile(i, True)
        for p in range(SB_PAIRS):
            dq_ref[:, lanes(p)] = jnp.where(lane < HEAD_DIM, acc[2 * p], acc[2 * p + 1])
        pl.when(at_last)(x_finish)

    blk = lambda rows, imap: pl.BlockSpec((rows, SB_LANES), imap)
    return pl.pallas_call(
        body, name=name, grid=(ngrp, nq),
        in_specs=[blk(tq, lambda j, i: (i, j)), blk(t, lambda j, i: (0, j)), blk(t, lambda j, i: (0, j)),
                  blk(tq, lambda j, i: (i, j)), blk(tq, lambda j, i: (i, ngrp + j)),
                  _fix_spec((2 * tk, tk)), _fix_spec((2 * tk, tk)), pl.BlockSpec(memory_space=pltpu.SMEM), HB],
        out_specs=[blk(tq, lambda j, i: (i, j)), blk(t, lambda j, i: (0, j)), blk(t, lambda j, i: (0, j)), HB],
        out_shape=[jax.ShapeDtypeStruct((t, D_MODEL), F32)] * 3
        + [jax.ShapeDtypeStruct((3,) + csum_b.shape[1:], csum_b.dtype)],
        scratch_shapes=[pltpu.VMEM((nh, tq, 128), F32), pltpu.VMEM((nh, tq, 1), F32), pltpu.VMEM((nh, tq, 1), F32)]
        + EXCHANGE_SEMS,
        compiler_params=_cp(("arbitrary", "arbitrary")),
    )(qs, kn, vb, rtot, dycat, _doubled(cst["ule"], tk), _doubled(cst["ult"], tk), cnt, csum_b)


def _adamw(w, g, m, v, name):
    lead = (1,) * (w.ndim - 2)
    rows, cols = w.shape[-2:]
    fits = [d for d in range(8, rows, 8) if rows % d == 0 and d * cols * 4 <= ADAM_BLOCK_BYTES]
    tr = max(fits) if fits else rows
    c1 = 1.0 - ADAM_B1 ** ADAM_STEP
    c2 = 1.0 - ADAM_B2 ** ADAM_STEP

    def body(w_ref, g_ref, m_ref, v_ref, d_ref, nm_ref, nv_ref):
        gv = g_ref[...]
        nm = ADAM_B1 * m_ref[...] + (1.0 - ADAM_B1) * gv
        nv = ADAM_B2 * v_ref[...] + (1.0 - ADAM_B2) * (gv * gv)
        nm_ref[...] = nm
        nv_ref[...] = nv
        d_ref[...] = -ADAM_LR * ((nm / c1) / (jnp.sqrt(nv / c2) + ADAM_EPS) + ADAM_WD * w_ref[...])

    spec = pl.BlockSpec(lead + (tr, cols), lambda i: (0,) * len(lead) + (i, 0))
    return pl.pallas_call(
        body, name=name, grid=(rows // tr,), in_specs=[spec] * 4, out_specs=[spec] * 3,
        out_shape=[jax.ShapeDtypeStruct(w.shape, F32)] * 3, compiler_params=_cp(("parallel",)),
    )(w, g, m, v)


def _place():
    x, y, c = lax.axis_index("x"), lax.axis_index("y"), lax.axis_index("c")
    chips = [(1 - x, y), (x, 1 - y), (1 - x, 1 - y)]
    return x, y, c, chips


VM = pl.BlockSpec(memory_space=pltpu.VMEM)
HB = pl.BlockSpec(memory_space=pltpu.HBM)


def _gather_all(p_ref, gat_ref, ss, rs):
    x, y, c, _ = _place()
    me = 4 * x + 2 * y + c
    peers = [(x, y, 1 - c), (1 - x, y, c), (x, 1 - y, c), (1 - x, 1 - y, c),
             (1 - x, y, 1 - c), (x, 1 - y, 1 - c), (1 - x, 1 - y, 1 - c)]

    def copy(k, slot, to):
        return pltpu.make_async_remote_copy(src_ref=p_ref, dst_ref=gat_ref.at[slot], send_sem=ss.at[k],
                                            recv_sem=rs.at[k], device_id=to, device_id_type=MESH)

    sends = [copy(k, me, p) for k, p in enumerate(peers)]
    for s in sends:
        s.start()
    gat_ref[me] = p_ref[...]
    for k, p in enumerate(peers):
        copy(k, 4 * p[0] + 2 * p[1] + p[2], p).wait_recv()
    for s in sends:
        s.wait_send()


ALL_SEMS = [pltpu.SemaphoreType.DMA((7,)), pltpu.SemaphoreType.DMA((7,))]


def _small_reduce(pack, name):
    rows = pack.shape[0]

    def body(p_ref, gat_ref, sum_ref, ss, rs):
        _gather_all(p_ref, gat_ref, ss, rs)
        tot = gat_ref[0]
        for b in range(1, 8):
            tot = tot + gat_ref[b]
        sum_ref[...] = tot

    return pl.pallas_call(
        body, name=name, in_specs=[VM], out_specs=[VM, VM],
        out_shape=[jax.ShapeDtypeStruct((8, rows, D_MODEL), F32), jax.ShapeDtypeStruct((rows, D_MODEL), F32)],
        scratch_shapes=ALL_SEMS, compiler_params=_cp(),
    )(pack)


def _prologue(cpack, w_ada, b_shard, wpack_in, name):
    def body(cp_ref, w_ref, b_ref, p_ref, gat_ref, modp_ref, gin_ref, ss1, rs1, ss2, rs2, gss, grs):
        g_start, g_relay, g_finish = _gather_stages(p_ref, gin_ref, gss, grs)
        g_start()
        _gather_all(cp_ref, gat_ref, ss1, rs1)
        x, y, c, chips = _place()
        sh = 2 * x + y
        row = lax.broadcasted_iota(jnp.int32, (8, D_MODEL), 0)
        cv = jnp.zeros((8, D_MODEL), F32)
        for b in range(8):
            cv = jnp.where(row == b, gat_ref[b, 0:8, :], cv)
        cv = cv * _sig(cv)
        modp_ref[sh] = jnp.dot(cv, w_ref[...], precision=lax.Precision.HIGHEST,
                               preferred_element_type=F32) + b_ref[...]

        def copy(k, slot, to):
            return pltpu.make_async_remote_copy(src_ref=modp_ref.at[slot], dst_ref=modp_ref.at[slot],
                                                send_sem=ss2.at[k], recv_sem=rs2.at[k], device_id=to,
                                                device_id_type=MESH)

        sends = [copy(k, sh, (*ch, c)) for k, ch in enumerate(chips)]
        for s in sends:
            s.start()
        for k, ch in enumerate(chips):
            copy(k, 2 * ch[0] + ch[1], (*ch, c)).wait_recv()
        for s in sends:
            s.wait_send()
        g_relay()
        g_finish()

    return pl.pallas_call(
        body, name=name, in_specs=[VM, VM, VM, HB], out_specs=[VM, VM, HB],
        out_shape=[jax.ShapeDtypeStruct((8,) + cpack.shape, F32),
                   jax.ShapeDtypeStruct((N_SHARDS, 8, 6 * D_MODEL // N_SHARDS), F32),
                   jax.ShapeDtypeStruct((N_SHARDS,) + wpack_in.shape, wpack_in.dtype)],
        scratch_shapes=ALL_SEMS + EXCHANGE_SEMS + GATHER_SEMS, compiler_params=_cp(),
    )(cpack, w_ada, b_shard, wpack_in)


def _gather_stages(p_ref, out_ref, ss, rs):
    hf = p_ref.shape[0] // 2
    x, y, c, chips = _place()
    sh = 2 * x + y
    sib = (x, y, 1 - c)
    slots = [2 * ch[0] + ch[1] for ch in chips]

    def half(slot, hc):
        return out_ref.at[slot, pl.ds(hc * hf, hf), :]

    def copy(k, src, slot, hc, to):
        return pltpu.make_async_remote_copy(src_ref=src, dst_ref=half(slot, hc), send_sem=ss.at[k],
                                            recv_sem=rs.at[k], device_id=to, device_id_type=MESH)

    def first():
        return [copy(j, p_ref.at[pl.ds(c * hf, hf), :], sh, c, (*ch, c)) for j, ch in enumerate(chips)]

    def passed():
        return [copy(3 + j, half(slots[j], c), slots[j], c, sib) for j in range(3)]

    def own():
        return pltpu.make_async_remote_copy(src_ref=p_ref, dst_ref=out_ref.at[sh], send_sem=ss.at[6],
                                            recv_sem=rs.at[6], device_id=sib, device_id_type=MESH)

    def start():
        for cp in first() + [own()]:
            cp.start()

    def relay():
        for j, cp in enumerate(passed()):
            copy(j, half(slots[j], c), slots[j], c, (*chips[j], c)).wait_recv()
            cp.start()

    def finish():
        for j in range(3):
            copy(3 + j, half(slots[j], 1 - c), slots[j], 1 - c, sib).wait_recv()
        own().wait()
        for cp in first() + passed():
            cp.wait_send()

    return start, relay, finish


GATHER_SEMS = [pltpu.SemaphoreType.DMA((7,)), pltpu.SemaphoreType.DMA((7,))]


def _swap_stages(g_ref, out_ref, ss, rs):
    hf = g_ref.shape[1] // 2
    x, y, c, _ = _place()

    def copy():
        return pltpu.make_async_remote_copy(
            src_ref=g_ref.at[pl.ds(0, N_SHARDS), pl.ds((1 - c) * hf, hf), :], dst_ref=out_ref,
            send_sem=ss, recv_sem=rs, device_id=(x, y, 1 - c), device_id_type=MESH)

    return (lambda: copy().start()), (lambda: copy().wait())


SWAP_SEMS = [pltpu.SemaphoreType.DMA, pltpu.SemaphoreType.DMA]


def _swap_shape(g):
    return jax.ShapeDtypeStruct((N_SHARDS, g.shape[1] // 2, D_MODEL), g.dtype)


def _sibling_swap(g, name):
    def body(g_ref, out_ref, ss, rs):
        for stage in _swap_stages(g_ref, out_ref, ss, rs):
            stage()

    return pl.pallas_call(
        body, name=name, in_specs=[HB], out_specs=HB, out_shape=_swap_shape(g),
        scratch_shapes=SWAP_SEMS, compiler_params=_cp(),
    )(g)


def _row_tile(rows, width_bytes, cap_bytes):
    fits = [d for d in range(8, rows + 1, 8) if rows % d == 0 and d * width_bytes <= cap_bytes]
    return max(fits)


def _chip_sum(g, got, c_idx, name):
    hf = got.shape[1]
    tr = _row_tile(hf, D_MODEL * 4, 3 << 20)
    nb = hf // tr

    def body(c_ref, a_ref, b_ref, s_ref, sb_ref):
        s = a_ref[...] + b_ref[...]
        s_ref[...] = s
        sb_ref[...] = s.astype(BF16)

    blk = pl.BlockSpec((1, tr, D_MODEL), lambda s, i, c_ref: (s, i, 0))
    return pl.pallas_call(
        body, name=name,
        grid_spec=pltpu.PrefetchScalarGridSpec(
            num_scalar_prefetch=1, grid=(N_SHARDS, nb),
            in_specs=[pl.BlockSpec((1, tr, D_MODEL), lambda s, i, c_ref: (s, c_ref[0] * nb + i, 0)), blk],
            out_specs=[blk, blk]),
        out_shape=[jax.ShapeDtypeStruct((N_SHARDS, hf, D_MODEL), F32),
                   jax.ShapeDtypeStruct((N_SHARDS, hf, D_MODEL), BF16)],
        compiler_params=_cp(("parallel", "parallel")),
    )(c_idx, g, got)


def _exchange_stages(s_ref, out_ref, ss, rs):
    x, y, c, chips = _place()

    def sends():
        return [pltpu.make_async_remote_copy(src_ref=s_ref.at[2 * ch[0] + ch[1]], dst_ref=out_ref.at[k],
                                             send_sem=ss.at[k], recv_sem=rs.at[k], device_id=(*ch, c),
                                             device_id_type=MESH) for k, ch in enumerate(chips)]

    def start():
        for cp in sends():
            cp.start()

    def finish():
        for cp in sends():
            cp.wait()

    return start, finish


EXCHANGE_SEMS = [pltpu.SemaphoreType.DMA((3,)), pltpu.SemaphoreType.DMA((3,))]


def _total_half(s, got, sh_idx, name):
    hf = got.shape[1]
    tr = _row_tile(hf, D_MODEL * 4, 3 << 20)
    nb = hf // tr

    def body(sh_ref, a_ref, r0, r1, r2, o_ref):
        o_ref[...] = ((a_ref[0] + r0[0].astype(F32)) + r1[0].astype(F32)) + r2[0].astype(F32)

    rspec = lambda k: pl.BlockSpec((1, tr, D_MODEL), lambda i, sh_ref: (k, i, 0))
    return pl.pallas_call(
        body, name=name,
        grid_spec=pltpu.PrefetchScalarGridSpec(
            num_scalar_prefetch=1, grid=(nb,),
            in_specs=[pl.BlockSpec((1, tr, D_MODEL), lambda i, sh_ref: (sh_ref[0], i, 0)),
                      rspec(0), rspec(1), rspec(2)],
            out_specs=pl.BlockSpec((tr, D_MODEL), lambda i, sh_ref: (i, 0))),
        out_shape=jax.ShapeDtypeStruct((hf, D_MODEL), F32),
        compiler_params=_cp(("parallel",)),
    )(sh_idx, s, got, got, got)


def _join_halves(tot, name):
    def body(t_ref, out_ref, ss, rs):
        x, y, c, _ = _place()
        cp = pltpu.make_async_remote_copy(src_ref=t_ref, dst_ref=out_ref, send_sem=ss, recv_sem=rs,
                                          device_id=(x, y, 1 - c), device_id_type=MESH)
        cp.start()
        cp.wait()

    return pl.pallas_call(
        body, name=name, in_specs=[HB], out_specs=HB,
        out_shape=jax.ShapeDtypeStruct(tot.shape, F32),
        scratch_shapes=[pltpu.SemaphoreType.DMA, pltpu.SemaphoreType.DMA],
        compiler_params=_cp(),
    )(tot)


def _w_ada_grad(cond, dmod_cols, name):
    def body(c_ref, d_ref, o_ref):
        cv = c_ref[...]
        cv = cv * _sig(cv)
        o_ref[...] = lax.dot_general(cv, d_ref[...], _DN["tn"], precision=lax.Precision.HIGHEST,
                                     preferred_element_type=F32)

    return pl.pallas_call(
        body, name=name, in_specs=[VM, VM], out_specs=VM,
        out_shape=jax.ShapeDtypeStruct((D_MODEL, dmod_cols.shape[1]), F32), compiler_params=_cp(),
    )(cond, dmod_cols)


def _pad_rows(a, rows):
    return jnp.pad(a, ((0, rows - a.shape[0]), (0, 0)))


def _pad_cols(a, cols):
    return jnp.pad(a, ((0, 0), (0, cols - a.shape[1])))


PROJ_SEGMENTS = [(0, 1024, 0), (1024, 2560, 4096), (2560, 2576, 5632), (2576, 5648, 1024)]


def _layout_pieces(lo, hi):
    out = []
    for a, b, p in PROJ_SEGMENTS:
        s, e = max(lo, a), min(hi, b)
        if s < e:
            out.append((p + s - a, p + e - a))
    return out


def _shard_pieces(lo, hi, width):
    out = []
    for s in range(N_SHARDS):
        a, b = max(lo, s * width), min(hi, (s + 1) * width)
        if a < b:
            out.append((s, a - s * width, b - s * width))
    return out


def _unpack_rest(p):
    o = 0
    out = []
    for r in (R_OUT, R_FF, R_FF, R_FF):
        out.append(p[..., o:o + r, :])
        o += r
    return out


def _reduce_tail(csum, got2, shard, ac, tag):
    tot = _total_half(csum, got2, shard.reshape(1).astype(jnp.int32), "rs_total_" + tag)
    other = _join_halves(tot, "rs_join_" + tag)
    return jnp.where(ac == 0, jnp.concatenate([tot, other], axis=0), jnp.concatenate([other, tot], axis=0))


def _reduce_head(gpack, ac, tag):
    got = _sibling_swap(gpack, "rs_sibling_swap_" + tag)
    return _chip_sum(gpack, got, ac.reshape(1).astype(jnp.int32), "rs_chip_sum_" + tag)


def kernel(x, c, w_ada, b_ada, norm1_w, w_in, conv_w, conv_b, dt_bias, a_log, d_skip, ssd_norm_w, q_norm_w, k_norm_w, w_out, norm2_w, w_gate, w_up, w_down, loss_target, m_w_ada, m_b_ada, m_norm1_w, m_w_in, m_conv_w, m_conv_b, m_dt_bias, m_a_log, m_d_skip, m_ssd_norm_w, m_q_norm_w, m_k_norm_w, m_w_out, m_norm2_w, m_w_gate, m_w_up, m_w_down, v_w_ada, v_b_ada, v_norm1_w, v_w_in, v_conv_w, v_conv_b, v_dt_bias, v_a_log, v_d_skip, v_ssd_norm_w, v_q_norm_w, v_k_norm_w, v_w_out, v_norm2_w, v_w_gate, v_w_up, v_w_down):
    cst = _consts()
    ax, ay, ac = lax.axis_index("x"), lax.axis_index("y"), lax.axis_index("c")
    shard = 2 * ax + ay
    me = 4 * ax + 2 * ay + ac
    xs = x[0]
    tgt = loss_target[0]
    w_in_cols = w_in.shape[2]
    conv_cols = conv_w.shape[2]

    tr3 = lambda a: jnp.transpose(a, (0, 2, 1))
    lin = lambda a: tr3(a).reshape(-1, 128)
    unlin = lambda a: tr3(a.reshape(1, w_in_cols, D_MODEL))
    wpack_in = _pad_rows(tr3(w_in.astype(BF16))[0], R_IN)
    wpack_rest = jnp.concatenate([w_out[0], tr3(w_gate)[0], tr3(w_up)[0], w_down[0]], axis=0).astype(BF16)

    cw_flat = _pad_cols(conv_w[0].reshape(1, -1), 2 * D_MODEL).reshape(2, D_MODEL)
    cpack = jnp.concatenate([jnp.broadcast_to(c, (8, D_MODEL)), _pad_rows(cw_flat, 8)], axis=0)
    mod_w = 6 * D_MODEL // N_SHARDS
    b_shard = lax.dynamic_slice(b_ada, (0, shard * mod_w), (1, mod_w))
    gat, modp, gp_in = _prologue(cpack, w_ada[0], b_shard, wpack_in, "prologue")
    c_all = gat[:, 0, :]
    cw = gat[0::2, 8:10, :].reshape(N_SHARDS, 2 * D_MODEL)[:, :4 * conv_cols].reshape(N_SHARDS, 4, conv_cols)
    conv_w_full = jnp.transpose(cw, (1, 0, 2)).reshape(4, D_CONV)
    mod_mine = lax.dynamic_slice(modp, (0, me, 0), (N_SHARDS, 1, mod_w)).reshape(6, D_MODEL)
    mod = _pad_rows(mod_mine, 8)
    w_inp_t = jnp.concatenate(
        [gp_in[s, a:b] for lo, hi, _ in sorted(PROJ_SEGMENTS, key=lambda seg: seg[2])
         for s, a, b in _shard_pieces(lo, hi, w_in_cols)]
        + [jnp.zeros((D_PROJ_PAD - D_IN_PROJ, D_MODEL), BF16)], axis=0)

    pad128 = lambda a: _pad_cols(a, 128)
    dtb, alog = pad128(dt_bias), pad128(a_log)
    dsk = jnp.repeat(d_skip, HEAD_DIM, axis=1)
    qw, kw = jnp.tile(q_norm_w, (1, N_HEADS)), jnp.tile(k_norm_w, (1, N_HEADS))

    h1 = _norm_mod(xs, norm1_w, mod, 0, "norm1")
    proj = _matmul(h1, w_inp_t, "nt", F32, "in_proj")
    u = _conv_fwd(proj, conv_w_full, conv_b, "conv_fwd")
    y_ssd, yn, prev = _ssd_fwd(u, proj, dtb, alog, dsk, ssd_norm_w, cst, "ssd_fwd")
    qs, kn, vb = _qk_fwd(proj, qw, kw, cst, "qk_norm")
    rtot, ycat, cnt, gp_rest = _sb_fwd(qs, kn, vb, wpack_rest, yn, cst, "sb_fwd")
    p_out, p_gate, p_up, p_down = _unpack_rest(gp_rest)
    w_o = p_out.reshape(2 * D_MODEL, D_MODEL)
    w_gu_t = jnp.concatenate([p_gate.reshape(D_FF, D_MODEL), p_up.reshape(D_FF, D_MODEL)], axis=0)
    w_d = p_down.reshape(D_FF, D_MODEL)
    mix = _matmul(ycat, w_o, "nn", F32, "out_proj")
    x1, h2 = _resid_norm(xs, mix, norm2_w, mod, "resid_norm2")
    gu = _matmul(h2, w_gu_t, "nt", BF16, "ffn_in")
    act = _act_fwd(gu, "ffn_act")
    ffn = _matmul(act, w_d, "nn", F32, "ffn_out", tk_cap=1408)
    dffn, dout, dg2, loss8 = _loss_head(x1, ffn, tgt, mod, "loss_head")
    loss = lax.psum(loss8[0, 0], ("x", "y", "c"))

    dact = _matmul(dffn, w_d, "nt", BF16, "d_act")
    g_down = _matmul(act, dffn, "tn", F32, "g_w_down", tm_cap=1408)
    dgu = _act_bwd(dact, gu, "ffn_act_bwd")
    dh2 = _matmul(dgu, w_gu_t, "nn", F32, "d_h2", tk_cap=1408)
    g_gu_t = _matmul(dgu, h2, "tn", F32, "g_w_gu", tm_cap=1408)
    dx1, dmix, acc2 = _norm_bwd(dh2, x1, dout, mix, norm2_w, mod, 3, 2, "norm2_bwd")
    dycat = _matmul(dmix, w_o, "nt", F32, "d_ycat")
    g_out = _matmul(ycat, dmix, "tn", F32, "g_w_out")
    gpack_rest = jnp.concatenate([
        g_out.reshape(N_SHARDS, R_OUT, D_MODEL),
        g_gu_t[:D_FF].reshape(N_SHARDS, R_FF, D_MODEL), g_gu_t[D_FF:].reshape(N_SHARDS, R_FF, D_MODEL),
        g_down.reshape(N_SHARDS, R_FF, D_MODEL)], axis=1)
    du, ddt, dz, acc_ssd, acc16, got_s = _ssd_bwd(u, proj, y_ssd, prev, dycat, dtb, alog, dsk, ssd_norm_w,
                                                  gpack_rest, cst, "ssd_bwd")
    csum_r, csum_rb = _chip_sum(gpack_rest, got_s, ac.reshape(1).astype(jnp.int32), "rs_chip_sum_rest")
    dqs, dkn, dv, got_r = _sb_bwd(qs, kn, vb, rtot, cnt, dycat, csum_rb, cst, "sb_bwd")
    r_out, r_gate, r_up, r_down = _unpack_rest(_reduce_tail(csum_r, got_r, shard, ac, "rest"))
    dq, dk, dvb, acc_qk = _qk_bwd(proj, dqs, dkn, dv, qw, kw, cst, "qk_norm_bwd")
    dxbc, g_conv_w, g_conv_b = _conv_bwd(proj, du, conv_w_full, conv_b, "conv_bwd")
    dproj = jnp.concatenate([dz, dq, dk, dvb, dxbc, ddt], axis=1)
    g_inp_t = _matmul(dproj, h1, "tn", F32, "g_w_in", tm_cap=1920)
    gpack_in = jnp.stack([
        jnp.concatenate([g_inp_t[a:b] for a, b in _layout_pieces(s * w_in_cols, (s + 1) * w_in_cols)]
                        + [jnp.zeros((R_IN - w_in_cols, D_MODEL), F32)], axis=0) for s in range(N_SHARDS)])
    csum_i, csum_ib = _reduce_head(gpack_in, ac, "in")
    dh1, got_i = _matmul(dproj, w_inp_t, "nn", F32, "d_h1", tk_cap=1152, exchange=csum_ib)
    r_in = _reduce_tail(csum_i, got_i, shard, ac, "in")
    grad_x, acc1 = _norm_bwd(dh1, xs, dx1, None, norm1_w, mod, 0, None, "norm1_bwd")

    last = jnp.concatenate([acc_qk[0:1, 0:64], acc_qk[1:2, 0:64], acc16[0:1, 0:16], acc16[1:2, 0:16],
                            acc16[2:3, 0:16]], axis=1)
    spack = jnp.concatenate([
        acc1[0:2], acc2[3:4], acc2[0:2], dg2,
        acc1[2:3], acc2[2:3], acc_ssd[0:1],
        _pad_cols(g_conv_b, 2 * D_MODEL).reshape(2, D_MODEL),
        g_conv_w.reshape(6, D_MODEL),
        _pad_cols(last, D_MODEL)], axis=0)
    sgat, ssum = _small_reduce(_pad_rows(spack, SMALL_ROWS), "gather_small")
    g_b_ada = ssum[0:6].reshape(1, 6 * D_MODEL)
    g_norm1, g_norm2, g_ssdn = ssum[6:7], ssum[7:8], ssum[8:9]
    g_cb = ssum[9:11].reshape(1, 2 * D_MODEL)[:, :D_CONV]
    g_cw = lax.dynamic_slice(ssum[11:17].reshape(4, D_CONV), (0, shard * conv_cols), (4, conv_cols))
    g_qn, g_kn = ssum[17:18, 0:64], ssum[17:18, 64:128]
    g_dtb, g_alog, g_dsk = ssum[17:18, 128:144], ssum[17:18, 144:160], ssum[17:18, 160:176]
    dmod_all = sgat[:, 0:6, :].reshape(8, 6 * D_MODEL)
    g_w_ada = _w_ada_grad(c_all, lax.dynamic_slice(dmod_all, (0, shard * mod_w), (8, mod_w)), "g_w_ada")


    grads = dict(w_ada=g_w_ada, b_ada=g_b_ada, norm1_w=g_norm1, w_in=r_in[:w_in_cols].reshape(-1, 128), conv_w=g_cw,
                 conv_b=g_cb, dt_bias=g_dtb, a_log=g_alog, d_skip=g_dsk, ssd_norm_w=g_ssdn, q_norm_w=g_qn,
                 k_norm_w=g_kn, w_out=r_out, norm2_w=g_norm2, w_gate=r_gate, w_up=r_up, w_down=r_down)
    weights = dict(w_ada=(w_ada, m_w_ada, v_w_ada), b_ada=(b_ada, m_b_ada, v_b_ada),
                   norm1_w=(norm1_w, m_norm1_w, v_norm1_w), w_in=(w_in, m_w_in, v_w_in),
                   conv_w=(conv_w, m_conv_w, v_conv_w), conv_b=(conv_b, m_conv_b, v_conv_b),
                   dt_bias=(dt_bias, m_dt_bias, v_dt_bias), a_log=(a_log, m_a_log, v_a_log),
                   d_skip=(d_skip, m_d_skip, v_d_skip), ssd_norm_w=(ssd_norm_w, m_ssd_norm_w, v_ssd_norm_w),
                   q_norm_w=(q_norm_w, m_q_norm_w, v_q_norm_w), k_norm_w=(k_norm_w, m_k_norm_w, v_k_norm_w),
                   w_out=(w_out, m_w_out, v_w_out), norm2_w=(norm2_w, m_norm2_w, v_norm2_w),
                   w_gate=(w_gate, m_w_gate, v_w_gate), w_up=(w_up, m_w_up, v_w_up),
                   w_down=(w_down, m_w_down, v_w_down))
    views = dict(w_in=(lin, unlin), w_gate=(tr3, tr3), w_up=(tr3, tr3))
    same = lambda a: a
    names = list(weights)
    g_out_l, d_out_l, m_out_l, v_out_l = [], [], [], []
    for n in names:
        view, back = views.get(n, (same, same))
        w, m, v = (view(a) for a in weights[n])
        g = grads[n].reshape(w.shape)
        d, nm, nv = _adamw(w, g, m, v, "adamw_" + n)
        g_out_l.append(back(g))
        d_out_l.append(back(d))
        m_out_l.append(back(nm))
        v_out_l.append(back(nv))
    return (loss, grad_x[None], *g_out_l, *d_out_l, *m_out_l, *v_out_l)
```

```python
import functools

import numpy as np
import jax
import jax.numpy as jnp
from jax import lax
from jax.experimental import pallas as pl
from jax.experimental.pallas import tpu as pltpu

F32, BF16 = jnp.float32, jnp.bfloat16
MESH = pl.DeviceIdType.MESH

D_MODEL = 1024
HEAD_DIM = 64
N_HEADS = 16
D_CONV = 1536
D_FF = 2816
D_IN_PROJ = 5648
D_PROJ_PAD = 5760
CHUNK = 128
SB_TILE = 256
FWD_PAIRS, BWD_PAIRS = 4, 2
SB_DEAD = -105.0
EPS = 1e-6
N_SHARDS = 4
R_IN, R_OUT, R_FF = 1440, 512, 704
SMALL_ROWS = 24

ADAM_LR, ADAM_B1, ADAM_B2, ADAM_EPS, ADAM_WD, ADAM_STEP = 0.001, 0.9, 0.999, 1e-08, 0.01, 10

VMEM_LIMIT = 48 * 1024 * 1024
ADAM_BLOCK_BYTES = 3 * 512 * 1024

_DN = {"nn": (((1,), (0,)), ((), ())), "nt": (((1,), (1,)), ((), ())), "tn": (((0,), (0,)), ((), ()))}


def _dot(a, b, dims="nn"):
    return lax.dot_general(a, b, _DN[dims], preferred_element_type=F32)


def _pieces(x, n):
    out = []
    for _ in range(n - 1):
        hi = lax.bitcast_convert_type(lax.bitcast_convert_type(x, jnp.int32) & jnp.int32(-65536), F32)
        out.append(hi.astype(BF16))
        x = x - hi
    out.append(x.astype(BF16))
    return out


def _dotx_r(x, b_exact, n=3):
    return _dot(jnp.concatenate(_pieces(x, n), axis=1), jnp.concatenate([b_exact] * n, axis=0))


def _dotx_l(a_exact, x, n=3):
    return _dot(jnp.concatenate([a_exact] * n, axis=1), jnp.concatenate(_pieces(x, n), axis=0))


def _dot2(x, b2):
    return _dot(jnp.concatenate(_pieces(x, 2), axis=1), b2)


def _sig(x):
    return 1.0 / (1.0 + jnp.exp(-x))


def _softplus(x):
    return jnp.maximum(x, 0.0) + jnp.log(1.0 + jnp.exp(-jnp.abs(x)))


def _cp(sem=None, vmem=VMEM_LIMIT):
    return pltpu.CompilerParams(dimension_semantics=sem, vmem_limit_bytes=vmem)


def _colsum(x):
    return jnp.sum(x, axis=0, keepdims=True)


def _consts():
    ch = np.arange(D_MODEL)
    expand = (np.arange(128)[:, None] == (ch // HEAD_DIM)[None, :]).astype(np.float32)
    fold = (ch[:, None] % HEAD_DIM == np.arange(128)[None, :]).astype(np.float32)
    i = np.arange(CHUNK)
    tril = (i[:, None] >= i[None, :]).astype(np.float32)
    j = np.arange(SB_TILE)
    ustrict = (j[:, None] > j[None, :]).astype(np.float32)
    ule = (j[:, None] <= j[None, :]).astype(np.float32)
    ult = (j[:, None] < j[None, :]).astype(np.float32)
    c = lambda a: jnp.asarray(a, BF16)
    return dict(expand=c(expand), hsum=c(expand.T), fold=c(fold), tril=c(tril), triu=c(tril.T),
                ustrict=ustrict, ule=ule, ult=ult)


def _doubled(tri, tk):
    b = tri[:tk, :tk]
    return jnp.asarray(np.concatenate([b, b], axis=0), BF16)


def _pick(n, cap):
    best = 128
    for t in range(128, min(n, cap) + 1, 128):
        if n % t == 0:
            best = t
    return n if n <= cap else best


def _matmul(a, b, dims, out_dtype, name, tm_cap=1024, tn_cap=2048, tk_cap=1024, exchange=None):
    if dims == "nn":
        (m, k), (_, n) = a.shape, b.shape
    elif dims == "nt":
        (m, k), (n, _) = a.shape, b.shape
    else:
        (k, m), (_, n) = a.shape, b.shape
    tm, tn, tk = _pick(m, tm_cap), _pick(n, tn_cap), _pick(k, tk_cap)
    nk = k // tk
    a_spec = (pl.BlockSpec((tk, tm), lambda i, j, kk: (kk, i)) if dims == "tn"
              else pl.BlockSpec((tm, tk), lambda i, j, kk: (i, kk)))
    b_spec = (pl.BlockSpec((tn, tk), lambda i, j, kk: (j, kk)) if dims == "nt"
              else pl.BlockSpec((tk, tn), lambda i, j, kk: (kk, j)))

    grid = (m // tm, n // tn, nk)

    def body(a_ref, b_ref, *rest):
        if exchange is None:
            o_ref, acc_ref = rest
        else:
            xs_ref, o_ref, xr_ref, acc_ref, xss, xrs = rest
            ids = [pl.program_id(d) for d in range(3)]
            x_start, x_finish = _exchange_stages(xs_ref, xr_ref, xss, xrs)
            pl.when(functools.reduce(jnp.logical_and, [p == 0 for p in ids]))(x_start)
        kk = pl.program_id(2)
        part = _dot(a_ref[...], b_ref[...], dims)
        if nk == 1:
            o_ref[...] = part.astype(out_dtype)
        else:
            @pl.when(kk == 0)
            def _():
                acc_ref[...] = part

            @pl.when(kk > 0)
            def _():
                acc_ref[...] += part

            @pl.when(kk == nk - 1)
            def _():
                o_ref[...] = acc_ref[...].astype(out_dtype)
        if exchange is not None:
            pl.when(functools.reduce(jnp.logical_and, [p == g - 1 for p, g in zip(ids, grid)]))(x_finish)

    in_specs = [a_spec, b_spec]
    out_specs = [pl.BlockSpec((tm, tn), lambda i, j, kk: (i, j))]
    out_shape = [jax.ShapeDtypeStruct((m, n), out_dtype)]
    scratch = [pltpu.VMEM((tm, tn) if nk > 1 else (8, 128), F32)]
    args = [a, b]
    if exchange is not None:
        in_specs.append(HB)
        out_specs.append(HB)
        out_shape.append(jax.ShapeDtypeStruct((3,) + exchange.shape[1:], exchange.dtype))
        scratch += EXCHANGE_SEMS
        args.append(exchange)
    out = pl.pallas_call(
        body, name=name, grid=grid, in_specs=in_specs, out_specs=out_specs, out_shape=out_shape,
        scratch_shapes=scratch,
        compiler_params=_cp(("parallel", "parallel", "arbitrary") if exchange is None else ("arbitrary",) * 3),
    )(*args)
    return out[0] if exchange is None else out


def _row_spec(tm, width=D_MODEL, col=0):
    return pl.BlockSpec((tm, width), lambda i: (i, col))


def _fix_spec(shape):
    return pl.BlockSpec(shape, lambda *_: (0,) * len(shape))


def _norm_mod(x, nw, mod, row_sh, name):
    t = x.shape[0]
    tm = min(t, 512)

    def body(x_ref, nw_ref, mod_ref, h_ref):
        xv = x_ref[...]
        r = lax.rsqrt(jnp.mean(xv * xv, axis=-1, keepdims=True) + EPS)
        sh = mod_ref[row_sh:row_sh + 1, :]
        sc = mod_ref[row_sh + 1:row_sh + 2, :]
        h_ref[...] = (xv * r * nw_ref[...] * (1.0 + sc) + sh).astype(BF16)

    return pl.pallas_call(
        body, name=name, grid=(t // tm,),
        in_specs=[_row_spec(tm), _fix_spec((1, D_MODEL)), _fix_spec((8, D_MODEL))],
        out_specs=_row_spec(tm), out_shape=jax.ShapeDtypeStruct((t, D_MODEL), BF16),
        compiler_params=_cp(("parallel",)),
    )(x, nw, mod)


def _resid_norm(x, mix, nw, mod, name):
    t = x.shape[0]
    tm = min(t, 512)

    def body(x_ref, mix_ref, nw_ref, mod_ref, x1_ref, h_ref):
        x1 = x_ref[...] + mod_ref[2:3, :] * mix_ref[...]
        x1_ref[...] = x1
        r = lax.rsqrt(jnp.mean(x1 * x1, axis=-1, keepdims=True) + EPS)
        h_ref[...] = (x1 * r * nw_ref[...] * (1.0 + mod_ref[4:5, :]) + mod_ref[3:4, :]).astype(BF16)

    return pl.pallas_call(
        body, name=name, grid=(t // tm,),
        in_specs=[_row_spec(tm), _row_spec(tm), _fix_spec((1, D_MODEL)), _fix_spec((8, D_MODEL))],
        out_specs=[_row_spec(tm), _row_spec(tm)],
        out_shape=[jax.ShapeDtypeStruct((t, D_MODEL), F32), jax.ShapeDtypeStruct((t, D_MODEL), BF16)],
        compiler_params=_cp(("parallel",)),
    )(x, mix, nw, mod)


def _act_fwd(gu, name):
    t = gu.shape[0]
    tm, tn = min(t, 512), D_FF // 2
    nb = D_FF // tn

    def body(g_ref, u_ref, a_ref):
        g = g_ref[...].astype(F32)
        a_ref[...] = (g * _sig(g) * u_ref[...].astype(F32)).astype(BF16)

    return pl.pallas_call(
        body, name=name, grid=(t // tm, nb),
        in_specs=[pl.BlockSpec((tm, tn), lambda i, j: (i, j)), pl.BlockSpec((tm, tn), lambda i, j: (i, j + nb))],
        out_specs=pl.BlockSpec((tm, tn), lambda i, j: (i, j)),
        out_shape=jax.ShapeDtypeStruct((t, D_FF), BF16),
        compiler_params=_cp(("parallel", "parallel")),
    )(gu, gu)


def _act_bwd(dact, gu, name):
    t = gu.shape[0]
    tm = min(t, 256)

    def body(d_ref, g_ref, u_ref, o_ref):
        g, d = g_ref[...].astype(F32), d_ref[...].astype(F32)
        s = _sig(g)
        o_ref[:, 0:D_FF] = (d * u_ref[...].astype(F32) * s * (1.0 + g * (1.0 - s))).astype(BF16)
        o_ref[:, D_FF:2 * D_FF] = (d * g * s).astype(BF16)

    return pl.pallas_call(
        body, name=name, grid=(t // tm,),
        in_specs=[pl.BlockSpec((tm, D_FF), lambda i: (i, 0)), pl.BlockSpec((tm, D_FF), lambda i: (i, 0)),
                  pl.BlockSpec((tm, D_FF), lambda i: (i, 1))],
        out_specs=pl.BlockSpec((tm, 2 * D_FF), lambda i: (i, 0)),
        out_shape=jax.ShapeDtypeStruct((t, 2 * D_FF), BF16),
        compiler_params=_cp(("parallel",)),
    )(dact, gu, gu)


def _loss_head(x1, ffn, tgt, mod, name):
    t = x1.shape[0]
    tm = min(t, 512)

    def body(x1_ref, f_ref, t_ref, mod_ref, dffn_ref, dout_ref, dg2_ref, loss_ref):
        i = pl.program_id(0)
        g2 = mod_ref[5:6, :]
        f = f_ref[...]
        err = x1_ref[...] + g2 * f - t_ref[...]
        dout = err * (1.0 / D_MODEL)
        dout_ref[...] = dout
        dffn_ref[...] = (dout * g2).astype(BF16)
        part = jnp.zeros((8, 128), F32) + 0.5 * jnp.sum(jnp.mean(err * err, axis=-1, keepdims=True))

        @pl.when(i == 0)
        def _():
            dg2_ref[...] = _colsum(dout * f)
            loss_ref[...] = part

        @pl.when(i > 0)
        def _():
            dg2_ref[...] += _colsum(dout * f)
            loss_ref[...] += part

    return pl.pallas_call(
        body, name=name, grid=(t // tm,),
        in_specs=[_row_spec(tm), _row_spec(tm), _row_spec(tm), _fix_spec((8, D_MODEL))],
        out_specs=[_row_spec(tm), _row_spec(tm), _fix_spec((1, D_MODEL)), _fix_spec((8, 128))],
        out_shape=[jax.ShapeDtypeStruct((t, D_MODEL), BF16), jax.ShapeDtypeStruct((t, D_MODEL), F32),
                   jax.ShapeDtypeStruct((1, D_MODEL), F32), jax.ShapeDtypeStruct((8, 128), F32)],
        compiler_params=_cp(("arbitrary",)),
    )(x1, ffn, tgt, mod)


def _norm_bwd(dh, xin, dres, aux, nw, mod, row_sh, gate_row, name):
    t = xin.shape[0]
    tm = min(t, 512)
    with_gate = gate_row is not None

    def body(*refs):
        if with_gate:
            dh_ref, x_ref, dr_ref, aux_ref, nw_ref, mod_ref, dx_ref, dg_ref, acc_ref = refs
        else:
            dh_ref, x_ref, dr_ref, nw_ref, mod_ref, dx_ref, acc_ref = refs
        i = pl.program_id(0)
        xv, dhv = x_ref[...], dh_ref[...]
        r = lax.rsqrt(jnp.mean(xv * xv, axis=-1, keepdims=True) + EPS)
        xn = xv * r
        nwv = nw_ref[...]
        sc1 = 1.0 + mod_ref[row_sh + 1:row_sh + 2, :]
        dxn = dhv * (nwv * sc1)
        dx = dr_ref[...] + r * (dxn - xn * jnp.mean(dxn * xn, axis=-1, keepdims=True))
        dx_ref[...] = dx
        dhx = dhv * xn
        rows = [_colsum(dhv), _colsum(dhx * nwv), _colsum(dhx * sc1)]
        if with_gate:
            dg_ref[...] = (dx * mod_ref[gate_row:gate_row + 1, :]).astype(BF16)
            rows.append(_colsum(dx * aux_ref[...]))

        @pl.when(i == 0)
        def _():
            acc_ref[...] = jnp.zeros_like(acc_ref)

        for k, v in enumerate(rows):
            acc_ref[k:k + 1, :] += v

    ins = [dh, xin, dres] + ([aux] if with_gate else []) + [nw, mod]
    in_specs = [_row_spec(tm)] * (4 if with_gate else 3) + [_fix_spec((1, D_MODEL)), _fix_spec((8, D_MODEL))]
    out_specs = [_row_spec(tm)] + ([_row_spec(tm)] if with_gate else []) + [_fix_spec((8, D_MODEL))]
    out_shape = ([jax.ShapeDtypeStruct((t, D_MODEL), F32)]
                 + ([jax.ShapeDtypeStruct((t, D_MODEL), BF16)] if with_gate else [])
                 + [jax.ShapeDtypeStruct((8, D_MODEL), F32)])
    return pl.pallas_call(
        body, name=name, grid=(t // tm,), in_specs=in_specs, out_specs=out_specs, out_shape=out_shape,
        compiler_params=_cp(("arbitrary",)),
    )(*ins)


XBC_COL0 = 4096 // 128
DT_COL = 5632 // 128


def _conv_pre(xv, w_ref, b_ref):
    t = xv.shape[0]
    row = lax.broadcasted_iota(jnp.int32, xv.shape, 0)
    pre = xv * w_ref[3:4, :] + b_ref[...]
    shifted = []
    for k in range(3):
        s = 3 - k
        xs = jnp.where(row >= s, pltpu.roll(xv, s, 0), 0.0)
        shifted.append(xs)
        pre = pre + xs * w_ref[k:k + 1, :]
    return pre, shifted, row, t


def _conv_fwd(proj, conv_w, conv_b, name):
    t = proj.shape[0]

    def body(x_ref, w_ref, b_ref, u_ref):
        pre, _, _, _ = _conv_pre(x_ref[...], w_ref, b_ref)
        u_ref[...] = pre * _sig(pre)

    return pl.pallas_call(
        body, name=name, grid=(D_CONV // 128,),
        in_specs=[pl.BlockSpec((t, 128), lambda j: (0, XBC_COL0 + j)), pl.BlockSpec((4, 128), lambda j: (0, j)),
                  pl.BlockSpec((1, 128), lambda j: (0, j))],
        out_specs=pl.BlockSpec((t, 128), lambda j: (0, j)),
        out_shape=jax.ShapeDtypeStruct((t, D_CONV), F32),
        compiler_params=_cp(("parallel",)),
    )(proj, conv_w, conv_b)


def _conv_bwd(proj, du, conv_w, conv_b, name):
    t = proj.shape[0]

    def body(x_ref, du_ref, w_ref, b_ref, dx_ref, dw_ref, db_ref):
        pre, shifted, row, _ = _conv_pre(x_ref[...], w_ref, b_ref)
        s = _sig(pre)
        dpre = du_ref[...] * s * (1.0 + pre * (1.0 - s))
        db_ref[...] = _colsum(dpre)
        dx = dpre * w_ref[3:4, :]
        dw_ref[3:4, :] = _colsum(dpre * x_ref[...])
        for k in range(3):
            sft = 3 - k
            dw_ref[k:k + 1, :] = _colsum(dpre * shifted[k])
            back = jnp.where(row < t - sft, pltpu.roll(dpre, t - sft, 0), 0.0)
            dx = dx + back * w_ref[k:k + 1, :]
        dx_ref[...] = dx.astype(BF16)

    return pl.pallas_call(
        body, name=name, grid=(D_CONV // 128,),
        in_specs=[pl.BlockSpec((t, 128), lambda j: (0, XBC_COL0 + j)), pl.BlockSpec((t, 128), lambda j: (0, j)),
                  pl.BlockSpec((4, 128), lambda j: (0, j)), pl.BlockSpec((1, 128), lambda j: (0, j))],
        out_specs=[pl.BlockSpec((t, 128), lambda j: (0, j)), pl.BlockSpec((4, 128), lambda j: (0, j)),
                   pl.BlockSpec((1, 128), lambda j: (0, j))],
        out_shape=[jax.ShapeDtypeStruct((t, D_CONV), BF16), jax.ShapeDtypeStruct((4, D_CONV), F32),
                   jax.ShapeDtypeStruct((1, D_CONV), F32)],
        compiler_params=_cp(("parallel",)),
    )(proj, du, conv_w, conv_b)


def _ssd_common(dtraw_ref, dtb_ref, alog_ref, tril, expand, l_s, lt_s):
    lane = lax.broadcasted_iota(jnp.int32, (1, 128), 1)
    dt = _softplus(dtraw_ref[...] + dtb_ref[...])
    a = jnp.where(lane < N_HEADS, -jnp.exp(alog_ref[...]), 0.0)
    lcs = _dotx_l(tril, dt * a)
    l_s[...] = lcs
    lt_s[...] = lcs.T
    llast = l_s[CHUNK - 1:CHUNK, :]
    ea = _dotx_r(jnp.exp(lcs), expand, n=2)
    ds = _dotx_r(jnp.exp(llast - lcs), expand, n=2)
    dtx = _dotx_r(dt, expand, n=2)
    return dt, a, lcs, llast, ea, ds, dtx


def _head_col(lcs, h):
    lane = lax.broadcasted_iota(jnp.int32, lcs.shape, 1)
    return jnp.sum(jnp.where(lane == h, lcs, 0.0), axis=1, keepdims=True)


def _decay(lcs, lt_s, h, causal):
    seg = _head_col(lcs, h) - lt_s[h:h + 1, :]
    return jnp.exp(jnp.where(causal, seg, -1e30))


def _ssd_fwd(u, proj, dtb, alog, dsk, nw, cst, name):
    t = u.shape[0]
    nc = t // CHUNK

    def body(xs_ref, b_ref, c_ref, dtraw_ref, z_ref, dtb_ref, alog_ref, dsk_ref, nw_ref, tril_ref, exp_ref,
             y_ref, yn_ref, prev_ref, carry, l_s, lt_s, yd_s):
        i = pl.program_id(0)

        @pl.when(i == 0)
        def _():
            carry[...] = jnp.zeros_like(carry)

        expand = exp_ref[...]
        dt, a, lcs, llast, ea, ds, dtx = _ssd_common(dtraw_ref, dtb_ref, alog_ref, tril_ref[...], expand, l_s, lt_s)
        xs = xs_ref[...]
        xg = xs * dtx
        xgb = xg.astype(BF16)
        xgd = (xg * ds).astype(BF16)
        prev = carry[...]
        prev_ref[0] = prev
        prevb = prev.astype(BF16)
        ri = lax.broadcasted_iota(jnp.int32, (CHUNK, CHUNK), 0)
        ci = lax.broadcasted_iota(jnp.int32, (CHUNK, CHUNK), 1)
        causal = ri >= ci
        lane = lax.broadcasted_iota(jnp.int32, (1, 128), 1)
        new_states, yoff = [], []
        for g in range(2):
            bg = b_ref[:, g * 128:(g + 1) * 128].astype(BF16)
            cg = c_ref[:, g * 128:(g + 1) * 128].astype(BF16)
            sc = _dot(cg, bg, "nt")
            gs = slice(g * 512, (g + 1) * 512)
            new_states.append(_dot(bg, xgd[:, gs], "tn"))
            yoff.append(_dot(cg, prevb[:, gs]))
            for pr in range(4):
                col = g * 512 + pr * 128
                xp = xgb[:, col:col + 128]
                acc = jnp.zeros((CHUNK, 128), F32)
                for half in range(2):
                    h = g * 8 + pr * 2 + half
                    m = (sc * _decay(lcs, lt_s, h, causal)).astype(BF16)
                    keep = (lane < HEAD_DIM) if half == 0 else (lane >= HEAD_DIM)
                    acc = acc + _dot(m, jnp.where(keep, xp, jnp.zeros_like(xp)))
                yd_s[:, col:col + 128] = acc
        y = yd_s[...] + jnp.concatenate(yoff, axis=1) * ea + xs * dsk_ref[...]
        y_ref[...] = y
        carry[...] = prev * jnp.max(_dotx_r(jnp.exp(llast) + jnp.zeros((8, 128), F32), expand, n=2), axis=0, keepdims=True) \
            + jnp.concatenate(new_states, axis=1)
        z = z_ref[...]
        yz = y * (z * _sig(z))
        nwv = nw_ref[...]
        for g in range(2):
            gs = slice(g * 512, (g + 1) * 512)
            v = yz[:, gs]
            r = lax.rsqrt(jnp.mean(v * v, axis=-1, keepdims=True) + EPS)
            yn_ref[:, gs] = (v * r * nwv[:, gs]).astype(BF16)

    row = lambda w, col: pl.BlockSpec((CHUNK, w), lambda i: (i, col))
    return pl.pallas_call(
        body, name=name, grid=(nc,),
        in_specs=[row(1024, 0), row(256, 4), row(256, 5), row(128, DT_COL), row(1024, 0),
                  _fix_spec((1, 128)), _fix_spec((1, 128)), _fix_spec((1, D_MODEL)), _fix_spec((1, D_MODEL)),
                  _fix_spec((CHUNK, CHUNK)), _fix_spec((128, D_MODEL))],
        out_specs=[row(1024, 0), row(1024, 0), pl.BlockSpec((1, 128, D_MODEL), lambda i: (i, 0, 0))],
        out_shape=[jax.ShapeDtypeStruct((t, D_MODEL), F32), jax.ShapeDtypeStruct((t, 2 * D_MODEL), BF16),
                   jax.ShapeDtypeStruct((nc, 128, D_MODEL), F32)],
        scratch_shapes=[pltpu.VMEM((128, D_MODEL), F32), pltpu.VMEM((128, 128), F32), pltpu.VMEM((128, 128), F32),
                        pltpu.VMEM((CHUNK, D_MODEL), F32)],
        compiler_params=_cp(("arbitrary",)),
    )(u, u, u, proj, proj, dtb, alog, dsk, nw, cst["tril"], cst["expand"])


def _ssd_bwd(u, proj, y, prev, dycat, dtb, alog, dsk, nw, gpack, cst, name):
    t = u.shape[0]
    nc = t // CHUNK

    def body(xs_ref, b_ref, c_ref, dtraw_ref, z_ref, y_ref, prev_ref, dyn_ref, dtb_ref, alog_ref, dsk_ref, nw_ref,
             tril_ref, triu_ref, exp_ref, hs_ref, g_ref,
             du_ref, ddt_ref, dz_ref, acc_ref, acc16_ref, got_ref, dcarry, l_s, lt_s, dxg_s, wss, wrs):
        i = pl.program_id(0)
        w_start, w_finish = _swap_stages(g_ref, got_ref, wss, wrs)
        pl.when(i == 0)(w_start)

        @pl.when(i == 0)
        def _():
            dcarry[...] = jnp.zeros_like(dcarry)
            acc_ref[...] = jnp.zeros_like(acc_ref)
            acc16_ref[...] = jnp.zeros_like(acc16_ref)

        expand, hsum = exp_ref[...], hs_ref[...]
        dt, a, lcs, llast, ea, ds, dtx = _ssd_common(dtraw_ref, dtb_ref, alog_ref, tril_ref[...], expand, l_s, lt_s)
        xs = xs_ref[...]
        xg = xs * dtx
        xgb = xg.astype(BF16)
        xgdf = xg * ds
        xgd = xgdf.astype(BF16)
        dsk_v, nwv = dsk_ref[...], nw_ref[...]
        z, y = z_ref[...], y_ref[...]
        sz = _sig(z)
        silz = z * sz
        yz = y * silz
        dyn = dyn_ref[...].astype(F32)
        dyz_parts, dnw_parts = [], []
        for g in range(2):
            gs = slice(g * 512, (g + 1) * 512)
            v = yz[:, gs]
            r = lax.rsqrt(jnp.mean(v * v, axis=-1, keepdims=True) + EPS)
            yhat = v * r
            dnw_parts.append(_colsum(dyn[:, gs] * yhat))
            dw = dyn[:, gs] * nwv[:, gs]
            dyz_parts.append(r * (dw - yhat * jnp.mean(dw * yhat, axis=-1, keepdims=True)))
        dyz = jnp.concatenate(dyz_parts, axis=1)
        dy = dyz * silz
        dz_ref[...] = (dyz * y * (sz * (1.0 + z * (1.0 - sz)))).astype(BF16)
        acc_ref[0:1, :] += jnp.concatenate(dnw_parts, axis=1)
        acc_ref[1:2, :] += _colsum(dy * xs)
        dyb = dy.astype(BF16)
        dq = (dy * ea).astype(BF16)
        dcar = dcarry[...]
        dcarb = dcar.astype(BF16)
        prev = prev_ref[0]
        prevb = prev.astype(BF16)
        ri = lax.broadcasted_iota(jnp.int32, (CHUNK, CHUNK), 0)
        ci = lax.broadcasted_iota(jnp.int32, (CHUNK, CHUNK), 1)
        causal = ri >= ci
        lane = lax.broadcasted_iota(jnp.int32, (1, 128), 1)
        dprev, dxgd, yoff = [], [], []
        dl_l = jnp.zeros((CHUNK, CHUNK), F32)
        dl_s = jnp.zeros((CHUNK, CHUNK), F32)
        for g in range(2):
            gs = slice(g * 512, (g + 1) * 512)
            bg = b_ref[:, g * 128:(g + 1) * 128].astype(BF16)
            cg = c_ref[:, g * 128:(g + 1) * 128].astype(BF16)
            sc = _dot(cg, bg, "nt")
            yoff.append(_dot(cg, prevb[:, gs]))
            dcg = _dot(dq[:, gs], prevb[:, gs], "nt")
            dprev.append(_dot(cg, dq[:, gs], "tn"))
            dbg = _dot(xgd[:, gs], dcarb[:, gs], "nt")
            dxgd.append(_dot(bg, dcarb[:, gs]))
            dsc = jnp.zeros((CHUNK, CHUNK), F32)
            for pr in range(4):
                col = g * 512 + pr * 128
                xp = xgb[:, col:col + 128]
                dyp = dyb[:, col:col + 128]
                acc = jnp.zeros((CHUNK, 128), F32)
                for half in range(2):
                    h = g * 8 + pr * 2 + half
                    dec = _decay(lcs, lt_s, h, causal)
                    mf = sc * dec
                    keep = (lane < HEAD_DIM) if half == 0 else (lane >= HEAD_DIM)
                    dyh = jnp.where(keep, dyp, jnp.zeros_like(dyp))
                    dm = _dot(dyh, xp, "nt")
                    acc = acc + _dot(mf.astype(BF16), dyh, "tn")
                    dsc = dsc + dm * dec
                    gm = dm * mf
                    dl_l = dl_l + jnp.where(ci == h, jnp.sum(gm, axis=1, keepdims=True), 0.0)
                    dl_s = dl_s + jnp.where(ri == h, jnp.sum(gm, axis=0, keepdims=True), 0.0)
                dxg_s[:, col:col + 128] = acc
            dscb = dsc.astype(BF16)
            dcg = dcg + _dot(dscb, bg)
            dbg = dbg + _dot(dscb, cg, "tn")
            du_ref[:, 1024 + g * 128:1024 + (g + 1) * 128] = dbg
            du_ref[:, 1280 + g * 128:1280 + (g + 1) * 128] = dcg
        dxgd = jnp.concatenate(dxgd, axis=1)
        dxg = dxg_s[...] + dxgd * ds
        du_ref[:, 0:1024] = dy * dsk_v + dxg * dtx
        hs1 = _dotx_r(dxg * xs, hsum, n=2)
        yoff = jnp.concatenate(yoff, axis=1) * ea
        dl = dl_l - dl_s.T + _dotx_r(dy * yoff - xgdf * dxgd, hsum, n=2)
        rows8 = lax.broadcasted_iota(jnp.int32, (8, D_MODEL), 0)
        two = jnp.where(rows8 == 0, _colsum(dxgd * xgdf), jnp.where(rows8 == 1, _colsum(dcar * prev), 0.0))
        two = _dotx_r(two, hsum)
        r8 = lax.broadcasted_iota(jnp.int32, (8, 128), 0)
        dllast = _colsum(jnp.where(r8 == 0, two, 0.0)) + _colsum(jnp.where(r8 == 1, two, 0.0)) * jnp.exp(llast)
        rowi = lax.broadcasted_iota(jnp.int32, (CHUNK, 128), 0)
        dl = dl + jnp.where(rowi == CHUNK - 1, dllast, 0.0)
        dadt = _dotx_l(triu_ref[...], dl)
        ddt = dadt * a + hs1
        draw = ddt * _sig(dtraw_ref[...] + dtb_ref[...])
        ddt_ref[...] = draw.astype(BF16)
        acc16_ref[0:1, :] += _colsum(draw)
        acc16_ref[1:2, :] += _colsum(dadt * dt) * a
        dcarry[...] = dcar * jnp.max(_dotx_r(jnp.exp(llast) + jnp.zeros((8, 128), F32), expand, n=2), axis=0, keepdims=True) \
            + jnp.concatenate(dprev, axis=1)

        @pl.when(i == nc - 1)
        def _():
            hd = _dotx_r(acc_ref[...], hsum)
            acc16_ref[2:3, :] = _colsum(jnp.where(lax.broadcasted_iota(jnp.int32, (8, 128), 0) == 1, hd, 0.0))

        pl.when(i == nc - 1)(w_finish)

    rev = lambda w, col: pl.BlockSpec((CHUNK, w), lambda i: (nc - 1 - i, col))
    return pl.pallas_call(
        body, name=name, grid=(nc,),
        in_specs=[rev(1024, 0), rev(256, 4), rev(256, 5), rev(128, DT_COL), rev(1024, 0), rev(1024, 0),
                  pl.BlockSpec((1, 128, D_MODEL), lambda i: (nc - 1 - i, 0, 0)), rev(1024, 0),
                  _fix_spec((1, 128)), _fix_spec((1, 128)), _fix_spec((1, D_MODEL)), _fix_spec((1, D_MODEL)),
                  _fix_spec((CHUNK, CHUNK)), _fix_spec((CHUNK, CHUNK)), _fix_spec((128, D_MODEL)),
                  _fix_spec((D_MODEL, 128)), HB],
        out_specs=[rev(D_CONV, 0), rev(128, 0), rev(1024, 0), _fix_spec((8, D_MODEL)), _fix_spec((8, 128)), HB],
        out_shape=[jax.ShapeDtypeStruct((t, D_CONV), F32), jax.ShapeDtypeStruct((t, 128), BF16),
                   jax.ShapeDtypeStruct((t, D_MODEL), BF16), jax.ShapeDtypeStruct((8, D_MODEL), F32),
                   jax.ShapeDtypeStruct((8, 128), F32), _swap_shape(gpack)],
        scratch_shapes=[pltpu.VMEM((128, D_MODEL), F32), pltpu.VMEM((128, 128), F32), pltpu.VMEM((128, 128), F32),
                        pltpu.VMEM((CHUNK, D_MODEL), F32)] + SWAP_SEMS,
        compiler_params=_cp(("arbitrary",)),
    )(u, u, u, proj, proj, y, prev, dycat, dtb, alog, dsk, nw,
      cst["tril"], cst["triu"], cst["expand"], cst["hsum"], gpack)


def _head_rms(v, hsum, expand):
    ms = _dotx_r(v * v, hsum, n=2) * (1.0 / HEAD_DIM)
    return _dotx_r(lax.rsqrt(ms + EPS), expand, n=2)


def _qk_fwd(proj, qw, kw, cst, name):
    t = proj.shape[0]
    tm = min(t, 256)
    scale = HEAD_DIM ** -0.5

    def body(q_ref, k_ref, v_ref, qw_ref, kw_ref, hs_ref, exp_ref, qs_ref, kn_ref, vb_ref):
        hsum, expand = hs_ref[...], exp_ref[...]
        q, k = q_ref[...], k_ref[...]
        qs_ref[...] = (q * _head_rms(q, hsum, expand) * qw_ref[...] * scale).astype(BF16)
        kn_ref[...] = (k * _head_rms(k, hsum, expand) * kw_ref[...]).astype(BF16)
        vb_ref[...] = v_ref[...].astype(BF16)

    return pl.pallas_call(
        body, name=name, grid=(t // tm,),
        in_specs=[_row_spec(tm, col=1), _row_spec(tm, col=2), _row_spec(tm, col=3),
                  _fix_spec((1, D_MODEL)), _fix_spec((1, D_MODEL)), _fix_spec((D_MODEL, 128)),
                  _fix_spec((128, D_MODEL))],
        out_specs=[_row_spec(tm)] * 3, out_shape=[jax.ShapeDtypeStruct((t, D_MODEL), BF16)] * 3,
        compiler_params=_cp(("parallel",)),
    )(proj, proj, proj, qw, kw, cst["hsum"], cst["expand"])


def _qk_bwd(proj, dqs, dkn, dv, qw, kw, cst, name):
    t = proj.shape[0]
    tm = min(t, 256)
    scale = HEAD_DIM ** -0.5

    def body(q_ref, k_ref, dq_ref, dk_ref, dv_ref, qw_ref, kw_ref, hs_ref, exp_ref, fold_ref,
             oq_ref, ok_ref, ov_ref, dw_ref):
        i = pl.program_id(0)
        hsum, expand = hs_ref[...], exp_ref[...]
        rows8 = lax.broadcasted_iota(jnp.int32, (8, D_MODEL), 0)
        sums = jnp.zeros((8, D_MODEL), F32)
        for n, (x_ref, d_ref, w_ref, o_ref, sc) in enumerate(
                [(q_ref, dq_ref, qw_ref, oq_ref, scale), (k_ref, dk_ref, kw_ref, ok_ref, 1.0)]):
            xv = x_ref[...]
            r = _head_rms(xv, hsum, expand)
            xhat = xv * r
            dn = d_ref[...] * sc
            sums = sums + jnp.where(rows8 == n, _colsum(dn * xhat), 0.0)
            dw = dn * w_ref[...]
            mean = _dotx_r(_dotx_r(dw * xhat, hsum, n=1), expand, n=2) * (1.0 / HEAD_DIM)
            o_ref[...] = (r * (dw - xhat * mean)).astype(BF16)
        ov_ref[...] = dv_ref[...].astype(BF16)
        folded = _dotx_r(sums, fold_ref[...])

        @pl.when(i == 0)
        def _():
            dw_ref[...] = folded

        @pl.when(i > 0)
        def _():
            dw_ref[...] += folded

    return pl.pallas_call(
        body, name=name, grid=(t // tm,),
        in_specs=[_row_spec(tm, col=1), _row_spec(tm, col=2), _row_spec(tm), _row_spec(tm), _row_spec(tm),
                  _fix_spec((1, D_MODEL)), _fix_spec((1, D_MODEL)), _fix_spec((D_MODEL, 128)),
                  _fix_spec((128, D_MODEL)), _fix_spec((D_MODEL, 128))],
        out_specs=[_row_spec(tm)] * 3 + [_fix_spec((8, 128))],
        out_shape=[jax.ShapeDtypeStruct((t, D_MODEL), BF16)] * 3 + [jax.ShapeDtypeStruct((8, 128), F32)],
        compiler_params=_cp(("arbitrary",)),
    )(proj, proj, dqs, dkn, dv, qw, kw, cst["hsum"], cst["expand"], cst["fold"])


def _sb_masks(i, kb, tq, tk):
    tpos = i * tq + lax.broadcasted_iota(jnp.int32, (tq, 1), 0)
    spos = kb * tk + lax.broadcasted_iota(jnp.int32, (1, tk), 1)
    return spos < tpos


def _grid_marks(n0, n1):
    j, i = pl.program_id(0), pl.program_id(1)
    return (jnp.logical_and(j == 0, i == 0), jnp.logical_and(j == n0 // 2, i == 0),
            jnp.logical_and(j == n0 - 1, i == n1 - 1))


def _sb_fwd(qs, kn, vb, pack, ycat, cst, name):
    t = qs.shape[0]
    tq = tk = min(t, SB_TILE)
    nq = t // tq
    pairs = FWD_PAIRS
    ngrp = D_MODEL // (128 * pairs)
    nh = 2 * pairs
    lanes = lambda p: slice(p * 128, (p + 1) * 128)

    def body(q_ref, k_ref, v_ref, u_ref, p_ref, yc_ref, rt_ref, ob_ref, cnt_ref, gat_ref, acc, rs, gss, grs):
        del yc_ref
        at_first, at_mid, at_last = _grid_marks(ngrp, nq)
        g_start, g_relay, g_finish = _gather_stages(p_ref, gat_ref, gss, grs)
        pl.when(at_first)(g_start)
        pl.when(at_mid)(g_relay)
        i = pl.program_id(1)
        lane = lax.broadcasted_iota(jnp.int32, (1, 128), 1)
        qh = []
        for p in range(pairs):
            q2 = q_ref[:, lanes(p)]
            zero = jnp.zeros_like(q2)
            qh += [jnp.where(lane < HEAD_DIM, q2, zero), jnp.where(lane >= HEAD_DIM, q2, zero)]
        acc[...] = jnp.zeros_like(acc)
        rs[...] = jnp.zeros_like(rs)
        ustrict = u_ref[...]

        def tile(kb, masked):
            off = pl.multiple_of(kb * tk, tk)
            k2 = [k_ref[pl.ds(off, tk), lanes(p)] for p in range(pairs)]
            v2 = [v_ref[pl.ds(off, tk), lanes(p)] for p in range(pairs)]
            strict = _sb_masks(i, kb, tq, tk) if masked else None
            s = [_dot(qh[h], k2[h // 2], "nt") for h in range(nh)]
            a, r, lb = [None] * nh, [None] * nh, [None] * nh
            for h in range(nh):
                sp = _softplus(s[h])
                a[h] = s[h] - sp
                r[h] = jnp.where(strict, -sp, 0.0) if masked else -sp
                lb[h] = _dot2(r[h], ustrict)
            for h in range(nh):
                lw = a[h] + lb[h] + rs[h]
                w = jnp.exp(jnp.where(strict, lw, -1e30) if masked else lw)
                rs[h] = rs[h] + jnp.sum(r[h], axis=1, keepdims=True)
                acc[h] = acc[h] + _dot(w.astype(BF16), v2[h // 2])

        tile(i, True)

        def live():
            return jnp.max(functools.reduce(jnp.maximum, [rs[h] for h in range(nh)]))

        def more(c):
            return jnp.logical_and(c[0] < i, c[1] > SB_DEAD)

        def step(c):
            tile(i - 1 - c[0], False)
            return c[0] + 1, live()

        n_off, _ = lax.while_loop(more, step, (jnp.int32(0), live()))
        cnt_ref[pl.program_id(0), i] = n_off.astype(F32)
        for p in range(pairs):
            rt_ref[:, lanes(p)] = jnp.where(lane < HEAD_DIM, rs[2 * p], rs[2 * p + 1])
            ob_ref[:, lanes(p)] = jnp.where(lane < HEAD_DIM, acc[2 * p], acc[2 * p + 1]).astype(BF16)
        pl.when(at_last)(g_finish)

    blk = lambda rows, imap: pl.BlockSpec((rows, (128 * pairs)), imap)
    return pl.pallas_call(
        body, name=name, grid=(ngrp, nq),
        in_specs=[blk(tq, lambda j, i: (i, j)), blk(t, lambda j, i: (0, j)), blk(t, lambda j, i: (0, j)),
                  _fix_spec((2 * tk, tk)), HB, pl.BlockSpec(memory_space=pl.ANY)],
        out_specs=[blk(tq, lambda j, i: (i, j)), blk(tq, lambda j, i: (i, ngrp + j)),
                   pl.BlockSpec(memory_space=pltpu.SMEM), HB],
        out_shape=[jax.ShapeDtypeStruct((t, D_MODEL), F32), jax.ShapeDtypeStruct(ycat.shape, ycat.dtype),
                   jax.ShapeDtypeStruct((ngrp, nq), F32),
                   jax.ShapeDtypeStruct((N_SHARDS,) + pack.shape, pack.dtype)],
        scratch_shapes=[pltpu.VMEM((nh, tq, 128), F32), pltpu.VMEM((nh, tq, 1), F32)] + GATHER_SEMS,
        input_output_aliases={5: 1},
        compiler_params=_cp(("arbitrary", "arbitrary")),
    )(qs, kn, vb, _doubled(cst["ustrict"], tk), pack, ycat)


def _sb_bwd(qs, kn, vb, rtot, cnt, dycat, csum_b, cst, name):
    t = qs.shape[0]
    tq = tk = min(t, SB_TILE)
    nq = t // tq
    pairs = BWD_PAIRS
    ngrp = D_MODEL // (128 * pairs)
    nh = 2 * pairs
    lanes = lambda p: slice(p * 128, (p + 1) * 128)

    def body(q_ref, k_ref, v_ref, rt_ref, do_ref, us_ref, ui_ref, cnt_ref, xs_ref, dq_ref, dk_ref, dv_ref, xr_ref,
             acc, rs, es, xss, xrs):
        at_first, _, at_last = _grid_marks(ngrp, nq)
        x_start, x_finish = _exchange_stages(xs_ref, xr_ref, xss, xrs)
        pl.when(at_first)(x_start)
        i = pl.program_id(1)
        lane = lax.broadcasted_iota(jnp.int32, (1, 128), 1)
        keep = [lane < HEAD_DIM, lane >= HEAD_DIM]
        qh, doh, rtot_h = [], [], []
        for p in range(pairs):
            q2, rt = q_ref[:, lanes(p)], rt_ref[:, lanes(p)]
            do2b = do_ref[:, lanes(p)].astype(BF16)
            qh += [jnp.where(kp, q2, jnp.zeros_like(q2)) for kp in keep]
            doh += [jnp.where(kp, do2b, jnp.zeros_like(do2b)) for kp in keep]
            rtot_h += [jnp.sum(jnp.where(lane == n * HEAD_DIM, rt, 0.0), axis=1, keepdims=True) for n in range(2)]
        acc[...] = jnp.zeros_like(acc)
        rs[...] = jnp.zeros_like(rs)
        es[...] = jnp.zeros_like(es)

        @pl.when(i == 0)
        def _():
            dk_ref[...] = jnp.zeros_like(dk_ref)
            dv_ref[...] = jnp.zeros_like(dv_ref)

        ule, ult = us_ref[...], ui_ref[...]

        def tile(kb, masked):
            off = pl.multiple_of(kb * tk, tk)
            k2 = [k_ref[pl.ds(off, tk), lanes(p)] for p in range(pairs)]
            v2 = [v_ref[pl.ds(off, tk), lanes(p)] for p in range(pairs)]
            strict = _sb_masks(i, kb, tq, tk) if masked else None
            s = [_dot(qh[h], k2[h // 2], "nt") for h in range(nh)]
            dw = [_dot(doh[h], v2[h // 2], "nt") for h in range(nh)]
            a, sg, r, pin, w, e, cin = ([None] * nh for _ in range(7))
            for h in range(nh):
                sp = _softplus(s[h])
                a[h] = s[h] - sp
                sg[h] = jnp.exp(a[h])
                r[h] = jnp.where(strict, -sp, 0.0) if masked else -sp
                pin[h] = _dot2(r[h], ule)
            for h in range(nh):
                lw = a[h] + ((rtot_h[h] - rs[h]) - pin[h])
                w[h] = jnp.exp(jnp.where(strict, lw, -1e30) if masked else lw)
                e[h] = w[h] * dw[h]
                cin[h] = _dot2(e[h], ult)
            for p in range(pairs):
                dk_t = jnp.zeros((tk, 128), F32)
                dv_t = jnp.zeros((tk, 128), F32)
                for h in (2 * p, 2 * p + 1):
                    dl = e[h] * (1.0 - sg[h]) - (es[h] + cin[h]) * sg[h]
                    dl = (jnp.where(strict, dl, 0.0) if masked else dl).astype(BF16)
                    rs[h] = rs[h] + jnp.sum(r[h], axis=1, keepdims=True)
                    es[h] = es[h] + jnp.sum(e[h], axis=1, keepdims=True)
                    acc[h] = acc[h] + _dot(dl, k2[p])
                    dk_t = dk_t + _dot(dl, qh[h], "tn")
                    dv_t = dv_t + _dot(w[h].astype(BF16), doh[h], "tn")
                dk_ref[pl.ds(off, tk), lanes(p)] += dk_t
                dv_ref[pl.ds(off, tk), lanes(p)] += dv_t

        def step(kb, carry):
            tile(kb, False)
            return carry

        n_off = cnt_ref[pl.program_id(0) * BWD_PAIRS // FWD_PAIRS, i].astype(jnp.int32)
        lax.fori_loop(i - n_off, i, step, 0)
        tile(i, True)
        for p in range(pairs):
            dq_ref[:, lanes(p)] = jnp.where(lane < HEAD_DIM, acc[2 * p], acc[2 * p + 1])
        pl.when(at_last)(x_finish)

    blk = lambda rows, imap: pl.BlockSpec((rows, (128 * pairs)), imap)
    return pl.pallas_call(
        body, name=name, grid=(ngrp, nq),
        in_specs=[blk(tq, lambda j, i: (i, j)), blk(t, lambda j, i: (0, j)), blk(t, lambda j, i: (0, j)),
                  blk(tq, lambda j, i: (i, j)), blk(tq, lambda j, i: (i, ngrp + j)),
                  _fix_spec((2 * tk, tk)), _fix_spec((2 * tk, tk)), pl.BlockSpec(memory_space=pltpu.SMEM), HB],
        out_specs=[blk(tq, lambda j, i: (i, j)), blk(t, lambda j, i: (0, j)), blk(t, lambda j, i: (0, j)), HB],
        out_shape=[jax.ShapeDtypeStruct((t, D_MODEL), F32)] * 3
        + [jax.ShapeDtypeStruct((3,) + csum_b.shape[1:], csum_b.dtype)],
        scratch_shapes=[pltpu.VMEM((nh, tq, 128), F32), pltpu.VMEM((nh, tq, 1), F32), pltpu.VMEM((nh, tq, 1), F32)]
        + EXCHANGE_SEMS,
        compiler_params=_cp(("arbitrary", "arbitrary")),
    )(qs, kn, vb, rtot, dycat, _doubled(cst["ule"], tk), _doubled(cst["ult"], tk), cnt, csum_b)


def _adamw(w, g, m, v, name):
    lead = (1,) * (w.ndim - 2)
    rows, cols = w.shape[-2:]
    fits = [d for d in range(8, rows, 8) if rows % d == 0 and d * cols * 4 <= ADAM_BLOCK_BYTES]
    tr = max(fits) if fits else rows
    c1 = 1.0 - ADAM_B1 ** ADAM_STEP
    c2 = 1.0 - ADAM_B2 ** ADAM_STEP

    def body(w_ref, g_ref, m_ref, v_ref, d_ref, nm_ref, nv_ref):
        gv = g_ref[...]
        nm = ADAM_B1 * m_ref[...] + (1.0 - ADAM_B1) * gv
        nv = ADAM_B2 * v_ref[...] + (1.0 - ADAM_B2) * (gv * gv)
        nm_ref[...] = nm
        nv_ref[...] = nv
        d_ref[...] = -ADAM_LR * ((nm / c1) / (jnp.sqrt(nv / c2) + ADAM_EPS) + ADAM_WD * w_ref[...])

    spec = pl.BlockSpec(lead + (tr, cols), lambda i: (0,) * len(lead) + (i, 0))
    return pl.pallas_call(
        body, name=name, grid=(rows // tr,), in_specs=[spec] * 4, out_specs=[spec] * 3,
        out_shape=[jax.ShapeDtypeStruct(w.shape, F32)] * 3, compiler_params=_cp(("parallel",)),
    )(w, g, m, v)


def _place():
    x, y, c = lax.axis_index("x"), lax.axis_index("y"), lax.axis_index("c")
    chips = [(1 - x, y), (x, 1 - y), (1 - x, 1 - y)]
    return x, y, c, chips


VM = pl.BlockSpec(memory_space=pltpu.VMEM)
HB = pl.BlockSpec(memory_space=pltpu.HBM)


def _gather_all(p_ref, gat_ref, ss, rs):
    x, y, c, _ = _place()
    me = 4 * x + 2 * y + c
    peers = [(x, y, 1 - c), (1 - x, y, c), (x, 1 - y, c), (1 - x, 1 - y, c),
             (1 - x, y, 1 - c), (x, 1 - y, 1 - c), (1 - x, 1 - y, 1 - c)]

    def copy(k, slot, to):
        return pltpu.make_async_remote_copy(src_ref=p_ref, dst_ref=gat_ref.at[slot], send_sem=ss.at[k],
                                            recv_sem=rs.at[k], device_id=to, device_id_type=MESH)

    sends = [copy(k, me, p) for k, p in enumerate(peers)]
    for s in sends:
        s.start()
    gat_ref[me] = p_ref[...]
    for k, p in enumerate(peers):
        copy(k, 4 * p[0] + 2 * p[1] + p[2], p).wait_recv()
    for s in sends:
        s.wait_send()


ALL_SEMS = [pltpu.SemaphoreType.DMA((7,)), pltpu.SemaphoreType.DMA((7,))]


def _small_reduce(pack, name):
    rows = pack.shape[0]

    def body(p_ref, gat_ref, sum_ref, ss, rs):
        _gather_all(p_ref, gat_ref, ss, rs)
        tot = gat_ref[0]
        for b in range(1, 8):
            tot = tot + gat_ref[b]
        sum_ref[...] = tot

    return pl.pallas_call(
        body, name=name, in_specs=[VM], out_specs=[VM, VM],
        out_shape=[jax.ShapeDtypeStruct((8, rows, D_MODEL), F32), jax.ShapeDtypeStruct((rows, D_MODEL), F32)],
        scratch_shapes=ALL_SEMS, compiler_params=_cp(),
    )(pack)


def _prologue(cpack, w_ada, b_shard, wpack_in, name):
    def body(cp_ref, w_ref, b_ref, p_ref, gat_ref, modp_ref, gin_ref, ss1, rs1, ss2, rs2, gss, grs):
        g_start, g_relay, g_finish = _gather_stages(p_ref, gin_ref, gss, grs)
        g_start()
        _gather_all(cp_ref, gat_ref, ss1, rs1)
        x, y, c, chips = _place()
        sh = 2 * x + y
        row = lax.broadcasted_iota(jnp.int32, (8, D_MODEL), 0)
        cv = jnp.zeros((8, D_MODEL), F32)
        for b in range(8):
            cv = jnp.where(row == b, gat_ref[b, 0:8, :], cv)
        cv = cv * _sig(cv)
        modp_ref[sh] = jnp.dot(cv, w_ref[...], precision=lax.Precision.HIGHEST,
                               preferred_element_type=F32) + b_ref[...]

        def copy(k, slot, to):
            return pltpu.make_async_remote_copy(src_ref=modp_ref.at[slot], dst_ref=modp_ref.at[slot],
                                                send_sem=ss2.at[k], recv_sem=rs2.at[k], device_id=to,
                                                device_id_type=MESH)

        sends = [copy(k, sh, (*ch, c)) for k, ch in enumerate(chips)]
        for s in sends:
            s.start()
        for k, ch in enumerate(chips):
            copy(k, 2 * ch[0] + ch[1], (*ch, c)).wait_recv()
        for s in sends:
            s.wait_send()
        g_relay()
        g_finish()

    return pl.pallas_call(
        body, name=name, in_specs=[VM, VM, VM, HB], out_specs=[VM, VM, HB],
        out_shape=[jax.ShapeDtypeStruct((8,) + cpack.shape, F32),
                   jax.ShapeDtypeStruct((N_SHARDS, 8, 6 * D_MODEL // N_SHARDS), F32),
                   jax.ShapeDtypeStruct((N_SHARDS,) + wpack_in.shape, wpack_in.dtype)],
        scratch_shapes=ALL_SEMS + EXCHANGE_SEMS + GATHER_SEMS, compiler_params=_cp(),
    )(cpack, w_ada, b_shard, wpack_in)


def _gather_stages(p_ref, out_ref, ss, rs):
    hf = p_ref.shape[0] // 2
    x, y, c, chips = _place()
    sh = 2 * x + y
    sib = (x, y, 1 - c)
    slots = [2 * ch[0] + ch[1] for ch in chips]

    def half(slot, hc):
        return out_ref.at[slot, pl.ds(hc * hf, hf), :]

    def copy(k, src, slot, hc, to):
        return pltpu.make_async_remote_copy(src_ref=src, dst_ref=half(slot, hc), send_sem=ss.at[k],
                                            recv_sem=rs.at[k], device_id=to, device_id_type=MESH)

    def first():
        return [copy(j, p_ref.at[pl.ds(c * hf, hf), :], sh, c, (*ch, c)) for j, ch in enumerate(chips)]

    def passed():
        return [copy(3 + j, half(slots[j], c), slots[j], c, sib) for j in range(3)]

    def own():
        return pltpu.make_async_remote_copy(src_ref=p_ref, dst_ref=out_ref.at[sh], send_sem=ss.at[6],
                                            recv_sem=rs.at[6], device_id=sib, device_id_type=MESH)

    def start():
        for cp in first() + [own()]:
            cp.start()

    def relay():
        for j, cp in enumerate(passed()):
            copy(j, half(slots[j], c), slots[j], c, (*chips[j], c)).wait_recv()
            cp.start()

    def finish():
        for j in range(3):
            copy(3 + j, half(slots[j], 1 - c), slots[j], 1 - c, sib).wait_recv()
        own().wait()
        for cp in first() + passed():
            cp.wait_send()

    return start, relay, finish


GATHER_SEMS = [pltpu.SemaphoreType.DMA((7,)), pltpu.SemaphoreType.DMA((7,))]


def _swap_stages(g_ref, out_ref, ss, rs):
    hf = g_ref.shape[1] // 2
    x, y, c, _ = _place()

    def copy():
        return pltpu.make_async_remote_copy(
            src_ref=g_ref.at[pl.ds(0, N_SHARDS), pl.ds((1 - c) * hf, hf), :], dst_ref=out_ref,
            send_sem=ss, recv_sem=rs, device_id=(x, y, 1 - c), device_id_type=MESH)

    return (lambda: copy().start()), (lambda: copy().wait())


SWAP_SEMS = [pltpu.SemaphoreType.DMA, pltpu.SemaphoreType.DMA]


def _swap_shape(g):
    return jax.ShapeDtypeStruct((N_SHARDS, g.shape[1] // 2, D_MODEL), g.dtype)


def _sibling_swap(g, name):
    def body(g_ref, out_ref, ss, rs):
        for stage in _swap_stages(g_ref, out_ref, ss, rs):
            stage()

    return pl.pallas_call(
        body, name=name, in_specs=[HB], out_specs=HB, out_shape=_swap_shape(g),
        scratch_shapes=SWAP_SEMS, compiler_params=_cp(),
    )(g)


def _row_tile(rows, width_bytes, cap_bytes):
    fits = [d for d in range(8, rows + 1, 8) if rows % d == 0 and d * width_bytes <= cap_bytes]
    return max(fits)


def _chip_sum(g, got, c_idx, name):
    hf = got.shape[1]
    tr = _row_tile(hf, D_MODEL * 4, 3 << 20)
    nb = hf // tr

    def body(c_ref, a_ref, b_ref, s_ref, sb_ref):
        s = a_ref[...] + b_ref[...]
        s_ref[...] = s
        sb_ref[...] = s.astype(BF16)

    blk = pl.BlockSpec((1, tr, D_MODEL), lambda s, i, c_ref: (s, i, 0))
    return pl.pallas_call(
        body, name=name,
        grid_spec=pltpu.PrefetchScalarGridSpec(
            num_scalar_prefetch=1, grid=(N_SHARDS, nb),
            in_specs=[pl.BlockSpec((1, tr, D_MODEL), lambda s, i, c_ref: (s, c_ref[0] * nb + i, 0)), blk],
            out_specs=[blk, blk]),
        out_shape=[jax.ShapeDtypeStruct((N_SHARDS, hf, D_MODEL), F32),
                   jax.ShapeDtypeStruct((N_SHARDS, hf, D_MODEL), BF16)],
        compiler_params=_cp(("parallel", "parallel")),
    )(c_idx, g, got)


def _exchange_stages(s_ref, out_ref, ss, rs):
    x, y, c, chips = _place()

    def sends():
        return [pltpu.make_async_remote_copy(src_ref=s_ref.at[2 * ch[0] + ch[1]], dst_ref=out_ref.at[k],
                                             send_sem=ss.at[k], recv_sem=rs.at[k], device_id=(*ch, c),
                                             device_id_type=MESH) for k, ch in enumerate(chips)]

    def start():
        for cp in sends():
            cp.start()

    def finish():
        for cp in sends():
            cp.wait()

    return start, finish


EXCHANGE_SEMS = [pltpu.SemaphoreType.DMA((3,)), pltpu.SemaphoreType.DMA((3,))]


def _total_half(s, got, sh_idx, name):
    hf = got.shape[1]
    tr = _row_tile(hf, D_MODEL * 4, 3 << 20)
    nb = hf // tr

    def body(sh_ref, a_ref, r0, r1, r2, o_ref):
        o_ref[...] = ((a_ref[0] + r0[0].astype(F32)) + r1[0].astype(F32)) + r2[0].astype(F32)

    rspec = lambda k: pl.BlockSpec((1, tr, D_MODEL), lambda i, sh_ref: (k, i, 0))
    return pl.pallas_call(
        body, name=name,
        grid_spec=pltpu.PrefetchScalarGridSpec(
            num_scalar_prefetch=1, grid=(nb,),
            in_specs=[pl.BlockSpec((1, tr, D_MODEL), lambda i, sh_ref: (sh_ref[0], i, 0)),
                      rspec(0), rspec(1), rspec(2)],
            out_specs=pl.BlockSpec((tr, D_MODEL), lambda i, sh_ref: (i, 0))),
        out_shape=jax.ShapeDtypeStruct((hf, D_MODEL), F32),
        compiler_params=_cp(("parallel",)),
    )(sh_idx, s, got, got, got)


def _join_halves(tot, name):
    def body(t_ref, out_ref, ss, rs):
        x, y, c, _ = _place()
        cp = pltpu.make_async_remote_copy(src_ref=t_ref, dst_ref=out_ref, send_sem=ss, recv_sem=rs,
                                          device_id=(x, y, 1 - c), device_id_type=MESH)
        cp.start()
        cp.wait()

    return pl.pallas_call(
        body, name=name, in_specs=[HB], out_specs=HB,
        out_shape=jax.ShapeDtypeStruct(tot.shape, F32),
        scratch_shapes=[pltpu.SemaphoreType.DMA, pltpu.SemaphoreType.DMA],
        compiler_params=_cp(),
    )(tot)


def _w_ada_grad(cond, dmod_cols, name):
    def body(c_ref, d_ref, o_ref):
        cv = c_ref[...]
        cv = cv * _sig(cv)
        o_ref[...] = lax.dot_general(cv, d_ref[...], _DN["tn"], precision=lax.Precision.HIGHEST,
                                     preferred_element_type=F32)

    return pl.pallas_call(
        body, name=name, in_specs=[VM, VM], out_specs=VM,
        out_shape=jax.ShapeDtypeStruct((D_MODEL, dmod_cols.shape[1]), F32), compiler_params=_cp(),
    )(cond, dmod_cols)


def _pad_rows(a, rows):
    return jnp.pad(a, ((0, rows - a.shape[0]), (0, 0)))


def _pad_cols(a, cols):
    return jnp.pad(a, ((0, 0), (0, cols - a.shape[1])))


def _unpack_rest(p):
    o = 0
    out = []
    for r in (R_OUT, R_FF, R_FF, R_FF):
        out.append(p[..., o:o + r, :])
        o += r
    return out


def _reduce_tail(csum, got2, shard, ac, tag):
    tot = _total_half(csum, got2, shard.reshape(1).astype(jnp.int32), "rs_total_" + tag)
    other = _join_halves(tot, "rs_join_" + tag)
    return jnp.where(ac == 0, jnp.concatenate([tot, other], axis=0), jnp.concatenate([other, tot], axis=0))


def _reduce_head(gpack, ac, tag):
    got = _sibling_swap(gpack, "rs_sibling_swap_" + tag)
    return _chip_sum(gpack, got, ac.reshape(1).astype(jnp.int32), "rs_chip_sum_" + tag)


def kernel(x, c, w_ada, b_ada, norm1_w, w_in, conv_w, conv_b, dt_bias, a_log, d_skip, ssd_norm_w, q_norm_w, k_norm_w, w_out, norm2_w, w_gate, w_up, w_down, loss_target, m_w_ada, m_b_ada, m_norm1_w, m_w_in, m_conv_w, m_conv_b, m_dt_bias, m_a_log, m_d_skip, m_ssd_norm_w, m_q_norm_w, m_k_norm_w, m_w_out, m_norm2_w, m_w_gate, m_w_up, m_w_down, v_w_ada, v_b_ada, v_norm1_w, v_w_in, v_conv_w, v_conv_b, v_dt_bias, v_a_log, v_d_skip, v_ssd_norm_w, v_q_norm_w, v_k_norm_w, v_w_out, v_norm2_w, v_w_gate, v_w_up, v_w_down):
    cst = _consts()
    ax, ay, ac = lax.axis_index("x"), lax.axis_index("y"), lax.axis_index("c")
    shard = 2 * ax + ay
    me = 4 * ax + 2 * ay + ac
    xs = x[0]
    tgt = loss_target[0]
    w_in_cols = w_in.shape[2]
    conv_cols = conv_w.shape[2]

    tr3 = lambda a: jnp.transpose(a, (0, 2, 1))
    lin = lambda a: tr3(a).reshape(-1, 128)
    unlin = lambda a: tr3(a.reshape(1, w_in_cols, D_MODEL))
    wpack_in = _pad_rows(tr3(w_in.astype(BF16))[0], R_IN)
    wpack_rest = jnp.concatenate([w_out[0], tr3(w_gate)[0], tr3(w_up)[0], w_down[0]], axis=0).astype(BF16)

    cw_flat = _pad_cols(conv_w[0].reshape(1, -1), 2 * D_MODEL).reshape(2, D_MODEL)
    cpack = jnp.concatenate([jnp.broadcast_to(c, (8, D_MODEL)), _pad_rows(cw_flat, 8)], axis=0)
    mod_w = 6 * D_MODEL // N_SHARDS
    b_shard = lax.dynamic_slice(b_ada, (0, shard * mod_w), (1, mod_w))
    gat, modp, gp_in = _prologue(cpack, w_ada[0], b_shard, wpack_in, "prologue")
    c_all = gat[:, 0, :]
    cw = gat[0::2, 8:10, :].reshape(N_SHARDS, 2 * D_MODEL)[:, :4 * conv_cols].reshape(N_SHARDS, 4, conv_cols)
    conv_w_full = jnp.transpose(cw, (1, 0, 2)).reshape(4, D_CONV)
    mod_mine = lax.dynamic_slice(modp, (0, me, 0), (N_SHARDS, 1, mod_w)).reshape(6, D_MODEL)
    mod = _pad_rows(mod_mine, 8)
    wi_t = gp_in[:, :w_in_cols, :].reshape(D_IN_PROJ, D_MODEL)
    w_inp_t = jnp.concatenate([wi_t[0:1024], wi_t[2576:5648], wi_t[1024:2560], wi_t[2560:2576],
                               jnp.zeros((D_PROJ_PAD - D_IN_PROJ, D_MODEL), BF16)], axis=0)

    pad128 = lambda a: _pad_cols(a, 128)
    dtb, alog = pad128(dt_bias), pad128(a_log)
    dsk = jnp.repeat(d_skip, HEAD_DIM, axis=1)
    qw, kw = jnp.tile(q_norm_w, (1, N_HEADS)), jnp.tile(k_norm_w, (1, N_HEADS))

    h1 = _norm_mod(xs, norm1_w, mod, 0, "norm1")
    proj = _matmul(h1, w_inp_t, "nt", F32, "in_proj")
    u = _conv_fwd(proj, conv_w_full, conv_b, "conv_fwd")
    y_ssd, yn, prev = _ssd_fwd(u, proj, dtb, alog, dsk, ssd_norm_w, cst, "ssd_fwd")
    qs, kn, vb = _qk_fwd(proj, qw, kw, cst, "qk_norm")
    rtot, ycat, cnt, gp_rest = _sb_fwd(qs, kn, vb, wpack_rest, yn, cst, "sb_fwd")
    p_out, p_gate, p_up, p_down = _unpack_rest(gp_rest)
    w_o = p_out.reshape(2 * D_MODEL, D_MODEL)
    w_gu_t = jnp.concatenate([p_gate.reshape(D_FF, D_MODEL), p_up.reshape(D_FF, D_MODEL)], axis=0)
    w_d = p_down.reshape(D_FF, D_MODEL)
    mix = _matmul(ycat, w_o, "nn", F32, "out_proj")
    x1, h2 = _resid_norm(xs, mix, norm2_w, mod, "resid_norm2")
    gu = _matmul(h2, w_gu_t, "nt", BF16, "ffn_in")
    act = _act_fwd(gu, "ffn_act")
    ffn = _matmul(act, w_d, "nn", F32, "ffn_out", tk_cap=1408)
    dffn, dout, dg2, loss8 = _loss_head(x1, ffn, tgt, mod, "loss_head")
    loss = lax.psum(loss8[0, 0], ("x", "y", "c"))

    dact = _matmul(dffn, w_d, "nt", BF16, "d_act")
    g_down = _matmul(act, dffn, "tn", F32, "g_w_down", tm_cap=1408)
    dgu = _act_bwd(dact, gu, "ffn_act_bwd")
    dh2 = _matmul(dgu, w_gu_t, "nn", F32, "d_h2", tk_cap=1408)
    g_gu_t = _matmul(dgu, h2, "tn", F32, "g_w_gu", tm_cap=1408)
    dx1, dmix, acc2 = _norm_bwd(dh2, x1, dout, mix, norm2_w, mod, 3, 2, "norm2_bwd")
    dycat = _matmul(dmix, w_o, "nt", BF16, "d_ycat")
    g_out = _matmul(ycat, dmix, "tn", F32, "g_w_out")
    gpack_rest = jnp.concatenate([
        g_out.reshape(N_SHARDS, R_OUT, D_MODEL),
        g_gu_t[:D_FF].reshape(N_SHARDS, R_FF, D_MODEL), g_gu_t[D_FF:].reshape(N_SHARDS, R_FF, D_MODEL),
        g_down.reshape(N_SHARDS, R_FF, D_MODEL)], axis=1)
    du, ddt, dz, acc_ssd, acc16, got_s = _ssd_bwd(u, proj, y_ssd, prev, dycat, dtb, alog, dsk, ssd_norm_w,
                                                  gpack_rest, cst, "ssd_bwd")
    csum_r, csum_rb = _chip_sum(gpack_rest, got_s, ac.reshape(1).astype(jnp.int32), "rs_chip_sum_rest")
    dqs, dkn, dv, got_r = _sb_bwd(qs, kn, vb, rtot, cnt, dycat, csum_rb, cst, "sb_bwd")
    r_out, r_gate, r_up, r_down = _unpack_rest(_reduce_tail(csum_r, got_r, shard, ac, "rest"))
    dq, dk, dvb, acc_qk = _qk_bwd(proj, dqs, dkn, dv, qw, kw, cst, "qk_norm_bwd")
    dxbc, g_conv_w, g_conv_b = _conv_bwd(proj, du, conv_w_full, conv_b, "conv_bwd")
    dproj = jnp.concatenate([dz, dq, dk, dvb, dxbc, ddt], axis=1)
    g_inp_t = _matmul(dproj, h1, "tn", F32, "g_w_in", tm_cap=1920)
    gi_t = jnp.concatenate([g_inp_t[0:1024], g_inp_t[4096:5632], g_inp_t[5632:5648], g_inp_t[1024:4096]], axis=0)
    gpack_in = jnp.pad(gi_t.reshape(N_SHARDS, w_in_cols, D_MODEL), ((0, 0), (0, R_IN - w_in_cols), (0, 0)))
    csum_i, csum_ib = _reduce_head(gpack_in, ac, "in")
    dh1, got_i = _matmul(dproj, w_inp_t, "nn", F32, "d_h1", tk_cap=1152, exchange=csum_ib)
    r_in = _reduce_tail(csum_i, got_i, shard, ac, "in")
    grad_x, acc1 = _norm_bwd(dh1, xs, dx1, None, norm1_w, mod, 0, None, "norm1_bwd")

    last = jnp.concatenate([acc_qk[0:1, 0:64], acc_qk[1:2, 0:64], acc16[0:1, 0:16], acc16[1:2, 0:16],
                            acc16[2:3, 0:16]], axis=1)
    spack = jnp.concatenate([
        acc1[0:2], acc2[3:4], acc2[0:2], dg2,
        acc1[2:3], acc2[2:3], acc_ssd[0:1],
        _pad_cols(g_conv_b, 2 * D_MODEL).reshape(2, D_MODEL),
        g_conv_w.reshape(6, D_MODEL),
        _pad_cols(last, D_MODEL)], axis=0)
    sgat, ssum = _small_reduce(_pad_rows(spack, SMALL_ROWS), "gather_small")
    g_b_ada = ssum[0:6].reshape(1, 6 * D_MODEL)
    g_norm1, g_norm2, g_ssdn = ssum[6:7], ssum[7:8], ssum[8:9]
    g_cb = ssum[9:11].reshape(1, 2 * D_MODEL)[:, :D_CONV]
    g_cw = lax.dynamic_slice(ssum[11:17].reshape(4, D_CONV), (0, shard * conv_cols), (4, conv_cols))
    g_qn, g_kn = ssum[17:18, 0:64], ssum[17:18, 64:128]
    g_dtb, g_alog, g_dsk = ssum[17:18, 128:144], ssum[17:18, 144:160], ssum[17:18, 160:176]
    dmod_all = sgat[:, 0:6, :].reshape(8, 6 * D_MODEL)
    g_w_ada = _w_ada_grad(c_all, lax.dynamic_slice(dmod_all, (0, shard * mod_w), (8, mod_w)), "g_w_ada")


    grads = dict(w_ada=g_w_ada, b_ada=g_b_ada, norm1_w=g_norm1, w_in=r_in[:w_in_cols].reshape(-1, 128), conv_w=g_cw,
                 conv_b=g_cb, dt_bias=g_dtb, a_log=g_alog, d_skip=g_dsk, ssd_norm_w=g_ssdn, q_norm_w=g_qn,
                 k_norm_w=g_kn, w_out=r_out, norm2_w=g_norm2, w_gate=r_gate, w_up=r_up, w_down=r_down)
    weights = dict(w_ada=(w_ada, m_w_ada, v_w_ada), b_ada=(b_ada, m_b_ada, v_b_ada),
                   norm1_w=(norm1_w, m_norm1_w, v_norm1_w), w_in=(w_in, m_w_in, v_w_in),
                   conv_w=(conv_w, m_conv_w, v_conv_w), conv_b=(conv_b, m_conv_b, v_conv_b),
                   dt_bias=(dt_bias, m_dt_bias, v_dt_bias), a_log=(a_log, m_a_log, v_a_log),
                   d_skip=(d_skip, m_d_skip, v_d_skip), ssd_norm_w=(ssd_norm_w, m_ssd_norm_w, v_ssd_norm_w),
                   q_norm_w=(q_norm_w, m_q_norm_w, v_q_norm_w), k_norm_w=(k_norm_w, m_k_norm_w, v_k_norm_w),
                   w_out=(w_out, m_w_out, v_w_out), norm2_w=(norm2_w, m_norm2_w, v_norm2_w),
                   w_gate=(w_gate, m_w_gate, v_w_gate), w_up=(w_up, m_w_up, v_w_up),
                   w_down=(w_down, m_w_down, v_w_down))
    views = dict(w_in=(lin, unlin), w_gate=(tr3, tr3), w_up=(tr3, tr3))
    same = lambda a: a
    names = list(weights)
    g_out_l, d_out_l, m_out_l, v_out_l = [], [], [], []
    for n in names:
        view, back = views.get(n, (same, same))
        w, m, v = (view(a) for a in weights[n])
        g = grads[n].reshape(w.shape)
        d, nm, nv = _adamw(w, g, m, v, "adamw_" + n)
        g_out_l.append(back(g))
        d_out_l.append(back(d))
        m_out_l.append(back(nm))
        v_out_l.append(back(nv))
    return (loss, grad_x[None], *g_out_l, *d_out_l, *m_out_l, *v_out_l)
```

```python
import functools

import numpy as np
import jax
import jax.numpy as jnp
from jax import lax
from jax.experimental import pallas as pl
from jax.experimental.pallas import tpu as pltpu

F32, BF16 = jnp.float32, jnp.bfloat16
MESH = pl.DeviceIdType.MESH

D_MODEL = 1024
HEAD_DIM = 64
N_HEADS = 16
D_CONV = 1536
D_FF = 2816
D_IN_PROJ = 5648
D_PROJ_PAD = 5760
CHUNK = 128
SB_TILE = 256
FWD_PAIRS, BWD_PAIRS = 4, 2
SB_DEAD = -105.0
EPS = 1e-6
N_SHARDS = 4
R_IN, R_OUT, R_FF = 1440, 512, 704
SMALL_ROWS = 24

ADAM_LR, ADAM_B1, ADAM_B2, ADAM_EPS, ADAM_WD, ADAM_STEP = 0.001, 0.9, 0.999, 1e-08, 0.01, 10

VMEM_LIMIT = 48 * 1024 * 1024
ADAM_BLOCK_BYTES = 3 * 512 * 1024

_DN = {"nn": (((1,), (0,)), ((), ())), "nt": (((1,), (1,)), ((), ())), "tn": (((0,), (0,)), ((), ()))}


def _dot(a, b, dims="nn"):
    return lax.dot_general(a, b, _DN[dims], preferred_element_type=F32)


def _pieces(x, n):
    out = []
    for _ in range(n - 1):
        hi = lax.bitcast_convert_type(lax.bitcast_convert_type(x, jnp.int32) & jnp.int32(-65536), F32)
        out.append(hi.astype(BF16))
        x = x - hi
    out.append(x.astype(BF16))
    return out


def _dotx_r(x, b_exact, n=3):
    return _dot(jnp.concatenate(_pieces(x, n), axis=1), jnp.concatenate([b_exact] * n, axis=0))


def _dotx_l(a_exact, x, n=3):
    return _dot(jnp.concatenate([a_exact] * n, axis=1), jnp.concatenate(_pieces(x, n), axis=0))


def _dot2(x, b2):
    return _dot(jnp.concatenate(_pieces(x, 2), axis=1), b2)


def _sig(x):
    return 1.0 / (1.0 + jnp.exp(-x))


def _softplus(x):
    return jnp.maximum(x, 0.0) + jnp.log(1.0 + jnp.exp(-jnp.abs(x)))


def _cp(sem=None, vmem=VMEM_LIMIT):
    return pltpu.CompilerParams(dimension_semantics=sem, vmem_limit_bytes=vmem)


def _colsum(x):
    return jnp.sum(x, axis=0, keepdims=True)


def _consts():
    ch = np.arange(D_MODEL)
    expand = (np.arange(128)[:, None] == (ch // HEAD_DIM)[None, :]).astype(np.float32)
    fold = (ch[:, None] % HEAD_DIM == np.arange(128)[None, :]).astype(np.float32)
    i = np.arange(CHUNK)
    tril = (i[:, None] >= i[None, :]).astype(np.float32)
    j = np.arange(SB_TILE)
    ustrict = (j[:, None] > j[None, :]).astype(np.float32)
    ule = (j[:, None] <= j[None, :]).astype(np.float32)
    ult = (j[:, None] < j[None, :]).astype(np.float32)
    c = lambda a: jnp.asarray(a, BF16)
    return dict(expand=c(expand), hsum=c(expand.T), fold=c(fold), tril=c(tril), triu=c(tril.T),
                ustrict=ustrict, ule=ule, ult=ult)


def _doubled(tri, tk):
    b = tri[:tk, :tk]
    return jnp.asarray(np.concatenate([b, b], axis=0), BF16)


def _pick(n, cap):
    best = 128
    for t in range(128, min(n, cap) + 1, 128):
        if n % t == 0:
            best = t
    return n if n <= cap else best


def _matmul(a, b, dims, out_dtype, name, tm_cap=1024, tn_cap=2048, tk_cap=1024, exchange=None):
    if dims == "nn":
        (m, k), (_, n) = a.shape, b.shape
    elif dims == "nt":
        (m, k), (n, _) = a.shape, b.shape
    else:
        (k, m), (_, n) = a.shape, b.shape
    tm, tn, tk = _pick(m, tm_cap), _pick(n, tn_cap), _pick(k, tk_cap)
    nk = k // tk
    a_spec = (pl.BlockSpec((tk, tm), lambda i, j, kk: (kk, i)) if dims == "tn"
              else pl.BlockSpec((tm, tk), lambda i, j, kk: (i, kk)))
    b_spec = (pl.BlockSpec((tn, tk), lambda i, j, kk: (j, kk)) if dims == "nt"
              else pl.BlockSpec((tk, tn), lambda i, j, kk: (kk, j)))

    grid = (m // tm, n // tn, nk)

    def body(a_ref, b_ref, *rest):
        if exchange is None:
            o_ref, acc_ref = rest
        else:
            xs_ref, o_ref, xr_ref, acc_ref, xss, xrs = rest
            ids = [pl.program_id(d) for d in range(3)]
            x_start, x_finish = _exchange_stages(xs_ref, xr_ref, xss, xrs)
            pl.when(functools.reduce(jnp.logical_and, [p == 0 for p in ids]))(x_start)
        kk = pl.program_id(2)
        part = _dot(a_ref[...], b_ref[...], dims)
        if nk == 1:
            o_ref[...] = part.astype(out_dtype)
        else:
            @pl.when(kk == 0)
            def _():
                acc_ref[...] = part

            @pl.when(kk > 0)
            def _():
                acc_ref[...] += part

            @pl.when(kk == nk - 1)
            def _():
                o_ref[...] = acc_ref[...].astype(out_dtype)
        if exchange is not None:
            pl.when(functools.reduce(jnp.logical_and, [p == g - 1 for p, g in zip(ids, grid)]))(x_finish)

    in_specs = [a_spec, b_spec]
    out_specs = [pl.BlockSpec((tm, tn), lambda i, j, kk: (i, j))]
    out_shape = [jax.ShapeDtypeStruct((m, n), out_dtype)]
    scratch = [pltpu.VMEM((tm, tn) if nk > 1 else (8, 128), F32)]
    args = [a, b]
    if exchange is not None:
        in_specs.append(HB)
        out_specs.append(HB)
        out_shape.append(jax.ShapeDtypeStruct((3,) + exchange.shape[1:], exchange.dtype))
        scratch += EXCHANGE_SEMS
        args.append(exchange)
    out = pl.pallas_call(
        body, name=name, grid=grid, in_specs=in_specs, out_specs=out_specs, out_shape=out_shape,
        scratch_shapes=scratch,
        compiler_params=_cp(("parallel", "parallel", "arbitrary") if exchange is None else ("arbitrary",) * 3),
    )(*args)
    return out[0] if exchange is None else out


def _row_spec(tm, width=D_MODEL, col=0):
    return pl.BlockSpec((tm, width), lambda i: (i, col))


def _fix_spec(shape):
    return pl.BlockSpec(shape, lambda *_: (0,) * len(shape))


def _norm_mod(x, nw, mod, row_sh, name):
    t = x.shape[0]
    tm = min(t, 512)

    def body(x_ref, nw_ref, mod_ref, h_ref):
        xv = x_ref[...]
        r = lax.rsqrt(jnp.mean(xv * xv, axis=-1, keepdims=True) + EPS)
        sh = mod_ref[row_sh:row_sh + 1, :]
        sc = mod_ref[row_sh + 1:row_sh + 2, :]
        h_ref[...] = (xv * r * nw_ref[...] * (1.0 + sc) + sh).astype(BF16)

    return pl.pallas_call(
        body, name=name, grid=(t // tm,),
        in_specs=[_row_spec(tm), _fix_spec((1, D_MODEL)), _fix_spec((8, D_MODEL))],
        out_specs=_row_spec(tm), out_shape=jax.ShapeDtypeStruct((t, D_MODEL), BF16),
        compiler_params=_cp(("parallel",)),
    )(x, nw, mod)


def _resid_norm(x, mix, nw, mod, name):
    t = x.shape[0]
    tm = min(t, 512)

    def body(x_ref, mix_ref, nw_ref, mod_ref, x1_ref, h_ref):
        x1 = x_ref[...] + mod_ref[2:3, :] * mix_ref[...]
        x1_ref[...] = x1
        r = lax.rsqrt(jnp.mean(x1 * x1, axis=-1, keepdims=True) + EPS)
        h_ref[...] = (x1 * r * nw_ref[...] * (1.0 + mod_ref[4:5, :]) + mod_ref[3:4, :]).astype(BF16)

    return pl.pallas_call(
        body, name=name, grid=(t // tm,),
        in_specs=[_row_spec(tm), _row_spec(tm), _fix_spec((1, D_MODEL)), _fix_spec((8, D_MODEL))],
        out_specs=[_row_spec(tm), _row_spec(tm)],
        out_shape=[jax.ShapeDtypeStruct((t, D_MODEL), F32), jax.ShapeDtypeStruct((t, D_MODEL), BF16)],
        compiler_params=_cp(("parallel",)),
    )(x, mix, nw, mod)


def _act_fwd(gu, name):
    t = gu.shape[0]
    tm, tn = min(t, 512), D_FF // 2
    nb = D_FF // tn

    def body(g_ref, u_ref, a_ref):
        g = g_ref[...].astype(F32)
        a_ref[...] = (g * _sig(g) * u_ref[...].astype(F32)).astype(BF16)

    return pl.pallas_call(
        body, name=name, grid=(t // tm, nb),
        in_specs=[pl.BlockSpec((tm, tn), lambda i, j: (i, j)), pl.BlockSpec((tm, tn), lambda i, j: (i, j + nb))],
        out_specs=pl.BlockSpec((tm, tn), lambda i, j: (i, j)),
        out_shape=jax.ShapeDtypeStruct((t, D_FF), BF16),
        compiler_params=_cp(("parallel", "parallel")),
    )(gu, gu)


def _act_bwd(dact, gu, name):
    t = gu.shape[0]
    tm = min(t, 256)

    def body(d_ref, g_ref, u_ref, o_ref):
        g, d = g_ref[...].astype(F32), d_ref[...].astype(F32)
        s = _sig(g)
        o_ref[:, 0:D_FF] = (d * u_ref[...].astype(F32) * s * (1.0 + g * (1.0 - s))).astype(BF16)
        o_ref[:, D_FF:2 * D_FF] = (d * g * s).astype(BF16)

    return pl.pallas_call(
        body, name=name, grid=(t // tm,),
        in_specs=[pl.BlockSpec((tm, D_FF), lambda i: (i, 0)), pl.BlockSpec((tm, D_FF), lambda i: (i, 0)),
                  pl.BlockSpec((tm, D_FF), lambda i: (i, 1))],
        out_specs=pl.BlockSpec((tm, 2 * D_FF), lambda i: (i, 0)),
        out_shape=jax.ShapeDtypeStruct((t, 2 * D_FF), BF16),
        compiler_params=_cp(("parallel",)),
    )(dact, gu, gu)


def _loss_head(x1, ffn, tgt, mod, name):
    t = x1.shape[0]
    tm = min(t, 512)

    def body(x1_ref, f_ref, t_ref, mod_ref, dffn_ref, dout_ref, dg2_ref, loss_ref):
        i = pl.program_id(0)
        g2 = mod_ref[5:6, :]
        f = f_ref[...]
        err = x1_ref[...] + g2 * f - t_ref[...]
        dout = err * (1.0 / D_MODEL)
        dout_ref[...] = dout
        dffn_ref[...] = (dout * g2).astype(BF16)
        part = jnp.zeros((8, 128), F32) + 0.5 * jnp.sum(jnp.mean(err * err, axis=-1, keepdims=True))

        @pl.when(i == 0)
        def _():
            dg2_ref[...] = _colsum(dout * f)
            loss_ref[...] = part

        @pl.when(i > 0)
        def _():
            dg2_ref[...] += _colsum(dout * f)
            loss_ref[...] += part

    return pl.pallas_call(
        body, name=name, grid=(t // tm,),
        in_specs=[_row_spec(tm), _row_spec(tm), _row_spec(tm), _fix_spec((8, D_MODEL))],
        out_specs=[_row_spec(tm), _row_spec(tm), _fix_spec((1, D_MODEL)), _fix_spec((8, 128))],
        out_shape=[jax.ShapeDtypeStruct((t, D_MODEL), BF16), jax.ShapeDtypeStruct((t, D_MODEL), F32),
                   jax.ShapeDtypeStruct((1, D_MODEL), F32), jax.ShapeDtypeStruct((8, 128), F32)],
        compiler_params=_cp(("arbitrary",)),
    )(x1, ffn, tgt, mod)


def _norm_bwd(dh, xin, dres, aux, nw, mod, row_sh, gate_row, name):
    t = xin.shape[0]
    tm = min(t, 512)
    with_gate = gate_row is not None

    def body(*refs):
        if with_gate:
            dh_ref, x_ref, dr_ref, aux_ref, nw_ref, mod_ref, dx_ref, dg_ref, acc_ref = refs
        else:
            dh_ref, x_ref, dr_ref, nw_ref, mod_ref, dx_ref, acc_ref = refs
        i = pl.program_id(0)
        xv, dhv = x_ref[...], dh_ref[...].astype(F32)
        r = lax.rsqrt(jnp.mean(xv * xv, axis=-1, keepdims=True) + EPS)
        xn = xv * r
        nwv = nw_ref[...]
        sc1 = 1.0 + mod_ref[row_sh + 1:row_sh + 2, :]
        dxn = dhv * (nwv * sc1)
        dx = dr_ref[...] + r * (dxn - xn * jnp.mean(dxn * xn, axis=-1, keepdims=True))
        dx_ref[...] = dx
        dhx = dhv * xn
        rows = [_colsum(dhv), _colsum(dhx * nwv), _colsum(dhx * sc1)]
        if with_gate:
            dg_ref[...] = (dx * mod_ref[gate_row:gate_row + 1, :]).astype(BF16)
            rows.append(_colsum(dx * aux_ref[...]))

        @pl.when(i == 0)
        def _():
            acc_ref[...] = jnp.zeros_like(acc_ref)

        for k, v in enumerate(rows):
            acc_ref[k:k + 1, :] += v

    ins = [dh, xin, dres] + ([aux] if with_gate else []) + [nw, mod]
    in_specs = [_row_spec(tm)] * (4 if with_gate else 3) + [_fix_spec((1, D_MODEL)), _fix_spec((8, D_MODEL))]
    out_specs = [_row_spec(tm)] + ([_row_spec(tm)] if with_gate else []) + [_fix_spec((8, D_MODEL))]
    out_shape = ([jax.ShapeDtypeStruct((t, D_MODEL), F32)]
                 + ([jax.ShapeDtypeStruct((t, D_MODEL), BF16)] if with_gate else [])
                 + [jax.ShapeDtypeStruct((8, D_MODEL), F32)])
    return pl.pallas_call(
        body, name=name, grid=(t // tm,), in_specs=in_specs, out_specs=out_specs, out_shape=out_shape,
        compiler_params=_cp(("arbitrary",)),
    )(*ins)


XBC_COL0 = 4096 // 128
DT_COL = 5632 // 128


def _conv_pre(xv, w_ref, b_ref):
    t = xv.shape[0]
    row = lax.broadcasted_iota(jnp.int32, xv.shape, 0)
    pre = xv * w_ref[3:4, :] + b_ref[...]
    shifted = []
    for k in range(3):
        s = 3 - k
        xs = jnp.where(row >= s, pltpu.roll(xv, s, 0), 0.0)
        shifted.append(xs)
        pre = pre + xs * w_ref[k:k + 1, :]
    return pre, shifted, row, t


def _conv_fwd(proj, conv_w, conv_b, name):
    t = proj.shape[0]

    def body(x_ref, w_ref, b_ref, u_ref):
        pre, _, _, _ = _conv_pre(x_ref[...], w_ref, b_ref)
        u_ref[...] = pre * _sig(pre)

    return pl.pallas_call(
        body, name=name, grid=(D_CONV // 128,),
        in_specs=[pl.BlockSpec((t, 128), lambda j: (0, XBC_COL0 + j)), pl.BlockSpec((4, 128), lambda j: (0, j)),
                  pl.BlockSpec((1, 128), lambda j: (0, j))],
        out_specs=pl.BlockSpec((t, 128), lambda j: (0, j)),
        out_shape=jax.ShapeDtypeStruct((t, D_CONV), F32),
        compiler_params=_cp(("parallel",)),
    )(proj, conv_w, conv_b)


def _conv_bwd(proj, du, conv_w, conv_b, name):
    t = proj.shape[0]

    def body(x_ref, du_ref, w_ref, b_ref, dx_ref, dw_ref, db_ref):
        pre, shifted, row, _ = _conv_pre(x_ref[...], w_ref, b_ref)
        s = _sig(pre)
        dpre = du_ref[...] * s * (1.0 + pre * (1.0 - s))
        db_ref[...] = _colsum(dpre)
        dx = dpre * w_ref[3:4, :]
        dw_ref[3:4, :] = _colsum(dpre * x_ref[...])
        for k in range(3):
            sft = 3 - k
            dw_ref[k:k + 1, :] = _colsum(dpre * shifted[k])
            back = jnp.where(row < t - sft, pltpu.roll(dpre, t - sft, 0), 0.0)
            dx = dx + back * w_ref[k:k + 1, :]
        dx_ref[...] = dx.astype(BF16)

    return pl.pallas_call(
        body, name=name, grid=(D_CONV // 128,),
        in_specs=[pl.BlockSpec((t, 128), lambda j: (0, XBC_COL0 + j)), pl.BlockSpec((t, 128), lambda j: (0, j)),
                  pl.BlockSpec((4, 128), lambda j: (0, j)), pl.BlockSpec((1, 128), lambda j: (0, j))],
        out_specs=[pl.BlockSpec((t, 128), lambda j: (0, j)), pl.BlockSpec((4, 128), lambda j: (0, j)),
                   pl.BlockSpec((1, 128), lambda j: (0, j))],
        out_shape=[jax.ShapeDtypeStruct((t, D_CONV), BF16), jax.ShapeDtypeStruct((4, D_CONV), F32),
                   jax.ShapeDtypeStruct((1, D_CONV), F32)],
        compiler_params=_cp(("parallel",)),
    )(proj, du, conv_w, conv_b)


def _ssd_common(dtraw_ref, dtb_ref, alog_ref, tril, expand, l_s, lt_s):
    lane = lax.broadcasted_iota(jnp.int32, (1, 128), 1)
    dt = _softplus(dtraw_ref[...] + dtb_ref[...])
    a = jnp.where(lane < N_HEADS, -jnp.exp(alog_ref[...]), 0.0)
    lcs = _dotx_l(tril, dt * a)
    l_s[...] = lcs
    lt_s[...] = lcs.T
    llast = l_s[CHUNK - 1:CHUNK, :]
    ea = _dotx_r(jnp.exp(lcs), expand, n=2)
    ds = _dotx_r(jnp.exp(llast - lcs), expand, n=2)
    dtx = _dotx_r(dt, expand, n=2)
    return dt, a, lcs, llast, ea, ds, dtx


def _head_col(lcs, h):
    lane = lax.broadcasted_iota(jnp.int32, lcs.shape, 1)
    return jnp.sum(jnp.where(lane == h, lcs, 0.0), axis=1, keepdims=True)


def _decay(lcs, lt_s, h, causal):
    seg = _head_col(lcs, h) - lt_s[h:h + 1, :]
    return jnp.exp(jnp.where(causal, seg, -1e30))


def _ssd_fwd(u, proj, dtb, alog, dsk, nw, cst, name):
    t = u.shape[0]
    nc = t // CHUNK

    def body(xs_ref, b_ref, c_ref, dtraw_ref, z_ref, dtb_ref, alog_ref, dsk_ref, nw_ref, tril_ref, exp_ref,
             y_ref, yn_ref, prev_ref, carry, l_s, lt_s, yd_s):
        i = pl.program_id(0)

        @pl.when(i == 0)
        def _():
            carry[...] = jnp.zeros_like(carry)

        expand = exp_ref[...]
        dt, a, lcs, llast, ea, ds, dtx = _ssd_common(dtraw_ref, dtb_ref, alog_ref, tril_ref[...], expand, l_s, lt_s)
        xs = xs_ref[...]
        xg = xs * dtx
        xgb = xg.astype(BF16)
        xgd = (xg * ds).astype(BF16)
        prev = carry[...]
        prev_ref[0] = prev
        prevb = prev.astype(BF16)
        ri = lax.broadcasted_iota(jnp.int32, (CHUNK, CHUNK), 0)
        ci = lax.broadcasted_iota(jnp.int32, (CHUNK, CHUNK), 1)
        causal = ri >= ci
        lane = lax.broadcasted_iota(jnp.int32, (1, 128), 1)
        new_states, yoff = [], []
        for g in range(2):
            bg = b_ref[:, g * 128:(g + 1) * 128].astype(BF16)
            cg = c_ref[:, g * 128:(g + 1) * 128].astype(BF16)
            sc = _dot(cg, bg, "nt")
            gs = slice(g * 512, (g + 1) * 512)
            new_states.append(_dot(bg, xgd[:, gs], "tn"))
            yoff.append(_dot(cg, prevb[:, gs]))
            for pr in range(4):
                col = g * 512 + pr * 128
                xp = xgb[:, col:col + 128]
                acc = jnp.zeros((CHUNK, 128), F32)
                for half in range(2):
                    h = g * 8 + pr * 2 + half
                    m = (sc * _decay(lcs, lt_s, h, causal)).astype(BF16)
                    keep = (lane < HEAD_DIM) if half == 0 else (lane >= HEAD_DIM)
                    acc = acc + _dot(m, jnp.where(keep, xp, jnp.zeros_like(xp)))
                yd_s[:, col:col + 128] = acc
        y = yd_s[...] + jnp.concatenate(yoff, axis=1) * ea + xs * dsk_ref[...]
        y_ref[...] = y
        carry[...] = prev * jnp.max(_dotx_r(jnp.exp(llast) + jnp.zeros((8, 128), F32), expand, n=2), axis=0, keepdims=True) \
            + jnp.concatenate(new_states, axis=1)
        z = z_ref[...]
        yz = y * (z * _sig(z))
        nwv = nw_ref[...]
        for g in range(2):
            gs = slice(g * 512, (g + 1) * 512)
            v = yz[:, gs]
            r = lax.rsqrt(jnp.mean(v * v, axis=-1, keepdims=True) + EPS)
            yn_ref[:, gs] = (v * r * nwv[:, gs]).astype(BF16)

    row = lambda w, col: pl.BlockSpec((CHUNK, w), lambda i: (i, col))
    return pl.pallas_call(
        body, name=name, grid=(nc,),
        in_specs=[row(1024, 0), row(256, 4), row(256, 5), row(128, DT_COL), row(1024, 0),
                  _fix_spec((1, 128)), _fix_spec((1, 128)), _fix_spec((1, D_MODEL)), _fix_spec((1, D_MODEL)),
                  _fix_spec((CHUNK, CHUNK)), _fix_spec((128, D_MODEL))],
        out_specs=[row(1024, 0), row(1024, 0), pl.BlockSpec((1, 128, D_MODEL), lambda i: (i, 0, 0))],
        out_shape=[jax.ShapeDtypeStruct((t, D_MODEL), F32), jax.ShapeDtypeStruct((t, 2 * D_MODEL), BF16),
                   jax.ShapeDtypeStruct((nc, 128, D_MODEL), F32)],
        scratch_shapes=[pltpu.VMEM((128, D_MODEL), F32), pltpu.VMEM((128, 128), F32), pltpu.VMEM((128, 128), F32),
                        pltpu.VMEM((CHUNK, D_MODEL), F32)],
        compiler_params=_cp(("arbitrary",)),
    )(u, u, u, proj, proj, dtb, alog, dsk, nw, cst["tril"], cst["expand"])


def _ssd_bwd(u, proj, y, prev, dycat, dtb, alog, dsk, nw, gpack, cst, name):
    t = u.shape[0]
    nc = t // CHUNK

    def body(xs_ref, b_ref, c_ref, dtraw_ref, z_ref, y_ref, prev_ref, dyn_ref, dtb_ref, alog_ref, dsk_ref, nw_ref,
             tril_ref, triu_ref, exp_ref, hs_ref, g_ref,
             du_ref, ddt_ref, dz_ref, acc_ref, acc16_ref, got_ref, dcarry, l_s, lt_s, dxg_s, wss, wrs):
        i = pl.program_id(0)
        w_start, w_finish = _swap_stages(g_ref, got_ref, wss, wrs)
        pl.when(i == 0)(w_start)

        @pl.when(i == 0)
        def _():
            dcarry[...] = jnp.zeros_like(dcarry)
            acc_ref[...] = jnp.zeros_like(acc_ref)
            acc16_ref[...] = jnp.zeros_like(acc16_ref)

        expand, hsum = exp_ref[...], hs_ref[...]
        dt, a, lcs, llast, ea, ds, dtx = _ssd_common(dtraw_ref, dtb_ref, alog_ref, tril_ref[...], expand, l_s, lt_s)
        xs = xs_ref[...]
        xg = xs * dtx
        xgb = xg.astype(BF16)
        xgdf = xg * ds
        xgd = xgdf.astype(BF16)
        dsk_v, nwv = dsk_ref[...], nw_ref[...]
        z, y = z_ref[...], y_ref[...]
        sz = _sig(z)
        silz = z * sz
        yz = y * silz
        dyn = dyn_ref[...].astype(F32)
        dyz_parts, dnw_parts = [], []
        for g in range(2):
            gs = slice(g * 512, (g + 1) * 512)
            v = yz[:, gs]
            r = lax.rsqrt(jnp.mean(v * v, axis=-1, keepdims=True) + EPS)
            yhat = v * r
            dnw_parts.append(_colsum(dyn[:, gs] * yhat))
            dw = dyn[:, gs] * nwv[:, gs]
            dyz_parts.append(r * (dw - yhat * jnp.mean(dw * yhat, axis=-1, keepdims=True)))
        dyz = jnp.concatenate(dyz_parts, axis=1)
        dy = dyz * silz
        dz_ref[...] = (dyz * y * (sz * (1.0 + z * (1.0 - sz)))).astype(BF16)
        acc_ref[0:1, :] += jnp.concatenate(dnw_parts, axis=1)
        acc_ref[1:2, :] += _colsum(dy * xs)
        dyb = dy.astype(BF16)
        dq = (dy * ea).astype(BF16)
        dcar = dcarry[...]
        dcarb = dcar.astype(BF16)
        prev = prev_ref[0]
        prevb = prev.astype(BF16)
        ri = lax.broadcasted_iota(jnp.int32, (CHUNK, CHUNK), 0)
        ci = lax.broadcasted_iota(jnp.int32, (CHUNK, CHUNK), 1)
        causal = ri >= ci
        lane = lax.broadcasted_iota(jnp.int32, (1, 128), 1)
        dprev, dxgd, yoff = [], [], []
        dl_l = jnp.zeros((CHUNK, CHUNK), F32)
        dl_s = jnp.zeros((CHUNK, CHUNK), F32)
        for g in range(2):
            gs = slice(g * 512, (g + 1) * 512)
            bg = b_ref[:, g * 128:(g + 1) * 128].astype(BF16)
            cg = c_ref[:, g * 128:(g + 1) * 128].astype(BF16)
            sc = _dot(cg, bg, "nt")
            yoff.append(_dot(cg, prevb[:, gs]))
            dcg = _dot(dq[:, gs], prevb[:, gs], "nt")
            dprev.append(_dot(cg, dq[:, gs], "tn"))
            dbg = _dot(xgd[:, gs], dcarb[:, gs], "nt")
            dxgd.append(_dot(bg, dcarb[:, gs]))
            dsc = jnp.zeros((CHUNK, CHUNK), F32)
            for pr in range(4):
                col = g * 512 + pr * 128
                xp = xgb[:, col:col + 128]
                dyp = dyb[:, col:col + 128]
                acc = jnp.zeros((CHUNK, 128), F32)
                for half in range(2):
                    h = g * 8 + pr * 2 + half
                    dec = _decay(lcs, lt_s, h, causal)
                    mf = sc * dec
                    keep = (lane < HEAD_DIM) if half == 0 else (lane >= HEAD_DIM)
                    dyh = jnp.where(keep, dyp, jnp.zeros_like(dyp))
                    dm = _dot(dyh, xp, "nt")
                    acc = acc + _dot(mf.astype(BF16), dyh, "tn")
                    dsc = dsc + dm * dec
                    gm = dm * mf
                    dl_l = dl_l + jnp.where(ci == h, jnp.sum(gm, axis=1, keepdims=True), 0.0)
                    dl_s = dl_s + jnp.where(ri == h, jnp.sum(gm, axis=0, keepdims=True), 0.0)
                dxg_s[:, col:col + 128] = acc
            dscb = dsc.astype(BF16)
            dcg = dcg + _dot(dscb, bg)
            dbg = dbg + _dot(dscb, cg, "tn")
            du_ref[:, 1024 + g * 128:1024 + (g + 1) * 128] = dbg
            du_ref[:, 1280 + g * 128:1280 + (g + 1) * 128] = dcg
        dxgd = jnp.concatenate(dxgd, axis=1)
        dxg = dxg_s[...] + dxgd * ds
        du_ref[:, 0:1024] = dy * dsk_v + dxg * dtx
        hs1 = _dotx_r(dxg * xs, hsum, n=2)
        yoff = jnp.concatenate(yoff, axis=1) * ea
        dl = dl_l - dl_s.T + _dotx_r(dy * yoff - xgdf * dxgd, hsum, n=2)
        rows8 = lax.broadcasted_iota(jnp.int32, (8, D_MODEL), 0)
        two = jnp.where(rows8 == 0, _colsum(dxgd * xgdf), jnp.where(rows8 == 1, _colsum(dcar * prev), 0.0))
        two = _dotx_r(two, hsum)
        r8 = lax.broadcasted_iota(jnp.int32, (8, 128), 0)
        dllast = _colsum(jnp.where(r8 == 0, two, 0.0)) + _colsum(jnp.where(r8 == 1, two, 0.0)) * jnp.exp(llast)
        rowi = lax.broadcasted_iota(jnp.int32, (CHUNK, 128), 0)
        dl = dl + jnp.where(rowi == CHUNK - 1, dllast, 0.0)
        dadt = _dotx_l(triu_ref[...], dl)
        ddt = dadt * a + hs1
        draw = ddt * _sig(dtraw_ref[...] + dtb_ref[...])
        ddt_ref[...] = draw.astype(BF16)
        acc16_ref[0:1, :] += _colsum(draw)
        acc16_ref[1:2, :] += _colsum(dadt * dt) * a
        dcarry[...] = dcar * jnp.max(_dotx_r(jnp.exp(llast) + jnp.zeros((8, 128), F32), expand, n=2), axis=0, keepdims=True) \
            + jnp.concatenate(dprev, axis=1)

        @pl.when(i == nc - 1)
        def _():
            hd = _dotx_r(acc_ref[...], hsum)
            acc16_ref[2:3, :] = _colsum(jnp.where(lax.broadcasted_iota(jnp.int32, (8, 128), 0) == 1, hd, 0.0))

        pl.when(i == nc - 1)(w_finish)

    rev = lambda w, col: pl.BlockSpec((CHUNK, w), lambda i: (nc - 1 - i, col))
    return pl.pallas_call(
        body, name=name, grid=(nc,),
        in_specs=[rev(1024, 0), rev(256, 4), rev(256, 5), rev(128, DT_COL), rev(1024, 0), rev(1024, 0),
                  pl.BlockSpec((1, 128, D_MODEL), lambda i: (nc - 1 - i, 0, 0)), rev(1024, 0),
                  _fix_spec((1, 128)), _fix_spec((1, 128)), _fix_spec((1, D_MODEL)), _fix_spec((1, D_MODEL)),
                  _fix_spec((CHUNK, CHUNK)), _fix_spec((CHUNK, CHUNK)), _fix_spec((128, D_MODEL)),
                  _fix_spec((D_MODEL, 128)), HB],
        out_specs=[rev(D_CONV, 0), rev(128, 0), rev(1024, 0), _fix_spec((8, D_MODEL)), _fix_spec((8, 128)), HB],
        out_shape=[jax.ShapeDtypeStruct((t, D_CONV), F32), jax.ShapeDtypeStruct((t, 128), BF16),
                   jax.ShapeDtypeStruct((t, D_MODEL), BF16), jax.ShapeDtypeStruct((8, D_MODEL), F32),
                   jax.ShapeDtypeStruct((8, 128), F32), _swap_shape(gpack)],
        scratch_shapes=[pltpu.VMEM((128, D_MODEL), F32), pltpu.VMEM((128, 128), F32), pltpu.VMEM((128, 128), F32),
                        pltpu.VMEM((CHUNK, D_MODEL), F32)] + SWAP_SEMS,
        compiler_params=_cp(("arbitrary",)),
    )(u, u, u, proj, proj, y, prev, dycat, dtb, alog, dsk, nw,
      cst["tril"], cst["triu"], cst["expand"], cst["hsum"], gpack)


def _head_rms(v, hsum, expand):
    ms = _dotx_r(v * v, hsum, n=2) * (1.0 / HEAD_DIM)
    return _dotx_r(lax.rsqrt(ms + EPS), expand, n=2)


def _qk_fwd(proj, qw, kw, cst, name):
    t = proj.shape[0]
    tm = min(t, 256)
    scale = HEAD_DIM ** -0.5

    def body(q_ref, k_ref, v_ref, qw_ref, kw_ref, hs_ref, exp_ref, qs_ref, kn_ref, vb_ref):
        hsum, expand = hs_ref[...], exp_ref[...]
        q, k = q_ref[...], k_ref[...]
        qs_ref[...] = (q * _head_rms(q, hsum, expand) * qw_ref[...] * scale).astype(BF16)
        kn_ref[...] = (k * _head_rms(k, hsum, expand) * kw_ref[...]).astype(BF16)
        vb_ref[...] = v_ref[...].astype(BF16)

    return pl.pallas_call(
        body, name=name, grid=(t // tm,),
        in_specs=[_row_spec(tm, col=1), _row_spec(tm, col=2), _row_spec(tm, col=3),
                  _fix_spec((1, D_MODEL)), _fix_spec((1, D_MODEL)), _fix_spec((D_MODEL, 128)),
                  _fix_spec((128, D_MODEL))],
        out_specs=[_row_spec(tm)] * 3, out_shape=[jax.ShapeDtypeStruct((t, D_MODEL), BF16)] * 3,
        compiler_params=_cp(("parallel",)),
    )(proj, proj, proj, qw, kw, cst["hsum"], cst["expand"])


def _qk_bwd(proj, dqs, dkn, dv, qw, kw, cst, name):
    t = proj.shape[0]
    tm = min(t, 256)
    scale = HEAD_DIM ** -0.5

    def body(q_ref, k_ref, dq_ref, dk_ref, dv_ref, qw_ref, kw_ref, hs_ref, exp_ref, fold_ref,
             oq_ref, ok_ref, ov_ref, dw_ref):
        i = pl.program_id(0)
        hsum, expand = hs_ref[...], exp_ref[...]
        rows8 = lax.broadcasted_iota(jnp.int32, (8, D_MODEL), 0)
        sums = jnp.zeros((8, D_MODEL), F32)
        for n, (x_ref, d_ref, w_ref, o_ref, sc) in enumerate(
                [(q_ref, dq_ref, qw_ref, oq_ref, scale), (k_ref, dk_ref, kw_ref, ok_ref, 1.0)]):
            xv = x_ref[...]
            r = _head_rms(xv, hsum, expand)
            xhat = xv * r
            dn = d_ref[...] * sc
            sums = sums + jnp.where(rows8 == n, _colsum(dn * xhat), 0.0)
            dw = dn * w_ref[...]
            mean = _dotx_r(_dotx_r(dw * xhat, hsum, n=1), expand, n=2) * (1.0 / HEAD_DIM)
            o_ref[...] = (r * (dw - xhat * mean)).astype(BF16)
        ov_ref[...] = dv_ref[...].astype(BF16)
        folded = _dotx_r(sums, fold_ref[...])

        @pl.when(i == 0)
        def _():
            dw_ref[...] = folded

        @pl.when(i > 0)
        def _():
            dw_ref[...] += folded

    return pl.pallas_call(
        body, name=name, grid=(t // tm,),
        in_specs=[_row_spec(tm, col=1), _row_spec(tm, col=2), _row_spec(tm), _row_spec(tm), _row_spec(tm),
                  _fix_spec((1, D_MODEL)), _fix_spec((1, D_MODEL)), _fix_spec((D_MODEL, 128)),
                  _fix_spec((128, D_MODEL)), _fix_spec((D_MODEL, 128))],
        out_specs=[_row_spec(tm)] * 3 + [_fix_spec((8, 128))],
        out_shape=[jax.ShapeDtypeStruct((t, D_MODEL), BF16)] * 3 + [jax.ShapeDtypeStruct((8, 128), F32)],
        compiler_params=_cp(("arbitrary",)),
    )(proj, proj, dqs, dkn, dv, qw, kw, cst["hsum"], cst["expand"], cst["fold"])


def _sb_masks(i, kb, tq, tk):
    tpos = i * tq + lax.broadcasted_iota(jnp.int32, (tq, 1), 0)
    spos = kb * tk + lax.broadcasted_iota(jnp.int32, (1, tk), 1)
    return spos < tpos


def _grid_marks(n0, n1):
    j, i = pl.program_id(0), pl.program_id(1)
    return (jnp.logical_and(j == 0, i == 0), jnp.logical_and(j == n0 // 2, i == 0),
            jnp.logical_and(j == n0 - 1, i == n1 - 1))


def _sb_fwd(qs, kn, vb, pack, ycat, cst, name):
    t = qs.shape[0]
    tq = tk = min(t, SB_TILE)
    nq = t // tq
    pairs = FWD_PAIRS
    ngrp = D_MODEL // (128 * pairs)
    nh = 2 * pairs
    lanes = lambda p: slice(p * 128, (p + 1) * 128)

    def body(q_ref, k_ref, v_ref, u_ref, p_ref, yc_ref, rt_ref, ob_ref, cnt_ref, gat_ref, acc, rs, gss, grs):
        del yc_ref
        at_first, at_mid, at_last = _grid_marks(ngrp, nq)
        g_start, g_relay, g_finish = _gather_stages(p_ref, gat_ref, gss, grs)
        pl.when(at_first)(g_start)
        pl.when(at_mid)(g_relay)
        i = pl.program_id(1)
        lane = lax.broadcasted_iota(jnp.int32, (1, 128), 1)
        qh = []
        for p in range(pairs):
            q2 = q_ref[:, lanes(p)]
            zero = jnp.zeros_like(q2)
            qh += [jnp.where(lane < HEAD_DIM, q2, zero), jnp.where(lane >= HEAD_DIM, q2, zero)]
        acc[...] = jnp.zeros_like(acc)
        rs[...] = jnp.zeros_like(rs)
        ustrict = u_ref[...]

        def tile(kb, masked):
            off = pl.multiple_of(kb * tk, tk)
            k2 = [k_ref[pl.ds(off, tk), lanes(p)] for p in range(pairs)]
            v2 = [v_ref[pl.ds(off, tk), lanes(p)] for p in range(pairs)]
            strict = _sb_masks(i, kb, tq, tk) if masked else None
            s = [_dot(qh[h], k2[h // 2], "nt") for h in range(nh)]
            a, r, lb = [None] * nh, [None] * nh, [None] * nh
            for h in range(nh):
                sp = _softplus(s[h])
                a[h] = s[h] - sp
                r[h] = jnp.where(strict, -sp, 0.0) if masked else -sp
                lb[h] = _dot2(r[h], ustrict)
            for h in range(nh):
                lw = a[h] + lb[h] + rs[h]
                w = jnp.exp(jnp.where(strict, lw, -1e30) if masked else lw)
                rs[h] = rs[h] + jnp.sum(r[h], axis=1, keepdims=True)
                acc[h] = acc[h] + _dot(w.astype(BF16), v2[h // 2])

        tile(i, True)

        def live():
            return jnp.max(functools.reduce(jnp.maximum, [rs[h] for h in range(nh)]))

        def more(c):
            return jnp.logical_and(c[0] < i, c[1] > SB_DEAD)

        def step(c):
            tile(i - 1 - c[0], False)
            return c[0] + 1, live()

        n_off, _ = lax.while_loop(more, step, (jnp.int32(0), live()))
        cnt_ref[pl.program_id(0), i] = n_off.astype(F32)
        for p in range(pairs):
            rt_ref[:, lanes(p)] = jnp.where(lane < HEAD_DIM, rs[2 * p], rs[2 * p + 1])
            ob_ref[:, lanes(p)] = jnp.where(lane < HEAD_DIM, acc[2 * p], acc[2 * p + 1]).astype(BF16)
        pl.when(at_last)(g_finish)

    blk = lambda rows, imap: pl.BlockSpec((rows, (128 * pairs)), imap)
    return pl.pallas_call(
        body, name=name, grid=(ngrp, nq),
        in_specs=[blk(tq, lambda j, i: (i, j)), blk(t, lambda j, i: (0, j)), blk(t, lambda j, i: (0, j)),
                  _fix_spec((2 * tk, tk)), HB, pl.BlockSpec(memory_space=pl.ANY)],
        out_specs=[blk(tq, lambda j, i: (i, j)), blk(tq, lambda j, i: (i, ngrp + j)),
                   pl.BlockSpec(memory_space=pltpu.SMEM), HB],
        out_shape=[jax.ShapeDtypeStruct((t, D_MODEL), F32), jax.ShapeDtypeStruct(ycat.shape, ycat.dtype),
                   jax.ShapeDtypeStruct((ngrp, nq), F32),
                   jax.ShapeDtypeStruct((N_SHARDS,) + pack.shape, pack.dtype)],
        scratch_shapes=[pltpu.VMEM((nh, tq, 128), F32), pltpu.VMEM((nh, tq, 1), F32)] + GATHER_SEMS,
        input_output_aliases={5: 1},
        compiler_params=_cp(("arbitrary", "arbitrary")),
    )(qs, kn, vb, _doubled(cst["ustrict"], tk), pack, ycat)


def _sb_bwd(qs, kn, vb, rtot, cnt, dycat, csum_b, cst, name):
    t = qs.shape[0]
    tq = tk = min(t, SB_TILE)
    nq = t // tq
    pairs = BWD_PAIRS
    ngrp = D_MODEL // (128 * pairs)
    nh = 2 * pairs
    lanes = lambda p: slice(p * 128, (p + 1) * 128)

    def body(q_ref, k_ref, v_ref, rt_ref, do_ref, us_ref, ui_ref, cnt_ref, xs_ref, dq_ref, dk_ref, dv_ref, xr_ref,
             acc, rs, es, xss, xrs):
        at_first, _, at_last = _grid_marks(ngrp, nq)
        x_start, x_finish = _exchange_stages(xs_ref, xr_ref, xss, xrs)
        pl.when(at_first)(x_start)
        i = pl.program_id(1)
        lane = lax.broadcasted_iota(jnp.int32, (1, 128), 1)
        keep = [lane < HEAD_DIM, lane >= HEAD_DIM]
        qh, doh, rtot_h = [], [], []
        for p in range(pairs):
            q2, rt = q_ref[:, lanes(p)], rt_ref[:, lanes(p)]
            do2b = do_ref[:, lanes(p)].astype(BF16)
            qh += [jnp.where(kp, q2, jnp.zeros_like(q2)) for kp in keep]
            doh += [jnp.where(kp, do2b, jnp.zeros_like(do2b)) for kp in keep]
            rtot_h += [jnp.sum(jnp.where(lane == n * HEAD_DIM, rt, 0.0), axis=1, keepdims=True) for n in range(2)]
        acc[...] = jnp.zeros_like(acc)
        rs[...] = jnp.zeros_like(rs)
        es[...] = jnp.zeros_like(es)

        @pl.when(i == 0)
        def _():
            dk_ref[...] = jnp.zeros_like(dk_ref)
            dv_ref[...] = jnp.zeros_like(dv_ref)

        ule, ult = us_ref[...], ui_ref[...]

        def tile(kb, masked):
            off = pl.multiple_of(kb * tk, tk)
            k2 = [k_ref[pl.ds(off, tk), lanes(p)] for p in range(pairs)]
            v2 = [v_ref[pl.ds(off, tk), lanes(p)] for p in range(pairs)]
            strict = _sb_masks(i, kb, tq, tk) if masked else None
            s = [_dot(qh[h], k2[h // 2], "nt") for h in range(nh)]
            dw = [_dot(doh[h], v2[h // 2], "nt") for h in range(nh)]
            a, sg, r, pin, w, e, cin = ([None] * nh for _ in range(7))
            for h in range(nh):
                sp = _softplus(s[h])
                a[h] = s[h] - sp
                sg[h] = jnp.exp(a[h])
                r[h] = jnp.where(strict, -sp, 0.0) if masked else -sp
                pin[h] = _dot2(r[h], ule)
            for h in range(nh):
                lw = a[h] + ((rtot_h[h] - rs[h]) - pin[h])
                w[h] = jnp.exp(jnp.where(strict, lw, -1e30) if masked else lw)
                e[h] = w[h] * dw[h]
                cin[h] = _dot2(e[h], ult)
            for p in range(pairs):
                dk_t = jnp.zeros((tk, 128), F32)
                dv_t = jnp.zeros((tk, 128), F32)
                for h in (2 * p, 2 * p + 1):
                    dl = e[h] * (1.0 - sg[h]) - (es[h] + cin[h]) * sg[h]
                    dl = (jnp.where(strict, dl, 0.0) if masked else dl).astype(BF16)
                    rs[h] = rs[h] + jnp.sum(r[h], axis=1, keepdims=True)
                    es[h] = es[h] + jnp.sum(e[h], axis=1, keepdims=True)
                    acc[h] = acc[h] + _dot(dl, k2[p])
                    dk_t = dk_t + _dot(dl, qh[h], "tn")
                    dv_t = dv_t + _dot(w[h].astype(BF16), doh[h], "tn")
                dk_ref[pl.ds(off, tk), lanes(p)] += dk_t
                dv_ref[pl.ds(off, tk), lanes(p)] += dv_t

        def step(kb, carry):
            tile(kb, False)
            return carry

        n_off = cnt_ref[pl.program_id(0) * BWD_PAIRS // FWD_PAIRS, i].astype(jnp.int32)
        lax.fori_loop(i - n_off, i, step, 0)
        tile(i, True)
        for p in range(pairs):
            dq_ref[:, lanes(p)] = jnp.where(lane < HEAD_DIM, acc[2 * p], acc[2 * p + 1])
        pl.when(at_last)(x_finish)

    blk = lambda rows, imap: pl.BlockSpec((rows, (128 * pairs)), imap)
    return pl.pallas_call(
        body, name=name, grid=(ngrp, nq),
        in_specs=[blk(tq, lambda j, i: (i, j)), blk(t, lambda j, i: (0, j)), blk(t, lambda j, i: (0, j)),
                  blk(tq, lambda j, i: (i, j)), blk(tq, lambda j, i: (i, ngrp + j)),
                  _fix_spec((2 * tk, tk)), _fix_spec((2 * tk, tk)), pl.BlockSpec(memory_space=pltpu.SMEM), HB],
        out_specs=[blk(tq, lambda j, i: (i, j)), blk(t, lambda j, i: (0, j)), blk(t, lambda j, i: (0, j)), HB],
        out_shape=[jax.ShapeDtypeStruct((t, D_MODEL), F32)] * 3
        + [jax.ShapeDtypeStruct((3,) + csum_b.shape[1:], csum_b.dtype)],
        scratch_shapes=[pltpu.VMEM((nh, tq, 128), F32), pltpu.VMEM((nh, tq, 1), F32), pltpu.VMEM((nh, tq, 1), F32)]
        + EXCHANGE_SEMS,
        compiler_params=_cp(("arbitrary", "arbitrary")),
    )(qs, kn, vb, rtot, dycat, _doubled(cst["ule"], tk), _doubled(cst["ult"], tk), cnt, csum_b)


def _adamw(w, g, m, v, name):
    lead = (1,) * (w.ndim - 2)
    rows, cols = w.shape[-2:]
    fits = [d for d in range(8, rows, 8) if rows % d == 0 and d * cols * 4 <= ADAM_BLOCK_BYTES]
    tr = max(fits) if fits else rows
    c1 = 1.0 - ADAM_B1 ** ADAM_STEP
    c2 = 1.0 - ADAM_B2 ** ADAM_STEP

    def body(w_ref, g_ref, m_ref, v_ref, d_ref, nm_ref, nv_ref):
        gv = g_ref[...]
        nm = ADAM_B1 * m_ref[...] + (1.0 - ADAM_B1) * gv
        nv = ADAM_B2 * v_ref[...] + (1.0 - ADAM_B2) * (gv * gv)
        nm_ref[...] = nm
        nv_ref[...] = nv
        d_ref[...] = -ADAM_LR * ((nm / c1) / (jnp.sqrt(nv / c2) + ADAM_EPS) + ADAM_WD * w_ref[...])

    spec = pl.BlockSpec(lead + (tr, cols), lambda i: (0,) * len(lead) + (i, 0))
    return pl.pallas_call(
        body, name=name, grid=(rows // tr,), in_specs=[spec] * 4, out_specs=[spec] * 3,
        out_shape=[jax.ShapeDtypeStruct(w.shape, F32)] * 3, compiler_params=_cp(("parallel",)),
    )(w, g, m, v)


def _place():
    x, y, c = lax.axis_index("x"), lax.axis_index("y"), lax.axis_index("c")
    chips = [(1 - x, y), (x, 1 - y), (1 - x, 1 - y)]
    return x, y, c, chips


VM = pl.BlockSpec(memory_space=pltpu.VMEM)
HB = pl.BlockSpec(memory_space=pltpu.HBM)


def _gather_all(p_ref, gat_ref, ss, rs):
    x, y, c, _ = _place()
    me = 4 * x + 2 * y + c
    peers = [(x, y, 1 - c), (1 - x, y, c), (x, 1 - y, c), (1 - x, 1 - y, c),
             (1 - x, y, 1 - c), (x, 1 - y, 1 - c), (1 - x, 1 - y, 1 - c)]

    def copy(k, slot, to):
        return pltpu.make_async_remote_copy(src_ref=p_ref, dst_ref=gat_ref.at[slot], send_sem=ss.at[k],
                                            recv_sem=rs.at[k], device_id=to, device_id_type=MESH)

    sends = [copy(k, me, p) for k, p in enumerate(peers)]
    for s in sends:
        s.start()
    gat_ref[me] = p_ref[...]
    for k, p in enumerate(peers):
        copy(k, 4 * p[0] + 2 * p[1] + p[2], p).wait_recv()
    for s in sends:
        s.wait_send()


ALL_SEMS = [pltpu.SemaphoreType.DMA((7,)), pltpu.SemaphoreType.DMA((7,))]


def _small_reduce(pack, name):
    rows = pack.shape[0]

    def body(p_ref, gat_ref, sum_ref, ss, rs):
        _gather_all(p_ref, gat_ref, ss, rs)
        tot = gat_ref[0]
        for b in range(1, 8):
            tot = tot + gat_ref[b]
        sum_ref[...] = tot

    return pl.pallas_call(
        body, name=name, in_specs=[VM], out_specs=[VM, VM],
        out_shape=[jax.ShapeDtypeStruct((8, rows, D_MODEL), F32), jax.ShapeDtypeStruct((rows, D_MODEL), F32)],
        scratch_shapes=ALL_SEMS, compiler_params=_cp(),
    )(pack)


def _prologue(cpack, w_ada, b_shard, wpack_in, name):
    def body(cp_ref, w_ref, b_ref, p_ref, gat_ref, modp_ref, gin_ref, ss1, rs1, ss2, rs2, gss, grs):
        g_start, g_relay, g_finish = _gather_stages(p_ref, gin_ref, gss, grs)
        g_start()
        _gather_all(cp_ref, gat_ref, ss1, rs1)
        x, y, c, chips = _place()
        sh = 2 * x + y
        row = lax.broadcasted_iota(jnp.int32, (8, D_MODEL), 0)
        cv = jnp.zeros((8, D_MODEL), F32)
        for b in range(8):
            cv = jnp.where(row == b, gat_ref[b, 0:8, :], cv)
        cv = cv * _sig(cv)
        modp_ref[sh] = jnp.dot(cv, w_ref[...], precision=lax.Precision.HIGHEST,
                               preferred_element_type=F32) + b_ref[...]

        def copy(k, slot, to):
            return pltpu.make_async_remote_copy(src_ref=modp_ref.at[slot], dst_ref=modp_ref.at[slot],
                                                send_sem=ss2.at[k], recv_sem=rs2.at[k], device_id=to,
                                                device_id_type=MESH)

        sends = [copy(k, sh, (*ch, c)) for k, ch in enumerate(chips)]
        for s in sends:
            s.start()
        for k, ch in enumerate(chips):
            copy(k, 2 * ch[0] + ch[1], (*ch, c)).wait_recv()
        for s in sends:
            s.wait_send()
        g_relay()
        g_finish()

    return pl.pallas_call(
        body, name=name, in_specs=[VM, VM, VM, HB], out_specs=[VM, VM, HB],
        out_shape=[jax.ShapeDtypeStruct((8,) + cpack.shape, F32),
                   jax.ShapeDtypeStruct((N_SHARDS, 8, 6 * D_MODEL // N_SHARDS), F32),
                   jax.ShapeDtypeStruct((N_SHARDS,) + wpack_in.shape, wpack_in.dtype)],
        scratch_shapes=ALL_SEMS + EXCHANGE_SEMS + GATHER_SEMS, compiler_params=_cp(),
    )(cpack, w_ada, b_shard, wpack_in)


def _gather_stages(p_ref, out_ref, ss, rs):
    hf = p_ref.shape[0] // 2
    x, y, c, chips = _place()
    sh = 2 * x + y
    sib = (x, y, 1 - c)
    slots = [2 * ch[0] + ch[1] for ch in chips]

    def half(slot, hc):
        return out_ref.at[slot, pl.ds(hc * hf, hf), :]

    def copy(k, src, slot, hc, to):
        return pltpu.make_async_remote_copy(src_ref=src, dst_ref=half(slot, hc), send_sem=ss.at[k],
                                            recv_sem=rs.at[k], device_id=to, device_id_type=MESH)

    def first():
        return [copy(j, p_ref.at[pl.ds(c * hf, hf), :], sh, c, (*ch, c)) for j, ch in enumerate(chips)]

    def passed():
        return [copy(3 + j, half(slots[j], c), slots[j], c, sib) for j in range(3)]

    def own():
        return pltpu.make_async_remote_copy(src_ref=p_ref, dst_ref=out_ref.at[sh], send_sem=ss.at[6],
                                            recv_sem=rs.at[6], device_id=sib, device_id_type=MESH)

    def start():
        for cp in first() + [own()]:
            cp.start()

    def relay():
        for j, cp in enumerate(passed()):
            copy(j, half(slots[j], c), slots[j], c, (*chips[j], c)).wait_recv()
            cp.start()

    def finish():
        for j in range(3):
            copy(3 + j, half(slots[j], 1 - c), slots[j], 1 - c, sib).wait_recv()
        own().wait()
        for cp in first() + passed():
            cp.wait_send()

    return start, relay, finish


GATHER_SEMS = [pltpu.SemaphoreType.DMA((7,)), pltpu.SemaphoreType.DMA((7,))]


def _swap_stages(g_ref, out_ref, ss, rs):
    hf = g_ref.shape[1] // 2
    x, y, c, _ = _place()

    def copy():
        return pltpu.make_async_remote_copy(
            src_ref=g_ref.at[pl.ds(0, N_SHARDS), pl.ds((1 - c) * hf, hf), :], dst_ref=out_ref,
            send_sem=ss, recv_sem=rs, device_id=(x, y, 1 - c), device_id_type=MESH)

    return (lambda: copy().start()), (lambda: copy().wait())


SWAP_SEMS = [pltpu.SemaphoreType.DMA, pltpu.SemaphoreType.DMA]


def _swap_shape(g):
    return jax.ShapeDtypeStruct((N_SHARDS, g.shape[1] // 2, D_MODEL), g.dtype)


def _sibling_swap(g, name):
    def body(g_ref, out_ref, ss, rs):
        for stage in _swap_stages(g_ref, out_ref, ss, rs):
            stage()

    return pl.pallas_call(
        body, name=name, in_specs=[HB], out_specs=HB, out_shape=_swap_shape(g),
        scratch_shapes=SWAP_SEMS, compiler_params=_cp(),
    )(g)


def _row_tile(rows, width_bytes, cap_bytes):
    fits = [d for d in range(8, rows + 1, 8) if rows % d == 0 and d * width_bytes <= cap_bytes]
    return max(fits)


def _chip_sum(g, got, c_idx, name):
    hf = got.shape[1]
    tr = _row_tile(hf, D_MODEL * 4, 3 << 20)
    nb = hf // tr

    def body(c_ref, a_ref, b_ref, s_ref, sb_ref):
        s = a_ref[...] + b_ref[...]
        s_ref[...] = s
        sb_ref[...] = s.astype(BF16)

    blk = pl.BlockSpec((1, tr, D_MODEL), lambda s, i, c_ref: (s, i, 0))
    return pl.pallas_call(
        body, name=name,
        grid_spec=pltpu.PrefetchScalarGridSpec(
            num_scalar_prefetch=1, grid=(N_SHARDS, nb),
            in_specs=[pl.BlockSpec((1, tr, D_MODEL), lambda s, i, c_ref: (s, c_ref[0] * nb + i, 0)), blk],
            out_specs=[blk, blk]),
        out_shape=[jax.ShapeDtypeStruct((N_SHARDS, hf, D_MODEL), F32),
                   jax.ShapeDtypeStruct((N_SHARDS, hf, D_MODEL), BF16)],
        compiler_params=_cp(("parallel", "parallel")),
    )(c_idx, g, got)


def _exchange_stages(s_ref, out_ref, ss, rs):
    x, y, c, chips = _place()

    def sends():
        return [pltpu.make_async_remote_copy(src_ref=s_ref.at[2 * ch[0] + ch[1]], dst_ref=out_ref.at[k],
                                             send_sem=ss.at[k], recv_sem=rs.at[k], device_id=(*ch, c),
                                             device_id_type=MESH) for k, ch in enumerate(chips)]

    def start():
        for cp in sends():
            cp.start()

    def finish():
        for cp in sends():
            cp.wait()

    return start, finish


EXCHANGE_SEMS = [pltpu.SemaphoreType.DMA((3,)), pltpu.SemaphoreType.DMA((3,))]


def _total_half(s, got, sh_idx, name):
    hf = got.shape[1]
    tr = _row_tile(hf, D_MODEL * 4, 3 << 20)
    nb = hf // tr

    def body(sh_ref, a_ref, r0, r1, r2, o_ref):
        o_ref[...] = ((a_ref[0] + r0[0].astype(F32)) + r1[0].astype(F32)) + r2[0].astype(F32)

    rspec = lambda k: pl.BlockSpec((1, tr, D_MODEL), lambda i, sh_ref: (k, i, 0))
    return pl.pallas_call(
        body, name=name,
        grid_spec=pltpu.PrefetchScalarGridSpec(
            num_scalar_prefetch=1, grid=(nb,),
            in_specs=[pl.BlockSpec((1, tr, D_MODEL), lambda i, sh_ref: (sh_ref[0], i, 0)),
                      rspec(0), rspec(1), rspec(2)],
            out_specs=pl.BlockSpec((tr, D_MODEL), lambda i, sh_ref: (i, 0))),
        out_shape=jax.ShapeDtypeStruct((hf, D_MODEL), F32),
        compiler_params=_cp(("parallel",)),
    )(sh_idx, s, got, got, got)


def _join_halves(tot, name):
    def body(t_ref, out_ref, ss, rs):
        x, y, c, _ = _place()
        cp = pltpu.make_async_remote_copy(src_ref=t_ref, dst_ref=out_ref, send_sem=ss, recv_sem=rs,
                                          device_id=(x, y, 1 - c), device_id_type=MESH)
        cp.start()
        cp.wait()

    return pl.pallas_call(
        body, name=name, in_specs=[HB], out_specs=HB,
        out_shape=jax.ShapeDtypeStruct(tot.shape, F32),
        scratch_shapes=[pltpu.SemaphoreType.DMA, pltpu.SemaphoreType.DMA],
        compiler_params=_cp(),
    )(tot)


def _w_ada_grad(cond, dmod_cols, name):
    def body(c_ref, d_ref, o_ref):
        cv = c_ref[...]
        cv = cv * _sig(cv)
        o_ref[...] = lax.dot_general(cv, d_ref[...], _DN["tn"], precision=lax.Precision.HIGHEST,
                                     preferred_element_type=F32)

    return pl.pallas_call(
        body, name=name, in_specs=[VM, VM], out_specs=VM,
        out_shape=jax.ShapeDtypeStruct((D_MODEL, dmod_cols.shape[1]), F32), compiler_params=_cp(),
    )(cond, dmod_cols)


def _pad_rows(a, rows):
    return jnp.pad(a, ((0, rows - a.shape[0]), (0, 0)))


def _pad_cols(a, cols):
    return jnp.pad(a, ((0, 0), (0, cols - a.shape[1])))


def _unpack_rest(p):
    o = 0
    out = []
    for r in (R_OUT, R_FF, R_FF, R_FF):
        out.append(p[..., o:o + r, :])
        o += r
    return out


def _reduce_tail(csum, got2, shard, ac, tag):
    tot = _total_half(csum, got2, shard.reshape(1).astype(jnp.int32), "rs_total_" + tag)
    other = _join_halves(tot, "rs_join_" + tag)
    return jnp.where(ac == 0, jnp.concatenate([tot, other], axis=0), jnp.concatenate([other, tot], axis=0))


def _reduce_head(gpack, ac, tag):
    got = _sibling_swap(gpack, "rs_sibling_swap_" + tag)
    return _chip_sum(gpack, got, ac.reshape(1).astype(jnp.int32), "rs_chip_sum_" + tag)


def kernel(x, c, w_ada, b_ada, norm1_w, w_in, conv_w, conv_b, dt_bias, a_log, d_skip, ssd_norm_w, q_norm_w, k_norm_w, w_out, norm2_w, w_gate, w_up, w_down, loss_target, m_w_ada, m_b_ada, m_norm1_w, m_w_in, m_conv_w, m_conv_b, m_dt_bias, m_a_log, m_d_skip, m_ssd_norm_w, m_q_norm_w, m_k_norm_w, m_w_out, m_norm2_w, m_w_gate, m_w_up, m_w_down, v_w_ada, v_b_ada, v_norm1_w, v_w_in, v_conv_w, v_conv_b, v_dt_bias, v_a_log, v_d_skip, v_ssd_norm_w, v_q_norm_w, v_k_norm_w, v_w_out, v_norm2_w, v_w_gate, v_w_up, v_w_down):
    cst = _consts()
    ax, ay, ac = lax.axis_index("x"), lax.axis_index("y"), lax.axis_index("c")
    shard = 2 * ax + ay
    me = 4 * ax + 2 * ay + ac
    xs = x[0]
    tgt = loss_target[0]
    w_in_cols = w_in.shape[2]
    conv_cols = conv_w.shape[2]

    tr3 = lambda a: jnp.transpose(a, (0, 2, 1))
    lin = lambda a: tr3(a).reshape(-1, 128)
    unlin = lambda a: tr3(a.reshape(1, w_in_cols, D_MODEL))
    wpack_in = _pad_rows(tr3(w_in.astype(BF16))[0], R_IN)
    wpack_rest = jnp.concatenate([w_out[0], tr3(w_gate)[0], tr3(w_up)[0], w_down[0]], axis=0).astype(BF16)

    cw_flat = _pad_cols(conv_w[0].reshape(1, -1), 2 * D_MODEL).reshape(2, D_MODEL)
    cpack = jnp.concatenate([jnp.broadcast_to(c, (8, D_MODEL)), _pad_rows(cw_flat, 8)], axis=0)
    mod_w = 6 * D_MODEL // N_SHARDS
    b_shard = lax.dynamic_slice(b_ada, (0, shard * mod_w), (1, mod_w))
    gat, modp, gp_in = _prologue(cpack, w_ada[0], b_shard, wpack_in, "prologue")
    c_all = gat[:, 0, :]
    cw = gat[0::2, 8:10, :].reshape(N_SHARDS, 2 * D_MODEL)[:, :4 * conv_cols].reshape(N_SHARDS, 4, conv_cols)
    conv_w_full = jnp.transpose(cw, (1, 0, 2)).reshape(4, D_CONV)
    mod_mine = lax.dynamic_slice(modp, (0, me, 0), (N_SHARDS, 1, mod_w)).reshape(6, D_MODEL)
    mod = _pad_rows(mod_mine, 8)
    wi_t = gp_in[:, :w_in_cols, :].reshape(D_IN_PROJ, D_MODEL)
    w_inp_t = jnp.concatenate([wi_t[0:1024], wi_t[2576:5648], wi_t[1024:2560], wi_t[2560:2576],
                               jnp.zeros((D_PROJ_PAD - D_IN_PROJ, D_MODEL), BF16)], axis=0)

    pad128 = lambda a: _pad_cols(a, 128)
    dtb, alog = pad128(dt_bias), pad128(a_log)
    dsk = jnp.repeat(d_skip, HEAD_DIM, axis=1)
    qw, kw = jnp.tile(q_norm_w, (1, N_HEADS)), jnp.tile(k_norm_w, (1, N_HEADS))

    h1 = _norm_mod(xs, norm1_w, mod, 0, "norm1")
    proj = _matmul(h1, w_inp_t, "nt", F32, "in_proj")
    u = _conv_fwd(proj, conv_w_full, conv_b, "conv_fwd")
    y_ssd, yn, prev = _ssd_fwd(u, proj, dtb, alog, dsk, ssd_norm_w, cst, "ssd_fwd")
    qs, kn, vb = _qk_fwd(proj, qw, kw, cst, "qk_norm")
    rtot, ycat, cnt, gp_rest = _sb_fwd(qs, kn, vb, wpack_rest, yn, cst, "sb_fwd")
    p_out, p_gate, p_up, p_down = _unpack_rest(gp_rest)
    w_o = p_out.reshape(2 * D_MODEL, D_MODEL)
    w_gu_t = jnp.concatenate([p_gate.reshape(D_FF, D_MODEL), p_up.reshape(D_FF, D_MODEL)], axis=0)
    w_d = p_down.reshape(D_FF, D_MODEL)
    mix = _matmul(ycat, w_o, "nn", F32, "out_proj")
    x1, h2 = _resid_norm(xs, mix, norm2_w, mod, "resid_norm2")
    gu = _matmul(h2, w_gu_t, "nt", BF16, "ffn_in")
    act = _act_fwd(gu, "ffn_act")
    ffn = _matmul(act, w_d, "nn", F32, "ffn_out", tk_cap=1408)
    dffn, dout, dg2, loss8 = _loss_head(x1, ffn, tgt, mod, "loss_head")
    loss = lax.psum(loss8[0, 0], ("x", "y", "c"))

    dact = _matmul(dffn, w_d, "nt", BF16, "d_act")
    g_down = _matmul(act, dffn, "tn", F32, "g_w_down", tm_cap=1408)
    dgu = _act_bwd(dact, gu, "ffn_act_bwd")
    dh2 = _matmul(dgu, w_gu_t, "nn", BF16, "d_h2", tk_cap=1408)
    g_gu_t = _matmul(dgu, h2, "tn", F32, "g_w_gu", tm_cap=1408)
    dx1, dmix, acc2 = _norm_bwd(dh2, x1, dout, mix, norm2_w, mod, 3, 2, "norm2_bwd")
    dycat = _matmul(dmix, w_o, "nt", BF16, "d_ycat")
    g_out = _matmul(ycat, dmix, "tn", F32, "g_w_out")
    gpack_rest = jnp.concatenate([
        g_out.reshape(N_SHARDS, R_OUT, D_MODEL),
        g_gu_t[:D_FF].reshape(N_SHARDS, R_FF, D_MODEL), g_gu_t[D_FF:].reshape(N_SHARDS, R_FF, D_MODEL),
        g_down.reshape(N_SHARDS, R_FF, D_MODEL)], axis=1)
    du, ddt, dz, acc_ssd, acc16, got_s = _ssd_bwd(u, proj, y_ssd, prev, dycat, dtb, alog, dsk, ssd_norm_w,
                                                  gpack_rest, cst, "ssd_bwd")
    csum_r, csum_rb = _chip_sum(gpack_rest, got_s, ac.reshape(1).astype(jnp.int32), "rs_chip_sum_rest")
    dqs, dkn, dv, got_r = _sb_bwd(qs, kn, vb, rtot, cnt, dycat, csum_rb, cst, "sb_bwd")
    r_out, r_gate, r_up, r_down = _unpack_rest(_reduce_tail(csum_r, got_r, shard, ac, "rest"))
    dq, dk, dvb, acc_qk = _qk_bwd(proj, dqs, dkn, dv, qw, kw, cst, "qk_norm_bwd")
    dxbc, g_conv_w, g_conv_b = _conv_bwd(proj, du, conv_w_full, conv_b, "conv_bwd")
    dproj = jnp.concatenate([dz, dq, dk, dvb, dxbc, ddt], axis=1)
    g_inp_t = _matmul(dproj, h1, "tn", F32, "g_w_in", tm_cap=1920)
    gi_t = jnp.concatenate([g_inp_t[0:1024], g_inp_t[4096:5632], g_inp_t[5632:5648], g_inp_t[1024:4096]], axis=0)
    gpack_in = jnp.pad(gi_t.reshape(N_SHARDS, w_in_cols, D_MODEL), ((0, 0), (0, R_IN - w_in_cols), (0, 0)))
    csum_i, csum_ib = _reduce_head(gpack_in, ac, "in")
    dh1, got_i = _matmul(dproj, w_inp_t, "nn", BF16, "d_h1", tk_cap=1152, exchange=csum_ib)
    r_in = _reduce_tail(csum_i, got_i, shard, ac, "in")
    grad_x, acc1 = _norm_bwd(dh1, xs, dx1, None, norm1_w, mod, 0, None, "norm1_bwd")

    last = jnp.concatenate([acc_qk[0:1, 0:64], acc_qk[1:2, 0:64], acc16[0:1, 0:16], acc16[1:2, 0:16],
                            acc16[2:3, 0:16]], axis=1)
    spack = jnp.concatenate([
        acc1[0:2], acc2[3:4], acc2[0:2], dg2,
        acc1[2:3], acc2[2:3], acc_ssd[0:1],
        _pad_cols(g_conv_b, 2 * D_MODEL).reshape(2, D_MODEL),
        g_conv_w.reshape(6, D_MODEL),
        _pad_cols(last, D_MODEL)], axis=0)
    sgat, ssum = _small_reduce(_pad_rows(spack, SMALL_ROWS), "gather_small")
    g_b_ada = ssum[0:6].reshape(1, 6 * D_MODEL)
    g_norm1, g_norm2, g_ssdn = ssum[6:7], ssum[7:8], ssum[8:9]
    g_cb = ssum[9:11].reshape(1, 2 * D_MODEL)[:, :D_CONV]
    g_cw = lax.dynamic_slice(ssum[11:17].reshape(4, D_CONV), (0, shard * conv_cols), (4, conv_cols))
    g_qn, g_kn = ssum[17:18, 0:64], ssum[17:18, 64:128]
    g_dtb, g_alog, g_dsk = ssum[17:18, 128:144], ssum[17:18, 144:160], ssum[17:18, 160:176]
    dmod_all = sgat[:, 0:6, :].reshape(8, 6 * D_MODEL)
    g_w_ada = _w_ada_grad(c_all, lax.dynamic_slice(dmod_all, (0, shard * mod_w), (8, mod_w)), "g_w_ada")


    grads = dict(w_ada=g_w_ada, b_ada=g_b_ada, norm1_w=g_norm1, w_in=r_in[:w_in_cols].reshape(-1, 128), conv_w=g_cw,
                 conv_b=g_cb, dt_bias=g_dtb, a_log=g_alog, d_skip=g_dsk, ssd_norm_w=g_ssdn, q_norm_w=g_qn,
                 k_norm_w=g_kn, w_out=r_out, norm2_w=g_norm2, w_gate=r_gate, w_up=r_up, w_down=r_down)
    weights = dict(w_ada=(w_ada, m_w_ada, v_w_ada), b_ada=(b_ada, m_b_ada, v_b_ada),
                   norm1_w=(norm1_w, m_norm1_w, v_norm1_w), w_in=(w_in, m_w_in, v_w_in),
                   conv_w=(conv_w, m_conv_w, v_conv_w), conv_b=(conv_b, m_conv_b, v_conv_b),
                   dt_bias=(dt_bias, m_dt_bias, v_dt_bias), a_log=(a_log, m_a_log, v_a_log),
                   d_skip=(d_skip, m_d_skip, v_d_skip), ssd_norm_w=(ssd_norm_w, m_ssd_norm_w, v_ssd_norm_w),
                   q_norm_w=(q_norm_w, m_q_norm_w, v_q_norm_w), k_norm_w=(k_norm_w, m_k_norm_w, v_k_norm_w),
                   w_out=(w_out, m_w_out, v_w_out), norm2_w=(norm2_w, m_norm2_w, v_norm2_w),
                   w_gate=(w_gate, m_w_gate, v_w_gate), w_up=(w_up, m_w_up, v_w_up),
                   w_down=(w_down, m_w_down, v_w_down))
    views = dict(w_in=(lin, unlin), w_gate=(tr3, tr3), w_up=(tr3, tr3))
    same = lambda a: a
    names = list(weights)
    g_out_l, d_out_l, m_out_l, v_out_l = [], [], [], []
    for n in names:
        view, back = views.get(n, (same, same))
        w, m, v = (view(a) for a in weights[n])
        g = grads[n].reshape(w.shape)
        d, nm, nv = _adamw(w, g, m, v, "adamw_" + n)
        g_out_l.append(back(g))
        d_out_l.append(back(d))
        m_out_l.append(back(nm))
        v_out_l.append(back(nv))
    return (loss, grad_x[None], *g_out_l, *d_out_l, *m_out_l, *v_out_l)
```

```python
import functools

import numpy as np
import jax
import jax.numpy as jnp
from jax import lax
from jax.experimental import pallas as pl
from jax.experimental.pallas import tpu as pltpu

F32, BF16 = jnp.float32, jnp.bfloat16
MESH = pl.DeviceIdType.MESH

D_MODEL = 1024
HEAD_DIM = 64
N_HEADS = 16
D_CONV = 1536
D_FF = 2816
D_IN_PROJ = 5648
D_PROJ_PAD = 5760
CHUNK = 128
SB_TILE = 256
FWD_PAIRS, BWD_PAIRS = 4, 2
SB_DEAD = -105.0
EPS = 1e-6
N_SHARDS = 4
R_IN, R_OUT, R_FF = 1440, 512, 704
SMALL_ROWS = 24

ADAM_LR, ADAM_B1, ADAM_B2, ADAM_EPS, ADAM_WD, ADAM_STEP = 0.001, 0.9, 0.999, 1e-08, 0.01, 10

VMEM_LIMIT = 48 * 1024 * 1024
ADAM_BLOCK_BYTES = 3 * 512 * 1024

_DN = {"nn": (((1,), (0,)), ((), ())), "nt": (((1,), (1,)), ((), ())), "tn": (((0,), (0,)), ((), ()))}


def _dot(a, b, dims="nn"):
    return lax.dot_general(a, b, _DN[dims], preferred_element_type=F32)


def _pieces(x, n):
    out = []
    for _ in range(n - 1):
        hi = lax.bitcast_convert_type(lax.bitcast_convert_type(x, jnp.int32) & jnp.int32(-65536), F32)
        out.append(hi.astype(BF16))
        x = x - hi
    out.append(x.astype(BF16))
    return out


def _dotx_r(x, b_exact, n=3):
    return _dot(jnp.concatenate(_pieces(x, n), axis=1), jnp.concatenate([b_exact] * n, axis=0))


def _dotx_l(a_exact, x, n=3):
    return _dot(jnp.concatenate([a_exact] * n, axis=1), jnp.concatenate(_pieces(x, n), axis=0))


def _dot2(x, b2):
    return _dot(jnp.concatenate(_pieces(x, 2), axis=1), b2)


def _sig(x):
    return 1.0 / (1.0 + jnp.exp(-x))


def _softplus(x):
    return jnp.maximum(x, 0.0) + jnp.log(1.0 + jnp.exp(-jnp.abs(x)))


def _cp(sem=None, vmem=VMEM_LIMIT):
    return pltpu.CompilerParams(dimension_semantics=sem, vmem_limit_bytes=vmem)


def _colsum(x):
    return jnp.sum(x, axis=0, keepdims=True)


def _consts():
    ch = np.arange(D_MODEL)
    expand = (np.arange(128)[:, None] == (ch // HEAD_DIM)[None, :]).astype(np.float32)
    fold = (ch[:, None] % HEAD_DIM == np.arange(128)[None, :]).astype(np.float32)
    i = np.arange(CHUNK)
    tril = (i[:, None] >= i[None, :]).astype(np.float32)
    j = np.arange(SB_TILE)
    ustrict = (j[:, None] > j[None, :]).astype(np.float32)
    ule = (j[:, None] <= j[None, :]).astype(np.float32)
    ult = (j[:, None] < j[None, :]).astype(np.float32)
    c = lambda a: jnp.asarray(a, BF16)
    return dict(expand=c(expand), hsum=c(expand.T), fold=c(fold), tril=c(tril), triu=c(tril.T),
                ustrict=ustrict, ule=ule, ult=ult)


def _doubled(tri, tk):
    b = tri[:tk, :tk]
    return jnp.asarray(np.concatenate([b, b], axis=0), BF16)


def _pick(n, cap):
    best = 128
    for t in range(128, min(n, cap) + 1, 128):
        if n % t == 0:
            best = t
    return n if n <= cap else best


def _matmul(a, b, dims, out_dtype, name, tm_cap=1024, tn_cap=2048, tk_cap=1024, exchange=None):
    if dims == "nn":
        (m, k), (_, n) = a.shape, b.shape
    elif dims == "nt":
        (m, k), (n, _) = a.shape, b.shape
    else:
        (k, m), (_, n) = a.shape, b.shape
    tm, tn, tk = _pick(m, tm_cap), _pick(n, tn_cap), _pick(k, tk_cap)
    nk = k // tk
    a_spec = (pl.BlockSpec((tk, tm), lambda i, j, kk: (kk, i)) if dims == "tn"
              else pl.BlockSpec((tm, tk), lambda i, j, kk: (i, kk)))
    b_spec = (pl.BlockSpec((tn, tk), lambda i, j, kk: (j, kk)) if dims == "nt"
              else pl.BlockSpec((tk, tn), lambda i, j, kk: (kk, j)))

    grid = (m // tm, n // tn, nk)

    def body(a_ref, b_ref, *rest):
        if exchange is None:
            o_ref, acc_ref = rest
        else:
            xs_ref, o_ref, xr_ref, acc_ref, xss, xrs = rest
            ids = [pl.program_id(d) for d in range(3)]
            x_start, x_finish = _exchange_stages(xs_ref, xr_ref, xss, xrs)
            pl.when(functools.reduce(jnp.logical_and, [p == 0 for p in ids]))(x_start)
        kk = pl.program_id(2)
        part = _dot(a_ref[...], b_ref[...], dims)
        if nk == 1:
            o_ref[...] = part.astype(out_dtype)
        else:
            @pl.when(kk == 0)
            def _():
                acc_ref[...] = part

            @pl.when(kk > 0)
            def _():
                acc_ref[...] += part

            @pl.when(kk == nk - 1)
            def _():
                o_ref[...] = acc_ref[...].astype(out_dtype)
        if exchange is not None:
            pl.when(functools.reduce(jnp.logical_and, [p == g - 1 for p, g in zip(ids, grid)]))(x_finish)

    in_specs = [a_spec, b_spec]
    out_specs = [pl.BlockSpec((tm, tn), lambda i, j, kk: (i, j))]
    out_shape = [jax.ShapeDtypeStruct((m, n), out_dtype)]
    scratch = [pltpu.VMEM((tm, tn) if nk > 1 else (8, 128), F32)]
    args = [a, b]
    if exchange is not None:
        in_specs.append(HB)
        out_specs.append(HB)
        out_shape.append(jax.ShapeDtypeStruct((3,) + exchange.shape[1:], exchange.dtype))
        scratch += EXCHANGE_SEMS
        args.append(exchange)
    out = pl.pallas_call(
        body, name=name, grid=grid, in_specs=in_specs, out_specs=out_specs, out_shape=out_shape,
        scratch_shapes=scratch,
        compiler_params=_cp(("parallel", "parallel", "arbitrary") if exchange is None else ("arbitrary",) * 3),
    )(*args)
    return out[0] if exchange is None else out


def _row_spec(tm, width=D_MODEL, col=0):
    return pl.BlockSpec((tm, width), lambda i: (i, col))


def _fix_spec(shape):
    return pl.BlockSpec(shape, lambda *_: (0,) * len(shape))


def _norm_mod(x, nw, mod, row_sh, name):
    t = x.shape[0]
    tm = min(t, 512)

    def body(x_ref, nw_ref, mod_ref, h_ref):
        xv = x_ref[...]
        r = lax.rsqrt(jnp.mean(xv * xv, axis=-1, keepdims=True) + EPS)
        sh = mod_ref[row_sh:row_sh + 1, :]
        sc = mod_ref[row_sh + 1:row_sh + 2, :]
        h_ref[...] = (xv * r * nw_ref[...] * (1.0 + sc) + sh).astype(BF16)

    return pl.pallas_call(
        body, name=name, grid=(t // tm,),
        in_specs=[_row_spec(tm), _fix_spec((1, D_MODEL)), _fix_spec((8, D_MODEL))],
        out_specs=_row_spec(tm), out_shape=jax.ShapeDtypeStruct((t, D_MODEL), BF16),
        compiler_params=_cp(("parallel",)),
    )(x, nw, mod)


def _resid_norm(x, mix, nw, mod, name):
    t = x.shape[0]
    tm = min(t, 512)

    def body(x_ref, mix_ref, nw_ref, mod_ref, x1_ref, h_ref):
        x1 = x_ref[...] + mod_ref[2:3, :] * mix_ref[...].astype(F32)
        x1_ref[...] = x1
        r = lax.rsqrt(jnp.mean(x1 * x1, axis=-1, keepdims=True) + EPS)
        h_ref[...] = (x1 * r * nw_ref[...] * (1.0 + mod_ref[4:5, :]) + mod_ref[3:4, :]).astype(BF16)

    return pl.pallas_call(
        body, name=name, grid=(t // tm,),
        in_specs=[_row_spec(tm), _row_spec(tm), _fix_spec((1, D_MODEL)), _fix_spec((8, D_MODEL))],
        out_specs=[_row_spec(tm), _row_spec(tm)],
        out_shape=[jax.ShapeDtypeStruct((t, D_MODEL), F32), jax.ShapeDtypeStruct((t, D_MODEL), BF16)],
        compiler_params=_cp(("parallel",)),
    )(x, mix, nw, mod)


def _act_fwd(gu, name):
    t = gu.shape[0]
    tm, tn = min(t, 512), D_FF // 2
    nb = D_FF // tn

    def body(g_ref, u_ref, a_ref):
        g = g_ref[...].astype(F32)
        a_ref[...] = (g * _sig(g) * u_ref[...].astype(F32)).astype(BF16)

    return pl.pallas_call(
        body, name=name, grid=(t // tm, nb),
        in_specs=[pl.BlockSpec((tm, tn), lambda i, j: (i, j)), pl.BlockSpec((tm, tn), lambda i, j: (i, j + nb))],
        out_specs=pl.BlockSpec((tm, tn), lambda i, j: (i, j)),
        out_shape=jax.ShapeDtypeStruct((t, D_FF), BF16),
        compiler_params=_cp(("parallel", "parallel")),
    )(gu, gu)


def _act_bwd(dact, gu, name):
    t = gu.shape[0]
    tm = min(t, 256)

    def body(d_ref, g_ref, u_ref, o_ref):
        g, d = g_ref[...].astype(F32), d_ref[...].astype(F32)
        s = _sig(g)
        o_ref[:, 0:D_FF] = (d * u_ref[...].astype(F32) * s * (1.0 + g * (1.0 - s))).astype(BF16)
        o_ref[:, D_FF:2 * D_FF] = (d * g * s).astype(BF16)

    return pl.pallas_call(
        body, name=name, grid=(t // tm,),
        in_specs=[pl.BlockSpec((tm, D_FF), lambda i: (i, 0)), pl.BlockSpec((tm, D_FF), lambda i: (i, 0)),
                  pl.BlockSpec((tm, D_FF), lambda i: (i, 1))],
        out_specs=pl.BlockSpec((tm, 2 * D_FF), lambda i: (i, 0)),
        out_shape=jax.ShapeDtypeStruct((t, 2 * D_FF), BF16),
        compiler_params=_cp(("parallel",)),
    )(dact, gu, gu)


def _loss_head(x1, ffn, tgt, mod, name):
    t = x1.shape[0]
    tm = min(t, 512)

    def body(x1_ref, f_ref, t_ref, mod_ref, dffn_ref, dout_ref, dg2_ref, loss_ref):
        i = pl.program_id(0)
        g2 = mod_ref[5:6, :]
        f = f_ref[...].astype(F32)
        err = x1_ref[...] + g2 * f - t_ref[...]
        dout = err * (1.0 / D_MODEL)
        dout_ref[...] = dout
        dffn_ref[...] = (dout * g2).astype(BF16)
        part = jnp.zeros((8, 128), F32) + 0.5 * jnp.sum(jnp.mean(err * err, axis=-1, keepdims=True))

        @pl.when(i == 0)
        def _():
            dg2_ref[...] = _colsum(dout * f)
            loss_ref[...] = part

        @pl.when(i > 0)
        def _():
            dg2_ref[...] += _colsum(dout * f)
            loss_ref[...] += part

    return pl.pallas_call(
        body, name=name, grid=(t // tm,),
        in_specs=[_row_spec(tm), _row_spec(tm), _row_spec(tm), _fix_spec((8, D_MODEL))],
        out_specs=[_row_spec(tm), _row_spec(tm), _fix_spec((1, D_MODEL)), _fix_spec((8, 128))],
        out_shape=[jax.ShapeDtypeStruct((t, D_MODEL), BF16), jax.ShapeDtypeStruct((t, D_MODEL), F32),
                   jax.ShapeDtypeStruct((1, D_MODEL), F32), jax.ShapeDtypeStruct((8, 128), F32)],
        compiler_params=_cp(("arbitrary",)),
    )(x1, ffn, tgt, mod)


def _norm_bwd(dh, xin, dres, aux, nw, mod, row_sh, gate_row, name):
    t = xin.shape[0]
    tm = min(t, 512)
    with_gate = gate_row is not None

    def body(*refs):
        if with_gate:
            dh_ref, x_ref, dr_ref, aux_ref, nw_ref, mod_ref, dx_ref, dg_ref, acc_ref = refs
        else:
            dh_ref, x_ref, dr_ref, nw_ref, mod_ref, dx_ref, acc_ref = refs
        i = pl.program_id(0)
        xv, dhv = x_ref[...], dh_ref[...].astype(F32)
        r = lax.rsqrt(jnp.mean(xv * xv, axis=-1, keepdims=True) + EPS)
        xn = xv * r
        nwv = nw_ref[...]
        sc1 = 1.0 + mod_ref[row_sh + 1:row_sh + 2, :]
        dxn = dhv * (nwv * sc1)
        dx = dr_ref[...] + r * (dxn - xn * jnp.mean(dxn * xn, axis=-1, keepdims=True))
        dx_ref[...] = dx
        dhx = dhv * xn
        rows = [_colsum(dhv), _colsum(dhx * nwv), _colsum(dhx * sc1)]
        if with_gate:
            dg_ref[...] = (dx * mod_ref[gate_row:gate_row + 1, :]).astype(BF16)
            rows.append(_colsum(dx * aux_ref[...].astype(F32)))

        @pl.when(i == 0)
        def _():
            acc_ref[...] = jnp.zeros_like(acc_ref)

        for k, v in enumerate(rows):
            acc_ref[k:k + 1, :] += v

    ins = [dh, xin, dres] + ([aux] if with_gate else []) + [nw, mod]
    in_specs = [_row_spec(tm)] * (4 if with_gate else 3) + [_fix_spec((1, D_MODEL)), _fix_spec((8, D_MODEL))]
    out_specs = [_row_spec(tm)] + ([_row_spec(tm)] if with_gate else []) + [_fix_spec((8, D_MODEL))]
    out_shape = ([jax.ShapeDtypeStruct((t, D_MODEL), F32)]
                 + ([jax.ShapeDtypeStruct((t, D_MODEL), BF16)] if with_gate else [])
                 + [jax.ShapeDtypeStruct((8, D_MODEL), F32)])
    return pl.pallas_call(
        body, name=name, grid=(t // tm,), in_specs=in_specs, out_specs=out_specs, out_shape=out_shape,
        compiler_params=_cp(("arbitrary",)),
    )(*ins)


XBC_COL0 = 4096 // 128
DT_COL = 5632 // 128


def _conv_pre(xv, w_ref, b_ref):
    t = xv.shape[0]
    row = lax.broadcasted_iota(jnp.int32, xv.shape, 0)
    pre = xv * w_ref[3:4, :] + b_ref[...]
    shifted = []
    for k in range(3):
        s = 3 - k
        xs = jnp.where(row >= s, pltpu.roll(xv, s, 0), 0.0)
        shifted.append(xs)
        pre = pre + xs * w_ref[k:k + 1, :]
    return pre, shifted, row, t


def _conv_fwd(proj, conv_w, conv_b, name):
    t = proj.shape[0]

    def body(x_ref, w_ref, b_ref, u_ref):
        pre, _, _, _ = _conv_pre(x_ref[...], w_ref, b_ref)
        u_ref[...] = pre * _sig(pre)

    return pl.pallas_call(
        body, name=name, grid=(D_CONV // 128,),
        in_specs=[pl.BlockSpec((t, 128), lambda j: (0, XBC_COL0 + j)), pl.BlockSpec((4, 128), lambda j: (0, j)),
                  pl.BlockSpec((1, 128), lambda j: (0, j))],
        out_specs=pl.BlockSpec((t, 128), lambda j: (0, j)),
        out_shape=jax.ShapeDtypeStruct((t, D_CONV), F32),
        compiler_params=_cp(("parallel",)),
    )(proj, conv_w, conv_b)


def _conv_bwd(proj, du, conv_w, conv_b, name):
    t = proj.shape[0]

    def body(x_ref, du_ref, w_ref, b_ref, dx_ref, dw_ref, db_ref):
        pre, shifted, row, _ = _conv_pre(x_ref[...], w_ref, b_ref)
        s = _sig(pre)
        dpre = du_ref[...] * s * (1.0 + pre * (1.0 - s))
        db_ref[...] = _colsum(dpre)
        dx = dpre * w_ref[3:4, :]
        dw_ref[3:4, :] = _colsum(dpre * x_ref[...])
        for k in range(3):
            sft = 3 - k
            dw_ref[k:k + 1, :] = _colsum(dpre * shifted[k])
            back = jnp.where(row < t - sft, pltpu.roll(dpre, t - sft, 0), 0.0)
            dx = dx + back * w_ref[k:k + 1, :]
        dx_ref[...] = dx.astype(BF16)

    return pl.pallas_call(
        body, name=name, grid=(D_CONV // 128,),
        in_specs=[pl.BlockSpec((t, 128), lambda j: (0, XBC_COL0 + j)), pl.BlockSpec((t, 128), lambda j: (0, j)),
                  pl.BlockSpec((4, 128), lambda j: (0, j)), pl.BlockSpec((1, 128), lambda j: (0, j))],
        out_specs=[pl.BlockSpec((t, 128), lambda j: (0, j)), pl.BlockSpec((4, 128), lambda j: (0, j)),
                   pl.BlockSpec((1, 128), lambda j: (0, j))],
        out_shape=[jax.ShapeDtypeStruct((t, D_CONV), BF16), jax.ShapeDtypeStruct((4, D_CONV), F32),
                   jax.ShapeDtypeStruct((1, D_CONV), F32)],
        compiler_params=_cp(("parallel",)),
    )(proj, du, conv_w, conv_b)


def _ssd_common(dtraw_ref, dtb_ref, alog_ref, tril, expand, l_s, lt_s):
    lane = lax.broadcasted_iota(jnp.int32, (1, 128), 1)
    dt = _softplus(dtraw_ref[...] + dtb_ref[...])
    a = jnp.where(lane < N_HEADS, -jnp.exp(alog_ref[...]), 0.0)
    lcs = _dotx_l(tril, dt * a)
    l_s[...] = lcs
    lt_s[...] = lcs.T
    llast = l_s[CHUNK - 1:CHUNK, :]
    ea = _dotx_r(jnp.exp(lcs), expand, n=2)
    ds = _dotx_r(jnp.exp(llast - lcs), expand, n=2)
    dtx = _dotx_r(dt, expand, n=2)
    return dt, a, lcs, llast, ea, ds, dtx


def _head_col(lcs, h):
    lane = lax.broadcasted_iota(jnp.int32, lcs.shape, 1)
    return jnp.sum(jnp.where(lane == h, lcs, 0.0), axis=1, keepdims=True)


def _decay(lcs, lt_s, h, causal):
    seg = _head_col(lcs, h) - lt_s[h:h + 1, :]
    return jnp.exp(jnp.where(causal, seg, -1e30))


def _ssd_fwd(u, proj, dtb, alog, dsk, nw, cst, name):
    t = u.shape[0]
    nc = t // CHUNK

    def body(xs_ref, b_ref, c_ref, dtraw_ref, z_ref, dtb_ref, alog_ref, dsk_ref, nw_ref, tril_ref, exp_ref,
             y_ref, yn_ref, prev_ref, carry, l_s, lt_s, yd_s):
        i = pl.program_id(0)

        @pl.when(i == 0)
        def _():
            carry[...] = jnp.zeros_like(carry)

        expand = exp_ref[...]
        dt, a, lcs, llast, ea, ds, dtx = _ssd_common(dtraw_ref, dtb_ref, alog_ref, tril_ref[...], expand, l_s, lt_s)
        xs = xs_ref[...]
        xg = xs * dtx
        xgb = xg.astype(BF16)
        xgd = (xg * ds).astype(BF16)
        prev = carry[...]
        prev_ref[0] = prev
        prevb = prev.astype(BF16)
        ri = lax.broadcasted_iota(jnp.int32, (CHUNK, CHUNK), 0)
        ci = lax.broadcasted_iota(jnp.int32, (CHUNK, CHUNK), 1)
        causal = ri >= ci
        lane = lax.broadcasted_iota(jnp.int32, (1, 128), 1)
        new_states, yoff = [], []
        for g in range(2):
            bg = b_ref[:, g * 128:(g + 1) * 128].astype(BF16)
            cg = c_ref[:, g * 128:(g + 1) * 128].astype(BF16)
            sc = _dot(cg, bg, "nt")
            gs = slice(g * 512, (g + 1) * 512)
            new_states.append(_dot(bg, xgd[:, gs], "tn"))
            yoff.append(_dot(cg, prevb[:, gs]))
            for pr in range(4):
                col = g * 512 + pr * 128
                xp = xgb[:, col:col + 128]
                acc = jnp.zeros((CHUNK, 128), F32)
                for half in range(2):
                    h = g * 8 + pr * 2 + half
                    m = (sc * _decay(lcs, lt_s, h, causal)).astype(BF16)
                    keep = (lane < HEAD_DIM) if half == 0 else (lane >= HEAD_DIM)
                    acc = acc + _dot(m, jnp.where(keep, xp, jnp.zeros_like(xp)))
                yd_s[:, col:col + 128] = acc
        y = yd_s[...] + jnp.concatenate(yoff, axis=1) * ea + xs * dsk_ref[...]
        y_ref[...] = y
        carry[...] = prev * jnp.max(_dotx_r(jnp.exp(llast) + jnp.zeros((8, 128), F32), expand, n=2), axis=0, keepdims=True) \
            + jnp.concatenate(new_states, axis=1)
        z = z_ref[...]
        yz = y * (z * _sig(z))
        nwv = nw_ref[...]
        for g in range(2):
            gs = slice(g * 512, (g + 1) * 512)
            v = yz[:, gs]
            r = lax.rsqrt(jnp.mean(v * v, axis=-1, keepdims=True) + EPS)
            yn_ref[:, gs] = (v * r * nwv[:, gs]).astype(BF16)

    row = lambda w, col: pl.BlockSpec((CHUNK, w), lambda i: (i, col))
    return pl.pallas_call(
        body, name=name, grid=(nc,),
        in_specs=[row(1024, 0), row(256, 4), row(256, 5), row(128, DT_COL), row(1024, 0),
                  _fix_spec((1, 128)), _fix_spec((1, 128)), _fix_spec((1, D_MODEL)), _fix_spec((1, D_MODEL)),
                  _fix_spec((CHUNK, CHUNK)), _fix_spec((128, D_MODEL))],
        out_specs=[row(1024, 0), row(1024, 0), pl.BlockSpec((1, 128, D_MODEL), lambda i: (i, 0, 0))],
        out_shape=[jax.ShapeDtypeStruct((t, D_MODEL), F32), jax.ShapeDtypeStruct((t, 2 * D_MODEL), BF16),
                   jax.ShapeDtypeStruct((nc, 128, D_MODEL), F32)],
        scratch_shapes=[pltpu.VMEM((128, D_MODEL), F32), pltpu.VMEM((128, 128), F32), pltpu.VMEM((128, 128), F32),
                        pltpu.VMEM((CHUNK, D_MODEL), F32)],
        compiler_params=_cp(("arbitrary",)),
    )(u, u, u, proj, proj, dtb, alog, dsk, nw, cst["tril"], cst["expand"])


def _ssd_bwd(u, proj, y, prev, dycat, dtb, alog, dsk, nw, gpack, cst, name):
    t = u.shape[0]
    nc = t // CHUNK

    def body(xs_ref, b_ref, c_ref, dtraw_ref, z_ref, y_ref, prev_ref, dyn_ref, dtb_ref, alog_ref, dsk_ref, nw_ref,
             tril_ref, triu_ref, exp_ref, hs_ref, g_ref,
             du_ref, ddt_ref, dz_ref, acc_ref, acc16_ref, got_ref, dcarry, l_s, lt_s, dxg_s, wss, wrs):
        i = pl.program_id(0)
        w_start, w_finish = _swap_stages(g_ref, got_ref, wss, wrs)
        pl.when(i == 0)(w_start)

        @pl.when(i == 0)
        def _():
            dcarry[...] = jnp.zeros_like(dcarry)
            acc_ref[...] = jnp.zeros_like(acc_ref)
            acc16_ref[...] = jnp.zeros_like(acc16_ref)

        expand, hsum = exp_ref[...], hs_ref[...]
        dt, a, lcs, llast, ea, ds, dtx = _ssd_common(dtraw_ref, dtb_ref, alog_ref, tril_ref[...], expand, l_s, lt_s)
        xs = xs_ref[...]
        xg = xs * dtx
        xgb = xg.astype(BF16)
        xgdf = xg * ds
        xgd = xgdf.astype(BF16)
        dsk_v, nwv = dsk_ref[...], nw_ref[...]
        z, y = z_ref[...], y_ref[...]
        sz = _sig(z)
        silz = z * sz
        yz = y * silz
        dyn = dyn_ref[...].astype(F32)
        dyz_parts, dnw_parts = [], []
        for g in range(2):
            gs = slice(g * 512, (g + 1) * 512)
            v = yz[:, gs]
            r = lax.rsqrt(jnp.mean(v * v, axis=-1, keepdims=True) + EPS)
            yhat = v * r
            dnw_parts.append(_colsum(dyn[:, gs] * yhat))
            dw = dyn[:, gs] * nwv[:, gs]
            dyz_parts.append(r * (dw - yhat * jnp.mean(dw * yhat, axis=-1, keepdims=True)))
        dyz = jnp.concatenate(dyz_parts, axis=1)
        dy = dyz * silz
        dz_ref[...] = (dyz * y * (sz * (1.0 + z * (1.0 - sz)))).astype(BF16)
        acc_ref[0:1, :] += jnp.concatenate(dnw_parts, axis=1)
        acc_ref[1:2, :] += _colsum(dy * xs)
        dyb = dy.astype(BF16)
        dq = (dy * ea).astype(BF16)
        dcar = dcarry[...]
        dcarb = dcar.astype(BF16)
        prev = prev_ref[0]
        prevb = prev.astype(BF16)
        ri = lax.broadcasted_iota(jnp.int32, (CHUNK, CHUNK), 0)
        ci = lax.broadcasted_iota(jnp.int32, (CHUNK, CHUNK), 1)
        causal = ri >= ci
        lane = lax.broadcasted_iota(jnp.int32, (1, 128), 1)
        dprev, dxgd, yoff = [], [], []
        dl_l = jnp.zeros((CHUNK, CHUNK), F32)
        dl_s = jnp.zeros((CHUNK, CHUNK), F32)
        for g in range(2):
            gs = slice(g * 512, (g + 1) * 512)
            bg = b_ref[:, g * 128:(g + 1) * 128].astype(BF16)
            cg = c_ref[:, g * 128:(g + 1) * 128].astype(BF16)
            sc = _dot(cg, bg, "nt")
            yoff.append(_dot(cg, prevb[:, gs]))
            dcg = _dot(dq[:, gs], prevb[:, gs], "nt")
            dprev.append(_dot(cg, dq[:, gs], "tn"))
            dbg = _dot(xgd[:, gs], dcarb[:, gs], "nt")
            dxgd.append(_dot(bg, dcarb[:, gs]))
            dsc = jnp.zeros((CHUNK, CHUNK), F32)
            for pr in range(4):
                col = g * 512 + pr * 128
                xp = xgb[:, col:col + 128]
                dyp = dyb[:, col:col + 128]
                acc = jnp.zeros((CHUNK, 128), F32)
                for half in range(2):
                    h = g * 8 + pr * 2 + half
                    dec = _decay(lcs, lt_s, h, causal)
                    mf = sc * dec
                    keep = (lane < HEAD_DIM) if half == 0 else (lane >= HEAD_DIM)
                    dyh = jnp.where(keep, dyp, jnp.zeros_like(dyp))
                    dm = _dot(dyh, xp, "nt")
                    acc = acc + _dot(mf.astype(BF16), dyh, "tn")
                    dsc = dsc + dm * dec
                    gm = dm * mf
                    dl_l = dl_l + jnp.where(ci == h, jnp.sum(gm, axis=1, keepdims=True), 0.0)
                    dl_s = dl_s + jnp.where(ri == h, jnp.sum(gm, axis=0, keepdims=True), 0.0)
                dxg_s[:, col:col + 128] = acc
            dscb = dsc.astype(BF16)
            dcg = dcg + _dot(dscb, bg)
            dbg = dbg + _dot(dscb, cg, "tn")
            du_ref[:, 1024 + g * 128:1024 + (g + 1) * 128] = dbg
            du_ref[:, 1280 + g * 128:1280 + (g + 1) * 128] = dcg
        dxgd = jnp.concatenate(dxgd, axis=1)
        dxg = dxg_s[...] + dxgd * ds
        du_ref[:, 0:1024] = dy * dsk_v + dxg * dtx
        hs1 = _dotx_r(dxg * xs, hsum, n=2)
        yoff = jnp.concatenate(yoff, axis=1) * ea
        dl = dl_l - dl_s.T + _dotx_r(dy * yoff - xgdf * dxgd, hsum, n=2)
        rows8 = lax.broadcasted_iota(jnp.int32, (8, D_MODEL), 0)
        two = jnp.where(rows8 == 0, _colsum(dxgd * xgdf), jnp.where(rows8 == 1, _colsum(dcar * prev), 0.0))
        two = _dotx_r(two, hsum)
        r8 = lax.broadcasted_iota(jnp.int32, (8, 128), 0)
        dllast = _colsum(jnp.where(r8 == 0, two, 0.0)) + _colsum(jnp.where(r8 == 1, two, 0.0)) * jnp.exp(llast)
        rowi = lax.broadcasted_iota(jnp.int32, (CHUNK, 128), 0)
        dl = dl + jnp.where(rowi == CHUNK - 1, dllast, 0.0)
        dadt = _dotx_l(triu_ref[...], dl)
        ddt = dadt * a + hs1
        draw = ddt * _sig(dtraw_ref[...] + dtb_ref[...])
        ddt_ref[...] = draw.astype(BF16)
        acc16_ref[0:1, :] += _colsum(draw)
        acc16_ref[1:2, :] += _colsum(dadt * dt) * a
        dcarry[...] = dcar * jnp.max(_dotx_r(jnp.exp(llast) + jnp.zeros((8, 128), F32), expand, n=2), axis=0, keepdims=True) \
            + jnp.concatenate(dprev, axis=1)

        @pl.when(i == nc - 1)
        def _():
            hd = _dotx_r(acc_ref[...], hsum)
            acc16_ref[2:3, :] = _colsum(jnp.where(lax.broadcasted_iota(jnp.int32, (8, 128), 0) == 1, hd, 0.0))

        pl.when(i == nc - 1)(w_finish)

    rev = lambda w, col: pl.BlockSpec((CHUNK, w), lambda i: (nc - 1 - i, col))
    return pl.pallas_call(
        body, name=name, grid=(nc,),
        in_specs=[rev(1024, 0), rev(256, 4), rev(256, 5), rev(128, DT_COL), rev(1024, 0), rev(1024, 0),
                  pl.BlockSpec((1, 128, D_MODEL), lambda i: (nc - 1 - i, 0, 0)), rev(1024, 0),
                  _fix_spec((1, 128)), _fix_spec((1, 128)), _fix_spec((1, D_MODEL)), _fix_spec((1, D_MODEL)),
                  _fix_spec((CHUNK, CHUNK)), _fix_spec((CHUNK, CHUNK)), _fix_spec((128, D_MODEL)),
                  _fix_spec((D_MODEL, 128)), HB],
        out_specs=[rev(D_CONV, 0), rev(128, 0), rev(1024, 0), _fix_spec((8, D_MODEL)), _fix_spec((8, 128)), HB],
        out_shape=[jax.ShapeDtypeStruct((t, D_CONV), F32), jax.ShapeDtypeStruct((t, 128), BF16),
                   jax.ShapeDtypeStruct((t, D_MODEL), BF16), jax.ShapeDtypeStruct((8, D_MODEL), F32),
                   jax.ShapeDtypeStruct((8, 128), F32), _swap_shape(gpack)],
        scratch_shapes=[pltpu.VMEM((128, D_MODEL), F32), pltpu.VMEM((128, 128), F32), pltpu.VMEM((128, 128), F32),
                        pltpu.VMEM((CHUNK, D_MODEL), F32)] + SWAP_SEMS,
        compiler_params=_cp(("arbitrary",)),
    )(u, u, u, proj, proj, y, prev, dycat, dtb, alog, dsk, nw,
      cst["tril"], cst["triu"], cst["expand"], cst["hsum"], gpack)


def _head_rms(v, hsum, expand):
    ms = _dotx_r(v * v, hsum, n=2) * (1.0 / HEAD_DIM)
    return _dotx_r(lax.rsqrt(ms + EPS), expand, n=2)


def _qk_fwd(proj, qw, kw, cst, name):
    t = proj.shape[0]
    tm = min(t, 256)
    scale = HEAD_DIM ** -0.5

    def body(q_ref, k_ref, v_ref, qw_ref, kw_ref, hs_ref, exp_ref, qs_ref, kn_ref, vb_ref):
        hsum, expand = hs_ref[...], exp_ref[...]
        q, k = q_ref[...], k_ref[...]
        qs_ref[...] = (q * _head_rms(q, hsum, expand) * qw_ref[...] * scale).astype(BF16)
        kn_ref[...] = (k * _head_rms(k, hsum, expand) * kw_ref[...]).astype(BF16)
        vb_ref[...] = v_ref[...].astype(BF16)

    return pl.pallas_call(
        body, name=name, grid=(t // tm,),
        in_specs=[_row_spec(tm, col=1), _row_spec(tm, col=2), _row_spec(tm, col=3),
                  _fix_spec((1, D_MODEL)), _fix_spec((1, D_MODEL)), _fix_spec((D_MODEL, 128)),
                  _fix_spec((128, D_MODEL))],
        out_specs=[_row_spec(tm)] * 3, out_shape=[jax.ShapeDtypeStruct((t, D_MODEL), BF16)] * 3,
        compiler_params=_cp(("parallel",)),
    )(proj, proj, proj, qw, kw, cst["hsum"], cst["expand"])


def _qk_bwd(proj, dqs, dkn, dv, qw, kw, cst, name):
    t = proj.shape[0]
    tm = min(t, 256)
    scale = HEAD_DIM ** -0.5

    def body(q_ref, k_ref, dq_ref, dk_ref, dv_ref, qw_ref, kw_ref, hs_ref, exp_ref, fold_ref,
             oq_ref, ok_ref, ov_ref, dw_ref):
        i = pl.program_id(0)
        hsum, expand = hs_ref[...], exp_ref[...]
        rows8 = lax.broadcasted_iota(jnp.int32, (8, D_MODEL), 0)
        sums = jnp.zeros((8, D_MODEL), F32)
        for n, (x_ref, d_ref, w_ref, o_ref, sc) in enumerate(
                [(q_ref, dq_ref, qw_ref, oq_ref, scale), (k_ref, dk_ref, kw_ref, ok_ref, 1.0)]):
            xv = x_ref[...]
            r = _head_rms(xv, hsum, expand)
            xhat = xv * r
            dn = d_ref[...] * sc
            sums = sums + jnp.where(rows8 == n, _colsum(dn * xhat), 0.0)
            dw = dn * w_ref[...]
            mean = _dotx_r(_dotx_r(dw * xhat, hsum, n=1), expand, n=2) * (1.0 / HEAD_DIM)
            o_ref[...] = (r * (dw - xhat * mean)).astype(BF16)
        ov_ref[...] = dv_ref[...].astype(BF16)
        folded = _dotx_r(sums, fold_ref[...])

        @pl.when(i == 0)
        def _():
            dw_ref[...] = folded

        @pl.when(i > 0)
        def _():
            dw_ref[...] += folded

    return pl.pallas_call(
        body, name=name, grid=(t // tm,),
        in_specs=[_row_spec(tm, col=1), _row_spec(tm, col=2), _row_spec(tm), _row_spec(tm), _row_spec(tm),
                  _fix_spec((1, D_MODEL)), _fix_spec((1, D_MODEL)), _fix_spec((D_MODEL, 128)),
                  _fix_spec((128, D_MODEL)), _fix_spec((D_MODEL, 128))],
        out_specs=[_row_spec(tm)] * 3 + [_fix_spec((8, 128))],
        out_shape=[jax.ShapeDtypeStruct((t, D_MODEL), BF16)] * 3 + [jax.ShapeDtypeStruct((8, 128), F32)],
        compiler_params=_cp(("arbitrary",)),
    )(proj, proj, dqs, dkn, dv, qw, kw, cst["hsum"], cst["expand"], cst["fold"])


def _sb_masks(i, kb, tq, tk):
    tpos = i * tq + lax.broadcasted_iota(jnp.int32, (tq, 1), 0)
    spos = kb * tk + lax.broadcasted_iota(jnp.int32, (1, tk), 1)
    return spos < tpos


def _grid_marks(n0, n1):
    j, i = pl.program_id(0), pl.program_id(1)
    return (jnp.logical_and(j == 0, i == 0), jnp.logical_and(j == n0 // 2, i == 0),
            jnp.logical_and(j == n0 - 1, i == n1 - 1))


def _sb_fwd(qs, kn, vb, pack, ycat, cst, name):
    t = qs.shape[0]
    tq = tk = min(t, SB_TILE)
    nq = t // tq
    pairs = FWD_PAIRS
    ngrp = D_MODEL // (128 * pairs)
    nh = 2 * pairs
    lanes = lambda p: slice(p * 128, (p + 1) * 128)

    def body(q_ref, k_ref, v_ref, u_ref, p_ref, yc_ref, rt_ref, ob_ref, cnt_ref, gat_ref, acc, rs, gss, grs):
        del yc_ref
        at_first, at_mid, at_last = _grid_marks(ngrp, nq)
        g_start, g_relay, g_finish = _gather_stages(p_ref, gat_ref, gss, grs)
        pl.when(at_first)(g_start)
        pl.when(at_mid)(g_relay)
        i = pl.program_id(1)
        lane = lax.broadcasted_iota(jnp.int32, (1, 128), 1)
        qh = []
        for p in range(pairs):
            q2 = q_ref[:, lanes(p)]
            zero = jnp.zeros_like(q2)
            qh += [jnp.where(lane < HEAD_DIM, q2, zero), jnp.where(lane >= HEAD_DIM, q2, zero)]
        acc[...] = jnp.zeros_like(acc)
        rs[...] = jnp.zeros_like(rs)
        ustrict = u_ref[...]

        def tile(kb, masked):
            off = pl.multiple_of(kb * tk, tk)
            k2 = [k_ref[pl.ds(off, tk), lanes(p)] for p in range(pairs)]
            v2 = [v_ref[pl.ds(off, tk), lanes(p)] for p in range(pairs)]
            strict = _sb_masks(i, kb, tq, tk) if masked else None
            s = [_dot(qh[h], k2[h // 2], "nt") for h in range(nh)]
            a, r, lb = [None] * nh, [None] * nh, [None] * nh
            for h in range(nh):
                sp = _softplus(s[h])
                a[h] = s[h] - sp
                r[h] = jnp.where(strict, -sp, 0.0) if masked else -sp
                lb[h] = _dot2(r[h], ustrict)
            for h in range(nh):
                lw = a[h] + lb[h] + rs[h]
                w = jnp.exp(jnp.where(strict, lw, -1e30) if masked else lw)
                rs[h] = rs[h] + jnp.sum(r[h], axis=1, keepdims=True)
                acc[h] = acc[h] + _dot(w.astype(BF16), v2[h // 2])

        tile(i, True)

        def live():
            return jnp.max(functools.reduce(jnp.maximum, [rs[h] for h in range(nh)]))

        def more(c):
            return jnp.logical_and(c[0] < i, c[1] > SB_DEAD)

        def step(c):
            tile(i - 1 - c[0], False)
            return c[0] + 1, live()

        n_off, _ = lax.while_loop(more, step, (jnp.int32(0), live()))
        cnt_ref[pl.program_id(0), i] = n_off.astype(F32)
        for p in range(pairs):
            rt_ref[:, lanes(p)] = jnp.where(lane < HEAD_DIM, rs[2 * p], rs[2 * p + 1])
            ob_ref[:, lanes(p)] = jnp.where(lane < HEAD_DIM, acc[2 * p], acc[2 * p + 1]).astype(BF16)
        pl.when(at_last)(g_finish)

    blk = lambda rows, imap: pl.BlockSpec((rows, (128 * pairs)), imap)
    return pl.pallas_call(
        body, name=name, grid=(ngrp, nq),
        in_specs=[blk(tq, lambda j, i: (i, j)), blk(t, lambda j, i: (0, j)), blk(t, lambda j, i: (0, j)),
                  _fix_spec((2 * tk, tk)), HB, pl.BlockSpec(memory_space=pl.ANY)],
        out_specs=[blk(tq, lambda j, i: (i, j)), blk(tq, lambda j, i: (i, ngrp + j)),
                   pl.BlockSpec(memory_space=pltpu.SMEM), HB],
        out_shape=[jax.ShapeDtypeStruct((t, D_MODEL), F32), jax.ShapeDtypeStruct(ycat.shape, ycat.dtype),
                   jax.ShapeDtypeStruct((ngrp, nq), F32),
                   jax.ShapeDtypeStruct((N_SHARDS,) + pack.shape, pack.dtype)],
        scratch_shapes=[pltpu.VMEM((nh, tq, 128), F32), pltpu.VMEM((nh, tq, 1), F32)] + GATHER_SEMS,
        input_output_aliases={5: 1},
        compiler_params=_cp(("arbitrary", "arbitrary")),
    )(qs, kn, vb, _doubled(cst["ustrict"], tk), pack, ycat)


def _sb_bwd(qs, kn, vb, rtot, cnt, dycat, csum_b, cst, name):
    t = qs.shape[0]
    tq = tk = min(t, SB_TILE)
    nq = t // tq
    pairs = BWD_PAIRS
    ngrp = D_MODEL // (128 * pairs)
    nh = 2 * pairs
    lanes = lambda p: slice(p * 128, (p + 1) * 128)

    def body(q_ref, k_ref, v_ref, rt_ref, do_ref, us_ref, ui_ref, cnt_ref, xs_ref, dq_ref, dk_ref, dv_ref, xr_ref,
             acc, rs, es, xss, xrs):
        at_first, _, at_last = _grid_marks(ngrp, nq)
        x_start, x_finish = _exchange_stages(xs_ref, xr_ref, xss, xrs)
        pl.when(at_first)(x_start)
        i = pl.program_id(1)
        lane = lax.broadcasted_iota(jnp.int32, (1, 128), 1)
        keep = [lane < HEAD_DIM, lane >= HEAD_DIM]
        qh, doh, rtot_h = [], [], []
        for p in range(pairs):
            q2, rt = q_ref[:, lanes(p)], rt_ref[:, lanes(p)]
            do2b = do_ref[:, lanes(p)].astype(BF16)
            qh += [jnp.where(kp, q2, jnp.zeros_like(q2)) for kp in keep]
            doh += [jnp.where(kp, do2b, jnp.zeros_like(do2b)) for kp in keep]
            rtot_h += [jnp.sum(jnp.where(lane == n * HEAD_DIM, rt, 0.0), axis=1, keepdims=True) for n in range(2)]
        acc[...] = jnp.zeros_like(acc)
        rs[...] = jnp.zeros_like(rs)
        es[...] = jnp.zeros_like(es)

        @pl.when(i == 0)
        def _():
            dk_ref[...] = jnp.zeros_like(dk_ref)
            dv_ref[...] = jnp.zeros_like(dv_ref)

        ule, ult = us_ref[...], ui_ref[...]

        def tile(kb, masked):
            off = pl.multiple_of(kb * tk, tk)
            k2 = [k_ref[pl.ds(off, tk), lanes(p)] for p in range(pairs)]
            v2 = [v_ref[pl.ds(off, tk), lanes(p)] for p in range(pairs)]
            strict = _sb_masks(i, kb, tq, tk) if masked else None
            s = [_dot(qh[h], k2[h // 2], "nt") for h in range(nh)]
            dw = [_dot(doh[h], v2[h // 2], "nt") for h in range(nh)]
            a, sg, r, pin, w, e, cin = ([None] * nh for _ in range(7))
            for h in range(nh):
                sp = _softplus(s[h])
                a[h] = s[h] - sp
                sg[h] = jnp.exp(a[h])
                r[h] = jnp.where(strict, -sp, 0.0) if masked else -sp
                pin[h] = _dot2(r[h], ule)
            for h in range(nh):
                lw = a[h] + ((rtot_h[h] - rs[h]) - pin[h])
                w[h] = jnp.exp(jnp.where(strict, lw, -1e30) if masked else lw)
                e[h] = w[h] * dw[h]
                cin[h] = _dot2(e[h], ult)
            for p in range(pairs):
                dk_t = jnp.zeros((tk, 128), F32)
                dv_t = jnp.zeros((tk, 128), F32)
                for h in (2 * p, 2 * p + 1):
                    dl = e[h] * (1.0 - sg[h]) - (es[h] + cin[h]) * sg[h]
                    dl = (jnp.where(strict, dl, 0.0) if masked else dl).astype(BF16)
                    rs[h] = rs[h] + jnp.sum(r[h], axis=1, keepdims=True)
                    es[h] = es[h] + jnp.sum(e[h], axis=1, keepdims=True)
                    acc[h] = acc[h] + _dot(dl, k2[p])
                    dk_t = dk_t + _dot(dl, qh[h], "tn")
                    dv_t = dv_t + _dot(w[h].astype(BF16), doh[h], "tn")
                dk_ref[pl.ds(off, tk), lanes(p)] += dk_t
                dv_ref[pl.ds(off, tk), lanes(p)] += dv_t

        def step(kb, carry):
            tile(kb, False)
            return carry

        n_off = cnt_ref[pl.program_id(0) * BWD_PAIRS // FWD_PAIRS, i].astype(jnp.int32)
        lax.fori_loop(i - n_off, i, step, 0)
        tile(i, True)
        for p in range(pairs):
            dq_ref[:, lanes(p)] = jnp.where(lane < HEAD_DIM, acc[2 * p], acc[2 * p + 1])
        pl.when(at_last)(x_finish)

    blk = lambda rows, imap: pl.BlockSpec((rows, (128 * pairs)), imap)
    return pl.pallas_call(
        body, name=name, grid=(ngrp, nq),
        in_specs=[blk(tq, lambda j, i: (i, j)), blk(t, lambda j, i: (0, j)), blk(t, lambda j, i: (0, j)),
                  blk(tq, lambda j, i: (i, j)), blk(tq, lambda j, i: (i, ngrp + j)),
                  _fix_spec((2 * tk, tk)), _fix_spec((2 * tk, tk)), pl.BlockSpec(memory_space=pltpu.SMEM), HB],
        out_specs=[blk(tq, lambda j, i: (i, j)), blk(t, lambda j, i: (0, j)), blk(t, lambda j, i: (0, j)), HB],
        out_shape=[jax.ShapeDtypeStruct((t, D_MODEL), F32)] * 3
        + [jax.ShapeDtypeStruct((3,) + csum_b.shape[1:], csum_b.dtype)],
        scratch_shapes=[pltpu.VMEM((nh, tq, 128), F32), pltpu.VMEM((nh, tq, 1), F32), pltpu.VMEM((nh, tq, 1), F32)]
        + EXCHANGE_SEMS,
        compiler_params=_cp(("arbitrary", "arbitrary")),
    )(qs, kn, vb, rtot, dycat, _doubled(cst["ule"], tk), _doubled(cst["ult"], tk), cnt, csum_b)


def _adamw(w, g, m, v, name):
    lead = (1,) * (w.ndim - 2)
    rows, cols = w.shape[-2:]
    fits = [d for d in range(8, rows, 8) if rows % d == 0 and d * cols * 4 <= ADAM_BLOCK_BYTES]
    tr = max(fits) if fits else rows
    c1 = 1.0 - ADAM_B1 ** ADAM_STEP
    c2 = 1.0 - ADAM_B2 ** ADAM_STEP

    def body(w_ref, g_ref, m_ref, v_ref, d_ref, nm_ref, nv_ref):
        gv = g_ref[...]
        nm = ADAM_B1 * m_ref[...] + (1.0 - ADAM_B1) * gv
        nv = ADAM_B2 * v_ref[...] + (1.0 - ADAM_B2) * (gv * gv)
        nm_ref[...] = nm
        nv_ref[...] = nv
        d_ref[...] = -ADAM_LR * ((nm / c1) / (jnp.sqrt(nv / c2) + ADAM_EPS) + ADAM_WD * w_ref[...])

    spec = pl.BlockSpec(lead + (tr, cols), lambda i: (0,) * len(lead) + (i, 0))
    return pl.pallas_call(
        body, name=name, grid=(rows // tr,), in_specs=[spec] * 4, out_specs=[spec] * 3,
        out_shape=[jax.ShapeDtypeStruct(w.shape, F32)] * 3, compiler_params=_cp(("parallel",)),
    )(w, g, m, v)


def _place():
    x, y, c = lax.axis_index("x"), lax.axis_index("y"), lax.axis_index("c")
    chips = [(1 - x, y), (x, 1 - y), (1 - x, 1 - y)]
    return x, y, c, chips


VM = pl.BlockSpec(memory_space=pltpu.VMEM)
HB = pl.BlockSpec(memory_space=pltpu.HBM)


def _gather_all(p_ref, gat_ref, ss, rs):
    x, y, c, _ = _place()
    me = 4 * x + 2 * y + c
    peers = [(x, y, 1 - c), (1 - x, y, c), (x, 1 - y, c), (1 - x, 1 - y, c),
             (1 - x, y, 1 - c), (x, 1 - y, 1 - c), (1 - x, 1 - y, 1 - c)]

    def copy(k, slot, to):
        return pltpu.make_async_remote_copy(src_ref=p_ref, dst_ref=gat_ref.at[slot], send_sem=ss.at[k],
                                            recv_sem=rs.at[k], device_id=to, device_id_type=MESH)

    sends = [copy(k, me, p) for k, p in enumerate(peers)]
    for s in sends:
        s.start()
    gat_ref[me] = p_ref[...]
    for k, p in enumerate(peers):
        copy(k, 4 * p[0] + 2 * p[1] + p[2], p).wait_recv()
    for s in sends:
        s.wait_send()


ALL_SEMS = [pltpu.SemaphoreType.DMA((7,)), pltpu.SemaphoreType.DMA((7,))]


def _small_reduce(pack, name):
    rows = pack.shape[0]

    def body(p_ref, gat_ref, sum_ref, ss, rs):
        _gather_all(p_ref, gat_ref, ss, rs)
        tot = gat_ref[0]
        for b in range(1, 8):
            tot = tot + gat_ref[b]
        sum_ref[...] = tot

    return pl.pallas_call(
        body, name=name, in_specs=[VM], out_specs=[VM, VM],
        out_shape=[jax.ShapeDtypeStruct((8, rows, D_MODEL), F32), jax.ShapeDtypeStruct((rows, D_MODEL), F32)],
        scratch_shapes=ALL_SEMS, compiler_params=_cp(),
    )(pack)


def _prologue(cpack, w_ada, b_shard, wpack_in, name):
    def body(cp_ref, w_ref, b_ref, p_ref, gat_ref, modp_ref, gin_ref, ss1, rs1, ss2, rs2, gss, grs):
        g_start, g_relay, g_finish = _gather_stages(p_ref, gin_ref, gss, grs)
        g_start()
        _gather_all(cp_ref, gat_ref, ss1, rs1)
        x, y, c, chips = _place()
        sh = 2 * x + y
        row = lax.broadcasted_iota(jnp.int32, (8, D_MODEL), 0)
        cv = jnp.zeros((8, D_MODEL), F32)
        for b in range(8):
            cv = jnp.where(row == b, gat_ref[b, 0:8, :], cv)
        cv = cv * _sig(cv)
        modp_ref[sh] = jnp.dot(cv, w_ref[...], precision=lax.Precision.HIGHEST,
                               preferred_element_type=F32) + b_ref[...]

        def copy(k, slot, to):
            return pltpu.make_async_remote_copy(src_ref=modp_ref.at[slot], dst_ref=modp_ref.at[slot],
                                                send_sem=ss2.at[k], recv_sem=rs2.at[k], device_id=to,
                                                device_id_type=MESH)

        sends = [copy(k, sh, (*ch, c)) for k, ch in enumerate(chips)]
        for s in sends:
            s.start()
        for k, ch in enumerate(chips):
            copy(k, 2 * ch[0] + ch[1], (*ch, c)).wait_recv()
        for s in sends:
            s.wait_send()
        g_relay()
        g_finish()

    return pl.pallas_call(
        body, name=name, in_specs=[VM, VM, VM, HB], out_specs=[VM, VM, HB],
        out_shape=[jax.ShapeDtypeStruct((8,) + cpack.shape, F32),
                   jax.ShapeDtypeStruct((N_SHARDS, 8, 6 * D_MODEL // N_SHARDS), F32),
                   jax.ShapeDtypeStruct((N_SHARDS,) + wpack_in.shape, wpack_in.dtype)],
        scratch_shapes=ALL_SEMS + EXCHANGE_SEMS + GATHER_SEMS, compiler_params=_cp(),
    )(cpack, w_ada, b_shard, wpack_in)


def _gather_stages(p_ref, out_ref, ss, rs):
    hf = p_ref.shape[0] // 2
    x, y, c, chips = _place()
    sh = 2 * x + y
    sib = (x, y, 1 - c)
    slots = [2 * ch[0] + ch[1] for ch in chips]

    def half(slot, hc):
        return out_ref.at[slot, pl.ds(hc * hf, hf), :]

    def copy(k, src, slot, hc, to):
        return pltpu.make_async_remote_copy(src_ref=src, dst_ref=half(slot, hc), send_sem=ss.at[k],
                                            recv_sem=rs.at[k], device_id=to, device_id_type=MESH)

    def first():
        return [copy(j, p_ref.at[pl.ds(c * hf, hf), :], sh, c, (*ch, c)) for j, ch in enumerate(chips)]

    def passed():
        return [copy(3 + j, half(slots[j], c), slots[j], c, sib) for j in range(3)]

    def own():
        return pltpu.make_async_remote_copy(src_ref=p_ref, dst_ref=out_ref.at[sh], send_sem=ss.at[6],
                                            recv_sem=rs.at[6], device_id=sib, device_id_type=MESH)

    def start():
        for cp in first() + [own()]:
            cp.start()

    def relay():
        for j, cp in enumerate(passed()):
            copy(j, half(slots[j], c), slots[j], c, (*chips[j], c)).wait_recv()
            cp.start()

    def finish():
        for j in range(3):
            copy(3 + j, half(slots[j], 1 - c), slots[j], 1 - c, sib).wait_recv()
        own().wait()
        for cp in first() + passed():
            cp.wait_send()

    return start, relay, finish


GATHER_SEMS = [pltpu.SemaphoreType.DMA((7,)), pltpu.SemaphoreType.DMA((7,))]


def _swap_stages(g_ref, out_ref, ss, rs):
    hf = g_ref.shape[1] // 2
    x, y, c, _ = _place()

    def copy():
        return pltpu.make_async_remote_copy(
            src_ref=g_ref.at[pl.ds(0, N_SHARDS), pl.ds((1 - c) * hf, hf), :], dst_ref=out_ref,
            send_sem=ss, recv_sem=rs, device_id=(x, y, 1 - c), device_id_type=MESH)

    return (lambda: copy().start()), (lambda: copy().wait())


SWAP_SEMS = [pltpu.SemaphoreType.DMA, pltpu.SemaphoreType.DMA]


def _swap_shape(g):
    return jax.ShapeDtypeStruct((N_SHARDS, g.shape[1] // 2, D_MODEL), g.dtype)


def _sibling_swap(g, name):
    def body(g_ref, out_ref, ss, rs):
        for stage in _swap_stages(g_ref, out_ref, ss, rs):
            stage()

    return pl.pallas_call(
        body, name=name, in_specs=[HB], out_specs=HB, out_shape=_swap_shape(g),
        scratch_shapes=SWAP_SEMS, compiler_params=_cp(),
    )(g)


def _row_tile(rows, width_bytes, cap_bytes):
    fits = [d for d in range(8, rows + 1, 8) if rows % d == 0 and d * width_bytes <= cap_bytes]
    return max(fits)


def _chip_sum(g, got, c_idx, name):
    hf = got.shape[1]
    tr = _row_tile(hf, D_MODEL * 4, 3 << 20)
    nb = hf // tr

    def body(c_ref, a_ref, b_ref, s_ref, sb_ref):
        s = a_ref[...] + b_ref[...]
        s_ref[...] = s
        sb_ref[...] = s.astype(BF16)

    blk = pl.BlockSpec((1, tr, D_MODEL), lambda s, i, c_ref: (s, i, 0))
    return pl.pallas_call(
        body, name=name,
        grid_spec=pltpu.PrefetchScalarGridSpec(
            num_scalar_prefetch=1, grid=(N_SHARDS, nb),
            in_specs=[pl.BlockSpec((1, tr, D_MODEL), lambda s, i, c_ref: (s, c_ref[0] * nb + i, 0)), blk],
            out_specs=[blk, blk]),
        out_shape=[jax.ShapeDtypeStruct((N_SHARDS, hf, D_MODEL), F32),
                   jax.ShapeDtypeStruct((N_SHARDS, hf, D_MODEL), BF16)],
        compiler_params=_cp(("parallel", "parallel")),
    )(c_idx, g, got)


def _exchange_stages(s_ref, out_ref, ss, rs):
    x, y, c, chips = _place()

    def sends():
        return [pltpu.make_async_remote_copy(src_ref=s_ref.at[2 * ch[0] + ch[1]], dst_ref=out_ref.at[k],
                                             send_sem=ss.at[k], recv_sem=rs.at[k], device_id=(*ch, c),
                                             device_id_type=MESH) for k, ch in enumerate(chips)]

    def start():
        for cp in sends():
            cp.start()

    def finish():
        for cp in sends():
            cp.wait()

    return start, finish


EXCHANGE_SEMS = [pltpu.SemaphoreType.DMA((3,)), pltpu.SemaphoreType.DMA((3,))]


def _total_half(s, got, sh_idx, name):
    hf = got.shape[1]
    tr = _row_tile(hf, D_MODEL * 4, 3 << 20)
    nb = hf // tr

    def body(sh_ref, a_ref, r0, r1, r2, o_ref):
        o_ref[...] = ((a_ref[0] + r0[0].astype(F32)) + r1[0].astype(F32)) + r2[0].astype(F32)

    rspec = lambda k: pl.BlockSpec((1, tr, D_MODEL), lambda i, sh_ref: (k, i, 0))
    return pl.pallas_call(
        body, name=name,
        grid_spec=pltpu.PrefetchScalarGridSpec(
            num_scalar_prefetch=1, grid=(nb,),
            in_specs=[pl.BlockSpec((1, tr, D_MODEL), lambda i, sh_ref: (sh_ref[0], i, 0)),
                      rspec(0), rspec(1), rspec(2)],
            out_specs=pl.BlockSpec((tr, D_MODEL), lambda i, sh_ref: (i, 0))),
        out_shape=jax.ShapeDtypeStruct((hf, D_MODEL), F32),
        compiler_params=_cp(("parallel",)),
    )(sh_idx, s, got, got, got)


def _join_halves(tot, name):
    def body(t_ref, out_ref, ss, rs):
        x, y, c, _ = _place()
        cp = pltpu.make_async_remote_copy(src_ref=t_ref, dst_ref=out_ref, send_sem=ss, recv_sem=rs,
                                          device_id=(x, y, 1 - c), device_id_type=MESH)
        cp.start()
        cp.wait()

    return pl.pallas_call(
        body, name=name, in_specs=[HB], out_specs=HB,
        out_shape=jax.ShapeDtypeStruct(tot.shape, F32),
        scratch_shapes=[pltpu.SemaphoreType.DMA, pltpu.SemaphoreType.DMA],
        compiler_params=_cp(),
    )(tot)


def _w_ada_grad(cond, dmod_cols, name):
    def body(c_ref, d_ref, o_ref):
        cv = c_ref[...]
        cv = cv * _sig(cv)
        o_ref[...] = lax.dot_general(cv, d_ref[...], _DN["tn"], precision=lax.Precision.HIGHEST,
                                     preferred_element_type=F32)

    return pl.pallas_call(
        body, name=name, in_specs=[VM, VM], out_specs=VM,
        out_shape=jax.ShapeDtypeStruct((D_MODEL, dmod_cols.shape[1]), F32), compiler_params=_cp(),
    )(cond, dmod_cols)


def _pad_rows(a, rows):
    return jnp.pad(a, ((0, rows - a.shape[0]), (0, 0)))


def _pad_cols(a, cols):
    return jnp.pad(a, ((0, 0), (0, cols - a.shape[1])))


def _unpack_rest(p):
    o = 0
    out = []
    for r in (R_OUT, R_FF, R_FF, R_FF):
        out.append(p[..., o:o + r, :])
        o += r
    return out


def _reduce_tail(csum, got2, shard, ac, tag):
    tot = _total_half(csum, got2, shard.reshape(1).astype(jnp.int32), "rs_total_" + tag)
    other = _join_halves(tot, "rs_join_" + tag)
    return jnp.where(ac == 0, jnp.concatenate([tot, other], axis=0), jnp.concatenate([other, tot], axis=0))


def _reduce_head(gpack, ac, tag):
    got = _sibling_swap(gpack, "rs_sibling_swap_" + tag)
    return _chip_sum(gpack, got, ac.reshape(1).astype(jnp.int32), "rs_chip_sum_" + tag)


def kernel(x, c, w_ada, b_ada, norm1_w, w_in, conv_w, conv_b, dt_bias, a_log, d_skip, ssd_norm_w, q_norm_w, k_norm_w, w_out, norm2_w, w_gate, w_up, w_down, loss_target, m_w_ada, m_b_ada, m_norm1_w, m_w_in, m_conv_w, m_conv_b, m_dt_bias, m_a_log, m_d_skip, m_ssd_norm_w, m_q_norm_w, m_k_norm_w, m_w_out, m_norm2_w, m_w_gate, m_w_up, m_w_down, v_w_ada, v_b_ada, v_norm1_w, v_w_in, v_conv_w, v_conv_b, v_dt_bias, v_a_log, v_d_skip, v_ssd_norm_w, v_q_norm_w, v_k_norm_w, v_w_out, v_norm2_w, v_w_gate, v_w_up, v_w_down):
    cst = _consts()
    ax, ay, ac = lax.axis_index("x"), lax.axis_index("y"), lax.axis_index("c")
    shard = 2 * ax + ay
    me = 4 * ax + 2 * ay + ac
    xs = x[0]
    tgt = loss_target[0]
    w_in_cols = w_in.shape[2]
    conv_cols = conv_w.shape[2]

    tr3 = lambda a: jnp.transpose(a, (0, 2, 1))
    lin = lambda a: tr3(a).reshape(-1, 128)
    unlin = lambda a: tr3(a.reshape(1, w_in_cols, D_MODEL))
    wpack_in = _pad_rows(tr3(w_in.astype(BF16))[0], R_IN)
    wpack_rest = jnp.concatenate([w_out[0], tr3(w_gate)[0], tr3(w_up)[0], w_down[0]], axis=0).astype(BF16)

    cw_flat = _pad_cols(conv_w[0].reshape(1, -1), 2 * D_MODEL).reshape(2, D_MODEL)
    cpack = jnp.concatenate([jnp.broadcast_to(c, (8, D_MODEL)), _pad_rows(cw_flat, 8)], axis=0)
    mod_w = 6 * D_MODEL // N_SHARDS
    b_shard = lax.dynamic_slice(b_ada, (0, shard * mod_w), (1, mod_w))
    gat, modp, gp_in = _prologue(cpack, w_ada[0], b_shard, wpack_in, "prologue")
    c_all = gat[:, 0, :]
    cw = gat[0::2, 8:10, :].reshape(N_SHARDS, 2 * D_MODEL)[:, :4 * conv_cols].reshape(N_SHARDS, 4, conv_cols)
    conv_w_full = jnp.transpose(cw, (1, 0, 2)).reshape(4, D_CONV)
    mod_mine = lax.dynamic_slice(modp, (0, me, 0), (N_SHARDS, 1, mod_w)).reshape(6, D_MODEL)
    mod = _pad_rows(mod_mine, 8)
    wi_t = gp_in[:, :w_in_cols, :].reshape(D_IN_PROJ, D_MODEL)
    w_inp_t = jnp.concatenate([wi_t[0:1024], wi_t[2576:5648], wi_t[1024:2560], wi_t[2560:2576],
                               jnp.zeros((D_PROJ_PAD - D_IN_PROJ, D_MODEL), BF16)], axis=0)

    pad128 = lambda a: _pad_cols(a, 128)
    dtb, alog = pad128(dt_bias), pad128(a_log)
    dsk = jnp.repeat(d_skip, HEAD_DIM, axis=1)
    qw, kw = jnp.tile(q_norm_w, (1, N_HEADS)), jnp.tile(k_norm_w, (1, N_HEADS))

    h1 = _norm_mod(xs, norm1_w, mod, 0, "norm1")
    proj = _matmul(h1, w_inp_t, "nt", F32, "in_proj")
    u = _conv_fwd(proj, conv_w_full, conv_b, "conv_fwd")
    y_ssd, yn, prev = _ssd_fwd(u, proj, dtb, alog, dsk, ssd_norm_w, cst, "ssd_fwd")
    qs, kn, vb = _qk_fwd(proj, qw, kw, cst, "qk_norm")
    rtot, ycat, cnt, gp_rest = _sb_fwd(qs, kn, vb, wpack_rest, yn, cst, "sb_fwd")
    p_out, p_gate, p_up, p_down = _unpack_rest(gp_rest)
    w_o = p_out.reshape(2 * D_MODEL, D_MODEL)
    w_gu_t = jnp.concatenate([p_gate.reshape(D_FF, D_MODEL), p_up.reshape(D_FF, D_MODEL)], axis=0)
    w_d = p_down.reshape(D_FF, D_MODEL)
    mix = _matmul(ycat, w_o, "nn", BF16, "out_proj")
    x1, h2 = _resid_norm(xs, mix, norm2_w, mod, "resid_norm2")
    gu = _matmul(h2, w_gu_t, "nt", BF16, "ffn_in")
    act = _act_fwd(gu, "ffn_act")
    ffn = _matmul(act, w_d, "nn", BF16, "ffn_out", tk_cap=1408)
    dffn, dout, dg2, loss8 = _loss_head(x1, ffn, tgt, mod, "loss_head")
    loss = lax.psum(loss8[0, 0], ("x", "y", "c"))

    dact = _matmul(dffn, w_d, "nt", BF16, "d_act")
    g_down = _matmul(act, dffn, "tn", F32, "g_w_down", tm_cap=1408)
    dgu = _act_bwd(dact, gu, "ffn_act_bwd")
    dh2 = _matmul(dgu, w_gu_t, "nn", BF16, "d_h2", tk_cap=1408)
    g_gu_t = _matmul(dgu, h2, "tn", F32, "g_w_gu", tm_cap=1408)
    dx1, dmix, acc2 = _norm_bwd(dh2, x1, dout, mix, norm2_w, mod, 3, 2, "norm2_bwd")
    dycat = _matmul(dmix, w_o, "nt", BF16, "d_ycat")
    g_out = _matmul(ycat, dmix, "tn", F32, "g_w_out")
    gpack_rest = jnp.concatenate([
        g_out.reshape(N_SHARDS, R_OUT, D_MODEL),
        g_gu_t[:D_FF].reshape(N_SHARDS, R_FF, D_MODEL), g_gu_t[D_FF:].reshape(N_SHARDS, R_FF, D_MODEL),
        g_down.reshape(N_SHARDS, R_FF, D_MODEL)], axis=1)
    du, ddt, dz, acc_ssd, acc16, got_s = _ssd_bwd(u, proj, y_ssd, prev, dycat, dtb, alog, dsk, ssd_norm_w,
                                                  gpack_rest, cst, "ssd_bwd")
    csum_r, csum_rb = _chip_sum(gpack_rest, got_s, ac.reshape(1).astype(jnp.int32), "rs_chip_sum_rest")
    dqs, dkn, dv, got_r = _sb_bwd(qs, kn, vb, rtot, cnt, dycat, csum_rb, cst, "sb_bwd")
    r_out, r_gate, r_up, r_down = _unpack_rest(_reduce_tail(csum_r, got_r, shard, ac, "rest"))
    dq, dk, dvb, acc_qk = _qk_bwd(proj, dqs, dkn, dv, qw, kw, cst, "qk_norm_bwd")
    dxbc, g_conv_w, g_conv_b = _conv_bwd(proj, du, conv_w_full, conv_b, "conv_bwd")
    dproj = jnp.concatenate([dz, dq, dk, dvb, dxbc, ddt], axis=1)
    g_inp_t = _matmul(dproj, h1, "tn", F32, "g_w_in", tm_cap=1920)
    gi_t = jnp.concatenate([g_inp_t[0:1024], g_inp_t[4096:5632], g_inp_t[5632:5648], g_inp_t[1024:4096]], axis=0)
    gpack_in = jnp.pad(gi_t.reshape(N_SHARDS, w_in_cols, D_MODEL), ((0, 0), (0, R_IN - w_in_cols), (0, 0)))
    csum_i, csum_ib = _reduce_head(gpack_in, ac, "in")
    dh1, got_i = _matmul(dproj, w_inp_t, "nn", BF16, "d_h1", tk_cap=1152, exchange=csum_ib)
    r_in = _reduce_tail(csum_i, got_i, shard, ac, "in")
    grad_x, acc1 = _norm_bwd(dh1, xs, dx1, None, norm1_w, mod, 0, None, "norm1_bwd")

    last = jnp.concatenate([acc_qk[0:1, 0:64], acc_qk[1:2, 0:64], acc16[0:1, 0:16], acc16[1:2, 0:16],
                            acc16[2:3, 0:16]], axis=1)
    spack = jnp.concatenate([
        acc1[0:2], acc2[3:4], acc2[0:2], dg2,
        acc1[2:3], acc2[2:3], acc_ssd[0:1],
        _pad_cols(g_conv_b, 2 * D_MODEL).reshape(2, D_MODEL),
        g_conv_w.reshape(6, D_MODEL),
        _pad_cols(last, D_MODEL)], axis=0)
    sgat, ssum = _small_reduce(_pad_rows(spack, SMALL_ROWS), "gather_small")
    g_b_ada = ssum[0:6].reshape(1, 6 * D_MODEL)
    g_norm1, g_norm2, g_ssdn = ssum[6:7], ssum[7:8], ssum[8:9]
    g_cb = ssum[9:11].reshape(1, 2 * D_MODEL)[:, :D_CONV]
    g_cw = lax.dynamic_slice(ssum[11:17].reshape(4, D_CONV), (0, shard * conv_cols), (4, conv_cols))
    g_qn, g_kn = ssum[17:18, 0:64], ssum[17:18, 64:128]
    g_dtb, g_alog, g_dsk = ssum[17:18, 128:144], ssum[17:18, 144:160], ssum[17:18, 160:176]
    dmod_all = sgat[:, 0:6, :].reshape(8, 6 * D_MODEL)
    g_w_ada = _w_ada_grad(c_all, lax.dynamic_slice(dmod_all, (0, shard * mod_w), (8, mod_w)), "g_w_ada")


    grads = dict(w_ada=g_w_ada, b_ada=g_b_ada, norm1_w=g_norm1, w_in=r_in[:w_in_cols].reshape(-1, 128), conv_w=g_cw,
                 conv_b=g_cb, dt_bias=g_dtb, a_log=g_alog, d_skip=g_dsk, ssd_norm_w=g_ssdn, q_norm_w=g_qn,
                 k_norm_w=g_kn, w_out=r_out, norm2_w=g_norm2, w_gate=r_gate, w_up=r_up, w_down=r_down)
    weights = dict(w_ada=(w_ada, m_w_ada, v_w_ada), b_ada=(b_ada, m_b_ada, v_b_ada),
                   norm1_w=(norm1_w, m_norm1_w, v_norm1_w), w_in=(w_in, m_w_in, v_w_in),
                   conv_w=(conv_w, m_conv_w, v_conv_w), conv_b=(conv_b, m_conv_b, v_conv_b),
                   dt_bias=(dt_bias, m_dt_bias, v_dt_bias), a_log=(a_log, m_a_log, v_a_log),
                   d_skip=(d_skip, m_d_skip, v_d_skip), ssd_norm_w=(ssd_norm_w, m_ssd_norm_w, v_ssd_norm_w),
                   q_norm_w=(q_norm_w, m_q_norm_w, v_q_norm_w), k_norm_w=(k_norm_w, m_k_norm_w, v_k_norm_w),
                   w_out=(w_out, m_w_out, v_w_out), norm2_w=(norm2_w, m_norm2_w, v_norm2_w),
                   w_gate=(w_gate, m_w_gate, v_w_gate), w_up=(w_up, m_w_up, v_w_up),
                   w_down=(w_down, m_w_down, v_w_down))
    views = dict(w_in=(lin, unlin), w_gate=(tr3, tr3), w_up=(tr3, tr3))
    same = lambda a: a
    names = list(weights)
    g_out_l, d_out_l, m_out_l, v_out_l = [], [], [], []
    for n in names:
        view, back = views.get(n, (same, same))
        w, m, v = (view(a) for a in weights[n])
        g = grads[n].reshape(w.shape)
        d, nm, nv = _adamw(w, g, m, v, "adamw_" + n)
        g_out_l.append(back(g))
        d_out_l.append(back(d))
        m_out_l.append(back(nm))
        v_out_l.append(back(nv))
    return (loss, grad_x[None], *g_out_l, *d_out_l, *m_out_l, *v_out_l)
```

```python
import functools

import numpy as np
import jax
import jax.numpy as jnp
from jax import lax
from jax.experimental import pallas as pl
from jax.experimental.pallas import tpu as pltpu

F32, BF16 = jnp.float32, jnp.bfloat16
MESH = pl.DeviceIdType.MESH

D_MODEL = 1024
HEAD_DIM = 64
N_HEADS = 16
D_CONV = 1536
D_FF = 2816
D_IN_PROJ = 5648
D_PROJ_PAD = 5760
CHUNK = 128
SB_TILE = 256
FWD_PAIRS, BWD_PAIRS = 4, 2
SB_DEAD = -105.0
EPS = 1e-6
N_SHARDS = 4
R_IN, R_OUT, R_FF = 1440, 512, 704
SMALL_ROWS = 24

ADAM_LR, ADAM_B1, ADAM_B2, ADAM_EPS, ADAM_WD, ADAM_STEP = 0.001, 0.9, 0.999, 1e-08, 0.01, 10

VMEM_LIMIT = 48 * 1024 * 1024
ADAM_BLOCK_BYTES = 3 * 512 * 1024

_DN = {"nn": (((1,), (0,)), ((), ())), "nt": (((1,), (1,)), ((), ())), "tn": (((0,), (0,)), ((), ()))}


def _dot(a, b, dims="nn"):
    return lax.dot_general(a, b, _DN[dims], preferred_element_type=F32)


def _pieces(x, n):
    out = []
    for _ in range(n - 1):
        hi = lax.bitcast_convert_type(lax.bitcast_convert_type(x, jnp.int32) & jnp.int32(-65536), F32)
        out.append(hi.astype(BF16))
        x = x - hi
    out.append(x.astype(BF16))
    return out


def _dotx_r(x, b_exact, n=3):
    return _dot(jnp.concatenate(_pieces(x, n), axis=1), jnp.concatenate([b_exact] * n, axis=0))


def _dotx_l(a_exact, x, n=3):
    return _dot(jnp.concatenate([a_exact] * n, axis=1), jnp.concatenate(_pieces(x, n), axis=0))


def _dot2(x, b2):
    return _dot(jnp.concatenate(_pieces(x, 2), axis=1), b2)


def _sig(x):
    return 1.0 / (1.0 + jnp.exp(-x))


def _softplus(x):
    return jnp.maximum(x, 0.0) + jnp.log(1.0 + jnp.exp(-jnp.abs(x)))


def _cp(sem=None, vmem=VMEM_LIMIT):
    return pltpu.CompilerParams(dimension_semantics=sem, vmem_limit_bytes=vmem)


def _colsum(x):
    return jnp.sum(x, axis=0, keepdims=True)


def _consts():
    ch = np.arange(D_MODEL)
    expand = (np.arange(128)[:, None] == (ch // HEAD_DIM)[None, :]).astype(np.float32)
    fold = (ch[:, None] % HEAD_DIM == np.arange(128)[None, :]).astype(np.float32)
    i = np.arange(CHUNK)
    tril = (i[:, None] >= i[None, :]).astype(np.float32)
    j = np.arange(SB_TILE)
    ustrict = (j[:, None] > j[None, :]).astype(np.float32)
    ule = (j[:, None] <= j[None, :]).astype(np.float32)
    ult = (j[:, None] < j[None, :]).astype(np.float32)
    c = lambda a: jnp.asarray(a, BF16)
    return dict(expand=c(expand), hsum=c(expand.T), fold=c(fold), tril=c(tril), triu=c(tril.T),
                ustrict=ustrict, ule=ule, ult=ult)


def _doubled(tri, tk):
    b = tri[:tk, :tk]
    return jnp.asarray(np.concatenate([b, b], axis=0), BF16)


def _pick(n, cap):
    best = 128
    for t in range(128, min(n, cap) + 1, 128):
        if n % t == 0:
            best = t
    return n if n <= cap else best


def _matmul(a, b, dims, out_dtype, name, tm_cap=1024, tn_cap=2048, tk_cap=1024, exchange=None):
    if dims == "nn":
        (m, k), (_, n) = a.shape, b.shape
    elif dims == "nt":
        (m, k), (n, _) = a.shape, b.shape
    else:
        (k, m), (_, n) = a.shape, b.shape
    tm, tn, tk = _pick(m, tm_cap), _pick(n, tn_cap), _pick(k, tk_cap)
    nk = k // tk
    a_spec = (pl.BlockSpec((tk, tm), lambda i, j, kk: (kk, i)) if dims == "tn"
              else pl.BlockSpec((tm, tk), lambda i, j, kk: (i, kk)))
    b_spec = (pl.BlockSpec((tn, tk), lambda i, j, kk: (j, kk)) if dims == "nt"
              else pl.BlockSpec((tk, tn), lambda i, j, kk: (kk, j)))

    grid = (m // tm, n // tn, nk)

    def body(a_ref, b_ref, *rest):
        if exchange is None:
            o_ref, acc_ref = rest
        else:
            xs_ref, o_ref, xr_ref, acc_ref, xss, xrs = rest
            ids = [pl.program_id(d) for d in range(3)]
            x_start, x_finish = _exchange_stages(xs_ref, xr_ref, xss, xrs)
            pl.when(functools.reduce(jnp.logical_and, [p == 0 for p in ids]))(x_start)
        kk = pl.program_id(2)
        part = _dot(a_ref[...], b_ref[...], dims)
        if nk == 1:
            o_ref[...] = part.astype(out_dtype)
        else:
            @pl.when(kk == 0)
            def _():
                acc_ref[...] = part

            @pl.when(kk > 0)
            def _():
                acc_ref[...] += part

            @pl.when(kk == nk - 1)
            def _():
                o_ref[...] = acc_ref[...].astype(out_dtype)
        if exchange is not None:
            pl.when(functools.reduce(jnp.logical_and, [p == g - 1 for p, g in zip(ids, grid)]))(x_finish)

    in_specs = [a_spec, b_spec]
    out_specs = [pl.BlockSpec((tm, tn), lambda i, j, kk: (i, j))]
    out_shape = [jax.ShapeDtypeStruct((m, n), out_dtype)]
    scratch = [pltpu.VMEM((tm, tn) if nk > 1 else (8, 128), F32)]
    args = [a, b]
    if exchange is not None:
        in_specs.append(HB)
        out_specs.append(HB)
        out_shape.append(jax.ShapeDtypeStruct((3,) + exchange.shape[1:], exchange.dtype))
        scratch += EXCHANGE_SEMS
        args.append(exchange)
    out = pl.pallas_call(
        body, name=name, grid=grid, in_specs=in_specs, out_specs=out_specs, out_shape=out_shape,
        scratch_shapes=scratch,
        compiler_params=_cp(("parallel", "parallel", "arbitrary") if exchange is None else ("arbitrary",) * 3),
    )(*args)
    return out[0] if exchange is None else out


def _row_spec(tm, width=D_MODEL, col=0):
    return pl.BlockSpec((tm, width), lambda i: (i, col))


def _fix_spec(shape):
    return pl.BlockSpec(shape, lambda *_: (0,) * len(shape))


def _norm_mod(x, nw, mod, row_sh, name):
    t = x.shape[0]
    tm = min(t, 512)

    def body(x_ref, nw_ref, mod_ref, h_ref):
        xv = x_ref[...]
        r = lax.rsqrt(jnp.mean(xv * xv, axis=-1, keepdims=True) + EPS)
        sh = mod_ref[row_sh:row_sh + 1, :]
        sc = mod_ref[row_sh + 1:row_sh + 2, :]
        h_ref[...] = (xv * r * nw_ref[...] * (1.0 + sc) + sh).astype(BF16)

    return pl.pallas_call(
        body, name=name, grid=(t // tm,),
        in_specs=[_row_spec(tm), _fix_spec((1, D_MODEL)), _fix_spec((8, D_MODEL))],
        out_specs=_row_spec(tm), out_shape=jax.ShapeDtypeStruct((t, D_MODEL), BF16),
        compiler_params=_cp(("parallel",)),
    )(x, nw, mod)


def _resid_norm(x, mix, nw, mod, name):
    t = x.shape[0]
    tm = min(t, 512)

    def body(x_ref, mix_ref, nw_ref, mod_ref, x1_ref, h_ref):
        x1 = x_ref[...] + mod_ref[2:3, :] * mix_ref[...].astype(F32)
        x1_ref[...] = x1
        r = lax.rsqrt(jnp.mean(x1 * x1, axis=-1, keepdims=True) + EPS)
        h_ref[...] = (x1 * r * nw_ref[...] * (1.0 + mod_ref[4:5, :]) + mod_ref[3:4, :]).astype(BF16)

    return pl.pallas_call(
        body, name=name, grid=(t // tm,),
        in_specs=[_row_spec(tm), _row_spec(tm), _fix_spec((1, D_MODEL)), _fix_spec((8, D_MODEL))],
        out_specs=[_row_spec(tm), _row_spec(tm)],
        out_shape=[jax.ShapeDtypeStruct((t, D_MODEL), F32), jax.ShapeDtypeStruct((t, D_MODEL), BF16)],
        compiler_params=_cp(("parallel",)),
    )(x, mix, nw, mod)


def _act_fwd(gu, name):
    t = gu.shape[0]
    tm, tn = min(t, 512), D_FF // 2
    nb = D_FF // tn

    def body(g_ref, u_ref, a_ref):
        g = g_ref[...].astype(F32)
        a_ref[...] = (g * _sig(g) * u_ref[...].astype(F32)).astype(BF16)

    return pl.pallas_call(
        body, name=name, grid=(t // tm, nb),
        in_specs=[pl.BlockSpec((tm, tn), lambda i, j: (i, j)), pl.BlockSpec((tm, tn), lambda i, j: (i, j + nb))],
        out_specs=pl.BlockSpec((tm, tn), lambda i, j: (i, j)),
        out_shape=jax.ShapeDtypeStruct((t, D_FF), BF16),
        compiler_params=_cp(("parallel", "parallel")),
    )(gu, gu)


def _act_bwd(dact, gu, name):
    t = gu.shape[0]
    tm = min(t, 256)

    def body(d_ref, g_ref, u_ref, o_ref):
        g, d = g_ref[...].astype(F32), d_ref[...].astype(F32)
        s = _sig(g)
        o_ref[:, 0:D_FF] = (d * u_ref[...].astype(F32) * s * (1.0 + g * (1.0 - s))).astype(BF16)
        o_ref[:, D_FF:2 * D_FF] = (d * g * s).astype(BF16)

    return pl.pallas_call(
        body, name=name, grid=(t // tm,),
        in_specs=[pl.BlockSpec((tm, D_FF), lambda i: (i, 0)), pl.BlockSpec((tm, D_FF), lambda i: (i, 0)),
                  pl.BlockSpec((tm, D_FF), lambda i: (i, 1))],
        out_specs=pl.BlockSpec((tm, 2 * D_FF), lambda i: (i, 0)),
        out_shape=jax.ShapeDtypeStruct((t, 2 * D_FF), BF16),
        compiler_params=_cp(("parallel",)),
    )(dact, gu, gu)


def _loss_head(x1, ffn, tgt, mod, name):
    t = x1.shape[0]
    tm = min(t, 512)

    def body(x1_ref, f_ref, t_ref, mod_ref, dffn_ref, dout_ref, dg2_ref, loss_ref):
        i = pl.program_id(0)
        g2 = mod_ref[5:6, :]
        f = f_ref[...].astype(F32)
        err = x1_ref[...] + g2 * f - t_ref[...]
        dout = err * (1.0 / D_MODEL)
        dout_ref[...] = dout
        dffn_ref[...] = (dout * g2).astype(BF16)
        part = jnp.zeros((8, 128), F32) + 0.5 * jnp.sum(jnp.mean(err * err, axis=-1, keepdims=True))

        @pl.when(i == 0)
        def _():
            dg2_ref[...] = _colsum(dout * f)
            loss_ref[...] = part

        @pl.when(i > 0)
        def _():
            dg2_ref[...] += _colsum(dout * f)
            loss_ref[...] += part

    return pl.pallas_call(
        body, name=name, grid=(t // tm,),
        in_specs=[_row_spec(tm), _row_spec(tm), _row_spec(tm), _fix_spec((8, D_MODEL))],
        out_specs=[_row_spec(tm), _row_spec(tm), _fix_spec((1, D_MODEL)), _fix_spec((8, 128))],
        out_shape=[jax.ShapeDtypeStruct((t, D_MODEL), BF16), jax.ShapeDtypeStruct((t, D_MODEL), F32),
                   jax.ShapeDtypeStruct((1, D_MODEL), F32), jax.ShapeDtypeStruct((8, 128), F32)],
        compiler_params=_cp(("arbitrary",)),
    )(x1, ffn, tgt, mod)


def _norm_bwd(dh, xin, dres, aux, nw, mod, row_sh, gate_row, name):
    t = xin.shape[0]
    tm = min(t, 512)
    with_gate = gate_row is not None

    def body(*refs):
        if with_gate:
            dh_ref, x_ref, dr_ref, aux_ref, nw_ref, mod_ref, dx_ref, dg_ref, acc_ref = refs
        else:
            dh_ref, x_ref, dr_ref, nw_ref, mod_ref, dx_ref, acc_ref = refs
        i = pl.program_id(0)
        xv, dhv = x_ref[...], dh_ref[...].astype(F32)
        r = lax.rsqrt(jnp.mean(xv * xv, axis=-1, keepdims=True) + EPS)
        xn = xv * r
        nwv = nw_ref[...]
        sc1 = 1.0 + mod_ref[row_sh + 1:row_sh + 2, :]
        dxn = dhv * (nwv * sc1)
        dx = dr_ref[...] + r * (dxn - xn * jnp.mean(dxn * xn, axis=-1, keepdims=True))
        dx_ref[...] = dx
        dhx = dhv * xn
        rows = [_colsum(dhv), _colsum(dhx * nwv), _colsum(dhx * sc1)]
        if with_gate:
            dg_ref[...] = (dx * mod_ref[gate_row:gate_row + 1, :]).astype(BF16)
            rows.append(_colsum(dx * aux_ref[...].astype(F32)))

        @pl.when(i == 0)
        def _():
            acc_ref[...] = jnp.zeros_like(acc_ref)

        for k, v in enumerate(rows):
            acc_ref[k:k + 1, :] += v

    ins = [dh, xin, dres] + ([aux] if with_gate else []) + [nw, mod]
    in_specs = [_row_spec(tm)] * (4 if with_gate else 3) + [_fix_spec((1, D_MODEL)), _fix_spec((8, D_MODEL))]
    out_specs = [_row_spec(tm)] + ([_row_spec(tm)] if with_gate else []) + [_fix_spec((8, D_MODEL))]
    out_shape = ([jax.ShapeDtypeStruct((t, D_MODEL), F32)]
                 + ([jax.ShapeDtypeStruct((t, D_MODEL), BF16)] if with_gate else [])
                 + [jax.ShapeDtypeStruct((8, D_MODEL), F32)])
    return pl.pallas_call(
        body, name=name, grid=(t // tm,), in_specs=in_specs, out_specs=out_specs, out_shape=out_shape,
        compiler_params=_cp(("arbitrary",)),
    )(*ins)


XBC_COL0 = 4096 // 128
DT_COL = 5632 // 128


def _conv_pre(xv, w_ref, b_ref):
    t = xv.shape[0]
    row = lax.broadcasted_iota(jnp.int32, xv.shape, 0)
    pre = xv * w_ref[3:4, :] + b_ref[...]
    shifted = []
    for k in range(3):
        s = 3 - k
        xs = jnp.where(row >= s, pltpu.roll(xv, s, 0), 0.0)
        shifted.append(xs)
        pre = pre + xs * w_ref[k:k + 1, :]
    return pre, shifted, row, t


def _conv_fwd(proj, conv_w, conv_b, name):
    t = proj.shape[0]

    def body(x_ref, w_ref, b_ref, u_ref):
        pre, _, _, _ = _conv_pre(x_ref[...], w_ref, b_ref)
        u_ref[...] = pre * _sig(pre)

    return pl.pallas_call(
        body, name=name, grid=(D_CONV // 128,),
        in_specs=[pl.BlockSpec((t, 128), lambda j: (0, XBC_COL0 + j)), pl.BlockSpec((4, 128), lambda j: (0, j)),
                  pl.BlockSpec((1, 128), lambda j: (0, j))],
        out_specs=pl.BlockSpec((t, 128), lambda j: (0, j)),
        out_shape=jax.ShapeDtypeStruct((t, D_CONV), F32),
        compiler_params=_cp(("parallel",)),
    )(proj, conv_w, conv_b)


def _conv_bwd(proj, du, conv_w, conv_b, name):
    t = proj.shape[0]

    def body(x_ref, du_ref, w_ref, b_ref, dx_ref, dw_ref, db_ref):
        pre, shifted, row, _ = _conv_pre(x_ref[...], w_ref, b_ref)
        s = _sig(pre)
        dpre = du_ref[...] * s * (1.0 + pre * (1.0 - s))
        db_ref[...] = _colsum(dpre)
        dx = dpre * w_ref[3:4, :]
        dw_ref[3:4, :] = _colsum(dpre * x_ref[...])
        for k in range(3):
            sft = 3 - k
            dw_ref[k:k + 1, :] = _colsum(dpre * shifted[k])
            back = jnp.where(row < t - sft, pltpu.roll(dpre, t - sft, 0), 0.0)
            dx = dx + back * w_ref[k:k + 1, :]
        dx_ref[...] = dx.astype(BF16)

    return pl.pallas_call(
        body, name=name, grid=(D_CONV // 128,),
        in_specs=[pl.BlockSpec((t, 128), lambda j: (0, XBC_COL0 + j)), pl.BlockSpec((t, 128), lambda j: (0, j)),
                  pl.BlockSpec((4, 128), lambda j: (0, j)), pl.BlockSpec((1, 128), lambda j: (0, j))],
        out_specs=[pl.BlockSpec((t, 128), lambda j: (0, j)), pl.BlockSpec((4, 128), lambda j: (0, j)),
                   pl.BlockSpec((1, 128), lambda j: (0, j))],
        out_shape=[jax.ShapeDtypeStruct((t, D_CONV), BF16), jax.ShapeDtypeStruct((4, D_CONV), F32),
                   jax.ShapeDtypeStruct((1, D_CONV), F32)],
        compiler_params=_cp(("parallel",)),
    )(proj, du, conv_w, conv_b)


def _ssd_common(dtraw_ref, dtb_ref, alog_ref, tril, expand, l_s, lt_s):
    lane = lax.broadcasted_iota(jnp.int32, (1, 128), 1)
    dt = _softplus(dtraw_ref[...] + dtb_ref[...])
    a = jnp.where(lane < N_HEADS, -jnp.exp(alog_ref[...]), 0.0)
    lcs = _dotx_l(tril, dt * a)
    l_s[...] = lcs
    lt_s[...] = lcs.T
    llast = l_s[CHUNK - 1:CHUNK, :]
    ea = _dotx_r(jnp.exp(lcs), expand, n=2)
    ds = _dotx_r(jnp.exp(llast - lcs), expand, n=2)
    dtx = _dotx_r(dt, expand, n=2)
    return dt, a, lcs, llast, ea, ds, dtx


def _head_col(lcs, h):
    lane = lax.broadcasted_iota(jnp.int32, lcs.shape, 1)
    return jnp.sum(jnp.where(lane == h, lcs, 0.0), axis=1, keepdims=True)


def _decay(lcs, lt_s, h, causal):
    seg = _head_col(lcs, h) - lt_s[h:h + 1, :]
    return jnp.exp(jnp.where(causal, seg, -1e30))


def _ssd_fwd(u, proj, dtb, alog, dsk, nw, cst, name):
    t = u.shape[0]
    nc = t // CHUNK

    def body(xs_ref, b_ref, c_ref, dtraw_ref, z_ref, dtb_ref, alog_ref, dsk_ref, nw_ref, tril_ref, exp_ref,
             y_ref, yn_ref, prev_ref, carry, l_s, lt_s, yd_s):
        i = pl.program_id(0)

        @pl.when(i == 0)
        def _():
            carry[...] = jnp.zeros_like(carry)

        expand = exp_ref[...]
        dt, a, lcs, llast, ea, ds, dtx = _ssd_common(dtraw_ref, dtb_ref, alog_ref, tril_ref[...], expand, l_s, lt_s)
        xs = xs_ref[...]
        xg = xs * dtx
        xgb = xg.astype(BF16)
        xgd = (xg * ds).astype(BF16)
        prev = carry[...]
        prev_ref[0] = prev
        prevb = prev.astype(BF16)
        ri = lax.broadcasted_iota(jnp.int32, (CHUNK, CHUNK), 0)
        ci = lax.broadcasted_iota(jnp.int32, (CHUNK, CHUNK), 1)
        causal = ri >= ci
        lane = lax.broadcasted_iota(jnp.int32, (1, 128), 1)
        new_states, yoff = [], []
        for g in range(2):
            bg = b_ref[:, g * 128:(g + 1) * 128].astype(BF16)
            cg = c_ref[:, g * 128:(g + 1) * 128].astype(BF16)
            sc = _dot(cg, bg, "nt")
            gs = slice(g * 512, (g + 1) * 512)
            new_states.append(_dot(bg, xgd[:, gs], "tn"))
            yoff.append(_dot(cg, prevb[:, gs]))
            for pr in range(4):
                col = g * 512 + pr * 128
                xp = xgb[:, col:col + 128]
                acc = jnp.zeros((CHUNK, 128), F32)
                for half in range(2):
                    h = g * 8 + pr * 2 + half
                    m = (sc * _decay(lcs, lt_s, h, causal)).astype(BF16)
                    keep = (lane < HEAD_DIM) if half == 0 else (lane >= HEAD_DIM)
                    acc = acc + _dot(m, jnp.where(keep, xp, jnp.zeros_like(xp)))
                yd_s[:, col:col + 128] = acc
        y = yd_s[...] + jnp.concatenate(yoff, axis=1) * ea + xs * dsk_ref[...]
        y_ref[...] = y
        carry[...] = prev * jnp.max(_dotx_r(jnp.exp(llast) + jnp.zeros((8, 128), F32), expand, n=2), axis=0, keepdims=True) \
            + jnp.concatenate(new_states, axis=1)
        z = z_ref[...]
        yz = y * (z * _sig(z))
        nwv = nw_ref[...]
        for g in range(2):
            gs = slice(g * 512, (g + 1) * 512)
            v = yz[:, gs]
            r = lax.rsqrt(jnp.mean(v * v, axis=-1, keepdims=True) + EPS)
            yn_ref[:, gs] = (v * r * nwv[:, gs]).astype(BF16)

    row = lambda w, col: pl.BlockSpec((CHUNK, w), lambda i: (i, col))
    return pl.pallas_call(
        body, name=name, grid=(nc,),
        in_specs=[row(1024, 0), row(256, 4), row(256, 5), row(128, DT_COL), row(1024, 0),
                  _fix_spec((1, 128)), _fix_spec((1, 128)), _fix_spec((1, D_MODEL)), _fix_spec((1, D_MODEL)),
                  _fix_spec((CHUNK, CHUNK)), _fix_spec((128, D_MODEL))],
        out_specs=[row(1024, 0), row(1024, 0), pl.BlockSpec((1, 128, D_MODEL), lambda i: (i, 0, 0))],
        out_shape=[jax.ShapeDtypeStruct((t, D_MODEL), F32), jax.ShapeDtypeStruct((t, 2 * D_MODEL), BF16),
                   jax.ShapeDtypeStruct((nc, 128, D_MODEL), F32)],
        scratch_shapes=[pltpu.VMEM((128, D_MODEL), F32), pltpu.VMEM((128, 128), F32), pltpu.VMEM((128, 128), F32),
                        pltpu.VMEM((CHUNK, D_MODEL), F32)],
        compiler_params=_cp(("arbitrary",)),
    )(u, u, u, proj, proj, dtb, alog, dsk, nw, cst["tril"], cst["expand"])


def _ssd_bwd(u, proj, y, prev, dycat, dtb, alog, dsk, nw, gpack, cst, name):
    t = u.shape[0]
    nc = t // CHUNK

    def body(xs_ref, b_ref, c_ref, dtraw_ref, z_ref, y_ref, prev_ref, dyn_ref, dtb_ref, alog_ref, dsk_ref, nw_ref,
             tril_ref, triu_ref, exp_ref, hs_ref, g_ref,
             du_ref, ddt_ref, dz_ref, acc_ref, acc16_ref, got_ref, dcarry, l_s, lt_s, dxg_s, wss, wrs):
        i = pl.program_id(0)
        w_start, w_finish = _swap_stages(g_ref, got_ref, wss, wrs)
        pl.when(i == 0)(w_start)

        @pl.when(i == 0)
        def _():
            dcarry[...] = jnp.zeros_like(dcarry)
            acc_ref[...] = jnp.zeros_like(acc_ref)
            acc16_ref[...] = jnp.zeros_like(acc16_ref)

        expand, hsum = exp_ref[...], hs_ref[...]
        dt, a, lcs, llast, ea, ds, dtx = _ssd_common(dtraw_ref, dtb_ref, alog_ref, tril_ref[...], expand, l_s, lt_s)
        xs = xs_ref[...]
        xg = xs * dtx
        xgb = xg.astype(BF16)
        xgdf = xg * ds
        xgd = xgdf.astype(BF16)
        dsk_v, nwv = dsk_ref[...], nw_ref[...]
        z, y = z_ref[...], y_ref[...]
        sz = _sig(z)
        silz = z * sz
        yz = y * silz
        dyn = dyn_ref[...].astype(F32)
        dyz_parts, dnw_parts = [], []
        for g in range(2):
            gs = slice(g * 512, (g + 1) * 512)
            v = yz[:, gs]
            r = lax.rsqrt(jnp.mean(v * v, axis=-1, keepdims=True) + EPS)
            yhat = v * r
            dnw_parts.append(_colsum(dyn[:, gs] * yhat))
            dw = dyn[:, gs] * nwv[:, gs]
            dyz_parts.append(r * (dw - yhat * jnp.mean(dw * yhat, axis=-1, keepdims=True)))
        dyz = jnp.concatenate(dyz_parts, axis=1)
        dy = dyz * silz
        dz_ref[...] = (dyz * y * (sz * (1.0 + z * (1.0 - sz)))).astype(BF16)
        acc_ref[0:1, :] += jnp.concatenate(dnw_parts, axis=1)
        acc_ref[1:2, :] += _colsum(dy * xs)
        dyb = dy.astype(BF16)
        dq = (dy * ea).astype(BF16)
        dcar = dcarry[...]
        dcarb = dcar.astype(BF16)
        prev = prev_ref[0]
        prevb = prev.astype(BF16)
        ri = lax.broadcasted_iota(jnp.int32, (CHUNK, CHUNK), 0)
        ci = lax.broadcasted_iota(jnp.int32, (CHUNK, CHUNK), 1)
        causal = ri >= ci
        lane = lax.broadcasted_iota(jnp.int32, (1, 128), 1)
        dprev, dxgd, yoff = [], [], []
        dl_l = jnp.zeros((CHUNK, CHUNK), F32)
        dl_s = jnp.zeros((CHUNK, CHUNK), F32)
        for g in range(2):
            gs = slice(g * 512, (g + 1) * 512)
            bg = b_ref[:, g * 128:(g + 1) * 128].astype(BF16)
            cg = c_ref[:, g * 128:(g + 1) * 128].astype(BF16)
            sc = _dot(cg, bg, "nt")
            yoff.append(_dot(cg, prevb[:, gs]))
            dcg = _dot(dq[:, gs], prevb[:, gs], "nt")
            dprev.append(_dot(cg, dq[:, gs], "tn"))
            dbg = _dot(xgd[:, gs], dcarb[:, gs], "nt")
            dxgd.append(_dot(bg, dcarb[:, gs]))
            dsc = jnp.zeros((CHUNK, CHUNK), F32)
            for pr in range(4):
                col = g * 512 + pr * 128
                xp = xgb[:, col:col + 128]
                dyp = dyb[:, col:col + 128]
                acc = jnp.zeros((CHUNK, 128), F32)
                for half in range(2):
                    h = g * 8 + pr * 2 + half
                    dec = _decay(lcs, lt_s, h, causal)
                    mf = sc * dec
                    keep = (lane < HEAD_DIM) if half == 0 else (lane >= HEAD_DIM)
                    dyh = jnp.where(keep, dyp, jnp.zeros_like(dyp))
                    dm = _dot(dyh, xp, "nt")
                    acc = acc + _dot(mf.astype(BF16), dyh, "tn")
                    dsc = dsc + dm * dec
                    gm = dm * mf
                    dl_l = dl_l + jnp.where(ci == h, jnp.sum(gm, axis=1, keepdims=True), 0.0)
                    dl_s = dl_s + jnp.where(ri == h, jnp.sum(gm, axis=0, keepdims=True), 0.0)
                dxg_s[:, col:col + 128] = acc
            dscb = dsc.astype(BF16)
            dcg = dcg + _dot(dscb, bg)
            dbg = dbg + _dot(dscb, cg, "tn")
            du_ref[:, 1024 + g * 128:1024 + (g + 1) * 128] = dbg
            du_ref[:, 1280 + g * 128:1280 + (g + 1) * 128] = dcg
        dxgd = jnp.concatenate(dxgd, axis=1)
        dxg = dxg_s[...] + dxgd * ds
        du_ref[:, 0:1024] = dy * dsk_v + dxg * dtx
        hs1 = _dotx_r(dxg * xs, hsum, n=2)
        yoff = jnp.concatenate(yoff, axis=1) * ea
        dl = dl_l - dl_s.T + _dotx_r(dy * yoff - xgdf * dxgd, hsum, n=2)
        rows8 = lax.broadcasted_iota(jnp.int32, (8, D_MODEL), 0)
        two = jnp.where(rows8 == 0, _colsum(dxgd * xgdf), jnp.where(rows8 == 1, _colsum(dcar * prev), 0.0))
        two = _dotx_r(two, hsum)
        r8 = lax.broadcasted_iota(jnp.int32, (8, 128), 0)
        dllast = _colsum(jnp.where(r8 == 0, two, 0.0)) + _colsum(jnp.where(r8 == 1, two, 0.0)) * jnp.exp(llast)
        rowi = lax.broadcasted_iota(jnp.int32, (CHUNK, 128), 0)
        dl = dl + jnp.where(rowi == CHUNK - 1, dllast, 0.0)
        dadt = _dotx_l(triu_ref[...], dl)
        ddt = dadt * a + hs1
        draw = ddt * _sig(dtraw_ref[...] + dtb_ref[...])
        ddt_ref[...] = draw.astype(BF16)
        acc16_ref[0:1, :] += _colsum(draw)
        acc16_ref[1:2, :] += _colsum(dadt * dt) * a
        dcarry[...] = dcar * jnp.max(_dotx_r(jnp.exp(llast) + jnp.zeros((8, 128), F32), expand, n=2), axis=0, keepdims=True) \
            + jnp.concatenate(dprev, axis=1)

        @pl.when(i == nc - 1)
        def _():
            hd = _dotx_r(acc_ref[...], hsum)
            acc16_ref[2:3, :] = _colsum(jnp.where(lax.broadcasted_iota(jnp.int32, (8, 128), 0) == 1, hd, 0.0))

        pl.when(i == nc - 1)(w_finish)

    rev = lambda w, col: pl.BlockSpec((CHUNK, w), lambda i: (nc - 1 - i, col))
    return pl.pallas_call(
        body, name=name, grid=(nc,),
        in_specs=[rev(1024, 0), rev(256, 4), rev(256, 5), rev(128, DT_COL), rev(1024, 0), rev(1024, 0),
                  pl.BlockSpec((1, 128, D_MODEL), lambda i: (nc - 1 - i, 0, 0)), rev(1024, 0),
                  _fix_spec((1, 128)), _fix_spec((1, 128)), _fix_spec((1, D_MODEL)), _fix_spec((1, D_MODEL)),
                  _fix_spec((CHUNK, CHUNK)), _fix_spec((CHUNK, CHUNK)), _fix_spec((128, D_MODEL)),
                  _fix_spec((D_MODEL, 128)), HB],
        out_specs=[rev(D_CONV, 0), rev(128, 0), rev(1024, 0), _fix_spec((8, D_MODEL)), _fix_spec((8, 128)), HB],
        out_shape=[jax.ShapeDtypeStruct((t, D_CONV), F32), jax.ShapeDtypeStruct((t, 128), BF16),
                   jax.ShapeDtypeStruct((t, D_MODEL), BF16), jax.ShapeDtypeStruct((8, D_MODEL), F32),
                   jax.ShapeDtypeStruct((8, 128), F32), _swap_shape(gpack)],
        scratch_shapes=[pltpu.VMEM((128, D_MODEL), F32), pltpu.VMEM((128, 128), F32), pltpu.VMEM((128, 128), F32),
                        pltpu.VMEM((CHUNK, D_MODEL), F32)] + SWAP_SEMS,
        compiler_params=_cp(("arbitrary",)),
    )(u, u, u, proj, proj, y, prev, dycat, dtb, alog, dsk, nw,
      cst["tril"], cst["triu"], cst["expand"], cst["hsum"], gpack)


def _head_rms(v, hsum, expand):
    ms = _dotx_r(v * v, hsum, n=2) * (1.0 / HEAD_DIM)
    return _dotx_r(lax.rsqrt(ms + EPS), expand, n=2)


def _qk_fwd(proj, qw, kw, cst, name):
    t = proj.shape[0]
    tm = min(t, 256)
    scale = HEAD_DIM ** -0.5

    def body(q_ref, k_ref, v_ref, qw_ref, kw_ref, hs_ref, exp_ref, qs_ref, kn_ref, vb_ref):
        hsum, expand = hs_ref[...], exp_ref[...]
        q, k = q_ref[...], k_ref[...]
        qs_ref[...] = (q * _head_rms(q, hsum, expand) * qw_ref[...] * scale).astype(BF16)
        kn_ref[...] = (k * _head_rms(k, hsum, expand) * kw_ref[...]).astype(BF16)
        vb_ref[...] = v_ref[...].astype(BF16)

    return pl.pallas_call(
        body, name=name, grid=(t // tm,),
        in_specs=[_row_spec(tm, col=1), _row_spec(tm, col=2), _row_spec(tm, col=3),
                  _fix_spec((1, D_MODEL)), _fix_spec((1, D_MODEL)), _fix_spec((D_MODEL, 128)),
                  _fix_spec((128, D_MODEL))],
        out_specs=[_row_spec(tm)] * 3, out_shape=[jax.ShapeDtypeStruct((t, D_MODEL), BF16)] * 3,
        compiler_params=_cp(("parallel",)),
    )(proj, proj, proj, qw, kw, cst["hsum"], cst["expand"])


def _qk_bwd(proj, dqs, dkn, dv, qw, kw, cst, name):
    t = proj.shape[0]
    tm = min(t, 256)
    scale = HEAD_DIM ** -0.5

    def body(q_ref, k_ref, dq_ref, dk_ref, dv_ref, qw_ref, kw_ref, hs_ref, exp_ref, fold_ref,
             oq_ref, ok_ref, ov_ref, dw_ref):
        i = pl.program_id(0)
        hsum, expand = hs_ref[...], exp_ref[...]
        rows8 = lax.broadcasted_iota(jnp.int32, (8, D_MODEL), 0)
        sums = jnp.zeros((8, D_MODEL), F32)
        for n, (x_ref, d_ref, w_ref, o_ref, sc) in enumerate(
                [(q_ref, dq_ref, qw_ref, oq_ref, scale), (k_ref, dk_ref, kw_ref, ok_ref, 1.0)]):
            xv = x_ref[...]
            r = _head_rms(xv, hsum, expand)
            xhat = xv * r
            dn = d_ref[...] * sc
            sums = sums + jnp.where(rows8 == n, _colsum(dn * xhat), 0.0)
            dw = dn * w_ref[...]
            mean = _dotx_r(_dotx_r(dw * xhat, hsum, n=1), expand, n=2) * (1.0 / HEAD_DIM)
            o_ref[...] = (r * (dw - xhat * mean)).astype(BF16)
        ov_ref[...] = dv_ref[...].astype(BF16)
        folded = _dotx_r(sums, fold_ref[...])

        @pl.when(i == 0)
        def _():
            dw_ref[...] = folded

        @pl.when(i > 0)
        def _():
            dw_ref[...] += folded

    return pl.pallas_call(
        body, name=name, grid=(t // tm,),
        in_specs=[_row_spec(tm, col=1), _row_spec(tm, col=2), _row_spec(tm), _row_spec(tm), _row_spec(tm),
                  _fix_spec((1, D_MODEL)), _fix_spec((1, D_MODEL)), _fix_spec((D_MODEL, 128)),
                  _fix_spec((128, D_MODEL)), _fix_spec((D_MODEL, 128))],
        out_specs=[_row_spec(tm)] * 3 + [_fix_spec((8, 128))],
        out_shape=[jax.ShapeDtypeStruct((t, D_MODEL), BF16)] * 3 + [jax.ShapeDtypeStruct((8, 128), F32)],
        compiler_params=_cp(("arbitrary",)),
    )(proj, proj, dqs, dkn, dv, qw, kw, cst["hsum"], cst["expand"], cst["fold"])


def _sb_masks(i, kb, tq, tk):
    tpos = i * tq + lax.broadcasted_iota(jnp.int32, (tq, 1), 0)
    spos = kb * tk + lax.broadcasted_iota(jnp.int32, (1, tk), 1)
    return spos < tpos


def _grid_marks(n0, n1):
    j, i = pl.program_id(0), pl.program_id(1)
    return (jnp.logical_and(j == 0, i == 0), jnp.logical_and(j == n0 // 2, i == 0),
            jnp.logical_and(j == n0 - 1, i == n1 - 1))


def _sb_fwd(qs, kn, vb, pack, ycat, cst, name):
    t = qs.shape[0]
    tq = tk = min(t, SB_TILE)
    nq = t // tq
    pairs = FWD_PAIRS
    ngrp = D_MODEL // (128 * pairs)
    nh = 2 * pairs
    lanes = lambda p: slice(p * 128, (p + 1) * 128)

    def body(q_ref, k_ref, v_ref, u_ref, p_ref, yc_ref, rt_ref, ob_ref, cnt_ref, gat_ref, acc, rs, gss, grs):
        del yc_ref
        at_first, at_mid, at_last = _grid_marks(ngrp, nq)
        g_start, g_relay, g_finish = _gather_stages(p_ref, gat_ref, gss, grs)
        pl.when(at_first)(g_start)
        pl.when(at_mid)(g_relay)
        i = pl.program_id(1)
        lane = lax.broadcasted_iota(jnp.int32, (1, 128), 1)
        qh = []
        for p in range(pairs):
            q2 = q_ref[:, lanes(p)]
            zero = jnp.zeros_like(q2)
            qh += [jnp.where(lane < HEAD_DIM, q2, zero), jnp.where(lane >= HEAD_DIM, q2, zero)]
        acc[...] = jnp.zeros_like(acc)
        rs[...] = jnp.zeros_like(rs)
        ustrict = u_ref[...]

        def tile(kb, masked):
            off = pl.multiple_of(kb * tk, tk)
            k2 = [k_ref[pl.ds(off, tk), lanes(p)] for p in range(pairs)]
            v2 = [v_ref[pl.ds(off, tk), lanes(p)] for p in range(pairs)]
            strict = _sb_masks(i, kb, tq, tk) if masked else None
            s = [_dot(qh[h], k2[h // 2], "nt") for h in range(nh)]
            a, r, lb = [None] * nh, [None] * nh, [None] * nh
            for h in range(nh):
                sp = _softplus(s[h])
                a[h] = s[h] - sp
                r[h] = jnp.where(strict, -sp, 0.0) if masked else -sp
                lb[h] = _dot2(r[h], ustrict)
            for h in range(nh):
                lw = a[h] + lb[h] + rs[h]
                w = jnp.exp(jnp.where(strict, lw, -1e30) if masked else lw)
                rs[h] = rs[h] + jnp.sum(r[h], axis=1, keepdims=True)
                acc[h] = acc[h] + _dot(w.astype(BF16), v2[h // 2])

        tile(i, True)

        def live():
            return jnp.max(functools.reduce(jnp.maximum, [rs[h] for h in range(nh)]))

        def more(c):
            return jnp.logical_and(c[0] < i, c[1] > SB_DEAD)

        def step(c):
            tile(i - 1 - c[0], False)
            return c[0] + 1, live()

        n_off, _ = lax.while_loop(more, step, (jnp.int32(0), live()))
        cnt_ref[pl.program_id(0), i] = n_off.astype(F32)
        for p in range(pairs):
            rt_ref[:, lanes(p)] = jnp.where(lane < HEAD_DIM, rs[2 * p], rs[2 * p + 1])
            ob_ref[:, lanes(p)] = jnp.where(lane < HEAD_DIM, acc[2 * p], acc[2 * p + 1]).astype(BF16)
        pl.when(at_last)(g_finish)

    blk = lambda rows, imap: pl.BlockSpec((rows, (128 * pairs)), imap)
    return pl.pallas_call(
        body, name=name, grid=(ngrp, nq),
        in_specs=[blk(tq, lambda j, i: (i, j)), blk(t, lambda j, i: (0, j)), blk(t, lambda j, i: (0, j)),
                  _fix_spec((2 * tk, tk)), HB, pl.BlockSpec(memory_space=pl.ANY)],
        out_specs=[blk(tq, lambda j, i: (i, j)), blk(tq, lambda j, i: (i, ngrp + j)),
                   pl.BlockSpec(memory_space=pltpu.SMEM), HB],
        out_shape=[jax.ShapeDtypeStruct((t, D_MODEL), F32), jax.ShapeDtypeStruct(ycat.shape, ycat.dtype),
                   jax.ShapeDtypeStruct((ngrp, nq), F32),
                   jax.ShapeDtypeStruct((N_SHARDS,) + pack.shape, pack.dtype)],
        scratch_shapes=[pltpu.VMEM((nh, tq, 128), F32), pltpu.VMEM((nh, tq, 1), F32)] + GATHER_SEMS,
        input_output_aliases={5: 1},
        compiler_params=_cp(("arbitrary", "arbitrary")),
    )(qs, kn, vb, _doubled(cst["ustrict"], tk), pack, ycat)


def _sb_bwd(qs, kn, vb, rtot, cnt, dycat, csum_b, cst, name):
    t = qs.shape[0]
    tk = min(t, SB_TILE)
    tq = tk // 2
    nq = t // tq
    pairs = BWD_PAIRS
    ngrp = D_MODEL // (128 * pairs)
    nh = 2 * pairs
    lanes = lambda p: slice(p * 128, (p + 1) * 128)

    def body(q_ref, k_ref, v_ref, rt_ref, do_ref, us_ref, ui_ref, cnt_ref, xs_ref, dq_ref, dk_ref, dv_ref, xr_ref,
             acc, rs, es, xss, xrs):
        at_first, _, at_last = _grid_marks(ngrp, nq)
        x_start, x_finish = _exchange_stages(xs_ref, xr_ref, xss, xrs)
        pl.when(at_first)(x_start)
        i = pl.program_id(1)
        lane = lax.broadcasted_iota(jnp.int32, (1, 128), 1)
        keep = [lane < HEAD_DIM, lane >= HEAD_DIM]
        qh, doh, rtot_h = [], [], []
        for p in range(pairs):
            q2, rt = q_ref[:, lanes(p)], rt_ref[:, lanes(p)]
            do2b = do_ref[:, lanes(p)].astype(BF16)
            qh += [jnp.where(kp, q2, jnp.zeros_like(q2)) for kp in keep]
            doh += [jnp.where(kp, do2b, jnp.zeros_like(do2b)) for kp in keep]
            rtot_h += [jnp.sum(jnp.where(lane == n * HEAD_DIM, rt, 0.0), axis=1, keepdims=True) for n in range(2)]
        acc[...] = jnp.zeros_like(acc)
        rs[...] = jnp.zeros_like(rs)
        es[...] = jnp.zeros_like(es)

        @pl.when(i == 0)
        def _():
            dk_ref[...] = jnp.zeros_like(dk_ref)
            dv_ref[...] = jnp.zeros_like(dv_ref)

        ule, ult = us_ref[...], ui_ref[...]

        def tile(kb, masked):
            off = pl.multiple_of(kb * tk, tk)
            k2 = [k_ref[pl.ds(off, tk), lanes(p)] for p in range(pairs)]
            v2 = [v_ref[pl.ds(off, tk), lanes(p)] for p in range(pairs)]
            strict = _sb_masks(i, kb, tq, tk) if masked else None
            s = [_dot(qh[h], k2[h // 2], "nt") for h in range(nh)]
            dw = [_dot(doh[h], v2[h // 2], "nt") for h in range(nh)]
            a, sg, r, pin, w, e, cin = ([None] * nh for _ in range(7))
            for h in range(nh):
                sp = _softplus(s[h])
                a[h] = s[h] - sp
                sg[h] = jnp.exp(a[h])
                r[h] = jnp.where(strict, -sp, 0.0) if masked else -sp
                pin[h] = _dot2(r[h], ule)
            for h in range(nh):
                lw = a[h] + ((rtot_h[h] - rs[h]) - pin[h])
                w[h] = jnp.exp(jnp.where(strict, lw, -1e30) if masked else lw)
                e[h] = w[h] * dw[h]
                cin[h] = _dot2(e[h], ult)
            for p in range(pairs):
                dk_t = jnp.zeros((tk, 128), F32)
                dv_t = jnp.zeros((tk, 128), F32)
                for h in (2 * p, 2 * p + 1):
                    dl = e[h] * (1.0 - sg[h]) - (es[h] + cin[h]) * sg[h]
                    dl = (jnp.where(strict, dl, 0.0) if masked else dl).astype(BF16)
                    rs[h] = rs[h] + jnp.sum(r[h], axis=1, keepdims=True)
                    es[h] = es[h] + jnp.sum(e[h], axis=1, keepdims=True)
                    acc[h] = acc[h] + _dot(dl, k2[p])
                    dk_t = dk_t + _dot(dl, qh[h], "tn")
                    dv_t = dv_t + _dot(w[h].astype(BF16), doh[h], "tn")
                dk_ref[pl.ds(off, tk), lanes(p)] += dk_t
                dv_ref[pl.ds(off, tk), lanes(p)] += dv_t

        def step(kb, carry):
            tile(kb, False)
            return carry

        kd = i // 2
        n_off = cnt_ref[pl.program_id(0) * BWD_PAIRS // FWD_PAIRS, kd].astype(jnp.int32)
        lax.fori_loop(kd - n_off, kd, step, 0)
        tile(kd, True)
        for p in range(pairs):
            dq_ref[:, lanes(p)] = jnp.where(lane < HEAD_DIM, acc[2 * p], acc[2 * p + 1])
        pl.when(at_last)(x_finish)

    blk = lambda rows, imap: pl.BlockSpec((rows, (128 * pairs)), imap)
    return pl.pallas_call(
        body, name=name, grid=(ngrp, nq),
        in_specs=[blk(tq, lambda j, i: (i, j)), blk(t, lambda j, i: (0, j)), blk(t, lambda j, i: (0, j)),
                  blk(tq, lambda j, i: (i, j)), blk(tq, lambda j, i: (i, ngrp + j)),
                  _fix_spec((2 * tk, tk)), _fix_spec((2 * tk, tk)), pl.BlockSpec(memory_space=pltpu.SMEM), HB],
        out_specs=[blk(tq, lambda j, i: (i, j)), blk(t, lambda j, i: (0, j)), blk(t, lambda j, i: (0, j)), HB],
        out_shape=[jax.ShapeDtypeStruct((t, D_MODEL), F32)] * 3
        + [jax.ShapeDtypeStruct((3,) + csum_b.shape[1:], csum_b.dtype)],
        scratch_shapes=[pltpu.VMEM((nh, tq, 128), F32), pltpu.VMEM((nh, tq, 1), F32), pltpu.VMEM((nh, tq, 1), F32)]
        + EXCHANGE_SEMS,
        compiler_params=_cp(("arbitrary", "arbitrary")),
    )(qs, kn, vb, rtot, dycat, _doubled(cst["ule"], tk), _doubled(cst["ult"], tk), cnt, csum_b)


def _adamw(w, g, m, v, name):
    lead = (1,) * (w.ndim - 2)
    rows, cols = w.shape[-2:]
    fits = [d for d in range(8, rows, 8) if rows % d == 0 and d * cols * 4 <= ADAM_BLOCK_BYTES]
    tr = max(fits) if fits else rows
    c1 = 1.0 - ADAM_B1 ** ADAM_STEP
    c2 = 1.0 - ADAM_B2 ** ADAM_STEP

    def body(w_ref, g_ref, m_ref, v_ref, d_ref, nm_ref, nv_ref):
        gv = g_ref[...]
        nm = ADAM_B1 * m_ref[...] + (1.0 - ADAM_B1) * gv
        nv = ADAM_B2 * v_ref[...] + (1.0 - ADAM_B2) * (gv * gv)
        nm_ref[...] = nm
        nv_ref[...] = nv
        d_ref[...] = -ADAM_LR * ((nm / c1) / (jnp.sqrt(nv / c2) + ADAM_EPS) + ADAM_WD * w_ref[...])

    spec = pl.BlockSpec(lead + (tr, cols), lambda i: (0,) * len(lead) + (i, 0))
    return pl.pallas_call(
        body, name=name, grid=(rows // tr,), in_specs=[spec] * 4, out_specs=[spec] * 3,
        out_shape=[jax.ShapeDtypeStruct(w.shape, F32)] * 3, compiler_params=_cp(("parallel",)),
    )(w, g, m, v)


def _place():
    x, y, c = lax.axis_index("x"), lax.axis_index("y"), lax.axis_index("c")
    chips = [(1 - x, y), (x, 1 - y), (1 - x, 1 - y)]
    return x, y, c, chips


VM = pl.BlockSpec(memory_space=pltpu.VMEM)
HB = pl.BlockSpec(memory_space=pltpu.HBM)


def _gather_all(p_ref, gat_ref, ss, rs):
    x, y, c, _ = _place()
    me = 4 * x + 2 * y + c
    peers = [(x, y, 1 - c), (1 - x, y, c), (x, 1 - y, c), (1 - x, 1 - y, c),
             (1 - x, y, 1 - c), (x, 1 - y, 1 - c), (1 - x, 1 - y, 1 - c)]

    def copy(k, slot, to):
        return pltpu.make_async_remote_copy(src_ref=p_ref, dst_ref=gat_ref.at[slot], send_sem=ss.at[k],
                                            recv_sem=rs.at[k], device_id=to, device_id_type=MESH)

    sends = [copy(k, me, p) for k, p in enumerate(peers)]
    for s in sends:
        s.start()
    gat_ref[me] = p_ref[...]
    for k, p in enumerate(peers):
        copy(k, 4 * p[0] + 2 * p[1] + p[2], p).wait_recv()
    for s in sends:
        s.wait_send()


ALL_SEMS = [pltpu.SemaphoreType.DMA((7,)), pltpu.SemaphoreType.DMA((7,))]


def _small_reduce(pack, name):
    rows = pack.shape[0]

    def body(p_ref, gat_ref, sum_ref, ss, rs):
        _gather_all(p_ref, gat_ref, ss, rs)
        tot = gat_ref[0]
        for b in range(1, 8):
            tot = tot + gat_ref[b]
        sum_ref[...] = tot

    return pl.pallas_call(
        body, name=name, in_specs=[VM], out_specs=[VM, VM],
        out_shape=[jax.ShapeDtypeStruct((8, rows, D_MODEL), F32), jax.ShapeDtypeStruct((rows, D_MODEL), F32)],
        scratch_shapes=ALL_SEMS, compiler_params=_cp(),
    )(pack)


def _prologue(cpack, w_ada, b_shard, wpack_in, name):
    def body(cp_ref, w_ref, b_ref, p_ref, gat_ref, modp_ref, gin_ref, ss1, rs1, ss2, rs2, gss, grs):
        g_start, g_relay, g_finish = _gather_stages(p_ref, gin_ref, gss, grs)
        g_start()
        _gather_all(cp_ref, gat_ref, ss1, rs1)
        x, y, c, chips = _place()
        sh = 2 * x + y
        row = lax.broadcasted_iota(jnp.int32, (8, D_MODEL), 0)
        cv = jnp.zeros((8, D_MODEL), F32)
        for b in range(8):
            cv = jnp.where(row == b, gat_ref[b, 0:8, :], cv)
        cv = cv * _sig(cv)
        modp_ref[sh] = jnp.dot(cv, w_ref[...], precision=lax.Precision.HIGHEST,
                               preferred_element_type=F32) + b_ref[...]

        def copy(k, slot, to):
            return pltpu.make_async_remote_copy(src_ref=modp_ref.at[slot], dst_ref=modp_ref.at[slot],
                                                send_sem=ss2.at[k], recv_sem=rs2.at[k], device_id=to,
                                                device_id_type=MESH)

        sends = [copy(k, sh, (*ch, c)) for k, ch in enumerate(chips)]
        for s in sends:
            s.start()
        for k, ch in enumerate(chips):
            copy(k, 2 * ch[0] + ch[1], (*ch, c)).wait_recv()
        for s in sends:
            s.wait_send()
        g_relay()
        g_finish()

    return pl.pallas_call(
        body, name=name, in_specs=[VM, VM, VM, HB], out_specs=[VM, VM, HB],
        out_shape=[jax.ShapeDtypeStruct((8,) + cpack.shape, F32),
                   jax.ShapeDtypeStruct((N_SHARDS, 8, 6 * D_MODEL // N_SHARDS), F32),
                   jax.ShapeDtypeStruct((N_SHARDS,) + wpack_in.shape, wpack_in.dtype)],
        scratch_shapes=ALL_SEMS + EXCHANGE_SEMS + GATHER_SEMS, compiler_params=_cp(),
    )(cpack, w_ada, b_shard, wpack_in)


def _gather_stages(p_ref, out_ref, ss, rs):
    hf = p_ref.shape[0] // 2
    x, y, c, chips = _place()
    sh = 2 * x + y
    sib = (x, y, 1 - c)
    slots = [2 * ch[0] + ch[1] for ch in chips]

    def half(slot, hc):
        return out_ref.at[slot, pl.ds(hc * hf, hf), :]

    def copy(k, src, slot, hc, to):
        return pltpu.make_async_remote_copy(src_ref=src, dst_ref=half(slot, hc), send_sem=ss.at[k],
                                            recv_sem=rs.at[k], device_id=to, device_id_type=MESH)

    def first():
        return [copy(j, p_ref.at[pl.ds(c * hf, hf), :], sh, c, (*ch, c)) for j, ch in enumerate(chips)]

    def passed():
        return [copy(3 + j, half(slots[j], c), slots[j], c, sib) for j in range(3)]

    def own():
        return pltpu.make_async_remote_copy(src_ref=p_ref, dst_ref=out_ref.at[sh], send_sem=ss.at[6],
                                            recv_sem=rs.at[6], device_id=sib, device_id_type=MESH)

    def start():
        for cp in first() + [own()]:
            cp.start()

    def relay():
        for j, cp in enumerate(passed()):
            copy(j, half(slots[j], c), slots[j], c, (*chips[j], c)).wait_recv()
            cp.start()

    def finish():
        for j in range(3):
            copy(3 + j, half(slots[j], 1 - c), slots[j], 1 - c, sib).wait_recv()
        own().wait()
        for cp in first() + passed():
            cp.wait_send()

    return start, relay, finish


GATHER_SEMS = [pltpu.SemaphoreType.DMA((7,)), pltpu.SemaphoreType.DMA((7,))]


def _swap_stages(g_ref, out_ref, ss, rs):
    hf = g_ref.shape[1] // 2
    x, y, c, _ = _place()

    def copy():
        return pltpu.make_async_remote_copy(
            src_ref=g_ref.at[pl.ds(0, N_SHARDS), pl.ds((1 - c) * hf, hf), :], dst_ref=out_ref,
            send_sem=ss, recv_sem=rs, device_id=(x, y, 1 - c), device_id_type=MESH)

    return (lambda: copy().start()), (lambda: copy().wait())


SWAP_SEMS = [pltpu.SemaphoreType.DMA, pltpu.SemaphoreType.DMA]


def _swap_shape(g):
    return jax.ShapeDtypeStruct((N_SHARDS, g.shape[1] // 2, D_MODEL), g.dtype)


def _sibling_swap(g, name):
    def body(g_ref, out_ref, ss, rs):
        for stage in _swap_stages(g_ref, out_ref, ss, rs):
            stage()

    return pl.pallas_call(
        body, name=name, in_specs=[HB], out_specs=HB, out_shape=_swap_shape(g),
        scratch_shapes=SWAP_SEMS, compiler_params=_cp(),
    )(g)


def _row_tile(rows, width_bytes, cap_bytes):
    fits = [d for d in range(8, rows + 1, 8) if rows % d == 0 and d * width_bytes <= cap_bytes]
    return max(fits)


def _chip_sum(g, got, c_idx, name):
    hf = got.shape[1]
    tr = _row_tile(hf, D_MODEL * 4, 3 << 20)
    nb = hf // tr

    def body(c_ref, a_ref, b_ref, s_ref, sb_ref):
        s = a_ref[...] + b_ref[...]
        s_ref[...] = s
        sb_ref[...] = s.astype(BF16)

    blk = pl.BlockSpec((1, tr, D_MODEL), lambda s, i, c_ref: (s, i, 0))
    return pl.pallas_call(
        body, name=name,
        grid_spec=pltpu.PrefetchScalarGridSpec(
            num_scalar_prefetch=1, grid=(N_SHARDS, nb),
            in_specs=[pl.BlockSpec((1, tr, D_MODEL), lambda s, i, c_ref: (s, c_ref[0] * nb + i, 0)), blk],
            out_specs=[blk, blk]),
        out_shape=[jax.ShapeDtypeStruct((N_SHARDS, hf, D_MODEL), F32),
                   jax.ShapeDtypeStruct((N_SHARDS, hf, D_MODEL), BF16)],
        compiler_params=_cp(("parallel", "parallel")),
    )(c_idx, g, got)


def _exchange_stages(s_ref, out_ref, ss, rs):
    x, y, c, chips = _place()

    def sends():
        return [pltpu.make_async_remote_copy(src_ref=s_ref.at[2 * ch[0] + ch[1]], dst_ref=out_ref.at[k],
                                             send_sem=ss.at[k], recv_sem=rs.at[k], device_id=(*ch, c),
                                             device_id_type=MESH) for k, ch in enumerate(chips)]

    def start():
        for cp in sends():
            cp.start()

    def finish():
        for cp in sends():
            cp.wait()

    return start, finish


EXCHANGE_SEMS = [pltpu.SemaphoreType.DMA((3,)), pltpu.SemaphoreType.DMA((3,))]


def _total_half(s, got, sh_idx, name):
    hf = got.shape[1]
    tr = _row_tile(hf, D_MODEL * 4, 3 << 20)
    nb = hf // tr

    def body(sh_ref, a_ref, r0, r1, r2, o_ref):
        o_ref[...] = ((a_ref[0] + r0[0].astype(F32)) + r1[0].astype(F32)) + r2[0].astype(F32)

    rspec = lambda k: pl.BlockSpec((1, tr, D_MODEL), lambda i, sh_ref: (k, i, 0))
    return pl.pallas_call(
        body, name=name,
        grid_spec=pltpu.PrefetchScalarGridSpec(
            num_scalar_prefetch=1, grid=(nb,),
            in_specs=[pl.BlockSpec((1, tr, D_MODEL), lambda i, sh_ref: (sh_ref[0], i, 0)),
                      rspec(0), rspec(1), rspec(2)],
            out_specs=pl.BlockSpec((tr, D_MODEL), lambda i, sh_ref: (i, 0))),
        out_shape=jax.ShapeDtypeStruct((hf, D_MODEL), F32),
        compiler_params=_cp(("parallel",)),
    )(sh_idx, s, got, got, got)


def _join_halves(tot, name):
    def body(t_ref, out_ref, ss, rs):
        x, y, c, _ = _place()
        cp = pltpu.make_async_remote_copy(src_ref=t_ref, dst_ref=out_ref, send_sem=ss, recv_sem=rs,
                                          device_id=(x, y, 1 - c), device_id_type=MESH)
        cp.start()
        cp.wait()

    return pl.pallas_call(
        body, name=name, in_specs=[HB], out_specs=HB,
        out_shape=jax.ShapeDtypeStruct(tot.shape, F32),
        scratch_shapes=[pltpu.SemaphoreType.DMA, pltpu.SemaphoreType.DMA],
        compiler_params=_cp(),
    )(tot)


def _w_ada_grad(cond, dmod_cols, name):
    def body(c_ref, d_ref, o_ref):
        cv = c_ref[...]
        cv = cv * _sig(cv)
        o_ref[...] = lax.dot_general(cv, d_ref[...], _DN["tn"], precision=lax.Precision.HIGHEST,
                                     preferred_element_type=F32)

    return pl.pallas_call(
        body, name=name, in_specs=[VM, VM], out_specs=VM,
        out_shape=jax.ShapeDtypeStruct((D_MODEL, dmod_cols.shape[1]), F32), compiler_params=_cp(),
    )(cond, dmod_cols)


def _pad_rows(a, rows):
    return jnp.pad(a, ((0, rows - a.shape[0]), (0, 0)))


def _pad_cols(a, cols):
    return jnp.pad(a, ((0, 0), (0, cols - a.shape[1])))


def _unpack_rest(p):
    o = 0
    out = []
    for r in (R_OUT, R_FF, R_FF, R_FF):
        out.append(p[..., o:o + r, :])
        o += r
    return out


def _reduce_tail(csum, got2, shard, ac, tag):
    tot = _total_half(csum, got2, shard.reshape(1).astype(jnp.int32), "rs_total_" + tag)
    other = _join_halves(tot, "rs_join_" + tag)
    return jnp.where(ac == 0, jnp.concatenate([tot, other], axis=0), jnp.concatenate([other, tot], axis=0))


def _reduce_head(gpack, ac, tag):
    got = _sibling_swap(gpack, "rs_sibling_swap_" + tag)
    return _chip_sum(gpack, got, ac.reshape(1).astype(jnp.int32), "rs_chip_sum_" + tag)


def kernel(x, c, w_ada, b_ada, norm1_w, w_in, conv_w, conv_b, dt_bias, a_log, d_skip, ssd_norm_w, q_norm_w, k_norm_w, w_out, norm2_w, w_gate, w_up, w_down, loss_target, m_w_ada, m_b_ada, m_norm1_w, m_w_in, m_conv_w, m_conv_b, m_dt_bias, m_a_log, m_d_skip, m_ssd_norm_w, m_q_norm_w, m_k_norm_w, m_w_out, m_norm2_w, m_w_gate, m_w_up, m_w_down, v_w_ada, v_b_ada, v_norm1_w, v_w_in, v_conv_w, v_conv_b, v_dt_bias, v_a_log, v_d_skip, v_ssd_norm_w, v_q_norm_w, v_k_norm_w, v_w_out, v_norm2_w, v_w_gate, v_w_up, v_w_down):
    cst = _consts()
    ax, ay, ac = lax.axis_index("x"), lax.axis_index("y"), lax.axis_index("c")
    shard = 2 * ax + ay
    me = 4 * ax + 2 * ay + ac
    xs = x[0]
    tgt = loss_target[0]
    w_in_cols = w_in.shape[2]
    conv_cols = conv_w.shape[2]

    tr3 = lambda a: jnp.transpose(a, (0, 2, 1))
    lin = lambda a: tr3(a).reshape(-1, 128)
    unlin = lambda a: tr3(a.reshape(1, w_in_cols, D_MODEL))
    wpack_in = _pad_rows(tr3(w_in.astype(BF16))[0], R_IN)
    wpack_rest = jnp.concatenate([w_out[0], tr3(w_gate)[0], tr3(w_up)[0], w_down[0]], axis=0).astype(BF16)

    cw_flat = _pad_cols(conv_w[0].reshape(1, -1), 2 * D_MODEL).reshape(2, D_MODEL)
    cpack = jnp.concatenate([jnp.broadcast_to(c, (8, D_MODEL)), _pad_rows(cw_flat, 8)], axis=0)
    mod_w = 6 * D_MODEL // N_SHARDS
    b_shard = lax.dynamic_slice(b_ada, (0, shard * mod_w), (1, mod_w))
    gat, modp, gp_in = _prologue(cpack, w_ada[0], b_shard, wpack_in, "prologue")
    c_all = gat[:, 0, :]
    cw = gat[0::2, 8:10, :].reshape(N_SHARDS, 2 * D_MODEL)[:, :4 * conv_cols].reshape(N_SHARDS, 4, conv_cols)
    conv_w_full = jnp.transpose(cw, (1, 0, 2)).reshape(4, D_CONV)
    mod_mine = lax.dynamic_slice(modp, (0, me, 0), (N_SHARDS, 1, mod_w)).reshape(6, D_MODEL)
    mod = _pad_rows(mod_mine, 8)
    wi_t = gp_in[:, :w_in_cols, :].reshape(D_IN_PROJ, D_MODEL)
    w_inp_t = jnp.concatenate([wi_t[0:1024], wi_t[2576:5648], wi_t[1024:2560], wi_t[2560:2576],
                               jnp.zeros((D_PROJ_PAD - D_IN_PROJ, D_MODEL), BF16)], axis=0)

    pad128 = lambda a: _pad_cols(a, 128)
    dtb, alog = pad128(dt_bias), pad128(a_log)
    dsk = jnp.repeat(d_skip, HEAD_DIM, axis=1)
    qw, kw = jnp.tile(q_norm_w, (1, N_HEADS)), jnp.tile(k_norm_w, (1, N_HEADS))

    h1 = _norm_mod(xs, norm1_w, mod, 0, "norm1")
    proj = _matmul(h1, w_inp_t, "nt", F32, "in_proj")
    u = _conv_fwd(proj, conv_w_full, conv_b, "conv_fwd")
    y_ssd, yn, prev = _ssd_fwd(u, proj, dtb, alog, dsk, ssd_norm_w, cst, "ssd_fwd")
    qs, kn, vb = _qk_fwd(proj, qw, kw, cst, "qk_norm")
    rtot, ycat, cnt, gp_rest = _sb_fwd(qs, kn, vb, wpack_rest, yn, cst, "sb_fwd")
    p_out, p_gate, p_up, p_down = _unpack_rest(gp_rest)
    w_o = p_out.reshape(2 * D_MODEL, D_MODEL)
    w_gu_t = jnp.concatenate([p_gate.reshape(D_FF, D_MODEL), p_up.reshape(D_FF, D_MODEL)], axis=0)
    w_d = p_down.reshape(D_FF, D_MODEL)
    mix = _matmul(ycat, w_o, "nn", BF16, "out_proj")
    x1, h2 = _resid_norm(xs, mix, norm2_w, mod, "resid_norm2")
    gu = _matmul(h2, w_gu_t, "nt", BF16, "ffn_in")
    act = _act_fwd(gu, "ffn_act")
    ffn = _matmul(act, w_d, "nn", BF16, "ffn_out", tk_cap=1408)
    dffn, dout, dg2, loss8 = _loss_head(x1, ffn, tgt, mod, "loss_head")
    loss = lax.psum(loss8[0, 0], ("x", "y", "c"))

    dact = _matmul(dffn, w_d, "nt", BF16, "d_act")
    g_down = _matmul(act, dffn, "tn", F32, "g_w_down", tm_cap=1408)
    dgu = _act_bwd(dact, gu, "ffn_act_bwd")
    dh2 = _matmul(dgu, w_gu_t, "nn", BF16, "d_h2", tk_cap=1408)
    g_gu_t = _matmul(dgu, h2, "tn", F32, "g_w_gu", tm_cap=1408)
    dx1, dmix, acc2 = _norm_bwd(dh2, x1, dout, mix, norm2_w, mod, 3, 2, "norm2_bwd")
    dycat = _matmul(dmix, w_o, "nt", BF16, "d_ycat")
    g_out = _matmul(ycat, dmix, "tn", F32, "g_w_out")
    gpack_rest = jnp.concatenate([
        g_out.reshape(N_SHARDS, R_OUT, D_MODEL),
        g_gu_t[:D_FF].reshape(N_SHARDS, R_FF, D_MODEL), g_gu_t[D_FF:].reshape(N_SHARDS, R_FF, D_MODEL),
        g_down.reshape(N_SHARDS, R_FF, D_MODEL)], axis=1)
    du, ddt, dz, acc_ssd, acc16, got_s = _ssd_bwd(u, proj, y_ssd, prev, dycat, dtb, alog, dsk, ssd_norm_w,
                                                  gpack_rest, cst, "ssd_bwd")
    csum_r, csum_rb = _chip_sum(gpack_rest, got_s, ac.reshape(1).astype(jnp.int32), "rs_chip_sum_rest")
    dqs, dkn, dv, got_r = _sb_bwd(qs, kn, vb, rtot, cnt, dycat, csum_rb, cst, "sb_bwd")
    r_out, r_gate, r_up, r_down = _unpack_rest(_reduce_tail(csum_r, got_r, shard, ac, "rest"))
    dq, dk, dvb, acc_qk = _qk_bwd(proj, dqs, dkn, dv, qw, kw, cst, "qk_norm_bwd")
    dxbc, g_conv_w, g_conv_b = _conv_bwd(proj, du, conv_w_full, conv_b, "conv_bwd")
    dproj = jnp.concatenate([dz, dq, dk, dvb, dxbc, ddt], axis=1)
    g_inp_t = _matmul(dproj, h1, "tn", F32, "g_w_in", tm_cap=1920)
    gi_t = jnp.concatenate([g_inp_t[0:1024], g_inp_t[4096:5632], g_inp_t[5632:5648], g_inp_t[1024:4096]], axis=0)
    gpack_in = jnp.pad(gi_t.reshape(N_SHARDS, w_in_cols, D_MODEL), ((0, 0), (0, R_IN - w_in_cols), (0, 0)))
    csum_i, csum_ib = _reduce_head(gpack_in, ac, "in")
    dh1, got_i = _matmul(dproj, w_inp_t, "nn", BF16, "d_h1", tk_cap=1152, exchange=csum_ib)
    r_in = _reduce_tail(csum_i, got_i, shard, ac, "in")
    grad_x, acc1 = _norm_bwd(dh1, xs, dx1, None, norm1_w, mod, 0, None, "norm1_bwd")

    last = jnp.concatenate([acc_qk[0:1, 0:64], acc_qk[1:2, 0:64], acc16[0:1, 0:16], acc16[1:2, 0:16],
                            acc16[2:3, 0:16]], axis=1)
    spack = jnp.concatenate([
        acc1[0:2], acc2[3:4], acc2[0:2], dg2,
        acc1[2:3], acc2[2:3], acc_ssd[0:1],
        _pad_cols(g_conv_b, 2 * D_MODEL).reshape(2, D_MODEL),
        g_conv_w.reshape(6, D_MODEL),
        _pad_cols(last, D_MODEL)], axis=0)
    sgat, ssum = _small_reduce(_pad_rows(spack, SMALL_ROWS), "gather_small")
    g_b_ada = ssum[0:6].reshape(1, 6 * D_MODEL)
    g_norm1, g_norm2, g_ssdn = ssum[6:7], ssum[7:8], ssum[8:9]
    g_cb = ssum[9:11].reshape(1, 2 * D_MODEL)[:, :D_CONV]
    g_cw = lax.dynamic_slice(ssum[11:17].reshape(4, D_CONV), (0, shard * conv_cols), (4, conv_cols))
    g_qn, g_kn = ssum[17:18, 0:64], ssum[17:18, 64:128]
    g_dtb, g_alog, g_dsk = ssum[17:18, 128:144], ssum[17:18, 144:160], ssum[17:18, 160:176]
    dmod_all = sgat[:, 0:6, :].reshape(8, 6 * D_MODEL)
    g_w_ada = _w_ada_grad(c_all, lax.dynamic_slice(dmod_all, (0, shard * mod_w), (8, mod_w)), "g_w_ada")


    grads = dict(w_ada=g_w_ada, b_ada=g_b_ada, norm1_w=g_norm1, w_in=r_in[:w_in_cols].reshape(-1, 128), conv_w=g_cw,
                 conv_b=g_cb, dt_bias=g_dtb, a_log=g_alog, d_skip=g_dsk, ssd_norm_w=g_ssdn, q_norm_w=g_qn,
                 k_norm_w=g_kn, w_out=r_out, norm2_w=g_norm2, w_gate=r_gate, w_up=r_up, w_down=r_down)
    weights = dict(w_ada=(w_ada, m_w_ada, v_w_ada), b_ada=(b_ada, m_b_ada, v_b_ada),
                   norm1_w=(norm1_w, m_norm1_w, v_norm1_w), w_in=(w_in, m_w_in, v_w_in),
                   conv_w=(conv_w, m_conv_w, v_conv_w), conv_b=(conv_b, m_conv_b, v_conv_b),
                   dt_bias=(dt_bias, m_dt_bias, v_dt_bias), a_log=(a_log, m_a_log, v_a_log),
                   d_skip=(d_skip, m_d_skip, v_d_skip), ssd_norm_w=(ssd_norm_w, m_ssd_norm_w, v_ssd_norm_w),
                   q_norm_w=(q_norm_w, m_q_norm_w, v_q_norm_w), k_norm_w=(k_norm_w, m_k_norm_w, v_k_norm_w),
                   w_out=(w_out, m_w_out, v_w_out), norm2_w=(norm2_w, m_norm2_w, v_norm2_w),
                   w_gate=(w_gate, m_w_gate, v_w_gate), w_up=(w_up, m_w_up, v_w_up),
                   w_down=(w_down, m_w_down, v_w_down))
    views = dict(w_in=(lin, unlin), w_gate=(tr3, tr3), w_up=(tr3, tr3))
    same = lambda a: a
    names = list(weights)
    g_out_l, d_out_l, m_out_l, v_out_l = [], [], [], []
    for n in names:
        view, back = views.get(n, (same, same))
        w, m, v = (view(a) for a in weights[n])
        g = grads[n].reshape(w.shape)
        d, nm, nv = _adamw(w, g, m, v, "adamw_" + n)
        g_out_l.append(back(g))
        d_out_l.append(back(d))
        m_out_l.append(back(nm))
        v_out_l.append(back(nv))
    return (loss, grad_x[None], *g_out_l, *d_out_l, *m_out_l, *v_out_l)
```
